```python
import jax, jax.numpy as jnp
from jax import lax
import numpy as np

D_MODEL = 2048
BATCH = 8
SEQ = 4096
DEPTH = 4

MIX_WIDTH = D_MODEL
EPS = 1e-6
BLOCK = 128
NEG = -1e30

SWA_WIDTH = MIX_WIDTH // 2
SWA_HEAD_DIM = 64
SWA_Q_HEADS = SWA_WIDTH // SWA_HEAD_DIM
SWA_KV_HEADS = 2
SWA_GROUP = SWA_Q_HEADS // SWA_KV_HEADS
WINDOW = 128

MLA_WIDTH = MIX_WIDTH - SWA_WIDTH
MLA_V_DIM = 128
MLA_HEADS = MLA_WIDTH // MLA_V_DIM
MLA_NOPE_DIM = 128
MLA_ROPE_DIM = 64
MLA_QK_DIM = MLA_NOPE_DIM + MLA_ROPE_DIM
Q_LORA_RANK = 384
KV_LORA_RANK = 256
ROPE_THETA = 10000.0

A_Q = SWA_Q_HEADS * SWA_HEAD_DIM
A_KV = SWA_KV_HEADS * SWA_HEAD_DIM
A_GATE = SWA_WIDTH
B_GATE = MLA_WIDTH
IN_WIDTH = A_Q + 2 * A_KV + A_GATE + Q_LORA_RANK + KV_LORA_RANK + MLA_ROPE_DIM + B_GATE
SPLIT_POINTS = (
    A_Q,
    A_Q + A_KV,
    A_Q + 2 * A_KV,
    A_Q + 2 * A_KV + A_GATE,
    A_Q + 2 * A_KV + A_GATE + Q_LORA_RANK,
    A_Q + 2 * A_KV + A_GATE + Q_LORA_RANK + KV_LORA_RANK,
    A_Q + 2 * A_KV + A_GATE + Q_LORA_RANK + KV_LORA_RANK + MLA_ROPE_DIM,
)

kernel_name = "hybrid_swa_sink_alibi_mla_gated_trunk"


def rmsnorm(x, g):
    xf = x.astype(jnp.float32)
    y = xf * lax.rsqrt(jnp.mean(xf * xf, axis=-1, keepdims=True) + EPS) * g.astype(jnp.float32)
    return y.astype(x.dtype)


def rope(x, cos, sin):
    half = x.shape[-1] // 2
    x1, x2 = x[..., :half], x[..., half:]
    cos = cos.astype(x.dtype)
    sin = sin.astype(x.dtype)
    return jnp.concatenate([x1 * cos - x2 * sin, x2 * cos + x1 * sin], axis=-1)


def swa_attention(q, k, v, sinks):
    b, s = q.shape[0], q.shape[1]
    nb = s // BLOCK
    qb = q.reshape(b, nb, BLOCK, SWA_KV_HEADS, SWA_GROUP, SWA_HEAD_DIM)
    pad = ((0, 0), (BLOCK, 0), (0, 0), (0, 0))
    kp = jnp.pad(k, pad).reshape(b, nb + 1, BLOCK, SWA_KV_HEADS, SWA_HEAD_DIM)
    vp = jnp.pad(v, pad).reshape(b, nb + 1, BLOCK, SWA_KV_HEADS, SWA_HEAD_DIM)
    kk = jnp.concatenate([kp[:, :-1], kp[:, 1:]], axis=2)
    vv = jnp.concatenate([vp[:, :-1], vp[:, 1:]], axis=2)
    scores = jnp.einsum('bnqhgd,bnkhd->bnhgqk', qb, kk).astype(jnp.float32) * (SWA_HEAD_DIM ** -0.5)
    qi = jnp.arange(BLOCK)[:, None]
    ki = jnp.arange(2 * BLOCK)[None, :]
    delta = BLOCK + qi - ki
    key_pos = (jnp.arange(nb)[:, None] - 1) * BLOCK + jnp.arange(2 * BLOCK)[None, :]
    valid = ((delta >= 0) & (delta < WINDOW))[None] & (key_pos >= 0)[:, None, :]
    slopes = jnp.exp2(-8.0 * jnp.arange(1, SWA_Q_HEADS + 1, dtype=jnp.float32) / SWA_Q_HEADS)
    slopes = slopes.reshape(SWA_KV_HEADS, SWA_GROUP)
    alibi = -slopes[:, :, None, None] * delta.astype(jnp.float32)[None, None]
    scores = jnp.where(valid[None, :, None, None], scores + alibi, NEG)
    sink = sinks.astype(jnp.float32).reshape(SWA_KV_HEADS, SWA_GROUP, 1, 1)
    sink = jnp.broadcast_to(sink, scores.shape[:-1] + (1,))
    probs = jax.nn.softmax(jnp.concatenate([scores, sink], axis=-1), axis=-1)[..., :-1]
    out = jnp.einsum('bnhgqk,bnkhd->bnqhgd', probs.astype(v.dtype), vv)
    return out.reshape(b, s, SWA_WIDTH)


def mla_attention(c_q, c_kv, k_rope, q_a_g, kv_a_g, w_q_b, w_kv_b, cos, sin):
    b, s = c_q.shape[0], c_q.shape[1]
    q = (rmsnorm(c_q, q_a_g) @ w_q_b).reshape(b, s, MLA_HEADS, MLA_QK_DIM)
    q_nope = q[..., :MLA_NOPE_DIM]
    q_rope = rope(q[..., MLA_NOPE_DIM:], cos[:, None, :], sin[:, None, :])
    kv = (rmsnorm(c_kv, kv_a_g) @ w_kv_b).reshape(b, s, MLA_HEADS, MLA_NOPE_DIM + MLA_V_DIM)
    k_nope = kv[..., :MLA_NOPE_DIM]
    v = kv[..., MLA_NOPE_DIM:]
    k_r = rope(k_rope, cos, sin)
    nb = s // BLOCK
    qn_b = q_nope.reshape(b, nb, BLOCK, MLA_HEADS, MLA_NOPE_DIM).transpose(1, 0, 2, 3, 4)
    qr_b = q_rope.reshape(b, nb, BLOCK, MLA_HEADS, MLA_ROPE_DIM).transpose(1, 0, 2, 3, 4)
    key_pos = jnp.arange(s)
    scale = MLA_QK_DIM ** -0.5

    def one_block(args):
        qn, qr, i = args
        sc = (jnp.einsum('bqhd,bkhd->bhqk', qn, k_nope)
              + jnp.einsum('bqhd,bkd->bhqk', qr, k_r)).astype(jnp.float32) * scale
        q_pos = i * BLOCK + jnp.arange(BLOCK)
        mask = key_pos[None, :] <= q_pos[:, None]
        p = jax.nn.softmax(jnp.where(mask, sc, NEG), axis=-1)
        return jnp.einsum('bhqk,bkhd->bqhd', p.astype(v.dtype), v)

    out = lax.map(one_block, (qn_b, qr_b, jnp.arange(nb)))
    return out.transpose(1, 0, 2, 3, 4).reshape(b, s, MLA_WIDTH)


def _fwd_setup_inputs(seed: int = 0) -> dict:
    key = jax.random.key(seed)
    ks = jax.random.split(key, 11)
    f32 = jnp.float32
    x = jax.random.normal(ks[0], (BATCH, SEQ, D_MODEL), f32)
    attn_norm_g = 1.0 + 0.02 * jax.random.normal(ks[1], (DEPTH, D_MODEL), f32)
    w_in = jax.random.normal(ks[2], (DEPTH, D_MODEL, IN_WIDTH), f32) * D_MODEL ** -0.5
    swa_sinks = 0.5 * jax.random.normal(ks[3], (DEPTH, SWA_Q_HEADS), f32)
    q_a_norm_g = 1.0 + 0.02 * jax.random.normal(ks[4], (DEPTH, Q_LORA_RANK), f32)
    kv_a_norm_g = 1.0 + 0.02 * jax.random.normal(ks[5], (DEPTH, KV_LORA_RANK), f32)
    w_q_b = jax.random.normal(ks[6], (DEPTH, Q_LORA_RANK, MLA_HEADS * MLA_QK_DIM), f32) * Q_LORA_RANK ** -0.5
    w_kv_b = jax.random.normal(ks[7], (DEPTH, KV_LORA_RANK, MLA_HEADS * (MLA_NOPE_DIM + MLA_V_DIM)), f32) * KV_LORA_RANK ** -0.5
    w_out = jax.random.normal(ks[8], (DEPTH, MIX_WIDTH, D_MODEL), f32) * MIX_WIDTH ** -0.5
    final_norm_g = 1.0 + 0.02 * jax.random.normal(ks[9], (D_MODEL,), f32)
    return {"x": x, "attn_norm_g": attn_norm_g, "w_in": w_in, "swa_sinks": swa_sinks,
            "q_a_norm_g": q_a_norm_g, "kv_a_norm_g": kv_a_norm_g, "w_q_b": w_q_b,
            "w_kv_b": w_kv_b, "w_out": w_out, "final_norm_g": final_norm_g}


def _fwd_reference(x, attn_norm_g, w_in, swa_sinks, q_a_norm_g, kv_a_norm_g, w_q_b, w_kv_b, w_out, final_norm_g):
    b, s = x.shape[0], x.shape[1]
    pos = jnp.arange(s, dtype=jnp.float32)
    inv_freq = ROPE_THETA ** (-jnp.arange(0, MLA_ROPE_DIM, 2, dtype=jnp.float32) / MLA_ROPE_DIM)
    ang = pos[:, None] * inv_freq[None, :]
    cos, sin = jnp.cos(ang), jnp.sin(ang)
    for l in range(DEPTH):
        h = rmsnorm(x, attn_norm_g[l])
        proj = h @ w_in[l]
        qa, ka, va, ga, cq, ckv, kr, gb = jnp.split(proj, SPLIT_POINTS, axis=-1)
        ya = swa_attention(qa.reshape(b, s, SWA_Q_HEADS, SWA_HEAD_DIM),
                           ka.reshape(b, s, SWA_KV_HEADS, SWA_HEAD_DIM),
                           va.reshape(b, s, SWA_KV_HEADS, SWA_HEAD_DIM),
                           swa_sinks[l]) * jax.nn.silu(ga)
        yb = mla_attention(cq, ckv, kr, q_a_norm_g[l], kv_a_norm_g[l], w_q_b[l], w_kv_b[l],
                           cos, sin) * jax.nn.silu(gb)
        x = x + jnp.concatenate([ya, yb], axis=-1) @ w_out[l]
    return rmsnorm(x, final_norm_g)


import jax as _jax
import jax.numpy as _jnp

TWIN_FORMAT = 'train_step'
FWD_PARAMS = ['x', 'attn_norm_g', 'w_in', 'swa_sinks', 'q_a_norm_g', 'kv_a_norm_g', 'w_q_b', 'w_kv_b', 'w_out', 'final_norm_g']
TWIN_WEIGHTS = ['attn_norm_g', 'w_in', 'swa_sinks', 'q_a_norm_g', 'kv_a_norm_g', 'w_q_b', 'w_kv_b', 'w_out', 'final_norm_g']
TWIN_DIFF_INPUT = 'x'
TWIN_INPUTS = ['x', 'attn_norm_g', 'w_in', 'swa_sinks', 'q_a_norm_g', 'kv_a_norm_g', 'w_q_b', 'w_kv_b', 'w_out', 'final_norm_g', 'loss_target', 'm_attn_norm_g', 'm_w_in', 'm_swa_sinks', 'm_q_a_norm_g', 'm_kv_a_norm_g', 'm_w_q_b', 'm_w_kv_b', 'm_w_out', 'm_final_norm_g', 'v_attn_norm_g', 'v_w_in', 'v_swa_sinks', 'v_q_a_norm_g', 'v_kv_a_norm_g', 'v_w_q_b', 'v_w_kv_b', 'v_w_out', 'v_final_norm_g']
TWIN_OUTPUTS = ['loss', 'grad_x', 'grad_attn_norm_g', 'grad_w_in', 'grad_swa_sinks', 'grad_q_a_norm_g', 'grad_kv_a_norm_g', 'grad_w_q_b', 'grad_w_kv_b', 'grad_w_out', 'grad_final_norm_g', 'delta_attn_norm_g', 'delta_w_in', 'delta_swa_sinks', 'delta_q_a_norm_g', 'delta_kv_a_norm_g', 'delta_w_q_b', 'delta_w_kv_b', 'delta_w_out', 'delta_final_norm_g', 'new_m_attn_norm_g', 'new_m_w_in', 'new_m_swa_sinks', 'new_m_q_a_norm_g', 'new_m_kv_a_norm_g', 'new_m_w_q_b', 'new_m_w_kv_b', 'new_m_w_out', 'new_m_final_norm_g', 'new_v_attn_norm_g', 'new_v_w_in', 'new_v_swa_sinks', 'new_v_q_a_norm_g', 'new_v_kv_a_norm_g', 'new_v_w_q_b', 'new_v_w_kv_b', 'new_v_w_out', 'new_v_final_norm_g']
TWIN_LEAF_KINDS = {'loss': 'loss', 'grad_x': 'grad_x', 'grad_attn_norm_g': 'grad_w', 'grad_w_in': 'grad_w', 'grad_swa_sinks': 'grad_w', 'grad_q_a_norm_g': 'grad_w', 'grad_kv_a_norm_g': 'grad_w', 'grad_w_q_b': 'grad_w', 'grad_w_kv_b': 'grad_w', 'grad_w_out': 'grad_w', 'grad_final_norm_g': 'grad_w', 'delta_attn_norm_g': 'delta_w', 'delta_w_in': 'delta_w', 'delta_swa_sinks': 'delta_w', 'delta_q_a_norm_g': 'delta_w', 'delta_kv_a_norm_g': 'delta_w', 'delta_w_q_b': 'delta_w', 'delta_w_kv_b': 'delta_w', 'delta_w_out': 'delta_w', 'delta_final_norm_g': 'delta_w', 'new_m_attn_norm_g': 'new_m', 'new_m_w_in': 'new_m', 'new_m_swa_sinks': 'new_m', 'new_m_q_a_norm_g': 'new_m', 'new_m_kv_a_norm_g': 'new_m', 'new_m_w_q_b': 'new_m', 'new_m_w_kv_b': 'new_m', 'new_m_w_out': 'new_m', 'new_m_final_norm_g': 'new_m', 'new_v_attn_norm_g': 'new_v', 'new_v_w_in': 'new_v', 'new_v_swa_sinks': 'new_v', 'new_v_q_a_norm_g': 'new_v', 'new_v_kv_a_norm_g': 'new_v', 'new_v_w_q_b': 'new_v', 'new_v_w_kv_b': 'new_v', 'new_v_w_out': 'new_v', 'new_v_final_norm_g': 'new_v'}


def _forward(args):
    return _fwd_reference(*[args[k] for k in FWD_PARAMS])


def _output_shape():
    def fwd():
        inp = _fwd_setup_inputs(0)
        return _fwd_reference(*[inp[k] for k in FWD_PARAMS])
    out = _jax.eval_shape(fwd)
    return out.shape, out.dtype

N_MICROBATCH = 1
ADAM_LR = 0.001
ADAM_B1 = 0.9
ADAM_B2 = 0.999
ADAM_EPS = 1e-08
ADAM_WD = 0.01
ADAM_STEP = 10
PER_EXAMPLE_BATCH_AXIS = {'x': 0, 'loss_target': 0}
SHARED_INPUTS = []
_WEIGHT_DTYPES = {'attn_norm_g': _jnp.float32, 'w_in': _jnp.float32, 'swa_sinks': _jnp.float32, 'q_a_norm_g': _jnp.float32, 'kv_a_norm_g': _jnp.float32, 'w_q_b': _jnp.float32, 'w_kv_b': _jnp.float32, 'w_out': _jnp.float32, 'final_norm_g': _jnp.float32}
MOMENT_SCALE = {'attn_norm_g': 2.609599e-02, 'w_in': 1.858586e-02, 'swa_sinks': 2.096257e-02, 'q_a_norm_g': 1.606421e-02, 'kv_a_norm_g': 3.036348e-02, 'w_q_b': 7.947457e-03, 'w_kv_b': 1.026613e-02, 'w_out': 1.453169e-02, 'final_norm_g': 1.598694e+01}


def _to_microbatches(a, axis):
    t = _jnp.moveaxis(a, axis, 0)
    t = t.reshape((N_MICROBATCH, t.shape[0] // N_MICROBATCH) + t.shape[1:])
    return _jnp.moveaxis(t, 1, axis + 1)


def setup_inputs(seed: int = 0) -> dict:
    inp = _fwd_setup_inputs(seed)
    key = _jax.random.fold_in(_jax.random.key(seed), 7919)
    shape, _ = _output_shape()
    out = dict(inp)
    out["loss_target"] = _jax.random.normal(_jax.random.fold_in(key, 0), shape, _jnp.float32)
    for i, name in enumerate(TWIN_WEIGHTS):
        w = inp[name].astype(_jnp.float32)
        if MOMENT_SCALE is None:
            s = _jnp.sqrt(_jnp.mean(_jnp.square(w)) + 1e-30)
        else:
            s = MOMENT_SCALE[name]
        km, kv = _jax.random.split(_jax.random.fold_in(key, i + 1))
        out[name] = w
        out["m_" + name] = s * _jax.random.normal(km, w.shape, _jnp.float32)
        out["v_" + name] = (s * s) * _jax.random.uniform(kv, w.shape, _jnp.float32, 0.5, 1.5)
    if N_MICROBATCH > 1:
        for name, axis in PER_EXAMPLE_BATCH_AXIS.items():
            out[name] = _to_microbatches(out[name], axis)
    return {'x': out['x'], 'attn_norm_g': out['attn_norm_g'], 'w_in': out['w_in'], 'swa_sinks': out['swa_sinks'], 'q_a_norm_g': out['q_a_norm_g'], 'kv_a_norm_g': out['kv_a_norm_g'], 'w_q_b': out['w_q_b'], 'w_kv_b': out['w_kv_b'], 'w_out': out['w_out'], 'final_norm_g': out['final_norm_g'], 'loss_target': out['loss_target'], 'm_attn_norm_g': out['m_attn_norm_g'], 'm_w_in': out['m_w_in'], 'm_swa_sinks': out['m_swa_sinks'], 'm_q_a_norm_g': out['m_q_a_norm_g'], 'm_kv_a_norm_g': out['m_kv_a_norm_g'], 'm_w_q_b': out['m_w_q_b'], 'm_w_kv_b': out['m_w_kv_b'], 'm_w_out': out['m_w_out'], 'm_final_norm_g': out['m_final_norm_g'], 'v_attn_norm_g': out['v_attn_norm_g'], 'v_w_in': out['v_w_in'], 'v_swa_sinks': out['v_swa_sinks'], 'v_q_a_norm_g': out['v_q_a_norm_g'], 'v_kv_a_norm_g': out['v_kv_a_norm_g'], 'v_w_q_b': out['v_w_q_b'], 'v_w_kv_b': out['v_w_kv_b'], 'v_w_out': out['v_w_out'], 'v_final_norm_g': out['v_final_norm_g']}


def _loss(weights, diff, rest, loss_target):
    with _jax.named_scope("forward"):
        args = {**rest, TWIN_DIFF_INPUT: diff, **{k: w.astype(_WEIGHT_DTYPES[k]) for k, w in weights.items()}}
        y = _forward(args)
    with _jax.named_scope("loss_head"):
        err = _jnp.square(y.astype(_jnp.float32) - loss_target)
        return 0.5 * _jnp.sum(_jnp.mean(err, axis=-1)) if err.ndim else 0.5 * err


def _adamw(w, g, m, v):
    m = ADAM_B1 * m + (1.0 - ADAM_B1) * g
    v = ADAM_B2 * v + (1.0 - ADAM_B2) * _jnp.square(g)
    m_hat = m / (1.0 - ADAM_B1 ** ADAM_STEP)
    v_hat = v / (1.0 - ADAM_B2 ** ADAM_STEP)
    delta = -ADAM_LR * (m_hat / (_jnp.sqrt(v_hat) + ADAM_EPS) + ADAM_WD * w)
    return delta, m, v


def reference(x, attn_norm_g, w_in, swa_sinks, q_a_norm_g, kv_a_norm_g, w_q_b, w_kv_b, w_out, final_norm_g, loss_target, m_attn_norm_g, m_w_in, m_swa_sinks, m_q_a_norm_g, m_kv_a_norm_g, m_w_q_b, m_w_kv_b, m_w_out, m_final_norm_g, v_attn_norm_g, v_w_in, v_swa_sinks, v_q_a_norm_g, v_kv_a_norm_g, v_w_q_b, v_w_kv_b, v_w_out, v_final_norm_g):
    given = dict(x=x, attn_norm_g=attn_norm_g, w_in=w_in, swa_sinks=swa_sinks, q_a_norm_g=q_a_norm_g, kv_a_norm_g=kv_a_norm_g, w_q_b=w_q_b, w_kv_b=w_kv_b, w_out=w_out, final_norm_g=final_norm_g, loss_target=loss_target, m_attn_norm_g=m_attn_norm_g, m_w_in=m_w_in, m_swa_sinks=m_swa_sinks, m_q_a_norm_g=m_q_a_norm_g, m_kv_a_norm_g=m_kv_a_norm_g, m_w_q_b=m_w_q_b, m_w_kv_b=m_w_kv_b, m_w_out=m_w_out, m_final_norm_g=m_final_norm_g, v_attn_norm_g=v_attn_norm_g, v_w_in=v_w_in, v_swa_sinks=v_swa_sinks, v_q_a_norm_g=v_q_a_norm_g, v_kv_a_norm_g=v_kv_a_norm_g, v_w_q_b=v_w_q_b, v_w_kv_b=v_w_kv_b, v_w_out=v_w_out, v_final_norm_g=v_final_norm_g)
    weights = {n: given[n] for n in TWIN_WEIGHTS}
    shared = {n: given[n] for n in SHARED_INPUTS}
    per_example = {n: given[n] for n in ['x']}
    grad_fn = _jax.value_and_grad(_loss, argnums=(0, 1))

    def one_microbatch(ex, loss_target):
        ex = dict(ex)
        diff = ex.pop(TWIN_DIFF_INPUT)
        return grad_fn(weights, diff, {**shared, **ex}, loss_target)

    if N_MICROBATCH == 1:
        loss, (grad_w, grad_x) = one_microbatch(per_example, given["loss_target"])
    else:
        def body(carry, xs):
            loss_sum, grad_sum = carry
            l_k, (gw_k, gx_k) = one_microbatch(xs[0], xs[1])
            with _jax.named_scope("update"):
                return (loss_sum + l_k, _jax.tree.map(_jnp.add, grad_sum, gw_k)), gx_k

        init = (_jnp.zeros((), _jnp.float32), _jax.tree.map(_jnp.zeros_like, weights))
        (loss, grad_w), grad_x = _jax.lax.scan(body, init, (per_example, given["loss_target"]))
    with _jax.named_scope("update"):
        delta_w, new_m, new_v = {}, {}, {}
        for n in TWIN_WEIGHTS:
            delta_w[n], new_m[n], new_v[n] = _adamw(weights[n], grad_w[n], given["m_" + n], given["v_" + n])
    return (loss, grad_x, *[grad_w[n] for n in TWIN_WEIGHTS], *[delta_w[n] for n in TWIN_WEIGHTS],
            *[new_m[n] for n in TWIN_WEIGHTS], *[new_v[n] for n in TWIN_WEIGHTS])
```

```python
import functools

import jax
import jax.numpy as jnp
from jax import lax
from jax.experimental import pallas as pl
from jax.experimental.pallas import tpu as pltpu

F32 = jnp.float32
BF16 = jnp.bfloat16
MXU_DTYPE = BF16

D_MODEL = 2048
DEPTH = 4
EPS = 1e-6
NEG = -1e30
BLOCK = 128
SWA_HEADS = 16
MLA_HEADS = 8
Q_RANK = 384
KV_RANK = 256
MLA_SCALE = 192 ** -0.5
ROPE_THETA = 10000.0
IN_WIDTH = 4032

ADAM_LR = 0.001
ADAM_B1 = 0.9
ADAM_B2 = 0.999
ADAM_EPS = 1e-08
ADAM_WD = 0.01
ADAM_STEP = 10

N_DEV = 8
MESH = pl.DeviceIdType.MESH

NP = 4096
QA, GA, GB, CQ, CKV, KR, KVA = 0, 1024, 2048, 3072, 3456, 3712, 3840

ROW_ATTN, ROW_QA, ROW_KVA, ROW_FINAL, ROW_SINK, ROW_LOSS, PACK_ROWS = 0, 64, 76, 84, 100, 101, 104

VMEM_LIMIT = 56 * 1024 * 1024
MLA_TILE = 512


def _sds(shape, dtype):
    return jax.ShapeDtypeStruct(shape, dtype)


def _params(n_axes):
    return pltpu.CompilerParams(dimension_semantics=("arbitrary",) * n_axes,
                                vmem_limit_bytes=VMEM_LIMIT)


def _mm(a, b):
    return jnp.dot(a.astype(MXU_DTYPE), b.astype(MXU_DTYPE), preferred_element_type=F32)


def _mm_nt(a, b):
    return lax.dot_general(a.astype(MXU_DTYPE), b.astype(MXU_DTYPE),
                           (((1,), (1,)), ((), ())), preferred_element_type=F32)


def _mm_tn(a, b):
    return lax.dot_general(a.astype(MXU_DTYPE), b.astype(MXU_DTYPE),
                           (((0,), (0,)), ((), ())), preferred_element_type=F32)


def _rownorm(x):
    r = lax.rsqrt(jnp.mean(x * x, axis=-1, keepdims=True) + EPS)
    return x * r, r


def _rownorm_bwd(dxh, xh, r):
    return r * (dxh - xh * jnp.mean(dxh * xh, axis=-1, keepdims=True))


def _rope(t, c, s1, s2):
    return t * c + pltpu.roll(t, 32, 1) * s1 + pltpu.roll(t, 96, 1) * s2


def _pad_in_cols(w):
    z = jnp.zeros(w.shape[:-1] + (64,), w.dtype)
    return jnp.concatenate([w[..., 0:1024], w[..., 1280:2304], w[..., 3008:4032], w[..., 2304:2688],
                            w[..., 2688:2944], w[..., 2944:3008], z, w[..., 1024:1152],
                            w[..., 1152:1280]], axis=-1)


def _unpad_in_cols(g):
    return jnp.concatenate([g[..., QA:QA + 1024], g[..., KVA:KVA + 256], g[..., GA:GA + 1024],
                            g[..., CQ:CQ + 384], g[..., CKV:CKV + 256], g[..., KR:KR + 64],
                            g[..., GB:GB + 1024]], axis=-1)


def _weights_from_gathered(g_in, g_qb, g_kvb, g_out):
    w_in = g_in.transpose(1, 2, 0, 3).reshape(DEPTH, D_MODEL, IN_WIDTH)
    w_in = _pad_in_cols(w_in)
    qb = g_qb.transpose(1, 2, 0, 3)
    rope = jnp.pad(qb[..., 128:], ((0, 0), (0, 0), (0, 0), (0, 64)))
    w_q = jnp.concatenate([qb[..., :128].reshape(DEPTH, Q_RANK, 1024),
                           rope.reshape(DEPTH, Q_RANK, 1024)], axis=-1)
    kvb = g_kvb.transpose(1, 2, 0, 3)
    w_kv = jnp.concatenate([kvb[..., :128].reshape(DEPTH, KV_RANK, 1024),
                            kvb[..., 128:].reshape(DEPTH, KV_RANK, 1024)], axis=-1)
    w_o = g_out.transpose(1, 0, 2, 3).reshape(DEPTH, D_MODEL, D_MODEL)
    return w_in, w_q, w_kv, w_o


def _grads_to_blocks(d_in, d_q, d_kv, d_o):
    b_in = _unpad_in_cols(d_in).reshape(DEPTH, D_MODEL, N_DEV, 504).transpose(2, 0, 1, 3)
    qn = d_q[..., :1024].reshape(DEPTH, Q_RANK, 8, 128)
    qr = d_q[..., 1024:].reshape(DEPTH, Q_RANK, 8, 128)[..., :64]
    b_q = jnp.concatenate([qn, qr], axis=-1).transpose(2, 0, 1, 3)
    kn = d_kv[..., :1024].reshape(DEPTH, KV_RANK, 8, 128)
    vv = d_kv[..., 1024:].reshape(DEPTH, KV_RANK, 8, 128)
    b_kv = jnp.concatenate([kn, vv], axis=-1).transpose(2, 0, 1, 3)
    b_o = d_o.reshape(DEPTH, N_DEV, 256, D_MODEL).transpose(1, 0, 2, 3)
    return b_in, b_q, b_kv, b_o


def _rope_tables(s):
    pos = jnp.arange(s, dtype=F32)
    inv_freq = ROPE_THETA ** (-jnp.arange(0, 64, 2, dtype=F32) / 64)
    ang = pos[:, None] * inv_freq[None, :]
    cos, sin = jnp.cos(ang), jnp.sin(ang)
    z32 = jnp.zeros((s, 32), F32)
    z64 = jnp.zeros((s, 64), F32)
    c = jnp.concatenate([cos, cos, z64], axis=1)
    s1 = jnp.concatenate([z32, sin, z64], axis=1)
    s2 = jnp.concatenate([-sin, z32, z64], axis=1)
    return c, s1, s2


def _in_proj(x, g, w):
    s = x.shape[0]
    tm, tn = min(512, s), 1024

    def body(x_ref, g_ref, w_ref, o_ref, h_ref):
        @pl.when(pl.program_id(1) == 0)
        def _():
            xh, _ = _rownorm(x_ref[...])
            h_ref[...] = (xh * g_ref[...]).astype(h_ref.dtype)
        o_ref[...] = jnp.dot(h_ref[...], w_ref[...], preferred_element_type=F32)

    return pl.pallas_call(
        body, name="in_proj", grid=(s // tm, NP // tn),
        in_specs=[pl.BlockSpec((tm, D_MODEL), lambda i, j: (i, 0)),
                  pl.BlockSpec((1, D_MODEL), lambda i, j: (0, 0)),
                  pl.BlockSpec((D_MODEL, tn), lambda i, j: (0, j))],
        out_specs=[pl.BlockSpec((tm, tn), lambda i, j: (i, j)),
                   pl.BlockSpec((tm, D_MODEL), lambda i, j: (i, 0))],
        out_shape=[_sds((s, NP), F32), _sds((s, D_MODEL), MXU_DTYPE)],
        compiler_params=_params(2),
    )(x, g, w)


def _swa_slopes():
    return [2.0 ** (-8.0 * (h + 1) / SWA_HEADS) for h in range(SWA_HEADS)]


def _swa_operands(kv_p, kv_c):
    kk = jnp.concatenate([kv_p[:, :128], kv_c[:, :128]], axis=0)
    vv = jnp.concatenate([kv_p[:, 128:], kv_c[:, 128:]], axis=0)
    left = lax.broadcasted_iota(jnp.int32, (2 * BLOCK, 128), 1) < 64

    def split(t):
        lo = jnp.where(left, t, 0.0)
        hi = jnp.where(left, 0.0, t)
        lefts = [lo, pltpu.roll(hi, 64, 1)]
        rights = [pltpu.roll(lo, 64, 1), hi]
        return lefts, rights

    return split(kk), split(vv), left


def _swa_mask(n):
    qi = lax.broadcasted_iota(jnp.int32, (BLOCK, 2 * BLOCK), 0)
    ki = lax.broadcasted_iota(jnp.int32, (BLOCK, 2 * BLOCK), 1)
    delta = BLOCK + qi - ki
    key_pos = (n - 1) * BLOCK + ki
    valid = (delta >= 0) & (delta < BLOCK) & (key_pos >= 0)
    return valid, delta.astype(F32)


def _swa_probs(qp, kmat, valid, deltaf, slope, sink):
    sc = _mm_nt(qp, kmat)
    sc = jnp.where(valid, sc - slope * deltaf, NEG)
    m = jnp.maximum(jnp.max(sc, axis=1, keepdims=True), sink)
    ex = jnp.exp(sc - m)
    es = jnp.exp(sink - m)
    l = jnp.sum(ex, axis=1, keepdims=True) + es
    return ex / l, es / l


def _swa_fwd(proj, sinks):
    s = proj.shape[0]
    nb = s // BLOCK
    slopes = _swa_slopes()

    def body(sink_ref, q_ref, kp_ref, kc_ref, o_ref):
        n = pl.program_id(0)
        (k_l, k_r), (v_l, v_r), _ = _swa_operands(kp_ref[...], kc_ref[...])
        valid, deltaf = _swa_mask(n)
        for pair in range(SWA_HEADS // 2):
            j = pair // 4
            qp = q_ref[:, 128 * pair:128 * (pair + 1)] * 0.125
            out = jnp.zeros((BLOCK, 128), F32)
            for e in range(2):
                h = 2 * pair + e
                kmat = (k_l, k_r)[e][j]
                vmat = (v_l, v_r)[e][j]
                p, _ = _swa_probs(qp, kmat, valid, deltaf, slopes[h], sink_ref[h])
                out = out + _mm(p, vmat)
            o_ref[:, 128 * pair:128 * (pair + 1)] = out

    return pl.pallas_call(
        body, name="swa_fwd", grid=(nb,),
        in_specs=[pl.BlockSpec(memory_space=pltpu.SMEM),
                  pl.BlockSpec((BLOCK, 1024), lambda n: (n, 0)),
                  pl.BlockSpec((BLOCK, 256), lambda n: (jnp.maximum(n - 1, 0), KVA // 256)),
                  pl.BlockSpec((BLOCK, 256), lambda n: (n, KVA // 256))],
        out_specs=pl.BlockSpec((BLOCK, 1024), lambda n: (n, 0)),
        out_shape=_sds((s, 1024), F32),
        compiler_params=_params(1),
    )(sinks, proj, proj, proj)


def _mla_prep(proj, gq, gkv, w_q, w_kv, tabs):
    s = proj.shape[0]
    tm = min(512, s)
    c, s1, s2 = tabs

    def body(p_ref, gq_ref, gkv_ref, wq_ref, wkv_ref, c_ref, s1_ref, s2_ref,
             q_ref, k_ref, v_ref, cqn_ref, ckvn_ref):
        cqh, _ = _rownorm(p_ref[:, 0:384])
        ckvh, _ = _rownorm(p_ref[:, 384:640])
        cqn = (cqh * gq_ref[...]).astype(MXU_DTYPE)
        ckvn = (ckvh * gkv_ref[...]).astype(MXU_DTYPE)
        cqn_ref[...] = cqn
        ckvn_ref[...] = ckvn
        q = _mm(cqn, wq_ref[...])
        kv = _mm(ckvn, wkv_ref[...])
        cc, ss1, ss2 = c_ref[...], s1_ref[...], s2_ref[...]
        krr = _rope(p_ref[:, 640:768], cc, ss1, ss2).astype(k_ref.dtype)
        for h in range(MLA_HEADS):
            q_ref[h, :, 0:128] = q[:, 128 * h:128 * (h + 1)].astype(q_ref.dtype)
            q_ref[h, :, 128:256] = _rope(q[:, 1024 + 128 * h:1024 + 128 * (h + 1)],
                                         cc, ss1, ss2).astype(q_ref.dtype)
            k_ref[h, :, 0:128] = kv[:, 128 * h:128 * (h + 1)].astype(k_ref.dtype)
            k_ref[h, :, 128:256] = krr
            v_ref[h] = kv[:, 1024 + 128 * h:1024 + 128 * (h + 1)].astype(v_ref.dtype)

    row = lambda i: (i, 0)
    fixed = lambda i: (0, 0)
    return pl.pallas_call(
        body, name="mla_prep", grid=(s // tm,),
        in_specs=[pl.BlockSpec((tm, 768), lambda i: (i, CQ // 768)),
                  pl.BlockSpec((1, Q_RANK), fixed), pl.BlockSpec((1, KV_RANK), fixed),
                  pl.BlockSpec((Q_RANK, 2048), fixed), pl.BlockSpec((KV_RANK, 2048), fixed),
                  pl.BlockSpec((tm, 128), row), pl.BlockSpec((tm, 128), row),
                  pl.BlockSpec((tm, 128), row)],
        out_specs=[pl.BlockSpec((MLA_HEADS, tm, 256), lambda i: (0, i, 0)),
                   pl.BlockSpec((MLA_HEADS, tm, 256), lambda i: (0, i, 0)),
                   pl.BlockSpec((MLA_HEADS, tm, 128), lambda i: (0, i, 0)),
                   pl.BlockSpec((tm, Q_RANK), row), pl.BlockSpec((tm, KV_RANK), row)],
        out_shape=[_sds((MLA_HEADS, s, 256), MXU_DTYPE), _sds((MLA_HEADS, s, 256), MXU_DTYPE),
                   _sds((MLA_HEADS, s, 128), MXU_DTYPE),
                   _sds((s, Q_RANK), MXU_DTYPE), _sds((s, KV_RANK), MXU_DTYPE)],
        compiler_params=_params(1),
    )(proj, gq, gkv, w_q, w_kv, c, s1, s2)


def _causal_scores(q, k, i, j, t):
    sc = _mm_nt(q, k) * MLA_SCALE
    row = i * t + lax.broadcasted_iota(jnp.int32, (t, t), 0)
    col = j * t + lax.broadcasted_iota(jnp.int32, (t, t), 1)
    return jnp.where(col <= row, sc, NEG)


def _mla_fwd(qh, kh, vh):
    s = qh.shape[1]
    t = min(MLA_TILE, s)
    nq = s // t

    def body(q_ref, k_ref, v_ref, o_ref, lse_ref, m_ref, l_ref, acc_ref):
        i, j = pl.program_id(1), pl.program_id(2)

        @pl.when(j == 0)
        def _():
            m_ref[...] = jnp.full(m_ref.shape, NEG, F32)
            l_ref[...] = jnp.zeros(l_ref.shape, F32)
            acc_ref[...] = jnp.zeros(acc_ref.shape, F32)

        @pl.when(j <= i)
        def _():
            sc = _causal_scores(q_ref[...], k_ref[...], i, j, t)
            m_prev = m_ref[...]
            m_new = jnp.maximum(m_prev, jnp.max(sc, axis=1, keepdims=True))
            alpha = jnp.exp(m_prev - m_new)
            p = jnp.exp(sc - m_new[:, :1])
            l_ref[...] = alpha * l_ref[...] + jnp.sum(p, axis=1, keepdims=True)
            acc_ref[...] = alpha * acc_ref[...] + _mm(p, v_ref[...])
            m_ref[...] = m_new

        @pl.when(j == i)
        def _():
            o_ref[...] = acc_ref[...] / l_ref[...]
            lse_ref[...] = m_ref[...] + jnp.log(l_ref[...])

    kv_map = lambda h, i, j: (h, jnp.minimum(j, i), 0)
    return pl.pallas_call(
        body, name="mla_fwd", grid=(MLA_HEADS, nq, nq),
        in_specs=[pl.BlockSpec((None, t, 256), lambda h, i, j: (h, i, 0)),
                  pl.BlockSpec((None, t, 256), kv_map),
                  pl.BlockSpec((None, t, 128), kv_map)],
        out_specs=[pl.BlockSpec((t, 128), lambda h, i, j: (i, h)),
                   pl.BlockSpec((None, t, 128), lambda h, i, j: (h, i, 0))],
        out_shape=[_sds((s, 1024), F32), _sds((MLA_HEADS, s, 128), F32)],
        scratch_shapes=[pltpu.VMEM((t, 128), F32), pltpu.VMEM((t, 128), F32),
                        pltpu.VMEM((t, 128), F32)],
        compiler_params=_params(3),
    )(qh, kh, vh)


def _silu_parts(g):
    sg = jax.nn.sigmoid(g)
    return g * sg, sg * (1.0 + g * (1.0 - sg))


def _out_proj(x, proj, swa, mla, w_out):
    s = x.shape[0]
    tm = min(256, s)

    def body(x_ref, ga_ref, gb_ref, a_ref, b_ref, w_ref, xo_ref, y_ref):
        sa, _ = _silu_parts(ga_ref[...])
        sb, _ = _silu_parts(gb_ref[...])
        y_ref[:, 0:1024] = (a_ref[...] * sa).astype(y_ref.dtype)
        y_ref[:, 1024:2048] = (b_ref[...] * sb).astype(y_ref.dtype)
        xo_ref[...] = x_ref[...] + jnp.dot(y_ref[...], w_ref[...], preferred_element_type=F32)

    row = lambda i: (i, 0)
    return pl.pallas_call(
        body, name="out_proj", grid=(s // tm,),
        in_specs=[pl.BlockSpec((tm, D_MODEL), row),
                  pl.BlockSpec((tm, 1024), lambda i: (i, GA // 1024)),
                  pl.BlockSpec((tm, 1024), lambda i: (i, GB // 1024)),
                  pl.BlockSpec((tm, 1024), row), pl.BlockSpec((tm, 1024), row),
                  pl.BlockSpec((D_MODEL, D_MODEL), lambda i: (0, 0))],
        out_specs=[pl.BlockSpec((tm, D_MODEL), row), pl.BlockSpec((tm, D_MODEL), row)],
        out_shape=[_sds((s, D_MODEL), F32), _sds((s, D_MODEL), MXU_DTYPE)],
        compiler_params=_params(1),
    )(x, proj, proj, swa, mla, w_out)


def _final_loss(x, tgt, g):
    s = x.shape[0]
    tm = min(512, s)

    def body(x_ref, t_ref, g_ref, dx_ref, dg_ref, loss_ref):
        @pl.when(pl.program_id(0) == 0)
        def _():
            dg_ref[...] = jnp.zeros(dg_ref.shape, F32)
            loss_ref[...] = jnp.zeros(loss_ref.shape, F32)
        xh, r = _rownorm(x_ref[...])
        gg = g_ref[...]
        err = xh * gg - t_ref[...]
        per_row = jnp.mean(err * err, axis=-1, keepdims=True)
        loss_ref[...] += 0.5 * jnp.sum(per_row, axis=0, keepdims=True)
        dy = err * (1.0 / D_MODEL)
        dg_ref[...] += jnp.sum(dy * xh, axis=0, keepdims=True)
        dx_ref[...] = _rownorm_bwd(dy * gg, xh, r)

    row = lambda i: (i, 0)
    fixed = lambda i: (0, 0)
    return pl.pallas_call(
        body, name="final_loss", grid=(s // tm,),
        in_specs=[pl.BlockSpec((tm, D_MODEL), row), pl.BlockSpec((tm, D_MODEL), row),
                  pl.BlockSpec((1, D_MODEL), fixed)],
        out_specs=[pl.BlockSpec((tm, D_MODEL), row), pl.BlockSpec((1, D_MODEL), fixed),
                   pl.BlockSpec((8, 128), fixed)],
        out_shape=[_sds((s, D_MODEL), F32), _sds((1, D_MODEL), F32), _sds((8, 128), F32)],
        compiler_params=_params(1),
    )(x, tgt, g)


def _out_proj_bwd(dx, proj, swa, mla, w_out):
    s = dx.shape[0]
    tm = min(256, s)

    def body(dx_ref, ga_ref, gb_ref, a_ref, b_ref, w_ref, doa_ref, dob_ref, dg_ref):
        dy = _mm_nt(dx_ref[...], w_ref[...])
        sa, dsa = _silu_parts(ga_ref[...])
        sb, dsb = _silu_parts(gb_ref[...])
        dya, dyb = dy[:, 0:1024], dy[:, 1024:2048]
        doa_ref[...] = dya * sa
        dob_ref[...] = dyb * sb
        dg_ref[:, 0:1024] = (dya * a_ref[...] * dsa).astype(dg_ref.dtype)
        dg_ref[:, 1024:2048] = (dyb * b_ref[...] * dsb).astype(dg_ref.dtype)

    row = lambda i: (i, 0)
    return pl.pallas_call(
        body, name="out_proj_bwd", grid=(s // tm,),
        in_specs=[pl.BlockSpec((tm, D_MODEL), row),
                  pl.BlockSpec((tm, 1024), lambda i: (i, GA // 1024)),
                  pl.BlockSpec((tm, 1024), lambda i: (i, GB // 1024)),
                  pl.BlockSpec((tm, 1024), row), pl.BlockSpec((tm, 1024), row),
                  pl.BlockSpec((D_MODEL, D_MODEL), lambda i: (0, 0))],
        out_specs=[pl.BlockSpec((tm, 1024), row), pl.BlockSpec((tm, 1024), row),
                   pl.BlockSpec((tm, D_MODEL), row)],
        out_shape=[_sds((s, 1024), F32), _sds((s, 1024), F32), _sds((s, D_MODEL), MXU_DTYPE)],
        compiler_params=_params(1),
    )(dx, proj, proj, swa, mla, w_out)


def _matmul_tn(a, b, name):
    s, m = a.shape
    n = b.shape[1]
    tm, tn, tk = min(512, m), min(1024, n), min(512, s)

    def body(a_ref, b_ref, o_ref):
        @pl.when(pl.program_id(2) == 0)
        def _():
            o_ref[...] = jnp.zeros(o_ref.shape, F32)
        o_ref[...] += _mm_tn(a_ref[...], b_ref[...])

    return pl.pallas_call(
        body, name=name, grid=(m // tm, n // tn, s // tk),
        in_specs=[pl.BlockSpec((tk, tm), lambda i, j, k: (k, i)),
                  pl.BlockSpec((tk, tn), lambda i, j, k: (k, j))],
        out_specs=pl.BlockSpec((tm, tn), lambda i, j, k: (i, j)),
        out_shape=_sds((m, n), F32),
        compiler_params=_params(3),
    )(a, b)


def _swa_bwd(proj, sinks, do, o):
    s = proj.shape[0]
    nb = s // BLOCK
    slopes = _swa_slopes()

    def body(sink_ref, q_ref, kp_ref, kc_ref, do_ref, o_ref, dq_ref, dkv_ref, dsink_ref, carry_ref):
        n = pl.program_id(0)

        @pl.when(n == 0)
        def _():
            carry_ref[...] = jnp.zeros(carry_ref.shape, F32)
            dsink_ref[...] = jnp.zeros(dsink_ref.shape, F32)

        @pl.when(n < nb)
        def _():
            (k_l, k_r), (v_l, v_r), left = _swa_operands(kp_ref[...], kc_ref[...])
            valid, deltaf = _swa_mask(n)
            left_q = lax.broadcasted_iota(jnp.int32, (BLOCK, 128), 1) < 64
            lane_s = lax.broadcasted_iota(jnp.int32, (8, 128), 1)
            zeros = jnp.zeros((2 * BLOCK, 128), F32)
            dk_l, dk_r, dv_l, dv_r = [zeros, zeros], [zeros, zeros], [zeros, zeros], [zeros, zeros]
            dsink = jnp.zeros((8, 128), F32)
            for pair in range(SWA_HEADS // 2):
                j = pair // 4
                cols = slice(128 * pair, 128 * (pair + 1))
                qp = q_ref[:, cols] * 0.125
                dop = do_ref[:, cols]
                prod = dop * o_ref[:, cols]
                dlt = [jnp.sum(jnp.where(left_q, prod, 0.0), axis=1, keepdims=True),
                       jnp.sum(jnp.where(left_q, 0.0, prod), axis=1, keepdims=True)]
                dq = jnp.zeros((BLOCK, 128), F32)
                for e in range(2):
                    h = 2 * pair + e
                    kmat = (k_l, k_r)[e][j]
                    vmat = (v_l, v_r)[e][j]
                    p, ps = _swa_probs(qp, kmat, valid, deltaf, slopes[h], sink_ref[h])
                    ds = p * (_mm_nt(dop, vmat) - dlt[e])
                    dsh = -jnp.sum(ps * dlt[e], axis=0, keepdims=True)
                    dsink = dsink + jnp.where(lane_s == h, dsh, 0.0)
                    dq = dq + _mm(ds, kmat)
                    dv_full = _mm_tn(p, dop)
                    dk_full = _mm_tn(ds, qp)
                    if e == 0:
                        dv_l[j] = dv_l[j] + dv_full
                        dk_l[j] = dk_l[j] + dk_full
                    else:
                        dv_r[j] = dv_r[j] + dv_full
                        dk_r[j] = dk_r[j] + dk_full
                dq_ref[:, cols] = (dq * 0.125).astype(dq_ref.dtype)

            def merge(t_l, t_r):
                head0 = t_l[0] + pltpu.roll(t_r[0], 64, 1)
                head1 = t_r[1] + pltpu.roll(t_l[1], 64, 1)
                return jnp.where(left, head0, head1)

            contrib = jnp.concatenate([merge(dk_l, dk_r), merge(dv_l, dv_r)], axis=1)
            dkv_ref[...] = (carry_ref[...] + contrib[0:BLOCK]).astype(dkv_ref.dtype)
            carry_ref[...] = contrib[BLOCK:2 * BLOCK]
            dsink_ref[...] += dsink

        @pl.when(n == nb)
        def _():
            dkv_ref[...] = carry_ref[...].astype(dkv_ref.dtype)

    cur = lambda n: (jnp.minimum(n, nb - 1), 0)
    return pl.pallas_call(
        body, name="swa_bwd", grid=(nb + 1,),
        in_specs=[pl.BlockSpec(memory_space=pltpu.SMEM),
                  pl.BlockSpec((BLOCK, 1024), cur),
                  pl.BlockSpec((BLOCK, 256), lambda n: (jnp.clip(n - 1, 0, nb - 1), KVA // 256)),
                  pl.BlockSpec((BLOCK, 256), lambda n: (jnp.minimum(n, nb - 1), KVA // 256)),
                  pl.BlockSpec((BLOCK, 1024), cur), pl.BlockSpec((BLOCK, 1024), cur)],
        out_specs=[pl.BlockSpec((BLOCK, 1024), cur),
                   pl.BlockSpec((BLOCK, 256), lambda n: (jnp.maximum(n - 1, 0), 0)),
                   pl.BlockSpec((8, 128), lambda n: (0, 0))],
        out_shape=[_sds((s, 1024), MXU_DTYPE), _sds((s, 256), MXU_DTYPE), _sds((8, 128), F32)],
        scratch_shapes=[pltpu.VMEM((BLOCK, 256), F32)],
        compiler_params=_params(1),
    )(sinks, proj, proj, proj, do, o)


def _mla_bwd_dq(qh, kh, vh, do, o, lse):
    s = qh.shape[1]
    t = min(MLA_TILE, s)
    nq = s // t

    def body(q_ref, k_ref, v_ref, do_ref, o_ref, lse_ref, dq_ref, dlt_ref, acc_ref):
        i, j = pl.program_id(1), pl.program_id(2)

        @pl.when(j == 0)
        def _():
            dlt = jnp.sum(do_ref[...] * o_ref[...], axis=1, keepdims=True)
            dlt_ref[...] = jnp.broadcast_to(dlt, dlt_ref.shape)
            acc_ref[...] = jnp.zeros(acc_ref.shape, F32)

        @pl.when(j <= i)
        def _():
            k = k_ref[...]
            sc = _causal_scores(q_ref[...], k, i, j, t)
            p = jnp.exp(sc - lse_ref[:, 0:1])
            dp = _mm_nt(do_ref[...], v_ref[...])
            ds = p * (dp - dlt_ref[:, 0:1]) * MLA_SCALE
            acc_ref[...] += _mm(ds, k)

        @pl.when(j == i)
        def _():
            dq_ref[...] = acc_ref[...]

    kv_map = lambda h, i, j: (h, jnp.minimum(j, i), 0)
    q_map = lambda h, i, j: (h, i, 0)
    return pl.pallas_call(
        body, name="mla_bwd_dq", grid=(MLA_HEADS, nq, nq),
        in_specs=[pl.BlockSpec((None, t, 256), q_map), pl.BlockSpec((None, t, 256), kv_map),
                  pl.BlockSpec((None, t, 128), kv_map),
                  pl.BlockSpec((t, 128), lambda h, i, j: (i, h)),
                  pl.BlockSpec((t, 128), lambda h, i, j: (i, h)),
                  pl.BlockSpec((None, t, 128), q_map)],
        out_specs=pl.BlockSpec((None, t, 256), q_map),
        out_shape=_sds((MLA_HEADS, s, 256), F32),
        scratch_shapes=[pltpu.VMEM((t, 128), F32), pltpu.VMEM((t, 256), F32)],
        compiler_params=_params(3),
    )(qh, kh, vh, do, o, lse)


def _mla_bwd_dkv(qh, kh, vh, do, o, lse):
    s = qh.shape[1]
    t = min(MLA_TILE, s)
    nq = s // t

    def body(q_ref, k_ref, v_ref, do_ref, o_ref, lse_ref, dk_ref, dv_ref, dk_acc, dv_acc):
        j, i = pl.program_id(1), pl.program_id(2)

        @pl.when(i == 0)
        def _():
            dk_acc[...] = jnp.zeros(dk_acc.shape, F32)
            dv_acc[...] = jnp.zeros(dv_acc.shape, F32)

        @pl.when(i >= j)
        def _():
            q = q_ref[...]
            dout = do_ref[...]
            sc = _causal_scores(q, k_ref[...], i, j, t)
            p = jnp.exp(sc - lse_ref[:, 0:1])
            dv_acc[...] += _mm_tn(p, dout)
            dlt = jnp.sum(dout * o_ref[...], axis=1, keepdims=True)
            ds = p * (_mm_nt(dout, v_ref[...]) - dlt) * MLA_SCALE
            dk_acc[...] += _mm_tn(ds, q)

        @pl.when(i == nq - 1)
        def _():
            dk_ref[...] = dk_acc[...]
            dv_ref[...] = dv_acc[...]

    q_map = lambda h, j, i: (h, jnp.maximum(i, j), 0)
    kv_map = lambda h, j, i: (h, j, 0)
    row_map = lambda h, j, i: (jnp.maximum(i, j), h)
    return pl.pallas_call(
        body, name="mla_bwd_dkv", grid=(MLA_HEADS, nq, nq),
        in_specs=[pl.BlockSpec((None, t, 256), q_map), pl.BlockSpec((None, t, 256), kv_map),
                  pl.BlockSpec((None, t, 128), kv_map),
                  pl.BlockSpec((t, 128), row_map), pl.BlockSpec((t, 128), row_map),
                  pl.BlockSpec((None, t, 128), q_map)],
        out_specs=[pl.BlockSpec((None, t, 256), kv_map), pl.BlockSpec((None, t, 128), kv_map)],
        out_shape=[_sds((MLA_HEADS, s, 256), F32), _sds((MLA_HEADS, s, 128), F32)],
        scratch_shapes=[pltpu.VMEM((t, 256), F32), pltpu.VMEM((t, 128), F32)],
        compiler_params=_params(3),
    )(qh, kh, vh, do, o, lse)


def _mla_prep_bwd(dqh, dkh, dvh, proj, gq, gkv, w_q, w_kv, tabs):
    s = proj.shape[0]
    tm = min(256, s)
    c, s1, s2 = tabs

    def body(dq_ref, dk_ref, dv_ref, p_ref, gq_ref, gkv_ref, wq_ref, wkv_ref,
             c_ref, s1_ref, s2_ref, dp_ref, dqf_ref, dkvf_ref, dgq_ref, dgkv_ref):
        @pl.when(pl.program_id(0) == 0)
        def _():
            dgq_ref[...] = jnp.zeros(dgq_ref.shape, F32)
            dgkv_ref[...] = jnp.zeros(dgkv_ref.shape, F32)
        cc, ns1, ns2 = c_ref[...], -s1_ref[...], -s2_ref[...]
        dkr = jnp.zeros((tm, 128), F32)
        for h in range(MLA_HEADS):
            dqf_ref[:, 128 * h:128 * (h + 1)] = dq_ref[h, :, 0:128].astype(dqf_ref.dtype)
            dqf_ref[:, 1024 + 128 * h:1024 + 128 * (h + 1)] = _rope(
                dq_ref[h, :, 128:256], cc, ns1, ns2).astype(dqf_ref.dtype)
            dkvf_ref[:, 128 * h:128 * (h + 1)] = dk_ref[h, :, 0:128].astype(dkvf_ref.dtype)
            dkvf_ref[:, 1024 + 128 * h:1024 + 128 * (h + 1)] = dv_ref[h].astype(dkvf_ref.dtype)
            dkr = dkr + dk_ref[h, :, 128:256]
        dcqn = _mm_nt(dqf_ref[...], wq_ref[...])
        dckvn = _mm_nt(dkvf_ref[...], wkv_ref[...])
        cqh, rq = _rownorm(p_ref[:, 0:384])
        ckvh, rkv = _rownorm(p_ref[:, 384:640])
        dgq_ref[...] += jnp.sum(dcqn * cqh, axis=0, keepdims=True)
        dgkv_ref[...] += jnp.sum(dckvn * ckvh, axis=0, keepdims=True)
        dp_ref[:, 0:384] = _rownorm_bwd(dcqn * gq_ref[...], cqh, rq).astype(dp_ref.dtype)
        dp_ref[:, 384:640] = _rownorm_bwd(dckvn * gkv_ref[...], ckvh, rkv).astype(dp_ref.dtype)
        dp_ref[:, 640:768] = _rope(dkr, cc, ns1, ns2).astype(dp_ref.dtype)

    row = lambda i: (i, 0)
    fixed = lambda i: (0, 0)
    head = lambda i: (0, i, 0)
    return pl.pallas_call(
        body, name="mla_prep_bwd", grid=(s // tm,),
        in_specs=[pl.BlockSpec((MLA_HEADS, tm, 256), head), pl.BlockSpec((MLA_HEADS, tm, 256), head),
                  pl.BlockSpec((MLA_HEADS, tm, 128), head),
                  pl.BlockSpec((tm, 768), lambda i: (i, CQ // 768)),
                  pl.BlockSpec((1, Q_RANK), fixed), pl.BlockSpec((1, KV_RANK), fixed),
                  pl.BlockSpec((Q_RANK, 2048), fixed), pl.BlockSpec((KV_RANK, 2048), fixed),
                  pl.BlockSpec((tm, 128), row), pl.BlockSpec((tm, 128), row),
                  pl.BlockSpec((tm, 128), row)],
        out_specs=[pl.BlockSpec((tm, 768), row), pl.BlockSpec((tm, 2048), row),
                   pl.BlockSpec((tm, 2048), row),
                   pl.BlockSpec((1, Q_RANK), fixed), pl.BlockSpec((1, KV_RANK), fixed)],
        out_shape=[_sds((s, 768), MXU_DTYPE), _sds((s, 2048), MXU_DTYPE), _sds((s, 2048), MXU_DTYPE),
                   _sds((1, Q_RANK), F32), _sds((1, KV_RANK), F32)],
        compiler_params=_params(1),
    )(dqh, dkh, dvh, proj, gq, gkv, w_q, w_kv, c, s1, s2)


def _in_proj_bwd(dproj, w, x, dx_out, g):
    s = x.shape[0]
    tm, tk = min(512, s), 1024
    nk = NP // tk

    def body(dp_ref, w_ref, x_ref, dxo_ref, g_ref, dx_ref, dg_ref, acc_ref):
        i, k = pl.program_id(0), pl.program_id(1)

        @pl.when((i == 0) & (k == 0))
        def _():
            dg_ref[...] = jnp.zeros(dg_ref.shape, F32)

        @pl.when(k == 0)
        def _():
            acc_ref[...] = jnp.zeros(acc_ref.shape, F32)

        acc_ref[...] += _mm_nt(dp_ref[...], w_ref[...])

        @pl.when(k == nk - 1)
        def _():
            dh = acc_ref[...]
            xh, r = _rownorm(x_ref[...])
            dg_ref[...] += jnp.sum(dh * xh, axis=0, keepdims=True)
            dx_ref[...] = dxo_ref[...] + _rownorm_bwd(dh * g_ref[...], xh, r)

    row = lambda i, k: (i, 0)
    fixed = lambda i, k: (0, 0)
    return pl.pallas_call(
        body, name="in_proj_bwd", grid=(s // tm, nk),
        in_specs=[pl.BlockSpec((tm, tk), lambda i, k: (i, k)),
                  pl.BlockSpec((D_MODEL, tk), lambda i, k: (0, k)),
                  pl.BlockSpec((tm, D_MODEL), row), pl.BlockSpec((tm, D_MODEL), row),
                  pl.BlockSpec((1, D_MODEL), fixed)],
        out_specs=[pl.BlockSpec((tm, D_MODEL), row), pl.BlockSpec((1, D_MODEL), fixed)],
        out_shape=[_sds((s, D_MODEL), F32), _sds((1, D_MODEL), F32)],
        scratch_shapes=[pltpu.VMEM((tm, D_MODEL), F32)],
        compiler_params=_params(2),
    )(dproj, w, x, dx_out, g)


def _reduce_adamw(parts, w, m, v, name):
    rows, cols = w.shape
    lanes = -(-cols // 128) * 128
    tr = rows
    for cand in (1024, 512, 256, 128, 64, 32, 16, 8):
        if rows % cand == 0 and N_DEV * cand * lanes * 4 <= 8 * 1024 * 1024:
            tr = cand
            break

    def body(p_ref, w_ref, m_ref, v_ref, g_ref, d_ref, nm_ref, nv_ref):
        g = p_ref[0]
        for k in range(1, N_DEV):
            g = g + p_ref[k]
        m2 = ADAM_B1 * m_ref[...] + (1.0 - ADAM_B1) * g
        v2 = ADAM_B2 * v_ref[...] + (1.0 - ADAM_B2) * (g * g)
        m_hat = m2 / (1.0 - ADAM_B1 ** ADAM_STEP)
        v_hat = v2 / (1.0 - ADAM_B2 ** ADAM_STEP)
        g_ref[...] = g
        d_ref[...] = -ADAM_LR * (m_hat / (jnp.sqrt(v_hat) + ADAM_EPS) + ADAM_WD * w_ref[...])
        nm_ref[...] = m2
        nv_ref[...] = v2

    blk = pl.BlockSpec((tr, cols), lambda i: (i, 0))
    return pl.pallas_call(
        body, name=name, grid=(rows // tr,),
        in_specs=[pl.BlockSpec((N_DEV, tr, cols), lambda i: (0, i, 0)), blk, blk, blk],
        out_specs=[blk, blk, blk, blk],
        out_shape=[_sds((rows, cols), F32)] * 4,
        compiler_params=_params(1),
    )(parts, w, m, v)


def _position():
    x, y, c = lax.axis_index("x"), lax.axis_index("y"), lax.axis_index("c")
    return x, y, c


def _index(px, py, pc):
    return 4 * px + 2 * py + pc


HBM_SPEC = pl.BlockSpec(memory_space=pltpu.HBM)


def _all_gather(shards, name):
    na = len(shards)

    def body(*refs):
        ins, outs = refs[:na], refs[na:2 * na]
        send_sems, recv_sems, local_sems = refs[2 * na:]
        x, y, c = _position()
        me, sibling = (x, y, c), (x, y, 1 - c)
        chips = [(1 - x, y), (x, 1 - y), (1 - x, 1 - y)]

        def copy(a, k, block, to, src=None):
            rows = outs[a].at[_index(*block)]
            return pltpu.make_async_remote_copy(
                src_ref=rows if src is None else src, dst_ref=rows,
                send_sem=send_sems.at[7 * a + k], recv_sem=recv_sems.at[7 * a + k],
                device_id=to, device_id_type=MESH)

        mine = [pltpu.make_async_copy(ins[a], outs[a].at[_index(*me)], local_sems.at[a])
                for a in range(na)]
        for cp in mine:
            cp.start()
        first = []
        for a in range(na):
            first.append(copy(a, 0, me, sibling, src=ins[a]))
            for j, chip in enumerate(chips):
                first.append(copy(a, 1 + j, me, (*chip, c), src=ins[a]))
        for cp in first:
            cp.start()
        passed = []
        for j, chip in enumerate(chips):
            for a in range(na):
                copy(a, 1 + j, (*chip, c), me).wait_recv()
                cp = copy(a, 4 + j, (*chip, c), sibling)
                cp.start()
                passed.append(cp)
        for a in range(na):
            copy(a, 0, sibling, me).wait_recv()
        for j, chip in enumerate(chips):
            for a in range(na):
                copy(a, 4 + j, (*chip, 1 - c), me).wait_recv()
        for cp in first + passed:
            cp.wait_send()
        for cp in mine:
            cp.wait()

    return pl.pallas_call(
        body, name=name,
        in_specs=[HBM_SPEC] * na, out_specs=[HBM_SPEC] * na,
        out_shape=[_sds((N_DEV,) + a.shape, a.dtype) for a in shards],
        scratch_shapes=[pltpu.SemaphoreType.DMA((7 * na,)), pltpu.SemaphoreType.DMA((7 * na,)),
                        pltpu.SemaphoreType.DMA((na,))],
    )(*shards)


def _all_to_all(blocks, name):
    na = len(blocks)

    def body(*refs):
        ins, outs = refs[:na], refs[na:2 * na]
        send_sems, recv_sems, local_sems = refs[2 * na:]
        x, y, c = _position()
        me = _index(x, y, c)
        peers = [(x ^ ((k >> 2) & 1), y ^ ((k >> 1) & 1), c ^ (k & 1)) for k in range(1, N_DEV)]

        def copy(a, k):
            peer = peers[k]
            return pltpu.make_async_remote_copy(
                src_ref=ins[a].at[_index(*peer)], dst_ref=outs[a].at[me],
                send_sem=send_sems.at[7 * a + k], recv_sem=recv_sems.at[7 * a + k],
                device_id=peer, device_id_type=MESH)

        def arrival(a, k):
            landed = outs[a].at[_index(*peers[k])]
            return pltpu.make_async_remote_copy(
                src_ref=landed, dst_ref=landed,
                send_sem=send_sems.at[7 * a + k], recv_sem=recv_sems.at[7 * a + k],
                device_id=peers[k], device_id_type=MESH)

        mine = [pltpu.make_async_copy(ins[a].at[me], outs[a].at[me], local_sems.at[a])
                for a in range(na)]
        for cp in mine:
            cp.start()
        sends = [copy(a, k) for a in range(na) for k in range(N_DEV - 1)]
        for cp in sends:
            cp.start()
        for a in range(na):
            for k in range(N_DEV - 1):
                arrival(a, k).wait_recv()
        for cp in sends:
            cp.wait_send()
        for cp in mine:
            cp.wait()

    return pl.pallas_call(
        body, name=name,
        in_specs=[HBM_SPEC] * na, out_specs=[HBM_SPEC] * na,
        out_shape=[_sds(a.shape, a.dtype) for a in blocks],
        scratch_shapes=[pltpu.SemaphoreType.DMA((7 * na,)), pltpu.SemaphoreType.DMA((7 * na,)),
                        pltpu.SemaphoreType.DMA((na,))],
    )(*blocks)


def _all_gather_direct(shards, name):
    na = len(shards)

    def body(*refs):
        ins, outs = refs[:na], refs[na:2 * na]
        send_sems, recv_sems, local_sems = refs[2 * na:]
        x, y, c = _position()
        me = _index(x, y, c)
        peers = [(x ^ ((k >> 2) & 1), y ^ ((k >> 1) & 1), c ^ (k & 1)) for k in range(1, N_DEV)]

        def copy(a, k, block):
            return pltpu.make_async_remote_copy(
                src_ref=ins[a], dst_ref=outs[a].at[block],
                send_sem=send_sems.at[7 * a + k], recv_sem=recv_sems.at[7 * a + k],
                device_id=peers[k], device_id_type=MESH)

        mine = [pltpu.make_async_copy(ins[a], outs[a].at[me], local_sems.at[a]) for a in range(na)]
        for cp in mine:
            cp.start()
        sends = [copy(a, k, me) for a in range(na) for k in range(N_DEV - 1)]
        for cp in sends:
            cp.start()
        for a in range(na):
            for k in range(N_DEV - 1):
                copy(a, k, _index(*peers[k])).wait_recv()
        for cp in sends:
            cp.wait_send()
        for cp in mine:
            cp.wait()

    return pl.pallas_call(
        body, name=name,
        in_specs=[HBM_SPEC] * na, out_specs=[HBM_SPEC] * na,
        out_shape=[_sds((N_DEV,) + a.shape, a.dtype) for a in shards],
        scratch_shapes=[pltpu.SemaphoreType.DMA((7 * na,)), pltpu.SemaphoreType.DMA((7 * na,)),
                        pltpu.SemaphoreType.DMA((na,))],
    )(*shards)


def _local_step(xs, tgt, attn_g, sinks, gq, gkv, final_g, w_in, w_q, w_kv, w_o):
    s = xs.shape[0]
    tabs = _rope_tables(s)
    saved = []
    x = xs
    for l in range(DEPTH):
        proj, h = _in_proj(x, attn_g[l:l + 1], w_in[l])
        swa = _swa_fwd(proj, sinks[l])
        qh, kh, vh, cqn, ckvn = _mla_prep(proj, gq[l:l + 1], gkv[l:l + 1], w_q[l], w_kv[l], tabs)
        mla, lse = _mla_fwd(qh, kh, vh)
        x_next, y = _out_proj(x, proj, swa, mla, w_o[l])
        saved.append((x, proj, h, swa, qh, kh, vh, cqn, ckvn, mla, lse, y))
        x = x_next

    dx, d_final, loss = _final_loss(x, tgt, final_g.reshape(1, D_MODEL))

    d_in, d_q, d_kv, d_o = [None] * DEPTH, [None] * DEPTH, [None] * DEPTH, [None] * DEPTH
    d_attn, d_gq, d_gkv, d_sink = [None] * DEPTH, [None] * DEPTH, [None] * DEPTH, [None] * DEPTH
    for l in reversed(range(DEPTH)):
        x, proj, h, swa, qh, kh, vh, cqn, ckvn, mla, lse, y = saved[l]
        d_o[l] = _matmul_tn(y, dx, "grad_w_out")
        do_a, do_b, dgate = _out_proj_bwd(dx, proj, swa, mla, w_o[l])
        dqa, dkva, dsk = _swa_bwd(proj, sinks[l], do_a, swa)
        dqh = _mla_bwd_dq(qh, kh, vh, do_b, mla, lse)
        dkh, dvh = _mla_bwd_dkv(qh, kh, vh, do_b, mla, lse)
        dlat, dqf, dkvf, d_gq[l], d_gkv[l] = _mla_prep_bwd(
            dqh, dkh, dvh, proj, gq[l:l + 1], gkv[l:l + 1], w_q[l], w_kv[l], tabs)
        d_q[l] = _matmul_tn(cqn, dqf, "grad_w_q_b")
        d_kv[l] = _matmul_tn(ckvn, dkvf, "grad_w_kv_b")
        dproj = jnp.concatenate([dqa, dgate, dlat, dkva], axis=1)
        d_in[l] = _matmul_tn(h, dproj, "grad_w_in")
        dx, d_attn[l] = _in_proj_bwd(dproj, w_in[l], x, dx, attn_g[l:l + 1])
        d_sink[l] = dsk[0:1, 0:SWA_HEADS]

    pack = jnp.concatenate([
        jnp.concatenate(d_attn, axis=0).reshape(64, 128),
        jnp.concatenate(d_gq, axis=0).reshape(12, 128),
        jnp.concatenate(d_gkv, axis=0).reshape(8, 128),
        d_final.reshape(16, 128),
        jnp.pad(jnp.concatenate(d_sink, axis=1), ((0, 0), (0, 64))),
        loss[0:1],
        jnp.zeros((PACK_ROWS - ROW_LOSS - 1, 128), F32)], axis=0)
    grads = tuple(jnp.stack(t, axis=0) for t in (d_in, d_q, d_kv, d_o))
    return dx, grads, pack


def _pack_small(attn, qa, kva, final, sinks):
    return jnp.concatenate([
        attn.reshape(64, 128), qa.reshape(12, 128), kva.reshape(8, 128), final.reshape(16, 128),
        jnp.pad(sinks.reshape(1, 64), ((0, 0), (0, 64))),
        jnp.zeros((PACK_ROWS - ROW_SINK - 1, 128), F32)], axis=0)


def _unpack_small(p):
    return (p[ROW_ATTN:ROW_QA].reshape(DEPTH, D_MODEL), p[ROW_SINK, 0:64].reshape(DEPTH, SWA_HEADS),
            p[ROW_QA:ROW_KVA].reshape(DEPTH, Q_RANK), p[ROW_KVA:ROW_FINAL].reshape(DEPTH, KV_RANK),
            p[ROW_FINAL:ROW_SINK].reshape(D_MODEL))


def kernel(x, attn_norm_g, w_in, swa_sinks, q_a_norm_g, kv_a_norm_g, w_q_b, w_kv_b, w_out, final_norm_g, loss_target, m_attn_norm_g, m_w_in, m_swa_sinks, m_q_a_norm_g, m_kv_a_norm_g, m_w_q_b, m_w_kv_b, m_w_out, m_final_norm_g, v_attn_norm_g, v_w_in, v_swa_sinks, v_q_a_norm_g, v_kv_a_norm_g, v_w_q_b, v_w_kv_b, v_w_out, v_final_norm_g):
    gathered = _all_gather([w_in.astype(MXU_DTYPE), w_q_b.astype(MXU_DTYPE),
                            w_kv_b.astype(MXU_DTYPE), w_out.astype(MXU_DTYPE)], "gather_weights")
    weights = _weights_from_gathered(*gathered)
    dx, grads, pack = _local_step(x[0], loss_target[0], attn_norm_g, swa_sinks, q_a_norm_g,
                                  kv_a_norm_g, final_norm_g, *weights)

    received = _all_to_all(list(_grads_to_blocks(*grads)), "scatter_grads")
    small = _all_gather_direct([pack], "gather_small")[0]

    big = []
    for parts, w, m, v, name in zip(received, (w_in, w_q_b, w_kv_b, w_out),
                                    (m_w_in, m_w_q_b, m_w_kv_b, m_w_out),
                                    (v_w_in, v_w_q_b, v_w_kv_b, v_w_out),
                                    ("adamw_w_in", "adamw_w_q_b", "adamw_w_kv_b", "adamw_w_out")):
        cols = w.shape[-1]
        flat = lambda t: t.reshape(-1, cols)
        outs = _reduce_adamw(parts.reshape(N_DEV, -1, cols), flat(w), flat(m), flat(v), name)
        big.append([t.reshape(w.shape) for t in outs])

    sm = _reduce_adamw(
        small,
        _pack_small(attn_norm_g, q_a_norm_g, kv_a_norm_g, final_norm_g, swa_sinks),
        _pack_small(m_attn_norm_g, m_q_a_norm_g, m_kv_a_norm_g, m_final_norm_g, m_swa_sinks),
        _pack_small(v_attn_norm_g, v_q_a_norm_g, v_kv_a_norm_g, v_final_norm_g, v_swa_sinks),
        "adamw_small")
    loss = sm[0][ROW_LOSS, 0]
    kinds = []
    for t in range(4):
        attn, sinks, qa, kva, final = _unpack_small(sm[t])
        b_in, b_q, b_kv, b_o = (big[i][t] for i in range(4))
        kinds.append((attn, b_in, sinks, qa, kva, b_q, b_kv, b_o, final))
    return (loss, dx[None], *kinds[0], *kinds[1], *kinds[2], *kinds[3])
```

```python
import functools

import jax
import jax.numpy as jnp
from jax import lax
from jax.experimental import pallas as pl
from jax.experimental.pallas import tpu as pltpu

F32 = jnp.float32
BF16 = jnp.bfloat16
MXU_DTYPE = BF16
GRAD_DTYPE = BF16

D_MODEL = 2048
DEPTH = 4
EPS = 1e-6
NEG = -1e30
BLOCK = 128
SWA_HEADS = 16
MLA_HEADS = 8
Q_RANK = 384
KV_RANK = 256
MLA_SCALE = 192 ** -0.5
MLA_C2 = MLA_SCALE * 1.4426950408889634
ROPE_THETA = 10000.0
IN_WIDTH = 4032

ADAM_LR = 0.001
ADAM_B1 = 0.9
ADAM_B2 = 0.999
ADAM_EPS = 1e-08
ADAM_WD = 0.01
ADAM_STEP = 10

N_DEV = 8
MESH = pl.DeviceIdType.MESH

NP = 4096
QA, GA, GB, CQ, CKV, KR, KVA = 0, 1024, 2048, 3072, 3456, 3712, 3840

ROW_ATTN, ROW_QA, ROW_KVA, ROW_FINAL, ROW_SINK, ROW_LOSS, PACK_ROWS = 0, 64, 76, 84, 100, 101, 104

VMEM_LIMIT = 56 * 1024 * 1024
MLA_TILE = 512


def _sds(shape, dtype):
    return jax.ShapeDtypeStruct(shape, dtype)


def _params(n_axes):
    return pltpu.CompilerParams(dimension_semantics=("arbitrary",) * n_axes,
                                vmem_limit_bytes=VMEM_LIMIT)


def _mm(a, b):
    return jnp.dot(a.astype(MXU_DTYPE), b.astype(MXU_DTYPE), preferred_element_type=F32)


def _mm_nt(a, b):
    return lax.dot_general(a.astype(MXU_DTYPE), b.astype(MXU_DTYPE),
                           (((1,), (1,)), ((), ())), preferred_element_type=F32)


def _mm_tn(a, b):
    return lax.dot_general(a.astype(MXU_DTYPE), b.astype(MXU_DTYPE),
                           (((0,), (0,)), ((), ())), preferred_element_type=F32)


def _rownorm(x):
    r = lax.rsqrt(jnp.mean(x * x, axis=-1, keepdims=True) + EPS)
    return x * r, r


def _rownorm_bwd(dxh, xh, r):
    return r * (dxh - xh * jnp.mean(dxh * xh, axis=-1, keepdims=True))


def _rope(t, c, s1, s2):
    return t * c + pltpu.roll(t, 32, 1) * s1 + pltpu.roll(t, 96, 1) * s2


def _pad_in_cols(w):
    z = jnp.zeros(w.shape[:-1] + (64,), w.dtype)
    return jnp.concatenate([w[..., 0:1024], w[..., 1280:2304], w[..., 3008:4032], w[..., 2304:2688],
                            w[..., 2688:2944], w[..., 2944:3008], z, w[..., 1024:1152],
                            w[..., 1152:1280]], axis=-1)


def _weights_from_gathered(g_in, g_qb, g_kvb, g_out):
    w_in = _pad_in_cols(g_in.transpose(1, 0, 2).reshape(D_MODEL, IN_WIDTH))
    qb = g_qb.transpose(1, 0, 2)
    rope = jnp.pad(qb[..., 128:], ((0, 0), (0, 0), (0, 64)))
    w_q = jnp.concatenate([qb[..., :128].reshape(Q_RANK, 1024),
                           rope.reshape(Q_RANK, 1024)], axis=-1)
    kvb = g_kvb.transpose(1, 0, 2)
    w_kv = jnp.concatenate([kvb[..., :128].reshape(KV_RANK, 1024),
                            kvb[..., 128:].reshape(KV_RANK, 1024)], axis=-1)
    w_o = g_out.reshape(D_MODEL, D_MODEL)
    return w_in, w_q, w_kv, w_o


def _grads_to_blocks(d_qa, d_gate, d_lat, d_kva, d_q, d_kv, d_o):
    d_in = jnp.concatenate([d_qa, d_kva, d_gate[:, :1024], d_lat[:, :704], d_gate[:, 1024:]], axis=1)
    b_in = d_in.reshape(D_MODEL, N_DEV, 504).transpose(1, 0, 2)
    qn = d_q[:, :1024].reshape(Q_RANK, 8, 128)
    qr = d_q[:, 1024:].reshape(Q_RANK, 8, 128)[..., :64]
    b_q = jnp.concatenate([qn, qr], axis=-1).transpose(1, 0, 2)
    kn = d_kv[:, :1024].reshape(KV_RANK, 8, 128)
    vv = d_kv[:, 1024:].reshape(KV_RANK, 8, 128)
    b_kv = jnp.concatenate([kn, vv], axis=-1).transpose(1, 0, 2)
    b_o = d_o.reshape(N_DEV, 256, D_MODEL)
    return [b_in, b_q, b_kv, b_o]


def _rope_tables(s):
    pos = jnp.arange(s, dtype=F32)
    inv_freq = ROPE_THETA ** (-jnp.arange(0, 64, 2, dtype=F32) / 64)
    ang = pos[:, None] * inv_freq[None, :]
    cos, sin = jnp.cos(ang), jnp.sin(ang)
    z32 = jnp.zeros((s, 32), F32)
    z64 = jnp.zeros((s, 64), F32)
    c = jnp.concatenate([cos, cos, z64], axis=1)
    s1 = jnp.concatenate([z32, sin, z64], axis=1)
    s2 = jnp.concatenate([-sin, z32, z64], axis=1)
    return c, s1, s2


def _in_proj(x, g, w):
    s = x.shape[0]
    tm, tn = min(512, s), 1024

    def body(x_ref, g_ref, w_ref, o_ref, h_ref):
        @pl.when(pl.program_id(1) == 0)
        def _():
            xh, _ = _rownorm(x_ref[...])
            h_ref[...] = (xh * g_ref[...]).astype(h_ref.dtype)
        o_ref[...] = jnp.dot(h_ref[...], w_ref[...], preferred_element_type=F32)

    return pl.pallas_call(
        body, name="in_proj", grid=(s // tm, NP // tn),
        in_specs=[pl.BlockSpec((tm, D_MODEL), lambda i, j: (i, 0)),
                  pl.BlockSpec((1, D_MODEL), lambda i, j: (0, 0)),
                  pl.BlockSpec((D_MODEL, tn), lambda i, j: (0, j))],
        out_specs=[pl.BlockSpec((tm, tn), lambda i, j: (i, j)),
                   pl.BlockSpec((tm, D_MODEL), lambda i, j: (i, 0))],
        out_shape=[_sds((s, NP), F32), _sds((s, D_MODEL), MXU_DTYPE)],
        compiler_params=_params(2),
    )(x, g, w)


def _swa_slopes():
    return [2.0 ** (-8.0 * (h + 1) / SWA_HEADS) for h in range(SWA_HEADS)]


def _swa_operands(kv_p, kv_c):
    kk = jnp.concatenate([kv_p[:, :128], kv_c[:, :128]], axis=0)
    vv = jnp.concatenate([kv_p[:, 128:], kv_c[:, 128:]], axis=0)
    left = lax.broadcasted_iota(jnp.int32, (2 * BLOCK, 128), 1) < 64

    def split(t):
        lo = jnp.where(left, t, 0.0)
        hi = jnp.where(left, 0.0, t)
        lefts = [lo, pltpu.roll(hi, 64, 1)]
        rights = [pltpu.roll(lo, 64, 1), hi]
        return lefts, rights

    return split(kk), split(vv), left


def _swa_mask(n):
    qi = lax.broadcasted_iota(jnp.int32, (BLOCK, 2 * BLOCK), 0)
    ki = lax.broadcasted_iota(jnp.int32, (BLOCK, 2 * BLOCK), 1)
    delta = BLOCK + qi - ki
    key_pos = (n - 1) * BLOCK + ki
    valid = (delta >= 0) & (delta < BLOCK) & (key_pos >= 0)
    return valid, delta.astype(F32)


def _swa_probs(qp, kmat, valid, deltaf, slope, sink):
    sc = _mm_nt(qp, kmat)
    sc = jnp.where(valid, sc - slope * deltaf, NEG)
    m = jnp.maximum(jnp.max(sc, axis=1, keepdims=True), sink)
    ex = jnp.exp(sc - m)
    es = jnp.exp(sink - m)
    l = jnp.sum(ex, axis=1, keepdims=True) + es
    return ex / l, es / l


def _swa_fwd(proj, sinks):
    s = proj.shape[0]
    nb = s // BLOCK
    slopes = _swa_slopes()

    def body(sink_ref, q_ref, kp_ref, kc_ref, o_ref):
        n = pl.program_id(0)
        (k_l, k_r), (v_l, v_r), _ = _swa_operands(kp_ref[...], kc_ref[...])
        valid, deltaf = _swa_mask(n)
        for pair in range(SWA_HEADS // 2):
            j = pair // 4
            qp = q_ref[:, 128 * pair:128 * (pair + 1)] * 0.125
            out = jnp.zeros((BLOCK, 128), F32)
            for e in range(2):
                h = 2 * pair + e
                kmat = (k_l, k_r)[e][j]
                vmat = (v_l, v_r)[e][j]
                p, _ = _swa_probs(qp, kmat, valid, deltaf, slopes[h], sink_ref[h])
                out = out + _mm(p, vmat)
            o_ref[:, 128 * pair:128 * (pair + 1)] = out

    return pl.pallas_call(
        body, name="swa_fwd", grid=(nb,),
        in_specs=[pl.BlockSpec(memory_space=pltpu.SMEM),
                  pl.BlockSpec((BLOCK, 1024), lambda n: (n, 0)),
                  pl.BlockSpec((BLOCK, 256), lambda n: (jnp.maximum(n - 1, 0), KVA // 256)),
                  pl.BlockSpec((BLOCK, 256), lambda n: (n, KVA // 256))],
        out_specs=pl.BlockSpec((BLOCK, 1024), lambda n: (n, 0)),
        out_shape=_sds((s, 1024), F32),
        compiler_params=_params(1),
    )(sinks, proj, proj, proj)


def _mla_prep(proj, gq, gkv, w_q, w_kv, tabs):
    s = proj.shape[0]
    tm = min(512, s)
    c, s1, s2 = tabs

    def body(p_ref, gq_ref, gkv_ref, wq_ref, wkv_ref, c_ref, s1_ref, s2_ref,
             q_ref, k_ref, v_ref, cqn_ref, ckvn_ref):
        cqh, _ = _rownorm(p_ref[:, 0:384])
        ckvh, _ = _rownorm(p_ref[:, 384:640])
        cqn = (cqh * gq_ref[...]).astype(MXU_DTYPE)
        ckvn = (ckvh * gkv_ref[...]).astype(MXU_DTYPE)
        cqn_ref[...] = cqn
        ckvn_ref[...] = ckvn
        q = _mm(cqn, wq_ref[...])
        kv = _mm(ckvn, wkv_ref[...])
        cc, ss1, ss2 = c_ref[...], s1_ref[...], s2_ref[...]
        krr = _rope(p_ref[:, 640:768], cc, ss1, ss2).astype(k_ref.dtype)
        for h in range(MLA_HEADS):
            q_ref[h, :, 0:128] = q[:, 128 * h:128 * (h + 1)].astype(q_ref.dtype)
            q_ref[h, :, 128:256] = _rope(q[:, 1024 + 128 * h:1024 + 128 * (h + 1)],
                                         cc, ss1, ss2).astype(q_ref.dtype)
            k_ref[h, :, 0:128] = kv[:, 128 * h:128 * (h + 1)].astype(k_ref.dtype)
            k_ref[h, :, 128:256] = krr
            v_ref[h] = kv[:, 1024 + 128 * h:1024 + 128 * (h + 1)].astype(v_ref.dtype)

    row = lambda i: (i, 0)
    fixed = lambda i: (0, 0)
    return pl.pallas_call(
        body, name="mla_prep", grid=(s // tm,),
        in_specs=[pl.BlockSpec((tm, 768), lambda i: (i, CQ // 768)),
                  pl.BlockSpec((1, Q_RANK), fixed), pl.BlockSpec((1, KV_RANK), fixed),
                  pl.BlockSpec((Q_RANK, 2048), fixed), pl.BlockSpec((KV_RANK, 2048), fixed),
                  pl.BlockSpec((tm, 128), row), pl.BlockSpec((tm, 128), row),
                  pl.BlockSpec((tm, 128), row)],
        out_specs=[pl.BlockSpec((MLA_HEADS, tm, 256), lambda i: (0, i, 0)),
                   pl.BlockSpec((MLA_HEADS, tm, 256), lambda i: (0, i, 0)),
                   pl.BlockSpec((MLA_HEADS, tm, 128), lambda i: (0, i, 0)),
                   pl.BlockSpec((tm, Q_RANK), row), pl.BlockSpec((tm, KV_RANK), row)],
        out_shape=[_sds((MLA_HEADS, s, 256), MXU_DTYPE), _sds((MLA_HEADS, s, 256), MXU_DTYPE),
                   _sds((MLA_HEADS, s, 128), MXU_DTYPE),
                   _sds((s, Q_RANK), MXU_DTYPE), _sds((s, KV_RANK), MXU_DTYPE)],
        compiler_params=_params(1),
    )(proj, gq, gkv, w_q, w_kv, c, s1, s2)


def _raw_scores(q, k, t, diagonal):
    sc = _mm_nt(q, k)
    if diagonal:
        row = lax.broadcasted_iota(jnp.int32, (t, t), 0)
        col = lax.broadcasted_iota(jnp.int32, (t, t), 1)
        sc = jnp.where(col <= row, sc, NEG)
    return sc


def _mla_fwd(qh, kh, vh, shards):
    s = qh.shape[1]
    t = min(MLA_TILE, s)
    nq = s // t

    na = len(shards)

    def body(*refs):
        q_ref, k_ref, v_ref = refs[:3]
        o_ref, lse_ref = refs[3 + na:5 + na]
        m_ref, l_ref, acc_ref = refs[5 + 2 * na:8 + 2 * na]
        h, i = pl.program_id(0), pl.program_id(1)
        if na:
            ex = _Exchange(refs[3:3 + na], refs[5 + na:5 + 2 * na], refs[8 + 2 * na:], gather=True)

            @pl.when((h == 0) & (i == 0))
            def _():
                ex.start()

        m_ref[...] = jnp.full(m_ref.shape, NEG, F32)
        l_ref[...] = jnp.zeros(l_ref.shape, F32)
        acc_ref[...] = jnp.zeros(acc_ref.shape, F32)

        def step(j, diagonal):
            keys = pl.ds(pl.multiple_of(j * t, t), t)
            sc = _raw_scores(q_ref[...], k_ref[keys, :], t, diagonal)
            m_prev = m_ref[...]
            m_new = jnp.maximum(m_prev, jnp.max(sc, axis=1, keepdims=True))
            alpha = jnp.exp2((m_prev - m_new) * MLA_C2)
            p = jnp.exp2((sc - m_new[:, :1]) * MLA_C2)
            l_ref[...] = alpha * l_ref[...] + jnp.sum(p, axis=1, keepdims=True)
            acc_ref[...] = alpha * acc_ref[...] + _mm(p, v_ref[keys, :])
            m_ref[...] = m_new

        def below_diagonal(j, carry):
            step(j, False)
            return carry

        lax.fori_loop(0, i, below_diagonal, 0)
        step(i, True)
        o_ref[...] = acc_ref[...] / l_ref[...]
        lse_ref[...] = m_ref[...] * MLA_C2 + jnp.log2(l_ref[...])

        if na:
            @pl.when((h == MLA_HEADS - 1) & (i == nq - 1))
            def _():
                ex.wait()

    head = lambda h, i: (h, 0, 0)
    outs = pl.pallas_call(
        body, name="mla_fwd_gather" if na else "mla_fwd", grid=(MLA_HEADS, nq),
        in_specs=[pl.BlockSpec((None, t, 256), lambda h, i: (h, i, 0)),
                  pl.BlockSpec((None, s, 256), head),
                  pl.BlockSpec((None, s, 128), head)] + [HBM_SPEC] * na,
        out_specs=[pl.BlockSpec((t, 128), lambda h, i: (i, h)),
                   pl.BlockSpec((None, t, 128), lambda h, i: (h, i, 0))] + [HBM_SPEC] * na,
        out_shape=[_sds((s, 1024), F32), _sds((MLA_HEADS, s, 128), F32)]
        + _exchange_shapes(shards, True),
        scratch_shapes=[pltpu.VMEM((t, 128), F32), pltpu.VMEM((t, 128), F32),
                        pltpu.VMEM((t, 128), F32)] + (_Exchange.semaphores(na) if na else []),
        compiler_params=_params(2),
    )(qh, kh, vh, *shards)
    return outs[0], outs[1], list(outs[2:])


def _silu_parts(g):
    sg = jax.nn.sigmoid(g)
    return g * sg, sg * (1.0 + g * (1.0 - sg))


def _out_proj(x, proj, swa, mla, w_out):
    s = x.shape[0]
    tm = min(256, s)

    def body(x_ref, ga_ref, gb_ref, a_ref, b_ref, w_ref, xo_ref, y_ref):
        sa, _ = _silu_parts(ga_ref[...])
        sb, _ = _silu_parts(gb_ref[...])
        y_ref[:, 0:1024] = (a_ref[...] * sa).astype(y_ref.dtype)
        y_ref[:, 1024:2048] = (b_ref[...] * sb).astype(y_ref.dtype)
        xo_ref[...] = x_ref[...] + jnp.dot(y_ref[...], w_ref[...], preferred_element_type=F32)

    row = lambda i: (i, 0)
    return pl.pallas_call(
        body, name="out_proj", grid=(s // tm,),
        in_specs=[pl.BlockSpec((tm, D_MODEL), row),
                  pl.BlockSpec((tm, 1024), lambda i: (i, GA // 1024)),
                  pl.BlockSpec((tm, 1024), lambda i: (i, GB // 1024)),
                  pl.BlockSpec((tm, 1024), row), pl.BlockSpec((tm, 1024), row),
                  pl.BlockSpec((D_MODEL, D_MODEL), lambda i: (0, 0))],
        out_specs=[pl.BlockSpec((tm, D_MODEL), row), pl.BlockSpec((tm, D_MODEL), row)],
        out_shape=[_sds((s, D_MODEL), F32), _sds((s, D_MODEL), MXU_DTYPE)],
        compiler_params=_params(1),
    )(x, proj, proj, swa, mla, w_out)


def _final_loss(x, tgt, g):
    s = x.shape[0]
    tm = min(512, s)

    def body(x_ref, t_ref, g_ref, dx_ref, dg_ref, loss_ref):
        @pl.when(pl.program_id(0) == 0)
        def _():
            dg_ref[...] = jnp.zeros(dg_ref.shape, F32)
            loss_ref[...] = jnp.zeros(loss_ref.shape, F32)
        xh, r = _rownorm(x_ref[...])
        gg = g_ref[...]
        err = xh * gg - t_ref[...]
        per_row = jnp.mean(err * err, axis=-1, keepdims=True)
        loss_ref[...] += 0.5 * jnp.sum(per_row, axis=0, keepdims=True)
        dy = err * (1.0 / D_MODEL)
        dg_ref[...] += jnp.sum(dy * xh, axis=0, keepdims=True)
        dx_ref[...] = _rownorm_bwd(dy * gg, xh, r)

    row = lambda i: (i, 0)
    fixed = lambda i: (0, 0)
    return pl.pallas_call(
        body, name="final_loss", grid=(s // tm,),
        in_specs=[pl.BlockSpec((tm, D_MODEL), row), pl.BlockSpec((tm, D_MODEL), row),
                  pl.BlockSpec((1, D_MODEL), fixed)],
        out_specs=[pl.BlockSpec((tm, D_MODEL), row), pl.BlockSpec((1, D_MODEL), fixed),
                   pl.BlockSpec((8, 128), fixed)],
        out_shape=[_sds((s, D_MODEL), F32), _sds((1, D_MODEL), F32), _sds((8, 128), F32)],
        compiler_params=_params(1),
    )(x, tgt, g)


def _out_proj_bwd(dx, proj, swa, mla, w_out):
    s = dx.shape[0]
    tm = min(256, s)

    def body(dx_ref, ga_ref, gb_ref, a_ref, b_ref, w_ref, doa_ref, dob_ref, dg_ref, dlt_ref):
        dy = _mm_nt(dx_ref[...], w_ref[...])
        sa, dsa = _silu_parts(ga_ref[...])
        sb, dsb = _silu_parts(gb_ref[...])
        dya, dyb = dy[:, 0:1024], dy[:, 1024:2048]
        doa_ref[...] = dya * sa
        dob = dyb * sb
        dob_ref[...] = dob.astype(dob_ref.dtype)
        dg_ref[:, 0:1024] = (dya * a_ref[...] * dsa).astype(dg_ref.dtype)
        dg_ref[:, 1024:2048] = (dyb * b_ref[...] * dsb).astype(dg_ref.dtype)
        prod = dob * b_ref[...]
        for h in range(MLA_HEADS):
            dlt = jnp.sum(prod[:, 128 * h:128 * (h + 1)], axis=1, keepdims=True)
            dlt_ref[h] = jnp.broadcast_to(dlt, (tm, 128))

    row = lambda i: (i, 0)
    return pl.pallas_call(
        body, name="out_proj_bwd", grid=(s // tm,),
        in_specs=[pl.BlockSpec((tm, D_MODEL), row),
                  pl.BlockSpec((tm, 1024), lambda i: (i, GA // 1024)),
                  pl.BlockSpec((tm, 1024), lambda i: (i, GB // 1024)),
                  pl.BlockSpec((tm, 1024), row), pl.BlockSpec((tm, 1024), row),
                  pl.BlockSpec((D_MODEL, D_MODEL), lambda i: (0, 0))],
        out_specs=[pl.BlockSpec((tm, 1024), row), pl.BlockSpec((tm, 1024), row),
                   pl.BlockSpec((tm, D_MODEL), row),
                   pl.BlockSpec((MLA_HEADS, tm, 128), lambda i: (0, i, 0))],
        out_shape=[_sds((s, 1024), F32), _sds((s, 1024), MXU_DTYPE), _sds((s, D_MODEL), MXU_DTYPE),
                   _sds((MLA_HEADS, s, 128), F32)],
        compiler_params=_params(1),
    )(dx, proj, proj, swa, mla, w_out)


def _matmul_tn(a, b, name):
    s, m = a.shape
    n = b.shape[1]
    tm, tn, tk = min(512, m), min(1024, n), min(512, s)
    nk = s // tk

    def body(a_ref, b_ref, o_ref, acc_ref):
        k = pl.program_id(2)

        @pl.when(k == 0)
        def _():
            acc_ref[...] = jnp.zeros(acc_ref.shape, F32)
        acc_ref[...] += _mm_tn(a_ref[...], b_ref[...])

        @pl.when(k == nk - 1)
        def _():
            o_ref[...] = acc_ref[...].astype(o_ref.dtype)

    return pl.pallas_call(
        body, name=name, grid=(m // tm, n // tn, nk),
        in_specs=[pl.BlockSpec((tk, tm), lambda i, j, k: (k, i)),
                  pl.BlockSpec((tk, tn), lambda i, j, k: (k, j))],
        out_specs=pl.BlockSpec((tm, tn), lambda i, j, k: (i, j)),
        out_shape=_sds((m, n), GRAD_DTYPE),
        scratch_shapes=[pltpu.VMEM((tm, tn), F32)],
        compiler_params=_params(3),
    )(a, b)


def _swa_bwd(proj, sinks, do, o):
    s = proj.shape[0]
    nb = s // BLOCK
    slopes = _swa_slopes()

    def body(sink_ref, q_ref, kp_ref, kc_ref, do_ref, o_ref, dq_ref, dkv_ref, dsink_ref, carry_ref):
        n = pl.program_id(0)

        @pl.when(n == 0)
        def _():
            carry_ref[...] = jnp.zeros(carry_ref.shape, F32)
            dsink_ref[...] = jnp.zeros(dsink_ref.shape, F32)

        @pl.when(n < nb)
        def _():
            (k_l, k_r), (v_l, v_r), left = _swa_operands(kp_ref[...], kc_ref[...])
            valid, deltaf = _swa_mask(n)
            left_q = lax.broadcasted_iota(jnp.int32, (BLOCK, 128), 1) < 64
            lane_s = lax.broadcasted_iota(jnp.int32, (8, 128), 1)
            zeros = jnp.zeros((2 * BLOCK, 128), F32)
            dk_l, dk_r, dv_l, dv_r = [zeros, zeros], [zeros, zeros], [zeros, zeros], [zeros, zeros]
            dsink = jnp.zeros((8, 128), F32)
            for pair in range(SWA_HEADS // 2):
                j = pair // 4
                cols = slice(128 * pair, 128 * (pair + 1))
                qp = q_ref[:, cols] * 0.125
                dop = do_ref[:, cols]
                prod = dop * o_ref[:, cols]
                dlt = [jnp.sum(jnp.where(left_q, prod, 0.0), axis=1, keepdims=True),
                       jnp.sum(jnp.where(left_q, 0.0, prod), axis=1, keepdims=True)]
                dq = jnp.zeros((BLOCK, 128), F32)
                for e in range(2):
                    h = 2 * pair + e
                    kmat = (k_l, k_r)[e][j]
                    vmat = (v_l, v_r)[e][j]
                    p, ps = _swa_probs(qp, kmat, valid, deltaf, slopes[h], sink_ref[h])
                    ds = p * (_mm_nt(dop, vmat) - dlt[e])
                    dsh = -jnp.sum(ps * dlt[e], axis=0, keepdims=True)
                    dsink = dsink + jnp.where(lane_s == h, dsh, 0.0)
                    dq = dq + _mm(ds, kmat)
                    dv_full = _mm_tn(p, dop)
                    dk_full = _mm_tn(ds, qp)
                    if e == 0:
                        dv_l[j] = dv_l[j] + dv_full
                        dk_l[j] = dk_l[j] + dk_full
                    else:
                        dv_r[j] = dv_r[j] + dv_full
                        dk_r[j] = dk_r[j] + dk_full
                dq_ref[:, cols] = (dq * 0.125).astype(dq_ref.dtype)

            def merge(t_l, t_r):
                head0 = t_l[0] + pltpu.roll(t_r[0], 64, 1)
                head1 = t_r[1] + pltpu.roll(t_l[1], 64, 1)
                return jnp.where(left, head0, head1)

            contrib = jnp.concatenate([merge(dk_l, dk_r), merge(dv_l, dv_r)], axis=1)
            dkv_ref[...] = (carry_ref[...] + contrib[0:BLOCK]).astype(dkv_ref.dtype)
            carry_ref[...] = contrib[BLOCK:2 * BLOCK]
            dsink_ref[...] += dsink

        @pl.when(n == nb)
        def _():
            dkv_ref[...] = carry_ref[...].astype(dkv_ref.dtype)

    cur = lambda n: (jnp.minimum(n, nb - 1), 0)
    return pl.pallas_call(
        body, name="swa_bwd", grid=(nb + 1,),
        in_specs=[pl.BlockSpec(memory_space=pltpu.SMEM),
                  pl.BlockSpec((BLOCK, 1024), cur),
                  pl.BlockSpec((BLOCK, 256), lambda n: (jnp.clip(n - 1, 0, nb - 1), KVA // 256)),
                  pl.BlockSpec((BLOCK, 256), lambda n: (jnp.minimum(n, nb - 1), KVA // 256)),
                  pl.BlockSpec((BLOCK, 1024), cur), pl.BlockSpec((BLOCK, 1024), cur)],
        out_specs=[pl.BlockSpec((BLOCK, 1024), cur),
                   pl.BlockSpec((BLOCK, 256), lambda n: (jnp.maximum(n - 1, 0), 0)),
                   pl.BlockSpec((8, 128), lambda n: (0, 0))],
        out_shape=[_sds((s, 1024), MXU_DTYPE), _sds((s, 256), MXU_DTYPE), _sds((8, 128), F32)],
        scratch_shapes=[pltpu.VMEM((BLOCK, 256), F32)],
        compiler_params=_params(1),
    )(sinks, proj, proj, proj, do, o)


def _mla_bwd(qh, kh, vh, do, dlt, lse, blocks):
    s = qh.shape[1]
    t = min(MLA_TILE, s)
    nq = s // t

    na = len(blocks)

    def body(*refs):
        q_ref, k_ref, v_ref, do_ref, dlt_ref, lse_ref = refs[:6]
        dq_ref, dk_ref, dv_ref = refs[6 + na:9 + na]
        h, j = pl.program_id(0), pl.program_id(1)
        if na:
            ex = _Exchange(refs[6:6 + na], refs[9 + na:9 + 2 * na], refs[9 + 2 * na:], gather=False)

            @pl.when((h == 0) & (j == 0))
            def _():
                ex.start()

        def step(i, diagonal):
            rows = pl.ds(pl.multiple_of(i * t, t), t)
            q, k, dout = q_ref[rows, :], k_ref[...], do_ref[rows, :]
            sc = _raw_scores(q, k, t, diagonal)
            p = jnp.exp2(sc * MLA_C2 - lse_ref[rows, 0:1])
            dv = _mm_tn(p, dout)
            ds = p * (_mm_nt(dout, v_ref[...]) - dlt_ref[rows, 0:1])
            dk = _mm_tn(ds, q)
            dq = _mm(ds, k)
            return rows, dq, dk, dv

        rows, dq, dk, dv = step(j, True)
        dk_ref[...] = dk
        dv_ref[...] = dv

        @pl.when(j == 0)
        def _():
            dq_ref[rows, :] = dq * MLA_SCALE

        @pl.when(j > 0)
        def _():
            dq_ref[rows, :] = (dq_ref[rows, :] + dq) * MLA_SCALE

        def above_diagonal(i, carry):
            rows, dq, dk, dv = step(i, False)
            dk_ref[...] += dk
            dv_ref[...] += dv

            @pl.when(j == 0)
            def _():
                dq_ref[rows, :] = dq

            @pl.when(j > 0)
            def _():
                dq_ref[rows, :] += dq
            return carry

        lax.fori_loop(j + 1, nq, above_diagonal, 0)
        dk_ref[...] *= MLA_SCALE

        if na:
            @pl.when((h == MLA_HEADS - 1) & (j == nq - 1))
            def _():
                ex.wait()

    head = lambda h, j: (h, 0, 0)
    kv_map = lambda h, j: (h, j, 0)
    outs = pl.pallas_call(
        body, name="mla_bwd_scatter" if na else "mla_bwd", grid=(MLA_HEADS, nq),
        in_specs=[pl.BlockSpec((None, s, 256), head), pl.BlockSpec((None, t, 256), kv_map),
                  pl.BlockSpec((None, t, 128), kv_map),
                  pl.BlockSpec((s, 128), lambda h, j: (0, h)),
                  pl.BlockSpec((None, s, 128), head), pl.BlockSpec((None, s, 128), head)]
        + [HBM_SPEC] * na,
        out_specs=[pl.BlockSpec((None, s, 256), head),
                   pl.BlockSpec((None, t, 256), kv_map), pl.BlockSpec((None, t, 128), kv_map)]
        + [HBM_SPEC] * na,
        out_shape=[_sds((MLA_HEADS, s, 256), F32), _sds((MLA_HEADS, s, 256), F32),
                   _sds((MLA_HEADS, s, 128), F32)] + _exchange_shapes(blocks, False),
        scratch_shapes=_Exchange.semaphores(na) if na else [],
        compiler_params=_params(2),
    )(qh, kh, vh, do, dlt, lse, *blocks)
    return outs[0], outs[1], outs[2], list(outs[3:])


def _mla_prep_bwd(dqh, dkh, dvh, proj, gq, gkv, w_q, w_kv, tabs):
    s = proj.shape[0]
    tm = min(256, s)
    c, s1, s2 = tabs

    def body(dq_ref, dk_ref, dv_ref, p_ref, gq_ref, gkv_ref, wq_ref, wkv_ref,
             c_ref, s1_ref, s2_ref, dp_ref, dqf_ref, dkvf_ref, dgq_ref, dgkv_ref):
        @pl.when(pl.program_id(0) == 0)
        def _():
            dgq_ref[...] = jnp.zeros(dgq_ref.shape, F32)
            dgkv_ref[...] = jnp.zeros(dgkv_ref.shape, F32)
        cc, ns1, ns2 = c_ref[...], -s1_ref[...], -s2_ref[...]
        dkr = jnp.zeros((tm, 128), F32)
        for h in range(MLA_HEADS):
            dqf_ref[:, 128 * h:128 * (h + 1)] = dq_ref[h, :, 0:128].astype(dqf_ref.dtype)
            dqf_ref[:, 1024 + 128 * h:1024 + 128 * (h + 1)] = _rope(
                dq_ref[h, :, 128:256], cc, ns1, ns2).astype(dqf_ref.dtype)
            dkvf_ref[:, 128 * h:128 * (h + 1)] = dk_ref[h, :, 0:128].astype(dkvf_ref.dtype)
            dkvf_ref[:, 1024 + 128 * h:1024 + 128 * (h + 1)] = dv_ref[h].astype(dkvf_ref.dtype)
            dkr = dkr + dk_ref[h, :, 128:256]
        dcqn = _mm_nt(dqf_ref[...], wq_ref[...])
        dckvn = _mm_nt(dkvf_ref[...], wkv_ref[...])
        cqh, rq = _rownorm(p_ref[:, 0:384])
        ckvh, rkv = _rownorm(p_ref[:, 384:640])
        dgq_ref[...] += jnp.sum(dcqn * cqh, axis=0, keepdims=True)
        dgkv_ref[...] += jnp.sum(dckvn * ckvh, axis=0, keepdims=True)
        dp_ref[:, 0:384] = _rownorm_bwd(dcqn * gq_ref[...], cqh, rq).astype(dp_ref.dtype)
        dp_ref[:, 384:640] = _rownorm_bwd(dckvn * gkv_ref[...], ckvh, rkv).astype(dp_ref.dtype)
        dp_ref[:, 640:768] = _rope(dkr, cc, ns1, ns2).astype(dp_ref.dtype)

    row = lambda i: (i, 0)
    fixed = lambda i: (0, 0)
    head = lambda i: (0, i, 0)
    return pl.pallas_call(
        body, name="mla_prep_bwd", grid=(s // tm,),
        in_specs=[pl.BlockSpec((MLA_HEADS, tm, 256), head), pl.BlockSpec((MLA_HEADS, tm, 256), head),
                  pl.BlockSpec((MLA_HEADS, tm, 128), head),
                  pl.BlockSpec((tm, 768), lambda i: (i, CQ // 768)),
                  pl.BlockSpec((1, Q_RANK), fixed), pl.BlockSpec((1, KV_RANK), fixed),
                  pl.BlockSpec((Q_RANK, 2048), fixed), pl.BlockSpec((KV_RANK, 2048), fixed),
                  pl.BlockSpec((tm, 128), row), pl.BlockSpec((tm, 128), row),
                  pl.BlockSpec((tm, 128), row)],
        out_specs=[pl.BlockSpec((tm, 768), row), pl.BlockSpec((tm, 2048), row),
                   pl.BlockSpec((tm, 2048), row),
                   pl.BlockSpec((1, Q_RANK), fixed), pl.BlockSpec((1, KV_RANK), fixed)],
        out_shape=[_sds((s, 768), MXU_DTYPE), _sds((s, 2048), MXU_DTYPE), _sds((s, 2048), MXU_DTYPE),
                   _sds((1, Q_RANK), F32), _sds((1, KV_RANK), F32)],
        compiler_params=_params(1),
    )(dqh, dkh, dvh, proj, gq, gkv, w_q, w_kv, c, s1, s2)


def _in_proj_bwd(dqa, dgate, dlat, dkva, w, x, dx_out, g):
    s = x.shape[0]
    tm, tk = min(512, s), 1024
    nk = NP // tk

    def body(dqa_ref, dg8_ref, dlat_ref, dkva_ref, w_ref, x_ref, dxo_ref, g_ref,
             dx_ref, dg_ref, acc_ref):
        i, k = pl.program_id(0), pl.program_id(1)

        @pl.when((i == 0) & (k == 0))
        def _():
            dg_ref[...] = jnp.zeros(dg_ref.shape, F32)

        @pl.when(k == 0)
        def _():
            acc_ref[...] = _mm_nt(dqa_ref[...], w_ref[...])

        @pl.when((k == 1) | (k == 2))
        def _():
            acc_ref[...] += _mm_nt(dg8_ref[...], w_ref[...])

        @pl.when(k == nk - 1)
        def _():
            dh = (acc_ref[...] + _mm_nt(dlat_ref[...], w_ref[:, 0:768])
                  + _mm_nt(dkva_ref[...], w_ref[:, 768:1024]))
            xh, r = _rownorm(x_ref[...])
            dg_ref[...] += jnp.sum(dh * xh, axis=0, keepdims=True)
            dx_ref[...] = dxo_ref[...] + _rownorm_bwd(dh * g_ref[...], xh, r)

    row = lambda i, k: (i, 0)
    fixed = lambda i, k: (0, 0)
    return pl.pallas_call(
        body, name="in_proj_bwd", grid=(s // tm, nk),
        in_specs=[pl.BlockSpec((tm, 1024), row),
                  pl.BlockSpec((tm, 1024), lambda i, k: (i, jnp.clip(k - 1, 0, 1))),
                  pl.BlockSpec((tm, 768), row), pl.BlockSpec((tm, 256), row),
                  pl.BlockSpec((D_MODEL, tk), lambda i, k: (0, k)),
                  pl.BlockSpec((tm, D_MODEL), row), pl.BlockSpec((tm, D_MODEL), row),
                  pl.BlockSpec((1, D_MODEL), fixed)],
        out_specs=[pl.BlockSpec((tm, D_MODEL), row), pl.BlockSpec((1, D_MODEL), fixed)],
        out_shape=[_sds((s, D_MODEL), F32), _sds((1, D_MODEL), F32)],
        scratch_shapes=[pltpu.VMEM((tm, D_MODEL), F32)],
        compiler_params=_params(2),
    )(dqa, dgate, dlat, dkva, w, x, dx_out, g)


def _reduce_adamw(parts, w, m, v, name):
    n_layers = len(parts)
    rows, cols = parts[0].shape[1:]
    lanes = -(-cols // 128) * 128
    tr = rows
    for cand in (1024, 512, 256, 128, 64, 32, 16, 8):
        if rows % cand == 0 and N_DEV * cand * lanes * 4 <= 8 * 1024 * 1024:
            tr = cand
            break
    nr = rows // tr

    def body(*refs):
        p_refs = refs[:n_layers]
        w_ref, m_ref, v_ref, g_ref, d_ref, nm_ref, nv_ref = refs[n_layers:]
        layer = pl.program_id(0)
        for l in range(n_layers):
            @pl.when(layer == l)
            def _(l=l):
                g = p_refs[l][0].astype(F32)
                for k in range(1, N_DEV):
                    g = g + p_refs[l][k].astype(F32)
                update(g, w_ref, m_ref, v_ref, g_ref, d_ref, nm_ref, nv_ref)

    def update(g, w_ref, m_ref, v_ref, g_ref, d_ref, nm_ref, nv_ref):
        m2 = ADAM_B1 * m_ref[...] + (1.0 - ADAM_B1) * g
        v2 = ADAM_B2 * v_ref[...] + (1.0 - ADAM_B2) * (g * g)
        m_hat = m2 / (1.0 - ADAM_B1 ** ADAM_STEP)
        v_hat = v2 / (1.0 - ADAM_B2 ** ADAM_STEP)
        g_ref[...] = g
        d_ref[...] = -ADAM_LR * (m_hat / (jnp.sqrt(v_hat) + ADAM_EPS) + ADAM_WD * w_ref[...])
        nm_ref[...] = m2
        nv_ref[...] = v2

    def part_spec(l):
        return pl.BlockSpec((N_DEV, tr, cols), lambda layer, i: (0, jnp.where(layer == l, i, 0), 0))

    blk = pl.BlockSpec((tr, cols), lambda layer, i: (layer * nr + i, 0))
    return pl.pallas_call(
        body, name=name, grid=(n_layers, nr),
        in_specs=[part_spec(l) for l in range(n_layers)] + [blk, blk, blk],
        out_specs=[blk, blk, blk, blk],
        out_shape=[_sds((n_layers * rows, cols), F32)] * 4,
        compiler_params=_params(2),
    )(*parts, w, m, v)


def _position():
    x, y, c = lax.axis_index("x"), lax.axis_index("y"), lax.axis_index("c")
    return x, y, c


def _index(px, py, pc):
    return 4 * px + 2 * py + pc


HBM_SPEC = pl.BlockSpec(memory_space=pltpu.HBM)


def _all_gather(shards, name):
    na = len(shards)

    def body(*refs):
        ins, outs = refs[:na], refs[na:2 * na]
        send_sems, recv_sems, local_sems = refs[2 * na:]
        x, y, c = _position()
        me, sibling = (x, y, c), (x, y, 1 - c)
        chips = [(1 - x, y), (x, 1 - y), (1 - x, 1 - y)]

        def copy(a, k, block, to, src=None):
            rows = outs[a].at[_index(*block)]
            return pltpu.make_async_remote_copy(
                src_ref=rows if src is None else src, dst_ref=rows,
                send_sem=send_sems.at[7 * a + k], recv_sem=recv_sems.at[7 * a + k],
                device_id=to, device_id_type=MESH)

        mine = [pltpu.make_async_copy(ins[a], outs[a].at[_index(*me)], local_sems.at[a])
                for a in range(na)]
        for cp in mine:
            cp.start()
        first = []
        for a in range(na):
            first.append(copy(a, 0, me, sibling, src=ins[a]))
            for j, chip in enumerate(chips):
                first.append(copy(a, 1 + j, me, (*chip, c), src=ins[a]))
        for cp in first:
            cp.start()
        passed = []
        for j, chip in enumerate(chips):
            for a in range(na):
                copy(a, 1 + j, (*chip, c), me).wait_recv()
                cp = copy(a, 4 + j, (*chip, c), sibling)
                cp.start()
                passed.append(cp)
        for a in range(na):
            copy(a, 0, sibling, me).wait_recv()
        for j, chip in enumerate(chips):
            for a in range(na):
                copy(a, 4 + j, (*chip, 1 - c), me).wait_recv()
        for cp in first + passed:
            cp.wait_send()
        for cp in mine:
            cp.wait()

    return pl.pallas_call(
        body, name=name,
        in_specs=[HBM_SPEC] * na, out_specs=[HBM_SPEC] * na,
        out_shape=[_sds((N_DEV,) + a.shape, a.dtype) for a in shards],
        scratch_shapes=[pltpu.SemaphoreType.DMA((7 * na,)), pltpu.SemaphoreType.DMA((7 * na,)),
                        pltpu.SemaphoreType.DMA((na,))],
    )(*shards)


class _Exchange:
    def __init__(self, srcs, dsts, sems, gather):
        self.srcs, self.dsts, self.gather = srcs, dsts, gather
        self.send_sems, self.recv_sems, self.local_sems = sems
        x, y, c = _position()
        self.me = _index(x, y, c)
        self.peers = [(x ^ ((k >> 2) & 1), y ^ ((k >> 1) & 1), c ^ (k & 1)) for k in range(1, N_DEV)]

    @staticmethod
    def semaphores(na):
        return [pltpu.SemaphoreType.DMA((7 * na,)), pltpu.SemaphoreType.DMA((7 * na,)),
                pltpu.SemaphoreType.DMA((na,))]

    def _src(self, a, slot):
        return self.srcs[a] if self.gather else self.srcs[a].at[slot]

    def _local(self, a):
        return pltpu.make_async_copy(self._src(a, self.me), self.dsts[a].at[self.me],
                                     self.local_sems.at[a])

    def _send(self, a, k):
        peer = self.peers[k]
        return pltpu.make_async_remote_copy(
            src_ref=self._src(a, _index(*peer)), dst_ref=self.dsts[a].at[self.me],
            send_sem=self.send_sems.at[7 * a + k], recv_sem=self.recv_sems.at[7 * a + k],
            device_id=peer, device_id_type=MESH)

    def _arrival(self, a, k):
        landed = self.dsts[a].at[_index(*self.peers[k])]
        return pltpu.make_async_remote_copy(
            src_ref=landed, dst_ref=landed,
            send_sem=self.send_sems.at[7 * a + k], recv_sem=self.recv_sems.at[7 * a + k],
            device_id=self.peers[k], device_id_type=MESH)

    def start(self):
        for a in range(len(self.srcs)):
            self._local(a).start()
            for k in range(N_DEV - 1):
                self._send(a, k).start()

    def wait(self):
        for a in range(len(self.srcs)):
            for k in range(N_DEV - 1):
                self._arrival(a, k).wait_recv()
            for k in range(N_DEV - 1):
                self._send(a, k).wait_send()
            self._local(a).wait()


def _exchange_shapes(arrays, gather):
    return [_sds(((N_DEV,) + a.shape) if gather else a.shape, a.dtype) for a in arrays]


def _exchange_call(arrays, gather, name):
    na = len(arrays)

    def body(*refs):
        ex = _Exchange(refs[:na], refs[na:2 * na], refs[2 * na:], gather)
        ex.start()
        ex.wait()

    return pl.pallas_call(
        body, name=name,
        in_specs=[HBM_SPEC] * na, out_specs=[HBM_SPEC] * na,
        out_shape=_exchange_shapes(arrays, gather),
        scratch_shapes=_Exchange.semaphores(na),
    )(*arrays)


def _layer_fwd(x, small, weights, tabs, next_shards):
    attn_g, sinks, gq, gkv = small
    w_in, w_q, w_kv, w_o = weights
    proj, h = _in_proj(x, attn_g, w_in)
    swa = _swa_fwd(proj, sinks)
    qh, kh, vh, cqn, ckvn = _mla_prep(proj, gq, gkv, w_q, w_kv, tabs)
    mla, lse, gathered = _mla_fwd(qh, kh, vh, next_shards)
    x_next, y = _out_proj(x, proj, swa, mla, w_o)
    return x_next, (x, proj, h, swa, qh, kh, vh, cqn, ckvn, mla, lse, y), gathered


def _layer_bwd(dx, saved, small, weights, tabs, pending):
    attn_g, sinks, gq, gkv = small
    w_in, w_q, w_kv, w_o = weights
    x, proj, h, swa, qh, kh, vh, cqn, ckvn, mla, lse, y = saved
    d_o = _matmul_tn(y, dx, "grad_w_out")
    do_a, do_b, dgate, dlt = _out_proj_bwd(dx, proj, swa, mla, w_o)
    dqa, dkva, dsk = _swa_bwd(proj, sinks, do_a, swa)
    dqh, dkh, dvh, received = _mla_bwd(qh, kh, vh, do_b, dlt, lse, pending)
    dlat, dqf, dkvf, d_gq, d_gkv = _mla_prep_bwd(dqh, dkh, dvh, proj, gq, gkv, w_q, w_kv, tabs)
    d_q = _matmul_tn(cqn, dqf, "grad_w_q_b")
    d_kv = _matmul_tn(ckvn, dkvf, "grad_w_kv_b")
    d_qa = _matmul_tn(h, dqa, "grad_w_in_qa")
    d_gate = _matmul_tn(h, dgate, "grad_w_in_gate")
    d_lat = _matmul_tn(h, dlat, "grad_w_in_lat")
    d_kva = _matmul_tn(h, dkva, "grad_w_in_kva")
    dx, d_attn = _in_proj_bwd(dqa, dgate, dlat, dkva, w_in, x, dx, attn_g)
    blocks = _grads_to_blocks(d_qa, d_gate, d_lat, d_kva, d_q, d_kv, d_o)
    return dx, (d_attn, dsk[0:1, 0:SWA_HEADS], d_gq, d_gkv), blocks, received


def _pack_small_grads(small_grads, d_final, loss):
    d_attn, d_sink, d_gq, d_gkv = zip(*small_grads)
    return jnp.concatenate([
        jnp.concatenate(d_attn, axis=0).reshape(64, 128),
        jnp.concatenate(d_gq, axis=0).reshape(12, 128),
        jnp.concatenate(d_gkv, axis=0).reshape(8, 128),
        d_final.reshape(16, 128),
        jnp.pad(jnp.concatenate(d_sink, axis=1), ((0, 0), (0, 64))),
        loss[0:1],
        jnp.zeros((PACK_ROWS - ROW_LOSS - 1, 128), F32)], axis=0)


def _pack_small(attn, qa, kva, final, sinks):
    return jnp.concatenate([
        attn.reshape(64, 128), qa.reshape(12, 128), kva.reshape(8, 128), final.reshape(16, 128),
        jnp.pad(sinks.reshape(1, 64), ((0, 0), (0, 64))),
        jnp.zeros((PACK_ROWS - ROW_SINK - 1, 128), F32)], axis=0)


def _unpack_small(p):
    return (p[ROW_ATTN:ROW_QA].reshape(DEPTH, D_MODEL), p[ROW_SINK, 0:64].reshape(DEPTH, SWA_HEADS),
            p[ROW_QA:ROW_KVA].reshape(DEPTH, Q_RANK), p[ROW_KVA:ROW_FINAL].reshape(DEPTH, KV_RANK),
            p[ROW_FINAL:ROW_SINK].reshape(D_MODEL))


def kernel(x, attn_norm_g, w_in, swa_sinks, q_a_norm_g, kv_a_norm_g, w_q_b, w_kv_b, w_out, final_norm_g, loss_target, m_attn_norm_g, m_w_in, m_swa_sinks, m_q_a_norm_g, m_kv_a_norm_g, m_w_q_b, m_w_kv_b, m_w_out, m_final_norm_g, v_attn_norm_g, v_w_in, v_swa_sinks, v_q_a_norm_g, v_kv_a_norm_g, v_w_q_b, v_w_kv_b, v_w_out, v_final_norm_g):
    xs, tgt = x[0], loss_target[0]
    tabs = _rope_tables(xs.shape[0])
    shards = [w.astype(MXU_DTYPE) for w in (w_in, w_q_b, w_kv_b, w_out)]
    layer_shards = lambda l: [w[l] for w in shards]
    smalls = [(attn_norm_g[l:l + 1], swa_sinks[l], q_a_norm_g[l:l + 1], kv_a_norm_g[l:l + 1])
              for l in range(DEPTH)]

    gathered = _all_gather(layer_shards(0), "gather_weights")
    weights, saved = [], []
    for l in range(DEPTH):
        weights.append(_weights_from_gathered(*gathered))
        next_shards = layer_shards(l + 1) if l + 1 < DEPTH else []
        xs, acts, gathered = _layer_fwd(xs, smalls[l], weights[l], tabs, next_shards)
        saved.append(acts)
    dx, d_final, loss = _final_loss(xs, tgt, final_norm_g.reshape(1, D_MODEL))

    received, small_grads, pending = [None] * DEPTH, [None] * DEPTH, []
    for l in reversed(range(DEPTH)):
        dx, small_grads[l], blocks, arrived = _layer_bwd(dx, saved[l], smalls[l], weights[l], tabs,
                                                         pending)
        if pending:
            received[l + 1] = arrived
        pending = blocks
    received[0] = _exchange_call(pending, False, "scatter_grads")
    small = _exchange_call([_pack_small_grads(small_grads, d_final, loss)], True, "gather_small")[0]

    big = []
    for a, (w, m, v, name) in enumerate(zip((w_in, w_q_b, w_kv_b, w_out),
                                            (m_w_in, m_w_q_b, m_w_kv_b, m_w_out),
                                            (v_w_in, v_w_q_b, v_w_kv_b, v_w_out),
                                            ("adamw_w_in", "adamw_w_q_b", "adamw_w_kv_b",
                                             "adamw_w_out"))):
        cols = w.shape[-1]
        flat = lambda t: t.reshape(-1, cols)
        outs = _reduce_adamw([received[l][a] for l in range(DEPTH)], flat(w), flat(m), flat(v), name)
        big.append([t.reshape(w.shape) for t in outs])

    sm = _reduce_adamw(
        [small],
        _pack_small(attn_norm_g, q_a_norm_g, kv_a_norm_g, final_norm_g, swa_sinks),
        _pack_small(m_attn_norm_g, m_q_a_norm_g, m_kv_a_norm_g, m_final_norm_g, m_swa_sinks),
        _pack_small(v_attn_norm_g, v_q_a_norm_g, v_kv_a_norm_g, v_final_norm_g, v_swa_sinks),
        "adamw_small")
    loss = sm[0][ROW_LOSS, 0]
    kinds = []
    for t in range(4):
        attn, sinks, qa, kva, final = _unpack_small(sm[t])
        b_in, b_q, b_kv, b_o = (big[i][t] for i in range(4))
        kinds.append((attn, b_in, sinks, qa, kva, b_q, b_kv, b_o, final))
    return (loss, dx[None], *kinds[0], *kinds[1], *kinds[2], *kinds[3])
```

```python
import functools

import jax
import jax.numpy as jnp
from jax import lax
from jax.experimental import pallas as pl
from jax.experimental.pallas import tpu as pltpu

F32 = jnp.float32
BF16 = jnp.bfloat16
MXU_DTYPE = BF16
GRAD_DTYPE = BF16

D_MODEL = 2048
DEPTH = 4
EPS = 1e-6
NEG = -1e30
BLOCK = 128
SWA_HEADS = 16
MLA_HEADS = 8
Q_RANK = 384
KV_RANK = 256
MLA_SCALE = 192 ** -0.5
MLA_C2 = MLA_SCALE * 1.4426950408889634
ROPE_THETA = 10000.0
IN_WIDTH = 4032

ADAM_LR = 0.001
ADAM_B1 = 0.9
ADAM_B2 = 0.999
ADAM_EPS = 1e-08
ADAM_WD = 0.01
ADAM_STEP = 10

N_DEV = 8
MESH = pl.DeviceIdType.MESH

NP = 4096
QA, GA, GB, CQ, CKV, KR, KVA = 0, 1024, 2048, 3072, 3456, 3712, 3840

ROW_ATTN, ROW_QA, ROW_KVA, ROW_FINAL, ROW_SINK, ROW_LOSS, PACK_ROWS = 0, 64, 76, 84, 100, 101, 104

VMEM_LIMIT = 56 * 1024 * 1024
MLA_TILE = 512


def _sds(shape, dtype):
    return jax.ShapeDtypeStruct(shape, dtype)


def _params(n_axes):
    return pltpu.CompilerParams(dimension_semantics=("arbitrary",) * n_axes,
                                vmem_limit_bytes=VMEM_LIMIT)


def _mm(a, b):
    return jnp.dot(a.astype(MXU_DTYPE), b.astype(MXU_DTYPE), preferred_element_type=F32)


def _mm_nt(a, b):
    return lax.dot_general(a.astype(MXU_DTYPE), b.astype(MXU_DTYPE),
                           (((1,), (1,)), ((), ())), preferred_element_type=F32)


def _mm_tn(a, b):
    return lax.dot_general(a.astype(MXU_DTYPE), b.astype(MXU_DTYPE),
                           (((0,), (0,)), ((), ())), preferred_element_type=F32)


def _rownorm(x):
    r = lax.rsqrt(jnp.mean(x * x, axis=-1, keepdims=True) + EPS)
    return x * r, r


def _rownorm_bwd(dxh, xh, r):
    return r * (dxh - xh * jnp.mean(dxh * xh, axis=-1, keepdims=True))


def _rope(t, c, s1, s2):
    return t * c + pltpu.roll(t, 32, 1) * s1 + pltpu.roll(t, 96, 1) * s2


def _pad_in_cols(w):
    z = jnp.zeros(w.shape[:-1] + (64,), w.dtype)
    return jnp.concatenate([w[..., 0:1024], w[..., 1280:2304], w[..., 3008:4032], w[..., 2304:2688],
                            w[..., 2688:2944], w[..., 2944:3008], z, w[..., 1024:1152],
                            w[..., 1152:1280]], axis=-1)


def _weights_from_gathered(g_in, g_qb, g_kvb, g_out):
    w_in = _pad_in_cols(g_in.transpose(1, 0, 2).reshape(D_MODEL, IN_WIDTH))
    qb = g_qb.transpose(1, 0, 2)
    rope = jnp.pad(qb[..., 128:], ((0, 0), (0, 0), (0, 64)))
    w_q = jnp.concatenate([qb[..., :128].reshape(Q_RANK, 1024),
                           rope.reshape(Q_RANK, 1024)], axis=-1)
    kvb = g_kvb.transpose(1, 0, 2)
    w_kv = jnp.concatenate([kvb[..., :128].reshape(KV_RANK, 1024),
                            kvb[..., 128:].reshape(KV_RANK, 1024)], axis=-1)
    w_o = g_out.reshape(D_MODEL, D_MODEL)
    return w_in, w_q, w_kv, w_o


def _grads_to_blocks(d_qa, d_gate, d_lat, d_kva, d_q, d_kv, d_o):
    d_in = jnp.concatenate([d_qa, d_kva, d_gate[:, :1024], d_lat[:, :704], d_gate[:, 1024:]], axis=1)
    b_in = d_in.reshape(D_MODEL, N_DEV, 504).transpose(1, 0, 2)
    qn = d_q[:, :1024].reshape(Q_RANK, 8, 128)
    qr = d_q[:, 1024:].reshape(Q_RANK, 8, 128)[..., :64]
    b_q = jnp.concatenate([qn, qr], axis=-1).transpose(1, 0, 2)
    kn = d_kv[:, :1024].reshape(KV_RANK, 8, 128)
    vv = d_kv[:, 1024:].reshape(KV_RANK, 8, 128)
    b_kv = jnp.concatenate([kn, vv], axis=-1).transpose(1, 0, 2)
    b_o = d_o.reshape(N_DEV, 256, D_MODEL)
    return [b_in, b_q, b_kv, b_o]


def _rope_tables(s):
    pos = jnp.arange(s, dtype=F32)
    inv_freq = ROPE_THETA ** (-jnp.arange(0, 64, 2, dtype=F32) / 64)
    ang = pos[:, None] * inv_freq[None, :]
    cos, sin = jnp.cos(ang), jnp.sin(ang)
    z32 = jnp.zeros((s, 32), F32)
    z64 = jnp.zeros((s, 64), F32)
    c = jnp.concatenate([cos, cos, z64], axis=1)
    s1 = jnp.concatenate([z32, sin, z64], axis=1)
    s2 = jnp.concatenate([-sin, z32, z64], axis=1)
    return c, s1, s2


def _in_proj(x, g, w):
    s = x.shape[0]
    tm, tn = min(512, s), 1024

    def body(x_ref, g_ref, w_ref, o_ref, h_ref):
        xh, _ = _rownorm(x_ref[...])
        h_ref[...] = (xh * g_ref[...]).astype(h_ref.dtype)
        for j in range(NP // tn):
            cols = slice(j * tn, (j + 1) * tn)
            o_ref[:, cols] = jnp.dot(h_ref[...], w_ref[:, cols], preferred_element_type=F32)

    row = lambda i: (i, 0)
    fixed = lambda i: (0, 0)
    return pl.pallas_call(
        body, name="in_proj", grid=(s // tm,),
        in_specs=[pl.BlockSpec((tm, D_MODEL), row), pl.BlockSpec((1, D_MODEL), fixed),
                  pl.BlockSpec((D_MODEL, NP), fixed, pipeline_mode=pl.Buffered(1))],
        out_specs=[pl.BlockSpec((tm, NP), row), pl.BlockSpec((tm, D_MODEL), row)],
        out_shape=[_sds((s, NP), F32), _sds((s, D_MODEL), MXU_DTYPE)],
        compiler_params=_params(1),
    )(x, g, w)


def _swa_slopes():
    return [2.0 ** (-8.0 * (h + 1) / SWA_HEADS) for h in range(SWA_HEADS)]


def _swa_operands(kv_p, kv_c):
    kk = jnp.concatenate([kv_p[:, :128], kv_c[:, :128]], axis=0)
    vv = jnp.concatenate([kv_p[:, 128:], kv_c[:, 128:]], axis=0)
    left = lax.broadcasted_iota(jnp.int32, (2 * BLOCK, 128), 1) < 64

    def split(t):
        lo = jnp.where(left, t, 0.0)
        hi = jnp.where(left, 0.0, t)
        lefts = [lo, pltpu.roll(hi, 64, 1)]
        rights = [pltpu.roll(lo, 64, 1), hi]
        return lefts, rights

    return split(kk), split(vv), left


SWA_STACK = 4 * BLOCK


def _swa_head(j, a, e):
    return 2 * (4 * j + a) + e


def _swa_bias(n, j, e):
    slopes = _swa_slopes()
    r = lax.broadcasted_iota(jnp.int32, (SWA_STACK, 2 * BLOCK), 0)
    ki = lax.broadcasted_iota(jnp.int32, (SWA_STACK, 2 * BLOCK), 1)
    a = r >> 7
    delta = BLOCK + (r & (BLOCK - 1)) - ki
    valid = (delta >= 0) & (delta < BLOCK) & ((n - 1) * BLOCK + ki >= 0)
    sl = [slopes[_swa_head(j, t, e)] for t in range(4)]
    slope = jnp.where(a == 0, sl[0], jnp.where(a == 1, sl[1], jnp.where(a == 2, sl[2], sl[3])))
    return jnp.where(valid, -slope * delta.astype(F32), NEG)


def _swa_fill_bias(n, bias_ref):
    @pl.when(n <= 1)
    def _():
        for j in range(2):
            for e in range(2):
                bias_ref[2 * j + e] = _swa_bias(n, j, e)


def _swa_sink_col(sink_ref, j, e):
    a = lax.broadcasted_iota(jnp.int32, (SWA_STACK, 1), 0) >> 7
    sk = [sink_ref[_swa_head(j, t, e)] for t in range(4)]
    return jnp.where(a == 0, sk[0], jnp.where(a == 1, sk[1], jnp.where(a == 2, sk[2], sk[3])))


def _swa_stack(ref, j):
    return jnp.concatenate([ref[:, 128 * (4 * j + a):128 * (4 * j + a + 1)] for a in range(4)], axis=0)


def _swa_softmax(qs, kmat, bias, sink):
    sc = _mm_nt(qs, kmat) + bias
    m = jnp.maximum(jnp.max(sc, axis=1, keepdims=True), sink)
    ex = jnp.exp(sc - m)
    es = jnp.exp(sink - m)
    return ex, es, 1.0 / (jnp.sum(ex, axis=1, keepdims=True) + es)


def _swa_fwd(proj, sinks):
    s = proj.shape[0]
    nb = s // BLOCK

    def body(sink_ref, q_ref, kp_ref, kc_ref, o_ref, bias_ref):
        n = pl.program_id(0)
        _swa_fill_bias(n, bias_ref)
        (k_l, k_r), (v_l, v_r), _ = _swa_operands(kp_ref[...], kc_ref[...])
        for j in range(2):
            qs = _swa_stack(q_ref, j) * 0.125
            out = None
            for e in range(2):
                ex, _, inv = _swa_softmax(qs, (k_l, k_r)[e][j], bias_ref[2 * j + e],
                                          _swa_sink_col(sink_ref, j, e))
                o = _mm(ex, (v_l, v_r)[e][j]) * inv
                out = o if out is None else out + o
            for a in range(4):
                o_ref[:, 128 * (4 * j + a):128 * (4 * j + a + 1)] = out[128 * a:128 * (a + 1)]

    return pl.pallas_call(
        body, name="swa_fwd", grid=(nb,),
        in_specs=[pl.BlockSpec(memory_space=pltpu.SMEM),
                  pl.BlockSpec((BLOCK, 1024), lambda n: (n, 0)),
                  pl.BlockSpec((BLOCK, 256), lambda n: (jnp.maximum(n - 1, 0), KVA // 256)),
                  pl.BlockSpec((BLOCK, 256), lambda n: (n, KVA // 256))],
        out_specs=pl.BlockSpec((BLOCK, 1024), lambda n: (n, 0)),
        out_shape=_sds((s, 1024), F32),
        scratch_shapes=[pltpu.VMEM((4, SWA_STACK, 2 * BLOCK), F32)],
        compiler_params=_params(1),
    )(sinks, proj, proj, proj)


def _mla_prep(proj, gq, gkv, w_q, w_kv, tabs):
    s = proj.shape[0]
    tm = min(512, s)
    c, s1, s2 = tabs

    def body(p_ref, gq_ref, gkv_ref, wq_ref, wkv_ref, c_ref, s1_ref, s2_ref,
             q_ref, k_ref, v_ref, cqn_ref, ckvn_ref):
        cqh, _ = _rownorm(p_ref[:, 0:384])
        ckvh, _ = _rownorm(p_ref[:, 384:640])
        cqn = (cqh * gq_ref[...]).astype(MXU_DTYPE)
        ckvn = (ckvh * gkv_ref[...]).astype(MXU_DTYPE)
        cqn_ref[...] = cqn
        ckvn_ref[...] = ckvn
        q = _mm(cqn, wq_ref[...])
        kv = _mm(ckvn, wkv_ref[...])
        cc, ss1, ss2 = c_ref[...], s1_ref[...], s2_ref[...]
        krr = _rope(p_ref[:, 640:768], cc, ss1, ss2).astype(k_ref.dtype)
        for h in range(MLA_HEADS):
            q_ref[h, :, 0:128] = q[:, 128 * h:128 * (h + 1)].astype(q_ref.dtype)
            q_ref[h, :, 128:256] = _rope(q[:, 1024 + 128 * h:1024 + 128 * (h + 1)],
                                         cc, ss1, ss2).astype(q_ref.dtype)
            k_ref[h, :, 0:128] = kv[:, 128 * h:128 * (h + 1)].astype(k_ref.dtype)
            k_ref[h, :, 128:256] = krr
            v_ref[h] = kv[:, 1024 + 128 * h:1024 + 128 * (h + 1)].astype(v_ref.dtype)

    row = lambda i: (i, 0)
    fixed = lambda i: (0, 0)
    return pl.pallas_call(
        body, name="mla_prep", grid=(s // tm,),
        in_specs=[pl.BlockSpec((tm, 768), lambda i: (i, CQ // 768)),
                  pl.BlockSpec((1, Q_RANK), fixed), pl.BlockSpec((1, KV_RANK), fixed),
                  pl.BlockSpec((Q_RANK, 2048), fixed), pl.BlockSpec((KV_RANK, 2048), fixed),
                  pl.BlockSpec((tm, 128), row), pl.BlockSpec((tm, 128), row),
                  pl.BlockSpec((tm, 128), row)],
        out_specs=[pl.BlockSpec((MLA_HEADS, tm, 256), lambda i: (0, i, 0)),
                   pl.BlockSpec((MLA_HEADS, tm, 256), lambda i: (0, i, 0)),
                   pl.BlockSpec((MLA_HEADS, tm, 128), lambda i: (0, i, 0)),
                   pl.BlockSpec((tm, Q_RANK), row), pl.BlockSpec((tm, KV_RANK), row)],
        out_shape=[_sds((MLA_HEADS, s, 256), MXU_DTYPE), _sds((MLA_HEADS, s, 256), MXU_DTYPE),
                   _sds((MLA_HEADS, s, 128), MXU_DTYPE),
                   _sds((s, Q_RANK), MXU_DTYPE), _sds((s, KV_RANK), MXU_DTYPE)],
        compiler_params=_params(1),
    )(proj, gq, gkv, w_q, w_kv, c, s1, s2)


def _raw_scores(q, k, t, diagonal):
    sc = _mm_nt(q, k)
    if diagonal:
        row = lax.broadcasted_iota(jnp.int32, (t, t), 0)
        col = lax.broadcasted_iota(jnp.int32, (t, t), 1)
        sc = jnp.where(col <= row, sc, NEG)
    return sc


def _mla_fwd(qh, kh, vh, shards):
    s = qh.shape[1]
    t = min(MLA_TILE, s)
    nq = s // t

    na = len(shards)

    def body(*refs):
        q_ref, k_ref, v_ref = refs[:3]
        o_ref, lse_ref = refs[3 + na:5 + na]
        m_ref, l_ref, acc_ref = refs[5 + 2 * na:8 + 2 * na]
        h, i = pl.program_id(0), pl.program_id(1)
        if na:
            ex = _Exchange(refs[3:3 + na], refs[5 + na:5 + 2 * na], refs[8 + 2 * na:], gather=True)

            @pl.when((h == 0) & (i == 0))
            def _():
                ex.start()

        m_ref[...] = jnp.full(m_ref.shape, NEG, F32)
        l_ref[...] = jnp.zeros(l_ref.shape, F32)
        acc_ref[...] = jnp.zeros(acc_ref.shape, F32)

        def step(j, diagonal):
            keys = pl.ds(pl.multiple_of(j * t, t), t)
            sc = _raw_scores(q_ref[...], k_ref[keys, :], t, diagonal)
            m_prev = m_ref[...]
            m_new = jnp.maximum(m_prev, jnp.max(sc, axis=1, keepdims=True))
            alpha = jnp.exp2((m_prev - m_new) * MLA_C2)
            p = jnp.exp2((sc - m_new[:, :1]) * MLA_C2)
            l_ref[...] = alpha * l_ref[...] + jnp.sum(p, axis=1, keepdims=True)
            acc_ref[...] = alpha * acc_ref[...] + _mm(p, v_ref[keys, :])
            m_ref[...] = m_new

        def below_diagonal(j, carry):
            step(j, False)
            return carry

        lax.fori_loop(0, i, below_diagonal, 0)
        step(i, True)
        o_ref[...] = acc_ref[...] / l_ref[...]
        lse_ref[...] = m_ref[...] * MLA_C2 + jnp.log2(l_ref[...])

        if na:
            @pl.when((h == MLA_HEADS - 1) & (i == nq - 1))
            def _():
                ex.wait()

    head = lambda h, i: (h, 0, 0)
    outs = pl.pallas_call(
        body, name="mla_fwd_gather" if na else "mla_fwd", grid=(MLA_HEADS, nq),
        in_specs=[pl.BlockSpec((None, t, 256), lambda h, i: (h, i, 0)),
                  pl.BlockSpec((None, s, 256), head),
                  pl.BlockSpec((None, s, 128), head)] + [HBM_SPEC] * na,
        out_specs=[pl.BlockSpec((t, 128), lambda h, i: (i, h)),
                   pl.BlockSpec((None, t, 128), lambda h, i: (h, i, 0))] + [HBM_SPEC] * na,
        out_shape=[_sds((s, 1024), F32), _sds((MLA_HEADS, s, 128), F32)]
        + _exchange_shapes(shards, True),
        scratch_shapes=[pltpu.VMEM((t, 128), F32), pltpu.VMEM((t, 128), F32),
                        pltpu.VMEM((t, 128), F32)] + (_Exchange.semaphores(na) if na else []),
        compiler_params=_params(2),
    )(qh, kh, vh, *shards)
    return outs[0], outs[1], list(outs[2:])


def _silu_parts(g):
    sg = jax.nn.sigmoid(g)
    return g * sg, sg * (1.0 + g * (1.0 - sg))


def _out_proj(x, proj, swa, mla, w_out):
    s = x.shape[0]
    tm = min(512, s)

    def body(x_ref, ga_ref, gb_ref, a_ref, b_ref, w_ref, xo_ref, y_ref):
        sa, _ = _silu_parts(ga_ref[...])
        sb, _ = _silu_parts(gb_ref[...])
        y_ref[:, 0:1024] = (a_ref[...] * sa).astype(y_ref.dtype)
        y_ref[:, 1024:2048] = (b_ref[...] * sb).astype(y_ref.dtype)
        xo_ref[...] = x_ref[...] + jnp.dot(y_ref[...], w_ref[...], preferred_element_type=F32)

    row = lambda i: (i, 0)
    return pl.pallas_call(
        body, name="out_proj", grid=(s // tm,),
        in_specs=[pl.BlockSpec((tm, D_MODEL), row),
                  pl.BlockSpec((tm, 1024), lambda i: (i, GA // 1024)),
                  pl.BlockSpec((tm, 1024), lambda i: (i, GB // 1024)),
                  pl.BlockSpec((tm, 1024), row), pl.BlockSpec((tm, 1024), row),
                  pl.BlockSpec((D_MODEL, D_MODEL), lambda i: (0, 0), pipeline_mode=pl.Buffered(1))],
        out_specs=[pl.BlockSpec((tm, D_MODEL), row), pl.BlockSpec((tm, D_MODEL), row)],
        out_shape=[_sds((s, D_MODEL), F32), _sds((s, D_MODEL), MXU_DTYPE)],
        compiler_params=_params(1),
    )(x, proj, proj, swa, mla, w_out)


def _final_loss(x, tgt, g):
    s = x.shape[0]
    tm = min(512, s)

    def body(x_ref, t_ref, g_ref, dx_ref, dg_ref, loss_ref):
        @pl.when(pl.program_id(0) == 0)
        def _():
            dg_ref[...] = jnp.zeros(dg_ref.shape, F32)
            loss_ref[...] = jnp.zeros(loss_ref.shape, F32)
        xh, r = _rownorm(x_ref[...])
        gg = g_ref[...]
        err = xh * gg - t_ref[...]
        per_row = jnp.mean(err * err, axis=-1, keepdims=True)
        loss_ref[...] += 0.5 * jnp.sum(per_row, axis=0, keepdims=True)
        dy = err * (1.0 / D_MODEL)
        dg_ref[...] += jnp.sum(dy * xh, axis=0, keepdims=True)
        dx_ref[...] = _rownorm_bwd(dy * gg, xh, r)

    row = lambda i: (i, 0)
    fixed = lambda i: (0, 0)
    return pl.pallas_call(
        body, name="final_loss", grid=(s // tm,),
        in_specs=[pl.BlockSpec((tm, D_MODEL), row), pl.BlockSpec((tm, D_MODEL), row),
                  pl.BlockSpec((1, D_MODEL), fixed)],
        out_specs=[pl.BlockSpec((tm, D_MODEL), row), pl.BlockSpec((1, D_MODEL), fixed),
                   pl.BlockSpec((8, 128), fixed)],
        out_shape=[_sds((s, D_MODEL), F32), _sds((1, D_MODEL), F32), _sds((8, 128), F32)],
        compiler_params=_params(1),
    )(x, tgt, g)


def _out_proj_bwd(dx, proj, swa, mla, w_out):
    s = dx.shape[0]
    tm = min(512, s)

    def body(dx_ref, ga_ref, gb_ref, a_ref, b_ref, w_ref, doa_ref, dob_ref, dg_ref, dlt_ref):
        dx = dx_ref[...].astype(MXU_DTYPE)
        dya = _mm_nt(dx, w_ref[0:1024, :])
        sa, dsa = _silu_parts(ga_ref[...])
        doa_ref[...] = dya * sa
        dg_ref[:, 0:1024] = (dya * a_ref[...] * dsa).astype(dg_ref.dtype)
        dyb = _mm_nt(dx, w_ref[1024:2048, :])
        sb, dsb = _silu_parts(gb_ref[...])
        b = b_ref[...]
        dob = dyb * sb
        dob_ref[...] = dob.astype(dob_ref.dtype)
        dg_ref[:, 1024:2048] = (dyb * b * dsb).astype(dg_ref.dtype)
        prod = dob * b
        for h in range(MLA_HEADS):
            dlt = jnp.sum(prod[:, 128 * h:128 * (h + 1)], axis=1, keepdims=True)
            dlt_ref[h] = jnp.broadcast_to(dlt, (tm, 128))

    row = lambda i: (i, 0)
    return pl.pallas_call(
        body, name="out_proj_bwd", grid=(s // tm,),
        in_specs=[pl.BlockSpec((tm, D_MODEL), row),
                  pl.BlockSpec((tm, 1024), lambda i: (i, GA // 1024)),
                  pl.BlockSpec((tm, 1024), lambda i: (i, GB // 1024)),
                  pl.BlockSpec((tm, 1024), row), pl.BlockSpec((tm, 1024), row),
                  pl.BlockSpec((D_MODEL, D_MODEL), lambda i: (0, 0), pipeline_mode=pl.Buffered(1))],
        out_specs=[pl.BlockSpec((tm, 1024), row), pl.BlockSpec((tm, 1024), row),
                   pl.BlockSpec((tm, D_MODEL), row),
                   pl.BlockSpec((MLA_HEADS, tm, 128), lambda i: (0, i, 0))],
        out_shape=[_sds((s, 1024), F32), _sds((s, 1024), MXU_DTYPE), _sds((s, D_MODEL), MXU_DTYPE),
                   _sds((MLA_HEADS, s, 128), F32)],
        compiler_params=_params(1),
    )(dx, proj, proj, swa, mla, w_out)


def _matmul_tn(a, b, name):
    s, m = a.shape
    n = b.shape[1]
    tm, tn, tk = min(512, m), min(1024, n), min(2048, s)
    nk = s // tk

    def body(a_ref, b_ref, o_ref, acc_ref):
        k = pl.program_id(2)

        @pl.when(k == 0)
        def _():
            acc_ref[...] = jnp.zeros(acc_ref.shape, F32)
        acc_ref[...] += _mm_tn(a_ref[...], b_ref[...])

        @pl.when(k == nk - 1)
        def _():
            o_ref[...] = acc_ref[...].astype(o_ref.dtype)

    return pl.pallas_call(
        body, name=name, grid=(m // tm, n // tn, nk),
        in_specs=[pl.BlockSpec((tk, tm), lambda i, j, k: (k, i)),
                  pl.BlockSpec((tk, tn), lambda i, j, k: (k, j))],
        out_specs=pl.BlockSpec((tm, tn), lambda i, j, k: (i, j)),
        out_shape=_sds((m, n), GRAD_DTYPE),
        scratch_shapes=[pltpu.VMEM((tm, tn), F32)],
        compiler_params=_params(3),
    )(a, b)


def _swa_bwd(proj, sinks, do, o):
    s = proj.shape[0]
    nb = s // BLOCK

    def body(sink_ref, q_ref, kp_ref, kc_ref, do_ref, o_ref, dq_ref, dkv_ref, dsink_ref,
             carry_ref, bias_ref):
        n = pl.program_id(0)
        _swa_fill_bias(n, bias_ref)

        @pl.when(n == 0)
        def _():
            carry_ref[...] = jnp.zeros(carry_ref.shape, F32)
            dsink_ref[...] = jnp.zeros(dsink_ref.shape, F32)

        @pl.when(n < nb)
        def _():
            (k_l, k_r), (v_l, v_r), left = _swa_operands(kp_ref[...], kc_ref[...])
            left_q = lax.broadcasted_iota(jnp.int32, (SWA_STACK, 128), 1) < 64
            lane_s = lax.broadcasted_iota(jnp.int32, (8, 128), 1)
            dk_l, dk_r, dv_l, dv_r = [None, None], [None, None], [None, None], [None, None]
            dsink = jnp.zeros((8, 128), F32)
            for j in range(2):
                qs = _swa_stack(q_ref, j) * 0.125
                dos = _swa_stack(do_ref, j)
                prod = dos * _swa_stack(o_ref, j)
                dlt = [jnp.sum(jnp.where(left_q, prod, 0.0), axis=1, keepdims=True),
                       jnp.sum(jnp.where(left_q, 0.0, prod), axis=1, keepdims=True)]
                dq = None
                for e in range(2):
                    kmat = (k_l, k_r)[e][j]
                    vmat = (v_l, v_r)[e][j]
                    ex, es, inv = _swa_softmax(qs, kmat, bias_ref[2 * j + e],
                                               _swa_sink_col(sink_ref, j, e))
                    p = ex * inv
                    ds = p * (_mm_nt(dos, vmat) - dlt[e])
                    sink_term = es * inv * dlt[e]
                    for a in range(4):
                        dsh = -jnp.sum(sink_term[128 * a:128 * (a + 1)], axis=0, keepdims=True)
                        dsink = dsink + jnp.where(lane_s == _swa_head(j, a, e), dsh, 0.0)
                    dq_e = _mm(ds, kmat)
                    dq = dq_e if dq is None else dq + dq_e
                    (dv_l, dv_r)[e][j] = _mm_tn(p, dos)
                    (dk_l, dk_r)[e][j] = _mm_tn(ds, qs)
                for a in range(4):
                    cols = slice(128 * (4 * j + a), 128 * (4 * j + a + 1))
                    dq_ref[:, cols] = (dq[128 * a:128 * (a + 1)] * 0.125).astype(dq_ref.dtype)

            def merge(t_l, t_r):
                head0 = t_l[0] + pltpu.roll(t_r[0], 64, 1)
                head1 = t_r[1] + pltpu.roll(t_l[1], 64, 1)
                return jnp.where(left, head0, head1)

            contrib = jnp.concatenate([merge(dk_l, dk_r), merge(dv_l, dv_r)], axis=1)
            dkv_ref[...] = (carry_ref[...] + contrib[0:BLOCK]).astype(dkv_ref.dtype)
            carry_ref[...] = contrib[BLOCK:2 * BLOCK]
            dsink_ref[...] += dsink

        @pl.when(n == nb)
        def _():
            dkv_ref[...] = carry_ref[...].astype(dkv_ref.dtype)

    cur = lambda n: (jnp.minimum(n, nb - 1), 0)
    return pl.pallas_call(
        body, name="swa_bwd", grid=(nb + 1,),
        in_specs=[pl.BlockSpec(memory_space=pltpu.SMEM),
                  pl.BlockSpec((BLOCK, 1024), cur),
                  pl.BlockSpec((BLOCK, 256), lambda n: (jnp.clip(n - 1, 0, nb - 1), KVA // 256)),
                  pl.BlockSpec((BLOCK, 256), lambda n: (jnp.minimum(n, nb - 1), KVA // 256)),
                  pl.BlockSpec((BLOCK, 1024), cur), pl.BlockSpec((BLOCK, 1024), cur)],
        out_specs=[pl.BlockSpec((BLOCK, 1024), cur),
                   pl.BlockSpec((BLOCK, 256), lambda n: (jnp.maximum(n - 1, 0), 0)),
                   pl.BlockSpec((8, 128), lambda n: (0, 0))],
        out_shape=[_sds((s, 1024), MXU_DTYPE), _sds((s, 256), MXU_DTYPE), _sds((8, 128), F32)],
        scratch_shapes=[pltpu.VMEM((BLOCK, 256), F32), pltpu.VMEM((4, SWA_STACK, 2 * BLOCK), F32)],
        compiler_params=_params(1),
    )(sinks, proj, proj, proj, do, o)


def _mla_bwd(qh, kh, vh, do, dlt, lse, blocks):
    s = qh.shape[1]
    t = min(MLA_TILE, s)
    nq = s // t

    na = len(blocks)

    def body(*refs):
        q_ref, k_ref, v_ref, do_ref, dlt_ref, lse_ref = refs[:6]
        dq_ref, dk_ref, dv_ref = refs[6 + na:9 + na]
        h, j = pl.program_id(0), pl.program_id(1)
        if na:
            ex = _Exchange(refs[6:6 + na], refs[9 + na:9 + 2 * na], refs[9 + 2 * na:], gather=False)

            @pl.when((h == 0) & (j == 0))
            def _():
                ex.start()

        def step(i, diagonal):
            rows = pl.ds(pl.multiple_of(i * t, t), t)
            q, k, dout = q_ref[rows, :], k_ref[...], do_ref[rows, :]
            sc = _raw_scores(q, k, t, diagonal)
            p = jnp.exp2(sc * MLA_C2 - lse_ref[rows, 0:1])
            dv = _mm_tn(p, dout)
            ds = p * (_mm_nt(dout, v_ref[...]) - dlt_ref[rows, 0:1])
            dk = _mm_tn(ds, q)
            dq = _mm(ds, k)
            return rows, dq, dk, dv

        rows, dq, dk, dv = step(j, True)
        dk_ref[...] = dk
        dv_ref[...] = dv

        @pl.when(j == 0)
        def _():
            dq_ref[rows, :] = dq * MLA_SCALE

        @pl.when(j > 0)
        def _():
            dq_ref[rows, :] = (dq_ref[rows, :] + dq) * MLA_SCALE

        def above_diagonal(i, carry):
            rows, dq, dk, dv = step(i, False)
            dk_ref[...] += dk
            dv_ref[...] += dv

            @pl.when(j == 0)
            def _():
                dq_ref[rows, :] = dq

            @pl.when(j > 0)
            def _():
                dq_ref[rows, :] += dq
            return carry

        lax.fori_loop(j + 1, nq, above_diagonal, 0)
        dk_ref[...] *= MLA_SCALE

        if na:
            @pl.when((h == MLA_HEADS - 1) & (j == nq - 1))
            def _():
                ex.wait()

    head = lambda h, j: (h, 0, 0)
    kv_map = lambda h, j: (h, j, 0)
    outs = pl.pallas_call(
        body, name="mla_bwd_scatter" if na else "mla_bwd", grid=(MLA_HEADS, nq),
        in_specs=[pl.BlockSpec((None, s, 256), head), pl.BlockSpec((None, t, 256), kv_map),
                  pl.BlockSpec((None, t, 128), kv_map),
                  pl.BlockSpec((s, 128), lambda h, j: (0, h)),
                  pl.BlockSpec((None, s, 128), head), pl.BlockSpec((None, s, 128), head)]
        + [HBM_SPEC] * na,
        out_specs=[pl.BlockSpec((None, s, 256), head),
                   pl.BlockSpec((None, t, 256), kv_map), pl.BlockSpec((None, t, 128), kv_map)]
        + [HBM_SPEC] * na,
        out_shape=[_sds((MLA_HEADS, s, 256), F32), _sds((MLA_HEADS, s, 256), F32),
                   _sds((MLA_HEADS, s, 128), F32)] + _exchange_shapes(blocks, False),
        scratch_shapes=_Exchange.semaphores(na) if na else [],
        compiler_params=_params(2),
    )(qh, kh, vh, do, dlt, lse, *blocks)
    return outs[0], outs[1], outs[2], list(outs[3:])


def _mla_prep_bwd(dqh, dkh, dvh, proj, gq, gkv, w_q, w_kv, tabs):
    s = proj.shape[0]
    tm = min(256, s)
    c, s1, s2 = tabs

    def body(dq_ref, dk_ref, dv_ref, p_ref, gq_ref, gkv_ref, wq_ref, wkv_ref,
             c_ref, s1_ref, s2_ref, dp_ref, dqf_ref, dkvf_ref, dgq_ref, dgkv_ref):
        @pl.when(pl.program_id(0) == 0)
        def _():
            dgq_ref[...] = jnp.zeros(dgq_ref.shape, F32)
            dgkv_ref[...] = jnp.zeros(dgkv_ref.shape, F32)
        cc, ns1, ns2 = c_ref[...], -s1_ref[...], -s2_ref[...]
        dkr = jnp.zeros((tm, 128), F32)
        for h in range(MLA_HEADS):
            dqf_ref[:, 128 * h:128 * (h + 1)] = dq_ref[h, :, 0:128].astype(dqf_ref.dtype)
            dqf_ref[:, 1024 + 128 * h:1024 + 128 * (h + 1)] = _rope(
                dq_ref[h, :, 128:256], cc, ns1, ns2).astype(dqf_ref.dtype)
            dkvf_ref[:, 128 * h:128 * (h + 1)] = dk_ref[h, :, 0:128].astype(dkvf_ref.dtype)
            dkvf_ref[:, 1024 + 128 * h:1024 + 128 * (h + 1)] = dv_ref[h].astype(dkvf_ref.dtype)
            dkr = dkr + dk_ref[h, :, 128:256]
        dcqn = _mm_nt(dqf_ref[...], wq_ref[...])
        dckvn = _mm_nt(dkvf_ref[...], wkv_ref[...])
        cqh, rq = _rownorm(p_ref[:, 0:384])
        ckvh, rkv = _rownorm(p_ref[:, 384:640])
        dgq_ref[...] += jnp.sum(dcqn * cqh, axis=0, keepdims=True)
        dgkv_ref[...] += jnp.sum(dckvn * ckvh, axis=0, keepdims=True)
        dp_ref[:, 0:384] = _rownorm_bwd(dcqn * gq_ref[...], cqh, rq).astype(dp_ref.dtype)
        dp_ref[:, 384:640] = _rownorm_bwd(dckvn * gkv_ref[...], ckvh, rkv).astype(dp_ref.dtype)
        dp_ref[:, 640:768] = _rope(dkr, cc, ns1, ns2).astype(dp_ref.dtype)

    row = lambda i: (i, 0)
    fixed = lambda i: (0, 0)
    head = lambda i: (0, i, 0)
    return pl.pallas_call(
        body, name="mla_prep_bwd", grid=(s // tm,),
        in_specs=[pl.BlockSpec((MLA_HEADS, tm, 256), head), pl.BlockSpec((MLA_HEADS, tm, 256), head),
                  pl.BlockSpec((MLA_HEADS, tm, 128), head),
                  pl.BlockSpec((tm, 768), lambda i: (i, CQ // 768)),
                  pl.BlockSpec((1, Q_RANK), fixed), pl.BlockSpec((1, KV_RANK), fixed),
                  pl.BlockSpec((Q_RANK, 2048), fixed), pl.BlockSpec((KV_RANK, 2048), fixed),
                  pl.BlockSpec((tm, 128), row), pl.BlockSpec((tm, 128), row),
                  pl.BlockSpec((tm, 128), row)],
        out_specs=[pl.BlockSpec((tm, 768), row), pl.BlockSpec((tm, 2048), row),
                   pl.BlockSpec((tm, 2048), row),
                   pl.BlockSpec((1, Q_RANK), fixed), pl.BlockSpec((1, KV_RANK), fixed)],
        out_shape=[_sds((s, 768), MXU_DTYPE), _sds((s, 2048), MXU_DTYPE), _sds((s, 2048), MXU_DTYPE),
                   _sds((1, Q_RANK), F32), _sds((1, KV_RANK), F32)],
        compiler_params=_params(1),
    )(dqh, dkh, dvh, proj, gq, gkv, w_q, w_kv, c, s1, s2)


def _in_proj_bwd(dqa, dgate, dlat, dkva, w, x, dx_out, g):
    s = x.shape[0]
    tm = min(256, s)

    def body(dqa_ref, dg8_ref, dlat_ref, dkva_ref, w_ref, x_ref, dxo_ref, g_ref, dx_ref, dg_ref):
        @pl.when(pl.program_id(0) == 0)
        def _():
            dg_ref[...] = jnp.zeros(dg_ref.shape, F32)

        dh = (_mm_nt(dqa_ref[...], w_ref[:, QA:QA + 1024])
              + _mm_nt(dg8_ref[:, 0:1024], w_ref[:, GA:GA + 1024])
              + _mm_nt(dg8_ref[:, 1024:2048], w_ref[:, GB:GB + 1024])
              + _mm_nt(dlat_ref[...], w_ref[:, CQ:CQ + 768])
              + _mm_nt(dkva_ref[...], w_ref[:, KVA:KVA + 256]))
        xh, r = _rownorm(x_ref[...])
        dg_ref[...] += jnp.sum(dh * xh, axis=0, keepdims=True)
        dx_ref[...] = dxo_ref[...] + _rownorm_bwd(dh * g_ref[...], xh, r)

    row = lambda i: (i, 0)
    fixed = lambda i: (0, 0)
    return pl.pallas_call(
        body, name="in_proj_bwd", grid=(s // tm,),
        in_specs=[pl.BlockSpec((tm, 1024), row), pl.BlockSpec((tm, 2048), row),
                  pl.BlockSpec((tm, 768), row), pl.BlockSpec((tm, 256), row),
                  pl.BlockSpec((D_MODEL, NP), fixed, pipeline_mode=pl.Buffered(1)),
                  pl.BlockSpec((tm, D_MODEL), row), pl.BlockSpec((tm, D_MODEL), row),
                  pl.BlockSpec((1, D_MODEL), fixed)],
        out_specs=[pl.BlockSpec((tm, D_MODEL), row), pl.BlockSpec((1, D_MODEL), fixed)],
        out_shape=[_sds((s, D_MODEL), F32), _sds((1, D_MODEL), F32)],
        compiler_params=_params(1),
    )(dqa, dgate, dlat, dkva, w, x, dx_out, g)


def _reduce_adamw(parts, w, m, v, name):
    n_layers = len(parts)
    rows, cols = parts[0].shape[1:]
    lanes = -(-cols // 128) * 128
    tr = rows
    for cand in (1024, 512, 256, 128, 64, 32, 16, 8):
        if rows % cand == 0 and N_DEV * cand * lanes * 4 <= 8 * 1024 * 1024:
            tr = cand
            break
    nr = rows // tr

    def body(*refs):
        p_refs = refs[:n_layers]
        w_ref, m_ref, v_ref, g_ref, d_ref, nm_ref, nv_ref = refs[n_layers:]
        layer = pl.program_id(0)
        for l in range(n_layers):
            @pl.when(layer == l)
            def _(l=l):
                g = p_refs[l][0].astype(F32)
                for k in range(1, N_DEV):
                    g = g + p_refs[l][k].astype(F32)
                update(g, w_ref, m_ref, v_ref, g_ref, d_ref, nm_ref, nv_ref)

    def update(g, w_ref, m_ref, v_ref, g_ref, d_ref, nm_ref, nv_ref):
        m2 = ADAM_B1 * m_ref[...] + (1.0 - ADAM_B1) * g
        v2 = ADAM_B2 * v_ref[...] + (1.0 - ADAM_B2) * (g * g)
        m_hat = m2 / (1.0 - ADAM_B1 ** ADAM_STEP)
        v_hat = v2 / (1.0 - ADAM_B2 ** ADAM_STEP)
        g_ref[...] = g
        d_ref[...] = -ADAM_LR * (m_hat / (jnp.sqrt(v_hat) + ADAM_EPS) + ADAM_WD * w_ref[...])
        nm_ref[...] = m2
        nv_ref[...] = v2

    def part_spec(l):
        return pl.BlockSpec((N_DEV, tr, cols), lambda layer, i: (0, jnp.where(layer == l, i, 0), 0))

    blk = pl.BlockSpec((tr, cols), lambda layer, i: (layer * nr + i, 0))
    return pl.pallas_call(
        body, name=name, grid=(n_layers, nr),
        in_specs=[part_spec(l) for l in range(n_layers)] + [blk, blk, blk],
        out_specs=[blk, blk, blk, blk],
        out_shape=[_sds((n_layers * rows, cols), F32)] * 4,
        compiler_params=_params(2),
    )(*parts, w, m, v)


def _position():
    x, y, c = lax.axis_index("x"), lax.axis_index("y"), lax.axis_index("c")
    return x, y, c


def _index(px, py, pc):
    return 4 * px + 2 * py + pc


HBM_SPEC = pl.BlockSpec(memory_space=pltpu.HBM)


def _all_gather(shards, name):
    na = len(shards)

    def body(*refs):
        ins, outs = refs[:na], refs[na:2 * na]
        send_sems, recv_sems, local_sems = refs[2 * na:]
        x, y, c = _position()
        me, sibling = (x, y, c), (x, y, 1 - c)
        chips = [(1 - x, y), (x, 1 - y), (1 - x, 1 - y)]

        def copy(a, k, block, to, src=None):
            rows = outs[a].at[_index(*block)]
            return pltpu.make_async_remote_copy(
                src_ref=rows if src is None else src, dst_ref=rows,
                send_sem=send_sems.at[7 * a + k], recv_sem=recv_sems.at[7 * a + k],
                device_id=to, device_id_type=MESH)

        mine = [pltpu.make_async_copy(ins[a], outs[a].at[_index(*me)], local_sems.at[a])
                for a in range(na)]
        for cp in mine:
            cp.start()
        first = []
        for a in range(na):
            first.append(copy(a, 0, me, sibling, src=ins[a]))
            for j, chip in enumerate(chips):
                first.append(copy(a, 1 + j, me, (*chip, c), src=ins[a]))
        for cp in first:
            cp.start()
        passed = []
        for j, chip in enumerate(chips):
            for a in range(na):
                copy(a, 1 + j, (*chip, c), me).wait_recv()
                cp = copy(a, 4 + j, (*chip, c), sibling)
                cp.start()
                passed.append(cp)
        for a in range(na):
            copy(a, 0, sibling, me).wait_recv()
        for j, chip in enumerate(chips):
            for a in range(na):
                copy(a, 4 + j, (*chip, 1 - c), me).wait_recv()
        for cp in first + passed:
            cp.wait_send()
        for cp in mine:
            cp.wait()

    return pl.pallas_call(
        body, name=name,
        in_specs=[HBM_SPEC] * na, out_specs=[HBM_SPEC] * na,
        out_shape=[_sds((N_DEV,) + a.shape, a.dtype) for a in shards],
        scratch_shapes=[pltpu.SemaphoreType.DMA((7 * na,)), pltpu.SemaphoreType.DMA((7 * na,)),
                        pltpu.SemaphoreType.DMA((na,))],
    )(*shards)


class _Exchange:
    def __init__(self, srcs, dsts, sems, gather):
        self.srcs, self.dsts, self.gather = srcs, dsts, gather
        self.send_sems, self.recv_sems, self.local_sems = sems
        x, y, c = _position()
        self.me = _index(x, y, c)
        self.peers = [(x ^ ((k >> 2) & 1), y ^ ((k >> 1) & 1), c ^ (k & 1)) for k in range(1, N_DEV)]

    @staticmethod
    def semaphores(na):
        return [pltpu.SemaphoreType.DMA((7 * na,)), pltpu.SemaphoreType.DMA((7 * na,)),
                pltpu.SemaphoreType.DMA((na,))]

    def _src(self, a, slot):
        return self.srcs[a] if self.gather else self.srcs[a].at[slot]

    def _local(self, a):
        return pltpu.make_async_copy(self._src(a, self.me), self.dsts[a].at[self.me],
                                     self.local_sems.at[a])

    def _send(self, a, k):
        peer = self.peers[k]
        return pltpu.make_async_remote_copy(
            src_ref=self._src(a, _index(*peer)), dst_ref=self.dsts[a].at[self.me],
            send_sem=self.send_sems.at[7 * a + k], recv_sem=self.recv_sems.at[7 * a + k],
            device_id=peer, device_id_type=MESH)

    def _arrival(self, a, k):
        landed = self.dsts[a].at[_index(*self.peers[k])]
        return pltpu.make_async_remote_copy(
            src_ref=landed, dst_ref=landed,
            send_sem=self.send_sems.at[7 * a + k], recv_sem=self.recv_sems.at[7 * a + k],
            device_id=self.peers[k], device_id_type=MESH)

    def start(self):
        for a in range(len(self.srcs)):
            self._local(a).start()
            for k in range(N_DEV - 1):
                self._send(a, k).start()

    def wait(self):
        for a in range(len(self.srcs)):
            for k in range(N_DEV - 1):
                self._arrival(a, k).wait_recv()
            for k in range(N_DEV - 1):
                self._send(a, k).wait_send()
            self._local(a).wait()


def _exchange_shapes(arrays, gather):
    return [_sds(((N_DEV,) + a.shape) if gather else a.shape, a.dtype) for a in arrays]


def _exchange_call(arrays, gather, name):
    na = len(arrays)

    def body(*refs):
        ex = _Exchange(refs[:na], refs[na:2 * na], refs[2 * na:], gather)
        ex.start()
        ex.wait()

    return pl.pallas_call(
        body, name=name,
        in_specs=[HBM_SPEC] * na, out_specs=[HBM_SPEC] * na,
        out_shape=_exchange_shapes(arrays, gather),
        scratch_shapes=_Exchange.semaphores(na),
    )(*arrays)


def _layer_fwd(x, small, weights, tabs, next_shards):
    attn_g, sinks, gq, gkv = small
    w_in, w_q, w_kv, w_o = weights
    proj, h = _in_proj(x, attn_g, w_in)
    swa = _swa_fwd(proj, sinks)
    qh, kh, vh, cqn, ckvn = _mla_prep(proj, gq, gkv, w_q, w_kv, tabs)
    mla, lse, gathered = _mla_fwd(qh, kh, vh, next_shards)
    x_next, y = _out_proj(x, proj, swa, mla, w_o)
    return x_next, (x, proj, h, swa, qh, kh, vh, cqn, ckvn, mla, lse, y), gathered


def _layer_bwd(dx, saved, small, weights, tabs, pending):
    attn_g, sinks, gq, gkv = small
    w_in, w_q, w_kv, w_o = weights
    x, proj, h, swa, qh, kh, vh, cqn, ckvn, mla, lse, y = saved
    d_o = _matmul_tn(y, dx, "grad_w_out")
    do_a, do_b, dgate, dlt = _out_proj_bwd(dx, proj, swa, mla, w_o)
    dqa, dkva, dsk = _swa_bwd(proj, sinks, do_a, swa)
    dqh, dkh, dvh, received = _mla_bwd(qh, kh, vh, do_b, dlt, lse, pending)
    dlat, dqf, dkvf, d_gq, d_gkv = _mla_prep_bwd(dqh, dkh, dvh, proj, gq, gkv, w_q, w_kv, tabs)
    d_q = _matmul_tn(cqn, dqf, "grad_w_q_b")
    d_kv = _matmul_tn(ckvn, dkvf, "grad_w_kv_b")
    d_qa = _matmul_tn(h, dqa, "grad_w_in_qa")
    d_gate = _matmul_tn(h, dgate, "grad_w_in_gate")
    d_lat = _matmul_tn(h, dlat, "grad_w_in_lat")
    d_kva = _matmul_tn(h, dkva, "grad_w_in_kva")
    dx, d_attn = _in_proj_bwd(dqa, dgate, dlat, dkva, w_in, x, dx, attn_g)
    blocks = _grads_to_blocks(d_qa, d_gate, d_lat, d_kva, d_q, d_kv, d_o)
    return dx, (d_attn, dsk[0:1, 0:SWA_HEADS], d_gq, d_gkv), blocks, received


def _pack_small_grads(small_grads, d_final, loss):
    d_attn, d_sink, d_gq, d_gkv = zip(*small_grads)
    return jnp.concatenate([
        jnp.concatenate(d_attn, axis=0).reshape(64, 128),
        jnp.concatenate(d_gq, axis=0).reshape(12, 128),
        jnp.concatenate(d_gkv, axis=0).reshape(8, 128),
        d_final.reshape(16, 128),
        jnp.pad(jnp.concatenate(d_sink, axis=1), ((0, 0), (0, 64))),
        loss[0:1],
        jnp.zeros((PACK_ROWS - ROW_LOSS - 1, 128), F32)], axis=0)


def _pack_small(attn, qa, kva, final, sinks):
    return jnp.concatenate([
        attn.reshape(64, 128), qa.reshape(12, 128), kva.reshape(8, 128), final.reshape(16, 128),
        jnp.pad(sinks.reshape(1, 64), ((0, 0), (0, 64))),
        jnp.zeros((PACK_ROWS - ROW_SINK - 1, 128), F32)], axis=0)


def _unpack_small(p):
    return (p[ROW_ATTN:ROW_QA].reshape(DEPTH, D_MODEL), p[ROW_SINK, 0:64].reshape(DEPTH, SWA_HEADS),
            p[ROW_QA:ROW_KVA].reshape(DEPTH, Q_RANK), p[ROW_KVA:ROW_FINAL].reshape(DEPTH, KV_RANK),
            p[ROW_FINAL:ROW_SINK].reshape(D_MODEL))


def kernel(x, attn_norm_g, w_in, swa_sinks, q_a_norm_g, kv_a_norm_g, w_q_b, w_kv_b, w_out, final_norm_g, loss_target, m_attn_norm_g, m_w_in, m_swa_sinks, m_q_a_norm_g, m_kv_a_norm_g, m_w_q_b, m_w_kv_b, m_w_out, m_final_norm_g, v_attn_norm_g, v_w_in, v_swa_sinks, v_q_a_norm_g, v_kv_a_norm_g, v_w_q_b, v_w_kv_b, v_w_out, v_final_norm_g):
    xs, tgt = x[0], loss_target[0]
    tabs = _rope_tables(xs.shape[0])
    shards = [w.astype(MXU_DTYPE) for w in (w_in, w_q_b, w_kv_b, w_out)]
    layer_shards = lambda l: [w[l] for w in shards]
    smalls = [(attn_norm_g[l:l + 1], swa_sinks[l], q_a_norm_g[l:l + 1], kv_a_norm_g[l:l + 1])
              for l in range(DEPTH)]

    gathered = _all_gather(layer_shards(0), "gather_weights")
    weights, saved = [], []
    for l in range(DEPTH):
        weights.append(_weights_from_gathered(*gathered))
        next_shards = layer_shards(l + 1) if l + 1 < DEPTH else []
        xs, acts, gathered = _layer_fwd(xs, smalls[l], weights[l], tabs, next_shards)
        saved.append(acts)
    dx, d_final, loss = _final_loss(xs, tgt, final_norm_g.reshape(1, D_MODEL))

    received, small_grads, pending = [None] * DEPTH, [None] * DEPTH, []
    for l in reversed(range(DEPTH)):
        dx, small_grads[l], blocks, arrived = _layer_bwd(dx, saved[l], smalls[l], weights[l], tabs,
                                                         pending)
        if pending:
            received[l + 1] = arrived
        pending = blocks
    received[0] = _exchange_call(pending, False, "scatter_grads")
    small = _exchange_call([_pack_small_grads(small_grads, d_final, loss)], True, "gather_small")[0]

    big = []
    for a, (w, m, v, name) in enumerate(zip((w_in, w_q_b, w_kv_b, w_out),
                                            (m_w_in, m_w_q_b, m_w_kv_b, m_w_out),
                                            (v_w_in, v_w_q_b, v_w_kv_b, v_w_out),
                                            ("adamw_w_in", "adamw_w_q_b", "adamw_w_kv_b",
                                             "adamw_w_out"))):
        cols = w.shape[-1]
        flat = lambda t: t.reshape(-1, cols)
        outs = _reduce_adamw([received[l][a] for l in range(DEPTH)], flat(w), flat(m), flat(v), name)
        big.append([t.reshape(w.shape) for t in outs])

    sm = _reduce_adamw(
        [small],
        _pack_small(attn_norm_g, q_a_norm_g, kv_a_norm_g, final_norm_g, swa_sinks),
        _pack_small(m_attn_norm_g, m_q_a_norm_g, m_kv_a_norm_g, m_final_norm_g, m_swa_sinks),
        _pack_small(v_attn_norm_g, v_q_a_norm_g, v_kv_a_norm_g, v_final_norm_g, v_swa_sinks),
        "adamw_small")
    loss = sm[0][ROW_LOSS, 0]
    kinds = []
    for t in range(4):
        attn, sinks, qa, kva, final = _unpack_small(sm[t])
        b_in, b_q, b_kv, b_o = (big[i][t] for i in range(4))
        kinds.append((attn, b_in, sinks, qa, kva, b_q, b_kv, b_o, final))
    return (loss, dx[None], *kinds[0], *kinds[1], *kinds[2], *kinds[3])
```

```python
import functools

import jax
import jax.numpy as jnp
from jax import lax
from jax.experimental import pallas as pl
from jax.experimental.pallas import tpu as pltpu

F32 = jnp.float32
BF16 = jnp.bfloat16
MXU_DTYPE = BF16
GRAD_DTYPE = BF16

D_MODEL = 2048
DEPTH = 4
EPS = 1e-6
NEG = -1e30
BLOCK = 128
SWA_HEADS = 16
MLA_HEADS = 8
Q_RANK = 384
KV_RANK = 256
MLA_SCALE = 192 ** -0.5
MLA_C2 = MLA_SCALE * 1.4426950408889634
ROPE_THETA = 10000.0
IN_WIDTH = 4032

ADAM_LR = 0.001
ADAM_B1 = 0.9
ADAM_B2 = 0.999
ADAM_EPS = 1e-08
ADAM_WD = 0.01
ADAM_STEP = 10

N_DEV = 8
MESH = pl.DeviceIdType.MESH

NP = 4096
QA, GA, GB, CQ, CKV, KR, KVA = 0, 1024, 2048, 3072, 3456, 3712, 3840

ROW_ATTN, ROW_QA, ROW_KVA, ROW_FINAL, ROW_SINK, ROW_LOSS, PACK_ROWS = 0, 64, 76, 84, 100, 101, 104

VMEM_LIMIT = 56 * 1024 * 1024
MLA_TILE = 512
GATHER_FORWARD_HEAD = 5


def _sds(shape, dtype):
    return jax.ShapeDtypeStruct(shape, dtype)


def _params(n_axes):
    return pltpu.CompilerParams(dimension_semantics=("arbitrary",) * n_axes,
                                vmem_limit_bytes=VMEM_LIMIT)


def _mm(a, b):
    return jnp.dot(a.astype(MXU_DTYPE), b.astype(MXU_DTYPE), preferred_element_type=F32)


def _mm_nt(a, b):
    return lax.dot_general(a.astype(MXU_DTYPE), b.astype(MXU_DTYPE),
                           (((1,), (1,)), ((), ())), preferred_element_type=F32)


def _mm_tn(a, b):
    return lax.dot_general(a.astype(MXU_DTYPE), b.astype(MXU_DTYPE),
                           (((0,), (0,)), ((), ())), preferred_element_type=F32)


def _rownorm(x):
    r = lax.rsqrt(jnp.mean(x * x, axis=-1, keepdims=True) + EPS)
    return x * r, r


def _rownorm_bwd(dxh, xh, r):
    return r * (dxh - xh * jnp.mean(dxh * xh, axis=-1, keepdims=True))


def _rope(t, c, s1, s2):
    return t * c + pltpu.roll(t, 32, 1) * s1 + pltpu.roll(t, 96, 1) * s2


def _pad_in_cols(w):
    z = jnp.zeros(w.shape[:-1] + (64,), w.dtype)
    return jnp.concatenate([w[..., 0:1024], w[..., 1280:2304], w[..., 3008:4032], w[..., 2304:2688],
                            w[..., 2688:2944], w[..., 2944:3008], z, w[..., 1024:1152],
                            w[..., 1152:1280]], axis=-1)


def _weights_from_gathered(g_in, g_qb, g_kvb, g_out):
    w_in = _pad_in_cols(g_in.transpose(1, 0, 2).reshape(D_MODEL, IN_WIDTH))
    qb = g_qb.transpose(1, 0, 2)
    rope = jnp.pad(qb[..., 128:], ((0, 0), (0, 0), (0, 64)))
    w_q = jnp.concatenate([qb[..., :128].reshape(Q_RANK, 1024),
                           rope.reshape(Q_RANK, 1024)], axis=-1)
    kvb = g_kvb.transpose(1, 0, 2)
    w_kv = jnp.concatenate([kvb[..., :128].reshape(KV_RANK, 1024),
                            kvb[..., 128:].reshape(KV_RANK, 1024)], axis=-1)
    w_o = g_out.reshape(D_MODEL, D_MODEL)
    return w_in, w_q, w_kv, w_o


def _grads_to_blocks(d_qa, d_gate, d_lat, d_kva, d_q, d_kv, d_o):
    d_in = jnp.concatenate([d_qa, d_kva, d_gate[:, :1024], d_lat[:, :704], d_gate[:, 1024:]], axis=1)
    b_in = d_in.reshape(D_MODEL, N_DEV, 504).transpose(1, 0, 2)
    qn = d_q[:, :1024].reshape(Q_RANK, 8, 128)
    qr = d_q[:, 1024:].reshape(Q_RANK, 8, 128)[..., :64]
    b_q = jnp.concatenate([qn, qr], axis=-1).transpose(1, 0, 2)
    kn = d_kv[:, :1024].reshape(KV_RANK, 8, 128)
    vv = d_kv[:, 1024:].reshape(KV_RANK, 8, 128)
    b_kv = jnp.concatenate([kn, vv], axis=-1).transpose(1, 0, 2)
    b_o = d_o.reshape(N_DEV, 256, D_MODEL)
    return [b_in, b_q, b_kv, b_o]


def _rope_tables(s):
    pos = jnp.arange(s, dtype=F32)
    inv_freq = ROPE_THETA ** (-jnp.arange(0, 64, 2, dtype=F32) / 64)
    ang = pos[:, None] * inv_freq[None, :]
    cos, sin = jnp.cos(ang), jnp.sin(ang)
    z32 = jnp.zeros((s, 32), F32)
    z64 = jnp.zeros((s, 64), F32)
    c = jnp.concatenate([cos, cos, z64], axis=1)
    s1 = jnp.concatenate([z32, sin, z64], axis=1)
    s2 = jnp.concatenate([-sin, z32, z64], axis=1)
    return c, s1, s2


def _in_proj(x, g, w):
    s = x.shape[0]
    tm, tn = min(512, s), 1024

    def body(x_ref, g_ref, w_ref, o_ref, h_ref):
        xh, _ = _rownorm(x_ref[...])
        h_ref[...] = (xh * g_ref[...]).astype(h_ref.dtype)
        for j in range(NP // tn):
            cols = slice(j * tn, (j + 1) * tn)
            o_ref[:, cols] = jnp.dot(h_ref[...], w_ref[:, cols], preferred_element_type=F32)

    row = lambda i: (i, 0)
    fixed = lambda i: (0, 0)
    return pl.pallas_call(
        body, name="in_proj", grid=(s // tm,),
        in_specs=[pl.BlockSpec((tm, D_MODEL), row), pl.BlockSpec((1, D_MODEL), fixed),
                  pl.BlockSpec((D_MODEL, NP), fixed, pipeline_mode=pl.Buffered(1))],
        out_specs=[pl.BlockSpec((tm, NP), row), pl.BlockSpec((tm, D_MODEL), row)],
        out_shape=[_sds((s, NP), F32), _sds((s, D_MODEL), MXU_DTYPE)],
        compiler_params=_params(1),
    )(x, g, w)


def _swa_slopes():
    return [2.0 ** (-8.0 * (h + 1) / SWA_HEADS) for h in range(SWA_HEADS)]


def _swa_operands(kv_p, kv_c):
    kk = jnp.concatenate([kv_p[:, :128], kv_c[:, :128]], axis=0)
    vv = jnp.concatenate([kv_p[:, 128:], kv_c[:, 128:]], axis=0)
    left = lax.broadcasted_iota(jnp.int32, (2 * BLOCK, 128), 1) < 64

    def split(t):
        lo = jnp.where(left, t, 0.0)
        hi = jnp.where(left, 0.0, t)
        lefts = [lo, pltpu.roll(hi, 64, 1)]
        rights = [pltpu.roll(lo, 64, 1), hi]
        return lefts, rights

    return split(kk), split(vv), left


SWA_STACK = 4 * BLOCK


def _swa_head(j, a, e):
    return 2 * (4 * j + a) + e


def _swa_bias(n, j, e):
    slopes = _swa_slopes()
    r = lax.broadcasted_iota(jnp.int32, (SWA_STACK, 2 * BLOCK), 0)
    ki = lax.broadcasted_iota(jnp.int32, (SWA_STACK, 2 * BLOCK), 1)
    a = r >> 7
    delta = BLOCK + (r & (BLOCK - 1)) - ki
    valid = (delta >= 0) & (delta < BLOCK) & ((n - 1) * BLOCK + ki >= 0)
    sl = [slopes[_swa_head(j, t, e)] for t in range(4)]
    slope = jnp.where(a == 0, sl[0], jnp.where(a == 1, sl[1], jnp.where(a == 2, sl[2], sl[3])))
    return jnp.where(valid, -slope * delta.astype(F32), NEG)


def _swa_fill_bias(n, bias_ref):
    @pl.when(n <= 1)
    def _():
        for j in range(2):
            for e in range(2):
                bias_ref[2 * j + e] = _swa_bias(n, j, e)


def _swa_sink_col(sink_ref, j, e):
    a = lax.broadcasted_iota(jnp.int32, (SWA_STACK, 1), 0) >> 7
    sk = [sink_ref[_swa_head(j, t, e)] for t in range(4)]
    return jnp.where(a == 0, sk[0], jnp.where(a == 1, sk[1], jnp.where(a == 2, sk[2], sk[3])))


def _swa_stack(ref, j):
    return jnp.concatenate([ref[:, 128 * (4 * j + a):128 * (4 * j + a + 1)] for a in range(4)], axis=0)


def _swa_softmax(qs, kmat, bias, sink):
    sc = _mm_nt(qs, kmat) + bias
    m = jnp.maximum(jnp.max(sc, axis=1, keepdims=True), sink)
    ex = jnp.exp(sc - m)
    es = jnp.exp(sink - m)
    return ex, es, 1.0 / (jnp.sum(ex, axis=1, keepdims=True) + es)


def _swa_fwd(proj, sinks):
    s = proj.shape[0]
    nb = s // BLOCK

    def body(sink_ref, q_ref, kp_ref, kc_ref, o_ref, bias_ref):
        n = pl.program_id(0)
        _swa_fill_bias(n, bias_ref)
        (k_l, k_r), (v_l, v_r), _ = _swa_operands(kp_ref[...], kc_ref[...])
        for j in range(2):
            qs = _swa_stack(q_ref, j) * 0.125
            out = None
            for e in range(2):
                ex, _, inv = _swa_softmax(qs, (k_l, k_r)[e][j], bias_ref[2 * j + e],
                                          _swa_sink_col(sink_ref, j, e))
                o = _mm(ex, (v_l, v_r)[e][j]) * inv
                out = o if out is None else out + o
            for a in range(4):
                o_ref[:, 128 * (4 * j + a):128 * (4 * j + a + 1)] = out[128 * a:128 * (a + 1)]

    return pl.pallas_call(
        body, name="swa_fwd", grid=(nb,),
        in_specs=[pl.BlockSpec(memory_space=pltpu.SMEM),
                  pl.BlockSpec((BLOCK, 1024), lambda n: (n, 0)),
                  pl.BlockSpec((BLOCK, 256), lambda n: (jnp.maximum(n - 1, 0), KVA // 256)),
                  pl.BlockSpec((BLOCK, 256), lambda n: (n, KVA // 256))],
        out_specs=pl.BlockSpec((BLOCK, 1024), lambda n: (n, 0)),
        out_shape=_sds((s, 1024), F32),
        scratch_shapes=[pltpu.VMEM((4, SWA_STACK, 2 * BLOCK), F32)],
        compiler_params=_params(1),
    )(sinks, proj, proj, proj)


def _mla_prep(proj, gq, gkv, w_q, w_kv, tabs):
    s = proj.shape[0]
    tm = min(512, s)
    c, s1, s2 = tabs

    def body(p_ref, gq_ref, gkv_ref, wq_ref, wkv_ref, c_ref, s1_ref, s2_ref,
             q_ref, k_ref, v_ref, vt_ref, cqn_ref, ckvn_ref):
        cqh, _ = _rownorm(p_ref[:, 0:384])
        ckvh, _ = _rownorm(p_ref[:, 384:640])
        cqn = (cqh * gq_ref[...]).astype(MXU_DTYPE)
        ckvn = (ckvh * gkv_ref[...]).astype(MXU_DTYPE)
        cqn_ref[...] = cqn
        ckvn_ref[...] = ckvn
        q = _mm(cqn, wq_ref[...])
        kv = _mm(ckvn, wkv_ref[...])
        cc, ss1, ss2 = c_ref[...], s1_ref[...], s2_ref[...]
        krr = _rope(p_ref[:, 640:768], cc, ss1, ss2).astype(k_ref.dtype)
        for h in range(MLA_HEADS):
            q_ref[h, :, 0:128] = q[:, 128 * h:128 * (h + 1)].astype(q_ref.dtype)
            q_ref[h, :, 128:256] = _rope(q[:, 1024 + 128 * h:1024 + 128 * (h + 1)],
                                         cc, ss1, ss2).astype(q_ref.dtype)
            k_ref[h, :, 0:128] = kv[:, 128 * h:128 * (h + 1)].astype(k_ref.dtype)
            k_ref[h, :, 128:256] = krr
            vv = kv[:, 1024 + 128 * h:1024 + 128 * (h + 1)]
            v_ref[h] = vv.astype(v_ref.dtype)
            vt_ref[h, 0:128, :] = vv.T.astype(vt_ref.dtype)
            vt_ref[h, 128:256, :] = jnp.ones((128, tm), vt_ref.dtype)

    row = lambda i: (i, 0)
    fixed = lambda i: (0, 0)
    return pl.pallas_call(
        body, name="mla_prep", grid=(s // tm,),
        in_specs=[pl.BlockSpec((tm, 768), lambda i: (i, CQ // 768)),
                  pl.BlockSpec((1, Q_RANK), fixed), pl.BlockSpec((1, KV_RANK), fixed),
                  pl.BlockSpec((Q_RANK, 2048), fixed), pl.BlockSpec((KV_RANK, 2048), fixed),
                  pl.BlockSpec((tm, 128), row), pl.BlockSpec((tm, 128), row),
                  pl.BlockSpec((tm, 128), row)],
        out_specs=[pl.BlockSpec((MLA_HEADS, tm, 256), lambda i: (0, i, 0)),
                   pl.BlockSpec((MLA_HEADS, tm, 256), lambda i: (0, i, 0)),
                   pl.BlockSpec((MLA_HEADS, tm, 128), lambda i: (0, i, 0)),
                   pl.BlockSpec((MLA_HEADS, 256, tm), lambda i: (0, 0, i)),
                   pl.BlockSpec((tm, Q_RANK), row), pl.BlockSpec((tm, KV_RANK), row)],
        out_shape=[_sds((MLA_HEADS, s, 256), MXU_DTYPE), _sds((MLA_HEADS, s, 256), MXU_DTYPE),
                   _sds((MLA_HEADS, s, 128), MXU_DTYPE), _sds((MLA_HEADS, 256, s), MXU_DTYPE),
                   _sds((s, Q_RANK), MXU_DTYPE), _sds((s, KV_RANK), MXU_DTYPE)],
        compiler_params=_params(1),
    )(proj, gq, gkv, w_q, w_kv, c, s1, s2)


def _raw_scores(q, k, t, diagonal):
    sc = _mm_nt(q, k)
    if diagonal:
        row = lax.broadcasted_iota(jnp.int32, (t, t), 0)
        col = lax.broadcasted_iota(jnp.int32, (t, t), 1)
        sc = jnp.where(col <= row, sc, NEG)
    return sc


def _scores_t(k, q, t, diagonal):
    sc = _mm_nt(k, q)
    if diagonal:
        key = lax.broadcasted_iota(jnp.int32, (t, t), 0)
        query = lax.broadcasted_iota(jnp.int32, (t, t), 1)
        sc = jnp.where(key <= query, sc, NEG)
    return sc


def _mla_fwd(qh, kh, vth, shards):
    s = qh.shape[1]
    t = min(MLA_TILE, s)
    nq = s // t
    na = len(shards)

    def body(*refs):
        q_ref, k_ref, vt_ref = refs[:3]
        o_ref, lse_ref = refs[3 + na:5 + na]
        m_ref, acc_ref = refs[5 + 2 * na:7 + 2 * na]
        h, i = pl.program_id(0), pl.program_id(1)
        if na:
            ex = _Gather(refs[3:3 + na], refs[5 + na:5 + 2 * na], refs[7 + 2 * na:])

            @pl.when((h == 0) & (i == 0))
            def _():
                ex.start()

            @pl.when((h == GATHER_FORWARD_HEAD) & (i == 0))
            def _():
                ex.forward()

        m_ref[...] = jnp.full(m_ref.shape, NEG, F32)
        acc_ref[...] = jnp.zeros(acc_ref.shape, F32)

        def step(j, diagonal):
            keys = pl.ds(pl.multiple_of(j * t, t), t)
            sc = _scores_t(k_ref[keys, :], q_ref[...], t, diagonal)
            m_prev = m_ref[...]
            m_new = jnp.maximum(m_prev, jnp.max(sc, axis=0, keepdims=True))
            alpha = jnp.exp2((m_prev - m_new) * MLA_C2)
            p = jnp.exp2((sc - m_new[0:1, :]) * MLA_C2)
            acc_ref[...] = alpha[0:1, :] * acc_ref[...] + _mm(vt_ref[:, keys], p)
            m_ref[...] = m_new

        def below_diagonal(j, carry):
            step(j, False)
            return carry

        lax.fori_loop(0, i, below_diagonal, 0)
        step(i, True)
        l = acc_ref[128:136, :]
        o_ref[...] = (acc_ref[0:128, :] / l[0:1, :]).T
        lse_ref[...] = m_ref[...] * MLA_C2 + jnp.log2(l)

        if na:
            @pl.when((h == MLA_HEADS - 1) & (i == nq - 1))
            def _():
                ex.finish()

    head = lambda h, i: (h, 0, 0)
    outs = pl.pallas_call(
        body, name="mla_fwd_gather" if na else "mla_fwd", grid=(MLA_HEADS, nq),
        in_specs=[pl.BlockSpec((None, t, 256), lambda h, i: (h, i, 0)),
                  pl.BlockSpec((None, s, 256), head),
                  pl.BlockSpec((None, 256, s), head)] + [HBM_SPEC] * na,
        out_specs=[pl.BlockSpec((t, 128), lambda h, i: (i, h)),
                   pl.BlockSpec((None, 8, t), lambda h, i: (h, 0, i))] + [HBM_SPEC] * na,
        out_shape=[_sds((s, 1024), F32), _sds((MLA_HEADS, 8, s), F32)]
        + _exchange_shapes(shards, True),
        scratch_shapes=[pltpu.VMEM((8, t), F32), pltpu.VMEM((256, t), F32)]
        + (_Exchange.semaphores(na) if na else []),
        compiler_params=_params(2),
    )(qh, kh, vth, *shards)
    return outs[0], outs[1], list(outs[2:])


def _silu_parts(g):
    sg = jax.nn.sigmoid(g)
    return g * sg, sg * (1.0 + g * (1.0 - sg))


def _out_proj(x, proj, swa, mla, w_out):
    s = x.shape[0]
    tm = min(512, s)

    def body(x_ref, ga_ref, gb_ref, a_ref, b_ref, w_ref, xo_ref, y_ref):
        sa, _ = _silu_parts(ga_ref[...])
        sb, _ = _silu_parts(gb_ref[...])
        y_ref[:, 0:1024] = (a_ref[...] * sa).astype(y_ref.dtype)
        y_ref[:, 1024:2048] = (b_ref[...] * sb).astype(y_ref.dtype)
        xo_ref[...] = x_ref[...] + jnp.dot(y_ref[...], w_ref[...], preferred_element_type=F32)

    row = lambda i: (i, 0)
    return pl.pallas_call(
        body, name="out_proj", grid=(s // tm,),
        in_specs=[pl.BlockSpec((tm, D_MODEL), row),
                  pl.BlockSpec((tm, 1024), lambda i: (i, GA // 1024)),
                  pl.BlockSpec((tm, 1024), lambda i: (i, GB // 1024)),
                  pl.BlockSpec((tm, 1024), row), pl.BlockSpec((tm, 1024), row),
                  pl.BlockSpec((D_MODEL, D_MODEL), lambda i: (0, 0), pipeline_mode=pl.Buffered(1))],
        out_specs=[pl.BlockSpec((tm, D_MODEL), row), pl.BlockSpec((tm, D_MODEL), row)],
        out_shape=[_sds((s, D_MODEL), F32), _sds((s, D_MODEL), MXU_DTYPE)],
        compiler_params=_params(1),
    )(x, proj, proj, swa, mla, w_out)


def _final_loss(x, tgt, g):
    s = x.shape[0]
    tm = min(512, s)

    def body(x_ref, t_ref, g_ref, dx_ref, dg_ref, loss_ref):
        @pl.when(pl.program_id(0) == 0)
        def _():
            dg_ref[...] = jnp.zeros(dg_ref.shape, F32)
            loss_ref[...] = jnp.zeros(loss_ref.shape, F32)
        xh, r = _rownorm(x_ref[...])
        gg = g_ref[...]
        err = xh * gg - t_ref[...]
        per_row = jnp.mean(err * err, axis=-1, keepdims=True)
        loss_ref[...] += 0.5 * jnp.sum(per_row, axis=0, keepdims=True)
        dy = err * (1.0 / D_MODEL)
        dg_ref[...] += jnp.sum(dy * xh, axis=0, keepdims=True)
        dx_ref[...] = _rownorm_bwd(dy * gg, xh, r)

    row = lambda i: (i, 0)
    fixed = lambda i: (0, 0)
    return pl.pallas_call(
        body, name="final_loss", grid=(s // tm,),
        in_specs=[pl.BlockSpec((tm, D_MODEL), row), pl.BlockSpec((tm, D_MODEL), row),
                  pl.BlockSpec((1, D_MODEL), fixed)],
        out_specs=[pl.BlockSpec((tm, D_MODEL), row), pl.BlockSpec((1, D_MODEL), fixed),
                   pl.BlockSpec((8, 128), fixed)],
        out_shape=[_sds((s, D_MODEL), F32), _sds((1, D_MODEL), F32), _sds((8, 128), F32)],
        compiler_params=_params(1),
    )(x, tgt, g)


def _out_proj_bwd(dx, proj, swa, mla, w_out):
    s = dx.shape[0]
    tm = min(512, s)

    def body(dx_ref, ga_ref, gb_ref, a_ref, b_ref, w_ref, doa_ref, dob_ref, dg_ref, dlt_ref):
        dx = dx_ref[...].astype(MXU_DTYPE)
        dya = _mm_nt(dx, w_ref[0:1024, :])
        sa, dsa = _silu_parts(ga_ref[...])
        doa_ref[...] = dya * sa
        dg_ref[:, 0:1024] = (dya * a_ref[...] * dsa).astype(dg_ref.dtype)
        dyb = _mm_nt(dx, w_ref[1024:2048, :])
        sb, dsb = _silu_parts(gb_ref[...])
        b = b_ref[...]
        dob = dyb * sb
        dob_ref[...] = dob.astype(dob_ref.dtype)
        dg_ref[:, 1024:2048] = (dyb * b * dsb).astype(dg_ref.dtype)
        prod = dob * b
        for h in range(MLA_HEADS):
            dlt = jnp.sum(prod[:, 128 * h:128 * (h + 1)], axis=1, keepdims=True)
            dlt_ref[h] = jnp.broadcast_to(dlt, (tm, 128)).T[0:8, :]

    row = lambda i: (i, 0)
    return pl.pallas_call(
        body, name="out_proj_bwd", grid=(s // tm,),
        in_specs=[pl.BlockSpec((tm, D_MODEL), row),
                  pl.BlockSpec((tm, 1024), lambda i: (i, GA // 1024)),
                  pl.BlockSpec((tm, 1024), lambda i: (i, GB // 1024)),
                  pl.BlockSpec((tm, 1024), row), pl.BlockSpec((tm, 1024), row),
                  pl.BlockSpec((D_MODEL, D_MODEL), lambda i: (0, 0), pipeline_mode=pl.Buffered(1))],
        out_specs=[pl.BlockSpec((tm, 1024), row), pl.BlockSpec((tm, 1024), row),
                   pl.BlockSpec((tm, D_MODEL), row),
                   pl.BlockSpec((MLA_HEADS, 8, tm), lambda i: (0, 0, i))],
        out_shape=[_sds((s, 1024), F32), _sds((s, 1024), MXU_DTYPE), _sds((s, D_MODEL), MXU_DTYPE),
                   _sds((MLA_HEADS, 8, s), F32)],
        compiler_params=_params(1),
    )(dx, proj, proj, swa, mla, w_out)


def _matmul_tn(a, b, name):
    s, m = a.shape
    n = b.shape[1]
    tm, tn, tk = min(512, m), min(1024, n), min(2048, s)
    nk = s // tk

    def body(a_ref, b_ref, o_ref, acc_ref):
        k = pl.program_id(2)

        @pl.when(k == 0)
        def _():
            acc_ref[...] = jnp.zeros(acc_ref.shape, F32)
        acc_ref[...] += _mm_tn(a_ref[...], b_ref[...])

        @pl.when(k == nk - 1)
        def _():
            o_ref[...] = acc_ref[...].astype(o_ref.dtype)

    return pl.pallas_call(
        body, name=name, grid=(m // tm, n // tn, nk),
        in_specs=[pl.BlockSpec((tk, tm), lambda i, j, k: (k, i)),
                  pl.BlockSpec((tk, tn), lambda i, j, k: (k, j))],
        out_specs=pl.BlockSpec((tm, tn), lambda i, j, k: (i, j)),
        out_shape=_sds((m, n), GRAD_DTYPE),
        scratch_shapes=[pltpu.VMEM((tm, tn), F32)],
        compiler_params=_params(3),
    )(a, b)


def _swa_bwd(proj, sinks, do, o):
    s = proj.shape[0]
    nb = s // BLOCK

    def body(sink_ref, q_ref, kp_ref, kc_ref, do_ref, o_ref, dq_ref, dkv_ref, dsink_ref,
             carry_ref, bias_ref):
        n = pl.program_id(0)
        _swa_fill_bias(n, bias_ref)

        @pl.when(n == 0)
        def _():
            carry_ref[...] = jnp.zeros(carry_ref.shape, F32)
            dsink_ref[...] = jnp.zeros(dsink_ref.shape, F32)

        @pl.when(n < nb)
        def _():
            (k_l, k_r), (v_l, v_r), left = _swa_operands(kp_ref[...], kc_ref[...])
            left_q = lax.broadcasted_iota(jnp.int32, (SWA_STACK, 128), 1) < 64
            lane_s = lax.broadcasted_iota(jnp.int32, (8, 128), 1)
            dk_l, dk_r, dv_l, dv_r = [None, None], [None, None], [None, None], [None, None]
            dsink = jnp.zeros((8, 128), F32)
            for j in range(2):
                qs = _swa_stack(q_ref, j) * 0.125
                dos = _swa_stack(do_ref, j)
                prod = dos * _swa_stack(o_ref, j)
                dlt = [jnp.sum(jnp.where(left_q, prod, 0.0), axis=1, keepdims=True),
                       jnp.sum(jnp.where(left_q, 0.0, prod), axis=1, keepdims=True)]
                dq = None
                for e in range(2):
                    kmat = (k_l, k_r)[e][j]
                    vmat = (v_l, v_r)[e][j]
                    ex, es, inv = _swa_softmax(qs, kmat, bias_ref[2 * j + e],
                                               _swa_sink_col(sink_ref, j, e))
                    p = ex * inv
                    ds = p * (_mm_nt(dos, vmat) - dlt[e])
                    sink_term = es * inv * dlt[e]
                    for a in range(4):
                        dsh = -jnp.sum(sink_term[128 * a:128 * (a + 1)], axis=0, keepdims=True)
                        dsink = dsink + jnp.where(lane_s == _swa_head(j, a, e), dsh, 0.0)
                    dq_e = _mm(ds, kmat)
                    dq = dq_e if dq is None else dq + dq_e
                    (dv_l, dv_r)[e][j] = _mm_tn(p, dos)
                    (dk_l, dk_r)[e][j] = _mm_tn(ds, qs)
                for a in range(4):
                    cols = slice(128 * (4 * j + a), 128 * (4 * j + a + 1))
                    dq_ref[:, cols] = (dq[128 * a:128 * (a + 1)] * 0.125).astype(dq_ref.dtype)

            def merge(t_l, t_r):
                head0 = t_l[0] + pltpu.roll(t_r[0], 64, 1)
                head1 = t_r[1] + pltpu.roll(t_l[1], 64, 1)
                return jnp.where(left, head0, head1)

            contrib = jnp.concatenate([merge(dk_l, dk_r), merge(dv_l, dv_r)], axis=1)
            dkv_ref[...] = (carry_ref[...] + contrib[0:BLOCK]).astype(dkv_ref.dtype)
            carry_ref[...] = contrib[BLOCK:2 * BLOCK]
            dsink_ref[...] += dsink

        @pl.when(n == nb)
        def _():
            dkv_ref[...] = carry_ref[...].astype(dkv_ref.dtype)

    cur = lambda n: (jnp.minimum(n, nb - 1), 0)
    return pl.pallas_call(
        body, name="swa_bwd", grid=(nb + 1,),
        in_specs=[pl.BlockSpec(memory_space=pltpu.SMEM),
                  pl.BlockSpec((BLOCK, 1024), cur),
                  pl.BlockSpec((BLOCK, 256), lambda n: (jnp.clip(n - 1, 0, nb - 1), KVA // 256)),
                  pl.BlockSpec((BLOCK, 256), lambda n: (jnp.minimum(n, nb - 1), KVA // 256)),
                  pl.BlockSpec((BLOCK, 1024), cur), pl.BlockSpec((BLOCK, 1024), cur)],
        out_specs=[pl.BlockSpec((BLOCK, 1024), cur),
                   pl.BlockSpec((BLOCK, 256), lambda n: (jnp.maximum(n - 1, 0), 0)),
                   pl.BlockSpec((8, 128), lambda n: (0, 0))],
        out_shape=[_sds((s, 1024), MXU_DTYPE), _sds((s, 256), MXU_DTYPE), _sds((8, 128), F32)],
        scratch_shapes=[pltpu.VMEM((BLOCK, 256), F32), pltpu.VMEM((4, SWA_STACK, 2 * BLOCK), F32)],
        compiler_params=_params(1),
    )(sinks, proj, proj, proj, do, o)


def _mla_bwd(qh, kh, vh, do, dlt, lse, blocks):
    s = qh.shape[1]
    t = min(MLA_TILE, s)
    nq = s // t

    na = len(blocks)

    def body(*refs):
        q_ref, k_ref, v_ref, do_ref, dlt_ref, lse_ref = refs[:6]
        dq_ref, dk_ref, dv_ref = refs[6 + na:9 + na]
        h, j = pl.program_id(0), pl.program_id(1)
        if na:
            ex = _Exchange(refs[6:6 + na], refs[9 + na:9 + 2 * na], refs[9 + 2 * na:], gather=False)

            @pl.when((h == 0) & (j == 0))
            def _():
                ex.start()

        def step(i, diagonal):
            rows = pl.ds(pl.multiple_of(i * t, t), t)
            q, k, dout = q_ref[rows, :], k_ref[...], do_ref[rows, :]
            sc = _scores_t(k, q, t, diagonal)
            p = jnp.exp2(sc * MLA_C2 - lse_ref[0:1, rows])
            dv = _mm(p, dout)
            ds = p * (_mm_nt(v_ref[...], dout) - dlt_ref[0:1, rows])
            dk = _mm(ds, q)
            dq = _mm_tn(ds, k)
            return rows, dq, dk, dv

        rows, dq, dk, dv = step(j, True)
        dk_ref[...] = dk
        dv_ref[...] = dv

        @pl.when(j == 0)
        def _():
            dq_ref[rows, :] = dq * MLA_SCALE

        @pl.when(j > 0)
        def _():
            dq_ref[rows, :] = (dq_ref[rows, :] + dq) * MLA_SCALE

        def above_diagonal(i, carry):
            rows, dq, dk, dv = step(i, False)
            dk_ref[...] += dk
            dv_ref[...] += dv

            @pl.when(j == 0)
            def _():
                dq_ref[rows, :] = dq

            @pl.when(j > 0)
            def _():
                dq_ref[rows, :] += dq
            return carry

        lax.fori_loop(j + 1, nq, above_diagonal, 0)
        dk_ref[...] *= MLA_SCALE

        if na:
            @pl.when((h == MLA_HEADS - 1) & (j == nq - 1))
            def _():
                ex.wait()

    head = lambda h, j: (h, 0, 0)
    kv_map = lambda h, j: (h, j, 0)
    outs = pl.pallas_call(
        body, name="mla_bwd_scatter" if na else "mla_bwd", grid=(MLA_HEADS, nq),
        in_specs=[pl.BlockSpec((None, s, 256), head), pl.BlockSpec((None, t, 256), kv_map),
                  pl.BlockSpec((None, t, 128), kv_map),
                  pl.BlockSpec((s, 128), lambda h, j: (0, h)),
                  pl.BlockSpec((None, 8, s), head), pl.BlockSpec((None, 8, s), head)]
        + [HBM_SPEC] * na,
        out_specs=[pl.BlockSpec((None, s, 256), head),
                   pl.BlockSpec((None, t, 256), kv_map), pl.BlockSpec((None, t, 128), kv_map)]
        + [HBM_SPEC] * na,
        out_shape=[_sds((MLA_HEADS, s, 256), F32), _sds((MLA_HEADS, s, 256), F32),
                   _sds((MLA_HEADS, s, 128), F32)] + _exchange_shapes(blocks, False),
        scratch_shapes=_Exchange.semaphores(na) if na else [],
        compiler_params=_params(2),
    )(qh, kh, vh, do, dlt, lse, *blocks)
    return outs[0], outs[1], outs[2], list(outs[3:])


def _mla_prep_bwd(dqh, dkh, dvh, proj, gq, gkv, w_q, w_kv, tabs):
    s = proj.shape[0]
    tm = min(256, s)
    c, s1, s2 = tabs

    def body(dq_ref, dk_ref, dv_ref, p_ref, gq_ref, gkv_ref, wq_ref, wkv_ref,
             c_ref, s1_ref, s2_ref, dp_ref, dqf_ref, dkvf_ref, dgq_ref, dgkv_ref):
        @pl.when(pl.program_id(0) == 0)
        def _():
            dgq_ref[...] = jnp.zeros(dgq_ref.shape, F32)
            dgkv_ref[...] = jnp.zeros(dgkv_ref.shape, F32)
        cc, ns1, ns2 = c_ref[...], -s1_ref[...], -s2_ref[...]
        dkr = jnp.zeros((tm, 128), F32)
        for h in range(MLA_HEADS):
            dqf_ref[:, 128 * h:128 * (h + 1)] = dq_ref[h, :, 0:128].astype(dqf_ref.dtype)
            dqf_ref[:, 1024 + 128 * h:1024 + 128 * (h + 1)] = _rope(
                dq_ref[h, :, 128:256], cc, ns1, ns2).astype(dqf_ref.dtype)
            dkvf_ref[:, 128 * h:128 * (h + 1)] = dk_ref[h, :, 0:128].astype(dkvf_ref.dtype)
            dkvf_ref[:, 1024 + 128 * h:1024 + 128 * (h + 1)] = dv_ref[h].astype(dkvf_ref.dtype)
            dkr = dkr + dk_ref[h, :, 128:256]
        dcqn = _mm_nt(dqf_ref[...], wq_ref[...])
        dckvn = _mm_nt(dkvf_ref[...], wkv_ref[...])
        cqh, rq = _rownorm(p_ref[:, 0:384])
        ckvh, rkv = _rownorm(p_ref[:, 384:640])
        dgq_ref[...] += jnp.sum(dcqn * cqh, axis=0, keepdims=True)
        dgkv_ref[...] += jnp.sum(dckvn * ckvh, axis=0, keepdims=True)
        dp_ref[:, 0:384] = _rownorm_bwd(dcqn * gq_ref[...], cqh, rq).astype(dp_ref.dtype)
        dp_ref[:, 384:640] = _rownorm_bwd(dckvn * gkv_ref[...], ckvh, rkv).astype(dp_ref.dtype)
        dp_ref[:, 640:768] = _rope(dkr, cc, ns1, ns2).astype(dp_ref.dtype)

    row = lambda i: (i, 0)
    fixed = lambda i: (0, 0)
    head = lambda i: (0, i, 0)
    return pl.pallas_call(
        body, name="mla_prep_bwd", grid=(s // tm,),
        in_specs=[pl.BlockSpec((MLA_HEADS, tm, 256), head), pl.BlockSpec((MLA_HEADS, tm, 256), head),
                  pl.BlockSpec((MLA_HEADS, tm, 128), head),
                  pl.BlockSpec((tm, 768), lambda i: (i, CQ // 768)),
                  pl.BlockSpec((1, Q_RANK), fixed), pl.BlockSpec((1, KV_RANK), fixed),
                  pl.BlockSpec((Q_RANK, 2048), fixed), pl.BlockSpec((KV_RANK, 2048), fixed),
                  pl.BlockSpec((tm, 128), row), pl.BlockSpec((tm, 128), row),
                  pl.BlockSpec((tm, 128), row)],
        out_specs=[pl.BlockSpec((tm, 768), row), pl.BlockSpec((tm, 2048), row),
                   pl.BlockSpec((tm, 2048), row),
                   pl.BlockSpec((1, Q_RANK), fixed), pl.BlockSpec((1, KV_RANK), fixed)],
        out_shape=[_sds((s, 768), MXU_DTYPE), _sds((s, 2048), MXU_DTYPE), _sds((s, 2048), MXU_DTYPE),
                   _sds((1, Q_RANK), F32), _sds((1, KV_RANK), F32)],
        compiler_params=_params(1),
    )(dqh, dkh, dvh, proj, gq, gkv, w_q, w_kv, c, s1, s2)


def _in_proj_bwd(dqa, dgate, dlat, dkva, w, x, dx_out, g, blocks):
    s = x.shape[0]
    tm = min(256, s)
    nm = s // tm
    na = len(blocks)

    def body(*refs):
        dqa_ref, dg8_ref, dlat_ref, dkva_ref, w_ref, x_ref, dxo_ref, g_ref = refs[:8]
        dx_ref, dg_ref = refs[8 + na:10 + na]
        if na:
            ex = _Exchange(refs[8:8 + na], refs[10 + na:10 + 2 * na], refs[10 + 2 * na:], gather=False)

        @pl.when(pl.program_id(0) == 0)
        def _():
            dg_ref[...] = jnp.zeros(dg_ref.shape, F32)
            if na:
                ex.start()

        dh = (_mm_nt(dqa_ref[...], w_ref[:, QA:QA + 1024])
              + _mm_nt(dg8_ref[:, 0:1024], w_ref[:, GA:GA + 1024])
              + _mm_nt(dg8_ref[:, 1024:2048], w_ref[:, GB:GB + 1024])
              + _mm_nt(dlat_ref[...], w_ref[:, CQ:CQ + 768])
              + _mm_nt(dkva_ref[...], w_ref[:, KVA:KVA + 256]))
        xh, r = _rownorm(x_ref[...])
        dg_ref[...] += jnp.sum(dh * xh, axis=0, keepdims=True)
        dx_ref[...] = dxo_ref[...] + _rownorm_bwd(dh * g_ref[...], xh, r)

        if na:
            @pl.when(pl.program_id(0) == nm - 1)
            def _():
                ex.wait()

    row = lambda i: (i, 0)
    fixed = lambda i: (0, 0)
    outs = pl.pallas_call(
        body, name="in_proj_bwd_scatter" if na else "in_proj_bwd", grid=(nm,),
        in_specs=[pl.BlockSpec((tm, 1024), row), pl.BlockSpec((tm, 2048), row),
                  pl.BlockSpec((tm, 768), row), pl.BlockSpec((tm, 256), row),
                  pl.BlockSpec((D_MODEL, NP), fixed, pipeline_mode=pl.Buffered(1)),
                  pl.BlockSpec((tm, D_MODEL), row), pl.BlockSpec((tm, D_MODEL), row),
                  pl.BlockSpec((1, D_MODEL), fixed)] + [HBM_SPEC] * na,
        out_specs=[pl.BlockSpec((tm, D_MODEL), row), pl.BlockSpec((1, D_MODEL), fixed)]
        + [HBM_SPEC] * na,
        out_shape=[_sds((s, D_MODEL), F32), _sds((1, D_MODEL), F32)]
        + _exchange_shapes(blocks, False),
        scratch_shapes=_Exchange.semaphores(na) if na else [],
        compiler_params=_params(1),
    )(dqa, dgate, dlat, dkva, w, x, dx_out, g, *blocks)
    return outs[0], outs[1], list(outs[2:])


def _reduce_adamw(parts, w, m, v, name):
    n_layers = len(parts)
    rows, cols = parts[0].shape[1:]
    lanes = -(-cols // 128) * 128
    tr = rows
    for cand in (1024, 512, 256, 128, 64, 32, 16, 8):
        if rows % cand == 0 and N_DEV * cand * lanes * 4 <= 8 * 1024 * 1024:
            tr = cand
            break
    nr = rows // tr

    def body(*refs):
        p_refs = refs[:n_layers]
        w_ref, m_ref, v_ref, g_ref, d_ref, nm_ref, nv_ref = refs[n_layers:]
        layer = pl.program_id(0)
        for l in range(n_layers):
            @pl.when(layer == l)
            def _(l=l):
                g = p_refs[l][0].astype(F32)
                for k in range(1, N_DEV):
                    g = g + p_refs[l][k].astype(F32)
                update(g, w_ref, m_ref, v_ref, g_ref, d_ref, nm_ref, nv_ref)

    def update(g, w_ref, m_ref, v_ref, g_ref, d_ref, nm_ref, nv_ref):
        m2 = ADAM_B1 * m_ref[...] + (1.0 - ADAM_B1) * g
        v2 = ADAM_B2 * v_ref[...] + (1.0 - ADAM_B2) * (g * g)
        m_hat = m2 / (1.0 - ADAM_B1 ** ADAM_STEP)
        v_hat = v2 / (1.0 - ADAM_B2 ** ADAM_STEP)
        g_ref[...] = g
        d_ref[...] = -ADAM_LR * (m_hat / (jnp.sqrt(v_hat) + ADAM_EPS) + ADAM_WD * w_ref[...])
        nm_ref[...] = m2
        nv_ref[...] = v2

    def part_spec(l):
        return pl.BlockSpec((N_DEV, tr, cols), lambda layer, i: (0, jnp.where(layer == l, i, 0), 0))

    blk = pl.BlockSpec((tr, cols), lambda layer, i: (layer * nr + i, 0))
    return pl.pallas_call(
        body, name=name, grid=(n_layers, nr),
        in_specs=[part_spec(l) for l in range(n_layers)] + [blk, blk, blk],
        out_specs=[blk, blk, blk, blk],
        out_shape=[_sds((n_layers * rows, cols), F32)] * 4,
        compiler_params=_params(2),
    )(*parts, w, m, v)


def _position():
    x, y, c = lax.axis_index("x"), lax.axis_index("y"), lax.axis_index("c")
    return x, y, c


def _index(px, py, pc):
    return 4 * px + 2 * py + pc


HBM_SPEC = pl.BlockSpec(memory_space=pltpu.HBM)


class _Gather:
    def __init__(self, srcs, dsts, sems):
        self.srcs, self.dsts = srcs, dsts
        self.send_sems, self.recv_sems, self.local_sems = sems
        x, y, c = _position()
        self.c = c
        self.me, self.sibling = (x, y, c), (x, y, 1 - c)
        self.chips = [(1 - x, y), (x, 1 - y), (1 - x, 1 - y)]

    def _copy(self, a, k, block, to, own=False):
        slot = self.dsts[a].at[_index(*block)]
        return pltpu.make_async_remote_copy(
            src_ref=self.srcs[a] if own else slot, dst_ref=slot,
            send_sem=self.send_sems.at[7 * a + k], recv_sem=self.recv_sems.at[7 * a + k],
            device_id=to, device_id_type=MESH)

    def _local(self, a):
        return pltpu.make_async_copy(self.srcs[a], self.dsts[a].at[_index(*self.me)],
                                     self.local_sems.at[a])

    def _first(self, a):
        return [self._copy(a, 0, self.me, self.sibling, own=True)] + [
            self._copy(a, 1 + j, self.me, (*chip, self.c), own=True)
            for j, chip in enumerate(self.chips)]

    def _passed(self, a, j):
        return self._copy(a, 4 + j, (*self.chips[j], self.c), self.sibling)

    def start(self):
        for a in range(len(self.srcs)):
            self._local(a).start()
            for cp in self._first(a):
                cp.start()

    def forward(self):
        for j, chip in enumerate(self.chips):
            for a in range(len(self.srcs)):
                self._copy(a, 1 + j, (*chip, self.c), self.me).wait_recv()
                self._passed(a, j).start()

    def finish(self):
        for a in range(len(self.srcs)):
            self._copy(a, 0, self.sibling, self.me).wait_recv()
            for j, chip in enumerate(self.chips):
                self._copy(a, 4 + j, (*chip, 1 - self.c), self.me).wait_recv()
            for cp in self._first(a):
                cp.wait_send()
            for j in range(3):
                self._passed(a, j).wait_send()
            self._local(a).wait()


def _all_gather(shards, name):
    na = len(shards)

    def body(*refs):
        g = _Gather(refs[:na], refs[na:2 * na], refs[2 * na:])
        g.start()
        g.forward()
        g.finish()

    return pl.pallas_call(
        body, name=name,
        in_specs=[HBM_SPEC] * na, out_specs=[HBM_SPEC] * na,
        out_shape=_exchange_shapes(shards, True),
        scratch_shapes=_Exchange.semaphores(na),
    )(*shards)


class _Exchange:
    def __init__(self, srcs, dsts, sems, gather):
        self.srcs, self.dsts, self.gather = srcs, dsts, gather
        self.send_sems, self.recv_sems, self.local_sems = sems
        x, y, c = _position()
        self.me = _index(x, y, c)
        self.peers = [(x ^ ((k >> 2) & 1), y ^ ((k >> 1) & 1), c ^ (k & 1)) for k in range(1, N_DEV)]

    @staticmethod
    def semaphores(na):
        return [pltpu.SemaphoreType.DMA((7 * na,)), pltpu.SemaphoreType.DMA((7 * na,)),
                pltpu.SemaphoreType.DMA((na,))]

    def _src(self, a, slot):
        return self.srcs[a] if self.gather else self.srcs[a].at[slot]

    def _local(self, a):
        return pltpu.make_async_copy(self._src(a, self.me), self.dsts[a].at[self.me],
                                     self.local_sems.at[a])

    def _send(self, a, k):
        peer = self.peers[k]
        return pltpu.make_async_remote_copy(
            src_ref=self._src(a, _index(*peer)), dst_ref=self.dsts[a].at[self.me],
            send_sem=self.send_sems.at[7 * a + k], recv_sem=self.recv_sems.at[7 * a + k],
            device_id=peer, device_id_type=MESH)

    def _arrival(self, a, k):
        landed = self.dsts[a].at[_index(*self.peers[k])]
        return pltpu.make_async_remote_copy(
            src_ref=landed, dst_ref=landed,
            send_sem=self.send_sems.at[7 * a + k], recv_sem=self.recv_sems.at[7 * a + k],
            device_id=self.peers[k], device_id_type=MESH)

    def start(self):
        for a in range(len(self.srcs)):
            self._local(a).start()
            for k in range(N_DEV - 1):
                self._send(a, k).start()

    def wait(self):
        for a in range(len(self.srcs)):
            for k in range(N_DEV - 1):
                self._arrival(a, k).wait_recv()
            for k in range(N_DEV - 1):
                self._send(a, k).wait_send()
            self._local(a).wait()


def _exchange_shapes(arrays, gather):
    return [_sds(((N_DEV,) + a.shape) if gather else a.shape, a.dtype) for a in arrays]


def _exchange_call(arrays, gather, name):
    na = len(arrays)

    def body(*refs):
        ex = _Exchange(refs[:na], refs[na:2 * na], refs[2 * na:], gather)
        ex.start()
        ex.wait()

    return pl.pallas_call(
        body, name=name,
        in_specs=[HBM_SPEC] * na, out_specs=[HBM_SPEC] * na,
        out_shape=_exchange_shapes(arrays, gather),
        scratch_shapes=_Exchange.semaphores(na),
    )(*arrays)


def _layer_fwd(x, small, weights, tabs, next_shards):
    attn_g, sinks, gq, gkv = small
    w_in, w_q, w_kv, w_o = weights
    proj, h = _in_proj(x, attn_g, w_in)
    swa = _swa_fwd(proj, sinks)
    qh, kh, vh, vth, cqn, ckvn = _mla_prep(proj, gq, gkv, w_q, w_kv, tabs)
    mla, lse, gathered = _mla_fwd(qh, kh, vth, next_shards)
    x_next, y = _out_proj(x, proj, swa, mla, w_o)
    return x_next, (x, proj, h, swa, qh, kh, vh, cqn, ckvn, mla, lse, y), gathered


def _layer_bwd(dx, saved, small, weights, tabs, pending, send_own):
    attn_g, sinks, gq, gkv = small
    w_in, w_q, w_kv, w_o = weights
    x, proj, h, swa, qh, kh, vh, cqn, ckvn, mla, lse, y = saved
    d_o = _matmul_tn(y, dx, "grad_w_out")
    do_a, do_b, dgate, dlt = _out_proj_bwd(dx, proj, swa, mla, w_o)
    dqa, dkva, dsk = _swa_bwd(proj, sinks, do_a, swa)
    dqh, dkh, dvh, received = _mla_bwd(qh, kh, vh, do_b, dlt, lse, pending)
    dlat, dqf, dkvf, d_gq, d_gkv = _mla_prep_bwd(dqh, dkh, dvh, proj, gq, gkv, w_q, w_kv, tabs)
    d_q = _matmul_tn(cqn, dqf, "grad_w_q_b")
    d_kv = _matmul_tn(ckvn, dkvf, "grad_w_kv_b")
    d_qa = _matmul_tn(h, dqa, "grad_w_in_qa")
    d_gate = _matmul_tn(h, dgate, "grad_w_in_gate")
    d_lat = _matmul_tn(h, dlat, "grad_w_in_lat")
    d_kva = _matmul_tn(h, dkva, "grad_w_in_kva")
    blocks = _grads_to_blocks(d_qa, d_gate, d_lat, d_kva, d_q, d_kv, d_o)
    dx, d_attn, received_own = _in_proj_bwd(dqa, dgate, dlat, dkva, w_in, x, dx, attn_g,
                                            blocks if send_own else [])
    return dx, (d_attn, dsk[0:1, 0:SWA_HEADS], d_gq, d_gkv), blocks, received, received_own


def _pack_small_grads(small_grads, d_final, loss):
    d_attn, d_sink, d_gq, d_gkv = zip(*small_grads)
    return jnp.concatenate([
        jnp.concatenate(d_attn, axis=0).reshape(64, 128),
        jnp.concatenate(d_gq, axis=0).reshape(12, 128),
        jnp.concatenate(d_gkv, axis=0).reshape(8, 128),
        d_final.reshape(16, 128),
        jnp.pad(jnp.concatenate(d_sink, axis=1), ((0, 0), (0, 64))),
        loss[0:1],
        jnp.zeros((PACK_ROWS - ROW_LOSS - 1, 128), F32)], axis=0)


def _pack_small(attn, qa, kva, final, sinks):
    return jnp.concatenate([
        attn.reshape(64, 128), qa.reshape(12, 128), kva.reshape(8, 128), final.reshape(16, 128),
        jnp.pad(sinks.reshape(1, 64), ((0, 0), (0, 64))),
        jnp.zeros((PACK_ROWS - ROW_SINK - 1, 128), F32)], axis=0)


def _unpack_small(p):
    return (p[ROW_ATTN:ROW_QA].reshape(DEPTH, D_MODEL), p[ROW_SINK, 0:64].reshape(DEPTH, SWA_HEADS),
            p[ROW_QA:ROW_KVA].reshape(DEPTH, Q_RANK), p[ROW_KVA:ROW_FINAL].reshape(DEPTH, KV_RANK),
            p[ROW_FINAL:ROW_SINK].reshape(D_MODEL))


def kernel(x, attn_norm_g, w_in, swa_sinks, q_a_norm_g, kv_a_norm_g, w_q_b, w_kv_b, w_out, final_norm_g, loss_target, m_attn_norm_g, m_w_in, m_swa_sinks, m_q_a_norm_g, m_kv_a_norm_g, m_w_q_b, m_w_kv_b, m_w_out, m_final_norm_g, v_attn_norm_g, v_w_in, v_swa_sinks, v_q_a_norm_g, v_kv_a_norm_g, v_w_q_b, v_w_kv_b, v_w_out, v_final_norm_g):
    xs, tgt = x[0], loss_target[0]
    tabs = _rope_tables(xs.shape[0])
    shards = [w.astype(MXU_DTYPE) for w in (w_in, w_q_b, w_kv_b, w_out)]
    layer_shards = lambda l: [w[l] for w in shards]
    smalls = [(attn_norm_g[l:l + 1], swa_sinks[l], q_a_norm_g[l:l + 1], kv_a_norm_g[l:l + 1])
              for l in range(DEPTH)]

    gathered = _all_gather(layer_shards(0), "gather_weights")
    weights, saved = [], []
    for l in range(DEPTH):
        weights.append(_weights_from_gathered(*gathered))
        next_shards = layer_shards(l + 1) if l + 1 < DEPTH else []
        xs, acts, gathered = _layer_fwd(xs, smalls[l], weights[l], tabs, next_shards)
        saved.append(acts)
    dx, d_final, loss = _final_loss(xs, tgt, final_norm_g.reshape(1, D_MODEL))

    received, small_grads, pending = [None] * DEPTH, [None] * DEPTH, []
    for l in reversed(range(DEPTH)):
        dx, small_grads[l], blocks, arrived, arrived_own = _layer_bwd(
            dx, saved[l], smalls[l], weights[l], tabs, pending, send_own=(l == 0))
        if pending:
            received[l + 1] = arrived
        pending = blocks
    received[0] = arrived_own
    small = _exchange_call([_pack_small_grads(small_grads, d_final, loss)], True, "gather_small")[0]

    big = []
    for a, (w, m, v, name) in enumerate(zip((w_in, w_q_b, w_kv_b, w_out),
                                            (m_w_in, m_w_q_b, m_w_kv_b, m_w_out),
                                            (v_w_in, v_w_q_b, v_w_kv_b, v_w_out),
                                            ("adamw_w_in", "adamw_w_q_b", "adamw_w_kv_b",
                                             "adamw_w_out"))):
        cols = w.shape[-1]
        flat = lambda t: t.reshape(-1, cols)
        outs = _reduce_adamw([received[l][a] for l in range(DEPTH)], flat(w), flat(m), flat(v), name)
        big.append([t.reshape(w.shape) for t in outs])

    sm = _reduce_adamw(
        [small],
        _pack_small(attn_norm_g, q_a_norm_g, kv_a_norm_g, final_norm_g, swa_sinks),
        _pack_small(m_attn_norm_g, m_q_a_norm_g, m_kv_a_norm_g, m_final_norm_g, m_swa_sinks),
        _pack_small(v_attn_norm_g, v_q_a_norm_g, v_kv_a_norm_g, v_final_norm_g, v_swa_sinks),
        "adamw_small")
    loss = sm[0][ROW_LOSS, 0]
    kinds = []
    for t in range(4):
        attn, sinks, qa, kva, final = _unpack_small(sm[t])
        b_in, b_q, b_kv, b_o = (big[i][t] for i in range(4))
        kinds.append((attn, b_in, sinks, qa, kva, b_q, b_kv, b_o, final))
    return (loss, dx[None], *kinds[0], *kinds[1], *kinds[2], *kinds[3])
```

```python
import functools

import jax
import jax.numpy as jnp
from jax import lax
from jax.experimental import pallas as pl
from jax.experimental.pallas import tpu as pltpu

F32 = jnp.float32
BF16 = jnp.bfloat16
MXU_DTYPE = BF16
GRAD_DTYPE = BF16
PROJ_DTYPE = BF16

D_MODEL = 2048
DEPTH = 4
EPS = 1e-6
NEG = -1e30
BLOCK = 128
SWA_HEADS = 16
MLA_HEADS = 8
Q_RANK = 384
KV_RANK = 256
MLA_SCALE = 192 ** -0.5
MLA_C2 = MLA_SCALE * 1.4426950408889634
ROPE_THETA = 10000.0
IN_WIDTH = 4032

ADAM_LR = 0.001
ADAM_B1 = 0.9
ADAM_B2 = 0.999
ADAM_EPS = 1e-08
ADAM_WD = 0.01
ADAM_STEP = 10

N_DEV = 8
MESH = pl.DeviceIdType.MESH

NP = 4096
QA, GA, GB, CQ, CKV, KR, KVA = 0, 1024, 2048, 3072, 3456, 3712, 3840

ROW_ATTN, ROW_QA, ROW_KVA, ROW_FINAL, ROW_SINK, ROW_LOSS, PACK_ROWS = 0, 64, 76, 84, 100, 101, 104

VMEM_LIMIT = 56 * 1024 * 1024
MLA_TILE = 512
GATHER_FORWARD_HEAD = 5


def _sds(shape, dtype):
    return jax.ShapeDtypeStruct(shape, dtype)


def _params(n_axes):
    return pltpu.CompilerParams(dimension_semantics=("arbitrary",) * n_axes,
                                vmem_limit_bytes=VMEM_LIMIT)


def _mm(a, b):
    return jnp.dot(a.astype(MXU_DTYPE), b.astype(MXU_DTYPE), preferred_element_type=F32)


def _mm_nt(a, b):
    return lax.dot_general(a.astype(MXU_DTYPE), b.astype(MXU_DTYPE),
                           (((1,), (1,)), ((), ())), preferred_element_type=F32)


def _mm_tn(a, b):
    return lax.dot_general(a.astype(MXU_DTYPE), b.astype(MXU_DTYPE),
                           (((0,), (0,)), ((), ())), preferred_element_type=F32)


def _rownorm(x):
    r = lax.rsqrt(jnp.mean(x * x, axis=-1, keepdims=True) + EPS)
    return x * r, r


def _rownorm_bwd(dxh, xh, r):
    return r * (dxh - xh * jnp.mean(dxh * xh, axis=-1, keepdims=True))


def _rope(t, c, s1, s2):
    return t * c + pltpu.roll(t, 32, 1) * s1 + pltpu.roll(t, 96, 1) * s2


def _pad_in_cols(w):
    z = jnp.zeros(w.shape[:-1] + (64,), w.dtype)
    return jnp.concatenate([w[..., 0:1024], w[..., 1280:2304], w[..., 3008:4032], w[..., 2304:2688],
                            w[..., 2688:2944], w[..., 2944:3008], z, w[..., 1024:1152],
                            w[..., 1152:1280]], axis=-1)


def _weights_from_gathered(g_in, g_qb, g_kvb, g_out):
    w_in = _pad_in_cols(g_in.transpose(1, 0, 2).reshape(D_MODEL, IN_WIDTH))
    qb = g_qb.transpose(1, 0, 2)
    rope = jnp.pad(qb[..., 128:], ((0, 0), (0, 0), (0, 64)))
    w_q = jnp.concatenate([qb[..., :128].reshape(Q_RANK, 1024),
                           rope.reshape(Q_RANK, 1024)], axis=-1)
    kvb = g_kvb.transpose(1, 0, 2)
    w_kv = jnp.concatenate([kvb[..., :128].reshape(KV_RANK, 1024),
                            kvb[..., 128:].reshape(KV_RANK, 1024)], axis=-1)
    w_o = g_out.reshape(D_MODEL, D_MODEL)
    return w_in, w_q, w_kv, w_o


def _grads_to_blocks(d_qa, d_gate, d_lat, d_kva, d_q, d_kv, d_o):
    d_in = jnp.concatenate([d_qa, d_kva, d_gate[:, :1024], d_lat[:, :704], d_gate[:, 1024:]], axis=1)
    b_in = d_in.reshape(D_MODEL, N_DEV, 504).transpose(1, 0, 2)
    qn = d_q[:, :1024].reshape(Q_RANK, 8, 128)
    qr = d_q[:, 1024:].reshape(Q_RANK, 8, 128)[..., :64]
    b_q = jnp.concatenate([qn, qr], axis=-1).transpose(1, 0, 2)
    kn = d_kv[:, :1024].reshape(KV_RANK, 8, 128)
    vv = d_kv[:, 1024:].reshape(KV_RANK, 8, 128)
    b_kv = jnp.concatenate([kn, vv], axis=-1).transpose(1, 0, 2)
    b_o = d_o.reshape(N_DEV, 256, D_MODEL)
    return [b_in, b_q, b_kv, b_o]


def _rope_tables(s):
    pos = jnp.arange(s, dtype=F32)
    inv_freq = ROPE_THETA ** (-jnp.arange(0, 64, 2, dtype=F32) / 64)
    ang = pos[:, None] * inv_freq[None, :]
    cos, sin = jnp.cos(ang), jnp.sin(ang)
    z32 = jnp.zeros((s, 32), F32)
    z64 = jnp.zeros((s, 64), F32)
    c = jnp.concatenate([cos, cos, z64], axis=1)
    s1 = jnp.concatenate([z32, sin, z64], axis=1)
    s2 = jnp.concatenate([-sin, z32, z64], axis=1)
    return c, s1, s2


def _in_proj(x, g, w):
    s = x.shape[0]
    tm, tn = min(512, s), 1024

    def body(x_ref, g_ref, w_ref, o_ref, h_ref):
        xh, _ = _rownorm(x_ref[...])
        h_ref[...] = (xh * g_ref[...]).astype(h_ref.dtype)
        for j in range(NP // tn):
            cols = slice(j * tn, (j + 1) * tn)
            o_ref[:, cols] = jnp.dot(h_ref[...], w_ref[:, cols],
                                     preferred_element_type=F32).astype(o_ref.dtype)

    row = lambda i: (i, 0)
    fixed = lambda i: (0, 0)
    return pl.pallas_call(
        body, name="in_proj", grid=(s // tm,),
        in_specs=[pl.BlockSpec((tm, D_MODEL), row), pl.BlockSpec((1, D_MODEL), fixed),
                  pl.BlockSpec((D_MODEL, NP), fixed, pipeline_mode=pl.Buffered(1))],
        out_specs=[pl.BlockSpec((tm, NP), row), pl.BlockSpec((tm, D_MODEL), row)],
        out_shape=[_sds((s, NP), PROJ_DTYPE), _sds((s, D_MODEL), MXU_DTYPE)],
        compiler_params=_params(1),
    )(x, g, w)


def _swa_slopes():
    return [2.0 ** (-8.0 * (h + 1) / SWA_HEADS) for h in range(SWA_HEADS)]


def _swa_operands(kv_p, kv_c):
    kk = jnp.concatenate([kv_p[:, :128], kv_c[:, :128]], axis=0)
    vv = jnp.concatenate([kv_p[:, 128:], kv_c[:, 128:]], axis=0)
    left = lax.broadcasted_iota(jnp.int32, (2 * BLOCK, 128), 1) < 64

    def split(t):
        lo = jnp.where(left, t, 0.0)
        hi = jnp.where(left, 0.0, t)
        lefts = [lo, pltpu.roll(hi, 64, 1)]
        rights = [pltpu.roll(lo, 64, 1), hi]
        return lefts, rights

    return split(kk), split(vv), left, vv


SWA_STACK = 4 * BLOCK


def _swa_head(j, a, e):
    return 2 * (4 * j + a) + e


def _swa_bias(n, j, e):
    slopes = _swa_slopes()
    ki = lax.broadcasted_iota(jnp.int32, (2 * BLOCK, SWA_STACK), 0)
    r = lax.broadcasted_iota(jnp.int32, (2 * BLOCK, SWA_STACK), 1)
    a = r >> 7
    delta = BLOCK + (r & (BLOCK - 1)) - ki
    valid = (delta >= 0) & (delta < BLOCK) & ((n - 1) * BLOCK + ki >= 0)
    sl = [slopes[_swa_head(j, t, e)] for t in range(4)]
    slope = jnp.where(a == 0, sl[0], jnp.where(a == 1, sl[1], jnp.where(a == 2, sl[2], sl[3])))
    return jnp.where(valid, -slope * delta.astype(F32), NEG)


def _swa_fill_bias(n, bias_ref):
    @pl.when(n <= 1)
    def _():
        for j in range(2):
            for e in range(2):
                bias_ref[2 * j + e] = _swa_bias(n, j, e)


def _swa_sink_row(sink_ref, j, e):
    a = lax.broadcasted_iota(jnp.int32, (1, SWA_STACK), 1) >> 7
    sk = [sink_ref[_swa_head(j, t, e)] for t in range(4)]
    return jnp.where(a == 0, sk[0], jnp.where(a == 1, sk[1], jnp.where(a == 2, sk[2], sk[3])))


def _swa_stack(ref, j):
    return jnp.concatenate([ref[:, 128 * (4 * j + a):128 * (4 * j + a + 1)] for a in range(4)], axis=0)


def _swa_softmax(qs, kmat, bias, sink):
    sc = _mm_nt(kmat, qs) + bias
    m = jnp.maximum(jnp.max(sc, axis=0, keepdims=True), sink)
    ex = jnp.exp(sc - m)
    es = jnp.exp(sink - m)
    return ex, es, 1.0 / (jnp.sum(ex, axis=0, keepdims=True) + es)


def _swa_values_t(vv):
    vt = vv.T
    top = lax.broadcasted_iota(jnp.int32, (128, 2 * BLOCK), 0) < 64
    lo = jnp.where(top, vt, 0.0)
    hi = jnp.where(top, 0.0, vt)
    return [lo, pltpu.roll(hi, 64, 0)], [pltpu.roll(lo, 64, 0), hi]


def _swa_fwd(proj, sinks):
    s = proj.shape[0]
    nb = s // BLOCK

    def body(sink_ref, q_ref, kp_ref, kc_ref, o_ref, bias_ref):
        n = pl.program_id(0)
        _swa_fill_bias(n, bias_ref)
        (k_l, k_r), _, _, vv = _swa_operands(kp_ref[...].astype(F32), kc_ref[...].astype(F32))
        vt_top, vt_bottom = _swa_values_t(vv)
        for j in range(2):
            qs = _swa_stack(q_ref, j) * 0.125
            out_t = None
            for e in range(2):
                ex, _, inv = _swa_softmax(qs, (k_l, k_r)[e][j], bias_ref[2 * j + e],
                                          _swa_sink_row(sink_ref, j, e))
                o_t = _mm((vt_top, vt_bottom)[e][j], ex) * inv
                out_t = o_t if out_t is None else out_t + o_t
            out = out_t.T
            for a in range(4):
                o_ref[:, 128 * (4 * j + a):128 * (4 * j + a + 1)] = out[128 * a:128 * (a + 1)]

    return pl.pallas_call(
        body, name="swa_fwd", grid=(nb,),
        in_specs=[pl.BlockSpec(memory_space=pltpu.SMEM),
                  pl.BlockSpec((BLOCK, 1024), lambda n: (n, 0)),
                  pl.BlockSpec((BLOCK, 256), lambda n: (jnp.maximum(n - 1, 0), KVA // 256)),
                  pl.BlockSpec((BLOCK, 256), lambda n: (n, KVA // 256))],
        out_specs=pl.BlockSpec((BLOCK, 1024), lambda n: (n, 0)),
        out_shape=_sds((s, 1024), F32),
        scratch_shapes=[pltpu.VMEM((4, 2 * BLOCK, SWA_STACK), F32)],
        compiler_params=_params(1),
    )(sinks, proj, proj, proj)


def _mla_prep(proj, gq, gkv, w_q, w_kv, tabs):
    s = proj.shape[0]
    tm = min(512, s)
    c, s1, s2 = tabs

    def body(p_ref, gq_ref, gkv_ref, wq_ref, wkv_ref, c_ref, s1_ref, s2_ref,
             q_ref, k_ref, v_ref, vt_ref, cqn_ref, ckvn_ref):
        cqh, _ = _rownorm(p_ref[:, 0:384].astype(F32))
        ckvh, _ = _rownorm(p_ref[:, 384:640].astype(F32))
        cqn = (cqh * gq_ref[...]).astype(MXU_DTYPE)
        ckvn = (ckvh * gkv_ref[...]).astype(MXU_DTYPE)
        cqn_ref[...] = cqn
        ckvn_ref[...] = ckvn
        q = _mm(cqn, wq_ref[...])
        kv = _mm(ckvn, wkv_ref[...])
        cc, ss1, ss2 = c_ref[...], s1_ref[...], s2_ref[...]
        krr = _rope(p_ref[:, 640:768].astype(F32), cc, ss1, ss2).astype(k_ref.dtype)
        for h in range(MLA_HEADS):
            q_ref[h, :, 0:128] = q[:, 128 * h:128 * (h + 1)].astype(q_ref.dtype)
            q_ref[h, :, 128:256] = _rope(q[:, 1024 + 128 * h:1024 + 128 * (h + 1)],
                                         cc, ss1, ss2).astype(q_ref.dtype)
            k_ref[h, :, 0:128] = kv[:, 128 * h:128 * (h + 1)].astype(k_ref.dtype)
            k_ref[h, :, 128:256] = krr
            vv = kv[:, 1024 + 128 * h:1024 + 128 * (h + 1)]
            v_ref[h] = vv.astype(v_ref.dtype)
            vt_ref[h, 0:128, :] = vv.T.astype(vt_ref.dtype)
            vt_ref[h, 128:256, :] = jnp.ones((128, tm), vt_ref.dtype)

    row = lambda i: (i, 0)
    fixed = lambda i: (0, 0)
    return pl.pallas_call(
        body, name="mla_prep", grid=(s // tm,),
        in_specs=[pl.BlockSpec((tm, 768), lambda i: (i, CQ // 768)),
                  pl.BlockSpec((1, Q_RANK), fixed), pl.BlockSpec((1, KV_RANK), fixed),
                  pl.BlockSpec((Q_RANK, 2048), fixed), pl.BlockSpec((KV_RANK, 2048), fixed),
                  pl.BlockSpec((tm, 128), row), pl.BlockSpec((tm, 128), row),
                  pl.BlockSpec((tm, 128), row)],
        out_specs=[pl.BlockSpec((MLA_HEADS, tm, 256), lambda i: (0, i, 0)),
                   pl.BlockSpec((MLA_HEADS, tm, 256), lambda i: (0, i, 0)),
                   pl.BlockSpec((MLA_HEADS, tm, 128), lambda i: (0, i, 0)),
                   pl.BlockSpec((MLA_HEADS, 256, tm), lambda i: (0, 0, i)),
                   pl.BlockSpec((tm, Q_RANK), row), pl.BlockSpec((tm, KV_RANK), row)],
        out_shape=[_sds((MLA_HEADS, s, 256), MXU_DTYPE), _sds((MLA_HEADS, s, 256), MXU_DTYPE),
                   _sds((MLA_HEADS, s, 128), MXU_DTYPE), _sds((MLA_HEADS, 256, s), MXU_DTYPE),
                   _sds((s, Q_RANK), MXU_DTYPE), _sds((s, KV_RANK), MXU_DTYPE)],
        compiler_params=_params(1),
    )(proj, gq, gkv, w_q, w_kv, c, s1, s2)


def _raw_scores(q, k, t, diagonal):
    sc = _mm_nt(q, k)
    if diagonal:
        row = lax.broadcasted_iota(jnp.int32, (t, t), 0)
        col = lax.broadcasted_iota(jnp.int32, (t, t), 1)
        sc = jnp.where(col <= row, sc, NEG)
    return sc


def _scores_t(k, q, t, diagonal):
    sc = _mm_nt(k, q)
    if diagonal:
        key = lax.broadcasted_iota(jnp.int32, (t, t), 0)
        query = lax.broadcasted_iota(jnp.int32, (t, t), 1)
        sc = jnp.where(key <= query, sc, NEG)
    return sc


def _mla_fwd(qh, kh, vth, shards):
    s = qh.shape[1]
    t = min(MLA_TILE, s)
    nq = s // t
    na = len(shards)

    def body(*refs):
        q_ref, k_ref, vt_ref = refs[:3]
        o_ref, lse_ref = refs[3 + na:5 + na]
        m_ref, acc_ref = refs[5 + 2 * na:7 + 2 * na]
        h, i = pl.program_id(0), pl.program_id(1)
        if na:
            ex = _Gather(refs[3:3 + na], refs[5 + na:5 + 2 * na], refs[7 + 2 * na:])

            @pl.when((h == 0) & (i == 0))
            def _():
                ex.start()

            @pl.when((h == GATHER_FORWARD_HEAD) & (i == 0))
            def _():
                ex.forward()

        m_ref[...] = jnp.full(m_ref.shape, NEG, F32)
        acc_ref[...] = jnp.zeros(acc_ref.shape, F32)

        def step(j, diagonal):
            keys = pl.ds(pl.multiple_of(j * t, t), t)
            sc = _scores_t(k_ref[keys, :], q_ref[...], t, diagonal)
            m_prev = m_ref[...]
            m_new = jnp.maximum(m_prev, jnp.max(sc, axis=0, keepdims=True))
            alpha = jnp.exp2((m_prev - m_new) * MLA_C2)
            p = jnp.exp2((sc - m_new[0:1, :]) * MLA_C2)
            acc_ref[...] = alpha[0:1, :] * acc_ref[...] + _mm(vt_ref[:, keys], p)
            m_ref[...] = m_new

        def below_diagonal(j, carry):
            step(j, False)
            return carry

        lax.fori_loop(0, i, below_diagonal, 0)
        step(i, True)
        l = acc_ref[128:136, :]
        o_ref[...] = (acc_ref[0:128, :] / l[0:1, :]).T
        lse_ref[...] = m_ref[...] * MLA_C2 + jnp.log2(l)

        if na:
            @pl.when((h == MLA_HEADS - 1) & (i == nq - 1))
            def _():
                ex.finish()

    head = lambda h, i: (h, 0, 0)
    outs = pl.pallas_call(
        body, name="mla_fwd_gather" if na else "mla_fwd", grid=(MLA_HEADS, nq),
        in_specs=[pl.BlockSpec((None, t, 256), lambda h, i: (h, i, 0)),
                  pl.BlockSpec((None, s, 256), head),
                  pl.BlockSpec((None, 256, s), head)] + [HBM_SPEC] * na,
        out_specs=[pl.BlockSpec((t, 128), lambda h, i: (i, h)),
                   pl.BlockSpec((None, 8, t), lambda h, i: (h, 0, i))] + [HBM_SPEC] * na,
        out_shape=[_sds((s, 1024), F32), _sds((MLA_HEADS, 8, s), F32)]
        + _exchange_shapes(shards, True),
        scratch_shapes=[pltpu.VMEM((8, t), F32), pltpu.VMEM((256, t), F32)]
        + (_Exchange.semaphores(na) if na else []),
        compiler_params=_params(2),
    )(qh, kh, vth, *shards)
    return outs[0], outs[1], list(outs[2:])


def _silu_parts(g):
    sg = jax.nn.sigmoid(g)
    return g * sg, sg * (1.0 + g * (1.0 - sg))


def _out_proj(x, proj, swa, mla, w_out):
    s = x.shape[0]
    tm = min(512, s)

    def body(x_ref, ga_ref, gb_ref, a_ref, b_ref, w_ref, xo_ref, y_ref):
        sa, _ = _silu_parts(ga_ref[...].astype(F32))
        sb, _ = _silu_parts(gb_ref[...].astype(F32))
        y_ref[:, 0:1024] = (a_ref[...] * sa).astype(y_ref.dtype)
        y_ref[:, 1024:2048] = (b_ref[...] * sb).astype(y_ref.dtype)
        xo_ref[...] = x_ref[...] + jnp.dot(y_ref[...], w_ref[...], preferred_element_type=F32)

    row = lambda i: (i, 0)
    return pl.pallas_call(
        body, name="out_proj", grid=(s // tm,),
        in_specs=[pl.BlockSpec((tm, D_MODEL), row),
                  pl.BlockSpec((tm, 1024), lambda i: (i, GA // 1024)),
                  pl.BlockSpec((tm, 1024), lambda i: (i, GB // 1024)),
                  pl.BlockSpec((tm, 1024), row), pl.BlockSpec((tm, 1024), row),
                  pl.BlockSpec((D_MODEL, D_MODEL), lambda i: (0, 0), pipeline_mode=pl.Buffered(1))],
        out_specs=[pl.BlockSpec((tm, D_MODEL), row), pl.BlockSpec((tm, D_MODEL), row)],
        out_shape=[_sds((s, D_MODEL), F32), _sds((s, D_MODEL), MXU_DTYPE)],
        compiler_params=_params(1),
    )(x, proj, proj, swa, mla, w_out)


def _final_loss(x, tgt, g):
    s = x.shape[0]
    tm = min(512, s)

    def body(x_ref, t_ref, g_ref, dx_ref, dxb_ref, dg_ref, loss_ref):
        @pl.when(pl.program_id(0) == 0)
        def _():
            dg_ref[...] = jnp.zeros(dg_ref.shape, F32)
            loss_ref[...] = jnp.zeros(loss_ref.shape, F32)
        xh, r = _rownorm(x_ref[...])
        gg = g_ref[...]
        err = xh * gg - t_ref[...]
        per_row = jnp.mean(err * err, axis=-1, keepdims=True)
        loss_ref[...] += 0.5 * jnp.sum(per_row, axis=0, keepdims=True)
        dy = err * (1.0 / D_MODEL)
        dg_ref[...] += jnp.sum(dy * xh, axis=0, keepdims=True)
        dx = _rownorm_bwd(dy * gg, xh, r)
        dx_ref[...] = dx
        dxb_ref[...] = dx.astype(dxb_ref.dtype)

    row = lambda i: (i, 0)
    fixed = lambda i: (0, 0)
    return pl.pallas_call(
        body, name="final_loss", grid=(s // tm,),
        in_specs=[pl.BlockSpec((tm, D_MODEL), row), pl.BlockSpec((tm, D_MODEL), row),
                  pl.BlockSpec((1, D_MODEL), fixed)],
        out_specs=[pl.BlockSpec((tm, D_MODEL), row), pl.BlockSpec((tm, D_MODEL), row),
                   pl.BlockSpec((1, D_MODEL), fixed), pl.BlockSpec((8, 128), fixed)],
        out_shape=[_sds((s, D_MODEL), F32), _sds((s, D_MODEL), MXU_DTYPE), _sds((1, D_MODEL), F32),
                   _sds((8, 128), F32)],
        compiler_params=_params(1),
    )(x, tgt, g)


def _out_proj_bwd(dx, proj, swa, mla, w_out):
    s = dx.shape[0]
    tm = min(512, s)

    def body(dx_ref, ga_ref, gb_ref, a_ref, b_ref, w_ref, doa_ref, dob_ref, dg_ref, dlt_ref):
        dx = dx_ref[...].astype(MXU_DTYPE)
        dya = _mm_nt(dx, w_ref[0:1024, :])
        sa, dsa = _silu_parts(ga_ref[...].astype(F32))
        doa_ref[...] = dya * sa
        dg_ref[:, 0:1024] = (dya * a_ref[...] * dsa).astype(dg_ref.dtype)
        dyb = _mm_nt(dx, w_ref[1024:2048, :])
        sb, dsb = _silu_parts(gb_ref[...].astype(F32))
        b = b_ref[...]
        dob = dyb * sb
        dob_ref[...] = dob.astype(dob_ref.dtype)
        dg_ref[:, 1024:2048] = (dyb * b * dsb).astype(dg_ref.dtype)
        prod = dob * b
        for h in range(MLA_HEADS):
            dlt = jnp.sum(prod[:, 128 * h:128 * (h + 1)], axis=1, keepdims=True)
            dlt_ref[h] = jnp.broadcast_to(dlt, (tm, 128)).T[0:8, :]

    row = lambda i: (i, 0)
    return pl.pallas_call(
        body, name="out_proj_bwd", grid=(s // tm,),
        in_specs=[pl.BlockSpec((tm, D_MODEL), row),
                  pl.BlockSpec((tm, 1024), lambda i: (i, GA // 1024)),
                  pl.BlockSpec((tm, 1024), lambda i: (i, GB // 1024)),
                  pl.BlockSpec((tm, 1024), row), pl.BlockSpec((tm, 1024), row),
                  pl.BlockSpec((D_MODEL, D_MODEL), lambda i: (0, 0), pipeline_mode=pl.Buffered(1))],
        out_specs=[pl.BlockSpec((tm, 1024), row), pl.BlockSpec((tm, 1024), row),
                   pl.BlockSpec((tm, D_MODEL), row),
                   pl.BlockSpec((MLA_HEADS, 8, tm), lambda i: (0, 0, i))],
        out_shape=[_sds((s, 1024), F32), _sds((s, 1024), MXU_DTYPE), _sds((s, D_MODEL), MXU_DTYPE),
                   _sds((MLA_HEADS, 8, s), F32)],
        compiler_params=_params(1),
    )(dx, proj, proj, swa, mla, w_out)


def _matmul_tn(a, b, name):
    s, m = a.shape
    n = b.shape[1]
    tm, tn, tk = min(1024, m), min(1024, n), min(2048, s)
    nk = s // tk

    def body(a_ref, b_ref, o_ref, acc_ref):
        k = pl.program_id(2)

        @pl.when(k == 0)
        def _():
            acc_ref[...] = jnp.zeros(acc_ref.shape, F32)
        acc_ref[...] += _mm_tn(a_ref[...], b_ref[...])

        @pl.when(k == nk - 1)
        def _():
            o_ref[...] = acc_ref[...].astype(o_ref.dtype)

    return pl.pallas_call(
        body, name=name, grid=(m // tm, n // tn, nk),
        in_specs=[pl.BlockSpec((tk, tm), lambda i, j, k: (k, i)),
                  pl.BlockSpec((tk, tn), lambda i, j, k: (k, j))],
        out_specs=pl.BlockSpec((tm, tn), lambda i, j, k: (i, j)),
        out_shape=_sds((m, n), GRAD_DTYPE),
        scratch_shapes=[pltpu.VMEM((tm, tn), F32)],
        compiler_params=_params(3),
    )(a, b)


def _swa_bwd(proj, sinks, do, o):
    s = proj.shape[0]
    nb = s // BLOCK

    def body(sink_ref, q_ref, kp_ref, kc_ref, do_ref, o_ref, dq_ref, dkv_ref, dsink_ref,
             carry_ref, bias_ref):
        n = pl.program_id(0)
        _swa_fill_bias(n, bias_ref)

        @pl.when(n == 0)
        def _():
            carry_ref[...] = jnp.zeros(carry_ref.shape, F32)
            dsink_ref[...] = jnp.zeros(dsink_ref.shape, F32)

        @pl.when(n < nb)
        def _():
            (k_l, k_r), (v_l, v_r), left, _ = _swa_operands(kp_ref[...].astype(F32),
                                                            kc_ref[...].astype(F32))
            lane_s = lax.broadcasted_iota(jnp.int32, (8, 128), 1)
            dk_l, dk_r, dv_l, dv_r = [None, None], [None, None], [None, None], [None, None]
            dsink = jnp.zeros((8, 128), F32)
            for j in range(2):
                qs = _swa_stack(q_ref, j) * 0.125
                dos = _swa_stack(do_ref, j)
                prod_t = (dos * _swa_stack(o_ref, j)).T
                dlt = [jnp.sum(prod_t[0:64], axis=0, keepdims=True),
                       jnp.sum(prod_t[64:128], axis=0, keepdims=True)]
                dq = None
                for e in range(2):
                    kmat = (k_l, k_r)[e][j]
                    vmat = (v_l, v_r)[e][j]
                    ex, es, inv = _swa_softmax(qs, kmat, bias_ref[2 * j + e],
                                               _swa_sink_row(sink_ref, j, e))
                    p = ex * inv
                    ds = p * (_mm_nt(vmat, dos) - dlt[e])
                    sink_term = es * inv * dlt[e]
                    for a in range(4):
                        dsh = -jnp.sum(sink_term[:, 128 * a:128 * (a + 1)], axis=1, keepdims=True)
                        dsink = dsink + jnp.where(lane_s == _swa_head(j, a, e), dsh, 0.0)
                    dq_e = _mm_tn(ds, kmat)
                    dq = dq_e if dq is None else dq + dq_e
                    (dv_l, dv_r)[e][j] = _mm(p, dos)
                    (dk_l, dk_r)[e][j] = _mm(ds, qs)
                for a in range(4):
                    cols = slice(128 * (4 * j + a), 128 * (4 * j + a + 1))
                    dq_ref[:, cols] = (dq[128 * a:128 * (a + 1)] * 0.125).astype(dq_ref.dtype)

            def merge(t_l, t_r):
                head0 = t_l[0] + pltpu.roll(t_r[0], 64, 1)
                head1 = t_r[1] + pltpu.roll(t_l[1], 64, 1)
                return jnp.where(left, head0, head1)

            contrib = jnp.concatenate([merge(dk_l, dk_r), merge(dv_l, dv_r)], axis=1)
            dkv_ref[...] = (carry_ref[...] + contrib[0:BLOCK]).astype(dkv_ref.dtype)
            carry_ref[...] = contrib[BLOCK:2 * BLOCK]
            dsink_ref[...] += dsink

        @pl.when(n == nb)
        def _():
            dkv_ref[...] = carry_ref[...].astype(dkv_ref.dtype)

    cur = lambda n: (jnp.minimum(n, nb - 1), 0)
    return pl.pallas_call(
        body, name="swa_bwd", grid=(nb + 1,),
        in_specs=[pl.BlockSpec(memory_space=pltpu.SMEM),
                  pl.BlockSpec((BLOCK, 1024), cur),
                  pl.BlockSpec((BLOCK, 256), lambda n: (jnp.clip(n - 1, 0, nb - 1), KVA // 256)),
                  pl.BlockSpec((BLOCK, 256), lambda n: (jnp.minimum(n, nb - 1), KVA // 256)),
                  pl.BlockSpec((BLOCK, 1024), cur), pl.BlockSpec((BLOCK, 1024), cur)],
        out_specs=[pl.BlockSpec((BLOCK, 1024), cur),
                   pl.BlockSpec((BLOCK, 256), lambda n: (jnp.maximum(n - 1, 0), 0)),
                   pl.BlockSpec((8, 128), lambda n: (0, 0))],
        out_shape=[_sds((s, 1024), MXU_DTYPE), _sds((s, 256), MXU_DTYPE), _sds((8, 128), F32)],
        scratch_shapes=[pltpu.VMEM((BLOCK, 256), F32), pltpu.VMEM((4, 2 * BLOCK, SWA_STACK), F32)],
        compiler_params=_params(1),
    )(sinks, proj, proj, proj, do, o)


def _mla_bwd(qh, kh, vh, do, dlt, lse, blocks):
    s = qh.shape[1]
    t = min(MLA_TILE, s)
    nq = s // t

    na = len(blocks)

    def body(*refs):
        q_ref, k_ref, v_ref, do_ref, dlt_ref, lse_ref = refs[:6]
        dq_ref, dk_ref, dv_ref = refs[6 + na:9 + na]
        h, j = pl.program_id(0), pl.program_id(1)
        if na:
            ex = _Exchange(refs[6:6 + na], refs[9 + na:9 + 2 * na], refs[9 + 2 * na:], gather=False)

            @pl.when((h == 0) & (j == 0))
            def _():
                ex.start()

        def step(i, diagonal):
            rows = pl.ds(pl.multiple_of(i * t, t), t)
            q, k, dout = q_ref[rows, :], k_ref[...], do_ref[rows, :]
            sc = _scores_t(k, q, t, diagonal)
            p = jnp.exp2(sc * MLA_C2 - lse_ref[0:1, rows])
            dv = _mm(p, dout)
            ds = p * (_mm_nt(v_ref[...], dout) - dlt_ref[0:1, rows])
            dk = _mm(ds, q)
            dq = _mm_tn(ds, k)
            return rows, dq, dk, dv

        rows, dq, dk, dv = step(j, True)
        dk_ref[...] = dk
        dv_ref[...] = dv

        @pl.when(j == 0)
        def _():
            dq_ref[rows, :] = dq * MLA_SCALE

        @pl.when(j > 0)
        def _():
            dq_ref[rows, :] = (dq_ref[rows, :] + dq) * MLA_SCALE

        def above_diagonal(i, carry):
            rows, dq, dk, dv = step(i, False)
            dk_ref[...] += dk
            dv_ref[...] += dv

            @pl.when(j == 0)
            def _():
                dq_ref[rows, :] = dq

            @pl.when(j > 0)
            def _():
                dq_ref[rows, :] += dq
            return carry

        lax.fori_loop(j + 1, nq, above_diagonal, 0)
        dk_ref[...] *= MLA_SCALE

        if na:
            @pl.when((h == MLA_HEADS - 1) & (j == nq - 1))
            def _():
                ex.wait()

    head = lambda h, j: (h, 0, 0)
    kv_map = lambda h, j: (h, j, 0)
    outs = pl.pallas_call(
        body, name="mla_bwd_scatter" if na else "mla_bwd", grid=(MLA_HEADS, nq),
        in_specs=[pl.BlockSpec((None, s, 256), head), pl.BlockSpec((None, t, 256), kv_map),
                  pl.BlockSpec((None, t, 128), kv_map),
                  pl.BlockSpec((s, 128), lambda h, j: (0, h)),
                  pl.BlockSpec((None, 8, s), head), pl.BlockSpec((None, 8, s), head)]
        + [HBM_SPEC] * na,
        out_specs=[pl.BlockSpec((None, s, 256), head),
                   pl.BlockSpec((None, t, 256), kv_map), pl.BlockSpec((None, t, 128), kv_map)]
        + [HBM_SPEC] * na,
        out_shape=[_sds((MLA_HEADS, s, 256), F32), _sds((MLA_HEADS, s, 256), F32),
                   _sds((MLA_HEADS, s, 128), F32)] + _exchange_shapes(blocks, False),
        scratch_shapes=_Exchange.semaphores(na) if na else [],
        compiler_params=_params(2),
    )(qh, kh, vh, do, dlt, lse, *blocks)
    return outs[0], outs[1], outs[2], list(outs[3:])


def _mla_prep_bwd(dqh, dkh, dvh, proj, gq, gkv, w_q, w_kv, tabs):
    s = proj.shape[0]
    tm = min(256, s)
    c, s1, s2 = tabs

    def body(dq_ref, dk_ref, dv_ref, p_ref, gq_ref, gkv_ref, wq_ref, wkv_ref,
             c_ref, s1_ref, s2_ref, dp_ref, dqf_ref, dkvf_ref, dgq_ref, dgkv_ref):
        @pl.when(pl.program_id(0) == 0)
        def _():
            dgq_ref[...] = jnp.zeros(dgq_ref.shape, F32)
            dgkv_ref[...] = jnp.zeros(dgkv_ref.shape, F32)
        cc, ns1, ns2 = c_ref[...], -s1_ref[...], -s2_ref[...]
        dkr = jnp.zeros((tm, 128), F32)
        for h in range(MLA_HEADS):
            dqf_ref[:, 128 * h:128 * (h + 1)] = dq_ref[h, :, 0:128].astype(dqf_ref.dtype)
            dqf_ref[:, 1024 + 128 * h:1024 + 128 * (h + 1)] = _rope(
                dq_ref[h, :, 128:256], cc, ns1, ns2).astype(dqf_ref.dtype)
            dkvf_ref[:, 128 * h:128 * (h + 1)] = dk_ref[h, :, 0:128].astype(dkvf_ref.dtype)
            dkvf_ref[:, 1024 + 128 * h:1024 + 128 * (h + 1)] = dv_ref[h].astype(dkvf_ref.dtype)
            dkr = dkr + dk_ref[h, :, 128:256]
        dcqn = _mm_nt(dqf_ref[...], wq_ref[...])
        dckvn = _mm_nt(dkvf_ref[...], wkv_ref[...])
        cqh, rq = _rownorm(p_ref[:, 0:384].astype(F32))
        ckvh, rkv = _rownorm(p_ref[:, 384:640].astype(F32))
        dgq_ref[...] += jnp.sum(dcqn * cqh, axis=0, keepdims=True)
        dgkv_ref[...] += jnp.sum(dckvn * ckvh, axis=0, keepdims=True)
        dp_ref[:, 0:384] = _rownorm_bwd(dcqn * gq_ref[...], cqh, rq).astype(dp_ref.dtype)
        dp_ref[:, 384:640] = _rownorm_bwd(dckvn * gkv_ref[...], ckvh, rkv).astype(dp_ref.dtype)
        dp_ref[:, 640:768] = _rope(dkr, cc, ns1, ns2).astype(dp_ref.dtype)

    row = lambda i: (i, 0)
    fixed = lambda i: (0, 0)
    head = lambda i: (0, i, 0)
    return pl.pallas_call(
        body, name="mla_prep_bwd", grid=(s // tm,),
        in_specs=[pl.BlockSpec((MLA_HEADS, tm, 256), head), pl.BlockSpec((MLA_HEADS, tm, 256), head),
                  pl.BlockSpec((MLA_HEADS, tm, 128), head),
                  pl.BlockSpec((tm, 768), lambda i: (i, CQ // 768)),
                  pl.BlockSpec((1, Q_RANK), fixed), pl.BlockSpec((1, KV_RANK), fixed),
                  pl.BlockSpec((Q_RANK, 2048), fixed), pl.BlockSpec((KV_RANK, 2048), fixed),
                  pl.BlockSpec((tm, 128), row), pl.BlockSpec((tm, 128), row),
                  pl.BlockSpec((tm, 128), row)],
        out_specs=[pl.BlockSpec((tm, 768), row), pl.BlockSpec((tm, 2048), row),
                   pl.BlockSpec((tm, 2048), row),
                   pl.BlockSpec((1, Q_RANK), fixed), pl.BlockSpec((1, KV_RANK), fixed)],
        out_shape=[_sds((s, 768), MXU_DTYPE), _sds((s, 2048), MXU_DTYPE), _sds((s, 2048), MXU_DTYPE),
                   _sds((1, Q_RANK), F32), _sds((1, KV_RANK), F32)],
        compiler_params=_params(1),
    )(dqh, dkh, dvh, proj, gq, gkv, w_q, w_kv, c, s1, s2)


def _in_proj_bwd(dqa, dgate, dlat, dkva, w, x, dx_out, g, blocks):
    s = x.shape[0]
    tm = min(256, s)
    nm = s // tm
    na = len(blocks)

    def body(*refs):
        dqa_ref, dg8_ref, dlat_ref, dkva_ref, w_ref, x_ref, dxo_ref, g_ref = refs[:8]
        dx_ref, dxb_ref, dg_ref = refs[8 + na:11 + na]
        if na:
            ex = _Exchange(refs[8:8 + na], refs[11 + na:11 + 2 * na], refs[11 + 2 * na:], gather=False)

        @pl.when(pl.program_id(0) == 0)
        def _():
            dg_ref[...] = jnp.zeros(dg_ref.shape, F32)
            if na:
                ex.start()

        dh = (_mm_nt(dqa_ref[...], w_ref[:, QA:QA + 1024])
              + _mm_nt(dg8_ref[:, 0:1024], w_ref[:, GA:GA + 1024])
              + _mm_nt(dg8_ref[:, 1024:2048], w_ref[:, GB:GB + 1024])
              + _mm_nt(dlat_ref[...], w_ref[:, CQ:CQ + 768])
              + _mm_nt(dkva_ref[...], w_ref[:, KVA:KVA + 256]))
        xh, r = _rownorm(x_ref[...])
        dg_ref[...] += jnp.sum(dh * xh, axis=0, keepdims=True)
        dx = dxo_ref[...] + _rownorm_bwd(dh * g_ref[...], xh, r)
        dx_ref[...] = dx
        dxb_ref[...] = dx.astype(dxb_ref.dtype)

        if na:
            @pl.when(pl.program_id(0) == nm - 1)
            def _():
                ex.wait()

    row = lambda i: (i, 0)
    fixed = lambda i: (0, 0)
    outs = pl.pallas_call(
        body, name="in_proj_bwd_scatter" if na else "in_proj_bwd", grid=(nm,),
        in_specs=[pl.BlockSpec((tm, 1024), row), pl.BlockSpec((tm, 2048), row),
                  pl.BlockSpec((tm, 768), row), pl.BlockSpec((tm, 256), row),
                  pl.BlockSpec((D_MODEL, NP), fixed, pipeline_mode=pl.Buffered(1)),
                  pl.BlockSpec((tm, D_MODEL), row), pl.BlockSpec((tm, D_MODEL), row),
                  pl.BlockSpec((1, D_MODEL), fixed)] + [HBM_SPEC] * na,
        out_specs=[pl.BlockSpec((tm, D_MODEL), row), pl.BlockSpec((tm, D_MODEL), row),
                   pl.BlockSpec((1, D_MODEL), fixed)] + [HBM_SPEC] * na,
        out_shape=[_sds((s, D_MODEL), F32), _sds((s, D_MODEL), MXU_DTYPE), _sds((1, D_MODEL), F32)]
        + _exchange_shapes(blocks, False),
        scratch_shapes=_Exchange.semaphores(na) if na else [],
        compiler_params=_params(1),
    )(dqa, dgate, dlat, dkva, w, x, dx_out, g, *blocks)
    return outs[0], outs[1], outs[2], list(outs[3:])


def _reduce_adamw(parts, w, m, v, name):
    n_layers = len(parts)
    rows, cols = parts[0].shape[1:]
    lanes = -(-cols // 128) * 128
    tr = rows
    for cand in (1024, 512, 256, 128, 64, 32, 16, 8):
        if rows % cand == 0 and N_DEV * cand * lanes * 4 <= 8 * 1024 * 1024:
            tr = cand
            break
    nr = rows // tr

    def body(*refs):
        p_refs = refs[:n_layers]
        w_ref, m_ref, v_ref, g_ref, d_ref, nm_ref, nv_ref = refs[n_layers:]
        layer = pl.program_id(0)
        for l in range(n_layers):
            @pl.when(layer == l)
            def _(l=l):
                g = p_refs[l][0].astype(F32)
                for k in range(1, N_DEV):
                    g = g + p_refs[l][k].astype(F32)
                update(g, w_ref, m_ref, v_ref, g_ref, d_ref, nm_ref, nv_ref)

    def update(g, w_ref, m_ref, v_ref, g_ref, d_ref, nm_ref, nv_ref):
        m2 = ADAM_B1 * m_ref[...] + (1.0 - ADAM_B1) * g
        v2 = ADAM_B2 * v_ref[...] + (1.0 - ADAM_B2) * (g * g)
        m_hat = m2 / (1.0 - ADAM_B1 ** ADAM_STEP)
        v_hat = v2 / (1.0 - ADAM_B2 ** ADAM_STEP)
        g_ref[...] = g
        d_ref[...] = -ADAM_LR * (m_hat / (jnp.sqrt(v_hat) + ADAM_EPS) + ADAM_WD * w_ref[...])
        nm_ref[...] = m2
        nv_ref[...] = v2

    def part_spec(l):
        return pl.BlockSpec((N_DEV, tr, cols), lambda layer, i: (0, jnp.where(layer == l, i, 0), 0))

    blk = pl.BlockSpec((tr, cols), lambda layer, i: (layer * nr + i, 0))
    return pl.pallas_call(
        body, name=name, grid=(n_layers, nr),
        in_specs=[part_spec(l) for l in range(n_layers)] + [blk, blk, blk],
        out_specs=[blk, blk, blk, blk],
        out_shape=[_sds((n_layers * rows, cols), F32)] * 4,
        compiler_params=_params(2),
    )(*parts, w, m, v)


def _position():
    x, y, c = lax.axis_index("x"), lax.axis_index("y"), lax.axis_index("c")
    return x, y, c


def _index(px, py, pc):
    return 4 * px + 2 * py + pc


HBM_SPEC = pl.BlockSpec(memory_space=pltpu.HBM)


class _Gather:
    def __init__(self, srcs, dsts, sems):
        self.srcs, self.dsts = srcs, dsts
        self.send_sems, self.recv_sems, self.local_sems = sems
        x, y, c = _position()
        self.c = c
        self.me, self.sibling = (x, y, c), (x, y, 1 - c)
        self.chips = [(1 - x, y), (x, 1 - y), (1 - x, 1 - y)]

    def _copy(self, a, k, block, to, own=False):
        slot = self.dsts[a].at[_index(*block)]
        return pltpu.make_async_remote_copy(
            src_ref=self.srcs[a] if own else slot, dst_ref=slot,
            send_sem=self.send_sems.at[7 * a + k], recv_sem=self.recv_sems.at[7 * a + k],
            device_id=to, device_id_type=MESH)

    def _local(self, a):
        return pltpu.make_async_copy(self.srcs[a], self.dsts[a].at[_index(*self.me)],
                                     self.local_sems.at[a])

    def _first(self, a):
        return [self._copy(a, 0, self.me, self.sibling, own=True)] + [
            self._copy(a, 1 + j, self.me, (*chip, self.c), own=True)
            for j, chip in enumerate(self.chips)]

    def _passed(self, a, j):
        return self._copy(a, 4 + j, (*self.chips[j], self.c), self.sibling)

    def start(self):
        for a in range(len(self.srcs)):
            self._local(a).start()
            for cp in self._first(a):
                cp.start()

    def forward(self):
        for j, chip in enumerate(self.chips):
            for a in range(len(self.srcs)):
                self._copy(a, 1 + j, (*chip, self.c), self.me).wait_recv()
                self._passed(a, j).start()

    def finish(self):
        for a in range(len(self.srcs)):
            self._copy(a, 0, self.sibling, self.me).wait_recv()
            for j, chip in enumerate(self.chips):
                self._copy(a, 4 + j, (*chip, 1 - self.c), self.me).wait_recv()
            for cp in self._first(a):
                cp.wait_send()
            for j in range(3):
                self._passed(a, j).wait_send()
            self._local(a).wait()


def _all_gather(shards, name):
    na = len(shards)

    def body(*refs):
        g = _Gather(refs[:na], refs[na:2 * na], refs[2 * na:])
        g.start()
        g.forward()
        g.finish()

    return pl.pallas_call(
        body, name=name,
        in_specs=[HBM_SPEC] * na, out_specs=[HBM_SPEC] * na,
        out_shape=_exchange_shapes(shards, True),
        scratch_shapes=_Exchange.semaphores(na),
    )(*shards)


class _Exchange:
    def __init__(self, srcs, dsts, sems, gather):
        self.srcs, self.dsts, self.gather = srcs, dsts, gather
        self.send_sems, self.recv_sems, self.local_sems = sems
        x, y, c = _position()
        self.me = _index(x, y, c)
        self.peers = [(x ^ ((k >> 2) & 1), y ^ ((k >> 1) & 1), c ^ (k & 1)) for k in range(1, N_DEV)]

    @staticmethod
    def semaphores(na):
        return [pltpu.SemaphoreType.DMA((7 * na,)), pltpu.SemaphoreType.DMA((7 * na,)),
                pltpu.SemaphoreType.DMA((na,))]

    def _src(self, a, slot):
        return self.srcs[a] if self.gather else self.srcs[a].at[slot]

    def _local(self, a):
        return pltpu.make_async_copy(self._src(a, self.me), self.dsts[a].at[self.me],
                                     self.local_sems.at[a])

    def _send(self, a, k):
        peer = self.peers[k]
        return pltpu.make_async_remote_copy(
            src_ref=self._src(a, _index(*peer)), dst_ref=self.dsts[a].at[self.me],
            send_sem=self.send_sems.at[7 * a + k], recv_sem=self.recv_sems.at[7 * a + k],
            device_id=peer, device_id_type=MESH)

    def _arrival(self, a, k):
        landed = self.dsts[a].at[_index(*self.peers[k])]
        return pltpu.make_async_remote_copy(
            src_ref=landed, dst_ref=landed,
            send_sem=self.send_sems.at[7 * a + k], recv_sem=self.recv_sems.at[7 * a + k],
            device_id=self.peers[k], device_id_type=MESH)

    def start(self):
        for a in range(len(self.srcs)):
            self._local(a).start()
            for k in range(N_DEV - 1):
                self._send(a, k).start()

    def wait(self):
        for a in range(len(self.srcs)):
            for k in range(N_DEV - 1):
                self._arrival(a, k).wait_recv()
            for k in range(N_DEV - 1):
                self._send(a, k).wait_send()
            self._local(a).wait()


def _exchange_shapes(arrays, gather):
    return [_sds(((N_DEV,) + a.shape) if gather else a.shape, a.dtype) for a in arrays]


def _exchange_call(arrays, gather, name):
    na = len(arrays)

    def body(*refs):
        ex = _Exchange(refs[:na], refs[na:2 * na], refs[2 * na:], gather)
        ex.start()
        ex.wait()

    return pl.pallas_call(
        body, name=name,
        in_specs=[HBM_SPEC] * na, out_specs=[HBM_SPEC] * na,
        out_shape=_exchange_shapes(arrays, gather),
        scratch_shapes=_Exchange.semaphores(na),
    )(*arrays)


def _layer_fwd(x, small, weights, tabs, next_shards):
    attn_g, sinks, gq, gkv = small
    w_in, w_q, w_kv, w_o = weights
    proj, h = _in_proj(x, attn_g, w_in)
    swa = _swa_fwd(proj, sinks)
    qh, kh, vh, vth, cqn, ckvn = _mla_prep(proj, gq, gkv, w_q, w_kv, tabs)
    mla, lse, gathered = _mla_fwd(qh, kh, vth, next_shards)
    x_next, y = _out_proj(x, proj, swa, mla, w_o)
    return x_next, (x, proj, h, swa, qh, kh, vh, cqn, ckvn, mla, lse, y), gathered


def _layer_bwd(dx, dxb, saved, small, weights, tabs, pending, send_own):
    attn_g, sinks, gq, gkv = small
    w_in, w_q, w_kv, w_o = weights
    x, proj, h, swa, qh, kh, vh, cqn, ckvn, mla, lse, y = saved
    d_o = _matmul_tn(y, dxb, "grad_w_out")
    do_a, do_b, dgate, dlt = _out_proj_bwd(dxb, proj, swa, mla, w_o)
    dqa, dkva, dsk = _swa_bwd(proj, sinks, do_a, swa)
    dqh, dkh, dvh, received = _mla_bwd(qh, kh, vh, do_b, dlt, lse, pending)
    dlat, dqf, dkvf, d_gq, d_gkv = _mla_prep_bwd(dqh, dkh, dvh, proj, gq, gkv, w_q, w_kv, tabs)
    d_q = _matmul_tn(cqn, dqf, "grad_w_q_b")
    d_kv = _matmul_tn(ckvn, dkvf, "grad_w_kv_b")
    d_qa = _matmul_tn(h, dqa, "grad_w_in_qa")
    d_gate = _matmul_tn(h, dgate, "grad_w_in_gate")
    d_lat = _matmul_tn(h, dlat, "grad_w_in_lat")
    d_kva = _matmul_tn(h, dkva, "grad_w_in_kva")
    blocks = _grads_to_blocks(d_qa, d_gate, d_lat, d_kva, d_q, d_kv, d_o)
    dx, dxb, d_attn, received_own = _in_proj_bwd(dqa, dgate, dlat, dkva, w_in, x, dx, attn_g,
                                                 blocks if send_own else [])
    small_grads = (d_attn, dsk[0:1, 0:SWA_HEADS], d_gq, d_gkv)
    return dx, dxb, small_grads, blocks, received, received_own


def _pack_small_grads(small_grads, d_final, loss):
    d_attn, d_sink, d_gq, d_gkv = zip(*small_grads)
    return jnp.concatenate([
        jnp.concatenate(d_attn, axis=0).reshape(64, 128),
        jnp.concatenate(d_gq, axis=0).reshape(12, 128),
        jnp.concatenate(d_gkv, axis=0).reshape(8, 128),
        d_final.reshape(16, 128),
        jnp.pad(jnp.concatenate(d_sink, axis=1), ((0, 0), (0, 64))),
        loss[0:1],
        jnp.zeros((PACK_ROWS - ROW_LOSS - 1, 128), F32)], axis=0)


def _pack_small(attn, qa, kva, final, sinks):
    return jnp.concatenate([
        attn.reshape(64, 128), qa.reshape(12, 128), kva.reshape(8, 128), final.reshape(16, 128),
        jnp.pad(sinks.reshape(1, 64), ((0, 0), (0, 64))),
        jnp.zeros((PACK_ROWS - ROW_SINK - 1, 128), F32)], axis=0)


def _unpack_small(p):
    return (p[ROW_ATTN:ROW_QA].reshape(DEPTH, D_MODEL), p[ROW_SINK, 0:64].reshape(DEPTH, SWA_HEADS),
            p[ROW_QA:ROW_KVA].reshape(DEPTH, Q_RANK), p[ROW_KVA:ROW_FINAL].reshape(DEPTH, KV_RANK),
            p[ROW_FINAL:ROW_SINK].reshape(D_MODEL))


def kernel(x, attn_norm_g, w_in, swa_sinks, q_a_norm_g, kv_a_norm_g, w_q_b, w_kv_b, w_out, final_norm_g, loss_target, m_attn_norm_g, m_w_in, m_swa_sinks, m_q_a_norm_g, m_kv_a_norm_g, m_w_q_b, m_w_kv_b, m_w_out, m_final_norm_g, v_attn_norm_g, v_w_in, v_swa_sinks, v_q_a_norm_g, v_kv_a_norm_g, v_w_q_b, v_w_kv_b, v_w_out, v_final_norm_g):
    xs, tgt = x[0], loss_target[0]
    tabs = _rope_tables(xs.shape[0])
    shards = [w.astype(MXU_DTYPE) for w in (w_in, w_q_b, w_kv_b, w_out)]
    layer_shards = lambda l: [w[l] for w in shards]
    smalls = [(attn_norm_g[l:l + 1], swa_sinks[l], q_a_norm_g[l:l + 1], kv_a_norm_g[l:l + 1])
              for l in range(DEPTH)]

    gathered = _all_gather(layer_shards(0), "gather_weights")
    weights, saved = [], []
    for l in range(DEPTH):
        weights.append(_weights_from_gathered(*gathered))
        next_shards = layer_shards(l + 1) if l + 1 < DEPTH else []
        xs, acts, gathered = _layer_fwd(xs, smalls[l], weights[l], tabs, next_shards)
        saved.append(acts)
    dx, dxb, d_final, loss = _final_loss(xs, tgt, final_norm_g.reshape(1, D_MODEL))

    received, small_grads, pending = [None] * DEPTH, [None] * DEPTH, []
    for l in reversed(range(DEPTH)):
        dx, dxb, small_grads[l], blocks, arrived, arrived_own = _layer_bwd(
            dx, dxb, saved[l], smalls[l], weights[l], tabs, pending, send_own=(l == 0))
        if pending:
            received[l + 1] = arrived
        pending = blocks
    received[0] = arrived_own
    small = _exchange_call([_pack_small_grads(small_grads, d_final, loss)], True, "gather_small")[0]

    big = []
    for a, (w, m, v, name) in enumerate(zip((w_in, w_q_b, w_kv_b, w_out),
                                            (m_w_in, m_w_q_b, m_w_kv_b, m_w_out),
                                            (v_w_in, v_w_q_b, v_w_kv_b, v_w_out),
                                            ("adamw_w_in", "adamw_w_q_b", "adamw_w_kv_b",
                                             "adamw_w_out"))):
        cols = w.shape[-1]
        flat = lambda t: t.reshape(-1, cols)
        outs = _reduce_adamw([received[l][a] for l in range(DEPTH)], flat(w), flat(m), flat(v), name)
        big.append([t.reshape(w.shape) for t in outs])

    sm = _reduce_adamw(
        [small],
        _pack_small(attn_norm_g, q_a_norm_g, kv_a_norm_g, final_norm_g, swa_sinks),
        _pack_small(m_attn_norm_g, m_q_a_norm_g, m_kv_a_norm_g, m_final_norm_g, m_swa_sinks),
        _pack_small(v_attn_norm_g, v_q_a_norm_g, v_kv_a_norm_g, v_final_norm_g, v_swa_sinks),
        "adamw_small")
    loss = sm[0][ROW_LOSS, 0]
    kinds = []
    for t in range(4):
        attn, sinks, qa, kva, final = _unpack_small(sm[t])
        b_in, b_q, b_kv, b_o = (big[i][t] for i in range(4))
        kinds.append((attn, b_in, sinks, qa, kva, b_q, b_kv, b_o, final))
    return (loss, dx[None], *kinds[0], *kinds[1], *kinds[2], *kinds[3])
```

```python
import functools

import jax
import jax.numpy as jnp
from jax import lax
from jax.experimental import pallas as pl
from jax.experimental.pallas import tpu as pltpu

F32 = jnp.float32
BF16 = jnp.bfloat16
MXU_DTYPE = BF16
GRAD_DTYPE = BF16
PROJ_DTYPE = BF16

D_MODEL = 2048
DEPTH = 4
EPS = 1e-6
NEG = -1e30
BLOCK = 128
SWA_HEADS = 16
MLA_HEADS = 8
Q_RANK = 384
KV_RANK = 256
MLA_SCALE = 192 ** -0.5
MLA_C2 = MLA_SCALE * 1.4426950408889634
ROPE_THETA = 10000.0
IN_WIDTH = 4032

ADAM_LR = 0.001
ADAM_B1 = 0.9
ADAM_B2 = 0.999
ADAM_EPS = 1e-08
ADAM_WD = 0.01
ADAM_STEP = 10

N_DEV = 8
MESH = pl.DeviceIdType.MESH

NP = 4096
QA, GA, GB, CQ, CKV, KR, KVA = 0, 1024, 2048, 3072, 3456, 3712, 3840

ROW_ATTN, ROW_QA, ROW_KVA, ROW_FINAL, ROW_SINK, ROW_LOSS, PACK_ROWS = 0, 64, 76, 84, 100, 101, 104

VMEM_LIMIT = 56 * 1024 * 1024
MLA_TILE = 512
GATHER_FORWARD_HEAD = 5


def _sds(shape, dtype):
    return jax.ShapeDtypeStruct(shape, dtype)


def _params(n_axes):
    return pltpu.CompilerParams(dimension_semantics=("arbitrary",) * n_axes,
                                vmem_limit_bytes=VMEM_LIMIT)


def _mm(a, b):
    return jnp.dot(a.astype(MXU_DTYPE), b.astype(MXU_DTYPE), preferred_element_type=F32)


def _mm_nt(a, b):
    return lax.dot_general(a.astype(MXU_DTYPE), b.astype(MXU_DTYPE),
                           (((1,), (1,)), ((), ())), preferred_element_type=F32)


def _mm_tn(a, b):
    return lax.dot_general(a.astype(MXU_DTYPE), b.astype(MXU_DTYPE),
                           (((0,), (0,)), ((), ())), preferred_element_type=F32)


def _rownorm(x):
    r = lax.rsqrt(jnp.mean(x * x, axis=-1, keepdims=True) + EPS)
    return x * r, r


def _rownorm_bwd(dxh, xh, r):
    return r * (dxh - xh * jnp.mean(dxh * xh, axis=-1, keepdims=True))


def _rope(t, c, s1, s2):
    return t * c + pltpu.roll(t, 32, 1) * s1 + pltpu.roll(t, 96, 1) * s2


def _pad_in_cols(w):
    z = jnp.zeros(w.shape[:-1] + (64,), w.dtype)
    return jnp.concatenate([w[..., 0:1024], w[..., 1280:2304], w[..., 3008:4032], w[..., 2304:2688],
                            w[..., 2688:2944], w[..., 2944:3008], z, w[..., 1024:1152],
                            w[..., 1152:1280]], axis=-1)


def _w_in_from_gathered(g_in):
    return _pad_in_cols(g_in.transpose(1, 0, 2).reshape(D_MODEL, IN_WIDTH))


def _rest_from_gathered(g_qb, g_kvb, g_out):
    qb = g_qb.transpose(1, 0, 2)
    rope = jnp.pad(qb[..., 128:], ((0, 0), (0, 0), (0, 64)))
    w_q = jnp.concatenate([qb[..., :128].reshape(Q_RANK, 1024),
                           rope.reshape(Q_RANK, 1024)], axis=-1)
    kvb = g_kvb.transpose(1, 0, 2)
    w_kv = jnp.concatenate([kvb[..., :128].reshape(KV_RANK, 1024),
                            kvb[..., 128:].reshape(KV_RANK, 1024)], axis=-1)
    w_o = g_out.reshape(D_MODEL, D_MODEL)
    return w_q, w_kv, w_o


def _in_grad_blocks(d_inp):
    d_in = jnp.concatenate([d_inp[:, QA:QA + 1024], d_inp[:, KVA:KVA + 256], d_inp[:, GA:GA + 1024],
                            d_inp[:, CQ:CQ + 704], d_inp[:, GB:GB + 1024]], axis=1)
    return d_in.reshape(D_MODEL, N_DEV, 504).transpose(1, 0, 2)


def _rest_grad_blocks(d_q, d_kv, d_o):
    qn = d_q[:, :1024].reshape(Q_RANK, 8, 128)
    qr = d_q[:, 1024:].reshape(Q_RANK, 8, 128)[..., :64]
    b_q = jnp.concatenate([qn, qr], axis=-1).transpose(1, 0, 2)
    kn = d_kv[:, :1024].reshape(KV_RANK, 8, 128)
    vv = d_kv[:, 1024:].reshape(KV_RANK, 8, 128)
    b_kv = jnp.concatenate([kn, vv], axis=-1).transpose(1, 0, 2)
    b_o = d_o.reshape(N_DEV, 256, D_MODEL)
    return [b_q, b_kv, b_o]


def _rope_tables(s):
    pos = jnp.arange(s, dtype=F32)
    inv_freq = ROPE_THETA ** (-jnp.arange(0, 64, 2, dtype=F32) / 64)
    ang = pos[:, None] * inv_freq[None, :]
    cos, sin = jnp.cos(ang), jnp.sin(ang)
    z32 = jnp.zeros((s, 32), F32)
    z64 = jnp.zeros((s, 64), F32)
    c = jnp.concatenate([cos, cos, z64], axis=1)
    s1 = jnp.concatenate([z32, sin, z64], axis=1)
    s2 = jnp.concatenate([-sin, z32, z64], axis=1)
    return c, s1, s2


def _in_proj(x, g, w, shards):
    s = x.shape[0]
    tm, tn = min(512, s), 1024
    nm = s // tm
    na = len(shards)

    def body(*refs):
        x_ref, g_ref, w_ref = refs[:3]
        o_ref, h_ref = refs[3 + na:5 + na]
        i = pl.program_id(0)
        if na:
            ex = _Gather(refs[3:3 + na], refs[5 + na:5 + 2 * na], refs[5 + 2 * na:])

            @pl.when(i == 0)
            def _():
                ex.start()

            @pl.when(i == nm // 2)
            def _():
                ex.forward()

        xh, _ = _rownorm(x_ref[...])
        h_ref[...] = (xh * g_ref[...]).astype(h_ref.dtype)
        for j in range(NP // tn):
            cols = slice(j * tn, (j + 1) * tn)
            o_ref[:, cols] = jnp.dot(h_ref[...], w_ref[:, cols],
                                     preferred_element_type=F32).astype(o_ref.dtype)

        if na:
            @pl.when(i == nm - 1)
            def _():
                ex.finish()

    row = lambda i: (i, 0)
    fixed = lambda i: (0, 0)
    outs = pl.pallas_call(
        body, name="in_proj_gather" if na else "in_proj", grid=(nm,),
        in_specs=[pl.BlockSpec((tm, D_MODEL), row), pl.BlockSpec((1, D_MODEL), fixed),
                  pl.BlockSpec((D_MODEL, NP), fixed, pipeline_mode=pl.Buffered(1))]
        + [HBM_SPEC] * na,
        out_specs=[pl.BlockSpec((tm, NP), row), pl.BlockSpec((tm, D_MODEL), row)]
        + [HBM_SPEC] * na,
        out_shape=[_sds((s, NP), PROJ_DTYPE), _sds((s, D_MODEL), MXU_DTYPE)]
        + _exchange_shapes(shards, True),
        scratch_shapes=_Exchange.semaphores(na) if na else [],
        compiler_params=_params(1),
    )(x, g, w, *shards)
    return outs[0], outs[1], list(outs[2:])


def _swa_slopes():
    return [2.0 ** (-8.0 * (h + 1) / SWA_HEADS) for h in range(SWA_HEADS)]


def _swa_operands(kv_p, kv_c):
    kk = jnp.concatenate([kv_p[:, :128], kv_c[:, :128]], axis=0)
    vv = jnp.concatenate([kv_p[:, 128:], kv_c[:, 128:]], axis=0)
    left = lax.broadcasted_iota(jnp.int32, (2 * BLOCK, 128), 1) < 64

    def split(t):
        lo = jnp.where(left, t, 0.0)
        hi = jnp.where(left, 0.0, t)
        lefts = [lo, pltpu.roll(hi, 64, 1)]
        rights = [pltpu.roll(lo, 64, 1), hi]
        return lefts, rights

    return split(kk), split(vv), left, vv


SWA_STACK = 4 * BLOCK


def _swa_head(j, a, e):
    return 2 * (4 * j + a) + e


def _swa_bias(n, j, e):
    slopes = _swa_slopes()
    ki = lax.broadcasted_iota(jnp.int32, (2 * BLOCK, SWA_STACK), 0)
    r = lax.broadcasted_iota(jnp.int32, (2 * BLOCK, SWA_STACK), 1)
    a = r >> 7
    delta = BLOCK + (r & (BLOCK - 1)) - ki
    valid = (delta >= 0) & (delta < BLOCK) & ((n - 1) * BLOCK + ki >= 0)
    sl = [slopes[_swa_head(j, t, e)] for t in range(4)]
    slope = jnp.where(a == 0, sl[0], jnp.where(a == 1, sl[1], jnp.where(a == 2, sl[2], sl[3])))
    return jnp.where(valid, -slope * delta.astype(F32), NEG)


def _swa_fill_bias(n, bias_ref):
    @pl.when(n <= 1)
    def _():
        for j in range(2):
            for e in range(2):
                bias_ref[2 * j + e] = _swa_bias(n, j, e)


def _swa_sink_row(sink_ref, j, e):
    a = lax.broadcasted_iota(jnp.int32, (1, SWA_STACK), 1) >> 7
    sk = [sink_ref[_swa_head(j, t, e)] for t in range(4)]
    return jnp.where(a == 0, sk[0], jnp.where(a == 1, sk[1], jnp.where(a == 2, sk[2], sk[3])))


def _swa_stack(ref, j):
    return jnp.concatenate([ref[:, 128 * (4 * j + a):128 * (4 * j + a + 1)] for a in range(4)], axis=0)


def _swa_softmax(qs, kmat, bias, sink):
    sc = _mm_nt(kmat, qs) + bias
    m = jnp.maximum(jnp.max(sc, axis=0, keepdims=True), sink)
    ex = jnp.exp(sc - m)
    es = jnp.exp(sink - m)
    return ex, es, 1.0 / (jnp.sum(ex, axis=0, keepdims=True) + es)


def _swa_values_t(vv):
    vt = vv.T
    top = lax.broadcasted_iota(jnp.int32, (128, 2 * BLOCK), 0) < 64
    lo = jnp.where(top, vt, 0.0)
    hi = jnp.where(top, 0.0, vt)
    return [lo, pltpu.roll(hi, 64, 0)], [pltpu.roll(lo, 64, 0), hi]


def _swa_fwd(proj, sinks):
    s = proj.shape[0]
    nb = s // BLOCK

    def body(sink_ref, q_ref, kp_ref, kc_ref, o_ref, bias_ref):
        n = pl.program_id(0)
        _swa_fill_bias(n, bias_ref)
        (k_l, k_r), _, _, vv = _swa_operands(kp_ref[...].astype(F32), kc_ref[...].astype(F32))
        vt_top, vt_bottom = _swa_values_t(vv)
        for j in range(2):
            qs = _swa_stack(q_ref, j) * 0.125
            out_t = None
            for e in range(2):
                ex, _, inv = _swa_softmax(qs, (k_l, k_r)[e][j], bias_ref[2 * j + e],
                                          _swa_sink_row(sink_ref, j, e))
                o_t = _mm((vt_top, vt_bottom)[e][j], ex) * inv
                out_t = o_t if out_t is None else out_t + o_t
            out = out_t.T
            for a in range(4):
                o_ref[:, 128 * (4 * j + a):128 * (4 * j + a + 1)] = out[128 * a:128 * (a + 1)]

    return pl.pallas_call(
        body, name="swa_fwd", grid=(nb,),
        in_specs=[pl.BlockSpec(memory_space=pltpu.SMEM),
                  pl.BlockSpec((BLOCK, 1024), lambda n: (n, 0)),
                  pl.BlockSpec((BLOCK, 256), lambda n: (jnp.maximum(n - 1, 0), KVA // 256)),
                  pl.BlockSpec((BLOCK, 256), lambda n: (n, KVA // 256))],
        out_specs=pl.BlockSpec((BLOCK, 1024), lambda n: (n, 0)),
        out_shape=_sds((s, 1024), F32),
        scratch_shapes=[pltpu.VMEM((4, 2 * BLOCK, SWA_STACK), F32)],
        compiler_params=_params(1),
    )(sinks, proj, proj, proj)


def _mla_prep(proj, gq, gkv, w_q, w_kv, tabs):
    s = proj.shape[0]
    tm = min(512, s)
    c, s1, s2 = tabs

    def body(p_ref, gq_ref, gkv_ref, wq_ref, wkv_ref, c_ref, s1_ref, s2_ref,
             q_ref, k_ref, v_ref, vt_ref):
        cqh, _ = _rownorm(p_ref[:, 0:384].astype(F32))
        ckvh, _ = _rownorm(p_ref[:, 384:640].astype(F32))
        q = _mm(cqh * gq_ref[...], wq_ref[...])
        kv = _mm(ckvh * gkv_ref[...], wkv_ref[...])
        cc, ss1, ss2 = c_ref[...], s1_ref[...], s2_ref[...]
        krr = _rope(p_ref[:, 640:768].astype(F32), cc, ss1, ss2).astype(k_ref.dtype)
        for h in range(MLA_HEADS):
            q_ref[h, :, 0:128] = q[:, 128 * h:128 * (h + 1)].astype(q_ref.dtype)
            q_ref[h, :, 128:256] = _rope(q[:, 1024 + 128 * h:1024 + 128 * (h + 1)],
                                         cc, ss1, ss2).astype(q_ref.dtype)
            k_ref[h, :, 0:128] = kv[:, 128 * h:128 * (h + 1)].astype(k_ref.dtype)
            k_ref[h, :, 128:256] = krr
            vv = kv[:, 1024 + 128 * h:1024 + 128 * (h + 1)]
            v_ref[h] = vv.astype(v_ref.dtype)
            vt_ref[h, 0:128, :] = vv.T.astype(vt_ref.dtype)
            vt_ref[h, 128:256, :] = jnp.ones((128, tm), vt_ref.dtype)

    row = lambda i: (i, 0)
    fixed = lambda i: (0, 0)
    return pl.pallas_call(
        body, name="mla_prep", grid=(s // tm,),
        in_specs=[pl.BlockSpec((tm, 768), lambda i: (i, CQ // 768)),
                  pl.BlockSpec((1, Q_RANK), fixed), pl.BlockSpec((1, KV_RANK), fixed),
                  pl.BlockSpec((Q_RANK, 2048), fixed), pl.BlockSpec((KV_RANK, 2048), fixed),
                  pl.BlockSpec((tm, 128), row), pl.BlockSpec((tm, 128), row),
                  pl.BlockSpec((tm, 128), row)],
        out_specs=[pl.BlockSpec((MLA_HEADS, tm, 256), lambda i: (0, i, 0)),
                   pl.BlockSpec((MLA_HEADS, tm, 256), lambda i: (0, i, 0)),
                   pl.BlockSpec((MLA_HEADS, tm, 128), lambda i: (0, i, 0)),
                   pl.BlockSpec((MLA_HEADS, 256, tm), lambda i: (0, 0, i))],
        out_shape=[_sds((MLA_HEADS, s, 256), MXU_DTYPE), _sds((MLA_HEADS, s, 256), MXU_DTYPE),
                   _sds((MLA_HEADS, s, 128), MXU_DTYPE), _sds((MLA_HEADS, 256, s), MXU_DTYPE)],
        compiler_params=_params(1),
    )(proj, gq, gkv, w_q, w_kv, c, s1, s2)


def _scores_t(k, q, t, diagonal):
    sc = _mm_nt(k, q)
    if diagonal:
        key = lax.broadcasted_iota(jnp.int32, (t, t), 0)
        query = lax.broadcasted_iota(jnp.int32, (t, t), 1)
        sc = jnp.where(key <= query, sc, NEG)
    return sc


def _mla_fwd(qh, kh, vth, shards):
    s = qh.shape[1]
    t = min(MLA_TILE, s)
    nq = s // t
    na = len(shards)

    def body(*refs):
        q_ref, k_ref, vt_ref = refs[:3]
        o_ref, lse_ref = refs[3 + na:5 + na]
        m_ref, acc_ref = refs[5 + 2 * na:7 + 2 * na]
        h, i = pl.program_id(0), pl.program_id(1)
        if na:
            ex = _Gather(refs[3:3 + na], refs[5 + na:5 + 2 * na], refs[7 + 2 * na:])

            @pl.when((h == 0) & (i == 0))
            def _():
                ex.start()

            @pl.when((h == GATHER_FORWARD_HEAD) & (i == 0))
            def _():
                ex.forward()

        m_ref[...] = jnp.full(m_ref.shape, NEG, F32)
        acc_ref[...] = jnp.zeros(acc_ref.shape, F32)

        def step(j, diagonal):
            keys = pl.ds(pl.multiple_of(j * t, t), t)
            sc = _scores_t(k_ref[keys, :], q_ref[...], t, diagonal)
            m_prev = m_ref[...]
            m_new = jnp.maximum(m_prev, jnp.max(sc, axis=0, keepdims=True))
            alpha = jnp.exp2((m_prev - m_new) * MLA_C2)
            p = jnp.exp2((sc - m_new[0:1, :]) * MLA_C2)
            acc_ref[...] = alpha[0:1, :] * acc_ref[...] + _mm(vt_ref[:, keys], p)
            m_ref[...] = m_new

        def below_diagonal(j, carry):
            step(j, False)
            return carry

        lax.fori_loop(0, i, below_diagonal, 0)
        step(i, True)
        l = acc_ref[128:136, :]
        o_ref[...] = (acc_ref[0:128, :] / l[0:1, :]).T
        lse_ref[...] = m_ref[...] * MLA_C2 + jnp.log2(l)

        if na:
            @pl.when((h == MLA_HEADS - 1) & (i == nq - 1))
            def _():
                ex.finish()

    head = lambda h, i: (h, 0, 0)
    outs = pl.pallas_call(
        body, name="mla_fwd_gather" if na else "mla_fwd", grid=(MLA_HEADS, nq),
        in_specs=[pl.BlockSpec((None, t, 256), lambda h, i: (h, i, 0)),
                  pl.BlockSpec((None, s, 256), head),
                  pl.BlockSpec((None, 256, s), head)] + [HBM_SPEC] * na,
        out_specs=[pl.BlockSpec((t, 128), lambda h, i: (i, h)),
                   pl.BlockSpec((None, 8, t), lambda h, i: (h, 0, i))] + [HBM_SPEC] * na,
        out_shape=[_sds((s, 1024), F32), _sds((MLA_HEADS, 8, s), F32)]
        + _exchange_shapes(shards, True),
        scratch_shapes=[pltpu.VMEM((8, t), F32), pltpu.VMEM((256, t), F32)]
        + (_Exchange.semaphores(na) if na else []),
        compiler_params=_params(2),
    )(qh, kh, vth, *shards)
    return outs[0], outs[1], list(outs[2:])


def _silu_parts(g):
    sg = jax.nn.sigmoid(g)
    return g * sg, sg * (1.0 + g * (1.0 - sg))


def _out_proj(x, proj, swa, mla, w_out):
    s = x.shape[0]
    tm = min(512, s)

    def body(x_ref, ga_ref, gb_ref, a_ref, b_ref, w_ref, xo_ref, y_ref):
        sa, _ = _silu_parts(ga_ref[...].astype(F32))
        sb, _ = _silu_parts(gb_ref[...].astype(F32))
        y_ref[:, 0:1024] = (a_ref[...] * sa).astype(y_ref.dtype)
        y_ref[:, 1024:2048] = (b_ref[...] * sb).astype(y_ref.dtype)
        xo_ref[...] = x_ref[...] + jnp.dot(y_ref[...], w_ref[...], preferred_element_type=F32)

    row = lambda i: (i, 0)
    return pl.pallas_call(
        body, name="out_proj", grid=(s // tm,),
        in_specs=[pl.BlockSpec((tm, D_MODEL), row),
                  pl.BlockSpec((tm, 1024), lambda i: (i, GA // 1024)),
                  pl.BlockSpec((tm, 1024), lambda i: (i, GB // 1024)),
                  pl.BlockSpec((tm, 1024), row), pl.BlockSpec((tm, 1024), row),
                  pl.BlockSpec((D_MODEL, D_MODEL), lambda i: (0, 0), pipeline_mode=pl.Buffered(1))],
        out_specs=[pl.BlockSpec((tm, D_MODEL), row), pl.BlockSpec((tm, D_MODEL), row)],
        out_shape=[_sds((s, D_MODEL), F32), _sds((s, D_MODEL), MXU_DTYPE)],
        compiler_params=_params(1),
    )(x, proj, proj, swa, mla, w_out)


def _final_loss(x, tgt, g):
    s = x.shape[0]
    tm = min(512, s)

    def body(x_ref, t_ref, g_ref, dx_ref, dxb_ref, dg_ref, loss_ref):
        @pl.when(pl.program_id(0) == 0)
        def _():
            dg_ref[...] = jnp.zeros(dg_ref.shape, F32)
            loss_ref[...] = jnp.zeros(loss_ref.shape, F32)
        xh, r = _rownorm(x_ref[...])
        gg = g_ref[...]
        err = xh * gg - t_ref[...]
        per_row = jnp.mean(err * err, axis=-1, keepdims=True)
        loss_ref[...] += 0.5 * jnp.sum(per_row, axis=0, keepdims=True)
        dy = err * (1.0 / D_MODEL)
        dg_ref[...] += jnp.sum(dy * xh, axis=0, keepdims=True)
        dx = _rownorm_bwd(dy * gg, xh, r)
        dx_ref[...] = dx
        dxb_ref[...] = dx.astype(dxb_ref.dtype)

    row = lambda i: (i, 0)
    fixed = lambda i: (0, 0)
    return pl.pallas_call(
        body, name="final_loss", grid=(s // tm,),
        in_specs=[pl.BlockSpec((tm, D_MODEL), row), pl.BlockSpec((tm, D_MODEL), row),
                  pl.BlockSpec((1, D_MODEL), fixed)],
        out_specs=[pl.BlockSpec((tm, D_MODEL), row), pl.BlockSpec((tm, D_MODEL), row),
                   pl.BlockSpec((1, D_MODEL), fixed), pl.BlockSpec((8, 128), fixed)],
        out_shape=[_sds((s, D_MODEL), F32), _sds((s, D_MODEL), MXU_DTYPE), _sds((1, D_MODEL), F32),
                   _sds((8, 128), F32)],
        compiler_params=_params(1),
    )(x, tgt, g)


def _out_proj_bwd(dx, proj, swa, mla, w_out):
    s = dx.shape[0]
    tm = min(512, s)

    def body(dx_ref, ga_ref, gb_ref, a_ref, b_ref, w_ref, doa_ref, dob_ref, dg_ref, dlt_ref):
        dx = dx_ref[...].astype(MXU_DTYPE)
        dya = _mm_nt(dx, w_ref[0:1024, :])
        sa, dsa = _silu_parts(ga_ref[...].astype(F32))
        doa_ref[...] = dya * sa
        dg_ref[:, 0:1024] = (dya * a_ref[...] * dsa).astype(dg_ref.dtype)
        dyb = _mm_nt(dx, w_ref[1024:2048, :])
        sb, dsb = _silu_parts(gb_ref[...].astype(F32))
        b = b_ref[...]
        dob = dyb * sb
        dob_ref[...] = dob.astype(dob_ref.dtype)
        dg_ref[:, 1024:2048] = (dyb * b * dsb).astype(dg_ref.dtype)
        prod = dob * b
        for h in range(MLA_HEADS):
            dlt = jnp.sum(prod[:, 128 * h:128 * (h + 1)], axis=1, keepdims=True)
            dlt_ref[h] = jnp.broadcast_to(dlt, (tm, 128)).T[0:8, :]

    row = lambda i: (i, 0)
    return pl.pallas_call(
        body, name="out_proj_bwd", grid=(s // tm,),
        in_specs=[pl.BlockSpec((tm, D_MODEL), row),
                  pl.BlockSpec((tm, 1024), lambda i: (i, GA // 1024)),
                  pl.BlockSpec((tm, 1024), lambda i: (i, GB // 1024)),
                  pl.BlockSpec((tm, 1024), row), pl.BlockSpec((tm, 1024), row),
                  pl.BlockSpec((D_MODEL, D_MODEL), lambda i: (0, 0), pipeline_mode=pl.Buffered(1))],
        out_specs=[pl.BlockSpec((tm, 1024), row), pl.BlockSpec((tm, 1024), row),
                   pl.BlockSpec((tm, D_MODEL), row),
                   pl.BlockSpec((MLA_HEADS, 8, tm), lambda i: (0, 0, i))],
        out_shape=[_sds((s, 1024), F32), _sds((s, 1024), MXU_DTYPE), _sds((s, D_MODEL), MXU_DTYPE),
                   _sds((MLA_HEADS, 8, s), F32)],
        compiler_params=_params(1),
    )(dx, proj, proj, swa, mla, w_out)


def _matmul_tn(a, b, name):
    s, m = a.shape
    n = b.shape[1]
    tm, tn, tk = min(1024, m), min(1024, n), min(2048, s)
    nk = s // tk

    def body(a_ref, b_ref, o_ref, acc_ref):
        k = pl.program_id(2)

        @pl.when(k == 0)
        def _():
            acc_ref[...] = jnp.zeros(acc_ref.shape, F32)
        acc_ref[...] += _mm_tn(a_ref[...], b_ref[...])

        @pl.when(k == nk - 1)
        def _():
            o_ref[...] = acc_ref[...].astype(o_ref.dtype)

    return pl.pallas_call(
        body, name=name, grid=(m // tm, n // tn, nk),
        in_specs=[pl.BlockSpec((tk, tm), lambda i, j, k: (k, i)),
                  pl.BlockSpec((tk, tn), lambda i, j, k: (k, j))],
        out_specs=pl.BlockSpec((tm, tn), lambda i, j, k: (i, j)),
        out_shape=_sds((m, n), GRAD_DTYPE),
        scratch_shapes=[pltpu.VMEM((tm, tn), F32)],
        compiler_params=_params(3),
    )(a, b)


def _grad_w_in(h, dqa, dgate, dlat, dkva, blocks):
    s = h.shape[0]
    tm, tn, tk = 1024, 1024, min(2048, s)
    nk = s // tk
    grid = (D_MODEL // tm, NP // tn, nk)
    na = len(blocks)

    def body(*refs):
        a_ref, dqa_ref, dg8_ref, dlat_ref, dkva_ref = refs[:5]
        o_ref = refs[5 + na]
        acc_ref = refs[6 + 2 * na]
        i, j, k = pl.program_id(0), pl.program_id(1), pl.program_id(2)
        if na:
            ex = _Exchange(refs[5:5 + na], refs[6 + na:6 + 2 * na], refs[7 + 2 * na:], gather=False)

            @pl.when((i == 0) & (j == 0) & (k == 0))
            def _():
                ex.start()

        @pl.when(k == 0)
        def _():
            acc_ref[...] = jnp.zeros(acc_ref.shape, F32)

        @pl.when(j == QA // tn)
        def _():
            acc_ref[...] += _mm_tn(a_ref[...], dqa_ref[...])

        @pl.when((j == GA // tn) | (j == GB // tn))
        def _():
            acc_ref[...] += _mm_tn(a_ref[...], dg8_ref[...])

        @pl.when(j == CQ // tn)
        def _():
            acc_ref[:, 0:768] += _mm_tn(a_ref[...], dlat_ref[...])
            acc_ref[:, 768:1024] += _mm_tn(a_ref[...], dkva_ref[...])

        @pl.when(k == nk - 1)
        def _():
            o_ref[...] = acc_ref[...].astype(o_ref.dtype)

        if na:
            @pl.when((i == grid[0] - 1) & (j == grid[1] - 1) & (k == nk - 1))
            def _():
                ex.wait()

    def when(group, width):
        return pl.BlockSpec((tk, width), lambda i, j, k: (jnp.where(j == group, k, 0), 0))

    outs = pl.pallas_call(
        body, name="grad_w_in_scatter" if na else "grad_w_in", grid=grid,
        in_specs=[pl.BlockSpec((tk, tm), lambda i, j, k: (k, i)),
                  when(QA // tn, 1024),
                  pl.BlockSpec((tk, 1024), lambda i, j, k: (
                      jnp.where((j == GA // tn) | (j == GB // tn), k, 0),
                      jnp.clip(j - GA // tn, 0, 1))),
                  when(CQ // tn, 768), when(CQ // tn, 256)] + [HBM_SPEC] * na,
        out_specs=[pl.BlockSpec((tm, tn), lambda i, j, k: (i, j))] + [HBM_SPEC] * na,
        out_shape=[_sds((D_MODEL, NP), GRAD_DTYPE)] + _exchange_shapes(blocks, False),
        scratch_shapes=[pltpu.VMEM((tm, tn), F32)] + (_Exchange.semaphores(na) if na else []),
        compiler_params=_params(3),
    )(h, dqa, dgate, dlat, dkva, *blocks)
    return outs[0], list(outs[1:])


def _swa_bwd(proj, sinks, do, o):
    s = proj.shape[0]
    nb = s // BLOCK

    def body(sink_ref, q_ref, kp_ref, kc_ref, do_ref, o_ref, dq_ref, dkv_ref, dsink_ref,
             carry_ref, bias_ref):
        n = pl.program_id(0)
        _swa_fill_bias(n, bias_ref)

        @pl.when(n == 0)
        def _():
            carry_ref[...] = jnp.zeros(carry_ref.shape, F32)
            dsink_ref[...] = jnp.zeros(dsink_ref.shape, F32)

        @pl.when(n < nb)
        def _():
            (k_l, k_r), (v_l, v_r), left, _ = _swa_operands(kp_ref[...].astype(F32),
                                                            kc_ref[...].astype(F32))
            lane_s = lax.broadcasted_iota(jnp.int32, (8, 128), 1)
            dk_l, dk_r, dv_l, dv_r = [None, None], [None, None], [None, None], [None, None]
            dsink = jnp.zeros((8, 128), F32)
            for j in range(2):
                qs = _swa_stack(q_ref, j) * 0.125
                dos = _swa_stack(do_ref, j)
                prod_t = (dos * _swa_stack(o_ref, j)).T
                dlt = [jnp.sum(prod_t[0:64], axis=0, keepdims=True),
                       jnp.sum(prod_t[64:128], axis=0, keepdims=True)]
                dq = None
                for e in range(2):
                    kmat = (k_l, k_r)[e][j]
                    vmat = (v_l, v_r)[e][j]
                    ex, es, inv = _swa_softmax(qs, kmat, bias_ref[2 * j + e],
                                               _swa_sink_row(sink_ref, j, e))
                    p = ex * inv
                    ds = p * (_mm_nt(vmat, dos) - dlt[e])
                    sink_term = es * inv * dlt[e]
                    for a in range(4):
                        dsh = -jnp.sum(sink_term[:, 128 * a:128 * (a + 1)], axis=1, keepdims=True)
                        dsink = dsink + jnp.where(lane_s == _swa_head(j, a, e), dsh, 0.0)
                    dq_e = _mm_tn(ds, kmat)
                    dq = dq_e if dq is None else dq + dq_e
                    (dv_l, dv_r)[e][j] = _mm(p, dos)
                    (dk_l, dk_r)[e][j] = _mm(ds, qs)
                for a in range(4):
                    cols = slice(128 * (4 * j + a), 128 * (4 * j + a + 1))
                    dq_ref[:, cols] = (dq[128 * a:128 * (a + 1)] * 0.125).astype(dq_ref.dtype)

            def merge(t_l, t_r):
                head0 = t_l[0] + pltpu.roll(t_r[0], 64, 1)
                head1 = t_r[1] + pltpu.roll(t_l[1], 64, 1)
                return jnp.where(left, head0, head1)

            contrib = jnp.concatenate([merge(dk_l, dk_r), merge(dv_l, dv_r)], axis=1)
            dkv_ref[...] = (carry_ref[...] + contrib[0:BLOCK]).astype(dkv_ref.dtype)
            carry_ref[...] = contrib[BLOCK:2 * BLOCK]
            dsink_ref[...] += dsink

        @pl.when(n == nb)
        def _():
            dkv_ref[...] = carry_ref[...].astype(dkv_ref.dtype)

    cur = lambda n: (jnp.minimum(n, nb - 1), 0)
    return pl.pallas_call(
        body, name="swa_bwd", grid=(nb + 1,),
        in_specs=[pl.BlockSpec(memory_space=pltpu.SMEM),
                  pl.BlockSpec((BLOCK, 1024), cur),
                  pl.BlockSpec((BLOCK, 256), lambda n: (jnp.clip(n - 1, 0, nb - 1), KVA // 256)),
                  pl.BlockSpec((BLOCK, 256), lambda n: (jnp.minimum(n, nb - 1), KVA // 256)),
                  pl.BlockSpec((BLOCK, 1024), cur), pl.BlockSpec((BLOCK, 1024), cur)],
        out_specs=[pl.BlockSpec((BLOCK, 1024), cur),
                   pl.BlockSpec((BLOCK, 256), lambda n: (jnp.maximum(n - 1, 0), 0)),
                   pl.BlockSpec((8, 128), lambda n: (0, 0))],
        out_shape=[_sds((s, 1024), MXU_DTYPE), _sds((s, 256), MXU_DTYPE), _sds((8, 128), F32)],
        scratch_shapes=[pltpu.VMEM((BLOCK, 256), F32), pltpu.VMEM((4, 2 * BLOCK, SWA_STACK), F32)],
        compiler_params=_params(1),
    )(sinks, proj, proj, proj, do, o)


def _mla_bwd(qh, kh, vh, do, dlt, lse, blocks):
    s = qh.shape[1]
    t = min(MLA_TILE, s)
    nq = s // t

    na = len(blocks)

    def body(*refs):
        q_ref, k_ref, v_ref, do_ref, dlt_ref, lse_ref = refs[:6]
        dq_ref, dk_ref, dv_ref = refs[6 + na:9 + na]
        h, j = pl.program_id(0), pl.program_id(1)
        if na:
            ex = _Exchange(refs[6:6 + na], refs[9 + na:9 + 2 * na], refs[9 + 2 * na:], gather=False)

            @pl.when((h == 0) & (j == 0))
            def _():
                ex.start()

        def step(i, diagonal):
            rows = pl.ds(pl.multiple_of(i * t, t), t)
            q, k, dout = q_ref[rows, :], k_ref[...], do_ref[rows, :]
            sc = _scores_t(k, q, t, diagonal)
            p = jnp.exp2(sc * MLA_C2 - lse_ref[0:1, rows])
            dv = _mm(p, dout)
            ds = p * (_mm_nt(v_ref[...], dout) - dlt_ref[0:1, rows])
            dk = _mm(ds, q)
            dq = _mm_tn(ds, k)
            return rows, dq, dk, dv

        rows, dq, dk, dv = step(j, True)
        dk_ref[...] = dk
        dv_ref[...] = dv

        @pl.when(j == 0)
        def _():
            dq_ref[rows, :] = dq * MLA_SCALE

        @pl.when(j > 0)
        def _():
            dq_ref[rows, :] = (dq_ref[rows, :] + dq) * MLA_SCALE

        def above_diagonal(i, carry):
            rows, dq, dk, dv = step(i, False)
            dk_ref[...] += dk
            dv_ref[...] += dv

            @pl.when(j == 0)
            def _():
                dq_ref[rows, :] = dq

            @pl.when(j > 0)
            def _():
                dq_ref[rows, :] += dq
            return carry

        lax.fori_loop(j + 1, nq, above_diagonal, 0)
        dk_ref[...] *= MLA_SCALE

        if na:
            @pl.when((h == MLA_HEADS - 1) & (j == nq - 1))
            def _():
                ex.wait()

    head = lambda h, j: (h, 0, 0)
    kv_map = lambda h, j: (h, j, 0)
    outs = pl.pallas_call(
        body, name="mla_bwd_scatter" if na else "mla_bwd", grid=(MLA_HEADS, nq),
        in_specs=[pl.BlockSpec((None, s, 256), head), pl.BlockSpec((None, t, 256), kv_map),
                  pl.BlockSpec((None, t, 128), kv_map),
                  pl.BlockSpec((s, 128), lambda h, j: (0, h)),
                  pl.BlockSpec((None, 8, s), head), pl.BlockSpec((None, 8, s), head)]
        + [HBM_SPEC] * na,
        out_specs=[pl.BlockSpec((None, s, 256), head),
                   pl.BlockSpec((None, t, 256), kv_map), pl.BlockSpec((None, t, 128), kv_map)]
        + [HBM_SPEC] * na,
        out_shape=[_sds((MLA_HEADS, s, 256), F32), _sds((MLA_HEADS, s, 256), F32),
                   _sds((MLA_HEADS, s, 128), F32)] + _exchange_shapes(blocks, False),
        scratch_shapes=_Exchange.semaphores(na) if na else [],
        compiler_params=_params(2),
    )(qh, kh, vh, do, dlt, lse, *blocks)
    return outs[0], outs[1], outs[2], list(outs[3:])


def _mla_prep_bwd(dqh, dkh, dvh, proj, gq, gkv, w_q, w_kv, tabs):
    s = proj.shape[0]
    tm = min(256, s)
    nm = s // tm
    c, s1, s2 = tabs

    def body(dq_ref, dk_ref, dv_ref, p_ref, gq_ref, gkv_ref, wq_ref, wkv_ref,
             c_ref, s1_ref, s2_ref, dp_ref, dwq_ref, dwkv_ref, dgq_ref, dgkv_ref,
             dqf_ref, dkvf_ref, dwq_acc, dwkv_acc):
        @pl.when(pl.program_id(0) == 0)
        def _():
            dgq_ref[...] = jnp.zeros(dgq_ref.shape, F32)
            dgkv_ref[...] = jnp.zeros(dgkv_ref.shape, F32)
            dwq_acc[...] = jnp.zeros(dwq_acc.shape, F32)
            dwkv_acc[...] = jnp.zeros(dwkv_acc.shape, F32)
        cc, ns1, ns2 = c_ref[...], -s1_ref[...], -s2_ref[...]
        dkr = jnp.zeros((tm, 128), F32)
        for h in range(MLA_HEADS):
            dqf_ref[:, 128 * h:128 * (h + 1)] = dq_ref[h, :, 0:128].astype(dqf_ref.dtype)
            dqf_ref[:, 1024 + 128 * h:1024 + 128 * (h + 1)] = _rope(
                dq_ref[h, :, 128:256], cc, ns1, ns2).astype(dqf_ref.dtype)
            dkvf_ref[:, 128 * h:128 * (h + 1)] = dk_ref[h, :, 0:128].astype(dkvf_ref.dtype)
            dkvf_ref[:, 1024 + 128 * h:1024 + 128 * (h + 1)] = dv_ref[h].astype(dkvf_ref.dtype)
            dkr = dkr + dk_ref[h, :, 128:256]
        dcqn = _mm_nt(dqf_ref[...], wq_ref[...])
        dckvn = _mm_nt(dkvf_ref[...], wkv_ref[...])
        cqh, rq = _rownorm(p_ref[:, 0:384].astype(F32))
        ckvh, rkv = _rownorm(p_ref[:, 384:640].astype(F32))
        dgq_ref[...] += jnp.sum(dcqn * cqh, axis=0, keepdims=True)
        dgkv_ref[...] += jnp.sum(dckvn * ckvh, axis=0, keepdims=True)
        dp_ref[:, 0:384] = _rownorm_bwd(dcqn * gq_ref[...], cqh, rq).astype(dp_ref.dtype)
        dp_ref[:, 384:640] = _rownorm_bwd(dckvn * gkv_ref[...], ckvh, rkv).astype(dp_ref.dtype)
        dp_ref[:, 640:768] = _rope(dkr, cc, ns1, ns2).astype(dp_ref.dtype)
        dwq_acc[...] += _mm_tn(cqh * gq_ref[...], dqf_ref[...])
        dwkv_acc[...] += _mm_tn(ckvh * gkv_ref[...], dkvf_ref[...])

        @pl.when(pl.program_id(0) == nm - 1)
        def _():
            dwq_ref[...] = dwq_acc[...].astype(dwq_ref.dtype)
            dwkv_ref[...] = dwkv_acc[...].astype(dwkv_ref.dtype)

    row = lambda i: (i, 0)
    fixed = lambda i: (0, 0)
    head = lambda i: (0, i, 0)
    return pl.pallas_call(
        body, name="mla_prep_bwd", grid=(nm,),
        in_specs=[pl.BlockSpec((MLA_HEADS, tm, 256), head), pl.BlockSpec((MLA_HEADS, tm, 256), head),
                  pl.BlockSpec((MLA_HEADS, tm, 128), head),
                  pl.BlockSpec((tm, 768), lambda i: (i, CQ // 768)),
                  pl.BlockSpec((1, Q_RANK), fixed), pl.BlockSpec((1, KV_RANK), fixed),
                  pl.BlockSpec((Q_RANK, 2048), fixed), pl.BlockSpec((KV_RANK, 2048), fixed),
                  pl.BlockSpec((tm, 128), row), pl.BlockSpec((tm, 128), row),
                  pl.BlockSpec((tm, 128), row)],
        out_specs=[pl.BlockSpec((tm, 768), row), pl.BlockSpec((Q_RANK, 2048), fixed),
                   pl.BlockSpec((KV_RANK, 2048), fixed),
                   pl.BlockSpec((1, Q_RANK), fixed), pl.BlockSpec((1, KV_RANK), fixed)],
        out_shape=[_sds((s, 768), MXU_DTYPE), _sds((Q_RANK, 2048), GRAD_DTYPE),
                   _sds((KV_RANK, 2048), GRAD_DTYPE),
                   _sds((1, Q_RANK), F32), _sds((1, KV_RANK), F32)],
        scratch_shapes=[pltpu.VMEM((tm, 2048), MXU_DTYPE), pltpu.VMEM((tm, 2048), MXU_DTYPE),
                        pltpu.VMEM((Q_RANK, 2048), F32), pltpu.VMEM((KV_RANK, 2048), F32)],
        compiler_params=_params(1),
    )(dqh, dkh, dvh, proj, gq, gkv, w_q, w_kv, c, s1, s2)


def _in_proj_bwd(dqa, dgate, dlat, dkva, w, x, dx_out, g, blocks):
    s = x.shape[0]
    tm = min(256, s)
    nm = s // tm
    na = len(blocks)

    def body(*refs):
        dqa_ref, dg8_ref, dlat_ref, dkva_ref, w_ref, x_ref, dxo_ref, g_ref = refs[:8]
        dx_ref, dxb_ref, dg_ref = refs[8 + na:11 + na]
        if na:
            ex = _Exchange(refs[8:8 + na], refs[11 + na:11 + 2 * na], refs[11 + 2 * na:], gather=False)

        @pl.when(pl.program_id(0) == 0)
        def _():
            dg_ref[...] = jnp.zeros(dg_ref.shape, F32)
            if na:
                ex.start()

        dh = (_mm_nt(dqa_ref[...], w_ref[:, QA:QA + 1024])
              + _mm_nt(dg8_ref[:, 0:1024], w_ref[:, GA:GA + 1024])
              + _mm_nt(dg8_ref[:, 1024:2048], w_ref[:, GB:GB + 1024])
              + _mm_nt(dlat_ref[...], w_ref[:, CQ:CQ + 768])
              + _mm_nt(dkva_ref[...], w_ref[:, KVA:KVA + 256]))
        xh, r = _rownorm(x_ref[...])
        dg_ref[...] += jnp.sum(dh * xh, axis=0, keepdims=True)
        dx = dxo_ref[...] + _rownorm_bwd(dh * g_ref[...], xh, r)
        dx_ref[...] = dx
        dxb_ref[...] = dx.astype(dxb_ref.dtype)

        if na:
            @pl.when(pl.program_id(0) == nm - 1)
            def _():
                ex.wait()

    row = lambda i: (i, 0)
    fixed = lambda i: (0, 0)
    outs = pl.pallas_call(
        body, name="in_proj_bwd_scatter" if na else "in_proj_bwd", grid=(nm,),
        in_specs=[pl.BlockSpec((tm, 1024), row), pl.BlockSpec((tm, 2048), row),
                  pl.BlockSpec((tm, 768), row), pl.BlockSpec((tm, 256), row),
                  pl.BlockSpec((D_MODEL, NP), fixed, pipeline_mode=pl.Buffered(1)),
                  pl.BlockSpec((tm, D_MODEL), row), pl.BlockSpec((tm, D_MODEL), row),
                  pl.BlockSpec((1, D_MODEL), fixed)] + [HBM_SPEC] * na,
        out_specs=[pl.BlockSpec((tm, D_MODEL), row), pl.BlockSpec((tm, D_MODEL), row),
                   pl.BlockSpec((1, D_MODEL), fixed)] + [HBM_SPEC] * na,
        out_shape=[_sds((s, D_MODEL), F32), _sds((s, D_MODEL), MXU_DTYPE), _sds((1, D_MODEL), F32)]
        + _exchange_shapes(blocks, False),
        scratch_shapes=_Exchange.semaphores(na) if na else [],
        compiler_params=_params(1),
    )(dqa, dgate, dlat, dkva, w, x, dx_out, g, *blocks)
    return outs[0], outs[1], outs[2], list(outs[3:])


def _reduce_adamw(parts, w, m, v, name):
    n_layers = len(parts)
    rows, cols = parts[0].shape[1:]
    lanes = -(-cols // 128) * 128
    tr = rows
    for cand in (1024, 512, 256, 128, 64, 32, 16, 8):
        if rows % cand == 0 and N_DEV * cand * lanes * 4 <= 8 * 1024 * 1024:
            tr = cand
            break
    nr = rows // tr

    def body(*refs):
        p_refs = refs[:n_layers]
        w_ref, m_ref, v_ref, g_ref, d_ref, nm_ref, nv_ref = refs[n_layers:]
        layer = pl.program_id(0)
        for l in range(n_layers):
            @pl.when(layer == l)
            def _(l=l):
                g = p_refs[l][0].astype(F32)
                for k in range(1, N_DEV):
                    g = g + p_refs[l][k].astype(F32)
                update(g, w_ref, m_ref, v_ref, g_ref, d_ref, nm_ref, nv_ref)

    def update(g, w_ref, m_ref, v_ref, g_ref, d_ref, nm_ref, nv_ref):
        m2 = ADAM_B1 * m_ref[...] + (1.0 - ADAM_B1) * g
        v2 = ADAM_B2 * v_ref[...] + (1.0 - ADAM_B2) * (g * g)
        m_hat = m2 / (1.0 - ADAM_B1 ** ADAM_STEP)
        v_hat = v2 / (1.0 - ADAM_B2 ** ADAM_STEP)
        g_ref[...] = g
        d_ref[...] = -ADAM_LR * (m_hat / (jnp.sqrt(v_hat) + ADAM_EPS) + ADAM_WD * w_ref[...])
        nm_ref[...] = m2
        nv_ref[...] = v2

    def part_spec(l):
        return pl.BlockSpec((N_DEV, tr, cols), lambda layer, i: (0, jnp.where(layer == l, i, 0), 0))

    blk = pl.BlockSpec((tr, cols), lambda layer, i: (layer * nr + i, 0))
    return pl.pallas_call(
        body, name=name, grid=(n_layers, nr),
        in_specs=[part_spec(l) for l in range(n_layers)] + [blk, blk, blk],
        out_specs=[blk, blk, blk, blk],
        out_shape=[_sds((n_layers * rows, cols), F32)] * 4,
        compiler_params=_params(2),
    )(*parts, w, m, v)


def _position():
    x, y, c = lax.axis_index("x"), lax.axis_index("y"), lax.axis_index("c")
    return x, y, c


def _index(px, py, pc):
    return 4 * px + 2 * py + pc


HBM_SPEC = pl.BlockSpec(memory_space=pltpu.HBM)


class _Gather:
    def __init__(self, srcs, dsts, sems):
        self.srcs, self.dsts = srcs, dsts
        self.send_sems, self.recv_sems, self.local_sems = sems
        x, y, c = _position()
        self.c = c
        self.me, self.sibling = (x, y, c), (x, y, 1 - c)
        self.chips = [(1 - x, y), (x, 1 - y), (1 - x, 1 - y)]

    def _copy(self, a, k, block, to, own=False):
        slot = self.dsts[a].at[_index(*block)]
        return pltpu.make_async_remote_copy(
            src_ref=self.srcs[a] if own else slot, dst_ref=slot,
            send_sem=self.send_sems.at[7 * a + k], recv_sem=self.recv_sems.at[7 * a + k],
            device_id=to, device_id_type=MESH)

    def _local(self, a):
        return pltpu.make_async_copy(self.srcs[a], self.dsts[a].at[_index(*self.me)],
                                     self.local_sems.at[a])

    def _first(self, a):
        return [self._copy(a, 0, self.me, self.sibling, own=True)] + [
            self._copy(a, 1 + j, self.me, (*chip, self.c), own=True)
            for j, chip in enumerate(self.chips)]

    def _passed(self, a, j):
        return self._copy(a, 4 + j, (*self.chips[j], self.c), self.sibling)

    def start(self):
        for a in range(len(self.srcs)):
            self._local(a).start()
            for cp in self._first(a):
                cp.start()

    def forward(self):
        for j, chip in enumerate(self.chips):
            for a in range(len(self.srcs)):
                self._copy(a, 1 + j, (*chip, self.c), self.me).wait_recv()
                self._passed(a, j).start()

    def finish(self):
        for a in range(len(self.srcs)):
            self._copy(a, 0, self.sibling, self.me).wait_recv()
            for j, chip in enumerate(self.chips):
                self._copy(a, 4 + j, (*chip, 1 - self.c), self.me).wait_recv()
            for cp in self._first(a):
                cp.wait_send()
            for j in range(3):
                self._passed(a, j).wait_send()
            self._local(a).wait()


def _all_gather(shards, name):
    na = len(shards)

    def body(*refs):
        g = _Gather(refs[:na], refs[na:2 * na], refs[2 * na:])
        g.start()
        g.forward()
        g.finish()

    return pl.pallas_call(
        body, name=name,
        in_specs=[HBM_SPEC] * na, out_specs=[HBM_SPEC] * na,
        out_shape=_exchange_shapes(shards, True),
        scratch_shapes=_Exchange.semaphores(na),
    )(*shards)


class _Exchange:
    def __init__(self, srcs, dsts, sems, gather):
        self.srcs, self.dsts, self.gather = srcs, dsts, gather
        self.send_sems, self.recv_sems, self.local_sems = sems
        x, y, c = _position()
        self.me = _index(x, y, c)
        self.peers = [(x ^ ((k >> 2) & 1), y ^ ((k >> 1) & 1), c ^ (k & 1)) for k in range(1, N_DEV)]

    @staticmethod
    def semaphores(na):
        return [pltpu.SemaphoreType.DMA((7 * na,)), pltpu.SemaphoreType.DMA((7 * na,)),
                pltpu.SemaphoreType.DMA((na,))]

    def _src(self, a, slot):
        return self.srcs[a] if self.gather else self.srcs[a].at[slot]

    def _local(self, a):
        return pltpu.make_async_copy(self._src(a, self.me), self.dsts[a].at[self.me],
                                     self.local_sems.at[a])

    def _send(self, a, k):
        peer = self.peers[k]
        return pltpu.make_async_remote_copy(
            src_ref=self._src(a, _index(*peer)), dst_ref=self.dsts[a].at[self.me],
            send_sem=self.send_sems.at[7 * a + k], recv_sem=self.recv_sems.at[7 * a + k],
            device_id=peer, device_id_type=MESH)

    def _arrival(self, a, k):
        landed = self.dsts[a].at[_index(*self.peers[k])]
        return pltpu.make_async_remote_copy(
            src_ref=landed, dst_ref=landed,
            send_sem=self.send_sems.at[7 * a + k], recv_sem=self.recv_sems.at[7 * a + k],
            device_id=self.peers[k], device_id_type=MESH)

    def start(self):
        for a in range(len(self.srcs)):
            self._local(a).start()
            for k in range(N_DEV - 1):
                self._send(a, k).start()

    def wait(self):
        for a in range(len(self.srcs)):
            for k in range(N_DEV - 1):
                self._arrival(a, k).wait_recv()
            for k in range(N_DEV - 1):
                self._send(a, k).wait_send()
            self._local(a).wait()


def _exchange_shapes(arrays, gather):
    return [_sds(((N_DEV,) + a.shape) if gather else a.shape, a.dtype) for a in arrays]


def _exchange_call(arrays, gather, name):
    na = len(arrays)

    def body(*refs):
        ex = _Exchange(refs[:na], refs[na:2 * na], refs[2 * na:], gather)
        ex.start()
        ex.wait()

    return pl.pallas_call(
        body, name=name,
        in_specs=[HBM_SPEC] * na, out_specs=[HBM_SPEC] * na,
        out_shape=_exchange_shapes(arrays, gather),
        scratch_shapes=_Exchange.semaphores(na),
    )(*arrays)


def _layer_fwd(x, small, w_in, g_rest, rest_shards, tabs, next_shards):
    attn_g, sinks, gq, gkv = small
    proj, h, gathered_rest = _in_proj(x, attn_g, w_in, rest_shards)
    w_q, w_kv, w_o = _rest_from_gathered(*(gathered_rest if rest_shards else g_rest))
    swa = _swa_fwd(proj, sinks)
    qh, kh, vh, vth = _mla_prep(proj, gq, gkv, w_q, w_kv, tabs)
    mla, lse, gathered = _mla_fwd(qh, kh, vth, next_shards)
    x_next, y = _out_proj(x, proj, swa, mla, w_o)
    return x_next, (x, proj, h, swa, qh, kh, vh, mla, lse, y), (w_in, w_q, w_kv, w_o), gathered


def _layer_bwd(dx, dxb, saved, small, weights, tabs, pending, send_own):
    attn_g, sinks, gq, gkv = small
    w_in, w_q, w_kv, w_o = weights
    x, proj, h, swa, qh, kh, vh, mla, lse, y = saved
    d_o = _matmul_tn(y, dxb, "grad_w_out")
    do_a, do_b, dgate, dlt = _out_proj_bwd(dxb, proj, swa, mla, w_o)
    dqa, dkva, dsk = _swa_bwd(proj, sinks, do_a, swa)
    dqh, dkh, dvh, received = _mla_bwd(qh, kh, vh, do_b, dlt, lse, pending)
    dlat, d_q, d_kv, d_gq, d_gkv = _mla_prep_bwd(dqh, dkh, dvh, proj, gq, gkv, w_q, w_kv, tabs)
    rest_blocks = _rest_grad_blocks(d_q, d_kv, d_o)
    d_inp, got_rest = _grad_w_in(h, dqa, dgate, dlat, dkva, rest_blocks if send_own else [])
    in_block = _in_grad_blocks(d_inp)
    dx, dxb, d_attn, got_in = _in_proj_bwd(dqa, dgate, dlat, dkva, w_in, x, dx, attn_g,
                                           [in_block] if send_own else [])
    small_grads = (d_attn, dsk[0:1, 0:SWA_HEADS], d_gq, d_gkv)
    return dx, dxb, small_grads, [in_block] + rest_blocks, received, got_in + got_rest


def _pack_small_grads(small_grads, d_final, loss):
    d_attn, d_sink, d_gq, d_gkv = zip(*small_grads)
    return jnp.concatenate([
        jnp.concatenate(d_attn, axis=0).reshape(64, 128),
        jnp.concatenate(d_gq, axis=0).reshape(12, 128),
        jnp.concatenate(d_gkv, axis=0).reshape(8, 128),
        d_final.reshape(16, 128),
        jnp.pad(jnp.concatenate(d_sink, axis=1), ((0, 0), (0, 64))),
        loss[0:1],
        jnp.zeros((PACK_ROWS - ROW_LOSS - 1, 128), F32)], axis=0)


def _pack_small(attn, qa, kva, final, sinks):
    return jnp.concatenate([
        attn.reshape(64, 128), qa.reshape(12, 128), kva.reshape(8, 128), final.reshape(16, 128),
        jnp.pad(sinks.reshape(1, 64), ((0, 0), (0, 64))),
        jnp.zeros((PACK_ROWS - ROW_SINK - 1, 128), F32)], axis=0)


def _unpack_small(p):
    return (p[ROW_ATTN:ROW_QA].reshape(DEPTH, D_MODEL), p[ROW_SINK, 0:64].reshape(DEPTH, SWA_HEADS),
            p[ROW_QA:ROW_KVA].reshape(DEPTH, Q_RANK), p[ROW_KVA:ROW_FINAL].reshape(DEPTH, KV_RANK),
            p[ROW_FINAL:ROW_SINK].reshape(D_MODEL))


def kernel(x, attn_norm_g, w_in, swa_sinks, q_a_norm_g, kv_a_norm_g, w_q_b, w_kv_b, w_out, final_norm_g, loss_target, m_attn_norm_g, m_w_in, m_swa_sinks, m_q_a_norm_g, m_kv_a_norm_g, m_w_q_b, m_w_kv_b, m_w_out, m_final_norm_g, v_attn_norm_g, v_w_in, v_swa_sinks, v_q_a_norm_g, v_kv_a_norm_g, v_w_q_b, v_w_kv_b, v_w_out, v_final_norm_g):
    xs, tgt = x[0], loss_target[0]
    tabs = _rope_tables(xs.shape[0])
    shards = [w.astype(MXU_DTYPE) for w in (w_in, w_q_b, w_kv_b, w_out)]
    layer_shards = lambda l: [w[l] for w in shards]
    smalls = [(attn_norm_g[l:l + 1], swa_sinks[l], q_a_norm_g[l:l + 1], kv_a_norm_g[l:l + 1])
              for l in range(DEPTH)]

    gathered = list(_all_gather(layer_shards(0)[:1], "gather_weights")) + [None] * 3
    weights, saved = [None] * DEPTH, []
    for l in range(DEPTH):
        next_shards = layer_shards(l + 1) if l + 1 < DEPTH else []
        rest_shards = layer_shards(0)[1:] if l == 0 else []
        xs, acts, weights[l], gathered = _layer_fwd(
            xs, smalls[l], _w_in_from_gathered(gathered[0]), gathered[1:], rest_shards, tabs,
            next_shards)
        saved.append(acts)
    dx, dxb, d_final, loss = _final_loss(xs, tgt, final_norm_g.reshape(1, D_MODEL))

    received, small_grads, pending = [None] * DEPTH, [None] * DEPTH, []
    for l in reversed(range(DEPTH)):
        dx, dxb, small_grads[l], blocks, arrived, arrived_own = _layer_bwd(
            dx, dxb, saved[l], smalls[l], weights[l], tabs, pending, send_own=(l == 0))
        if pending:
            received[l + 1] = arrived
        pending = blocks
    received[0] = arrived_own
    small = _exchange_call([_pack_small_grads(small_grads, d_final, loss)], True, "gather_small")[0]

    big = []
    for a, (w, m, v, name) in enumerate(zip((w_in, w_q_b, w_kv_b, w_out),
                                            (m_w_in, m_w_q_b, m_w_kv_b, m_w_out),
                                            (v_w_in, v_w_q_b, v_w_kv_b, v_w_out),
                                            ("adamw_w_in", "adamw_w_q_b", "adamw_w_kv_b",
                                             "adamw_w_out"))):
        cols = w.shape[-1]
        flat = lambda t: t.reshape(-1, cols)
        outs = _reduce_adamw([received[l][a] for l in range(DEPTH)], flat(w), flat(m), flat(v), name)
        big.append([t.reshape(w.shape) for t in outs])

    sm = _reduce_adamw(
        [small],
        _pack_small(attn_norm_g, q_a_norm_g, kv_a_norm_g, final_norm_g, swa_sinks),
        _pack_small(m_attn_norm_g, m_q_a_norm_g, m_kv_a_norm_g, m_final_norm_g, m_swa_sinks),
        _pack_small(v_attn_norm_g, v_q_a_norm_g, v_kv_a_norm_g, v_final_norm_g, v_swa_sinks),
        "adamw_small")
    loss = sm[0][ROW_LOSS, 0]
    kinds = []
    for t in range(4):
        attn, sinks, qa, kva, final = _unpack_small(sm[t])
        b_in, b_q, b_kv, b_o = (big[i][t] for i in range(4))
        kinds.append((attn, b_in, sinks, qa, kva, b_q, b_kv, b_o, final))
    return (loss, dx[None], *kinds[0], *kinds[1], *kinds[2], *kinds[3])
```

```python
import functools

import jax
import jax.numpy as jnp
from jax import lax
from jax.experimental import pallas as pl
from jax.experimental.pallas import tpu as pltpu

F32 = jnp.float32
BF16 = jnp.bfloat16
MXU_DTYPE = BF16
GRAD_DTYPE = BF16
PROJ_DTYPE = BF16

D_MODEL = 2048
DEPTH = 4
EPS = 1e-6
NEG = -1e30
BLOCK = 128
SWA_HEADS = 16
MLA_HEADS = 8
Q_RANK = 384
KV_RANK = 256
MLA_SCALE = 192 ** -0.5
MLA_C2 = MLA_SCALE * 1.4426950408889634
ROPE_THETA = 10000.0
IN_WIDTH = 4032

ADAM_LR = 0.001
ADAM_B1 = 0.9
ADAM_B2 = 0.999
ADAM_EPS = 1e-08
ADAM_WD = 0.01
ADAM_STEP = 10

N_DEV = 8
MESH = pl.DeviceIdType.MESH

NP = 4096
QA, GA, GB, CQ, CKV, KR, KVA = 0, 1024, 2048, 3072, 3456, 3712, 3840

ROW_ATTN, ROW_QA, ROW_KVA, ROW_FINAL, ROW_SINK, ROW_LOSS, PACK_ROWS = 0, 64, 76, 84, 100, 101, 104

VMEM_LIMIT = 56 * 1024 * 1024
MLA_TILE = 512
GATHER_FORWARD_HEAD = 5


def _sds(shape, dtype):
    return jax.ShapeDtypeStruct(shape, dtype)


def _params(n_axes):
    return pltpu.CompilerParams(dimension_semantics=("arbitrary",) * n_axes,
                                vmem_limit_bytes=VMEM_LIMIT)


def _mm(a, b):
    return jnp.dot(a.astype(MXU_DTYPE), b.astype(MXU_DTYPE), preferred_element_type=F32)


def _mm_nt(a, b):
    return lax.dot_general(a.astype(MXU_DTYPE), b.astype(MXU_DTYPE),
                           (((1,), (1,)), ((), ())), preferred_element_type=F32)


def _mm_tn(a, b):
    return lax.dot_general(a.astype(MXU_DTYPE), b.astype(MXU_DTYPE),
                           (((0,), (0,)), ((), ())), preferred_element_type=F32)


def _rownorm(x):
    r = lax.rsqrt(jnp.mean(x * x, axis=-1, keepdims=True) + EPS)
    return x * r, r


def _rownorm_bwd(dxh, xh, r):
    return r * (dxh - xh * jnp.mean(dxh * xh, axis=-1, keepdims=True))


def _rope(t, c, s1, s2):
    return t * c + pltpu.roll(t, 32, 1) * s1 + pltpu.roll(t, 96, 1) * s2


def _pad_in_cols(w):
    z = jnp.zeros(w.shape[:-1] + (64,), w.dtype)
    return jnp.concatenate([w[..., 0:1024], w[..., 1280:2304], w[..., 3008:4032], w[..., 2304:2688],
                            w[..., 2688:2944], w[..., 2944:3008], z, w[..., 1024:1152],
                            w[..., 1152:1280]], axis=-1)


def _w_in_from_gathered(g_in):
    return _pad_in_cols(g_in.transpose(1, 0, 2).reshape(D_MODEL, IN_WIDTH))


def _rest_from_gathered(g_qb, g_kvb, g_out):
    qb = g_qb.transpose(1, 0, 2)
    rope = jnp.pad(qb[..., 128:], ((0, 0), (0, 0), (0, 64)))
    w_q = jnp.concatenate([qb[..., :128].reshape(Q_RANK, 1024),
                           rope.reshape(Q_RANK, 1024)], axis=-1)
    kvb = g_kvb.transpose(1, 0, 2)
    w_kv = jnp.concatenate([kvb[..., :128].reshape(KV_RANK, 1024),
                            kvb[..., 128:].reshape(KV_RANK, 1024)], axis=-1)
    w_o = g_out.reshape(D_MODEL, D_MODEL)
    return w_q, w_kv, w_o


def _in_grad_blocks(d_inp):
    d_in = jnp.concatenate([d_inp[:, QA:QA + 1024], d_inp[:, KVA:KVA + 256], d_inp[:, GA:GA + 1024],
                            d_inp[:, CQ:CQ + 704], d_inp[:, GB:GB + 1024]], axis=1)
    return d_in.reshape(D_MODEL, N_DEV, 504).transpose(1, 0, 2)


def _rest_grad_blocks(d_q, d_kv, d_o):
    qn = d_q[:, :1024].reshape(Q_RANK, 8, 128)
    qr = d_q[:, 1024:].reshape(Q_RANK, 8, 128)[..., :64]
    b_q = jnp.concatenate([qn, qr], axis=-1).transpose(1, 0, 2)
    kn = d_kv[:, :1024].reshape(KV_RANK, 8, 128)
    vv = d_kv[:, 1024:].reshape(KV_RANK, 8, 128)
    b_kv = jnp.concatenate([kn, vv], axis=-1).transpose(1, 0, 2)
    b_o = d_o.reshape(N_DEV, 256, D_MODEL)
    return [b_q, b_kv, b_o]


def _rope_tables(s):
    pos = jnp.arange(s, dtype=F32)
    inv_freq = ROPE_THETA ** (-jnp.arange(0, 64, 2, dtype=F32) / 64)
    ang = pos[:, None] * inv_freq[None, :]
    cos, sin = jnp.cos(ang), jnp.sin(ang)
    z32 = jnp.zeros((s, 32), F32)
    z64 = jnp.zeros((s, 64), F32)
    c = jnp.concatenate([cos, cos, z64], axis=1)
    s1 = jnp.concatenate([z32, sin, z64], axis=1)
    s2 = jnp.concatenate([-sin, z32, z64], axis=1)
    return c, s1, s2


def _in_proj(x, g, w, shards):
    s = x.shape[0]
    tm, tn = min(512, s), 1024
    nm = s // tm
    na = len(shards)

    def body(*refs):
        x_ref, g_ref, w_ref = refs[:3]
        o_ref, h_ref = refs[3 + na:5 + na]
        i = pl.program_id(0)
        if na:
            ex = _Gather(refs[3:3 + na], refs[5 + na:5 + 2 * na], refs[5 + 2 * na:])

            @pl.when(i == 0)
            def _():
                ex.start()

            @pl.when(i == nm // 2)
            def _():
                ex.forward()

        xh, _ = _rownorm(x_ref[...])
        h_ref[...] = (xh * g_ref[...]).astype(h_ref.dtype)
        for j in range(NP // tn):
            cols = slice(j * tn, (j + 1) * tn)
            o_ref[:, cols] = jnp.dot(h_ref[...], w_ref[:, cols],
                                     preferred_element_type=F32).astype(o_ref.dtype)

        if na:
            @pl.when(i == nm - 1)
            def _():
                ex.finish()

    row = lambda i: (i, 0)
    fixed = lambda i: (0, 0)
    outs = pl.pallas_call(
        body, name="in_proj_gather" if na else "in_proj", grid=(nm,),
        in_specs=[pl.BlockSpec((tm, D_MODEL), row), pl.BlockSpec((1, D_MODEL), fixed),
                  pl.BlockSpec((D_MODEL, NP), fixed, pipeline_mode=pl.Buffered(1))]
        + [HBM_SPEC] * na,
        out_specs=[pl.BlockSpec((tm, NP), row), pl.BlockSpec((tm, D_MODEL), row)]
        + [HBM_SPEC] * na,
        out_shape=[_sds((s, NP), PROJ_DTYPE), _sds((s, D_MODEL), MXU_DTYPE)]
        + _exchange_shapes(shards, True),
        scratch_shapes=_Exchange.semaphores(na) if na else [],
        compiler_params=_params(1),
    )(x, g, w, *shards)
    return outs[0], outs[1], list(outs[2:])


def _swa_slopes():
    return [2.0 ** (-8.0 * (h + 1) / SWA_HEADS) for h in range(SWA_HEADS)]


def _swa_operands(kv_p, kv_c):
    kk = jnp.concatenate([kv_p[:, :128], kv_c[:, :128]], axis=0)
    vv = jnp.concatenate([kv_p[:, 128:], kv_c[:, 128:]], axis=0)
    left = lax.broadcasted_iota(jnp.int32, (2 * BLOCK, 128), 1) < 64

    def split(t):
        lo = jnp.where(left, t, 0.0)
        hi = jnp.where(left, 0.0, t)
        lefts = [lo, pltpu.roll(hi, 64, 1)]
        rights = [pltpu.roll(lo, 64, 1), hi]
        return lefts, rights

    return split(kk), split(vv), left, vv


SWA_STACK = 4 * BLOCK


def _swa_head(j, a, e):
    return 2 * (4 * j + a) + e


def _swa_bias(n, j, e):
    slopes = _swa_slopes()
    ki = lax.broadcasted_iota(jnp.int32, (2 * BLOCK, SWA_STACK), 0)
    r = lax.broadcasted_iota(jnp.int32, (2 * BLOCK, SWA_STACK), 1)
    a = r >> 7
    delta = BLOCK + (r & (BLOCK - 1)) - ki
    valid = (delta >= 0) & (delta < BLOCK) & ((n - 1) * BLOCK + ki >= 0)
    sl = [slopes[_swa_head(j, t, e)] for t in range(4)]
    slope = jnp.where(a == 0, sl[0], jnp.where(a == 1, sl[1], jnp.where(a == 2, sl[2], sl[3])))
    return jnp.where(valid, -slope * delta.astype(F32), NEG)


def _swa_fill_bias(n, bias_ref):
    @pl.when(n <= 1)
    def _():
        for j in range(2):
            for e in range(2):
                bias_ref[2 * j + e] = _swa_bias(n, j, e)


def _swa_sink_row(sink_ref, j, e):
    a = lax.broadcasted_iota(jnp.int32, (1, SWA_STACK), 1) >> 7
    sk = [sink_ref[_swa_head(j, t, e)] for t in range(4)]
    return jnp.where(a == 0, sk[0], jnp.where(a == 1, sk[1], jnp.where(a == 2, sk[2], sk[3])))


def _swa_stack(ref, j):
    return jnp.concatenate([ref[:, 128 * (4 * j + a):128 * (4 * j + a + 1)] for a in range(4)], axis=0)


def _swa_softmax(qs, kmat, bias, sink):
    sc = _mm_nt(kmat, qs) + bias
    m = jnp.maximum(jnp.max(sc, axis=0, keepdims=True), sink)
    ex = jnp.exp(sc - m)
    es = jnp.exp(sink - m)
    return ex, es, 1.0 / (jnp.sum(ex, axis=0, keepdims=True) + es)


def _swa_values_t(vv):
    vt = vv.T
    top = lax.broadcasted_iota(jnp.int32, (128, 2 * BLOCK), 0) < 64
    lo = jnp.where(top, vt, 0.0)
    hi = jnp.where(top, 0.0, vt)
    return [lo, pltpu.roll(hi, 64, 0)], [pltpu.roll(lo, 64, 0), hi]


def _swa_fwd(proj, sinks):
    s = proj.shape[0]
    nb = s // BLOCK

    def body(sink_ref, q_ref, kp_ref, kc_ref, o_ref, bias_ref):
        n = pl.program_id(0)
        _swa_fill_bias(n, bias_ref)
        (k_l, k_r), _, _, vv = _swa_operands(kp_ref[...].astype(F32), kc_ref[...].astype(F32))
        vt_top, vt_bottom = _swa_values_t(vv)
        for j in range(2):
            qs = _swa_stack(q_ref, j) * 0.125
            out_t = None
            for e in range(2):
                ex, _, inv = _swa_softmax(qs, (k_l, k_r)[e][j], bias_ref[2 * j + e],
                                          _swa_sink_row(sink_ref, j, e))
                o_t = _mm((vt_top, vt_bottom)[e][j], ex) * inv
                out_t = o_t if out_t is None else out_t + o_t
            out = out_t.T
            for a in range(4):
                o_ref[:, 128 * (4 * j + a):128 * (4 * j + a + 1)] = out[128 * a:128 * (a + 1)]

    return pl.pallas_call(
        body, name="swa_fwd", grid=(nb,),
        in_specs=[pl.BlockSpec(memory_space=pltpu.SMEM),
                  pl.BlockSpec((BLOCK, 1024), lambda n: (n, 0)),
                  pl.BlockSpec((BLOCK, 256), lambda n: (jnp.maximum(n - 1, 0), KVA // 256)),
                  pl.BlockSpec((BLOCK, 256), lambda n: (n, KVA // 256))],
        out_specs=pl.BlockSpec((BLOCK, 1024), lambda n: (n, 0)),
        out_shape=_sds((s, 1024), F32),
        scratch_shapes=[pltpu.VMEM((4, 2 * BLOCK, SWA_STACK), F32)],
        compiler_params=_params(1),
    )(sinks, proj, proj, proj)


def _mla_prep(proj, gq, gkv, w_q, w_kv, tabs):
    s = proj.shape[0]
    tm = min(512, s)
    c, s1, s2 = tabs

    def body(p_ref, gq_ref, gkv_ref, wq_ref, wkv_ref, c_ref, s1_ref, s2_ref,
             q_ref, k_ref, v_ref, vt_ref):
        cqh, _ = _rownorm(p_ref[:, 0:384].astype(F32))
        ckvh, _ = _rownorm(p_ref[:, 384:640].astype(F32))
        q = _mm(cqh * gq_ref[...], wq_ref[...])
        kv = _mm(ckvh * gkv_ref[...], wkv_ref[...])
        cc, ss1, ss2 = c_ref[...], s1_ref[...], s2_ref[...]
        krr = _rope(p_ref[:, 640:768].astype(F32), cc, ss1, ss2).astype(k_ref.dtype)
        for h in range(MLA_HEADS):
            q_ref[h, :, 0:128] = q[:, 128 * h:128 * (h + 1)].astype(q_ref.dtype)
            q_ref[h, :, 128:256] = _rope(q[:, 1024 + 128 * h:1024 + 128 * (h + 1)],
                                         cc, ss1, ss2).astype(q_ref.dtype)
            k_ref[h, :, 0:128] = kv[:, 128 * h:128 * (h + 1)].astype(k_ref.dtype)
            k_ref[h, :, 128:256] = krr
            vv = kv[:, 1024 + 128 * h:1024 + 128 * (h + 1)]
            v_ref[h] = vv.astype(v_ref.dtype)
            vt_ref[h, 0:128, :] = vv.T.astype(vt_ref.dtype)
            vt_ref[h, 128:256, :] = jnp.ones((128, tm), vt_ref.dtype)

    row = lambda i: (i, 0)
    fixed = lambda i: (0, 0)
    return pl.pallas_call(
        body, name="mla_prep", grid=(s // tm,),
        in_specs=[pl.BlockSpec((tm, 768), lambda i: (i, CQ // 768)),
                  pl.BlockSpec((1, Q_RANK), fixed), pl.BlockSpec((1, KV_RANK), fixed),
                  pl.BlockSpec((Q_RANK, 2048), fixed), pl.BlockSpec((KV_RANK, 2048), fixed),
                  pl.BlockSpec((tm, 128), row), pl.BlockSpec((tm, 128), row),
                  pl.BlockSpec((tm, 128), row)],
        out_specs=[pl.BlockSpec((MLA_HEADS, tm, 256), lambda i: (0, i, 0)),
                   pl.BlockSpec((MLA_HEADS, tm, 256), lambda i: (0, i, 0)),
                   pl.BlockSpec((MLA_HEADS, tm, 128), lambda i: (0, i, 0)),
                   pl.BlockSpec((MLA_HEADS, 256, tm), lambda i: (0, 0, i))],
        out_shape=[_sds((MLA_HEADS, s, 256), MXU_DTYPE), _sds((MLA_HEADS, s, 256), MXU_DTYPE),
                   _sds((MLA_HEADS, s, 128), MXU_DTYPE), _sds((MLA_HEADS, 256, s), MXU_DTYPE)],
        compiler_params=_params(1),
    )(proj, gq, gkv, w_q, w_kv, c, s1, s2)


def _scores_t(k, q, t, diagonal):
    sc = _mm_nt(k, q)
    if diagonal:
        key = lax.broadcasted_iota(jnp.int32, (t, t), 0)
        query = lax.broadcasted_iota(jnp.int32, (t, t), 1)
        sc = jnp.where(key <= query, sc, NEG)
    return sc


def _mla_fwd(qh, kh, vth, shards):
    s = qh.shape[1]
    t = min(MLA_TILE, s)
    nq = s // t
    na = len(shards)

    def body(*refs):
        q_ref, k_ref, vt_ref = refs[:3]
        o_ref, lse_ref = refs[3 + na:5 + na]
        m_ref, acc_ref = refs[5 + 2 * na:7 + 2 * na]
        h, i = pl.program_id(0), pl.program_id(1)
        if na:
            ex = _Gather(refs[3:3 + na], refs[5 + na:5 + 2 * na], refs[7 + 2 * na:])

            @pl.when((h == 0) & (i == 0))
            def _():
                ex.start()

            @pl.when((h == GATHER_FORWARD_HEAD) & (i == 0))
            def _():
                ex.forward()

        m_ref[...] = jnp.full(m_ref.shape, NEG, F32)
        acc_ref[...] = jnp.zeros(acc_ref.shape, F32)

        def step(start, width, diagonal):
            keys = pl.ds(pl.multiple_of(start, t), width)
            if diagonal:
                sc = _scores_t(k_ref[keys, :], q_ref[...], t, True)
            else:
                sc = _mm_nt(k_ref[keys, :], q_ref[...])
            m_prev = m_ref[...]
            m_new = jnp.maximum(m_prev, jnp.max(sc, axis=0, keepdims=True))
            alpha = jnp.exp2((m_prev - m_new) * MLA_C2)
            p = jnp.exp2((sc - m_new[0:1, :]) * MLA_C2)
            acc_ref[...] = alpha[0:1, :] * acc_ref[...] + _mm(vt_ref[:, keys], p)
            m_ref[...] = m_new

        def below_diagonal(jj, carry):
            step(jj * (2 * t), 2 * t, False)
            return carry

        lax.fori_loop(0, i // 2, below_diagonal, 0)

        @pl.when(i % 2 == 1)
        def _():
            step((i - 1) * t, t, False)

        step(i * t, t, True)
        l = acc_ref[128:136, :]
        o_ref[...] = (acc_ref[0:128, :] / l[0:1, :]).T
        lse_ref[...] = m_ref[...] * MLA_C2 + jnp.log2(l)

        if na:
            @pl.when((h == MLA_HEADS - 1) & (i == nq - 1))
            def _():
                ex.finish()

    head = lambda h, i: (h, 0, 0)
    outs = pl.pallas_call(
        body, name="mla_fwd_gather" if na else "mla_fwd", grid=(MLA_HEADS, nq),
        in_specs=[pl.BlockSpec((None, t, 256), lambda h, i: (h, i, 0)),
                  pl.BlockSpec((None, s, 256), head),
                  pl.BlockSpec((None, 256, s), head)] + [HBM_SPEC] * na,
        out_specs=[pl.BlockSpec((t, 128), lambda h, i: (i, h)),
                   pl.BlockSpec((None, 8, t), lambda h, i: (h, 0, i))] + [HBM_SPEC] * na,
        out_shape=[_sds((s, 1024), F32), _sds((MLA_HEADS, 8, s), F32)]
        + _exchange_shapes(shards, True),
        scratch_shapes=[pltpu.VMEM((8, t), F32), pltpu.VMEM((256, t), F32)]
        + (_Exchange.semaphores(na) if na else []),
        compiler_params=_params(2),
    )(qh, kh, vth, *shards)
    return outs[0], outs[1], list(outs[2:])


def _silu_parts(g):
    sg = jax.nn.sigmoid(g)
    return g * sg, sg * (1.0 + g * (1.0 - sg))


def _out_proj(x, proj, swa, mla, w_out):
    s = x.shape[0]
    tm = min(512, s)

    def body(x_ref, ga_ref, gb_ref, a_ref, b_ref, w_ref, xo_ref, y_ref):
        sa, _ = _silu_parts(ga_ref[...].astype(F32))
        sb, _ = _silu_parts(gb_ref[...].astype(F32))
        y_ref[:, 0:1024] = (a_ref[...] * sa).astype(y_ref.dtype)
        y_ref[:, 1024:2048] = (b_ref[...] * sb).astype(y_ref.dtype)
        xo_ref[...] = x_ref[...] + jnp.dot(y_ref[...], w_ref[...], preferred_element_type=F32)

    row = lambda i: (i, 0)
    return pl.pallas_call(
        body, name="out_proj", grid=(s // tm,),
        in_specs=[pl.BlockSpec((tm, D_MODEL), row),
                  pl.BlockSpec((tm, 1024), lambda i: (i, GA // 1024)),
                  pl.BlockSpec((tm, 1024), lambda i: (i, GB // 1024)),
                  pl.BlockSpec((tm, 1024), row), pl.BlockSpec((tm, 1024), row),
                  pl.BlockSpec((D_MODEL, D_MODEL), lambda i: (0, 0), pipeline_mode=pl.Buffered(1))],
        out_specs=[pl.BlockSpec((tm, D_MODEL), row), pl.BlockSpec((tm, D_MODEL), row)],
        out_shape=[_sds((s, D_MODEL), F32), _sds((s, D_MODEL), MXU_DTYPE)],
        compiler_params=_params(1),
    )(x, proj, proj, swa, mla, w_out)


def _final_loss(x, tgt, g):
    s = x.shape[0]
    tm = min(512, s)

    def body(x_ref, t_ref, g_ref, dx_ref, dxb_ref, dg_ref, loss_ref):
        @pl.when(pl.program_id(0) == 0)
        def _():
            dg_ref[...] = jnp.zeros(dg_ref.shape, F32)
            loss_ref[...] = jnp.zeros(loss_ref.shape, F32)
        xh, r = _rownorm(x_ref[...])
        gg = g_ref[...]
        err = xh * gg - t_ref[...]
        per_row = jnp.mean(err * err, axis=-1, keepdims=True)
        loss_ref[...] += 0.5 * jnp.sum(per_row, axis=0, keepdims=True)
        dy = err * (1.0 / D_MODEL)
        dg_ref[...] += jnp.sum(dy * xh, axis=0, keepdims=True)
        dx = _rownorm_bwd(dy * gg, xh, r)
        dx_ref[...] = dx
        dxb_ref[...] = dx.astype(dxb_ref.dtype)

    row = lambda i: (i, 0)
    fixed = lambda i: (0, 0)
    return pl.pallas_call(
        body, name="final_loss", grid=(s // tm,),
        in_specs=[pl.BlockSpec((tm, D_MODEL), row), pl.BlockSpec((tm, D_MODEL), row),
                  pl.BlockSpec((1, D_MODEL), fixed)],
        out_specs=[pl.BlockSpec((tm, D_MODEL), row), pl.BlockSpec((tm, D_MODEL), row),
                   pl.BlockSpec((1, D_MODEL), fixed), pl.BlockSpec((8, 128), fixed)],
        out_shape=[_sds((s, D_MODEL), F32), _sds((s, D_MODEL), MXU_DTYPE), _sds((1, D_MODEL), F32),
                   _sds((8, 128), F32)],
        compiler_params=_params(1),
    )(x, tgt, g)


def _out_proj_bwd(dx, proj, swa, mla, w_out):
    s = dx.shape[0]
    tm = min(512, s)

    def body(dx_ref, ga_ref, gb_ref, a_ref, b_ref, w_ref, doa_ref, dob_ref, dg_ref, dlt_ref):
        dx = dx_ref[...].astype(MXU_DTYPE)
        dya = _mm_nt(dx, w_ref[0:1024, :])
        sa, dsa = _silu_parts(ga_ref[...].astype(F32))
        doa_ref[...] = dya * sa
        dg_ref[:, 0:1024] = (dya * a_ref[...] * dsa).astype(dg_ref.dtype)
        dyb = _mm_nt(dx, w_ref[1024:2048, :])
        sb, dsb = _silu_parts(gb_ref[...].astype(F32))
        b = b_ref[...]
        dob = dyb * sb
        dob_ref[...] = dob.astype(dob_ref.dtype)
        dg_ref[:, 1024:2048] = (dyb * b * dsb).astype(dg_ref.dtype)
        prod = dob * b
        for h in range(MLA_HEADS):
            dlt = jnp.sum(prod[:, 128 * h:128 * (h + 1)], axis=1, keepdims=True)
            dlt_ref[h] = jnp.broadcast_to(dlt, (tm, 128)).T[0:8, :]

    row = lambda i: (i, 0)
    return pl.pallas_call(
        body, name="out_proj_bwd", grid=(s // tm,),
        in_specs=[pl.BlockSpec((tm, D_MODEL), row),
                  pl.BlockSpec((tm, 1024), lambda i: (i, GA // 1024)),
                  pl.BlockSpec((tm, 1024), lambda i: (i, GB // 1024)),
                  pl.BlockSpec((tm, 1024), row), pl.BlockSpec((tm, 1024), row),
                  pl.BlockSpec((D_MODEL, D_MODEL), lambda i: (0, 0), pipeline_mode=pl.Buffered(1))],
        out_specs=[pl.BlockSpec((tm, 1024), row), pl.BlockSpec((tm, 1024), row),
                   pl.BlockSpec((tm, D_MODEL), row),
                   pl.BlockSpec((MLA_HEADS, 8, tm), lambda i: (0, 0, i))],
        out_shape=[_sds((s, 1024), F32), _sds((s, 1024), MXU_DTYPE), _sds((s, D_MODEL), MXU_DTYPE),
                   _sds((MLA_HEADS, 8, s), F32)],
        compiler_params=_params(1),
    )(dx, proj, proj, swa, mla, w_out)


def _matmul_tn(a, b, name):
    s, m = a.shape
    n = b.shape[1]
    tm, tn, tk = min(1024, m), min(1024, n), min(2048, s)
    nk = s // tk

    def body(a_ref, b_ref, o_ref, acc_ref):
        k = pl.program_id(2)

        @pl.when(k == 0)
        def _():
            acc_ref[...] = jnp.zeros(acc_ref.shape, F32)
        acc_ref[...] += _mm_tn(a_ref[...], b_ref[...])

        @pl.when(k == nk - 1)
        def _():
            o_ref[...] = acc_ref[...].astype(o_ref.dtype)

    return pl.pallas_call(
        body, name=name, grid=(m // tm, n // tn, nk),
        in_specs=[pl.BlockSpec((tk, tm), lambda i, j, k: (k, i)),
                  pl.BlockSpec((tk, tn), lambda i, j, k: (k, j))],
        out_specs=pl.BlockSpec((tm, tn), lambda i, j, k: (i, j)),
        out_shape=_sds((m, n), GRAD_DTYPE),
        scratch_shapes=[pltpu.VMEM((tm, tn), F32)],
        compiler_params=_params(3),
    )(a, b)


def _grad_w_in(h, dqa, dgate, dlat, dkva, blocks):
    s = h.shape[0]
    tm, tn, tk = 1024, 1024, min(2048, s)
    nk = s // tk
    grid = (D_MODEL // tm, NP // tn, nk)
    na = len(blocks)

    def body(*refs):
        a_ref, dqa_ref, dg8_ref, dlat_ref, dkva_ref = refs[:5]
        o_ref = refs[5 + na]
        acc_ref = refs[6 + 2 * na]
        i, j, k = pl.program_id(0), pl.program_id(1), pl.program_id(2)
        if na:
            ex = _Exchange(refs[5:5 + na], refs[6 + na:6 + 2 * na], refs[7 + 2 * na:], gather=False)

            @pl.when((i == 0) & (j == 0) & (k == 0))
            def _():
                ex.start()

        @pl.when(k == 0)
        def _():
            acc_ref[...] = jnp.zeros(acc_ref.shape, F32)

        @pl.when(j == QA // tn)
        def _():
            acc_ref[...] += _mm_tn(a_ref[...], dqa_ref[...])

        @pl.when((j == GA // tn) | (j == GB // tn))
        def _():
            acc_ref[...] += _mm_tn(a_ref[...], dg8_ref[...])

        @pl.when(j == CQ // tn)
        def _():
            acc_ref[:, 0:768] += _mm_tn(a_ref[...], dlat_ref[...])
            acc_ref[:, 768:1024] += _mm_tn(a_ref[...], dkva_ref[...])

        @pl.when(k == nk - 1)
        def _():
            o_ref[...] = acc_ref[...].astype(o_ref.dtype)

        if na:
            @pl.when((i == grid[0] - 1) & (j == grid[1] - 1) & (k == nk - 1))
            def _():
                ex.wait()

    def when(group, width):
        return pl.BlockSpec((tk, width), lambda i, j, k: (jnp.where(j == group, k, 0), 0))

    outs = pl.pallas_call(
        body, name="grad_w_in_scatter" if na else "grad_w_in", grid=grid,
        in_specs=[pl.BlockSpec((tk, tm), lambda i, j, k: (k, i)),
                  when(QA // tn, 1024),
                  pl.BlockSpec((tk, 1024), lambda i, j, k: (
                      jnp.where((j == GA // tn) | (j == GB // tn), k, 0),
                      jnp.clip(j - GA // tn, 0, 1))),
                  when(CQ // tn, 768), when(CQ // tn, 256)] + [HBM_SPEC] * na,
        out_specs=[pl.BlockSpec((tm, tn), lambda i, j, k: (i, j))] + [HBM_SPEC] * na,
        out_shape=[_sds((D_MODEL, NP), GRAD_DTYPE)] + _exchange_shapes(blocks, False),
        scratch_shapes=[pltpu.VMEM((tm, tn), F32)] + (_Exchange.semaphores(na) if na else []),
        compiler_params=_params(3),
    )(h, dqa, dgate, dlat, dkva, *blocks)
    return outs[0], list(outs[1:])


def _swa_bwd(proj, sinks, do, o):
    s = proj.shape[0]
    nb = s // BLOCK

    def body(sink_ref, q_ref, kp_ref, kc_ref, do_ref, o_ref, dq_ref, dkv_ref, dsink_ref,
             carry_ref, bias_ref):
        n = pl.program_id(0)
        _swa_fill_bias(n, bias_ref)

        @pl.when(n == 0)
        def _():
            carry_ref[...] = jnp.zeros(carry_ref.shape, F32)
            dsink_ref[...] = jnp.zeros(dsink_ref.shape, F32)

        @pl.when(n < nb)
        def _():
            (k_l, k_r), (v_l, v_r), left, _ = _swa_operands(kp_ref[...].astype(F32),
                                                            kc_ref[...].astype(F32))
            lane_s = lax.broadcasted_iota(jnp.int32, (8, 128), 1)
            dk_l, dk_r, dv_l, dv_r = [None, None], [None, None], [None, None], [None, None]
            dsink = jnp.zeros((8, 128), F32)
            for j in range(2):
                qs = _swa_stack(q_ref, j) * 0.125
                dos = _swa_stack(do_ref, j)
                prod_t = (dos * _swa_stack(o_ref, j)).T
                dlt = [jnp.sum(prod_t[0:64], axis=0, keepdims=True),
                       jnp.sum(prod_t[64:128], axis=0, keepdims=True)]
                dq = None
                for e in range(2):
                    kmat = (k_l, k_r)[e][j]
                    vmat = (v_l, v_r)[e][j]
                    ex, es, inv = _swa_softmax(qs, kmat, bias_ref[2 * j + e],
                                               _swa_sink_row(sink_ref, j, e))
                    p = ex * inv
                    ds = p * (_mm_nt(vmat, dos) - dlt[e])
                    sink_term = es * inv * dlt[e]
                    for a in range(4):
                        dsh = -jnp.sum(sink_term[:, 128 * a:128 * (a + 1)], axis=1, keepdims=True)
                        dsink = dsink + jnp.where(lane_s == _swa_head(j, a, e), dsh, 0.0)
                    dq_e = _mm_tn(ds, kmat)
                    dq = dq_e if dq is None else dq + dq_e
                    (dv_l, dv_r)[e][j] = _mm(p, dos)
                    (dk_l, dk_r)[e][j] = _mm(ds, qs)
                for a in range(4):
                    cols = slice(128 * (4 * j + a), 128 * (4 * j + a + 1))
                    dq_ref[:, cols] = (dq[128 * a:128 * (a + 1)] * 0.125).astype(dq_ref.dtype)

            def merge(t_l, t_r):
                head0 = t_l[0] + pltpu.roll(t_r[0], 64, 1)
                head1 = t_r[1] + pltpu.roll(t_l[1], 64, 1)
                return jnp.where(left, head0, head1)

            contrib = jnp.concatenate([merge(dk_l, dk_r), merge(dv_l, dv_r)], axis=1)
            dkv_ref[...] = (carry_ref[...] + contrib[0:BLOCK]).astype(dkv_ref.dtype)
            carry_ref[...] = contrib[BLOCK:2 * BLOCK]
            dsink_ref[...] += dsink

        @pl.when(n == nb)
        def _():
            dkv_ref[...] = carry_ref[...].astype(dkv_ref.dtype)

    cur = lambda n: (jnp.minimum(n, nb - 1), 0)
    return pl.pallas_call(
        body, name="swa_bwd", grid=(nb + 1,),
        in_specs=[pl.BlockSpec(memory_space=pltpu.SMEM),
                  pl.BlockSpec((BLOCK, 1024), cur),
                  pl.BlockSpec((BLOCK, 256), lambda n: (jnp.clip(n - 1, 0, nb - 1), KVA // 256)),
                  pl.BlockSpec((BLOCK, 256), lambda n: (jnp.minimum(n, nb - 1), KVA // 256)),
                  pl.BlockSpec((BLOCK, 1024), cur), pl.BlockSpec((BLOCK, 1024), cur)],
        out_specs=[pl.BlockSpec((BLOCK, 1024), cur),
                   pl.BlockSpec((BLOCK, 256), lambda n: (jnp.maximum(n - 1, 0), 0)),
                   pl.BlockSpec((8, 128), lambda n: (0, 0))],
        out_shape=[_sds((s, 1024), MXU_DTYPE), _sds((s, 256), MXU_DTYPE), _sds((8, 128), F32)],
        scratch_shapes=[pltpu.VMEM((BLOCK, 256), F32), pltpu.VMEM((4, 2 * BLOCK, SWA_STACK), F32)],
        compiler_params=_params(1),
    )(sinks, proj, proj, proj, do, o)


def _mla_bwd(qh, kh, vh, do, dlt, lse, blocks):
    s = qh.shape[1]
    t = min(MLA_TILE, s)
    nq = s // t

    na = len(blocks)

    def body(*refs):
        q_ref, k_ref, v_ref, do_ref, dlt_ref, lse_ref = refs[:6]
        dq_ref, dk_ref, dv_ref = refs[6 + na:9 + na]
        h, j = pl.program_id(0), pl.program_id(1)
        if na:
            ex = _Exchange(refs[6:6 + na], refs[9 + na:9 + 2 * na], refs[9 + 2 * na:], gather=False)

            @pl.when((h == 0) & (j == 0))
            def _():
                ex.start()

        def step(start, width, diagonal):
            rows = pl.ds(pl.multiple_of(start, t), width)
            q, k, dout = q_ref[rows, :], k_ref[...], do_ref[rows, :]
            sc = _scores_t(k, q, t, True) if diagonal else _mm_nt(k, q)
            p = jnp.exp2(sc * MLA_C2 - lse_ref[0:1, rows])
            dv = _mm(p, dout)
            ds = p * (_mm_nt(v_ref[...], dout) - dlt_ref[0:1, rows])
            dk = _mm(ds, q)
            dq = _mm_tn(ds, k)
            return rows, dq, dk, dv

        rows, dq, dk, dv = step(j * t, t, True)
        dk_ref[...] = dk
        dv_ref[...] = dv

        @pl.when(j == 0)
        def _():
            dq_ref[rows, :] = dq * MLA_SCALE

        @pl.when(j > 0)
        def _():
            dq_ref[rows, :] = (dq_ref[rows, :] + dq) * MLA_SCALE

        def above_diagonal(start, width):
            rows, dq, dk, dv = step(start, width, False)
            dk_ref[...] += dk
            dv_ref[...] += dv

            @pl.when(j == 0)
            def _():
                dq_ref[rows, :] = dq

            @pl.when(j > 0)
            def _():
                dq_ref[rows, :] += dq

        n_above = nq - 1 - j

        def pair(jj, carry):
            above_diagonal((j + 1 + 2 * jj) * t, 2 * t)
            return carry

        lax.fori_loop(0, n_above // 2, pair, 0)

        @pl.when(n_above % 2 == 1)
        def _():
            above_diagonal((nq - 1) * t, t)

        dk_ref[...] *= MLA_SCALE

        if na:
            @pl.when((h == MLA_HEADS - 1) & (j == nq - 1))
            def _():
                ex.wait()

    head = lambda h, j: (h, 0, 0)
    kv_map = lambda h, j: (h, j, 0)
    outs = pl.pallas_call(
        body, name="mla_bwd_scatter" if na else "mla_bwd", grid=(MLA_HEADS, nq),
        in_specs=[pl.BlockSpec((None, s, 256), head), pl.BlockSpec((None, t, 256), kv_map),
                  pl.BlockSpec((None, t, 128), kv_map),
                  pl.BlockSpec((s, 128), lambda h, j: (0, h)),
                  pl.BlockSpec((None, 8, s), head), pl.BlockSpec((None, 8, s), head)]
        + [HBM_SPEC] * na,
        out_specs=[pl.BlockSpec((None, s, 256), head),
                   pl.BlockSpec((None, t, 256), kv_map), pl.BlockSpec((None, t, 128), kv_map)]
        + [HBM_SPEC] * na,
        out_shape=[_sds((MLA_HEADS, s, 256), F32), _sds((MLA_HEADS, s, 256), F32),
                   _sds((MLA_HEADS, s, 128), F32)] + _exchange_shapes(blocks, False),
        scratch_shapes=_Exchange.semaphores(na) if na else [],
        compiler_params=_params(2),
    )(qh, kh, vh, do, dlt, lse, *blocks)
    return outs[0], outs[1], outs[2], list(outs[3:])


def _mla_prep_bwd(dqh, dkh, dvh, proj, gq, gkv, w_q, w_kv, tabs):
    s = proj.shape[0]
    tm = min(256, s)
    nm = s // tm
    c, s1, s2 = tabs

    def body(dq_ref, dk_ref, dv_ref, p_ref, gq_ref, gkv_ref, wq_ref, wkv_ref,
             c_ref, s1_ref, s2_ref, dp_ref, dwq_ref, dwkv_ref, dgq_ref, dgkv_ref,
             dqf_ref, dkvf_ref, dwq_acc, dwkv_acc):
        @pl.when(pl.program_id(0) == 0)
        def _():
            dgq_ref[...] = jnp.zeros(dgq_ref.shape, F32)
            dgkv_ref[...] = jnp.zeros(dgkv_ref.shape, F32)
            dwq_acc[...] = jnp.zeros(dwq_acc.shape, F32)
            dwkv_acc[...] = jnp.zeros(dwkv_acc.shape, F32)
        cc, ns1, ns2 = c_ref[...], -s1_ref[...], -s2_ref[...]
        dkr = jnp.zeros((tm, 128), F32)
        for h in range(MLA_HEADS):
            dqf_ref[:, 128 * h:128 * (h + 1)] = dq_ref[h, :, 0:128].astype(dqf_ref.dtype)
            dqf_ref[:, 1024 + 128 * h:1024 + 128 * (h + 1)] = _rope(
                dq_ref[h, :, 128:256], cc, ns1, ns2).astype(dqf_ref.dtype)
            dkvf_ref[:, 128 * h:128 * (h + 1)] = dk_ref[h, :, 0:128].astype(dkvf_ref.dtype)
            dkvf_ref[:, 1024 + 128 * h:1024 + 128 * (h + 1)] = dv_ref[h].astype(dkvf_ref.dtype)
            dkr = dkr + dk_ref[h, :, 128:256]
        dcqn = _mm_nt(dqf_ref[...], wq_ref[...])
        dckvn = _mm_nt(dkvf_ref[...], wkv_ref[...])
        cqh, rq = _rownorm(p_ref[:, 0:384].astype(F32))
        ckvh, rkv = _rownorm(p_ref[:, 384:640].astype(F32))
        dgq_ref[...] += jnp.sum(dcqn * cqh, axis=0, keepdims=True)
        dgkv_ref[...] += jnp.sum(dckvn * ckvh, axis=0, keepdims=True)
        dp_ref[:, 0:384] = _rownorm_bwd(dcqn * gq_ref[...], cqh, rq).astype(dp_ref.dtype)
        dp_ref[:, 384:640] = _rownorm_bwd(dckvn * gkv_ref[...], ckvh, rkv).astype(dp_ref.dtype)
        dp_ref[:, 640:768] = _rope(dkr, cc, ns1, ns2).astype(dp_ref.dtype)
        dwq_acc[...] += _mm_tn(cqh * gq_ref[...], dqf_ref[...])
        dwkv_acc[...] += _mm_tn(ckvh * gkv_ref[...], dkvf_ref[...])

        @pl.when(pl.program_id(0) == nm - 1)
        def _():
            dwq_ref[...] = dwq_acc[...].astype(dwq_ref.dtype)
            dwkv_ref[...] = dwkv_acc[...].astype(dwkv_ref.dtype)

    row = lambda i: (i, 0)
    fixed = lambda i: (0, 0)
    head = lambda i: (0, i, 0)
    return pl.pallas_call(
        body, name="mla_prep_bwd", grid=(nm,),
        in_specs=[pl.BlockSpec((MLA_HEADS, tm, 256), head), pl.BlockSpec((MLA_HEADS, tm, 256), head),
                  pl.BlockSpec((MLA_HEADS, tm, 128), head),
                  pl.BlockSpec((tm, 768), lambda i: (i, CQ // 768)),
                  pl.BlockSpec((1, Q_RANK), fixed), pl.BlockSpec((1, KV_RANK), fixed),
                  pl.BlockSpec((Q_RANK, 2048), fixed), pl.BlockSpec((KV_RANK, 2048), fixed),
                  pl.BlockSpec((tm, 128), row), pl.BlockSpec((tm, 128), row),
                  pl.BlockSpec((tm, 128), row)],
        out_specs=[pl.BlockSpec((tm, 768), row), pl.BlockSpec((Q_RANK, 2048), fixed),
                   pl.BlockSpec((KV_RANK, 2048), fixed),
                   pl.BlockSpec((1, Q_RANK), fixed), pl.BlockSpec((1, KV_RANK), fixed)],
        out_shape=[_sds((s, 768), MXU_DTYPE), _sds((Q_RANK, 2048), GRAD_DTYPE),
                   _sds((KV_RANK, 2048), GRAD_DTYPE),
                   _sds((1, Q_RANK), F32), _sds((1, KV_RANK), F32)],
        scratch_shapes=[pltpu.VMEM((tm, 2048), MXU_DTYPE), pltpu.VMEM((tm, 2048), MXU_DTYPE),
                        pltpu.VMEM((Q_RANK, 2048), F32), pltpu.VMEM((KV_RANK, 2048), F32)],
        compiler_params=_params(1),
    )(dqh, dkh, dvh, proj, gq, gkv, w_q, w_kv, c, s1, s2)


def _in_proj_bwd(dqa, dgate, dlat, dkva, w, x, dx_out, g, blocks):
    s = x.shape[0]
    tm = min(256, s)
    nm = s // tm
    na = len(blocks)

    def body(*refs):
        dqa_ref, dg8_ref, dlat_ref, dkva_ref, w_ref, x_ref, dxo_ref, g_ref = refs[:8]
        dx_ref, dxb_ref, dg_ref = refs[8 + na:11 + na]
        if na:
            ex = _Exchange(refs[8:8 + na], refs[11 + na:11 + 2 * na], refs[11 + 2 * na:], gather=False)

        @pl.when(pl.program_id(0) == 0)
        def _():
            dg_ref[...] = jnp.zeros(dg_ref.shape, F32)
            if na:
                ex.start()

        dh = (_mm_nt(dqa_ref[...], w_ref[:, QA:QA + 1024])
              + _mm_nt(dg8_ref[:, 0:1024], w_ref[:, GA:GA + 1024])
              + _mm_nt(dg8_ref[:, 1024:2048], w_ref[:, GB:GB + 1024])
              + _mm_nt(dlat_ref[...], w_ref[:, CQ:CQ + 768])
              + _mm_nt(dkva_ref[...], w_ref[:, KVA:KVA + 256]))
        xh, r = _rownorm(x_ref[...])
        dg_ref[...] += jnp.sum(dh * xh, axis=0, keepdims=True)
        dx = dxo_ref[...] + _rownorm_bwd(dh * g_ref[...], xh, r)
        dx_ref[...] = dx
        dxb_ref[...] = dx.astype(dxb_ref.dtype)

        if na:
            @pl.when(pl.program_id(0) == nm - 1)
            def _():
                ex.wait()

    row = lambda i: (i, 0)
    fixed = lambda i: (0, 0)
    outs = pl.pallas_call(
        body, name="in_proj_bwd_scatter" if na else "in_proj_bwd", grid=(nm,),
        in_specs=[pl.BlockSpec((tm, 1024), row), pl.BlockSpec((tm, 2048), row),
                  pl.BlockSpec((tm, 768), row), pl.BlockSpec((tm, 256), row),
                  pl.BlockSpec((D_MODEL, NP), fixed, pipeline_mode=pl.Buffered(1)),
                  pl.BlockSpec((tm, D_MODEL), row), pl.BlockSpec((tm, D_MODEL), row),
                  pl.BlockSpec((1, D_MODEL), fixed)] + [HBM_SPEC] * na,
        out_specs=[pl.BlockSpec((tm, D_MODEL), row), pl.BlockSpec((tm, D_MODEL), row),
                   pl.BlockSpec((1, D_MODEL), fixed)] + [HBM_SPEC] * na,
        out_shape=[_sds((s, D_MODEL), F32), _sds((s, D_MODEL), MXU_DTYPE), _sds((1, D_MODEL), F32)]
        + _exchange_shapes(blocks, False),
        scratch_shapes=_Exchange.semaphores(na) if na else [],
        compiler_params=_params(1),
    )(dqa, dgate, dlat, dkva, w, x, dx_out, g, *blocks)
    return outs[0], outs[1], outs[2], list(outs[3:])


def _reduce_adamw(parts, w, m, v, name):
    n_layers = len(parts)
    rows, cols = parts[0].shape[1:]
    lanes = -(-cols // 128) * 128
    tr = rows
    for cand in (1024, 512, 256, 128, 64, 32, 16, 8):
        if rows % cand == 0 and N_DEV * cand * lanes * 4 <= 8 * 1024 * 1024:
            tr = cand
            break
    nr = rows // tr

    def body(*refs):
        p_refs = refs[:n_layers]
        w_ref, m_ref, v_ref, g_ref, d_ref, nm_ref, nv_ref = refs[n_layers:]
        layer = pl.program_id(0)
        for l in range(n_layers):
            @pl.when(layer == l)
            def _(l=l):
                g = p_refs[l][0].astype(F32)
                for k in range(1, N_DEV):
                    g = g + p_refs[l][k].astype(F32)
                update(g, w_ref, m_ref, v_ref, g_ref, d_ref, nm_ref, nv_ref)

    def update(g, w_ref, m_ref, v_ref, g_ref, d_ref, nm_ref, nv_ref):
        m2 = ADAM_B1 * m_ref[...] + (1.0 - ADAM_B1) * g
        v2 = ADAM_B2 * v_ref[...] + (1.0 - ADAM_B2) * (g * g)
        m_hat = m2 / (1.0 - ADAM_B1 ** ADAM_STEP)
        v_hat = v2 / (1.0 - ADAM_B2 ** ADAM_STEP)
        g_ref[...] = g
        d_ref[...] = -ADAM_LR * (m_hat / (jnp.sqrt(v_hat) + ADAM_EPS) + ADAM_WD * w_ref[...])
        nm_ref[...] = m2
        nv_ref[...] = v2

    def part_spec(l):
        return pl.BlockSpec((N_DEV, tr, cols), lambda layer, i: (0, jnp.where(layer == l, i, 0), 0))

    blk = pl.BlockSpec((tr, cols), lambda layer, i: (layer * nr + i, 0))
    return pl.pallas_call(
        body, name=name, grid=(n_layers, nr),
        in_specs=[part_spec(l) for l in range(n_layers)] + [blk, blk, blk],
        out_specs=[blk, blk, blk, blk],
        out_shape=[_sds((n_layers * rows, cols), F32)] * 4,
        compiler_params=_params(2),
    )(*parts, w, m, v)


def _position():
    x, y, c = lax.axis_index("x"), lax.axis_index("y"), lax.axis_index("c")
    return x, y, c


def _index(px, py, pc):
    return 4 * px + 2 * py + pc


HBM_SPEC = pl.BlockSpec(memory_space=pltpu.HBM)


class _Gather:
    def __init__(self, srcs, dsts, sems):
        self.srcs, self.dsts = srcs, dsts
        self.send_sems, self.recv_sems, self.local_sems = sems
        x, y, c = _position()
        self.c = c
        self.me, self.sibling = (x, y, c), (x, y, 1 - c)
        self.chips = [(1 - x, y), (x, 1 - y), (1 - x, 1 - y)]

    def _copy(self, a, k, block, to, own=False):
        slot = self.dsts[a].at[_index(*block)]
        return pltpu.make_async_remote_copy(
            src_ref=self.srcs[a] if own else slot, dst_ref=slot,
            send_sem=self.send_sems.at[7 * a + k], recv_sem=self.recv_sems.at[7 * a + k],
            device_id=to, device_id_type=MESH)

    def _local(self, a):
        return pltpu.make_async_copy(self.srcs[a], self.dsts[a].at[_index(*self.me)],
                                     self.local_sems.at[a])

    def _first(self, a):
        return [self._copy(a, 0, self.me, self.sibling, own=True)] + [
            self._copy(a, 1 + j, self.me, (*chip, self.c), own=True)
            for j, chip in enumerate(self.chips)]

    def _passed(self, a, j):
        return self._copy(a, 4 + j, (*self.chips[j], self.c), self.sibling)

    def start(self):
        for a in range(len(self.srcs)):
            self._local(a).start()
            for cp in self._first(a):
                cp.start()

    def forward(self):
        for j, chip in enumerate(self.chips):
            for a in range(len(self.srcs)):
                self._copy(a, 1 + j, (*chip, self.c), self.me).wait_recv()
                self._passed(a, j).start()

    def finish(self):
        for a in range(len(self.srcs)):
            self._copy(a, 0, self.sibling, self.me).wait_recv()
            for j, chip in enumerate(self.chips):
                self._copy(a, 4 + j, (*chip, 1 - self.c), self.me).wait_recv()
            for cp in self._first(a):
                cp.wait_send()
            for j in range(3):
                self._passed(a, j).wait_send()
            self._local(a).wait()


def _all_gather(shards, name):
    na = len(shards)

    def body(*refs):
        g = _Gather(refs[:na], refs[na:2 * na], refs[2 * na:])
        g.start()
        g.forward()
        g.finish()

    return pl.pallas_call(
        body, name=name,
        in_specs=[HBM_SPEC] * na, out_specs=[HBM_SPEC] * na,
        out_shape=_exchange_shapes(shards, True),
        scratch_shapes=_Exchange.semaphores(na),
    )(*shards)


class _Exchange:
    def __init__(self, srcs, dsts, sems, gather):
        self.srcs, self.dsts, self.gather = srcs, dsts, gather
        self.send_sems, self.recv_sems, self.local_sems = sems
        x, y, c = _position()
        self.me = _index(x, y, c)
        self.peers = [(x ^ ((k >> 2) & 1), y ^ ((k >> 1) & 1), c ^ (k & 1)) for k in range(1, N_DEV)]

    @staticmethod
    def semaphores(na):
        return [pltpu.SemaphoreType.DMA((7 * na,)), pltpu.SemaphoreType.DMA((7 * na,)),
                pltpu.SemaphoreType.DMA((na,))]

    def _src(self, a, slot):
        return self.srcs[a] if self.gather else self.srcs[a].at[slot]

    def _local(self, a):
        return pltpu.make_async_copy(self._src(a, self.me), self.dsts[a].at[self.me],
                                     self.local_sems.at[a])

    def _send(self, a, k):
        peer = self.peers[k]
        return pltpu.make_async_remote_copy(
            src_ref=self._src(a, _index(*peer)), dst_ref=self.dsts[a].at[self.me],
            send_sem=self.send_sems.at[7 * a + k], recv_sem=self.recv_sems.at[7 * a + k],
            device_id=peer, device_id_type=MESH)

    def _arrival(self, a, k):
        landed = self.dsts[a].at[_index(*self.peers[k])]
        return pltpu.make_async_remote_copy(
            src_ref=landed, dst_ref=landed,
            send_sem=self.send_sems.at[7 * a + k], recv_sem=self.recv_sems.at[7 * a + k],
            device_id=self.peers[k], device_id_type=MESH)

    def start(self):
        for a in range(len(self.srcs)):
            self._local(a).start()
            for k in range(N_DEV - 1):
                self._send(a, k).start()

    def wait(self):
        for a in range(len(self.srcs)):
            for k in range(N_DEV - 1):
                self._arrival(a, k).wait_recv()
            for k in range(N_DEV - 1):
                self._send(a, k).wait_send()
            self._local(a).wait()


def _exchange_shapes(arrays, gather):
    return [_sds(((N_DEV,) + a.shape) if gather else a.shape, a.dtype) for a in arrays]


def _exchange_call(arrays, gather, name):
    na = len(arrays)

    def body(*refs):
        ex = _Exchange(refs[:na], refs[na:2 * na], refs[2 * na:], gather)
        ex.start()
        ex.wait()

    return pl.pallas_call(
        body, name=name,
        in_specs=[HBM_SPEC] * na, out_specs=[HBM_SPEC] * na,
        out_shape=_exchange_shapes(arrays, gather),
        scratch_shapes=_Exchange.semaphores(na),
    )(*arrays)


def _layer_fwd(x, small, w_in, g_rest, rest_shards, tabs, next_shards):
    attn_g, sinks, gq, gkv = small
    proj, h, gathered_rest = _in_proj(x, attn_g, w_in, rest_shards)
    w_q, w_kv, w_o = _rest_from_gathered(*(gathered_rest if rest_shards else g_rest))
    swa = _swa_fwd(proj, sinks)
    qh, kh, vh, vth = _mla_prep(proj, gq, gkv, w_q, w_kv, tabs)
    mla, lse, gathered = _mla_fwd(qh, kh, vth, next_shards)
    x_next, y = _out_proj(x, proj, swa, mla, w_o)
    return x_next, (x, proj, h, swa, qh, kh, vh, mla, lse, y), (w_in, w_q, w_kv, w_o), gathered


def _layer_bwd(dx, dxb, saved, small, weights, tabs, pending, send_own):
    attn_g, sinks, gq, gkv = small
    w_in, w_q, w_kv, w_o = weights
    x, proj, h, swa, qh, kh, vh, mla, lse, y = saved
    d_o = _matmul_tn(y, dxb, "grad_w_out")
    do_a, do_b, dgate, dlt = _out_proj_bwd(dxb, proj, swa, mla, w_o)
    dqa, dkva, dsk = _swa_bwd(proj, sinks, do_a, swa)
    dqh, dkh, dvh, received = _mla_bwd(qh, kh, vh, do_b, dlt, lse, pending)
    dlat, d_q, d_kv, d_gq, d_gkv = _mla_prep_bwd(dqh, dkh, dvh, proj, gq, gkv, w_q, w_kv, tabs)
    rest_blocks = _rest_grad_blocks(d_q, d_kv, d_o)
    d_inp, got_rest = _grad_w_in(h, dqa, dgate, dlat, dkva, rest_blocks if send_own else [])
    in_block = _in_grad_blocks(d_inp)
    dx, dxb, d_attn, got_in = _in_proj_bwd(dqa, dgate, dlat, dkva, w_in, x, dx, attn_g,
                                           [in_block] if send_own else [])
    small_grads = (d_attn, dsk[0:1, 0:SWA_HEADS], d_gq, d_gkv)
    return dx, dxb, small_grads, [in_block] + rest_blocks, received, got_in + got_rest


def _pack_small_grads(small_grads, d_final, loss):
    d_attn, d_sink, d_gq, d_gkv = zip(*small_grads)
    return jnp.concatenate([
        jnp.concatenate(d_attn, axis=0).reshape(64, 128),
        jnp.concatenate(d_gq, axis=0).reshape(12, 128),
        jnp.concatenate(d_gkv, axis=0).reshape(8, 128),
        d_final.reshape(16, 128),
        jnp.pad(jnp.concatenate(d_sink, axis=1), ((0, 0), (0, 64))),
        loss[0:1],
        jnp.zeros((PACK_ROWS - ROW_LOSS - 1, 128), F32)], axis=0)


def _pack_small(attn, qa, kva, final, sinks):
    return jnp.concatenate([
        attn.reshape(64, 128), qa.reshape(12, 128), kva.reshape(8, 128), final.reshape(16, 128),
        jnp.pad(sinks.reshape(1, 64), ((0, 0), (0, 64))),
        jnp.zeros((PACK_ROWS - ROW_SINK - 1, 128), F32)], axis=0)


def _unpack_small(p):
    return (p[ROW_ATTN:ROW_QA].reshape(DEPTH, D_MODEL), p[ROW_SINK, 0:64].reshape(DEPTH, SWA_HEADS),
            p[ROW_QA:ROW_KVA].reshape(DEPTH, Q_RANK), p[ROW_KVA:ROW_FINAL].reshape(DEPTH, KV_RANK),
            p[ROW_FINAL:ROW_SINK].reshape(D_MODEL))


def kernel(x, attn_norm_g, w_in, swa_sinks, q_a_norm_g, kv_a_norm_g, w_q_b, w_kv_b, w_out, final_norm_g, loss_target, m_attn_norm_g, m_w_in, m_swa_sinks, m_q_a_norm_g, m_kv_a_norm_g, m_w_q_b, m_w_kv_b, m_w_out, m_final_norm_g, v_attn_norm_g, v_w_in, v_swa_sinks, v_q_a_norm_g, v_kv_a_norm_g, v_w_q_b, v_w_kv_b, v_w_out, v_final_norm_g):
    xs, tgt = x[0], loss_target[0]
    tabs = _rope_tables(xs.shape[0])
    shards = [w.astype(MXU_DTYPE) for w in (w_in, w_q_b, w_kv_b, w_out)]
    layer_shards = lambda l: [w[l] for w in shards]
    smalls = [(attn_norm_g[l:l + 1], swa_sinks[l], q_a_norm_g[l:l + 1], kv_a_norm_g[l:l + 1])
              for l in range(DEPTH)]

    gathered = list(_all_gather(layer_shards(0)[:1], "gather_weights")) + [None] * 3
    weights, saved = [None] * DEPTH, []
    for l in range(DEPTH):
        next_shards = layer_shards(l + 1) if l + 1 < DEPTH else []
        rest_shards = layer_shards(0)[1:] if l == 0 else []
        xs, acts, weights[l], gathered = _layer_fwd(
            xs, smalls[l], _w_in_from_gathered(gathered[0]), gathered[1:], rest_shards, tabs,
            next_shards)
        saved.append(acts)
    dx, dxb, d_final, loss = _final_loss(xs, tgt, final_norm_g.reshape(1, D_MODEL))

    received, small_grads, pending = [None] * DEPTH, [None] * DEPTH, []
    for l in reversed(range(DEPTH)):
        dx, dxb, small_grads[l], blocks, arrived, arrived_own = _layer_bwd(
            dx, dxb, saved[l], smalls[l], weights[l], tabs, pending, send_own=(l == 0))
        if pending:
            received[l + 1] = arrived
        pending = blocks
    received[0] = arrived_own
    small = _exchange_call([_pack_small_grads(small_grads, d_final, loss)], True, "gather_small")[0]

    big = []
    for a, (w, m, v, name) in enumerate(zip((w_in, w_q_b, w_kv_b, w_out),
                                            (m_w_in, m_w_q_b, m_w_kv_b, m_w_out),
                                            (v_w_in, v_w_q_b, v_w_kv_b, v_w_out),
                                            ("adamw_w_in", "adamw_w_q_b", "adamw_w_kv_b",
                                             "adamw_w_out"))):
        cols = w.shape[-1]
        flat = lambda t: t.reshape(-1, cols)
        outs = _reduce_adamw([received[l][a] for l in range(DEPTH)], flat(w), flat(m), flat(v), name)
        big.append([t.reshape(w.shape) for t in outs])

    sm = _reduce_adamw(
        [small],
        _pack_small(attn_norm_g, q_a_norm_g, kv_a_norm_g, final_norm_g, swa_sinks),
        _pack_small(m_attn_norm_g, m_q_a_norm_g, m_kv_a_norm_g, m_final_norm_g, m_swa_sinks),
        _pack_small(v_attn_norm_g, v_q_a_norm_g, v_kv_a_norm_g, v_final_norm_g, v_swa_sinks),
        "adamw_small")
    loss = sm[0][ROW_LOSS, 0]
    kinds = []
    for t in range(4):
        attn, sinks, qa, kva, final = _unpack_small(sm[t])
        b_in, b_q, b_kv, b_o = (big[i][t] for i in range(4))
        kinds.append((attn, b_in, sinks, qa, kva, b_q, b_kv, b_o, final))
    return (loss, dx[None], *kinds[0], *kinds[1], *kinds[2], *kinds[3])
```

```python
import functools

import jax
import jax.numpy as jnp
from jax import lax
from jax.experimental import pallas as pl
from jax.experimental.pallas import tpu as pltpu

F32 = jnp.float32
BF16 = jnp.bfloat16
MXU_DTYPE = BF16
GRAD_DTYPE = BF16
PROJ_DTYPE = BF16
ATTN_DTYPE = BF16

D_MODEL = 2048
DEPTH = 4
EPS = 1e-6
NEG = -1e30
BLOCK = 128
SWA_HEADS = 16
MLA_HEADS = 8
Q_RANK = 384
KV_RANK = 256
MLA_SCALE = 192 ** -0.5
MLA_C2 = MLA_SCALE * 1.4426950408889634
ROPE_THETA = 10000.0
IN_WIDTH = 4032

ADAM_LR = 0.001
ADAM_B1 = 0.9
ADAM_B2 = 0.999
ADAM_EPS = 1e-08
ADAM_WD = 0.01
ADAM_STEP = 10

N_DEV = 8
MESH = pl.DeviceIdType.MESH

NP = 4096
QA, GA, GB, CQ, CKV, KR, KVA = 0, 1024, 2048, 3072, 3456, 3712, 3840

ROW_ATTN, ROW_QA, ROW_KVA, ROW_FINAL, ROW_SINK, ROW_LOSS, PACK_ROWS = 0, 64, 76, 84, 100, 101, 104

VMEM_LIMIT = 56 * 1024 * 1024
MLA_TILE = 512
GATHER_FORWARD_HEAD = 5


def _sds(shape, dtype):
    return jax.ShapeDtypeStruct(shape, dtype)


def _params(n_axes):
    return pltpu.CompilerParams(dimension_semantics=("arbitrary",) * n_axes,
                                vmem_limit_bytes=VMEM_LIMIT)


def _mm(a, b):
    return jnp.dot(a.astype(MXU_DTYPE), b.astype(MXU_DTYPE), preferred_element_type=F32)


def _mm_nt(a, b):
    return lax.dot_general(a.astype(MXU_DTYPE), b.astype(MXU_DTYPE),
                           (((1,), (1,)), ((), ())), preferred_element_type=F32)


def _mm_tn(a, b):
    return lax.dot_general(a.astype(MXU_DTYPE), b.astype(MXU_DTYPE),
                           (((0,), (0,)), ((), ())), preferred_element_type=F32)


def _rownorm(x):
    r = lax.rsqrt(jnp.mean(x * x, axis=-1, keepdims=True) + EPS)
    return x * r, r


def _rownorm_bwd(dxh, xh, r):
    return r * (dxh - xh * jnp.mean(dxh * xh, axis=-1, keepdims=True))


def _rope(t, c, s1, s2):
    return t * c + pltpu.roll(t, 32, 1) * s1 + pltpu.roll(t, 96, 1) * s2


def _pad_in_cols(w):
    z = jnp.zeros(w.shape[:-1] + (64,), w.dtype)
    return jnp.concatenate([w[..., 0:1024], w[..., 1280:2304], w[..., 3008:4032], w[..., 2304:2688],
                            w[..., 2688:2944], w[..., 2944:3008], z, w[..., 1024:1152],
                            w[..., 1152:1280]], axis=-1)


def _w_in_from_gathered(g_in):
    return _pad_in_cols(g_in.transpose(1, 0, 2).reshape(D_MODEL, IN_WIDTH))


def _rest_from_gathered(g_qb, g_kvb, g_out):
    qb = g_qb.transpose(1, 0, 2)
    rope = jnp.pad(qb[..., 128:], ((0, 0), (0, 0), (0, 64)))
    w_q = jnp.concatenate([qb[..., :128].reshape(Q_RANK, 1024),
                           rope.reshape(Q_RANK, 1024)], axis=-1)
    kvb = g_kvb.transpose(1, 0, 2)
    w_kv = jnp.concatenate([kvb[..., :128].reshape(KV_RANK, 1024),
                            kvb[..., 128:].reshape(KV_RANK, 1024)], axis=-1)
    w_o = g_out.reshape(D_MODEL, D_MODEL)
    return w_q, w_kv, w_o


def _in_grad_blocks(d_inp):
    d_in = jnp.concatenate([d_inp[:, QA:QA + 1024], d_inp[:, KVA:KVA + 256], d_inp[:, GA:GA + 1024],
                            d_inp[:, CQ:CQ + 704], d_inp[:, GB:GB + 1024]], axis=1)
    return d_in.reshape(D_MODEL, N_DEV, 504).transpose(1, 0, 2)


def _rest_grad_blocks(d_q, d_kv, d_o):
    qn = d_q[:, :1024].reshape(Q_RANK, 8, 128)
    qr = d_q[:, 1024:].reshape(Q_RANK, 8, 128)[..., :64]
    b_q = jnp.concatenate([qn, qr], axis=-1).transpose(1, 0, 2)
    kn = d_kv[:, :1024].reshape(KV_RANK, 8, 128)
    vv = d_kv[:, 1024:].reshape(KV_RANK, 8, 128)
    b_kv = jnp.concatenate([kn, vv], axis=-1).transpose(1, 0, 2)
    b_o = d_o.reshape(N_DEV, 256, D_MODEL)
    return [b_q, b_kv, b_o]


def _rope_tables(s):
    pos = jnp.arange(s, dtype=F32)
    inv_freq = ROPE_THETA ** (-jnp.arange(0, 64, 2, dtype=F32) / 64)
    ang = pos[:, None] * inv_freq[None, :]
    cos, sin = jnp.cos(ang), jnp.sin(ang)
    z32 = jnp.zeros((s, 32), F32)
    z64 = jnp.zeros((s, 64), F32)
    c = jnp.concatenate([cos, cos, z64], axis=1)
    s1 = jnp.concatenate([z32, sin, z64], axis=1)
    s2 = jnp.concatenate([-sin, z32, z64], axis=1)
    return c, s1, s2


def _in_proj(x, g, w, shards):
    s = x.shape[0]
    tm, tn = min(512, s), 1024
    nm = s // tm
    na = len(shards)

    def body(*refs):
        x_ref, g_ref, w_ref = refs[:3]
        o_ref, h_ref = refs[3 + na:5 + na]
        i = pl.program_id(0)
        if na:
            ex = _Gather(refs[3:3 + na], refs[5 + na:5 + 2 * na], refs[5 + 2 * na:])

            @pl.when(i == 0)
            def _():
                ex.start()

            @pl.when(i == nm // 2)
            def _():
                ex.forward()

        xh, _ = _rownorm(x_ref[...])
        h_ref[...] = (xh * g_ref[...]).astype(h_ref.dtype)
        for j in range(NP // tn):
            cols = slice(j * tn, (j + 1) * tn)
            o_ref[:, cols] = jnp.dot(h_ref[...], w_ref[:, cols],
                                     preferred_element_type=F32).astype(o_ref.dtype)

        if na:
            @pl.when(i == nm - 1)
            def _():
                ex.finish()

    row = lambda i: (i, 0)
    fixed = lambda i: (0, 0)
    outs = pl.pallas_call(
        body, name="in_proj_gather" if na else "in_proj", grid=(nm,),
        in_specs=[pl.BlockSpec((tm, D_MODEL), row), pl.BlockSpec((1, D_MODEL), fixed),
                  pl.BlockSpec((D_MODEL, NP), fixed, pipeline_mode=pl.Buffered(1))]
        + [HBM_SPEC] * na,
        out_specs=[pl.BlockSpec((tm, NP), row), pl.BlockSpec((tm, D_MODEL), row)]
        + [HBM_SPEC] * na,
        out_shape=[_sds((s, NP), PROJ_DTYPE), _sds((s, D_MODEL), MXU_DTYPE)]
        + _exchange_shapes(shards, True),
        scratch_shapes=_Exchange.semaphores(na) if na else [],
        compiler_params=_params(1),
    )(x, g, w, *shards)
    return outs[0], outs[1], list(outs[2:])


def _swa_slopes():
    return [2.0 ** (-8.0 * (h + 1) / SWA_HEADS) for h in range(SWA_HEADS)]


SWA_STACK = 8 * BLOCK


def _swa_head(j, a):
    return 2 * (4 * j + a % 4) + a // 4


def _swa_operands(kv_p, kv_c):
    kk = jnp.concatenate([kv_p[:, :128], kv_c[:, :128]], axis=0)
    vv = jnp.concatenate([kv_p[:, 128:], kv_c[:, 128:]], axis=0)
    left = lax.broadcasted_iota(jnp.int32, (2 * BLOCK, 128), 1) < 64

    def heads(t):
        return [jnp.where(left, t, 0.0), jnp.where(left, pltpu.roll(t, 64, 1), 0.0)]

    return heads(kk), heads(vv), left


def _swa_by_block(vals):
    a = lax.broadcasted_iota(jnp.int32, (1, SWA_STACK), 1) >> 7
    row = jnp.full((1, SWA_STACK), vals[7], F32)
    for t in range(6, -1, -1):
        row = jnp.where(a == t, vals[t], row)
    return row


def _swa_bias(n, j):
    slopes = _swa_slopes()
    ki = lax.broadcasted_iota(jnp.int32, (2 * BLOCK, SWA_STACK), 0)
    r = lax.broadcasted_iota(jnp.int32, (2 * BLOCK, SWA_STACK), 1)
    delta = BLOCK + (r & (BLOCK - 1)) - ki
    valid = (delta >= 0) & (delta < BLOCK) & ((n - 1) * BLOCK + ki >= 0)
    slope = _swa_by_block([slopes[_swa_head(j, a)] for a in range(8)])
    return jnp.where(valid, -slope * delta.astype(F32), NEG)


def _swa_fill_bias(n, bias_ref):
    @pl.when(n <= 1)
    def _():
        for j in range(2):
            bias_ref[j] = _swa_bias(n, j)


def _swa_sink_row(sink_ref, j):
    return _swa_by_block([sink_ref[_swa_head(j, a)] for a in range(8)])


def _swa_pairs(ref, j):
    return jnp.concatenate([ref[:, 128 * (4 * j + a):128 * (4 * j + a + 1)] for a in range(4)],
                           axis=0).astype(F32)


def _swa_stack(pairs):
    return jnp.concatenate([pairs, pltpu.roll(pairs, 64, 1)], axis=0)


def _swa_unstack(t):
    left = lax.broadcasted_iota(jnp.int32, (4 * BLOCK, 128), 1) < 64
    return jnp.where(left, t[0:4 * BLOCK], pltpu.roll(t[4 * BLOCK:8 * BLOCK], 64, 1))


def _swa_softmax(qs, kmat, bias, sink):
    sc = _mm_nt(kmat, qs) + bias
    m = jnp.maximum(jnp.max(sc, axis=0, keepdims=True), sink)
    ex = jnp.exp(sc - m)
    es = jnp.exp(sink - m)
    return ex, es, 1.0 / (jnp.sum(ex, axis=0, keepdims=True) + es)


def _swa_fwd(proj, sinks):
    s = proj.shape[0]
    nb = s // BLOCK

    def body(sink_ref, q_ref, kp_ref, kc_ref, o_ref, bias_ref):
        n = pl.program_id(0)
        _swa_fill_bias(n, bias_ref)
        k, v, _ = _swa_operands(kp_ref[...].astype(F32), kc_ref[...].astype(F32))
        for j in range(2):
            qs = _swa_stack(_swa_pairs(q_ref, j) * 0.125)
            ex, _, inv = _swa_softmax(qs, k[j], bias_ref[j], _swa_sink_row(sink_ref, j))
            o_t = _mm(v[j].T, ex) * inv
            out = jnp.concatenate([o_t[0:64, 0:4 * BLOCK], o_t[0:64, 4 * BLOCK:8 * BLOCK]], axis=0).T
            for a in range(4):
                o_ref[:, 128 * (4 * j + a):128 * (4 * j + a + 1)] = out[128 * a:128 * (a + 1)].astype(
                    o_ref.dtype)

    return pl.pallas_call(
        body, name="swa_fwd", grid=(nb,),
        in_specs=[pl.BlockSpec(memory_space=pltpu.SMEM),
                  pl.BlockSpec((BLOCK, 1024), lambda n: (n, 0)),
                  pl.BlockSpec((BLOCK, 256), lambda n: (jnp.maximum(n - 1, 0), KVA // 256)),
                  pl.BlockSpec((BLOCK, 256), lambda n: (n, KVA // 256))],
        out_specs=pl.BlockSpec((BLOCK, 1024), lambda n: (n, 0)),
        out_shape=_sds((s, 1024), ATTN_DTYPE),
        scratch_shapes=[pltpu.VMEM((2, 2 * BLOCK, SWA_STACK), F32)],
        compiler_params=_params(1),
    )(sinks, proj, proj, proj)


def _mla_prep(proj, gq, gkv, w_q, w_kv, tabs):
    s = proj.shape[0]
    tm = min(512, s)
    c, s1, s2 = tabs

    def body(p_ref, gq_ref, gkv_ref, wq_ref, wkv_ref, c_ref, s1_ref, s2_ref,
             q_ref, k_ref, v_ref, vt_ref):
        cqh, _ = _rownorm(p_ref[:, 0:384].astype(F32))
        ckvh, _ = _rownorm(p_ref[:, 384:640].astype(F32))
        q = _mm(cqh * gq_ref[...], wq_ref[...])
        kv = _mm(ckvh * gkv_ref[...], wkv_ref[...])
        cc, ss1, ss2 = c_ref[...], s1_ref[...], s2_ref[...]
        krr = _rope(p_ref[:, 640:768].astype(F32), cc, ss1, ss2).astype(k_ref.dtype)
        for h in range(MLA_HEADS):
            q_ref[h, :, 0:128] = q[:, 128 * h:128 * (h + 1)].astype(q_ref.dtype)
            q_ref[h, :, 128:256] = _rope(q[:, 1024 + 128 * h:1024 + 128 * (h + 1)],
                                         cc, ss1, ss2).astype(q_ref.dtype)
            k_ref[h, :, 0:128] = kv[:, 128 * h:128 * (h + 1)].astype(k_ref.dtype)
            k_ref[h, :, 128:256] = krr
            vv = kv[:, 1024 + 128 * h:1024 + 128 * (h + 1)]
            v_ref[h] = vv.astype(v_ref.dtype)
            vt_ref[h, 0:128, :] = vv.T.astype(vt_ref.dtype)
            vt_ref[h, 128:256, :] = jnp.ones((128, tm), vt_ref.dtype)

    row = lambda i: (i, 0)
    fixed = lambda i: (0, 0)
    return pl.pallas_call(
        body, name="mla_prep", grid=(s // tm,),
        in_specs=[pl.BlockSpec((tm, 768), lambda i: (i, CQ // 768)),
                  pl.BlockSpec((1, Q_RANK), fixed), pl.BlockSpec((1, KV_RANK), fixed),
                  pl.BlockSpec((Q_RANK, 2048), fixed), pl.BlockSpec((KV_RANK, 2048), fixed),
                  pl.BlockSpec((tm, 128), row), pl.BlockSpec((tm, 128), row),
                  pl.BlockSpec((tm, 128), row)],
        out_specs=[pl.BlockSpec((MLA_HEADS, tm, 256), lambda i: (0, i, 0)),
                   pl.BlockSpec((MLA_HEADS, tm, 256), lambda i: (0, i, 0)),
                   pl.BlockSpec((MLA_HEADS, tm, 128), lambda i: (0, i, 0)),
                   pl.BlockSpec((MLA_HEADS, 256, tm), lambda i: (0, 0, i))],
        out_shape=[_sds((MLA_HEADS, s, 256), MXU_DTYPE), _sds((MLA_HEADS, s, 256), MXU_DTYPE),
                   _sds((MLA_HEADS, s, 128), MXU_DTYPE), _sds((MLA_HEADS, 256, s), MXU_DTYPE)],
        compiler_params=_params(1),
    )(proj, gq, gkv, w_q, w_kv, c, s1, s2)


def _scores_t(k, q, t, diagonal):
    sc = _mm_nt(k, q)
    if diagonal:
        key = lax.broadcasted_iota(jnp.int32, (t, t), 0)
        query = lax.broadcasted_iota(jnp.int32, (t, t), 1)
        sc = jnp.where(key <= query, sc, NEG)
    return sc


def _mla_fwd(qh, kh, vth, shards):
    s = qh.shape[1]
    t = min(MLA_TILE, s)
    nq = s // t
    na = len(shards)

    def body(*refs):
        q_ref, k_ref, vt_ref = refs[:3]
        o_ref, lse_ref = refs[3 + na:5 + na]
        m_ref, acc_ref = refs[5 + 2 * na:7 + 2 * na]
        h, i = pl.program_id(0), pl.program_id(1)
        if na:
            ex = _Gather(refs[3:3 + na], refs[5 + na:5 + 2 * na], refs[7 + 2 * na:])

            @pl.when((h == 0) & (i == 0))
            def _():
                ex.start()

            @pl.when((h == GATHER_FORWARD_HEAD) & (i == 0))
            def _():
                ex.forward()

        m_ref[...] = jnp.full(m_ref.shape, NEG, F32)
        acc_ref[...] = jnp.zeros(acc_ref.shape, F32)

        def step(start, width, diagonal):
            keys = pl.ds(pl.multiple_of(start, t), width)
            if diagonal:
                sc = _scores_t(k_ref[keys, :], q_ref[...], t, True)
            else:
                sc = _mm_nt(k_ref[keys, :], q_ref[...])
            m_prev = m_ref[...]
            m_new = jnp.maximum(m_prev, jnp.max(sc, axis=0, keepdims=True))
            alpha = jnp.exp2((m_prev - m_new) * MLA_C2)
            p = jnp.exp2((sc - m_new[0:1, :]) * MLA_C2)
            acc_ref[...] = alpha[0:1, :] * acc_ref[...] + _mm(vt_ref[:, keys], p)
            m_ref[...] = m_new

        def below_diagonal(jj, carry):
            step(jj * (2 * t), 2 * t, False)
            return carry

        lax.fori_loop(0, i // 2, below_diagonal, 0)

        @pl.when(i % 2 == 1)
        def _():
            step((i - 1) * t, t, False)

        step(i * t, t, True)
        l = acc_ref[128:136, :]
        o_ref[...] = (acc_ref[0:128, :] / l[0:1, :]).T.astype(o_ref.dtype)
        lse_ref[...] = m_ref[...] * MLA_C2 + jnp.log2(l)

        if na:
            @pl.when((h == MLA_HEADS - 1) & (i == nq - 1))
            def _():
                ex.finish()

    head = lambda h, i: (h, 0, 0)
    outs = pl.pallas_call(
        body, name="mla_fwd_gather" if na else "mla_fwd", grid=(MLA_HEADS, nq),
        in_specs=[pl.BlockSpec((None, t, 256), lambda h, i: (h, i, 0)),
                  pl.BlockSpec((None, s, 256), head),
                  pl.BlockSpec((None, 256, s), head)] + [HBM_SPEC] * na,
        out_specs=[pl.BlockSpec((t, 128), lambda h, i: (i, h)),
                   pl.BlockSpec((None, 8, t), lambda h, i: (h, 0, i))] + [HBM_SPEC] * na,
        out_shape=[_sds((s, 1024), ATTN_DTYPE), _sds((MLA_HEADS, 8, s), F32)]
        + _exchange_shapes(shards, True),
        scratch_shapes=[pltpu.VMEM((8, t), F32), pltpu.VMEM((256, t), F32)]
        + (_Exchange.semaphores(na) if na else []),
        compiler_params=_params(2),
    )(qh, kh, vth, *shards)
    return outs[0], outs[1], list(outs[2:])


def _silu_parts(g):
    sg = jax.nn.sigmoid(g)
    return g * sg, sg * (1.0 + g * (1.0 - sg))


def _out_proj(x, proj, swa, mla, w_out):
    s = x.shape[0]
    tm = min(512, s)

    def body(x_ref, ga_ref, gb_ref, a_ref, b_ref, w_ref, xo_ref, y_ref):
        sa, _ = _silu_parts(ga_ref[...].astype(F32))
        sb, _ = _silu_parts(gb_ref[...].astype(F32))
        y_ref[:, 0:1024] = (a_ref[...].astype(F32) * sa).astype(y_ref.dtype)
        y_ref[:, 1024:2048] = (b_ref[...].astype(F32) * sb).astype(y_ref.dtype)
        xo_ref[...] = x_ref[...] + jnp.dot(y_ref[...], w_ref[...], preferred_element_type=F32)

    row = lambda i: (i, 0)
    return pl.pallas_call(
        body, name="out_proj", grid=(s // tm,),
        in_specs=[pl.BlockSpec((tm, D_MODEL), row),
                  pl.BlockSpec((tm, 1024), lambda i: (i, GA // 1024)),
                  pl.BlockSpec((tm, 1024), lambda i: (i, GB // 1024)),
                  pl.BlockSpec((tm, 1024), row), pl.BlockSpec((tm, 1024), row),
                  pl.BlockSpec((D_MODEL, D_MODEL), lambda i: (0, 0), pipeline_mode=pl.Buffered(1))],
        out_specs=[pl.BlockSpec((tm, D_MODEL), row), pl.BlockSpec((tm, D_MODEL), row)],
        out_shape=[_sds((s, D_MODEL), F32), _sds((s, D_MODEL), MXU_DTYPE)],
        compiler_params=_params(1),
    )(x, proj, proj, swa, mla, w_out)


def _final_loss(x, tgt, g):
    s = x.shape[0]
    tm = min(512, s)

    def body(x_ref, t_ref, g_ref, dx_ref, dxb_ref, dg_ref, loss_ref):
        @pl.when(pl.program_id(0) == 0)
        def _():
            dg_ref[...] = jnp.zeros(dg_ref.shape, F32)
            loss_ref[...] = jnp.zeros(loss_ref.shape, F32)
        xh, r = _rownorm(x_ref[...])
        gg = g_ref[...]
        err = xh * gg - t_ref[...]
        per_row = jnp.mean(err * err, axis=-1, keepdims=True)
        loss_ref[...] += 0.5 * jnp.sum(per_row, axis=0, keepdims=True)
        dy = err * (1.0 / D_MODEL)
        dg_ref[...] += jnp.sum(dy * xh, axis=0, keepdims=True)
        dx = _rownorm_bwd(dy * gg, xh, r)
        dx_ref[...] = dx
        dxb_ref[...] = dx.astype(dxb_ref.dtype)

    row = lambda i: (i, 0)
    fixed = lambda i: (0, 0)
    return pl.pallas_call(
        body, name="final_loss", grid=(s // tm,),
        in_specs=[pl.BlockSpec((tm, D_MODEL), row), pl.BlockSpec((tm, D_MODEL), row),
                  pl.BlockSpec((1, D_MODEL), fixed)],
        out_specs=[pl.BlockSpec((tm, D_MODEL), row), pl.BlockSpec((tm, D_MODEL), row),
                   pl.BlockSpec((1, D_MODEL), fixed), pl.BlockSpec((8, 128), fixed)],
        out_shape=[_sds((s, D_MODEL), F32), _sds((s, D_MODEL), MXU_DTYPE), _sds((1, D_MODEL), F32),
                   _sds((8, 128), F32)],
        compiler_params=_params(1),
    )(x, tgt, g)


def _out_proj_bwd(dx, proj, swa, mla, w_out):
    s = dx.shape[0]
    tm = min(512, s)

    def body(dx_ref, ga_ref, gb_ref, a_ref, b_ref, w_ref, doa_ref, dob_ref, dg_ref, dlt_ref):
        dx = dx_ref[...].astype(MXU_DTYPE)
        dya = _mm_nt(dx, w_ref[0:1024, :])
        sa, dsa = _silu_parts(ga_ref[...].astype(F32))
        doa_ref[...] = dya * sa
        dg_ref[:, 0:1024] = (dya * a_ref[...].astype(F32) * dsa).astype(dg_ref.dtype)
        dyb = _mm_nt(dx, w_ref[1024:2048, :])
        sb, dsb = _silu_parts(gb_ref[...].astype(F32))
        b = b_ref[...].astype(F32)
        dob = dyb * sb
        dob_ref[...] = dob.astype(dob_ref.dtype)
        dg_ref[:, 1024:2048] = (dyb * b * dsb).astype(dg_ref.dtype)
        prod = dob * b
        for h in range(MLA_HEADS):
            dlt = jnp.sum(prod[:, 128 * h:128 * (h + 1)], axis=1, keepdims=True)
            dlt_ref[h] = jnp.broadcast_to(dlt, (tm, 128)).T[0:8, :]

    row = lambda i: (i, 0)
    return pl.pallas_call(
        body, name="out_proj_bwd", grid=(s // tm,),
        in_specs=[pl.BlockSpec((tm, D_MODEL), row),
                  pl.BlockSpec((tm, 1024), lambda i: (i, GA // 1024)),
                  pl.BlockSpec((tm, 1024), lambda i: (i, GB // 1024)),
                  pl.BlockSpec((tm, 1024), row), pl.BlockSpec((tm, 1024), row),
                  pl.BlockSpec((D_MODEL, D_MODEL), lambda i: (0, 0), pipeline_mode=pl.Buffered(1))],
        out_specs=[pl.BlockSpec((tm, 1024), row), pl.BlockSpec((tm, 1024), row),
                   pl.BlockSpec((tm, D_MODEL), row),
                   pl.BlockSpec((MLA_HEADS, 8, tm), lambda i: (0, 0, i))],
        out_shape=[_sds((s, 1024), F32), _sds((s, 1024), MXU_DTYPE), _sds((s, D_MODEL), MXU_DTYPE),
                   _sds((MLA_HEADS, 8, s), F32)],
        compiler_params=_params(1),
    )(dx, proj, proj, swa, mla, w_out)


def _matmul_tn(a, b, name):
    s, m = a.shape
    n = b.shape[1]
    tm, tn, tk = min(1024, m), min(1024, n), min(2048, s)
    nk = s // tk

    def body(a_ref, b_ref, o_ref, acc_ref):
        k = pl.program_id(2)

        @pl.when(k == 0)
        def _():
            acc_ref[...] = jnp.zeros(acc_ref.shape, F32)
        acc_ref[...] += _mm_tn(a_ref[...], b_ref[...])

        @pl.when(k == nk - 1)
        def _():
            o_ref[...] = acc_ref[...].astype(o_ref.dtype)

    return pl.pallas_call(
        body, name=name, grid=(m // tm, n // tn, nk),
        in_specs=[pl.BlockSpec((tk, tm), lambda i, j, k: (k, i)),
                  pl.BlockSpec((tk, tn), lambda i, j, k: (k, j))],
        out_specs=pl.BlockSpec((tm, tn), lambda i, j, k: (i, j)),
        out_shape=_sds((m, n), GRAD_DTYPE),
        scratch_shapes=[pltpu.VMEM((tm, tn), F32)],
        compiler_params=_params(3),
    )(a, b)


def _grad_w_in(h, dqa, dgate, dlat, dkva, blocks):
    s = h.shape[0]
    tm, tn, tk = 1024, 1024, min(2048, s)
    nk = s // tk
    grid = (D_MODEL // tm, NP // tn, nk)
    na = len(blocks)

    def body(*refs):
        a_ref, dqa_ref, dg8_ref, dlat_ref, dkva_ref = refs[:5]
        o_ref = refs[5 + na]
        acc_ref = refs[6 + 2 * na]
        i, j, k = pl.program_id(0), pl.program_id(1), pl.program_id(2)
        if na:
            ex = _Exchange(refs[5:5 + na], refs[6 + na:6 + 2 * na], refs[7 + 2 * na:], gather=False)

            @pl.when((i == 0) & (j == 0) & (k == 0))
            def _():
                ex.start()

        @pl.when(k == 0)
        def _():
            acc_ref[...] = jnp.zeros(acc_ref.shape, F32)

        @pl.when(j == QA // tn)
        def _():
            acc_ref[...] += _mm_tn(a_ref[...], dqa_ref[...])

        @pl.when((j == GA // tn) | (j == GB // tn))
        def _():
            acc_ref[...] += _mm_tn(a_ref[...], dg8_ref[...])

        @pl.when(j == CQ // tn)
        def _():
            acc_ref[:, 0:768] += _mm_tn(a_ref[...], dlat_ref[...])
            acc_ref[:, 768:1024] += _mm_tn(a_ref[...], dkva_ref[...])

        @pl.when(k == nk - 1)
        def _():
            o_ref[...] = acc_ref[...].astype(o_ref.dtype)

        if na:
            @pl.when((i == grid[0] - 1) & (j == grid[1] - 1) & (k == nk - 1))
            def _():
                ex.wait()

    def when(group, width):
        return pl.BlockSpec((tk, width), lambda i, j, k: (jnp.where(j == group, k, 0), 0))

    outs = pl.pallas_call(
        body, name="grad_w_in_scatter" if na else "grad_w_in", grid=grid,
        in_specs=[pl.BlockSpec((tk, tm), lambda i, j, k: (k, i)),
                  when(QA // tn, 1024),
                  pl.BlockSpec((tk, 1024), lambda i, j, k: (
                      jnp.where((j == GA // tn) | (j == GB // tn), k, 0),
                      jnp.clip(j - GA // tn, 0, 1))),
                  when(CQ // tn, 768), when(CQ // tn, 256)] + [HBM_SPEC] * na,
        out_specs=[pl.BlockSpec((tm, tn), lambda i, j, k: (i, j))] + [HBM_SPEC] * na,
        out_shape=[_sds((D_MODEL, NP), GRAD_DTYPE)] + _exchange_shapes(blocks, False),
        scratch_shapes=[pltpu.VMEM((tm, tn), F32)] + (_Exchange.semaphores(na) if na else []),
        compiler_params=_params(3),
    )(h, dqa, dgate, dlat, dkva, *blocks)
    return outs[0], list(outs[1:])


def _swa_bwd(proj, sinks, do, o):
    s = proj.shape[0]
    nb = s // BLOCK

    def body(sink_ref, q_ref, kp_ref, kc_ref, do_ref, o_ref, dq_ref, dkv_ref, dsink_ref,
             carry_ref, bias_ref):
        n = pl.program_id(0)
        _swa_fill_bias(n, bias_ref)

        @pl.when(n == 0)
        def _():
            carry_ref[...] = jnp.zeros(carry_ref.shape, F32)
            dsink_ref[...] = jnp.zeros(dsink_ref.shape, F32)

        @pl.when(n < nb)
        def _():
            k, v, left = _swa_operands(kp_ref[...].astype(F32), kc_ref[...].astype(F32))
            lane_s = lax.broadcasted_iota(jnp.int32, (8, 128), 1)
            dk, dv = [None, None], [None, None]
            dsink = jnp.zeros((8, 128), F32)
            for j in range(2):
                qs = _swa_stack(_swa_pairs(q_ref, j) * 0.125)
                do_pairs = _swa_pairs(do_ref, j)
                dos = _swa_stack(do_pairs)
                prod_t = (do_pairs * _swa_pairs(o_ref, j)).T
                dlt = jnp.concatenate([jnp.sum(prod_t[0:64], axis=0, keepdims=True),
                                       jnp.sum(prod_t[64:128], axis=0, keepdims=True)], axis=1)
                ex, es, inv = _swa_softmax(qs, k[j], bias_ref[j], _swa_sink_row(sink_ref, j))
                p = ex * inv
                ds = p * (_mm_nt(v[j], dos) - dlt)
                sink_term = es * inv * dlt
                for a in range(8):
                    dsh = -jnp.sum(sink_term[:, 128 * a:128 * (a + 1)], axis=1, keepdims=True)
                    dsink = dsink + jnp.where(lane_s == _swa_head(j, a), dsh, 0.0)
                dq = _swa_unstack(_mm_tn(ds, k[j]))
                dv[j] = _mm(p, dos)
                dk[j] = _mm(ds, qs)
                for a in range(4):
                    cols = slice(128 * (4 * j + a), 128 * (4 * j + a + 1))
                    dq_ref[:, cols] = (dq[128 * a:128 * (a + 1)] * 0.125).astype(dq_ref.dtype)

            def merge(t):
                return jnp.where(left, t[0], pltpu.roll(t[1], 64, 1))

            contrib = jnp.concatenate([merge(dk), merge(dv)], axis=1)
            dkv_ref[...] = (carry_ref[...] + contrib[0:BLOCK]).astype(dkv_ref.dtype)
            carry_ref[...] = contrib[BLOCK:2 * BLOCK]
            dsink_ref[...] += dsink

        @pl.when(n == nb)
        def _():
            dkv_ref[...] = carry_ref[...].astype(dkv_ref.dtype)

    cur = lambda n: (jnp.minimum(n, nb - 1), 0)
    return pl.pallas_call(
        body, name="swa_bwd", grid=(nb + 1,),
        in_specs=[pl.BlockSpec(memory_space=pltpu.SMEM),
                  pl.BlockSpec((BLOCK, 1024), cur),
                  pl.BlockSpec((BLOCK, 256), lambda n: (jnp.clip(n - 1, 0, nb - 1), KVA // 256)),
                  pl.BlockSpec((BLOCK, 256), lambda n: (jnp.minimum(n, nb - 1), KVA // 256)),
                  pl.BlockSpec((BLOCK, 1024), cur), pl.BlockSpec((BLOCK, 1024), cur)],
        out_specs=[pl.BlockSpec((BLOCK, 1024), cur),
                   pl.BlockSpec((BLOCK, 256), lambda n: (jnp.maximum(n - 1, 0), 0)),
                   pl.BlockSpec((8, 128), lambda n: (0, 0))],
        out_shape=[_sds((s, 1024), MXU_DTYPE), _sds((s, 256), MXU_DTYPE), _sds((8, 128), F32)],
        scratch_shapes=[pltpu.VMEM((BLOCK, 256), F32), pltpu.VMEM((2, 2 * BLOCK, SWA_STACK), F32)],
        compiler_params=_params(1),
    )(sinks, proj, proj, proj, do, o)


def _mla_bwd(qh, kh, vh, do, dlt, lse, blocks):
    s = qh.shape[1]
    t = min(MLA_TILE, s)
    nq = s // t

    na = len(blocks)

    def body(*refs):
        q_ref, k_ref, v_ref, do_ref, dlt_ref, lse_ref = refs[:6]
        dq_ref, dk_ref, dv_ref = refs[6 + na:9 + na]
        h, j = pl.program_id(0), pl.program_id(1)
        if na:
            ex = _Exchange(refs[6:6 + na], refs[9 + na:9 + 2 * na], refs[9 + 2 * na:], gather=False)

            @pl.when((h == 0) & (j == 0))
            def _():
                ex.start()

        def step(start, width, diagonal):
            rows = pl.ds(pl.multiple_of(start, t), width)
            q, k, dout = q_ref[rows, :], k_ref[...], do_ref[rows, :]
            sc = _scores_t(k, q, t, True) if diagonal else _mm_nt(k, q)
            p = jnp.exp2(sc * MLA_C2 - lse_ref[0:1, rows])
            dv = _mm(p, dout)
            ds = p * (_mm_nt(v_ref[...], dout) - dlt_ref[0:1, rows])
            dk = _mm(ds, q)
            dq = _mm_tn(ds, k)
            return rows, dq, dk, dv

        rows, dq, dk, dv = step(j * t, t, True)
        dk_ref[...] = dk
        dv_ref[...] = dv

        @pl.when(j == 0)
        def _():
            dq_ref[rows, :] = dq * MLA_SCALE

        @pl.when(j > 0)
        def _():
            dq_ref[rows, :] = (dq_ref[rows, :] + dq) * MLA_SCALE

        def above_diagonal(start, width):
            rows, dq, dk, dv = step(start, width, False)
            dk_ref[...] += dk
            dv_ref[...] += dv

            @pl.when(j == 0)
            def _():
                dq_ref[rows, :] = dq

            @pl.when(j > 0)
            def _():
                dq_ref[rows, :] += dq

        n_above = nq - 1 - j

        def pair(jj, carry):
            above_diagonal((j + 1 + 2 * jj) * t, 2 * t)
            return carry

        lax.fori_loop(0, n_above // 2, pair, 0)

        @pl.when(n_above % 2 == 1)
        def _():
            above_diagonal((nq - 1) * t, t)

        dk_ref[...] *= MLA_SCALE

        if na:
            @pl.when((h == MLA_HEADS - 1) & (j == nq - 1))
            def _():
                ex.wait()

    head = lambda h, j: (h, 0, 0)
    kv_map = lambda h, j: (h, j, 0)
    outs = pl.pallas_call(
        body, name="mla_bwd_scatter" if na else "mla_bwd", grid=(MLA_HEADS, nq),
        in_specs=[pl.BlockSpec((None, s, 256), head), pl.BlockSpec((None, t, 256), kv_map),
                  pl.BlockSpec((None, t, 128), kv_map),
                  pl.BlockSpec((s, 128), lambda h, j: (0, h)),
                  pl.BlockSpec((None, 8, s), head), pl.BlockSpec((None, 8, s), head)]
        + [HBM_SPEC] * na,
        out_specs=[pl.BlockSpec((None, s, 256), head),
                   pl.BlockSpec((None, t, 256), kv_map), pl.BlockSpec((None, t, 128), kv_map)]
        + [HBM_SPEC] * na,
        out_shape=[_sds((MLA_HEADS, s, 256), F32), _sds((MLA_HEADS, s, 256), F32),
                   _sds((MLA_HEADS, s, 128), F32)] + _exchange_shapes(blocks, False),
        scratch_shapes=_Exchange.semaphores(na) if na else [],
        compiler_params=_params(2),
    )(qh, kh, vh, do, dlt, lse, *blocks)
    return outs[0], outs[1], outs[2], list(outs[3:])


def _mla_prep_bwd(dqh, dkh, dvh, proj, gq, gkv, w_q, w_kv, tabs):
    s = proj.shape[0]
    tm = min(256, s)
    nm = s // tm
    c, s1, s2 = tabs

    def body(dq_ref, dk_ref, dv_ref, p_ref, gq_ref, gkv_ref, wq_ref, wkv_ref,
             c_ref, s1_ref, s2_ref, dp_ref, dwq_ref, dwkv_ref, dgq_ref, dgkv_ref,
             dqf_ref, dkvf_ref, dwq_acc, dwkv_acc):
        @pl.when(pl.program_id(0) == 0)
        def _():
            dgq_ref[...] = jnp.zeros(dgq_ref.shape, F32)
            dgkv_ref[...] = jnp.zeros(dgkv_ref.shape, F32)
            dwq_acc[...] = jnp.zeros(dwq_acc.shape, F32)
            dwkv_acc[...] = jnp.zeros(dwkv_acc.shape, F32)
        cc, ns1, ns2 = c_ref[...], -s1_ref[...], -s2_ref[...]
        dkr = jnp.zeros((tm, 128), F32)
        for h in range(MLA_HEADS):
            dqf_ref[:, 128 * h:128 * (h + 1)] = dq_ref[h, :, 0:128].astype(dqf_ref.dtype)
            dqf_ref[:, 1024 + 128 * h:1024 + 128 * (h + 1)] = _rope(
                dq_ref[h, :, 128:256], cc, ns1, ns2).astype(dqf_ref.dtype)
            dkvf_ref[:, 128 * h:128 * (h + 1)] = dk_ref[h, :, 0:128].astype(dkvf_ref.dtype)
            dkvf_ref[:, 1024 + 128 * h:1024 + 128 * (h + 1)] = dv_ref[h].astype(dkvf_ref.dtype)
            dkr = dkr + dk_ref[h, :, 128:256]
        dcqn = _mm_nt(dqf_ref[...], wq_ref[...])
        dckvn = _mm_nt(dkvf_ref[...], wkv_ref[...])
        cqh, rq = _rownorm(p_ref[:, 0:384].astype(F32))
        ckvh, rkv = _rownorm(p_ref[:, 384:640].astype(F32))
        dgq_ref[...] += jnp.sum(dcqn * cqh, axis=0, keepdims=True)
        dgkv_ref[...] += jnp.sum(dckvn * ckvh, axis=0, keepdims=True)
        dp_ref[:, 0:384] = _rownorm_bwd(dcqn * gq_ref[...], cqh, rq).astype(dp_ref.dtype)
        dp_ref[:, 384:640] = _rownorm_bwd(dckvn * gkv_ref[...], ckvh, rkv).astype(dp_ref.dtype)
        dp_ref[:, 640:768] = _rope(dkr, cc, ns1, ns2).astype(dp_ref.dtype)
        dwq_acc[...] += _mm_tn(cqh * gq_ref[...], dqf_ref[...])
        dwkv_acc[...] += _mm_tn(ckvh * gkv_ref[...], dkvf_ref[...])

        @pl.when(pl.program_id(0) == nm - 1)
        def _():
            dwq_ref[...] = dwq_acc[...].astype(dwq_ref.dtype)
            dwkv_ref[...] = dwkv_acc[...].astype(dwkv_ref.dtype)

    row = lambda i: (i, 0)
    fixed = lambda i: (0, 0)
    head = lambda i: (0, i, 0)
    return pl.pallas_call(
        body, name="mla_prep_bwd", grid=(nm,),
        in_specs=[pl.BlockSpec((MLA_HEADS, tm, 256), head), pl.BlockSpec((MLA_HEADS, tm, 256), head),
                  pl.BlockSpec((MLA_HEADS, tm, 128), head),
                  pl.BlockSpec((tm, 768), lambda i: (i, CQ // 768)),
                  pl.BlockSpec((1, Q_RANK), fixed), pl.BlockSpec((1, KV_RANK), fixed),
                  pl.BlockSpec((Q_RANK, 2048), fixed), pl.BlockSpec((KV_RANK, 2048), fixed),
                  pl.BlockSpec((tm, 128), row), pl.BlockSpec((tm, 128), row),
                  pl.BlockSpec((tm, 128), row)],
        out_specs=[pl.BlockSpec((tm, 768), row), pl.BlockSpec((Q_RANK, 2048), fixed),
                   pl.BlockSpec((KV_RANK, 2048), fixed),
                   pl.BlockSpec((1, Q_RANK), fixed), pl.BlockSpec((1, KV_RANK), fixed)],
        out_shape=[_sds((s, 768), MXU_DTYPE), _sds((Q_RANK, 2048), GRAD_DTYPE),
                   _sds((KV_RANK, 2048), GRAD_DTYPE),
                   _sds((1, Q_RANK), F32), _sds((1, KV_RANK), F32)],
        scratch_shapes=[pltpu.VMEM((tm, 2048), MXU_DTYPE), pltpu.VMEM((tm, 2048), MXU_DTYPE),
                        pltpu.VMEM((Q_RANK, 2048), F32), pltpu.VMEM((KV_RANK, 2048), F32)],
        compiler_params=_params(1),
    )(dqh, dkh, dvh, proj, gq, gkv, w_q, w_kv, c, s1, s2)


def _in_proj_bwd(dqa, dgate, dlat, dkva, w, x, dx_out, g, blocks):
    s = x.shape[0]
    tm = min(256, s)
    nm = s // tm
    na = len(blocks)

    def body(*refs):
        dqa_ref, dg8_ref, dlat_ref, dkva_ref, w_ref, x_ref, dxo_ref, g_ref = refs[:8]
        dx_ref, dxb_ref, dg_ref = refs[8 + na:11 + na]
        if na:
            ex = _Exchange(refs[8:8 + na], refs[11 + na:11 + 2 * na], refs[11 + 2 * na:], gather=False)

        @pl.when(pl.program_id(0) == 0)
        def _():
            dg_ref[...] = jnp.zeros(dg_ref.shape, F32)
            if na:
                ex.start()

        dh = (_mm_nt(dqa_ref[...], w_ref[:, QA:QA + 1024])
              + _mm_nt(dg8_ref[:, 0:1024], w_ref[:, GA:GA + 1024])
              + _mm_nt(dg8_ref[:, 1024:2048], w_ref[:, GB:GB + 1024])
              + _mm_nt(dlat_ref[...], w_ref[:, CQ:CQ + 768])
              + _mm_nt(dkva_ref[...], w_ref[:, KVA:KVA + 256]))
        xh, r = _rownorm(x_ref[...])
        dg_ref[...] += jnp.sum(dh * xh, axis=0, keepdims=True)
        dx = dxo_ref[...] + _rownorm_bwd(dh * g_ref[...], xh, r)
        dx_ref[...] = dx
        dxb_ref[...] = dx.astype(dxb_ref.dtype)

        if na:
            @pl.when(pl.program_id(0) == nm - 1)
            def _():
                ex.wait()

    row = lambda i: (i, 0)
    fixed = lambda i: (0, 0)
    outs = pl.pallas_call(
        body, name="in_proj_bwd_scatter" if na else "in_proj_bwd", grid=(nm,),
        in_specs=[pl.BlockSpec((tm, 1024), row), pl.BlockSpec((tm, 2048), row),
                  pl.BlockSpec((tm, 768), row), pl.BlockSpec((tm, 256), row),
                  pl.BlockSpec((D_MODEL, NP), fixed, pipeline_mode=pl.Buffered(1)),
                  pl.BlockSpec((tm, D_MODEL), row), pl.BlockSpec((tm, D_MODEL), row),
                  pl.BlockSpec((1, D_MODEL), fixed)] + [HBM_SPEC] * na,
        out_specs=[pl.BlockSpec((tm, D_MODEL), row), pl.BlockSpec((tm, D_MODEL), row),
                   pl.BlockSpec((1, D_MODEL), fixed)] + [HBM_SPEC] * na,
        out_shape=[_sds((s, D_MODEL), F32), _sds((s, D_MODEL), MXU_DTYPE), _sds((1, D_MODEL), F32)]
        + _exchange_shapes(blocks, False),
        scratch_shapes=_Exchange.semaphores(na) if na else [],
        compiler_params=_params(1),
    )(dqa, dgate, dlat, dkva, w, x, dx_out, g, *blocks)
    return outs[0], outs[1], outs[2], list(outs[3:])


def _reduce_adamw(parts, w, m, v, name):
    n_layers = len(parts)
    rows, cols = parts[0].shape[1:]
    lanes = -(-cols // 128) * 128
    tr = rows
    for cand in (1024, 512, 256, 128, 64, 32, 16, 8):
        if rows % cand == 0 and N_DEV * cand * lanes * 4 <= 8 * 1024 * 1024:
            tr = cand
            break
    nr = rows // tr

    def body(*refs):
        p_refs = refs[:n_layers]
        w_ref, m_ref, v_ref, g_ref, d_ref, nm_ref, nv_ref = refs[n_layers:]
        layer = pl.program_id(0)
        for l in range(n_layers):
            @pl.when(layer == l)
            def _(l=l):
                g = p_refs[l][0].astype(F32)
                for k in range(1, N_DEV):
                    g = g + p_refs[l][k].astype(F32)
                update(g, w_ref, m_ref, v_ref, g_ref, d_ref, nm_ref, nv_ref)

    def update(g, w_ref, m_ref, v_ref, g_ref, d_ref, nm_ref, nv_ref):
        m2 = ADAM_B1 * m_ref[...] + (1.0 - ADAM_B1) * g
        v2 = ADAM_B2 * v_ref[...] + (1.0 - ADAM_B2) * (g * g)
        m_hat = m2 / (1.0 - ADAM_B1 ** ADAM_STEP)
        v_hat = v2 / (1.0 - ADAM_B2 ** ADAM_STEP)
        g_ref[...] = g
        d_ref[...] = -ADAM_LR * (m_hat / (jnp.sqrt(v_hat) + ADAM_EPS) + ADAM_WD * w_ref[...])
        nm_ref[...] = m2
        nv_ref[...] = v2

    def part_spec(l):
        return pl.BlockSpec((N_DEV, tr, cols), lambda layer, i: (0, jnp.where(layer == l, i, 0), 0))

    blk = pl.BlockSpec((tr, cols), lambda layer, i: (layer * nr + i, 0))
    return pl.pallas_call(
        body, name=name, grid=(n_layers, nr),
        in_specs=[part_spec(l) for l in range(n_layers)] + [blk, blk, blk],
        out_specs=[blk, blk, blk, blk],
        out_shape=[_sds((n_layers * rows, cols), F32)] * 4,
        compiler_params=_params(2),
    )(*parts, w, m, v)


def _position():
    x, y, c = lax.axis_index("x"), lax.axis_index("y"), lax.axis_index("c")
    return x, y, c


def _index(px, py, pc):
    return 4 * px + 2 * py + pc


HBM_SPEC = pl.BlockSpec(memory_space=pltpu.HBM)


class _Gather:
    def __init__(self, srcs, dsts, sems):
        self.srcs, self.dsts = srcs, dsts
        self.send_sems, self.recv_sems, self.local_sems = sems
        x, y, c = _position()
        self.c = c
        self.me, self.sibling = (x, y, c), (x, y, 1 - c)
        self.chips = [(1 - x, y), (x, 1 - y), (1 - x, 1 - y)]

    def _copy(self, a, k, block, to, own=False):
        slot = self.dsts[a].at[_index(*block)]
        return pltpu.make_async_remote_copy(
            src_ref=self.srcs[a] if own else slot, dst_ref=slot,
            send_sem=self.send_sems.at[7 * a + k], recv_sem=self.recv_sems.at[7 * a + k],
            device_id=to, device_id_type=MESH)

    def _local(self, a):
        return pltpu.make_async_copy(self.srcs[a], self.dsts[a].at[_index(*self.me)],
                                     self.local_sems.at[a])

    def _first(self, a):
        return [self._copy(a, 0, self.me, self.sibling, own=True)] + [
            self._copy(a, 1 + j, self.me, (*chip, self.c), own=True)
            for j, chip in enumerate(self.chips)]

    def _passed(self, a, j):
        return self._copy(a, 4 + j, (*self.chips[j], self.c), self.sibling)

    def start(self):
        for a in range(len(self.srcs)):
            self._local(a).start()
            for cp in self._first(a):
                cp.start()

    def forward(self):
        for j, chip in enumerate(self.chips):
            for a in range(len(self.srcs)):
                self._copy(a, 1 + j, (*chip, self.c), self.me).wait_recv()
                self._passed(a, j).start()

    def finish(self):
        for a in range(len(self.srcs)):
            self._copy(a, 0, self.sibling, self.me).wait_recv()
            for j, chip in enumerate(self.chips):
                self._copy(a, 4 + j, (*chip, 1 - self.c), self.me).wait_recv()
            for cp in self._first(a):
                cp.wait_send()
            for j in range(3):
                self._passed(a, j).wait_send()
            self._local(a).wait()


def _all_gather(shards, name):
    na = len(shards)

    def body(*refs):
        g = _Gather(refs[:na], refs[na:2 * na], refs[2 * na:])
        g.start()
        g.forward()
        g.finish()

    return pl.pallas_call(
        body, name=name,
        in_specs=[HBM_SPEC] * na, out_specs=[HBM_SPEC] * na,
        out_shape=_exchange_shapes(shards, True),
        scratch_shapes=_Exchange.semaphores(na),
    )(*shards)


class _Exchange:
    def __init__(self, srcs, dsts, sems, gather):
        self.srcs, self.dsts, self.gather = srcs, dsts, gather
        self.send_sems, self.recv_sems, self.local_sems = sems
        x, y, c = _position()
        self.me = _index(x, y, c)
        self.peers = [(x ^ ((k >> 2) & 1), y ^ ((k >> 1) & 1), c ^ (k & 1)) for k in range(1, N_DEV)]

    @staticmethod
    def semaphores(na):
        return [pltpu.SemaphoreType.DMA((7 * na,)), pltpu.SemaphoreType.DMA((7 * na,)),
                pltpu.SemaphoreType.DMA((na,))]

    def _src(self, a, slot):
        return self.srcs[a] if self.gather else self.srcs[a].at[slot]

    def _local(self, a):
        return pltpu.make_async_copy(self._src(a, self.me), self.dsts[a].at[self.me],
                                     self.local_sems.at[a])

    def _send(self, a, k):
        peer = self.peers[k]
        return pltpu.make_async_remote_copy(
            src_ref=self._src(a, _index(*peer)), dst_ref=self.dsts[a].at[self.me],
            send_sem=self.send_sems.at[7 * a + k], recv_sem=self.recv_sems.at[7 * a + k],
            device_id=peer, device_id_type=MESH)

    def _arrival(self, a, k):
        landed = self.dsts[a].at[_index(*self.peers[k])]
        return pltpu.make_async_remote_copy(
            src_ref=landed, dst_ref=landed,
            send_sem=self.send_sems.at[7 * a + k], recv_sem=self.recv_sems.at[7 * a + k],
            device_id=self.peers[k], device_id_type=MESH)

    def start(self):
        for a in range(len(self.srcs)):
            self._local(a).start()
            for k in range(N_DEV - 1):
                self._send(a, k).start()

    def wait(self):
        for a in range(len(self.srcs)):
            for k in range(N_DEV - 1):
                self._arrival(a, k).wait_recv()
            for k in range(N_DEV - 1):
                self._send(a, k).wait_send()
            self._local(a).wait()


def _exchange_shapes(arrays, gather):
    return [_sds(((N_DEV,) + a.shape) if gather else a.shape, a.dtype) for a in arrays]


def _exchange_call(arrays, gather, name):
    na = len(arrays)

    def body(*refs):
        ex = _Exchange(refs[:na], refs[na:2 * na], refs[2 * na:], gather)
        ex.start()
        ex.wait()

    return pl.pallas_call(
        body, name=name,
        in_specs=[HBM_SPEC] * na, out_specs=[HBM_SPEC] * na,
        out_shape=_exchange_shapes(arrays, gather),
        scratch_shapes=_Exchange.semaphores(na),
    )(*arrays)


def _layer_fwd(x, small, w_in, g_rest, rest_shards, tabs, next_shards):
    attn_g, sinks, gq, gkv = small
    proj, h, gathered_rest = _in_proj(x, attn_g, w_in, rest_shards)
    w_q, w_kv, w_o = _rest_from_gathered(*(gathered_rest if rest_shards else g_rest))
    swa = _swa_fwd(proj, sinks)
    qh, kh, vh, vth = _mla_prep(proj, gq, gkv, w_q, w_kv, tabs)
    mla, lse, gathered = _mla_fwd(qh, kh, vth, next_shards)
    x_next, y = _out_proj(x, proj, swa, mla, w_o)
    return x_next, (x, proj, h, swa, qh, kh, vh, mla, lse, y), (w_in, w_q, w_kv, w_o), gathered


def _layer_bwd(dx, dxb, saved, small, weights, tabs, pending, send_own):
    attn_g, sinks, gq, gkv = small
    w_in, w_q, w_kv, w_o = weights
    x, proj, h, swa, qh, kh, vh, mla, lse, y = saved
    d_o = _matmul_tn(y, dxb, "grad_w_out")
    do_a, do_b, dgate, dlt = _out_proj_bwd(dxb, proj, swa, mla, w_o)
    dqa, dkva, dsk = _swa_bwd(proj, sinks, do_a, swa)
    dqh, dkh, dvh, received = _mla_bwd(qh, kh, vh, do_b, dlt, lse, pending)
    dlat, d_q, d_kv, d_gq, d_gkv = _mla_prep_bwd(dqh, dkh, dvh, proj, gq, gkv, w_q, w_kv, tabs)
    rest_blocks = _rest_grad_blocks(d_q, d_kv, d_o)
    d_inp, got_rest = _grad_w_in(h, dqa, dgate, dlat, dkva, rest_blocks if send_own else [])
    in_block = _in_grad_blocks(d_inp)
    dx, dxb, d_attn, got_in = _in_proj_bwd(dqa, dgate, dlat, dkva, w_in, x, dx, attn_g,
                                           [in_block] if send_own else [])
    small_grads = (d_attn, dsk[0:1, 0:SWA_HEADS], d_gq, d_gkv)
    return dx, dxb, small_grads, [in_block] + rest_blocks, received, got_in + got_rest


def _pack_small_grads(small_grads, d_final, loss):
    d_attn, d_sink, d_gq, d_gkv = zip(*small_grads)
    return jnp.concatenate([
        jnp.concatenate(d_attn, axis=0).reshape(64, 128),
        jnp.concatenate(d_gq, axis=0).reshape(12, 128),
        jnp.concatenate(d_gkv, axis=0).reshape(8, 128),
        d_final.reshape(16, 128),
        jnp.pad(jnp.concatenate(d_sink, axis=1), ((0, 0), (0, 64))),
        loss[0:1],
        jnp.zeros((PACK_ROWS - ROW_LOSS - 1, 128), F32)], axis=0)


def _pack_small(attn, qa, kva, final, sinks):
    return jnp.concatenate([
        attn.reshape(64, 128), qa.reshape(12, 128), kva.reshape(8, 128), final.reshape(16, 128),
        jnp.pad(sinks.reshape(1, 64), ((0, 0), (0, 64))),
        jnp.zeros((PACK_ROWS - ROW_SINK - 1, 128), F32)], axis=0)


def _unpack_small(p):
    return (p[ROW_ATTN:ROW_QA].reshape(DEPTH, D_MODEL), p[ROW_SINK, 0:64].reshape(DEPTH, SWA_HEADS),
            p[ROW_QA:ROW_KVA].reshape(DEPTH, Q_RANK), p[ROW_KVA:ROW_FINAL].reshape(DEPTH, KV_RANK),
            p[ROW_FINAL:ROW_SINK].reshape(D_MODEL))


def kernel(x, attn_norm_g, w_in, swa_sinks, q_a_norm_g, kv_a_norm_g, w_q_b, w_kv_b, w_out, final_norm_g, loss_target, m_attn_norm_g, m_w_in, m_swa_sinks, m_q_a_norm_g, m_kv_a_norm_g, m_w_q_b, m_w_kv_b, m_w_out, m_final_norm_g, v_attn_norm_g, v_w_in, v_swa_sinks, v_q_a_norm_g, v_kv_a_norm_g, v_w_q_b, v_w_kv_b, v_w_out, v_final_norm_g):
    xs, tgt = x[0], loss_target[0]
    tabs = _rope_tables(xs.shape[0])
    shards = [w.astype(MXU_DTYPE) for w in (w_in, w_q_b, w_kv_b, w_out)]
    layer_shards = lambda l: [w[l] for w in shards]
    smalls = [(attn_norm_g[l:l + 1], swa_sinks[l], q_a_norm_g[l:l + 1], kv_a_norm_g[l:l + 1])
              for l in range(DEPTH)]

    gathered = list(_all_gather(layer_shards(0)[:1], "gather_weights")) + [None] * 3
    weights, saved = [None] * DEPTH, []
    for l in range(DEPTH):
        next_shards = layer_shards(l + 1) if l + 1 < DEPTH else []
        rest_shards = layer_shards(0)[1:] if l == 0 else []
        xs, acts, weights[l], gathered = _layer_fwd(
            xs, smalls[l], _w_in_from_gathered(gathered[0]), gathered[1:], rest_shards, tabs,
            next_shards)
        saved.append(acts)
    dx, dxb, d_final, loss = _final_loss(xs, tgt, final_norm_g.reshape(1, D_MODEL))

    received, small_grads, pending = [None] * DEPTH, [None] * DEPTH, []
    for l in reversed(range(DEPTH)):
        dx, dxb, small_grads[l], blocks, arrived, arrived_own = _layer_bwd(
            dx, dxb, saved[l], smalls[l], weights[l], tabs, pending, send_own=(l == 0))
        if pending:
            received[l + 1] = arrived
        pending = blocks
    received[0] = arrived_own
    small = _exchange_call([_pack_small_grads(small_grads, d_final, loss)], True, "gather_small")[0]

    big = []
    for a, (w, m, v, name) in enumerate(zip((w_in, w_q_b, w_kv_b, w_out),
                                            (m_w_in, m_w_q_b, m_w_kv_b, m_w_out),
                                            (v_w_in, v_w_q_b, v_w_kv_b, v_w_out),
                                            ("adamw_w_in", "adamw_w_q_b", "adamw_w_kv_b",
                                             "adamw_w_out"))):
        cols = w.shape[-1]
        flat = lambda t: t.reshape(-1, cols)
        outs = _reduce_adamw([received[l][a] for l in range(DEPTH)], flat(w), flat(m), flat(v), name)
        big.append([t.reshape(w.shape) for t in outs])

    sm = _reduce_adamw(
        [small],
        _pack_small(attn_norm_g, q_a_norm_g, kv_a_norm_g, final_norm_g, swa_sinks),
        _pack_small(m_attn_norm_g, m_q_a_norm_g, m_kv_a_norm_g, m_final_norm_g, m_swa_sinks),
        _pack_small(v_attn_norm_g, v_q_a_norm_g, v_kv_a_norm_g, v_final_norm_g, v_swa_sinks),
        "adamw_small")
    loss = sm[0][ROW_LOSS, 0]
    kinds = []
    for t in range(4):
        attn, sinks, qa, kva, final = _unpack_small(sm[t])
        b_in, b_q, b_kv, b_o = (big[i][t] for i in range(4))
        kinds.append((attn, b_in, sinks, qa, kva, b_q, b_kv, b_o, final))
    return (loss, dx[None], *kinds[0], *kinds[1], *kinds[2], *kinds[3])
```

```python
import functools

import jax
import jax.numpy as jnp
import numpy as np
from jax import lax
from jax.experimental import pallas as pl
from jax.experimental.pallas import tpu as pltpu

F32 = jnp.float32
BF16 = jnp.bfloat16
MXU_DTYPE = BF16
GRAD_DTYPE = BF16
PROJ_DTYPE = BF16
ATTN_DTYPE = BF16

D_MODEL = 2048
DEPTH = 4
EPS = 1e-6
NEG = -1e30
BLOCK = 128
SWA_HEADS = 16
MLA_HEADS = 8
Q_RANK = 384
KV_RANK = 256
MLA_SCALE = 192 ** -0.5
MLA_C2 = MLA_SCALE * 1.4426950408889634
ROPE_THETA = 10000.0
IN_WIDTH = 4032

ADAM_LR = 0.001
ADAM_B1 = 0.9
ADAM_B2 = 0.999
ADAM_EPS = 1e-08
ADAM_WD = 0.01
ADAM_STEP = 10

N_DEV = 8
MESH = pl.DeviceIdType.MESH

NP = 4096
QA, GA, GB, CQ, CKV, KR, KVA = 0, 1024, 2048, 3072, 3456, 3712, 3840

ROW_ATTN, ROW_QA, ROW_KVA, ROW_FINAL, ROW_SINK, ROW_LOSS, PACK_ROWS = 0, 64, 76, 84, 100, 101, 104

VMEM_LIMIT = 56 * 1024 * 1024
MLA_TILE = 512
GATHER_FORWARD_HEAD = 5


def _sds(shape, dtype):
    return jax.ShapeDtypeStruct(shape, dtype)


def _params(n_axes):
    return pltpu.CompilerParams(dimension_semantics=("arbitrary",) * n_axes,
                                vmem_limit_bytes=VMEM_LIMIT)


def _mm(a, b):
    return jnp.dot(a.astype(MXU_DTYPE), b.astype(MXU_DTYPE), preferred_element_type=F32)


def _mm_nt(a, b):
    return lax.dot_general(a.astype(MXU_DTYPE), b.astype(MXU_DTYPE),
                           (((1,), (1,)), ((), ())), preferred_element_type=F32)


def _mm_tn(a, b):
    return lax.dot_general(a.astype(MXU_DTYPE), b.astype(MXU_DTYPE),
                           (((0,), (0,)), ((), ())), preferred_element_type=F32)


def _rownorm(x):
    r = lax.rsqrt(jnp.mean(x * x, axis=-1, keepdims=True) + EPS)
    return x * r, r


def _rownorm_bwd(dxh, xh, r):
    return r * (dxh - xh * jnp.mean(dxh * xh, axis=-1, keepdims=True))


def _rope(t, c, s1, s2):
    return t * c + pltpu.roll(t, 32, 1) * s1 + pltpu.roll(t, 96, 1) * s2


SHARD_COLS = IN_WIDTH // N_DEV
SHARD_PAD = 512


def _orig_col_of_padded():
    o = np.full((NP,), -1, np.int64)
    for start, width, orig in ((QA, 1024, 0), (KVA, 256, 1024), (GA, 1024, 1280), (CQ, 384, 2304),
                               (CKV, 256, 2688), (KR, 64, 2944), (GB, 1024, 3008)):
        o[start:start + width] = np.arange(orig, orig + width)
    return o


def _device_major_src():
    o = _orig_col_of_padded()
    return np.where(o >= 0, o + (SHARD_PAD - SHARD_COLS) * (o // SHARD_COLS), -1)


def _kernel_layout_src():
    o = _orig_col_of_padded()
    where = np.full((IN_WIDTH,), -1, np.int64)
    where[o[o >= 0]] = np.nonzero(o >= 0)[0]
    e = np.arange(N_DEV * SHARD_PAD)
    k, c = e // SHARD_PAD, e % SHARD_PAD
    return np.where(c < SHARD_COLS, where[np.minimum(SHARD_COLS * k + c, IN_WIDTH - 1)], -1)


def _permute_columns(x, src_of, name):
    rows, n_in = x.shape
    n_out = len(src_of)
    plan, mats = [], []
    for t in range(n_out // 128):
        srcs = src_of[128 * t:128 * (t + 1)]
        entry = []
        for u in sorted(set(int(s) // 128 for s in srcs if s >= 0)):
            m = np.zeros((128, 128), np.float32)
            for c, s in enumerate(srcs):
                if s >= 0 and s // 128 == u:
                    m[s % 128, c] = 1.0
            entry.append((u, len(mats)))
            mats.append(m)
        plan.append(entry)
    tr = min(512, rows)

    def body(x_ref, p_ref, o_ref):
        for t, entry in enumerate(plan):
            acc = jnp.zeros((tr, 128), F32)
            for u, idx in entry:
                acc = acc + jnp.dot(x_ref[:, 128 * u:128 * (u + 1)], p_ref[idx],
                                    preferred_element_type=F32)
            o_ref[:, 128 * t:128 * (t + 1)] = acc.astype(o_ref.dtype)

    table = jnp.asarray(np.stack(mats), x.dtype)
    return pl.pallas_call(
        body, name=name, grid=(rows // tr,),
        in_specs=[pl.BlockSpec((tr, n_in), lambda i: (i, 0)),
                  pl.BlockSpec(table.shape, lambda i: (0, 0, 0))],
        out_specs=pl.BlockSpec((tr, n_out), lambda i: (i, 0)),
        out_shape=_sds((rows, n_out), x.dtype),
        compiler_params=_params(1),
    )(x, table)


def _w_in_from_gathered(g_in):
    return _permute_columns(g_in, _device_major_src(), "w_in_layout")


def _rest_from_gathered(g_qb, g_kvb, g_out):
    qb = g_qb.transpose(1, 0, 2)
    rope = jnp.pad(qb[..., 128:], ((0, 0), (0, 0), (0, 64)))
    w_q = jnp.concatenate([qb[..., :128].reshape(Q_RANK, 1024),
                           rope.reshape(Q_RANK, 1024)], axis=-1)
    kvb = g_kvb.transpose(1, 0, 2)
    w_kv = jnp.concatenate([kvb[..., :128].reshape(KV_RANK, 1024),
                            kvb[..., 128:].reshape(KV_RANK, 1024)], axis=-1)
    w_o = g_out.reshape(D_MODEL, D_MODEL)
    return w_q, w_kv, w_o


def _in_grad_blocks(d_inp):
    return _permute_columns(d_inp, _kernel_layout_src(), "grad_w_in_layout")


def _rest_grad_blocks(d_q, d_kv, d_o):
    qn = d_q[:, :1024].reshape(Q_RANK, 8, 128)
    qr = d_q[:, 1024:].reshape(Q_RANK, 8, 128)[..., :64]
    b_q = jnp.concatenate([qn, qr], axis=-1).transpose(1, 0, 2)
    kn = d_kv[:, :1024].reshape(KV_RANK, 8, 128)
    vv = d_kv[:, 1024:].reshape(KV_RANK, 8, 128)
    b_kv = jnp.concatenate([kn, vv], axis=-1).transpose(1, 0, 2)
    b_o = d_o.reshape(N_DEV, 256, D_MODEL)
    return [b_q, b_kv, b_o]


def _rope_tables(s):
    pos = jnp.arange(s, dtype=F32)
    inv_freq = ROPE_THETA ** (-jnp.arange(0, 64, 2, dtype=F32) / 64)
    ang = pos[:, None] * inv_freq[None, :]
    cos, sin = jnp.cos(ang), jnp.sin(ang)
    z32 = jnp.zeros((s, 32), F32)
    z64 = jnp.zeros((s, 64), F32)
    c = jnp.concatenate([cos, cos, z64], axis=1)
    s1 = jnp.concatenate([z32, sin, z64], axis=1)
    s2 = jnp.concatenate([-sin, z32, z64], axis=1)
    return c, s1, s2


def _in_proj(x, g, w, shards):
    s = x.shape[0]
    tm, tn = min(512, s), 1024
    nm = s // tm
    na = len(shards)

    def body(*refs):
        x_ref, g_ref, w_ref = refs[:3]
        o_ref, h_ref = refs[3 + na:5 + na]
        i = pl.program_id(0)
        if na:
            ex = _Gather(refs[3:3 + na], refs[5 + na:5 + 2 * na], refs[5 + 2 * na:])

            @pl.when(i == 0)
            def _():
                ex.start()

            @pl.when(i == nm // 2)
            def _():
                ex.forward()

        xh, _ = _rownorm(x_ref[...])
        h_ref[...] = (xh * g_ref[...]).astype(h_ref.dtype)
        for j in range(NP // tn):
            cols = slice(j * tn, (j + 1) * tn)
            o_ref[:, cols] = jnp.dot(h_ref[...], w_ref[:, cols],
                                     preferred_element_type=F32).astype(o_ref.dtype)

        if na:
            @pl.when(i == nm - 1)
            def _():
                ex.finish()

    row = lambda i: (i, 0)
    fixed = lambda i: (0, 0)
    outs = pl.pallas_call(
        body, name="in_proj_gather" if na else "in_proj", grid=(nm,),
        in_specs=[pl.BlockSpec((tm, D_MODEL), row), pl.BlockSpec((1, D_MODEL), fixed),
                  pl.BlockSpec((D_MODEL, NP), fixed, pipeline_mode=pl.Buffered(1))]
        + [HBM_SPEC] * na,
        out_specs=[pl.BlockSpec((tm, NP), row), pl.BlockSpec((tm, D_MODEL), row)]
        + [HBM_SPEC] * na,
        out_shape=[_sds((s, NP), PROJ_DTYPE), _sds((s, D_MODEL), MXU_DTYPE)]
        + _exchange_shapes(shards, True),
        scratch_shapes=_Exchange.semaphores(na) if na else [],
        compiler_params=_params(1),
    )(x, g, w, *shards)
    return outs[0], outs[1], list(outs[2:])


def _swa_slopes():
    return [2.0 ** (-8.0 * (h + 1) / SWA_HEADS) for h in range(SWA_HEADS)]


SWA_STACK = 8 * BLOCK


def _swa_head(j, a):
    return 2 * (4 * j + a % 4) + a // 4


def _swa_operands(kv_p, kv_c):
    kk = jnp.concatenate([kv_p[:, :128], kv_c[:, :128]], axis=0)
    vv = jnp.concatenate([kv_p[:, 128:], kv_c[:, 128:]], axis=0)
    left = lax.broadcasted_iota(jnp.int32, (2 * BLOCK, 128), 1) < 64

    def heads(t):
        return [jnp.where(left, t, 0.0), jnp.where(left, pltpu.roll(t, 64, 1), 0.0)]

    return heads(kk), heads(vv), left


def _swa_by_block(vals):
    a = lax.broadcasted_iota(jnp.int32, (1, SWA_STACK), 1) >> 7
    row = jnp.full((1, SWA_STACK), vals[7], F32)
    for t in range(6, -1, -1):
        row = jnp.where(a == t, vals[t], row)
    return row


def _swa_bias(n, j):
    slopes = _swa_slopes()
    ki = lax.broadcasted_iota(jnp.int32, (2 * BLOCK, SWA_STACK), 0)
    r = lax.broadcasted_iota(jnp.int32, (2 * BLOCK, SWA_STACK), 1)
    delta = BLOCK + (r & (BLOCK - 1)) - ki
    valid = (delta >= 0) & (delta < BLOCK) & ((n - 1) * BLOCK + ki >= 0)
    slope = _swa_by_block([slopes[_swa_head(j, a)] for a in range(8)])
    return jnp.where(valid, -slope * delta.astype(F32), NEG)


def _swa_fill_bias(n, bias_ref):
    @pl.when(n <= 1)
    def _():
        for j in range(2):
            bias_ref[j] = _swa_bias(n, j)


def _swa_sink_row(sink_ref, j):
    return _swa_by_block([sink_ref[_swa_head(j, a)] for a in range(8)])


def _swa_pairs(ref, j):
    return jnp.concatenate([ref[:, 128 * (4 * j + a):128 * (4 * j + a + 1)] for a in range(4)],
                           axis=0).astype(F32)


def _swa_stack(pairs):
    return jnp.concatenate([pairs, pltpu.roll(pairs, 64, 1)], axis=0)


def _swa_unstack(t):
    left = lax.broadcasted_iota(jnp.int32, (4 * BLOCK, 128), 1) < 64
    return jnp.where(left, t[0:4 * BLOCK], pltpu.roll(t[4 * BLOCK:8 * BLOCK], 64, 1))


def _swa_softmax(qs, kmat, bias, sink):
    sc = _mm_nt(kmat, qs) + bias
    m = jnp.maximum(jnp.max(sc, axis=0, keepdims=True), sink)
    ex = jnp.exp(sc - m)
    es = jnp.exp(sink - m)
    return ex, es, 1.0 / (jnp.sum(ex, axis=0, keepdims=True) + es)


def _swa_fwd(proj, sinks):
    s = proj.shape[0]
    nb = s // BLOCK

    def body(sink_ref, q_ref, kp_ref, kc_ref, o_ref, bias_ref):
        n = pl.program_id(0)
        _swa_fill_bias(n, bias_ref)
        k, v, _ = _swa_operands(kp_ref[...].astype(F32), kc_ref[...].astype(F32))
        for j in range(2):
            qs = _swa_stack(_swa_pairs(q_ref, j) * 0.125)
            ex, _, inv = _swa_softmax(qs, k[j], bias_ref[j], _swa_sink_row(sink_ref, j))
            o_t = _mm(v[j].T, ex) * inv
            out = jnp.concatenate([o_t[0:64, 0:4 * BLOCK], o_t[0:64, 4 * BLOCK:8 * BLOCK]], axis=0).T
            for a in range(4):
                o_ref[:, 128 * (4 * j + a):128 * (4 * j + a + 1)] = out[128 * a:128 * (a + 1)].astype(
                    o_ref.dtype)

    return pl.pallas_call(
        body, name="swa_fwd", grid=(nb,),
        in_specs=[pl.BlockSpec(memory_space=pltpu.SMEM),
                  pl.BlockSpec((BLOCK, 1024), lambda n: (n, 0)),
                  pl.BlockSpec((BLOCK, 256), lambda n: (jnp.maximum(n - 1, 0), KVA // 256)),
                  pl.BlockSpec((BLOCK, 256), lambda n: (n, KVA // 256))],
        out_specs=pl.BlockSpec((BLOCK, 1024), lambda n: (n, 0)),
        out_shape=_sds((s, 1024), ATTN_DTYPE),
        scratch_shapes=[pltpu.VMEM((2, 2 * BLOCK, SWA_STACK), F32)],
        compiler_params=_params(1),
    )(sinks, proj, proj, proj)


def _mla_prep(proj, gq, gkv, w_q, w_kv, tabs):
    s = proj.shape[0]
    tm = min(512, s)
    c, s1, s2 = tabs

    def body(p_ref, gq_ref, gkv_ref, wq_ref, wkv_ref, c_ref, s1_ref, s2_ref,
             q_ref, k_ref, v_ref, vt_ref):
        cqh, _ = _rownorm(p_ref[:, 0:384].astype(F32))
        ckvh, _ = _rownorm(p_ref[:, 384:640].astype(F32))
        q = _mm(cqh * gq_ref[...], wq_ref[...])
        kv = _mm(ckvh * gkv_ref[...], wkv_ref[...])
        cc, ss1, ss2 = c_ref[...], s1_ref[...], s2_ref[...]
        krr = _rope(p_ref[:, 640:768].astype(F32), cc, ss1, ss2).astype(k_ref.dtype)
        for h in range(MLA_HEADS):
            q_ref[h, :, 0:128] = q[:, 128 * h:128 * (h + 1)].astype(q_ref.dtype)
            q_ref[h, :, 128:256] = _rope(q[:, 1024 + 128 * h:1024 + 128 * (h + 1)],
                                         cc, ss1, ss2).astype(q_ref.dtype)
            k_ref[h, :, 0:128] = kv[:, 128 * h:128 * (h + 1)].astype(k_ref.dtype)
            k_ref[h, :, 128:256] = krr
            vv = kv[:, 1024 + 128 * h:1024 + 128 * (h + 1)]
            v_ref[h] = vv.astype(v_ref.dtype)
            vt_ref[h, 0:128, :] = vv.T.astype(vt_ref.dtype)
            vt_ref[h, 128:256, :] = jnp.ones((128, tm), vt_ref.dtype)

    row = lambda i: (i, 0)
    fixed = lambda i: (0, 0)
    return pl.pallas_call(
        body, name="mla_prep", grid=(s // tm,),
        in_specs=[pl.BlockSpec((tm, 768), lambda i: (i, CQ // 768)),
                  pl.BlockSpec((1, Q_RANK), fixed), pl.BlockSpec((1, KV_RANK), fixed),
                  pl.BlockSpec((Q_RANK, 2048), fixed), pl.BlockSpec((KV_RANK, 2048), fixed),
                  pl.BlockSpec((tm, 128), row), pl.BlockSpec((tm, 128), row),
                  pl.BlockSpec((tm, 128), row)],
        out_specs=[pl.BlockSpec((MLA_HEADS, tm, 256), lambda i: (0, i, 0)),
                   pl.BlockSpec((MLA_HEADS, tm, 256), lambda i: (0, i, 0)),
                   pl.BlockSpec((MLA_HEADS, tm, 128), lambda i: (0, i, 0)),
                   pl.BlockSpec((MLA_HEADS, 256, tm), lambda i: (0, 0, i))],
        out_shape=[_sds((MLA_HEADS, s, 256), MXU_DTYPE), _sds((MLA_HEADS, s, 256), MXU_DTYPE),
                   _sds((MLA_HEADS, s, 128), MXU_DTYPE), _sds((MLA_HEADS, 256, s), MXU_DTYPE)],
        compiler_params=_params(1),
    )(proj, gq, gkv, w_q, w_kv, c, s1, s2)


def _scores_t(k, q, t, diagonal):
    sc = _mm_nt(k, q)
    if diagonal:
        key = lax.broadcasted_iota(jnp.int32, (t, t), 0)
        query = lax.broadcasted_iota(jnp.int32, (t, t), 1)
        sc = jnp.where(key <= query, sc, NEG)
    return sc


def _mla_fwd(qh, kh, vth, shards):
    s = qh.shape[1]
    t = min(MLA_TILE, s)
    nq = s // t
    na = len(shards)

    def body(*refs):
        q_ref, k_ref, vt_ref = refs[:3]
        o_ref, lse_ref = refs[3 + na:5 + na]
        m_ref, acc_ref = refs[5 + 2 * na:7 + 2 * na]
        h, i = pl.program_id(0), pl.program_id(1)
        if na:
            ex = _Gather(refs[3:3 + na], refs[5 + na:5 + 2 * na], refs[7 + 2 * na:])

            @pl.when((h == 0) & (i == 0))
            def _():
                ex.start()

            @pl.when((h == GATHER_FORWARD_HEAD) & (i == 0))
            def _():
                ex.forward()

        m_ref[...] = jnp.full(m_ref.shape, NEG, F32)
        acc_ref[...] = jnp.zeros(acc_ref.shape, F32)

        def step(start, width, diagonal):
            keys = pl.ds(pl.multiple_of(start, t), width)
            if diagonal:
                sc = _scores_t(k_ref[keys, :], q_ref[...], t, True)
            else:
                sc = _mm_nt(k_ref[keys, :], q_ref[...])
            m_prev = m_ref[...]
            m_new = jnp.maximum(m_prev, jnp.max(sc, axis=0, keepdims=True))
            alpha = jnp.exp2((m_prev - m_new) * MLA_C2)
            p = jnp.exp2((sc - m_new[0:1, :]) * MLA_C2)
            acc_ref[...] = alpha[0:1, :] * acc_ref[...] + _mm(vt_ref[:, keys], p)
            m_ref[...] = m_new

        def below_diagonal(jj, carry):
            step(jj * (2 * t), 2 * t, False)
            return carry

        lax.fori_loop(0, i // 2, below_diagonal, 0)

        @pl.when(i % 2 == 1)
        def _():
            step((i - 1) * t, t, False)

        step(i * t, t, True)
        l = acc_ref[128:136, :]
        o_ref[...] = (acc_ref[0:128, :] / l[0:1, :]).T.astype(o_ref.dtype)
        lse_ref[...] = m_ref[...] * MLA_C2 + jnp.log2(l)

        if na:
            @pl.when((h == MLA_HEADS - 1) & (i == nq - 1))
            def _():
                ex.finish()

    head = lambda h, i: (h, 0, 0)
    outs = pl.pallas_call(
        body, name="mla_fwd_gather" if na else "mla_fwd", grid=(MLA_HEADS, nq),
        in_specs=[pl.BlockSpec((None, t, 256), lambda h, i: (h, i, 0)),
                  pl.BlockSpec((None, s, 256), head),
                  pl.BlockSpec((None, 256, s), head)] + [HBM_SPEC] * na,
        out_specs=[pl.BlockSpec((t, 128), lambda h, i: (i, h)),
                   pl.BlockSpec((None, 8, t), lambda h, i: (h, 0, i))] + [HBM_SPEC] * na,
        out_shape=[_sds((s, 1024), ATTN_DTYPE), _sds((MLA_HEADS, 8, s), F32)]
        + _exchange_shapes(shards, True),
        scratch_shapes=[pltpu.VMEM((8, t), F32), pltpu.VMEM((256, t), F32)]
        + (_Exchange.semaphores(na) if na else []),
        compiler_params=_params(2),
    )(qh, kh, vth, *shards)
    return outs[0], outs[1], list(outs[2:])


def _silu_parts(g):
    sg = jax.nn.sigmoid(g)
    return g * sg, sg * (1.0 + g * (1.0 - sg))


def _out_proj(x, proj, swa, mla, w_out):
    s = x.shape[0]
    tm = min(512, s)

    def body(x_ref, ga_ref, gb_ref, a_ref, b_ref, w_ref, xo_ref, y_ref):
        sa, _ = _silu_parts(ga_ref[...].astype(F32))
        sb, _ = _silu_parts(gb_ref[...].astype(F32))
        y_ref[:, 0:1024] = (a_ref[...].astype(F32) * sa).astype(y_ref.dtype)
        y_ref[:, 1024:2048] = (b_ref[...].astype(F32) * sb).astype(y_ref.dtype)
        xo_ref[...] = x_ref[...] + jnp.dot(y_ref[...], w_ref[...], preferred_element_type=F32)

    row = lambda i: (i, 0)
    return pl.pallas_call(
        body, name="out_proj", grid=(s // tm,),
        in_specs=[pl.BlockSpec((tm, D_MODEL), row),
                  pl.BlockSpec((tm, 1024), lambda i: (i, GA // 1024)),
                  pl.BlockSpec((tm, 1024), lambda i: (i, GB // 1024)),
                  pl.BlockSpec((tm, 1024), row), pl.BlockSpec((tm, 1024), row),
                  pl.BlockSpec((D_MODEL, D_MODEL), lambda i: (0, 0), pipeline_mode=pl.Buffered(1))],
        out_specs=[pl.BlockSpec((tm, D_MODEL), row), pl.BlockSpec((tm, D_MODEL), row)],
        out_shape=[_sds((s, D_MODEL), F32), _sds((s, D_MODEL), MXU_DTYPE)],
        compiler_params=_params(1),
    )(x, proj, proj, swa, mla, w_out)


def _final_loss(x, tgt, g):
    s = x.shape[0]
    tm = min(512, s)

    def body(x_ref, t_ref, g_ref, dx_ref, dxb_ref, dg_ref, loss_ref):
        @pl.when(pl.program_id(0) == 0)
        def _():
            dg_ref[...] = jnp.zeros(dg_ref.shape, F32)
            loss_ref[...] = jnp.zeros(loss_ref.shape, F32)
        xh, r = _rownorm(x_ref[...])
        gg = g_ref[...]
        err = xh * gg - t_ref[...]
        per_row = jnp.mean(err * err, axis=-1, keepdims=True)
        loss_ref[...] += 0.5 * jnp.sum(per_row, axis=0, keepdims=True)
        dy = err * (1.0 / D_MODEL)
        dg_ref[...] += jnp.sum(dy * xh, axis=0, keepdims=True)
        dx = _rownorm_bwd(dy * gg, xh, r)
        dx_ref[...] = dx
        dxb_ref[...] = dx.astype(dxb_ref.dtype)

    row = lambda i: (i, 0)
    fixed = lambda i: (0, 0)
    return pl.pallas_call(
        body, name="final_loss", grid=(s // tm,),
        in_specs=[pl.BlockSpec((tm, D_MODEL), row), pl.BlockSpec((tm, D_MODEL), row),
                  pl.BlockSpec((1, D_MODEL), fixed)],
        out_specs=[pl.BlockSpec((tm, D_MODEL), row), pl.BlockSpec((tm, D_MODEL), row),
                   pl.BlockSpec((1, D_MODEL), fixed), pl.BlockSpec((8, 128), fixed)],
        out_shape=[_sds((s, D_MODEL), F32), _sds((s, D_MODEL), MXU_DTYPE), _sds((1, D_MODEL), F32),
                   _sds((8, 128), F32)],
        compiler_params=_params(1),
    )(x, tgt, g)


def _out_proj_bwd(dx, proj, swa, mla, w_out):
    s = dx.shape[0]
    tm = min(512, s)

    def body(dx_ref, ga_ref, gb_ref, a_ref, b_ref, w_ref, doa_ref, dob_ref, dg_ref, dlt_ref):
        dx = dx_ref[...].astype(MXU_DTYPE)
        dya = _mm_nt(dx, w_ref[0:1024, :])
        sa, dsa = _silu_parts(ga_ref[...].astype(F32))
        doa_ref[...] = dya * sa
        dg_ref[:, 0:1024] = (dya * a_ref[...].astype(F32) * dsa).astype(dg_ref.dtype)
        dyb = _mm_nt(dx, w_ref[1024:2048, :])
        sb, dsb = _silu_parts(gb_ref[...].astype(F32))
        b = b_ref[...].astype(F32)
        dob = dyb * sb
        dob_ref[...] = dob.astype(dob_ref.dtype)
        dg_ref[:, 1024:2048] = (dyb * b * dsb).astype(dg_ref.dtype)
        prod = dob * b
        for h in range(MLA_HEADS):
            dlt = jnp.sum(prod[:, 128 * h:128 * (h + 1)], axis=1, keepdims=True)
            dlt_ref[h] = jnp.broadcast_to(dlt, (tm, 128)).T[0:8, :]

    row = lambda i: (i, 0)
    return pl.pallas_call(
        body, name="out_proj_bwd", grid=(s // tm,),
        in_specs=[pl.BlockSpec((tm, D_MODEL), row),
                  pl.BlockSpec((tm, 1024), lambda i: (i, GA // 1024)),
                  pl.BlockSpec((tm, 1024), lambda i: (i, GB // 1024)),
                  pl.BlockSpec((tm, 1024), row), pl.BlockSpec((tm, 1024), row),
                  pl.BlockSpec((D_MODEL, D_MODEL), lambda i: (0, 0), pipeline_mode=pl.Buffered(1))],
        out_specs=[pl.BlockSpec((tm, 1024), row), pl.BlockSpec((tm, 1024), row),
                   pl.BlockSpec((tm, D_MODEL), row),
                   pl.BlockSpec((MLA_HEADS, 8, tm), lambda i: (0, 0, i))],
        out_shape=[_sds((s, 1024), F32), _sds((s, 1024), MXU_DTYPE), _sds((s, D_MODEL), MXU_DTYPE),
                   _sds((MLA_HEADS, 8, s), F32)],
        compiler_params=_params(1),
    )(dx, proj, proj, swa, mla, w_out)


def _matmul_tn(a, b, name):
    s, m = a.shape
    n = b.shape[1]
    tm, tn, tk = min(1024, m), min(1024, n), min(2048, s)
    nk = s // tk

    def body(a_ref, b_ref, o_ref, acc_ref):
        k = pl.program_id(2)

        @pl.when(k == 0)
        def _():
            acc_ref[...] = jnp.zeros(acc_ref.shape, F32)
        acc_ref[...] += _mm_tn(a_ref[...], b_ref[...])

        @pl.when(k == nk - 1)
        def _():
            o_ref[...] = acc_ref[...].astype(o_ref.dtype)

    return pl.pallas_call(
        body, name=name, grid=(m // tm, n // tn, nk),
        in_specs=[pl.BlockSpec((tk, tm), lambda i, j, k: (k, i)),
                  pl.BlockSpec((tk, tn), lambda i, j, k: (k, j))],
        out_specs=pl.BlockSpec((tm, tn), lambda i, j, k: (i, j)),
        out_shape=_sds((m, n), GRAD_DTYPE),
        scratch_shapes=[pltpu.VMEM((tm, tn), F32)],
        compiler_params=_params(3),
    )(a, b)


def _grad_w_in(h, dqa, dgate, dlat, dkva, blocks):
    s = h.shape[0]
    tm, tn, tk = 1024, 1024, min(2048, s)
    nk = s // tk
    grid = (D_MODEL // tm, NP // tn, nk)
    na = len(blocks)

    def body(*refs):
        a_ref, dqa_ref, dg8_ref, dlat_ref, dkva_ref = refs[:5]
        o_ref = refs[5 + na]
        acc_ref = refs[6 + 2 * na]
        i, j, k = pl.program_id(0), pl.program_id(1), pl.program_id(2)
        if na:
            ex = _Exchange(refs[5:5 + na], refs[6 + na:6 + 2 * na], refs[7 + 2 * na:], gather=False)

            @pl.when((i == 0) & (j == 0) & (k == 0))
            def _():
                ex.start()

        @pl.when(k == 0)
        def _():
            acc_ref[...] = jnp.zeros(acc_ref.shape, F32)

        @pl.when(j == QA // tn)
        def _():
            acc_ref[...] += _mm_tn(a_ref[...], dqa_ref[...])

        @pl.when((j == GA // tn) | (j == GB // tn))
        def _():
            acc_ref[...] += _mm_tn(a_ref[...], dg8_ref[...])

        @pl.when(j == CQ // tn)
        def _():
            acc_ref[:, 0:768] += _mm_tn(a_ref[...], dlat_ref[...])
            acc_ref[:, 768:1024] += _mm_tn(a_ref[...], dkva_ref[...])

        @pl.when(k == nk - 1)
        def _():
            o_ref[...] = acc_ref[...].astype(o_ref.dtype)

        if na:
            @pl.when((i == grid[0] - 1) & (j == grid[1] - 1) & (k == nk - 1))
            def _():
                ex.wait()

    def when(group, width):
        return pl.BlockSpec((tk, width), lambda i, j, k: (jnp.where(j == group, k, 0), 0))

    outs = pl.pallas_call(
        body, name="grad_w_in_scatter" if na else "grad_w_in", grid=grid,
        in_specs=[pl.BlockSpec((tk, tm), lambda i, j, k: (k, i)),
                  when(QA // tn, 1024),
                  pl.BlockSpec((tk, 1024), lambda i, j, k: (
                      jnp.where((j == GA // tn) | (j == GB // tn), k, 0),
                      jnp.clip(j - GA // tn, 0, 1))),
                  when(CQ // tn, 768), when(CQ // tn, 256)] + [HBM_SPEC] * na,
        out_specs=[pl.BlockSpec((tm, tn), lambda i, j, k: (i, j))] + [HBM_SPEC] * na,
        out_shape=[_sds((D_MODEL, NP), GRAD_DTYPE)] + _exchange_shapes(blocks, False),
        scratch_shapes=[pltpu.VMEM((tm, tn), F32)] + (_Exchange.semaphores(na) if na else []),
        compiler_params=_params(3),
    )(h, dqa, dgate, dlat, dkva, *blocks)
    return outs[0], list(outs[1:])


def _swa_bwd(proj, sinks, do, o):
    s = proj.shape[0]
    nb = s // BLOCK

    def body(sink_ref, q_ref, kp_ref, kc_ref, do_ref, o_ref, dq_ref, dkv_ref, dsink_ref,
             carry_ref, bias_ref):
        n = pl.program_id(0)
        _swa_fill_bias(n, bias_ref)

        @pl.when(n == 0)
        def _():
            carry_ref[...] = jnp.zeros(carry_ref.shape, F32)
            dsink_ref[...] = jnp.zeros(dsink_ref.shape, F32)

        @pl.when(n < nb)
        def _():
            k, v, left = _swa_operands(kp_ref[...].astype(F32), kc_ref[...].astype(F32))
            lane_s = lax.broadcasted_iota(jnp.int32, (8, 128), 1)
            dk, dv = [None, None], [None, None]
            dsink = jnp.zeros((8, 128), F32)
            for j in range(2):
                qs = _swa_stack(_swa_pairs(q_ref, j) * 0.125)
                do_pairs = _swa_pairs(do_ref, j)
                dos = _swa_stack(do_pairs)
                prod_t = (do_pairs * _swa_pairs(o_ref, j)).T
                dlt = jnp.concatenate([jnp.sum(prod_t[0:64], axis=0, keepdims=True),
                                       jnp.sum(prod_t[64:128], axis=0, keepdims=True)], axis=1)
                ex, es, inv = _swa_softmax(qs, k[j], bias_ref[j], _swa_sink_row(sink_ref, j))
                p = ex * inv
                ds = p * (_mm_nt(v[j], dos) - dlt)
                sink_term = es * inv * dlt
                for a in range(8):
                    dsh = -jnp.sum(sink_term[:, 128 * a:128 * (a + 1)], axis=1, keepdims=True)
                    dsink = dsink + jnp.where(lane_s == _swa_head(j, a), dsh, 0.0)
                dq = _swa_unstack(_mm_tn(ds, k[j]))
                dv[j] = _mm(p, dos)
                dk[j] = _mm(ds, qs)
                for a in range(4):
                    cols = slice(128 * (4 * j + a), 128 * (4 * j + a + 1))
                    dq_ref[:, cols] = (dq[128 * a:128 * (a + 1)] * 0.125).astype(dq_ref.dtype)

            def merge(t):
                return jnp.where(left, t[0], pltpu.roll(t[1], 64, 1))

            contrib = jnp.concatenate([merge(dk), merge(dv)], axis=1)
            dkv_ref[...] = (carry_ref[...] + contrib[0:BLOCK]).astype(dkv_ref.dtype)
            carry_ref[...] = contrib[BLOCK:2 * BLOCK]
            dsink_ref[...] += dsink

        @pl.when(n == nb)
        def _():
            dkv_ref[...] = carry_ref[...].astype(dkv_ref.dtype)

    cur = lambda n: (jnp.minimum(n, nb - 1), 0)
    return pl.pallas_call(
        body, name="swa_bwd", grid=(nb + 1,),
        in_specs=[pl.BlockSpec(memory_space=pltpu.SMEM),
                  pl.BlockSpec((BLOCK, 1024), cur),
                  pl.BlockSpec((BLOCK, 256), lambda n: (jnp.clip(n - 1, 0, nb - 1), KVA // 256)),
                  pl.BlockSpec((BLOCK, 256), lambda n: (jnp.minimum(n, nb - 1), KVA // 256)),
                  pl.BlockSpec((BLOCK, 1024), cur), pl.BlockSpec((BLOCK, 1024), cur)],
        out_specs=[pl.BlockSpec((BLOCK, 1024), cur),
                   pl.BlockSpec((BLOCK, 256), lambda n: (jnp.maximum(n - 1, 0), 0)),
                   pl.BlockSpec((8, 128), lambda n: (0, 0))],
        out_shape=[_sds((s, 1024), MXU_DTYPE), _sds((s, 256), MXU_DTYPE), _sds((8, 128), F32)],
        scratch_shapes=[pltpu.VMEM((BLOCK, 256), F32), pltpu.VMEM((2, 2 * BLOCK, SWA_STACK), F32)],
        compiler_params=_params(1),
    )(sinks, proj, proj, proj, do, o)


def _mla_bwd(qh, kh, vh, do, dlt, lse, blocks):
    s = qh.shape[1]
    t = min(MLA_TILE, s)
    nq = s // t

    na = len(blocks)

    def body(*refs):
        q_ref, k_ref, v_ref, do_ref, dlt_ref, lse_ref = refs[:6]
        dq_ref, dk_ref, dv_ref = refs[6 + na:9 + na]
        h, j = pl.program_id(0), pl.program_id(1)
        if na:
            ex = _Exchange(refs[6:6 + na], refs[9 + na:9 + 2 * na], refs[9 + 2 * na:], gather=False)

            @pl.when((h == 0) & (j == 0))
            def _():
                ex.start()

        def step(start, width, diagonal):
            rows = pl.ds(pl.multiple_of(start, t), width)
            q, k, dout = q_ref[rows, :], k_ref[...], do_ref[rows, :]
            sc = _scores_t(k, q, t, True) if diagonal else _mm_nt(k, q)
            p = jnp.exp2(sc * MLA_C2 - lse_ref[0:1, rows])
            dv = _mm(p, dout)
            ds = p * (_mm_nt(v_ref[...], dout) - dlt_ref[0:1, rows])
            dk = _mm(ds, q)
            dq = _mm_tn(ds, k)
            return rows, dq, dk, dv

        rows, dq, dk, dv = step(j * t, t, True)
        dk_ref[...] = dk
        dv_ref[...] = dv

        @pl.when(j == 0)
        def _():
            dq_ref[rows, :] = dq * MLA_SCALE

        @pl.when(j > 0)
        def _():
            dq_ref[rows, :] = (dq_ref[rows, :] + dq) * MLA_SCALE

        def above_diagonal(start, width):
            rows, dq, dk, dv = step(start, width, False)
            dk_ref[...] += dk
            dv_ref[...] += dv

            @pl.when(j == 0)
            def _():
                dq_ref[rows, :] = dq

            @pl.when(j > 0)
            def _():
                dq_ref[rows, :] += dq

        n_above = nq - 1 - j

        def pair(jj, carry):
            above_diagonal((j + 1 + 2 * jj) * t, 2 * t)
            return carry

        lax.fori_loop(0, n_above // 2, pair, 0)

        @pl.when(n_above % 2 == 1)
        def _():
            above_diagonal((nq - 1) * t, t)

        dk_ref[...] *= MLA_SCALE

        if na:
            @pl.when((h == MLA_HEADS - 1) & (j == nq - 1))
            def _():
                ex.wait()

    head = lambda h, j: (h, 0, 0)
    kv_map = lambda h, j: (h, j, 0)
    outs = pl.pallas_call(
        body, name="mla_bwd_scatter" if na else "mla_bwd", grid=(MLA_HEADS, nq),
        in_specs=[pl.BlockSpec((None, s, 256), head), pl.BlockSpec((None, t, 256), kv_map),
                  pl.BlockSpec((None, t, 128), kv_map),
                  pl.BlockSpec((s, 128), lambda h, j: (0, h)),
                  pl.BlockSpec((None, 8, s), head), pl.BlockSpec((None, 8, s), head)]
        + [HBM_SPEC] * na,
        out_specs=[pl.BlockSpec((None, s, 256), head),
                   pl.BlockSpec((None, t, 256), kv_map), pl.BlockSpec((None, t, 128), kv_map)]
        + [HBM_SPEC] * na,
        out_shape=[_sds((MLA_HEADS, s, 256), F32), _sds((MLA_HEADS, s, 256), F32),
                   _sds((MLA_HEADS, s, 128), F32)] + _exchange_shapes(blocks, False),
        scratch_shapes=_Exchange.semaphores(na) if na else [],
        compiler_params=_params(2),
    )(qh, kh, vh, do, dlt, lse, *blocks)
    return outs[0], outs[1], outs[2], list(outs[3:])


def _mla_prep_bwd(dqh, dkh, dvh, proj, gq, gkv, w_q, w_kv, tabs):
    s = proj.shape[0]
    tm = min(256, s)
    nm = s // tm
    c, s1, s2 = tabs

    def body(dq_ref, dk_ref, dv_ref, p_ref, gq_ref, gkv_ref, wq_ref, wkv_ref,
             c_ref, s1_ref, s2_ref, dp_ref, dwq_ref, dwkv_ref, dgq_ref, dgkv_ref,
             dqf_ref, dkvf_ref, dwq_acc, dwkv_acc):
        @pl.when(pl.program_id(0) == 0)
        def _():
            dgq_ref[...] = jnp.zeros(dgq_ref.shape, F32)
            dgkv_ref[...] = jnp.zeros(dgkv_ref.shape, F32)
            dwq_acc[...] = jnp.zeros(dwq_acc.shape, F32)
            dwkv_acc[...] = jnp.zeros(dwkv_acc.shape, F32)
        cc, ns1, ns2 = c_ref[...], -s1_ref[...], -s2_ref[...]
        dkr = jnp.zeros((tm, 128), F32)
        for h in range(MLA_HEADS):
            dqf_ref[:, 128 * h:128 * (h + 1)] = dq_ref[h, :, 0:128].astype(dqf_ref.dtype)
            dqf_ref[:, 1024 + 128 * h:1024 + 128 * (h + 1)] = _rope(
                dq_ref[h, :, 128:256], cc, ns1, ns2).astype(dqf_ref.dtype)
            dkvf_ref[:, 128 * h:128 * (h + 1)] = dk_ref[h, :, 0:128].astype(dkvf_ref.dtype)
            dkvf_ref[:, 1024 + 128 * h:1024 + 128 * (h + 1)] = dv_ref[h].astype(dkvf_ref.dtype)
            dkr = dkr + dk_ref[h, :, 128:256]
        dcqn = _mm_nt(dqf_ref[...], wq_ref[...])
        dckvn = _mm_nt(dkvf_ref[...], wkv_ref[...])
        cqh, rq = _rownorm(p_ref[:, 0:384].astype(F32))
        ckvh, rkv = _rownorm(p_ref[:, 384:640].astype(F32))
        dgq_ref[...] += jnp.sum(dcqn * cqh, axis=0, keepdims=True)
        dgkv_ref[...] += jnp.sum(dckvn * ckvh, axis=0, keepdims=True)
        dp_ref[:, 0:384] = _rownorm_bwd(dcqn * gq_ref[...], cqh, rq).astype(dp_ref.dtype)
        dp_ref[:, 384:640] = _rownorm_bwd(dckvn * gkv_ref[...], ckvh, rkv).astype(dp_ref.dtype)
        dp_ref[:, 640:768] = _rope(dkr, cc, ns1, ns2).astype(dp_ref.dtype)
        dwq_acc[...] += _mm_tn(cqh * gq_ref[...], dqf_ref[...])
        dwkv_acc[...] += _mm_tn(ckvh * gkv_ref[...], dkvf_ref[...])

        @pl.when(pl.program_id(0) == nm - 1)
        def _():
            dwq_ref[...] = dwq_acc[...].astype(dwq_ref.dtype)
            dwkv_ref[...] = dwkv_acc[...].astype(dwkv_ref.dtype)

    row = lambda i: (i, 0)
    fixed = lambda i: (0, 0)
    head = lambda i: (0, i, 0)
    return pl.pallas_call(
        body, name="mla_prep_bwd", grid=(nm,),
        in_specs=[pl.BlockSpec((MLA_HEADS, tm, 256), head), pl.BlockSpec((MLA_HEADS, tm, 256), head),
                  pl.BlockSpec((MLA_HEADS, tm, 128), head),
                  pl.BlockSpec((tm, 768), lambda i: (i, CQ // 768)),
                  pl.BlockSpec((1, Q_RANK), fixed), pl.BlockSpec((1, KV_RANK), fixed),
                  pl.BlockSpec((Q_RANK, 2048), fixed), pl.BlockSpec((KV_RANK, 2048), fixed),
                  pl.BlockSpec((tm, 128), row), pl.BlockSpec((tm, 128), row),
                  pl.BlockSpec((tm, 128), row)],
        out_specs=[pl.BlockSpec((tm, 768), row), pl.BlockSpec((Q_RANK, 2048), fixed),
                   pl.BlockSpec((KV_RANK, 2048), fixed),
                   pl.BlockSpec((1, Q_RANK), fixed), pl.BlockSpec((1, KV_RANK), fixed)],
        out_shape=[_sds((s, 768), MXU_DTYPE), _sds((Q_RANK, 2048), GRAD_DTYPE),
                   _sds((KV_RANK, 2048), GRAD_DTYPE),
                   _sds((1, Q_RANK), F32), _sds((1, KV_RANK), F32)],
        scratch_shapes=[pltpu.VMEM((tm, 2048), MXU_DTYPE), pltpu.VMEM((tm, 2048), MXU_DTYPE),
                        pltpu.VMEM((Q_RANK, 2048), F32), pltpu.VMEM((KV_RANK, 2048), F32)],
        compiler_params=_params(1),
    )(dqh, dkh, dvh, proj, gq, gkv, w_q, w_kv, c, s1, s2)


def _in_proj_bwd(dqa, dgate, dlat, dkva, w, x, dx_out, g, blocks):
    s = x.shape[0]
    tm = min(256, s)
    nm = s // tm
    na = len(blocks)

    def body(*refs):
        dqa_ref, dg8_ref, dlat_ref, dkva_ref, w_ref, x_ref, dxo_ref, g_ref = refs[:8]
        dx_ref, dxb_ref, dg_ref = refs[8 + na:11 + na]
        if na:
            ex = _Exchange(refs[8:8 + na], refs[11 + na:11 + 2 * na], refs[11 + 2 * na:], gather=False)

        @pl.when(pl.program_id(0) == 0)
        def _():
            dg_ref[...] = jnp.zeros(dg_ref.shape, F32)
            if na:
                ex.start()

        dh = (_mm_nt(dqa_ref[...], w_ref[:, QA:QA + 1024])
              + _mm_nt(dg8_ref[:, 0:1024], w_ref[:, GA:GA + 1024])
              + _mm_nt(dg8_ref[:, 1024:2048], w_ref[:, GB:GB + 1024])
              + _mm_nt(dlat_ref[...], w_ref[:, CQ:CQ + 768])
              + _mm_nt(dkva_ref[...], w_ref[:, KVA:KVA + 256]))
        xh, r = _rownorm(x_ref[...])
        dg_ref[...] += jnp.sum(dh * xh, axis=0, keepdims=True)
        dx = dxo_ref[...] + _rownorm_bwd(dh * g_ref[...], xh, r)
        dx_ref[...] = dx
        dxb_ref[...] = dx.astype(dxb_ref.dtype)

        if na:
            @pl.when(pl.program_id(0) == nm - 1)
            def _():
                ex.wait()

    row = lambda i: (i, 0)
    fixed = lambda i: (0, 0)
    outs = pl.pallas_call(
        body, name="in_proj_bwd_scatter" if na else "in_proj_bwd", grid=(nm,),
        in_specs=[pl.BlockSpec((tm, 1024), row), pl.BlockSpec((tm, 2048), row),
                  pl.BlockSpec((tm, 768), row), pl.BlockSpec((tm, 256), row),
                  pl.BlockSpec((D_MODEL, NP), fixed, pipeline_mode=pl.Buffered(1)),
                  pl.BlockSpec((tm, D_MODEL), row), pl.BlockSpec((tm, D_MODEL), row),
                  pl.BlockSpec((1, D_MODEL), fixed)] + [HBM_SPEC] * na,
        out_specs=[pl.BlockSpec((tm, D_MODEL), row), pl.BlockSpec((tm, D_MODEL), row),
                   pl.BlockSpec((1, D_MODEL), fixed)] + [HBM_SPEC] * na,
        out_shape=[_sds((s, D_MODEL), F32), _sds((s, D_MODEL), MXU_DTYPE), _sds((1, D_MODEL), F32)]
        + _exchange_shapes(blocks, False),
        scratch_shapes=_Exchange.semaphores(na) if na else [],
        compiler_params=_params(1),
    )(dqa, dgate, dlat, dkva, w, x, dx_out, g, *blocks)
    return outs[0], outs[1], outs[2], list(outs[3:])


def _reduce_adamw(parts, w, m, v, name):
    n_layers = len(parts)
    rows, part_cols = parts[0].shape[1:]
    cols = w.shape[-1]
    lanes = -(-cols // 128) * 128
    tr = rows
    for cand in (1024, 512, 256, 128, 64, 32, 16, 8):
        if rows % cand == 0 and N_DEV * cand * lanes * 4 <= 8 * 1024 * 1024:
            tr = cand
            break
    nr = rows // tr

    def body(*refs):
        p_refs = refs[:n_layers]
        w_ref, m_ref, v_ref, g_ref, d_ref, nm_ref, nv_ref = refs[n_layers:]
        layer = pl.program_id(0)
        for l in range(n_layers):
            @pl.when(layer == l)
            def _(l=l):
                g = p_refs[l][0, :, 0:cols].astype(F32)
                for k in range(1, N_DEV):
                    g = g + p_refs[l][k, :, 0:cols].astype(F32)
                update(g, w_ref, m_ref, v_ref, g_ref, d_ref, nm_ref, nv_ref)

    def update(g, w_ref, m_ref, v_ref, g_ref, d_ref, nm_ref, nv_ref):
        m2 = ADAM_B1 * m_ref[...] + (1.0 - ADAM_B1) * g
        v2 = ADAM_B2 * v_ref[...] + (1.0 - ADAM_B2) * (g * g)
        m_hat = m2 / (1.0 - ADAM_B1 ** ADAM_STEP)
        v_hat = v2 / (1.0 - ADAM_B2 ** ADAM_STEP)
        g_ref[...] = g
        d_ref[...] = -ADAM_LR * (m_hat / (jnp.sqrt(v_hat) + ADAM_EPS) + ADAM_WD * w_ref[...])
        nm_ref[...] = m2
        nv_ref[...] = v2

    def part_spec(l):
        return pl.BlockSpec((N_DEV, tr, part_cols),
                            lambda layer, i: (0, jnp.where(layer == l, i, 0), 0))

    blk = pl.BlockSpec((tr, cols), lambda layer, i: (layer * nr + i, 0))
    return pl.pallas_call(
        body, name=name, grid=(n_layers, nr),
        in_specs=[part_spec(l) for l in range(n_layers)] + [blk, blk, blk],
        out_specs=[blk, blk, blk, blk],
        out_shape=[_sds((n_layers * rows, cols), F32)] * 4,
        compiler_params=_params(2),
    )(*parts, w, m, v)


def _position():
    x, y, c = lax.axis_index("x"), lax.axis_index("y"), lax.axis_index("c")
    return x, y, c


def _index(px, py, pc):
    return 4 * px + 2 * py + pc


HBM_SPEC = pl.BlockSpec(memory_space=pltpu.HBM)
IN_BLOCKED = (D_MODEL, N_DEV * SHARD_PAD)


def _block(ref, idx):
    if tuple(ref.shape) == IN_BLOCKED:
        return ref.at[:, pl.ds(pl.multiple_of(idx * SHARD_PAD, SHARD_PAD), SHARD_PAD)]
    return ref.at[idx]


class _Gather:
    def __init__(self, srcs, dsts, sems):
        self.srcs, self.dsts = srcs, dsts
        self.send_sems, self.recv_sems, self.local_sems = sems
        x, y, c = _position()
        self.c = c
        self.me, self.sibling = (x, y, c), (x, y, 1 - c)
        self.chips = [(1 - x, y), (x, 1 - y), (1 - x, 1 - y)]

    def _copy(self, a, k, block, to, own=False):
        slot = _block(self.dsts[a], _index(*block))
        return pltpu.make_async_remote_copy(
            src_ref=self.srcs[a] if own else slot, dst_ref=slot,
            send_sem=self.send_sems.at[7 * a + k], recv_sem=self.recv_sems.at[7 * a + k],
            device_id=to, device_id_type=MESH)

    def _local(self, a):
        return pltpu.make_async_copy(self.srcs[a], _block(self.dsts[a], _index(*self.me)),
                                     self.local_sems.at[a])

    def _first(self, a):
        return [self._copy(a, 0, self.me, self.sibling, own=True)] + [
            self._copy(a, 1 + j, self.me, (*chip, self.c), own=True)
            for j, chip in enumerate(self.chips)]

    def _passed(self, a, j):
        return self._copy(a, 4 + j, (*self.chips[j], self.c), self.sibling)

    def start(self):
        for a in range(len(self.srcs)):
            self._local(a).start()
            for cp in self._first(a):
                cp.start()

    def forward(self):
        for j, chip in enumerate(self.chips):
            for a in range(len(self.srcs)):
                self._copy(a, 1 + j, (*chip, self.c), self.me).wait_recv()
                self._passed(a, j).start()

    def finish(self):
        for a in range(len(self.srcs)):
            self._copy(a, 0, self.sibling, self.me).wait_recv()
            for j, chip in enumerate(self.chips):
                self._copy(a, 4 + j, (*chip, 1 - self.c), self.me).wait_recv()
            for cp in self._first(a):
                cp.wait_send()
            for j in range(3):
                self._passed(a, j).wait_send()
            self._local(a).wait()


def _all_gather(shards, name):
    na = len(shards)

    def body(*refs):
        g = _Gather(refs[:na], refs[na:2 * na], refs[2 * na:])
        g.start()
        g.forward()
        g.finish()

    return pl.pallas_call(
        body, name=name,
        in_specs=[HBM_SPEC] * na, out_specs=[HBM_SPEC] * na,
        out_shape=_exchange_shapes(shards, True),
        scratch_shapes=_Exchange.semaphores(na),
    )(*shards)


class _Exchange:
    def __init__(self, srcs, dsts, sems, gather):
        self.srcs, self.dsts, self.gather = srcs, dsts, gather
        self.send_sems, self.recv_sems, self.local_sems = sems
        x, y, c = _position()
        self.me = _index(x, y, c)
        self.peers = [(x ^ ((k >> 2) & 1), y ^ ((k >> 1) & 1), c ^ (k & 1)) for k in range(1, N_DEV)]

    @staticmethod
    def semaphores(na):
        return [pltpu.SemaphoreType.DMA((7 * na,)), pltpu.SemaphoreType.DMA((7 * na,)),
                pltpu.SemaphoreType.DMA((na,))]

    def _src(self, a, slot):
        return self.srcs[a] if self.gather else _block(self.srcs[a], slot)

    def _local(self, a):
        return pltpu.make_async_copy(self._src(a, self.me), self.dsts[a].at[self.me],
                                     self.local_sems.at[a])

    def _send(self, a, k):
        peer = self.peers[k]
        return pltpu.make_async_remote_copy(
            src_ref=self._src(a, _index(*peer)), dst_ref=self.dsts[a].at[self.me],
            send_sem=self.send_sems.at[7 * a + k], recv_sem=self.recv_sems.at[7 * a + k],
            device_id=peer, device_id_type=MESH)

    def _arrival(self, a, k):
        landed = self.dsts[a].at[_index(*self.peers[k])]
        return pltpu.make_async_remote_copy(
            src_ref=landed, dst_ref=landed,
            send_sem=self.send_sems.at[7 * a + k], recv_sem=self.recv_sems.at[7 * a + k],
            device_id=self.peers[k], device_id_type=MESH)

    def start(self):
        for a in range(len(self.srcs)):
            self._local(a).start()
            for k in range(N_DEV - 1):
                self._send(a, k).start()

    def wait(self):
        for a in range(len(self.srcs)):
            for k in range(N_DEV - 1):
                self._arrival(a, k).wait_recv()
            for k in range(N_DEV - 1):
                self._send(a, k).wait_send()
            self._local(a).wait()


def _exchange_shapes(arrays, gather):
    def shape(a):
        if gather:
            return IN_BLOCKED if a.shape == (D_MODEL, SHARD_PAD) else (N_DEV,) + a.shape
        return (N_DEV, D_MODEL, SHARD_PAD) if a.shape == IN_BLOCKED else a.shape
    return [_sds(shape(a), a.dtype) for a in arrays]


def _exchange_call(arrays, gather, name):
    na = len(arrays)

    def body(*refs):
        ex = _Exchange(refs[:na], refs[na:2 * na], refs[2 * na:], gather)
        ex.start()
        ex.wait()

    return pl.pallas_call(
        body, name=name,
        in_specs=[HBM_SPEC] * na, out_specs=[HBM_SPEC] * na,
        out_shape=_exchange_shapes(arrays, gather),
        scratch_shapes=_Exchange.semaphores(na),
    )(*arrays)


def _layer_fwd(x, small, w_in, g_rest, rest_shards, tabs, next_shards):
    attn_g, sinks, gq, gkv = small
    proj, h, gathered_rest = _in_proj(x, attn_g, w_in, rest_shards)
    w_q, w_kv, w_o = _rest_from_gathered(*(gathered_rest if rest_shards else g_rest))
    swa = _swa_fwd(proj, sinks)
    qh, kh, vh, vth = _mla_prep(proj, gq, gkv, w_q, w_kv, tabs)
    mla, lse, gathered = _mla_fwd(qh, kh, vth, next_shards)
    x_next, y = _out_proj(x, proj, swa, mla, w_o)
    return x_next, (x, proj, h, swa, qh, kh, vh, mla, lse, y), (w_in, w_q, w_kv, w_o), gathered


def _layer_bwd(dx, dxb, saved, small, weights, tabs, pending, send_own):
    attn_g, sinks, gq, gkv = small
    w_in, w_q, w_kv, w_o = weights
    x, proj, h, swa, qh, kh, vh, mla, lse, y = saved
    d_o = _matmul_tn(y, dxb, "grad_w_out")
    do_a, do_b, dgate, dlt = _out_proj_bwd(dxb, proj, swa, mla, w_o)
    dqa, dkva, dsk = _swa_bwd(proj, sinks, do_a, swa)
    dqh, dkh, dvh, received = _mla_bwd(qh, kh, vh, do_b, dlt, lse, pending)
    dlat, d_q, d_kv, d_gq, d_gkv = _mla_prep_bwd(dqh, dkh, dvh, proj, gq, gkv, w_q, w_kv, tabs)
    rest_blocks = _rest_grad_blocks(d_q, d_kv, d_o)
    d_inp, got_rest = _grad_w_in(h, dqa, dgate, dlat, dkva, rest_blocks if send_own else [])
    in_block = _in_grad_blocks(d_inp)
    dx, dxb, d_attn, got_in = _in_proj_bwd(dqa, dgate, dlat, dkva, w_in, x, dx, attn_g,
                                           [in_block] if send_own else [])
    small_grads = (d_attn, dsk[0:1, 0:SWA_HEADS], d_gq, d_gkv)
    return dx, dxb, small_grads, [in_block] + rest_blocks, received, got_in + got_rest


def _pack_small_grads(small_grads, d_final, loss):
    d_attn, d_sink, d_gq, d_gkv = zip(*small_grads)
    return jnp.concatenate([
        jnp.concatenate(d_attn, axis=0).reshape(64, 128),
        jnp.concatenate(d_gq, axis=0).reshape(12, 128),
        jnp.concatenate(d_gkv, axis=0).reshape(8, 128),
        d_final.reshape(16, 128),
        jnp.pad(jnp.concatenate(d_sink, axis=1), ((0, 0), (0, 64))),
        loss[0:1],
        jnp.zeros((PACK_ROWS - ROW_LOSS - 1, 128), F32)], axis=0)


def _pack_small(attn, qa, kva, final, sinks):
    return jnp.concatenate([
        attn.reshape(64, 128), qa.reshape(12, 128), kva.reshape(8, 128), final.reshape(16, 128),
        jnp.pad(sinks.reshape(1, 64), ((0, 0), (0, 64))),
        jnp.zeros((PACK_ROWS - ROW_SINK - 1, 128), F32)], axis=0)


def _unpack_small(p):
    return (p[ROW_ATTN:ROW_QA].reshape(DEPTH, D_MODEL), p[ROW_SINK, 0:64].reshape(DEPTH, SWA_HEADS),
            p[ROW_QA:ROW_KVA].reshape(DEPTH, Q_RANK), p[ROW_KVA:ROW_FINAL].reshape(DEPTH, KV_RANK),
            p[ROW_FINAL:ROW_SINK].reshape(D_MODEL))


def kernel(x, attn_norm_g, w_in, swa_sinks, q_a_norm_g, kv_a_norm_g, w_q_b, w_kv_b, w_out, final_norm_g, loss_target, m_attn_norm_g, m_w_in, m_swa_sinks, m_q_a_norm_g, m_kv_a_norm_g, m_w_q_b, m_w_kv_b, m_w_out, m_final_norm_g, v_attn_norm_g, v_w_in, v_swa_sinks, v_q_a_norm_g, v_kv_a_norm_g, v_w_q_b, v_w_kv_b, v_w_out, v_final_norm_g):
    xs, tgt = x[0], loss_target[0]
    tabs = _rope_tables(xs.shape[0])
    shards = [w.astype(MXU_DTYPE) for w in (w_in, w_q_b, w_kv_b, w_out)]
    shards[0] = jnp.pad(shards[0], ((0, 0), (0, 0), (0, SHARD_PAD - SHARD_COLS)))
    layer_shards = lambda l: [w[l] for w in shards]
    smalls = [(attn_norm_g[l:l + 1], swa_sinks[l], q_a_norm_g[l:l + 1], kv_a_norm_g[l:l + 1])
              for l in range(DEPTH)]

    gathered = list(_all_gather(layer_shards(0)[:1], "gather_weights")) + [None] * 3
    weights, saved = [None] * DEPTH, []
    for l in range(DEPTH):
        next_shards = layer_shards(l + 1) if l + 1 < DEPTH else []
        rest_shards = layer_shards(0)[1:] if l == 0 else []
        xs, acts, weights[l], gathered = _layer_fwd(
            xs, smalls[l], _w_in_from_gathered(gathered[0]), gathered[1:], rest_shards, tabs,
            next_shards)
        saved.append(acts)
    dx, dxb, d_final, loss = _final_loss(xs, tgt, final_norm_g.reshape(1, D_MODEL))

    received, small_grads, pending = [None] * DEPTH, [None] * DEPTH, []
    for l in reversed(range(DEPTH)):
        dx, dxb, small_grads[l], blocks, arrived, arrived_own = _layer_bwd(
            dx, dxb, saved[l], smalls[l], weights[l], tabs, pending, send_own=(l == 0))
        if pending:
            received[l + 1] = arrived
        pending = blocks
    received[0] = arrived_own
    small = _exchange_call([_pack_small_grads(small_grads, d_final, loss)], True, "gather_small")[0]

    big = []
    for a, (w, m, v, name) in enumerate(zip((w_in, w_q_b, w_kv_b, w_out),
                                            (m_w_in, m_w_q_b, m_w_kv_b, m_w_out),
                                            (v_w_in, v_w_q_b, v_w_kv_b, v_w_out),
                                            ("adamw_w_in", "adamw_w_q_b", "adamw_w_kv_b",
                                             "adamw_w_out"))):
        cols = w.shape[-1]
        flat = lambda t: t.reshape(-1, cols)
        outs = _reduce_adamw([received[l][a] for l in range(DEPTH)], flat(w), flat(m), flat(v), name)
        big.append([t.reshape(w.shape) for t in outs])

    sm = _reduce_adamw(
        [small],
        _pack_small(attn_norm_g, q_a_norm_g, kv_a_norm_g, final_norm_g, swa_sinks),
        _pack_small(m_attn_norm_g, m_q_a_norm_g, m_kv_a_norm_g, m_final_norm_g, m_swa_sinks),
        _pack_small(v_attn_norm_g, v_q_a_norm_g, v_kv_a_norm_g, v_final_norm_g, v_swa_sinks),
        "adamw_small")
    loss = sm[0][ROW_LOSS, 0]
    kinds = []
    for t in range(4):
        attn, sinks, qa, kva, final = _unpack_small(sm[t])
        b_in, b_q, b_kv, b_o = (big[i][t] for i in range(4))
        kinds.append((attn, b_in, sinks, qa, kva, b_q, b_kv, b_o, final))
    return (loss, dx[None], *kinds[0], *kinds[1], *kinds[2], *kinds[3])
```

```python
import functools

import jax
import jax.numpy as jnp
import numpy as np
from jax import lax
from jax.experimental import pallas as pl
from jax.experimental.pallas import tpu as pltpu

F32 = jnp.float32
BF16 = jnp.bfloat16
MXU_DTYPE = BF16
GRAD_DTYPE = BF16
PROJ_DTYPE = BF16
ATTN_DTYPE = BF16

D_MODEL = 2048
DEPTH = 4
EPS = 1e-6
NEG = -1e30
BLOCK = 128
SWA_HEADS = 16
MLA_HEADS = 8
Q_RANK = 384
KV_RANK = 256
MLA_SCALE = 192 ** -0.5
MLA_C2 = MLA_SCALE * 1.4426950408889634
ROPE_THETA = 10000.0
IN_WIDTH = 4032

ADAM_LR = 0.001
ADAM_B1 = 0.9
ADAM_B2 = 0.999
ADAM_EPS = 1e-08
ADAM_WD = 0.01
ADAM_STEP = 10

N_DEV = 8
MESH = pl.DeviceIdType.MESH

NP = 4096
QA, GA, GB, CQ, CKV, KR, KVA = 0, 1024, 2048, 3072, 3456, 3712, 3840

ROW_ATTN, ROW_QA, ROW_KVA, ROW_FINAL, ROW_SINK, ROW_LOSS, PACK_ROWS = 0, 64, 76, 84, 100, 101, 104

VMEM_LIMIT = 56 * 1024 * 1024
MLA_TILE = 512
GATHER_FORWARD_HEAD = 5


def _sds(shape, dtype):
    return jax.ShapeDtypeStruct(shape, dtype)


def _params(n_axes):
    return pltpu.CompilerParams(dimension_semantics=("arbitrary",) * n_axes,
                                vmem_limit_bytes=VMEM_LIMIT)


def _mm(a, b):
    return jnp.dot(a.astype(MXU_DTYPE), b.astype(MXU_DTYPE), preferred_element_type=F32)


def _mm_nt(a, b):
    return lax.dot_general(a.astype(MXU_DTYPE), b.astype(MXU_DTYPE),
                           (((1,), (1,)), ((), ())), preferred_element_type=F32)


def _mm_tn(a, b):
    return lax.dot_general(a.astype(MXU_DTYPE), b.astype(MXU_DTYPE),
                           (((0,), (0,)), ((), ())), preferred_element_type=F32)


def _rownorm(x):
    r = lax.rsqrt(jnp.mean(x * x, axis=-1, keepdims=True) + EPS)
    return x * r, r


def _rownorm_bwd(dxh, xh, r):
    return r * (dxh - xh * jnp.mean(dxh * xh, axis=-1, keepdims=True))


def _rope(t, c, s1, s2):
    return t * c + pltpu.roll(t, 32, 1) * s1 + pltpu.roll(t, 96, 1) * s2


SHARD_COLS = IN_WIDTH // N_DEV
SHARD_PAD = 512


def _orig_col_of_padded():
    o = np.full((NP,), -1, np.int64)
    for start, width, orig in ((QA, 1024, 0), (KVA, 256, 1024), (GA, 1024, 1280), (CQ, 384, 2304),
                               (CKV, 256, 2688), (KR, 64, 2944), (GB, 1024, 3008)):
        o[start:start + width] = np.arange(orig, orig + width)
    return o


def _device_major_src():
    o = _orig_col_of_padded()
    return np.where(o >= 0, o + (SHARD_PAD - SHARD_COLS) * (o // SHARD_COLS), -1)


def _kernel_layout_src():
    o = _orig_col_of_padded()
    where = np.full((IN_WIDTH,), -1, np.int64)
    where[o[o >= 0]] = np.nonzero(o >= 0)[0]
    e = np.arange(N_DEV * SHARD_PAD)
    k, c = e // SHARD_PAD, e % SHARD_PAD
    return np.where(c < SHARD_COLS, where[np.minimum(SHARD_COLS * k + c, IN_WIDTH - 1)], -1)


def _permute_columns(x, src_of, name):
    rows, n_in = x.shape
    n_out = len(src_of)
    plan, mats = [], []
    for t in range(n_out // 128):
        srcs = src_of[128 * t:128 * (t + 1)]
        entry = []
        for u in sorted(set(int(s) // 128 for s in srcs if s >= 0)):
            m = np.zeros((128, 128), np.float32)
            for c, s in enumerate(srcs):
                if s >= 0 and s // 128 == u:
                    m[s % 128, c] = 1.0
            entry.append((u, len(mats)))
            mats.append(m)
        plan.append(entry)
    tr = min(512, rows)

    def body(x_ref, p_ref, o_ref):
        for t, entry in enumerate(plan):
            acc = jnp.zeros((tr, 128), F32)
            for u, idx in entry:
                acc = acc + jnp.dot(x_ref[:, 128 * u:128 * (u + 1)], p_ref[idx],
                                    preferred_element_type=F32)
            o_ref[:, 128 * t:128 * (t + 1)] = acc.astype(o_ref.dtype)

    table = jnp.asarray(np.stack(mats), x.dtype)
    return pl.pallas_call(
        body, name=name, grid=(rows // tr,),
        in_specs=[pl.BlockSpec((tr, n_in), lambda i: (i, 0)),
                  pl.BlockSpec(table.shape, lambda i: (0, 0, 0))],
        out_specs=pl.BlockSpec((tr, n_out), lambda i: (i, 0)),
        out_shape=_sds((rows, n_out), x.dtype),
        compiler_params=_params(1),
    )(x, table)


def _w_in_from_gathered(g_in):
    return _permute_columns(g_in, _device_major_src(), "w_in_layout")


def _rest_from_gathered(g_qb, g_kvb, g_out):
    qb = g_qb.transpose(1, 0, 2)
    rope = jnp.pad(qb[..., 128:], ((0, 0), (0, 0), (0, 64)))
    w_q = jnp.concatenate([qb[..., :128].reshape(Q_RANK, 1024),
                           rope.reshape(Q_RANK, 1024)], axis=-1)
    kvb = g_kvb.transpose(1, 0, 2)
    w_kv = jnp.concatenate([kvb[..., :128].reshape(KV_RANK, 1024),
                            kvb[..., 128:].reshape(KV_RANK, 1024)], axis=-1)
    w_o = g_out.reshape(D_MODEL, D_MODEL)
    return w_q, w_kv, w_o


def _in_grad_blocks(d_inp):
    return _permute_columns(d_inp, _kernel_layout_src(), "grad_w_in_layout")


def _rest_grad_blocks(d_q, d_kv, d_o):
    qn = d_q[:, :1024].reshape(Q_RANK, 8, 128)
    qr = d_q[:, 1024:].reshape(Q_RANK, 8, 128)[..., :64]
    b_q = jnp.concatenate([qn, qr], axis=-1).transpose(1, 0, 2)
    kn = d_kv[:, :1024].reshape(KV_RANK, 8, 128)
    vv = d_kv[:, 1024:].reshape(KV_RANK, 8, 128)
    b_kv = jnp.concatenate([kn, vv], axis=-1).transpose(1, 0, 2)
    b_o = d_o.reshape(N_DEV, 256, D_MODEL)
    return [b_q, b_kv, b_o]


def _rope_tables(s):
    pos = jnp.arange(s, dtype=F32)
    inv_freq = ROPE_THETA ** (-jnp.arange(0, 64, 2, dtype=F32) / 64)
    ang = pos[:, None] * inv_freq[None, :]
    cos, sin = jnp.cos(ang), jnp.sin(ang)
    z32 = jnp.zeros((s, 32), F32)
    z64 = jnp.zeros((s, 64), F32)
    c = jnp.concatenate([cos, cos, z64], axis=1)
    s1 = jnp.concatenate([z32, sin, z64], axis=1)
    s2 = jnp.concatenate([-sin, z32, z64], axis=1)
    return c, s1, s2


def _in_proj(x, g, w, shards):
    s = x.shape[0]
    tm, tn = min(512, s), 1024
    nm = s // tm
    na = len(shards)

    def body(*refs):
        x_ref, g_ref, w_ref = refs[:3]
        o_ref, h_ref = refs[3 + na:5 + na]
        i = pl.program_id(0)
        if na:
            ex = _Gather(refs[3:3 + na], refs[5 + na:5 + 2 * na], refs[5 + 2 * na:])

            @pl.when(i == 0)
            def _():
                ex.start()

            @pl.when(i == nm // 2)
            def _():
                ex.forward()

        xh, _ = _rownorm(x_ref[...])
        h_ref[...] = (xh * g_ref[...]).astype(h_ref.dtype)
        for j in range(NP // tn):
            cols = slice(j * tn, (j + 1) * tn)
            o_ref[:, cols] = jnp.dot(h_ref[...], w_ref[:, cols],
                                     preferred_element_type=F32).astype(o_ref.dtype)

        if na:
            @pl.when(i == nm - 1)
            def _():
                ex.finish()

    row = lambda i: (i, 0)
    fixed = lambda i: (0, 0)
    outs = pl.pallas_call(
        body, name="in_proj_gather" if na else "in_proj", grid=(nm,),
        in_specs=[pl.BlockSpec((tm, D_MODEL), row), pl.BlockSpec((1, D_MODEL), fixed),
                  pl.BlockSpec((D_MODEL, NP), fixed, pipeline_mode=pl.Buffered(1))]
        + [HBM_SPEC] * na,
        out_specs=[pl.BlockSpec((tm, NP), row), pl.BlockSpec((tm, D_MODEL), row)]
        + [HBM_SPEC] * na,
        out_shape=[_sds((s, NP), PROJ_DTYPE), _sds((s, D_MODEL), MXU_DTYPE)]
        + _exchange_shapes(shards, True),
        scratch_shapes=_Exchange.semaphores(na) if na else [],
        compiler_params=_params(1),
    )(x, g, w, *shards)
    return outs[0], outs[1], list(outs[2:])


def _swa_slopes():
    return [2.0 ** (-8.0 * (h + 1) / SWA_HEADS) for h in range(SWA_HEADS)]


SWA_STACK = 8 * BLOCK


def _swa_head(j, a):
    return 2 * (4 * j + a % 4) + a // 4


def _swa_operands(kv_p, kv_c):
    kk = jnp.concatenate([kv_p[:, :128], kv_c[:, :128]], axis=0)
    vv = jnp.concatenate([kv_p[:, 128:], kv_c[:, 128:]], axis=0)
    left = lax.broadcasted_iota(jnp.int32, (2 * BLOCK, 128), 1) < 64

    def heads(t):
        return [jnp.where(left, t, 0.0), jnp.where(left, pltpu.roll(t, 64, 1), 0.0)]

    return heads(kk), heads(vv), left


def _swa_by_block(vals):
    a = lax.broadcasted_iota(jnp.int32, (1, SWA_STACK), 1) >> 7
    row = jnp.full((1, SWA_STACK), vals[7], F32)
    for t in range(6, -1, -1):
        row = jnp.where(a == t, vals[t], row)
    return row


def _swa_bias(n, j):
    slopes = _swa_slopes()
    ki = lax.broadcasted_iota(jnp.int32, (2 * BLOCK, SWA_STACK), 0)
    r = lax.broadcasted_iota(jnp.int32, (2 * BLOCK, SWA_STACK), 1)
    delta = BLOCK + (r & (BLOCK - 1)) - ki
    valid = (delta >= 0) & (delta < BLOCK) & ((n - 1) * BLOCK + ki >= 0)
    slope = _swa_by_block([slopes[_swa_head(j, a)] for a in range(8)])
    return jnp.where(valid, -slope * delta.astype(F32), NEG)


def _swa_fill_bias(n, bias_ref):
    @pl.when(n <= 1)
    def _():
        for j in range(2):
            bias_ref[j] = _swa_bias(n, j)


def _swa_sink_row(sink_ref, j):
    return _swa_by_block([sink_ref[_swa_head(j, a)] for a in range(8)])


def _swa_pairs(ref, j):
    return jnp.concatenate([ref[:, 128 * (4 * j + a):128 * (4 * j + a + 1)] for a in range(4)],
                           axis=0).astype(F32)


def _swa_stack(pairs):
    return jnp.concatenate([pairs, pltpu.roll(pairs, 64, 1)], axis=0)


def _swa_unstack(t):
    left = lax.broadcasted_iota(jnp.int32, (4 * BLOCK, 128), 1) < 64
    return jnp.where(left, t[0:4 * BLOCK], pltpu.roll(t[4 * BLOCK:8 * BLOCK], 64, 1))


def _swa_softmax(qs, kmat, bias, sink):
    sc = _mm_nt(kmat, qs) + bias
    m = jnp.maximum(jnp.max(sc, axis=0, keepdims=True), sink)
    ex = jnp.exp(sc - m)
    es = jnp.exp(sink - m)
    return ex, es, 1.0 / (jnp.sum(ex, axis=0, keepdims=True) + es)


def _swa_fwd(proj, sinks):
    s = proj.shape[0]
    nb = s // BLOCK

    def body(sink_ref, q_ref, kp_ref, kc_ref, o_ref, bias_ref):
        n = pl.program_id(0)
        _swa_fill_bias(n, bias_ref)
        k, v, _ = _swa_operands(kp_ref[...].astype(F32), kc_ref[...].astype(F32))
        for j in range(2):
            qs = _swa_stack(_swa_pairs(q_ref, j) * 0.125)
            ex, _, inv = _swa_softmax(qs, k[j], bias_ref[j], _swa_sink_row(sink_ref, j))
            o_t = _mm(v[j].T, ex) * inv
            out = jnp.concatenate([o_t[0:64, 0:4 * BLOCK], o_t[0:64, 4 * BLOCK:8 * BLOCK]], axis=0).T
            for a in range(4):
                o_ref[:, 128 * (4 * j + a):128 * (4 * j + a + 1)] = out[128 * a:128 * (a + 1)].astype(
                    o_ref.dtype)

    return pl.pallas_call(
        body, name="swa_fwd", grid=(nb,),
        in_specs=[pl.BlockSpec(memory_space=pltpu.SMEM),
                  pl.BlockSpec((BLOCK, 1024), lambda n: (n, 0)),
                  pl.BlockSpec((BLOCK, 256), lambda n: (jnp.maximum(n - 1, 0), KVA // 256)),
                  pl.BlockSpec((BLOCK, 256), lambda n: (n, KVA // 256))],
        out_specs=pl.BlockSpec((BLOCK, 1024), lambda n: (n, 0)),
        out_shape=_sds((s, 1024), ATTN_DTYPE),
        scratch_shapes=[pltpu.VMEM((2, 2 * BLOCK, SWA_STACK), F32)],
        compiler_params=_params(1),
    )(sinks, proj, proj, proj)


def _mla_prep(proj, gq, gkv, w_q, w_kv, tabs):
    s = proj.shape[0]
    tm = min(512, s)
    c, s1, s2 = tabs

    def body(p_ref, gq_ref, gkv_ref, wq_ref, wkv_ref, c_ref, s1_ref, s2_ref,
             q_ref, k_ref, v_ref, vt_ref):
        cqh, _ = _rownorm(p_ref[:, 0:384].astype(F32))
        ckvh, _ = _rownorm(p_ref[:, 384:640].astype(F32))
        q = _mm(cqh * gq_ref[...], wq_ref[...])
        kv = _mm(ckvh * gkv_ref[...], wkv_ref[...])
        cc, ss1, ss2 = c_ref[...], s1_ref[...], s2_ref[...]
        krr = _rope(p_ref[:, 640:768].astype(F32), cc, ss1, ss2).astype(k_ref.dtype)
        for h in range(MLA_HEADS):
            q_ref[h, :, 0:128] = q[:, 128 * h:128 * (h + 1)].astype(q_ref.dtype)
            q_ref[h, :, 128:256] = _rope(q[:, 1024 + 128 * h:1024 + 128 * (h + 1)],
                                         cc, ss1, ss2).astype(q_ref.dtype)
            k_ref[h, :, 0:128] = kv[:, 128 * h:128 * (h + 1)].astype(k_ref.dtype)
            k_ref[h, :, 128:256] = krr
            vv = kv[:, 1024 + 128 * h:1024 + 128 * (h + 1)]
            v_ref[h] = vv.astype(v_ref.dtype)
            vt_ref[h, 0:128, :] = vv.T.astype(vt_ref.dtype)
            vt_ref[h, 128:256, :] = jnp.ones((128, tm), vt_ref.dtype)

    row = lambda i: (i, 0)
    fixed = lambda i: (0, 0)
    return pl.pallas_call(
        body, name="mla_prep", grid=(s // tm,),
        in_specs=[pl.BlockSpec((tm, 768), lambda i: (i, CQ // 768)),
                  pl.BlockSpec((1, Q_RANK), fixed), pl.BlockSpec((1, KV_RANK), fixed),
                  pl.BlockSpec((Q_RANK, 2048), fixed), pl.BlockSpec((KV_RANK, 2048), fixed),
                  pl.BlockSpec((tm, 128), row), pl.BlockSpec((tm, 128), row),
                  pl.BlockSpec((tm, 128), row)],
        out_specs=[pl.BlockSpec((MLA_HEADS, tm, 256), lambda i: (0, i, 0)),
                   pl.BlockSpec((MLA_HEADS, tm, 256), lambda i: (0, i, 0)),
                   pl.BlockSpec((MLA_HEADS, tm, 128), lambda i: (0, i, 0)),
                   pl.BlockSpec((MLA_HEADS, 256, tm), lambda i: (0, 0, i))],
        out_shape=[_sds((MLA_HEADS, s, 256), MXU_DTYPE), _sds((MLA_HEADS, s, 256), MXU_DTYPE),
                   _sds((MLA_HEADS, s, 128), MXU_DTYPE), _sds((MLA_HEADS, 256, s), MXU_DTYPE)],
        compiler_params=_params(1),
    )(proj, gq, gkv, w_q, w_kv, c, s1, s2)


def _scores_t(k, q, t, diagonal):
    sc = _mm_nt(k, q)
    if diagonal:
        key = lax.broadcasted_iota(jnp.int32, (t, t), 0)
        query = lax.broadcasted_iota(jnp.int32, (t, t), 1)
        sc = jnp.where(key <= query, sc, NEG)
    return sc


def _mla_fwd(qh, kh, vth, shards):
    s = qh.shape[1]
    t = min(MLA_TILE, s)
    nq = s // t
    na = len(shards)

    def body(*refs):
        q_ref, k_ref, vt_ref = refs[:3]
        o_ref, lse_ref = refs[3 + na:5 + na]
        m_ref, acc_ref = refs[5 + 2 * na:7 + 2 * na]
        h, i = pl.program_id(0), pl.program_id(1)
        if na:
            ex = _Gather(refs[3:3 + na], refs[5 + na:5 + 2 * na], refs[7 + 2 * na:])

            @pl.when((h == 0) & (i == 0))
            def _():
                ex.start()

            @pl.when((h == GATHER_FORWARD_HEAD) & (i == 0))
            def _():
                ex.forward()

        m_ref[...] = jnp.full(m_ref.shape, NEG, F32)
        acc_ref[...] = jnp.zeros(acc_ref.shape, F32)

        def step(start, width, diagonal):
            keys = pl.ds(pl.multiple_of(start, t), width)
            if diagonal:
                sc = _scores_t(k_ref[keys, :], q_ref[...], t, True)
            else:
                sc = _mm_nt(k_ref[keys, :], q_ref[...])
            m_prev = m_ref[...]
            m_new = jnp.maximum(m_prev, jnp.max(sc, axis=0, keepdims=True))
            alpha = jnp.exp2((m_prev - m_new) * MLA_C2)
            p = jnp.exp2((sc - m_new[0:1, :]) * MLA_C2)
            acc_ref[...] = alpha[0:1, :] * acc_ref[...] + _mm(vt_ref[:, keys], p)
            m_ref[...] = m_new

        def below_diagonal(jj, carry):
            step(jj * (4 * t), 4 * t, False)
            return carry

        lax.fori_loop(0, i // 4, below_diagonal, 0)

        @pl.when(i % 4 >= 2)
        def _():
            step((i // 4) * (4 * t), 2 * t, False)

        @pl.when(i % 2 == 1)
        def _():
            step((i - 1) * t, t, False)

        step(i * t, t, True)
        l = acc_ref[128:136, :]
        o_ref[...] = (acc_ref[0:128, :] / l[0:1, :]).T.astype(o_ref.dtype)
        lse_ref[...] = m_ref[...] * MLA_C2 + jnp.log2(l)

        if na:
            @pl.when((h == MLA_HEADS - 1) & (i == nq - 1))
            def _():
                ex.finish()

    head = lambda h, i: (h, 0, 0)
    outs = pl.pallas_call(
        body, name="mla_fwd_gather" if na else "mla_fwd", grid=(MLA_HEADS, nq),
        in_specs=[pl.BlockSpec((None, t, 256), lambda h, i: (h, i, 0)),
                  pl.BlockSpec((None, s, 256), head),
                  pl.BlockSpec((None, 256, s), head)] + [HBM_SPEC] * na,
        out_specs=[pl.BlockSpec((t, 128), lambda h, i: (i, h)),
                   pl.BlockSpec((None, 8, t), lambda h, i: (h, 0, i))] + [HBM_SPEC] * na,
        out_shape=[_sds((s, 1024), ATTN_DTYPE), _sds((MLA_HEADS, 8, s), F32)]
        + _exchange_shapes(shards, True),
        scratch_shapes=[pltpu.VMEM((8, t), F32), pltpu.VMEM((256, t), F32)]
        + (_Exchange.semaphores(na) if na else []),
        compiler_params=_params(2),
    )(qh, kh, vth, *shards)
    return outs[0], outs[1], list(outs[2:])


def _silu_parts(g):
    sg = jax.nn.sigmoid(g)
    return g * sg, sg * (1.0 + g * (1.0 - sg))


def _out_proj(x, proj, swa, mla, w_out):
    s = x.shape[0]
    tm = min(512, s)

    def body(x_ref, ga_ref, gb_ref, a_ref, b_ref, w_ref, xo_ref, y_ref):
        sa, _ = _silu_parts(ga_ref[...].astype(F32))
        sb, _ = _silu_parts(gb_ref[...].astype(F32))
        y_ref[:, 0:1024] = (a_ref[...].astype(F32) * sa).astype(y_ref.dtype)
        y_ref[:, 1024:2048] = (b_ref[...].astype(F32) * sb).astype(y_ref.dtype)
        xo_ref[...] = x_ref[...] + jnp.dot(y_ref[...], w_ref[...], preferred_element_type=F32)

    row = lambda i: (i, 0)
    return pl.pallas_call(
        body, name="out_proj", grid=(s // tm,),
        in_specs=[pl.BlockSpec((tm, D_MODEL), row),
                  pl.BlockSpec((tm, 1024), lambda i: (i, GA // 1024)),
                  pl.BlockSpec((tm, 1024), lambda i: (i, GB // 1024)),
                  pl.BlockSpec((tm, 1024), row), pl.BlockSpec((tm, 1024), row),
                  pl.BlockSpec((D_MODEL, D_MODEL), lambda i: (0, 0), pipeline_mode=pl.Buffered(1))],
        out_specs=[pl.BlockSpec((tm, D_MODEL), row), pl.BlockSpec((tm, D_MODEL), row)],
        out_shape=[_sds((s, D_MODEL), F32), _sds((s, D_MODEL), MXU_DTYPE)],
        compiler_params=_params(1),
    )(x, proj, proj, swa, mla, w_out)


def _final_loss(x, tgt, g):
    s = x.shape[0]
    tm = min(512, s)

    def body(x_ref, t_ref, g_ref, dx_ref, dxb_ref, dg_ref, loss_ref):
        @pl.when(pl.program_id(0) == 0)
        def _():
            dg_ref[...] = jnp.zeros(dg_ref.shape, F32)
            loss_ref[...] = jnp.zeros(loss_ref.shape, F32)
        xh, r = _rownorm(x_ref[...])
        gg = g_ref[...]
        err = xh * gg - t_ref[...]
        per_row = jnp.mean(err * err, axis=-1, keepdims=True)
        loss_ref[...] += 0.5 * jnp.sum(per_row, axis=0, keepdims=True)
        dy = err * (1.0 / D_MODEL)
        dg_ref[...] += jnp.sum(dy * xh, axis=0, keepdims=True)
        dx = _rownorm_bwd(dy * gg, xh, r)
        dx_ref[...] = dx
        dxb_ref[...] = dx.astype(dxb_ref.dtype)

    row = lambda i: (i, 0)
    fixed = lambda i: (0, 0)
    return pl.pallas_call(
        body, name="final_loss", grid=(s // tm,),
        in_specs=[pl.BlockSpec((tm, D_MODEL), row), pl.BlockSpec((tm, D_MODEL), row),
                  pl.BlockSpec((1, D_MODEL), fixed)],
        out_specs=[pl.BlockSpec((tm, D_MODEL), row), pl.BlockSpec((tm, D_MODEL), row),
                   pl.BlockSpec((1, D_MODEL), fixed), pl.BlockSpec((8, 128), fixed)],
        out_shape=[_sds((s, D_MODEL), F32), _sds((s, D_MODEL), MXU_DTYPE), _sds((1, D_MODEL), F32),
                   _sds((8, 128), F32)],
        compiler_params=_params(1),
    )(x, tgt, g)


def _out_proj_bwd(dx, proj, swa, mla, w_out):
    s = dx.shape[0]
    tm = min(512, s)

    def body(dx_ref, ga_ref, gb_ref, a_ref, b_ref, w_ref, doa_ref, dob_ref, dg_ref, dlt_ref):
        dx = dx_ref[...].astype(MXU_DTYPE)
        dya = _mm_nt(dx, w_ref[0:1024, :])
        sa, dsa = _silu_parts(ga_ref[...].astype(F32))
        doa_ref[...] = dya * sa
        dg_ref[:, 0:1024] = (dya * a_ref[...].astype(F32) * dsa).astype(dg_ref.dtype)
        dyb = _mm_nt(dx, w_ref[1024:2048, :])
        sb, dsb = _silu_parts(gb_ref[...].astype(F32))
        b = b_ref[...].astype(F32)
        dob = dyb * sb
        dob_ref[...] = dob.astype(dob_ref.dtype)
        dg_ref[:, 1024:2048] = (dyb * b * dsb).astype(dg_ref.dtype)
        prod = dob * b
        for h in range(MLA_HEADS):
            dlt = jnp.sum(prod[:, 128 * h:128 * (h + 1)], axis=1, keepdims=True)
            dlt_ref[h] = jnp.broadcast_to(dlt, (tm, 128)).T[0:8, :]

    row = lambda i: (i, 0)
    return pl.pallas_call(
        body, name="out_proj_bwd", grid=(s // tm,),
        in_specs=[pl.BlockSpec((tm, D_MODEL), row),
                  pl.BlockSpec((tm, 1024), lambda i: (i, GA // 1024)),
                  pl.BlockSpec((tm, 1024), lambda i: (i, GB // 1024)),
                  pl.BlockSpec((tm, 1024), row), pl.BlockSpec((tm, 1024), row),
                  pl.BlockSpec((D_MODEL, D_MODEL), lambda i: (0, 0), pipeline_mode=pl.Buffered(1))],
        out_specs=[pl.BlockSpec((tm, 1024), row), pl.BlockSpec((tm, 1024), row),
                   pl.BlockSpec((tm, D_MODEL), row),
                   pl.BlockSpec((MLA_HEADS, 8, tm), lambda i: (0, 0, i))],
        out_shape=[_sds((s, 1024), F32), _sds((s, 1024), MXU_DTYPE), _sds((s, D_MODEL), MXU_DTYPE),
                   _sds((MLA_HEADS, 8, s), F32)],
        compiler_params=_params(1),
    )(dx, proj, proj, swa, mla, w_out)


def _matmul_tn(a, b, name):
    s, m = a.shape
    n = b.shape[1]
    tm, tn, tk = min(1024, m), min(1024, n), min(2048, s)
    nk = s // tk

    def body(a_ref, b_ref, o_ref, acc_ref):
        k = pl.program_id(2)

        @pl.when(k == 0)
        def _():
            acc_ref[...] = jnp.zeros(acc_ref.shape, F32)
        acc_ref[...] += _mm_tn(a_ref[...], b_ref[...])

        @pl.when(k == nk - 1)
        def _():
            o_ref[...] = acc_ref[...].astype(o_ref.dtype)

    return pl.pallas_call(
        body, name=name, grid=(m // tm, n // tn, nk),
        in_specs=[pl.BlockSpec((tk, tm), lambda i, j, k: (k, i)),
                  pl.BlockSpec((tk, tn), lambda i, j, k: (k, j))],
        out_specs=pl.BlockSpec((tm, tn), lambda i, j, k: (i, j)),
        out_shape=_sds((m, n), GRAD_DTYPE),
        scratch_shapes=[pltpu.VMEM((tm, tn), F32)],
        compiler_params=_params(3),
    )(a, b)


def _grad_w_in(h, dqa, dgate, dlat, dkva, blocks):
    s = h.shape[0]
    tm, tn, tk = 1024, 1024, min(2048, s)
    nk = s // tk
    grid = (D_MODEL // tm, NP // tn, nk)
    na = len(blocks)

    def body(*refs):
        a_ref, dqa_ref, dg8_ref, dlat_ref, dkva_ref = refs[:5]
        o_ref = refs[5 + na]
        acc_ref = refs[6 + 2 * na]
        i, j, k = pl.program_id(0), pl.program_id(1), pl.program_id(2)
        if na:
            ex = _Exchange(refs[5:5 + na], refs[6 + na:6 + 2 * na], refs[7 + 2 * na:], gather=False)

            @pl.when((i == 0) & (j == 0) & (k == 0))
            def _():
                ex.start()

        @pl.when(k == 0)
        def _():
            acc_ref[...] = jnp.zeros(acc_ref.shape, F32)

        @pl.when(j == QA // tn)
        def _():
            acc_ref[...] += _mm_tn(a_ref[...], dqa_ref[...])

        @pl.when((j == GA // tn) | (j == GB // tn))
        def _():
            acc_ref[...] += _mm_tn(a_ref[...], dg8_ref[...])

        @pl.when(j == CQ // tn)
        def _():
            acc_ref[:, 0:768] += _mm_tn(a_ref[...], dlat_ref[...])
            acc_ref[:, 768:1024] += _mm_tn(a_ref[...], dkva_ref[...])

        @pl.when(k == nk - 1)
        def _():
            o_ref[...] = acc_ref[...].astype(o_ref.dtype)

        if na:
            @pl.when((i == grid[0] - 1) & (j == grid[1] - 1) & (k == nk - 1))
            def _():
                ex.wait()

    def when(group, width):
        return pl.BlockSpec((tk, width), lambda i, j, k: (jnp.where(j == group, k, 0), 0))

    outs = pl.pallas_call(
        body, name="grad_w_in_scatter" if na else "grad_w_in", grid=grid,
        in_specs=[pl.BlockSpec((tk, tm), lambda i, j, k: (k, i)),
                  when(QA // tn, 1024),
                  pl.BlockSpec((tk, 1024), lambda i, j, k: (
                      jnp.where((j == GA // tn) | (j == GB // tn), k, 0),
                      jnp.clip(j - GA // tn, 0, 1))),
                  when(CQ // tn, 768), when(CQ // tn, 256)] + [HBM_SPEC] * na,
        out_specs=[pl.BlockSpec((tm, tn), lambda i, j, k: (i, j))] + [HBM_SPEC] * na,
        out_shape=[_sds((D_MODEL, NP), GRAD_DTYPE)] + _exchange_shapes(blocks, False),
        scratch_shapes=[pltpu.VMEM((tm, tn), F32)] + (_Exchange.semaphores(na) if na else []),
        compiler_params=_params(3),
    )(h, dqa, dgate, dlat, dkva, *blocks)
    return outs[0], list(outs[1:])


def _swa_bwd(proj, sinks, do, o):
    s = proj.shape[0]
    nb = s // BLOCK

    def body(sink_ref, q_ref, kp_ref, kc_ref, do_ref, o_ref, dq_ref, dkv_ref, dsink_ref,
             carry_ref, bias_ref):
        n = pl.program_id(0)
        _swa_fill_bias(n, bias_ref)

        @pl.when(n == 0)
        def _():
            carry_ref[...] = jnp.zeros(carry_ref.shape, F32)
            dsink_ref[...] = jnp.zeros(dsink_ref.shape, F32)

        @pl.when(n < nb)
        def _():
            k, v, left = _swa_operands(kp_ref[...].astype(F32), kc_ref[...].astype(F32))
            lane_s = lax.broadcasted_iota(jnp.int32, (8, 128), 1)
            dk, dv = [None, None], [None, None]
            dsink = jnp.zeros((8, 128), F32)
            for j in range(2):
                qs = _swa_stack(_swa_pairs(q_ref, j) * 0.125)
                do_pairs = _swa_pairs(do_ref, j)
                dos = _swa_stack(do_pairs)
                prod_t = (do_pairs * _swa_pairs(o_ref, j)).T
                dlt = jnp.concatenate([jnp.sum(prod_t[0:64], axis=0, keepdims=True),
                                       jnp.sum(prod_t[64:128], axis=0, keepdims=True)], axis=1)
                ex, es, inv = _swa_softmax(qs, k[j], bias_ref[j], _swa_sink_row(sink_ref, j))
                p = ex * inv
                ds = p * (_mm_nt(v[j], dos) - dlt)
                sink_term = es * inv * dlt
                for a in range(8):
                    dsh = -jnp.sum(sink_term[:, 128 * a:128 * (a + 1)], axis=1, keepdims=True)
                    dsink = dsink + jnp.where(lane_s == _swa_head(j, a), dsh, 0.0)
                dq = _swa_unstack(_mm_tn(ds, k[j]))
                dv[j] = _mm(p, dos)
                dk[j] = _mm(ds, qs)
                for a in range(4):
                    cols = slice(128 * (4 * j + a), 128 * (4 * j + a + 1))
                    dq_ref[:, cols] = (dq[128 * a:128 * (a + 1)] * 0.125).astype(dq_ref.dtype)

            def merge(t):
                return jnp.where(left, t[0], pltpu.roll(t[1], 64, 1))

            contrib = jnp.concatenate([merge(dk), merge(dv)], axis=1)
            dkv_ref[...] = (carry_ref[...] + contrib[0:BLOCK]).astype(dkv_ref.dtype)
            carry_ref[...] = contrib[BLOCK:2 * BLOCK]
            dsink_ref[...] += dsink

        @pl.when(n == nb)
        def _():
            dkv_ref[...] = carry_ref[...].astype(dkv_ref.dtype)

    cur = lambda n: (jnp.minimum(n, nb - 1), 0)
    return pl.pallas_call(
        body, name="swa_bwd", grid=(nb + 1,),
        in_specs=[pl.BlockSpec(memory_space=pltpu.SMEM),
                  pl.BlockSpec((BLOCK, 1024), cur),
                  pl.BlockSpec((BLOCK, 256), lambda n: (jnp.clip(n - 1, 0, nb - 1), KVA // 256)),
                  pl.BlockSpec((BLOCK, 256), lambda n: (jnp.minimum(n, nb - 1), KVA // 256)),
                  pl.BlockSpec((BLOCK, 1024), cur), pl.BlockSpec((BLOCK, 1024), cur)],
        out_specs=[pl.BlockSpec((BLOCK, 1024), cur),
                   pl.BlockSpec((BLOCK, 256), lambda n: (jnp.maximum(n - 1, 0), 0)),
                   pl.BlockSpec((8, 128), lambda n: (0, 0))],
        out_shape=[_sds((s, 1024), MXU_DTYPE), _sds((s, 256), MXU_DTYPE), _sds((8, 128), F32)],
        scratch_shapes=[pltpu.VMEM((BLOCK, 256), F32), pltpu.VMEM((2, 2 * BLOCK, SWA_STACK), F32)],
        compiler_params=_params(1),
    )(sinks, proj, proj, proj, do, o)


def _mla_bwd(qh, kh, vh, do, dlt, lse, blocks):
    s = qh.shape[1]
    t = min(MLA_TILE, s)
    nq = s // t

    na = len(blocks)

    def body(*refs):
        q_ref, k_ref, v_ref, do_ref, dlt_ref, lse_ref = refs[:6]
        dq_ref, dk_ref, dv_ref = refs[6 + na:9 + na]
        h, j = pl.program_id(0), pl.program_id(1)
        if na:
            ex = _Exchange(refs[6:6 + na], refs[9 + na:9 + 2 * na], refs[9 + 2 * na:], gather=False)

            @pl.when((h == 0) & (j == 0))
            def _():
                ex.start()

        def step(start, width, diagonal):
            rows = pl.ds(pl.multiple_of(start, t), width)
            q, k, dout = q_ref[rows, :], k_ref[...], do_ref[rows, :]
            sc = _scores_t(k, q, t, True) if diagonal else _mm_nt(k, q)
            p = jnp.exp2(sc * MLA_C2 - lse_ref[0:1, rows])
            dv = _mm(p, dout)
            ds = p * (_mm_nt(v_ref[...], dout) - dlt_ref[0:1, rows])
            dk = _mm(ds, q)
            dq = _mm_tn(ds, k)
            return rows, dq, dk, dv

        rows, dq, dk, dv = step(j * t, t, True)
        dk_ref[...] = dk
        dv_ref[...] = dv

        @pl.when(j == 0)
        def _():
            dq_ref[rows, :] = dq * MLA_SCALE

        @pl.when(j > 0)
        def _():
            dq_ref[rows, :] = (dq_ref[rows, :] + dq) * MLA_SCALE

        def above_diagonal(start, width):
            rows, dq, dk, dv = step(start, width, False)
            dk_ref[...] += dk
            dv_ref[...] += dv

            @pl.when(j == 0)
            def _():
                dq_ref[rows, :] = dq

            @pl.when(j > 0)
            def _():
                dq_ref[rows, :] += dq

        n_above = nq - 1 - j

        def quad(jj, carry):
            above_diagonal((j + 1 + 4 * jj) * t, 4 * t)
            return carry

        lax.fori_loop(0, n_above // 4, quad, 0)

        @pl.when(n_above % 4 >= 2)
        def _():
            above_diagonal((j + 1 + 4 * (n_above // 4)) * t, 2 * t)

        @pl.when(n_above % 2 == 1)
        def _():
            above_diagonal((nq - 1) * t, t)

        dk_ref[...] *= MLA_SCALE

        if na:
            @pl.when((h == MLA_HEADS - 1) & (j == nq - 1))
            def _():
                ex.wait()

    head = lambda h, j: (h, 0, 0)
    kv_map = lambda h, j: (h, j, 0)
    outs = pl.pallas_call(
        body, name="mla_bwd_scatter" if na else "mla_bwd", grid=(MLA_HEADS, nq),
        in_specs=[pl.BlockSpec((None, s, 256), head), pl.BlockSpec((None, t, 256), kv_map),
                  pl.BlockSpec((None, t, 128), kv_map),
                  pl.BlockSpec((s, 128), lambda h, j: (0, h)),
                  pl.BlockSpec((None, 8, s), head), pl.BlockSpec((None, 8, s), head)]
        + [HBM_SPEC] * na,
        out_specs=[pl.BlockSpec((None, s, 256), head),
                   pl.BlockSpec((None, t, 256), kv_map), pl.BlockSpec((None, t, 128), kv_map)]
        + [HBM_SPEC] * na,
        out_shape=[_sds((MLA_HEADS, s, 256), F32), _sds((MLA_HEADS, s, 256), F32),
                   _sds((MLA_HEADS, s, 128), F32)] + _exchange_shapes(blocks, False),
        scratch_shapes=_Exchange.semaphores(na) if na else [],
        compiler_params=_params(2),
    )(qh, kh, vh, do, dlt, lse, *blocks)
    return outs[0], outs[1], outs[2], list(outs[3:])


def _mla_prep_bwd(dqh, dkh, dvh, proj, gq, gkv, w_q, w_kv, tabs):
    s = proj.shape[0]
    tm = min(256, s)
    nm = s // tm
    c, s1, s2 = tabs

    def body(dq_ref, dk_ref, dv_ref, p_ref, gq_ref, gkv_ref, wq_ref, wkv_ref,
             c_ref, s1_ref, s2_ref, dp_ref, dwq_ref, dwkv_ref, dgq_ref, dgkv_ref,
             dqf_ref, dkvf_ref, dwq_acc, dwkv_acc):
        @pl.when(pl.program_id(0) == 0)
        def _():
            dgq_ref[...] = jnp.zeros(dgq_ref.shape, F32)
            dgkv_ref[...] = jnp.zeros(dgkv_ref.shape, F32)
            dwq_acc[...] = jnp.zeros(dwq_acc.shape, F32)
            dwkv_acc[...] = jnp.zeros(dwkv_acc.shape, F32)
        cc, ns1, ns2 = c_ref[...], -s1_ref[...], -s2_ref[...]
        dkr = jnp.zeros((tm, 128), F32)
        for h in range(MLA_HEADS):
            dqf_ref[:, 128 * h:128 * (h + 1)] = dq_ref[h, :, 0:128].astype(dqf_ref.dtype)
            dqf_ref[:, 1024 + 128 * h:1024 + 128 * (h + 1)] = _rope(
                dq_ref[h, :, 128:256], cc, ns1, ns2).astype(dqf_ref.dtype)
            dkvf_ref[:, 128 * h:128 * (h + 1)] = dk_ref[h, :, 0:128].astype(dkvf_ref.dtype)
            dkvf_ref[:, 1024 + 128 * h:1024 + 128 * (h + 1)] = dv_ref[h].astype(dkvf_ref.dtype)
            dkr = dkr + dk_ref[h, :, 128:256]
        dcqn = _mm_nt(dqf_ref[...], wq_ref[...])
        dckvn = _mm_nt(dkvf_ref[...], wkv_ref[...])
        cqh, rq = _rownorm(p_ref[:, 0:384].astype(F32))
        ckvh, rkv = _rownorm(p_ref[:, 384:640].astype(F32))
        dgq_ref[...] += jnp.sum(dcqn * cqh, axis=0, keepdims=True)
        dgkv_ref[...] += jnp.sum(dckvn * ckvh, axis=0, keepdims=True)
        dp_ref[:, 0:384] = _rownorm_bwd(dcqn * gq_ref[...], cqh, rq).astype(dp_ref.dtype)
        dp_ref[:, 384:640] = _rownorm_bwd(dckvn * gkv_ref[...], ckvh, rkv).astype(dp_ref.dtype)
        dp_ref[:, 640:768] = _rope(dkr, cc, ns1, ns2).astype(dp_ref.dtype)
        dwq_acc[...] += _mm_tn(cqh * gq_ref[...], dqf_ref[...])
        dwkv_acc[...] += _mm_tn(ckvh * gkv_ref[...], dkvf_ref[...])

        @pl.when(pl.program_id(0) == nm - 1)
        def _():
            dwq_ref[...] = dwq_acc[...].astype(dwq_ref.dtype)
            dwkv_ref[...] = dwkv_acc[...].astype(dwkv_ref.dtype)

    row = lambda i: (i, 0)
    fixed = lambda i: (0, 0)
    head = lambda i: (0, i, 0)
    return pl.pallas_call(
        body, name="mla_prep_bwd", grid=(nm,),
        in_specs=[pl.BlockSpec((MLA_HEADS, tm, 256), head), pl.BlockSpec((MLA_HEADS, tm, 256), head),
                  pl.BlockSpec((MLA_HEADS, tm, 128), head),
                  pl.BlockSpec((tm, 768), lambda i: (i, CQ // 768)),
                  pl.BlockSpec((1, Q_RANK), fixed), pl.BlockSpec((1, KV_RANK), fixed),
                  pl.BlockSpec((Q_RANK, 2048), fixed), pl.BlockSpec((KV_RANK, 2048), fixed),
                  pl.BlockSpec((tm, 128), row), pl.BlockSpec((tm, 128), row),
                  pl.BlockSpec((tm, 128), row)],
        out_specs=[pl.BlockSpec((tm, 768), row), pl.BlockSpec((Q_RANK, 2048), fixed),
                   pl.BlockSpec((KV_RANK, 2048), fixed),
                   pl.BlockSpec((1, Q_RANK), fixed), pl.BlockSpec((1, KV_RANK), fixed)],
        out_shape=[_sds((s, 768), MXU_DTYPE), _sds((Q_RANK, 2048), GRAD_DTYPE),
                   _sds((KV_RANK, 2048), GRAD_DTYPE),
                   _sds((1, Q_RANK), F32), _sds((1, KV_RANK), F32)],
        scratch_shapes=[pltpu.VMEM((tm, 2048), MXU_DTYPE), pltpu.VMEM((tm, 2048), MXU_DTYPE),
                        pltpu.VMEM((Q_RANK, 2048), F32), pltpu.VMEM((KV_RANK, 2048), F32)],
        compiler_params=_params(1),
    )(dqh, dkh, dvh, proj, gq, gkv, w_q, w_kv, c, s1, s2)


def _in_proj_bwd(dqa, dgate, dlat, dkva, w, x, dx_out, g, blocks):
    s = x.shape[0]
    tm = min(256, s)
    nm = s // tm
    na = len(blocks)

    def body(*refs):
        dqa_ref, dg8_ref, dlat_ref, dkva_ref, w_ref, x_ref, dxo_ref, g_ref = refs[:8]
        dx_ref, dxb_ref, dg_ref = refs[8 + na:11 + na]
        if na:
            ex = _Exchange(refs[8:8 + na], refs[11 + na:11 + 2 * na], refs[11 + 2 * na:], gather=False)

        @pl.when(pl.program_id(0) == 0)
        def _():
            dg_ref[...] = jnp.zeros(dg_ref.shape, F32)
            if na:
                ex.start()

        dh = (_mm_nt(dqa_ref[...], w_ref[:, QA:QA + 1024])
              + _mm_nt(dg8_ref[:, 0:1024], w_ref[:, GA:GA + 1024])
              + _mm_nt(dg8_ref[:, 1024:2048], w_ref[:, GB:GB + 1024])
              + _mm_nt(dlat_ref[...], w_ref[:, CQ:CQ + 768])
              + _mm_nt(dkva_ref[...], w_ref[:, KVA:KVA + 256]))
        xh, r = _rownorm(x_ref[...])
        dg_ref[...] += jnp.sum(dh * xh, axis=0, keepdims=True)
        dx = dxo_ref[...] + _rownorm_bwd(dh * g_ref[...], xh, r)
        dx_ref[...] = dx
        dxb_ref[...] = dx.astype(dxb_ref.dtype)

        if na:
            @pl.when(pl.program_id(0) == nm - 1)
            def _():
                ex.wait()

    row = lambda i: (i, 0)
    fixed = lambda i: (0, 0)
    outs = pl.pallas_call(
        body, name="in_proj_bwd_scatter" if na else "in_proj_bwd", grid=(nm,),
        in_specs=[pl.BlockSpec((tm, 1024), row), pl.BlockSpec((tm, 2048), row),
                  pl.BlockSpec((tm, 768), row), pl.BlockSpec((tm, 256), row),
                  pl.BlockSpec((D_MODEL, NP), fixed, pipeline_mode=pl.Buffered(1)),
                  pl.BlockSpec((tm, D_MODEL), row), pl.BlockSpec((tm, D_MODEL), row),
                  pl.BlockSpec((1, D_MODEL), fixed)] + [HBM_SPEC] * na,
        out_specs=[pl.BlockSpec((tm, D_MODEL), row), pl.BlockSpec((tm, D_MODEL), row),
                   pl.BlockSpec((1, D_MODEL), fixed)] + [HBM_SPEC] * na,
        out_shape=[_sds((s, D_MODEL), F32), _sds((s, D_MODEL), MXU_DTYPE), _sds((1, D_MODEL), F32)]
        + _exchange_shapes(blocks, False),
        scratch_shapes=_Exchange.semaphores(na) if na else [],
        compiler_params=_params(1),
    )(dqa, dgate, dlat, dkva, w, x, dx_out, g, *blocks)
    return outs[0], outs[1], outs[2], list(outs[3:])


def _reduce_adamw(parts, w, m, v, name):
    n_layers = len(parts)
    rows, part_cols = parts[0].shape[1:]
    cols = w.shape[-1]
    lanes = -(-cols // 128) * 128
    tr = rows
    for cand in (1024, 512, 256, 128, 64, 32, 16, 8):
        if rows % cand == 0 and N_DEV * cand * lanes * 4 <= 8 * 1024 * 1024:
            tr = cand
            break
    nr = rows // tr

    def body(*refs):
        p_refs = refs[:n_layers]
        w_ref, m_ref, v_ref, g_ref, d_ref, nm_ref, nv_ref = refs[n_layers:]
        layer = pl.program_id(0)
        for l in range(n_layers):
            @pl.when(layer == l)
            def _(l=l):
                g = p_refs[l][0, :, 0:cols].astype(F32)
                for k in range(1, N_DEV):
                    g = g + p_refs[l][k, :, 0:cols].astype(F32)
                update(g, w_ref, m_ref, v_ref, g_ref, d_ref, nm_ref, nv_ref)

    def update(g, w_ref, m_ref, v_ref, g_ref, d_ref, nm_ref, nv_ref):
        m2 = ADAM_B1 * m_ref[...] + (1.0 - ADAM_B1) * g
        v2 = ADAM_B2 * v_ref[...] + (1.0 - ADAM_B2) * (g * g)
        m_hat = m2 / (1.0 - ADAM_B1 ** ADAM_STEP)
        v_hat = v2 / (1.0 - ADAM_B2 ** ADAM_STEP)
        g_ref[...] = g
        d_ref[...] = -ADAM_LR * (m_hat / (jnp.sqrt(v_hat) + ADAM_EPS) + ADAM_WD * w_ref[...])
        nm_ref[...] = m2
        nv_ref[...] = v2

    def part_spec(l):
        return pl.BlockSpec((N_DEV, tr, part_cols),
                            lambda layer, i: (0, jnp.where(layer == l, i, 0), 0))

    blk = pl.BlockSpec((tr, cols), lambda layer, i: (layer * nr + i, 0))
    return pl.pallas_call(
        body, name=name, grid=(n_layers, nr),
        in_specs=[part_spec(l) for l in range(n_layers)] + [blk, blk, blk],
        out_specs=[blk, blk, blk, blk],
        out_shape=[_sds((n_layers * rows, cols), F32)] * 4,
        compiler_params=_params(2),
    )(*parts, w, m, v)


def _position():
    x, y, c = lax.axis_index("x"), lax.axis_index("y"), lax.axis_index("c")
    return x, y, c


def _index(px, py, pc):
    return 4 * px + 2 * py + pc


HBM_SPEC = pl.BlockSpec(memory_space=pltpu.HBM)
IN_BLOCKED = (D_MODEL, N_DEV * SHARD_PAD)


def _block(ref, idx):
    if tuple(ref.shape) == IN_BLOCKED:
        return ref.at[:, pl.ds(pl.multiple_of(idx * SHARD_PAD, SHARD_PAD), SHARD_PAD)]
    return ref.at[idx]


class _Gather:
    def __init__(self, srcs, dsts, sems):
        self.srcs, self.dsts = srcs, dsts
        self.send_sems, self.recv_sems, self.local_sems = sems
        x, y, c = _position()
        self.c = c
        self.me, self.sibling = (x, y, c), (x, y, 1 - c)
        self.chips = [(1 - x, y), (x, 1 - y), (1 - x, 1 - y)]

    def _copy(self, a, k, block, to, own=False):
        slot = _block(self.dsts[a], _index(*block))
        return pltpu.make_async_remote_copy(
            src_ref=self.srcs[a] if own else slot, dst_ref=slot,
            send_sem=self.send_sems.at[7 * a + k], recv_sem=self.recv_sems.at[7 * a + k],
            device_id=to, device_id_type=MESH)

    def _local(self, a):
        return pltpu.make_async_copy(self.srcs[a], _block(self.dsts[a], _index(*self.me)),
                                     self.local_sems.at[a])

    def _first(self, a):
        return [self._copy(a, 0, self.me, self.sibling, own=True)] + [
            self._copy(a, 1 + j, self.me, (*chip, self.c), own=True)
            for j, chip in enumerate(self.chips)]

    def _passed(self, a, j):
        return self._copy(a, 4 + j, (*self.chips[j], self.c), self.sibling)

    def start(self):
        for a in range(len(self.srcs)):
            self._local(a).start()
            for cp in self._first(a):
                cp.start()

    def forward(self):
        for j, chip in enumerate(self.chips):
            for a in range(len(self.srcs)):
                self._copy(a, 1 + j, (*chip, self.c), self.me).wait_recv()
                self._passed(a, j).start()

    def finish(self):
        for a in range(len(self.srcs)):
            self._copy(a, 0, self.sibling, self.me).wait_recv()
            for j, chip in enumerate(self.chips):
                self._copy(a, 4 + j, (*chip, 1 - self.c), self.me).wait_recv()
            for cp in self._first(a):
                cp.wait_send()
            for j in range(3):
                self._passed(a, j).wait_send()
            self._local(a).wait()


def _all_gather(shards, name):
    na = len(shards)

    def body(*refs):
        g = _Gather(refs[:na], refs[na:2 * na], refs[2 * na:])
        g.start()
        g.forward()
        g.finish()

    return pl.pallas_call(
        body, name=name,
        in_specs=[HBM_SPEC] * na, out_specs=[HBM_SPEC] * na,
        out_shape=_exchange_shapes(shards, True),
        scratch_shapes=_Exchange.semaphores(na),
    )(*shards)


class _Exchange:
    def __init__(self, srcs, dsts, sems, gather):
        self.srcs, self.dsts, self.gather = srcs, dsts, gather
        self.send_sems, self.recv_sems, self.local_sems = sems
        x, y, c = _position()
        self.me = _index(x, y, c)
        self.peers = [(x ^ ((k >> 2) & 1), y ^ ((k >> 1) & 1), c ^ (k & 1)) for k in range(1, N_DEV)]

    @staticmethod
    def semaphores(na):
        return [pltpu.SemaphoreType.DMA((7 * na,)), pltpu.SemaphoreType.DMA((7 * na,)),
                pltpu.SemaphoreType.DMA((na,))]

    def _src(self, a, slot):
        return self.srcs[a] if self.gather else _block(self.srcs[a], slot)

    def _local(self, a):
        return pltpu.make_async_copy(self._src(a, self.me), self.dsts[a].at[self.me],
                                     self.local_sems.at[a])

    def _send(self, a, k):
        peer = self.peers[k]
        return pltpu.make_async_remote_copy(
            src_ref=self._src(a, _index(*peer)), dst_ref=self.dsts[a].at[self.me],
            send_sem=self.send_sems.at[7 * a + k], recv_sem=self.recv_sems.at[7 * a + k],
            device_id=peer, device_id_type=MESH)

    def _arrival(self, a, k):
        landed = self.dsts[a].at[_index(*self.peers[k])]
        return pltpu.make_async_remote_copy(
            src_ref=landed, dst_ref=landed,
            send_sem=self.send_sems.at[7 * a + k], recv_sem=self.recv_sems.at[7 * a + k],
            device_id=self.peers[k], device_id_type=MESH)

    def start(self):
        for a in range(len(self.srcs)):
            self._local(a).start()
            for k in range(N_DEV - 1):
                self._send(a, k).start()

    def wait(self):
        for a in range(len(self.srcs)):
            for k in range(N_DEV - 1):
                self._arrival(a, k).wait_recv()
            for k in range(N_DEV - 1):
                self._send(a, k).wait_send()
            self._local(a).wait()


def _exchange_shapes(arrays, gather):
    def shape(a):
        if gather:
            return IN_BLOCKED if a.shape == (D_MODEL, SHARD_PAD) else (N_DEV,) + a.shape
        return (N_DEV, D_MODEL, SHARD_PAD) if a.shape == IN_BLOCKED else a.shape
    return [_sds(shape(a), a.dtype) for a in arrays]


def _exchange_call(arrays, gather, name):
    na = len(arrays)

    def body(*refs):
        ex = _Exchange(refs[:na], refs[na:2 * na], refs[2 * na:], gather)
        ex.start()
        ex.wait()

    return pl.pallas_call(
        body, name=name,
        in_specs=[HBM_SPEC] * na, out_specs=[HBM_SPEC] * na,
        out_shape=_exchange_shapes(arrays, gather),
        scratch_shapes=_Exchange.semaphores(na),
    )(*arrays)


def _layer_fwd(x, small, w_in, g_rest, rest_shards, tabs, next_shards):
    attn_g, sinks, gq, gkv = small
    proj, h, gathered_rest = _in_proj(x, attn_g, w_in, rest_shards)
    w_q, w_kv, w_o = _rest_from_gathered(*(gathered_rest if rest_shards else g_rest))
    swa = _swa_fwd(proj, sinks)
    qh, kh, vh, vth = _mla_prep(proj, gq, gkv, w_q, w_kv, tabs)
    mla, lse, gathered = _mla_fwd(qh, kh, vth, next_shards)
    x_next, y = _out_proj(x, proj, swa, mla, w_o)
    return x_next, (x, proj, h, swa, qh, kh, vh, mla, lse, y), (w_in, w_q, w_kv, w_o), gathered


def _layer_bwd(dx, dxb, saved, small, weights, tabs, pending, send_own):
    attn_g, sinks, gq, gkv = small
    w_in, w_q, w_kv, w_o = weights
    x, proj, h, swa, qh, kh, vh, mla, lse, y = saved
    d_o = _matmul_tn(y, dxb, "grad_w_out")
    do_a, do_b, dgate, dlt = _out_proj_bwd(dxb, proj, swa, mla, w_o)
    dqa, dkva, dsk = _swa_bwd(proj, sinks, do_a, swa)
    dqh, dkh, dvh, received = _mla_bwd(qh, kh, vh, do_b, dlt, lse, pending)
    dlat, d_q, d_kv, d_gq, d_gkv = _mla_prep_bwd(dqh, dkh, dvh, proj, gq, gkv, w_q, w_kv, tabs)
    rest_blocks = _rest_grad_blocks(d_q, d_kv, d_o)
    d_inp, got_rest = _grad_w_in(h, dqa, dgate, dlat, dkva, rest_blocks if send_own else [])
    in_block = _in_grad_blocks(d_inp)
    dx, dxb, d_attn, got_in = _in_proj_bwd(dqa, dgate, dlat, dkva, w_in, x, dx, attn_g,
                                           [in_block] if send_own else [])
    small_grads = (d_attn, dsk[0:1, 0:SWA_HEADS], d_gq, d_gkv)
    return dx, dxb, small_grads, [in_block] + rest_blocks, received, got_in + got_rest


def _pack_small_grads(small_grads, d_final, loss):
    d_attn, d_sink, d_gq, d_gkv = zip(*small_grads)
    return jnp.concatenate([
        jnp.concatenate(d_attn, axis=0).reshape(64, 128),
        jnp.concatenate(d_gq, axis=0).reshape(12, 128),
        jnp.concatenate(d_gkv, axis=0).reshape(8, 128),
        d_final.reshape(16, 128),
        jnp.pad(jnp.concatenate(d_sink, axis=1), ((0, 0), (0, 64))),
        loss[0:1],
        jnp.zeros((PACK_ROWS - ROW_LOSS - 1, 128), F32)], axis=0)


def _pack_small(attn, qa, kva, final, sinks):
    return jnp.concatenate([
        attn.reshape(64, 128), qa.reshape(12, 128), kva.reshape(8, 128), final.reshape(16, 128),
        jnp.pad(sinks.reshape(1, 64), ((0, 0), (0, 64))),
        jnp.zeros((PACK_ROWS - ROW_SINK - 1, 128), F32)], axis=0)


def _unpack_small(p):
    return (p[ROW_ATTN:ROW_QA].reshape(DEPTH, D_MODEL), p[ROW_SINK, 0:64].reshape(DEPTH, SWA_HEADS),
            p[ROW_QA:ROW_KVA].reshape(DEPTH, Q_RANK), p[ROW_KVA:ROW_FINAL].reshape(DEPTH, KV_RANK),
            p[ROW_FINAL:ROW_SINK].reshape(D_MODEL))


def kernel(x, attn_norm_g, w_in, swa_sinks, q_a_norm_g, kv_a_norm_g, w_q_b, w_kv_b, w_out, final_norm_g, loss_target, m_attn_norm_g, m_w_in, m_swa_sinks, m_q_a_norm_g, m_kv_a_norm_g, m_w_q_b, m_w_kv_b, m_w_out, m_final_norm_g, v_attn_norm_g, v_w_in, v_swa_sinks, v_q_a_norm_g, v_kv_a_norm_g, v_w_q_b, v_w_kv_b, v_w_out, v_final_norm_g):
    xs, tgt = x[0], loss_target[0]
    tabs = _rope_tables(xs.shape[0])
    shards = [w.astype(MXU_DTYPE) for w in (w_in, w_q_b, w_kv_b, w_out)]
    shards[0] = jnp.pad(shards[0], ((0, 0), (0, 0), (0, SHARD_PAD - SHARD_COLS)))
    layer_shards = lambda l: [w[l] for w in shards]
    smalls = [(attn_norm_g[l:l + 1], swa_sinks[l], q_a_norm_g[l:l + 1], kv_a_norm_g[l:l + 1])
              for l in range(DEPTH)]

    gathered = list(_all_gather(layer_shards(0)[:1], "gather_weights")) + [None] * 3
    weights, saved = [None] * DEPTH, []
    for l in range(DEPTH):
        next_shards = layer_shards(l + 1) if l + 1 < DEPTH else []
        rest_shards = layer_shards(0)[1:] if l == 0 else []
        xs, acts, weights[l], gathered = _layer_fwd(
            xs, smalls[l], _w_in_from_gathered(gathered[0]), gathered[1:], rest_shards, tabs,
            next_shards)
        saved.append(acts)
    dx, dxb, d_final, loss = _final_loss(xs, tgt, final_norm_g.reshape(1, D_MODEL))

    received, small_grads, pending = [None] * DEPTH, [None] * DEPTH, []
    for l in reversed(range(DEPTH)):
        dx, dxb, small_grads[l], blocks, arrived, arrived_own = _layer_bwd(
            dx, dxb, saved[l], smalls[l], weights[l], tabs, pending, send_own=(l == 0))
        if pending:
            received[l + 1] = arrived
        pending = blocks
    received[0] = arrived_own
    small = _exchange_call([_pack_small_grads(small_grads, d_final, loss)], True, "gather_small")[0]

    big = []
    for a, (w, m, v, name) in enumerate(zip((w_in, w_q_b, w_kv_b, w_out),
                                            (m_w_in, m_w_q_b, m_w_kv_b, m_w_out),
                                            (v_w_in, v_w_q_b, v_w_kv_b, v_w_out),
                                            ("adamw_w_in", "adamw_w_q_b", "adamw_w_kv_b",
                                             "adamw_w_out"))):
        cols = w.shape[-1]
        flat = lambda t: t.reshape(-1, cols)
        outs = _reduce_adamw([received[l][a] for l in range(DEPTH)], flat(w), flat(m), flat(v), name)
        big.append([t.reshape(w.shape) for t in outs])

    sm = _reduce_adamw(
        [small],
        _pack_small(attn_norm_g, q_a_norm_g, kv_a_norm_g, final_norm_g, swa_sinks),
        _pack_small(m_attn_norm_g, m_q_a_norm_g, m_kv_a_norm_g, m_final_norm_g, m_swa_sinks),
        _pack_small(v_attn_norm_g, v_q_a_norm_g, v_kv_a_norm_g, v_final_norm_g, v_swa_sinks),
        "adamw_small")
    loss = sm[0][ROW_LOSS, 0]
    kinds = []
    for t in range(4):
        attn, sinks, qa, kva, final = _unpack_small(sm[t])
        b_in, b_q, b_kv, b_o = (big[i][t] for i in range(4))
        kinds.append((attn, b_in, sinks, qa, kva, b_q, b_kv, b_o, final))
    return (loss, dx[None], *kinds[0], *kinds[1], *kinds[2], *kinds[3])
```

```python
import functools

import jax
import jax.numpy as jnp
import numpy as np
from jax import lax
from jax.experimental import pallas as pl
from jax.experimental.pallas import tpu as pltpu

F32 = jnp.float32
BF16 = jnp.bfloat16
MXU_DTYPE = BF16
GRAD_DTYPE = BF16
PROJ_DTYPE = BF16
ATTN_DTYPE = BF16

D_MODEL = 2048
DEPTH = 4
EPS = 1e-6
NEG = -1e30
BLOCK = 128
SWA_HEADS = 16
MLA_HEADS = 8
Q_RANK = 384
KV_RANK = 256
MLA_SCALE = 192 ** -0.5
MLA_C2 = MLA_SCALE * 1.4426950408889634
ROPE_THETA = 10000.0
IN_WIDTH = 4032

ADAM_LR = 0.001
ADAM_B1 = 0.9
ADAM_B2 = 0.999
ADAM_EPS = 1e-08
ADAM_WD = 0.01
ADAM_STEP = 10

N_DEV = 8
MESH = pl.DeviceIdType.MESH

NP = 4096
QA, GA, GB, CQ, CKV, KR, KVA = 0, 1024, 2048, 3072, 3456, 3712, 3840

ROW_ATTN, ROW_QA, ROW_KVA, ROW_FINAL, ROW_SINK, ROW_LOSS, PACK_ROWS = 0, 64, 76, 84, 100, 101, 104

VMEM_LIMIT = 56 * 1024 * 1024
MLA_TILE = 512
GATHER_FORWARD_HEAD = 5


def _sds(shape, dtype):
    return jax.ShapeDtypeStruct(shape, dtype)


def _params(n_axes):
    return pltpu.CompilerParams(dimension_semantics=("arbitrary",) * n_axes,
                                vmem_limit_bytes=VMEM_LIMIT)


def _mm(a, b):
    return jnp.dot(a.astype(MXU_DTYPE), b.astype(MXU_DTYPE), preferred_element_type=F32)


def _mm_nt(a, b):
    return lax.dot_general(a.astype(MXU_DTYPE), b.astype(MXU_DTYPE),
                           (((1,), (1,)), ((), ())), preferred_element_type=F32)


def _mm_tn(a, b):
    return lax.dot_general(a.astype(MXU_DTYPE), b.astype(MXU_DTYPE),
                           (((0,), (0,)), ((), ())), preferred_element_type=F32)


def _rownorm(x):
    r = lax.rsqrt(jnp.mean(x * x, axis=-1, keepdims=True) + EPS)
    return x * r, r


def _rownorm_bwd(dxh, xh, r):
    return r * (dxh - xh * jnp.mean(dxh * xh, axis=-1, keepdims=True))


def _rope(t, c, s1, s2):
    return t * c + pltpu.roll(t, 32, 1) * s1 + pltpu.roll(t, 96, 1) * s2


SHARD_COLS = IN_WIDTH // N_DEV
SHARD_PAD = 512


def _orig_col_of_padded():
    o = np.full((NP,), -1, np.int64)
    for start, width, orig in ((QA, 1024, 0), (KVA, 256, 1024), (GA, 1024, 1280), (CQ, 384, 2304),
                               (CKV, 256, 2688), (KR, 64, 2944), (GB, 1024, 3008)):
        o[start:start + width] = np.arange(orig, orig + width)
    return o


def _device_major_src():
    o = _orig_col_of_padded()
    return np.where(o >= 0, o + (SHARD_PAD - SHARD_COLS) * (o // SHARD_COLS), -1)


def _kernel_layout_src():
    o = _orig_col_of_padded()
    where = np.full((IN_WIDTH,), -1, np.int64)
    where[o[o >= 0]] = np.nonzero(o >= 0)[0]
    e = np.arange(N_DEV * SHARD_PAD)
    k, c = e // SHARD_PAD, e % SHARD_PAD
    return np.where(c < SHARD_COLS, where[np.minimum(SHARD_COLS * k + c, IN_WIDTH - 1)], -1)


def _permute_columns(x, src_of, name, transposed=False):
    rows, n_in = x.shape
    n_out = len(src_of)
    plan, mats = [], []
    for t in range(n_out // 128):
        srcs = src_of[128 * t:128 * (t + 1)]
        entry = []
        for u in sorted(set(int(s) // 128 for s in srcs if s >= 0)):
            m = np.zeros((128, 128), np.float32)
            for c, s in enumerate(srcs):
                if s >= 0 and s // 128 == u:
                    m[s % 128, c] = 1.0
            entry.append((u, len(mats)))
            mats.append(m)
        plan.append(entry)
    tr = min(512, rows)

    def body(x_ref, p_ref, o_ref):
        for t, entry in enumerate(plan):
            acc = jnp.zeros((tr, 128), F32)
            for u, idx in entry:
                acc = acc + jnp.dot(x_ref[:, 128 * u:128 * (u + 1)], p_ref[idx],
                                    preferred_element_type=F32)
            if transposed:
                o_ref[128 * t:128 * (t + 1), :] = acc.T.astype(o_ref.dtype)
            else:
                o_ref[:, 128 * t:128 * (t + 1)] = acc.astype(o_ref.dtype)

    table = jnp.asarray(np.stack(mats), x.dtype)
    return pl.pallas_call(
        body, name=name, grid=(rows // tr,),
        in_specs=[pl.BlockSpec((tr, n_in), lambda i: (i, 0)),
                  pl.BlockSpec(table.shape, lambda i: (0, 0, 0))],
        out_specs=(pl.BlockSpec((n_out, tr), lambda i: (0, i)) if transposed
                   else pl.BlockSpec((tr, n_out), lambda i: (i, 0))),
        out_shape=_sds((n_out, rows) if transposed else (rows, n_out), x.dtype),
        compiler_params=_params(1),
    )(x, table)


def _w_in_from_gathered(g_in):
    return _permute_columns(g_in, _device_major_src(), "w_in_layout")


def _rest_from_gathered(g_qb, g_kvb, g_out):
    qb = g_qb.transpose(1, 0, 2)
    rope = jnp.pad(qb[..., 128:], ((0, 0), (0, 0), (0, 64)))
    w_q = jnp.concatenate([qb[..., :128].reshape(Q_RANK, 1024),
                           rope.reshape(Q_RANK, 1024)], axis=-1)
    kvb = g_kvb.transpose(1, 0, 2)
    w_kv = jnp.concatenate([kvb[..., :128].reshape(KV_RANK, 1024),
                            kvb[..., 128:].reshape(KV_RANK, 1024)], axis=-1)
    w_o = g_out.reshape(D_MODEL, D_MODEL)
    return w_q, w_kv, w_o


def _in_grad_blocks(d_inp):
    return _permute_columns(d_inp, _kernel_layout_src(), "grad_w_in_layout", transposed=True)


def _rest_grad_blocks(d_q, d_kv, d_o):
    qn = d_q[:, :1024].reshape(Q_RANK, 8, 128)
    qr = d_q[:, 1024:].reshape(Q_RANK, 8, 128)[..., :64]
    b_q = jnp.concatenate([qn, qr], axis=-1).transpose(1, 0, 2)
    kn = d_kv[:, :1024].reshape(KV_RANK, 8, 128)
    vv = d_kv[:, 1024:].reshape(KV_RANK, 8, 128)
    b_kv = jnp.concatenate([kn, vv], axis=-1).transpose(1, 0, 2)
    b_o = d_o.reshape(N_DEV, 256, D_MODEL)
    return [b_q, b_kv, b_o]


def _rope_tables(s):
    pos = jnp.arange(s, dtype=F32)
    inv_freq = ROPE_THETA ** (-jnp.arange(0, 64, 2, dtype=F32) / 64)
    ang = pos[:, None] * inv_freq[None, :]
    cos, sin = jnp.cos(ang), jnp.sin(ang)
    z32 = jnp.zeros((s, 32), F32)
    z64 = jnp.zeros((s, 64), F32)
    c = jnp.concatenate([cos, cos, z64], axis=1)
    s1 = jnp.concatenate([z32, sin, z64], axis=1)
    s2 = jnp.concatenate([-sin, z32, z64], axis=1)
    return c, s1, s2


def _in_proj(x, g, w, shards):
    s = x.shape[0]
    tm, tn = min(512, s), 1024
    nm = s // tm
    na = len(shards)

    def body(*refs):
        x_ref, g_ref, w_ref = refs[:3]
        o_ref, h_ref = refs[3 + na:5 + na]
        i = pl.program_id(0)
        if na:
            ex = _Gather(refs[3:3 + na], refs[5 + na:5 + 2 * na], refs[5 + 2 * na:])

            @pl.when(i == 0)
            def _():
                ex.start()

            @pl.when(i == nm // 2)
            def _():
                ex.forward()

        xh, _ = _rownorm(x_ref[...])
        h_ref[...] = (xh * g_ref[...]).astype(h_ref.dtype)
        for j in range(NP // tn):
            cols = slice(j * tn, (j + 1) * tn)
            o_ref[:, cols] = jnp.dot(h_ref[...], w_ref[:, cols],
                                     preferred_element_type=F32).astype(o_ref.dtype)

        if na:
            @pl.when(i == nm - 1)
            def _():
                ex.finish()

    row = lambda i: (i, 0)
    fixed = lambda i: (0, 0)
    outs = pl.pallas_call(
        body, name="in_proj_gather" if na else "in_proj", grid=(nm,),
        in_specs=[pl.BlockSpec((tm, D_MODEL), row), pl.BlockSpec((1, D_MODEL), fixed),
                  pl.BlockSpec((D_MODEL, NP), fixed, pipeline_mode=pl.Buffered(1))]
        + [HBM_SPEC] * na,
        out_specs=[pl.BlockSpec((tm, NP), row), pl.BlockSpec((tm, D_MODEL), row)]
        + [HBM_SPEC] * na,
        out_shape=[_sds((s, NP), PROJ_DTYPE), _sds((s, D_MODEL), MXU_DTYPE)]
        + _exchange_shapes(shards, True),
        scratch_shapes=_Exchange.semaphores(na) if na else [],
        compiler_params=_params(1),
    )(x, g, w, *shards)
    return outs[0], outs[1], list(outs[2:])


def _swa_slopes():
    return [2.0 ** (-8.0 * (h + 1) / SWA_HEADS) for h in range(SWA_HEADS)]


SWA_STACK = 8 * BLOCK


def _swa_head(j, a):
    return 2 * (4 * j + a % 4) + a // 4


def _swa_operands(kv_p, kv_c):
    kk = jnp.concatenate([kv_p[:, :128], kv_c[:, :128]], axis=0)
    vv = jnp.concatenate([kv_p[:, 128:], kv_c[:, 128:]], axis=0)
    left = lax.broadcasted_iota(jnp.int32, (2 * BLOCK, 128), 1) < 64

    def heads(t):
        return [jnp.where(left, t, 0.0), jnp.where(left, pltpu.roll(t, 64, 1), 0.0)]

    return heads(kk), heads(vv), left


def _swa_by_block(vals):
    a = lax.broadcasted_iota(jnp.int32, (1, SWA_STACK), 1) >> 7
    row = jnp.full((1, SWA_STACK), vals[7], F32)
    for t in range(6, -1, -1):
        row = jnp.where(a == t, vals[t], row)
    return row


def _swa_bias(n, j):
    slopes = _swa_slopes()
    ki = lax.broadcasted_iota(jnp.int32, (2 * BLOCK, SWA_STACK), 0)
    r = lax.broadcasted_iota(jnp.int32, (2 * BLOCK, SWA_STACK), 1)
    delta = BLOCK + (r & (BLOCK - 1)) - ki
    valid = (delta >= 0) & (delta < BLOCK) & ((n - 1) * BLOCK + ki >= 0)
    slope = _swa_by_block([slopes[_swa_head(j, a)] for a in range(8)])
    return jnp.where(valid, -slope * delta.astype(F32), NEG)


def _swa_fill_bias(n, bias_ref):
    @pl.when(n <= 1)
    def _():
        for j in range(2):
            bias_ref[j] = _swa_bias(n, j)


def _swa_sink_row(sink_ref, j):
    return _swa_by_block([sink_ref[_swa_head(j, a)] for a in range(8)])


def _swa_pairs(ref, j):
    return jnp.concatenate([ref[:, 128 * (4 * j + a):128 * (4 * j + a + 1)] for a in range(4)],
                           axis=0).astype(F32)


def _swa_stack(pairs):
    return jnp.concatenate([pairs, pltpu.roll(pairs, 64, 1)], axis=0)


def _swa_unstack(t):
    left = lax.broadcasted_iota(jnp.int32, (4 * BLOCK, 128), 1) < 64
    return jnp.where(left, t[0:4 * BLOCK], pltpu.roll(t[4 * BLOCK:8 * BLOCK], 64, 1))


def _swa_softmax(qs, kmat, bias, sink):
    sc = _mm_nt(kmat, qs) + bias
    m = jnp.maximum(jnp.max(sc, axis=0, keepdims=True), sink)
    ex = jnp.exp(sc - m)
    es = jnp.exp(sink - m)
    return ex, es, 1.0 / (jnp.sum(ex, axis=0, keepdims=True) + es)


def _swa_fwd(proj, sinks):
    s = proj.shape[0]
    nb = s // BLOCK

    def body(sink_ref, q_ref, kp_ref, kc_ref, o_ref, bias_ref):
        n = pl.program_id(0)
        _swa_fill_bias(n, bias_ref)
        k, v, _ = _swa_operands(kp_ref[...].astype(F32), kc_ref[...].astype(F32))
        for j in range(2):
            qs = _swa_stack(_swa_pairs(q_ref, j) * 0.125)
            ex, _, inv = _swa_softmax(qs, k[j], bias_ref[j], _swa_sink_row(sink_ref, j))
            o_t = _mm(v[j].T, ex) * inv
            out = jnp.concatenate([o_t[0:64, 0:4 * BLOCK], o_t[0:64, 4 * BLOCK:8 * BLOCK]], axis=0).T
            for a in range(4):
                o_ref[:, 128 * (4 * j + a):128 * (4 * j + a + 1)] = out[128 * a:128 * (a + 1)].astype(
                    o_ref.dtype)

    return pl.pallas_call(
        body, name="swa_fwd", grid=(nb,),
        in_specs=[pl.BlockSpec(memory_space=pltpu.SMEM),
                  pl.BlockSpec((BLOCK, 1024), lambda n: (n, 0)),
                  pl.BlockSpec((BLOCK, 256), lambda n: (jnp.maximum(n - 1, 0), KVA // 256)),
                  pl.BlockSpec((BLOCK, 256), lambda n: (n, KVA // 256))],
        out_specs=pl.BlockSpec((BLOCK, 1024), lambda n: (n, 0)),
        out_shape=_sds((s, 1024), ATTN_DTYPE),
        scratch_shapes=[pltpu.VMEM((2, 2 * BLOCK, SWA_STACK), F32)],
        compiler_params=_params(1),
    )(sinks, proj, proj, proj)


def _mla_prep(proj, gq, gkv, w_q, w_kv, tabs):
    s = proj.shape[0]
    tm = min(512, s)
    c, s1, s2 = tabs

    def body(p_ref, gq_ref, gkv_ref, wq_ref, wkv_ref, c_ref, s1_ref, s2_ref,
             q_ref, k_ref, v_ref, vt_ref):
        cqh, _ = _rownorm(p_ref[:, 0:384].astype(F32))
        ckvh, _ = _rownorm(p_ref[:, 384:640].astype(F32))
        q = _mm(cqh * gq_ref[...], wq_ref[...])
        kv = _mm(ckvh * gkv_ref[...], wkv_ref[...])
        cc, ss1, ss2 = c_ref[...], s1_ref[...], s2_ref[...]
        krr = _rope(p_ref[:, 640:768].astype(F32), cc, ss1, ss2).astype(k_ref.dtype)
        for h in range(MLA_HEADS):
            q_ref[h, :, 0:128] = q[:, 128 * h:128 * (h + 1)].astype(q_ref.dtype)
            q_ref[h, :, 128:256] = _rope(q[:, 1024 + 128 * h:1024 + 128 * (h + 1)],
                                         cc, ss1, ss2).astype(q_ref.dtype)
            k_ref[h, :, 0:128] = kv[:, 128 * h:128 * (h + 1)].astype(k_ref.dtype)
            k_ref[h, :, 128:256] = krr
            vv = kv[:, 1024 + 128 * h:1024 + 128 * (h + 1)]
            v_ref[h] = vv.astype(v_ref.dtype)
            vt_ref[h, 0:128, :] = vv.T.astype(vt_ref.dtype)
            vt_ref[h, 128:256, :] = jnp.ones((128, tm), vt_ref.dtype)

    row = lambda i: (i, 0)
    fixed = lambda i: (0, 0)
    return pl.pallas_call(
        body, name="mla_prep", grid=(s // tm,),
        in_specs=[pl.BlockSpec((tm, 768), lambda i: (i, CQ // 768)),
                  pl.BlockSpec((1, Q_RANK), fixed), pl.BlockSpec((1, KV_RANK), fixed),
                  pl.BlockSpec((Q_RANK, 2048), fixed), pl.BlockSpec((KV_RANK, 2048), fixed),
                  pl.BlockSpec((tm, 128), row), pl.BlockSpec((tm, 128), row),
                  pl.BlockSpec((tm, 128), row)],
        out_specs=[pl.BlockSpec((MLA_HEADS, tm, 256), lambda i: (0, i, 0)),
                   pl.BlockSpec((MLA_HEADS, tm, 256), lambda i: (0, i, 0)),
                   pl.BlockSpec((MLA_HEADS, tm, 128), lambda i: (0, i, 0)),
                   pl.BlockSpec((MLA_HEADS, 256, tm), lambda i: (0, 0, i))],
        out_shape=[_sds((MLA_HEADS, s, 256), MXU_DTYPE), _sds((MLA_HEADS, s, 256), MXU_DTYPE),
                   _sds((MLA_HEADS, s, 128), MXU_DTYPE), _sds((MLA_HEADS, 256, s), MXU_DTYPE)],
        compiler_params=_params(1),
    )(proj, gq, gkv, w_q, w_kv, c, s1, s2)


def _scores_t(k, q, t, diagonal):
    sc = _mm_nt(k, q)
    if diagonal:
        key = lax.broadcasted_iota(jnp.int32, (t, t), 0)
        query = lax.broadcasted_iota(jnp.int32, (t, t), 1)
        sc = jnp.where(key <= query, sc, NEG)
    return sc


def _mla_fwd(qh, kh, vth, shards):
    s = qh.shape[1]
    t = min(MLA_TILE, s)
    nq = s // t
    na = len(shards)

    def body(*refs):
        q_ref, k_ref, vt_ref = refs[:3]
        o_ref, lse_ref = refs[3 + na:5 + na]
        m_ref, acc_ref = refs[5 + 2 * na:7 + 2 * na]
        h, i = pl.program_id(0), pl.program_id(1)
        if na:
            ex = _Gather(refs[3:3 + na], refs[5 + na:5 + 2 * na], refs[7 + 2 * na:])

            @pl.when((h == 0) & (i == 0))
            def _():
                ex.start()

            @pl.when((h == GATHER_FORWARD_HEAD) & (i == 0))
            def _():
                ex.forward()

        m_ref[...] = jnp.full(m_ref.shape, NEG, F32)
        acc_ref[...] = jnp.zeros(acc_ref.shape, F32)

        def step(start, width, diagonal):
            keys = pl.ds(pl.multiple_of(start, t), width)
            if diagonal:
                sc = _scores_t(k_ref[keys, :], q_ref[...], t, True)
            else:
                sc = _mm_nt(k_ref[keys, :], q_ref[...])
            m_prev = m_ref[...]
            m_new = jnp.maximum(m_prev, jnp.max(sc, axis=0, keepdims=True))
            alpha = jnp.exp2((m_prev - m_new) * MLA_C2)
            p = jnp.exp2((sc - m_new[0:1, :]) * MLA_C2)
            acc_ref[...] = alpha[0:1, :] * acc_ref[...] + _mm(vt_ref[:, keys], p)
            m_ref[...] = m_new

        def below_diagonal(jj, carry):
            step(jj * (4 * t), 4 * t, False)
            return carry

        lax.fori_loop(0, i // 4, below_diagonal, 0)

        @pl.when(i % 4 >= 2)
        def _():
            step((i // 4) * (4 * t), 2 * t, False)

        @pl.when(i % 2 == 1)
        def _():
            step((i - 1) * t, t, False)

        step(i * t, t, True)
        l = acc_ref[128:136, :]
        o_ref[...] = (acc_ref[0:128, :] / l[0:1, :]).T.astype(o_ref.dtype)
        lse_ref[...] = m_ref[...] * MLA_C2 + jnp.log2(l)

        if na:
            @pl.when((h == MLA_HEADS - 1) & (i == nq - 1))
            def _():
                ex.finish()

    head = lambda h, i: (h, 0, 0)
    outs = pl.pallas_call(
        body, name="mla_fwd_gather" if na else "mla_fwd", grid=(MLA_HEADS, nq),
        in_specs=[pl.BlockSpec((None, t, 256), lambda h, i: (h, i, 0)),
                  pl.BlockSpec((None, s, 256), head),
                  pl.BlockSpec((None, 256, s), head)] + [HBM_SPEC] * na,
        out_specs=[pl.BlockSpec((t, 128), lambda h, i: (i, h)),
                   pl.BlockSpec((None, 8, t), lambda h, i: (h, 0, i))] + [HBM_SPEC] * na,
        out_shape=[_sds((s, 1024), ATTN_DTYPE), _sds((MLA_HEADS, 8, s), F32)]
        + _exchange_shapes(shards, True),
        scratch_shapes=[pltpu.VMEM((8, t), F32), pltpu.VMEM((256, t), F32)]
        + (_Exchange.semaphores(na) if na else []),
        compiler_params=_params(2),
    )(qh, kh, vth, *shards)
    return outs[0], outs[1], list(outs[2:])


def _silu_parts(g):
    sg = jax.nn.sigmoid(g)
    return g * sg, sg * (1.0 + g * (1.0 - sg))


def _out_proj(x, proj, swa, mla, w_out):
    s = x.shape[0]
    tm = min(512, s)

    def body(x_ref, ga_ref, gb_ref, a_ref, b_ref, w_ref, xo_ref, y_ref):
        sa, _ = _silu_parts(ga_ref[...].astype(F32))
        sb, _ = _silu_parts(gb_ref[...].astype(F32))
        y_ref[:, 0:1024] = (a_ref[...].astype(F32) * sa).astype(y_ref.dtype)
        y_ref[:, 1024:2048] = (b_ref[...].astype(F32) * sb).astype(y_ref.dtype)
        xo_ref[...] = x_ref[...] + jnp.dot(y_ref[...], w_ref[...], preferred_element_type=F32)

    row = lambda i: (i, 0)
    return pl.pallas_call(
        body, name="out_proj", grid=(s // tm,),
        in_specs=[pl.BlockSpec((tm, D_MODEL), row),
                  pl.BlockSpec((tm, 1024), lambda i: (i, GA // 1024)),
                  pl.BlockSpec((tm, 1024), lambda i: (i, GB // 1024)),
                  pl.BlockSpec((tm, 1024), row), pl.BlockSpec((tm, 1024), row),
                  pl.BlockSpec((D_MODEL, D_MODEL), lambda i: (0, 0), pipeline_mode=pl.Buffered(1))],
        out_specs=[pl.BlockSpec((tm, D_MODEL), row), pl.BlockSpec((tm, D_MODEL), row)],
        out_shape=[_sds((s, D_MODEL), F32), _sds((s, D_MODEL), MXU_DTYPE)],
        compiler_params=_params(1),
    )(x, proj, proj, swa, mla, w_out)


def _final_loss(x, tgt, g):
    s = x.shape[0]
    tm = min(512, s)

    def body(x_ref, t_ref, g_ref, dx_ref, dxb_ref, dg_ref, loss_ref):
        @pl.when(pl.program_id(0) == 0)
        def _():
            dg_ref[...] = jnp.zeros(dg_ref.shape, F32)
            loss_ref[...] = jnp.zeros(loss_ref.shape, F32)
        xh, r = _rownorm(x_ref[...])
        gg = g_ref[...]
        err = xh * gg - t_ref[...]
        per_row = jnp.mean(err * err, axis=-1, keepdims=True)
        loss_ref[...] += 0.5 * jnp.sum(per_row, axis=0, keepdims=True)
        dy = err * (1.0 / D_MODEL)
        dg_ref[...] += jnp.sum(dy * xh, axis=0, keepdims=True)
        dx = _rownorm_bwd(dy * gg, xh, r)
        dx_ref[...] = dx
        dxb_ref[...] = dx.astype(dxb_ref.dtype)

    row = lambda i: (i, 0)
    fixed = lambda i: (0, 0)
    return pl.pallas_call(
        body, name="final_loss", grid=(s // tm,),
        in_specs=[pl.BlockSpec((tm, D_MODEL), row), pl.BlockSpec((tm, D_MODEL), row),
                  pl.BlockSpec((1, D_MODEL), fixed)],
        out_specs=[pl.BlockSpec((tm, D_MODEL), row), pl.BlockSpec((tm, D_MODEL), row),
                   pl.BlockSpec((1, D_MODEL), fixed), pl.BlockSpec((8, 128), fixed)],
        out_shape=[_sds((s, D_MODEL), F32), _sds((s, D_MODEL), MXU_DTYPE), _sds((1, D_MODEL), F32),
                   _sds((8, 128), F32)],
        compiler_params=_params(1),
    )(x, tgt, g)


def _out_proj_bwd(dx, proj, swa, mla, w_out):
    s = dx.shape[0]
    tm = min(512, s)

    def body(dx_ref, ga_ref, gb_ref, a_ref, b_ref, w_ref, doa_ref, dob_ref, dg_ref, dlt_ref):
        dx = dx_ref[...].astype(MXU_DTYPE)
        dya = _mm_nt(dx, w_ref[0:1024, :])
        sa, dsa = _silu_parts(ga_ref[...].astype(F32))
        doa_ref[...] = dya * sa
        dg_ref[:, 0:1024] = (dya * a_ref[...].astype(F32) * dsa).astype(dg_ref.dtype)
        dyb = _mm_nt(dx, w_ref[1024:2048, :])
        sb, dsb = _silu_parts(gb_ref[...].astype(F32))
        b = b_ref[...].astype(F32)
        dob = dyb * sb
        dob_ref[...] = dob.astype(dob_ref.dtype)
        dg_ref[:, 1024:2048] = (dyb * b * dsb).astype(dg_ref.dtype)
        prod = dob * b
        for h in range(MLA_HEADS):
            dlt = jnp.sum(prod[:, 128 * h:128 * (h + 1)], axis=1, keepdims=True)
            dlt_ref[h] = jnp.broadcast_to(dlt, (tm, 128)).T[0:8, :]

    row = lambda i: (i, 0)
    return pl.pallas_call(
        body, name="out_proj_bwd", grid=(s // tm,),
        in_specs=[pl.BlockSpec((tm, D_MODEL), row),
                  pl.BlockSpec((tm, 1024), lambda i: (i, GA // 1024)),
                  pl.BlockSpec((tm, 1024), lambda i: (i, GB // 1024)),
                  pl.BlockSpec((tm, 1024), row), pl.BlockSpec((tm, 1024), row),
                  pl.BlockSpec((D_MODEL, D_MODEL), lambda i: (0, 0), pipeline_mode=pl.Buffered(1))],
        out_specs=[pl.BlockSpec((tm, 1024), row), pl.BlockSpec((tm, 1024), row),
                   pl.BlockSpec((tm, D_MODEL), row),
                   pl.BlockSpec((MLA_HEADS, 8, tm), lambda i: (0, 0, i))],
        out_shape=[_sds((s, 1024), F32), _sds((s, 1024), MXU_DTYPE), _sds((s, D_MODEL), MXU_DTYPE),
                   _sds((MLA_HEADS, 8, s), F32)],
        compiler_params=_params(1),
    )(dx, proj, proj, swa, mla, w_out)


def _matmul_tn(a, b, name):
    s, m = a.shape
    n = b.shape[1]
    tm, tn, tk = min(1024, m), min(1024, n), min(2048, s)
    nk = s // tk

    def body(a_ref, b_ref, o_ref, acc_ref):
        k = pl.program_id(2)

        @pl.when(k == 0)
        def _():
            acc_ref[...] = jnp.zeros(acc_ref.shape, F32)
        acc_ref[...] += _mm_tn(a_ref[...], b_ref[...])

        @pl.when(k == nk - 1)
        def _():
            o_ref[...] = acc_ref[...].astype(o_ref.dtype)

    return pl.pallas_call(
        body, name=name, grid=(m // tm, n // tn, nk),
        in_specs=[pl.BlockSpec((tk, tm), lambda i, j, k: (k, i)),
                  pl.BlockSpec((tk, tn), lambda i, j, k: (k, j))],
        out_specs=pl.BlockSpec((tm, tn), lambda i, j, k: (i, j)),
        out_shape=_sds((m, n), GRAD_DTYPE),
        scratch_shapes=[pltpu.VMEM((tm, tn), F32)],
        compiler_params=_params(3),
    )(a, b)


def _grad_w_in(h, dqa, dgate, dlat, dkva, blocks):
    s = h.shape[0]
    tm, tn, tk = 1024, 1024, min(2048, s)
    nk = s // tk
    grid = (D_MODEL // tm, NP // tn, nk)
    na = len(blocks)

    def body(*refs):
        a_ref, dqa_ref, dg8_ref, dlat_ref, dkva_ref = refs[:5]
        o_ref = refs[5 + na]
        acc_ref = refs[6 + 2 * na]
        i, j, k = pl.program_id(0), pl.program_id(1), pl.program_id(2)
        if na:
            ex = _Exchange(refs[5:5 + na], refs[6 + na:6 + 2 * na], refs[7 + 2 * na:], gather=False)

            @pl.when((i == 0) & (j == 0) & (k == 0))
            def _():
                ex.start()

        @pl.when(k == 0)
        def _():
            acc_ref[...] = jnp.zeros(acc_ref.shape, F32)

        @pl.when(j == QA // tn)
        def _():
            acc_ref[...] += _mm_tn(a_ref[...], dqa_ref[...])

        @pl.when((j == GA // tn) | (j == GB // tn))
        def _():
            acc_ref[...] += _mm_tn(a_ref[...], dg8_ref[...])

        @pl.when(j == CQ // tn)
        def _():
            acc_ref[:, 0:768] += _mm_tn(a_ref[...], dlat_ref[...])
            acc_ref[:, 768:1024] += _mm_tn(a_ref[...], dkva_ref[...])

        @pl.when(k == nk - 1)
        def _():
            o_ref[...] = acc_ref[...].astype(o_ref.dtype)

        if na:
            @pl.when((i == grid[0] - 1) & (j == grid[1] - 1) & (k == nk - 1))
            def _():
                ex.wait()

    def when(group, width):
        return pl.BlockSpec((tk, width), lambda i, j, k: (jnp.where(j == group, k, 0), 0))

    outs = pl.pallas_call(
        body, name="grad_w_in_scatter" if na else "grad_w_in", grid=grid,
        in_specs=[pl.BlockSpec((tk, tm), lambda i, j, k: (k, i)),
                  when(QA // tn, 1024),
                  pl.BlockSpec((tk, 1024), lambda i, j, k: (
                      jnp.where((j == GA // tn) | (j == GB // tn), k, 0),
                      jnp.clip(j - GA // tn, 0, 1))),
                  when(CQ // tn, 768), when(CQ // tn, 256)] + [HBM_SPEC] * na,
        out_specs=[pl.BlockSpec((tm, tn), lambda i, j, k: (i, j))] + [HBM_SPEC] * na,
        out_shape=[_sds((D_MODEL, NP), GRAD_DTYPE)] + _exchange_shapes(blocks, False),
        scratch_shapes=[pltpu.VMEM((tm, tn), F32)] + (_Exchange.semaphores(na) if na else []),
        compiler_params=_params(3),
    )(h, dqa, dgate, dlat, dkva, *blocks)
    return outs[0], list(outs[1:])


def _swa_bwd(proj, sinks, do, o):
    s = proj.shape[0]
    nb = s // BLOCK

    def body(sink_ref, q_ref, kp_ref, kc_ref, do_ref, o_ref, dq_ref, dkv_ref, dsink_ref,
             carry_ref, bias_ref):
        n = pl.program_id(0)
        _swa_fill_bias(n, bias_ref)

        @pl.when(n == 0)
        def _():
            carry_ref[...] = jnp.zeros(carry_ref.shape, F32)
            dsink_ref[...] = jnp.zeros(dsink_ref.shape, F32)

        @pl.when(n < nb)
        def _():
            k, v, left = _swa_operands(kp_ref[...].astype(F32), kc_ref[...].astype(F32))
            lane_s = lax.broadcasted_iota(jnp.int32, (8, 128), 1)
            dk, dv = [None, None], [None, None]
            dsink = jnp.zeros((8, 128), F32)
            for j in range(2):
                qs = _swa_stack(_swa_pairs(q_ref, j) * 0.125)
                do_pairs = _swa_pairs(do_ref, j)
                dos = _swa_stack(do_pairs)
                prod_t = (do_pairs * _swa_pairs(o_ref, j)).T
                dlt = jnp.concatenate([jnp.sum(prod_t[0:64], axis=0, keepdims=True),
                                       jnp.sum(prod_t[64:128], axis=0, keepdims=True)], axis=1)
                ex, es, inv = _swa_softmax(qs, k[j], bias_ref[j], _swa_sink_row(sink_ref, j))
                p = ex * inv
                ds = p * (_mm_nt(v[j], dos) - dlt)
                sink_term = es * inv * dlt
                for a in range(8):
                    dsh = -jnp.sum(sink_term[:, 128 * a:128 * (a + 1)], axis=1, keepdims=True)
                    dsink = dsink + jnp.where(lane_s == _swa_head(j, a), dsh, 0.0)
                dq = _swa_unstack(_mm_tn(ds, k[j]))
                dv[j] = _mm(p, dos)
                dk[j] = _mm(ds, qs)
                for a in range(4):
                    cols = slice(128 * (4 * j + a), 128 * (4 * j + a + 1))
                    dq_ref[:, cols] = (dq[128 * a:128 * (a + 1)] * 0.125).astype(dq_ref.dtype)

            def merge(t):
                return jnp.where(left, t[0], pltpu.roll(t[1], 64, 1))

            contrib = jnp.concatenate([merge(dk), merge(dv)], axis=1)
            dkv_ref[...] = (carry_ref[...] + contrib[0:BLOCK]).astype(dkv_ref.dtype)
            carry_ref[...] = contrib[BLOCK:2 * BLOCK]
            dsink_ref[...] += dsink

        @pl.when(n == nb)
        def _():
            dkv_ref[...] = carry_ref[...].astype(dkv_ref.dtype)

    cur = lambda n: (jnp.minimum(n, nb - 1), 0)
    return pl.pallas_call(
        body, name="swa_bwd", grid=(nb + 1,),
        in_specs=[pl.BlockSpec(memory_space=pltpu.SMEM),
                  pl.BlockSpec((BLOCK, 1024), cur),
                  pl.BlockSpec((BLOCK, 256), lambda n: (jnp.clip(n - 1, 0, nb - 1), KVA // 256)),
                  pl.BlockSpec((BLOCK, 256), lambda n: (jnp.minimum(n, nb - 1), KVA // 256)),
                  pl.BlockSpec((BLOCK, 1024), cur), pl.BlockSpec((BLOCK, 1024), cur)],
        out_specs=[pl.BlockSpec((BLOCK, 1024), cur),
                   pl.BlockSpec((BLOCK, 256), lambda n: (jnp.maximum(n - 1, 0), 0)),
                   pl.BlockSpec((8, 128), lambda n: (0, 0))],
        out_shape=[_sds((s, 1024), MXU_DTYPE), _sds((s, 256), MXU_DTYPE), _sds((8, 128), F32)],
        scratch_shapes=[pltpu.VMEM((BLOCK, 256), F32), pltpu.VMEM((2, 2 * BLOCK, SWA_STACK), F32)],
        compiler_params=_params(1),
    )(sinks, proj, proj, proj, do, o)


def _mla_bwd(qh, kh, vh, do, dlt, lse, blocks):
    s = qh.shape[1]
    t = min(MLA_TILE, s)
    nq = s // t

    na = len(blocks)

    def body(*refs):
        q_ref, k_ref, v_ref, do_ref, dlt_ref, lse_ref = refs[:6]
        dq_ref, dk_ref, dv_ref = refs[6 + na:9 + na]
        h, j = pl.program_id(0), pl.program_id(1)
        if na:
            ex = _Exchange(refs[6:6 + na], refs[9 + na:9 + 2 * na], refs[9 + 2 * na:], gather=False)

            @pl.when((h == 0) & (j == 0))
            def _():
                ex.start()

        def step(start, width, diagonal):
            rows = pl.ds(pl.multiple_of(start, t), width)
            q, k, dout = q_ref[rows, :], k_ref[...], do_ref[rows, :]
            sc = _scores_t(k, q, t, True) if diagonal else _mm_nt(k, q)
            p = jnp.exp2(sc * MLA_C2 - lse_ref[0:1, rows])
            dv = _mm(p, dout)
            ds = p * (_mm_nt(v_ref[...], dout) - dlt_ref[0:1, rows])
            dk = _mm(ds, q)
            dq = _mm_tn(ds, k)
            return rows, dq, dk, dv

        rows, dq, dk, dv = step(j * t, t, True)
        dk_ref[...] = dk
        dv_ref[...] = dv

        @pl.when(j == 0)
        def _():
            dq_ref[rows, :] = dq * MLA_SCALE

        @pl.when(j > 0)
        def _():
            dq_ref[rows, :] = (dq_ref[rows, :] + dq) * MLA_SCALE

        def above_diagonal(start, width):
            rows, dq, dk, dv = step(start, width, False)
            dk_ref[...] += dk
            dv_ref[...] += dv

            @pl.when(j == 0)
            def _():
                dq_ref[rows, :] = dq

            @pl.when(j > 0)
            def _():
                dq_ref[rows, :] += dq

        n_above = nq - 1 - j

        def quad(jj, carry):
            above_diagonal((j + 1 + 4 * jj) * t, 4 * t)
            return carry

        lax.fori_loop(0, n_above // 4, quad, 0)

        @pl.when(n_above % 4 >= 2)
        def _():
            above_diagonal((j + 1 + 4 * (n_above // 4)) * t, 2 * t)

        @pl.when(n_above % 2 == 1)
        def _():
            above_diagonal((nq - 1) * t, t)

        dk_ref[...] *= MLA_SCALE

        if na:
            @pl.when((h == MLA_HEADS - 1) & (j == nq - 1))
            def _():
                ex.wait()

    head = lambda h, j: (h, 0, 0)
    kv_map = lambda h, j: (h, j, 0)
    outs = pl.pallas_call(
        body, name="mla_bwd_scatter" if na else "mla_bwd", grid=(MLA_HEADS, nq),
        in_specs=[pl.BlockSpec((None, s, 256), head), pl.BlockSpec((None, t, 256), kv_map),
                  pl.BlockSpec((None, t, 128), kv_map),
                  pl.BlockSpec((s, 128), lambda h, j: (0, h)),
                  pl.BlockSpec((None, 8, s), head), pl.BlockSpec((None, 8, s), head)]
        + [HBM_SPEC] * na,
        out_specs=[pl.BlockSpec((None, s, 256), head),
                   pl.BlockSpec((None, t, 256), kv_map), pl.BlockSpec((None, t, 128), kv_map)]
        + [HBM_SPEC] * na,
        out_shape=[_sds((MLA_HEADS, s, 256), F32), _sds((MLA_HEADS, s, 256), F32),
                   _sds((MLA_HEADS, s, 128), F32)] + _exchange_shapes(blocks, False),
        scratch_shapes=_Exchange.semaphores(na) if na else [],
        compiler_params=_params(2),
    )(qh, kh, vh, do, dlt, lse, *blocks)
    return outs[0], outs[1], outs[2], list(outs[3:])


def _mla_prep_bwd(dqh, dkh, dvh, proj, gq, gkv, w_q, w_kv, tabs):
    s = proj.shape[0]
    tm = min(256, s)
    nm = s // tm
    c, s1, s2 = tabs

    def body(dq_ref, dk_ref, dv_ref, p_ref, gq_ref, gkv_ref, wq_ref, wkv_ref,
             c_ref, s1_ref, s2_ref, dp_ref, dwq_ref, dwkv_ref, dgq_ref, dgkv_ref,
             dqf_ref, dkvf_ref, dwq_acc, dwkv_acc):
        @pl.when(pl.program_id(0) == 0)
        def _():
            dgq_ref[...] = jnp.zeros(dgq_ref.shape, F32)
            dgkv_ref[...] = jnp.zeros(dgkv_ref.shape, F32)
            dwq_acc[...] = jnp.zeros(dwq_acc.shape, F32)
            dwkv_acc[...] = jnp.zeros(dwkv_acc.shape, F32)
        cc, ns1, ns2 = c_ref[...], -s1_ref[...], -s2_ref[...]
        dkr = jnp.zeros((tm, 128), F32)
        for h in range(MLA_HEADS):
            dqf_ref[:, 128 * h:128 * (h + 1)] = dq_ref[h, :, 0:128].astype(dqf_ref.dtype)
            dqf_ref[:, 1024 + 128 * h:1024 + 128 * (h + 1)] = _rope(
                dq_ref[h, :, 128:256], cc, ns1, ns2).astype(dqf_ref.dtype)
            dkvf_ref[:, 128 * h:128 * (h + 1)] = dk_ref[h, :, 0:128].astype(dkvf_ref.dtype)
            dkvf_ref[:, 1024 + 128 * h:1024 + 128 * (h + 1)] = dv_ref[h].astype(dkvf_ref.dtype)
            dkr = dkr + dk_ref[h, :, 128:256]
        dcqn = _mm_nt(dqf_ref[...], wq_ref[...])
        dckvn = _mm_nt(dkvf_ref[...], wkv_ref[...])
        cqh, rq = _rownorm(p_ref[:, 0:384].astype(F32))
        ckvh, rkv = _rownorm(p_ref[:, 384:640].astype(F32))
        dgq_ref[...] += jnp.sum(dcqn * cqh, axis=0, keepdims=True)
        dgkv_ref[...] += jnp.sum(dckvn * ckvh, axis=0, keepdims=True)
        dp_ref[:, 0:384] = _rownorm_bwd(dcqn * gq_ref[...], cqh, rq).astype(dp_ref.dtype)
        dp_ref[:, 384:640] = _rownorm_bwd(dckvn * gkv_ref[...], ckvh, rkv).astype(dp_ref.dtype)
        dp_ref[:, 640:768] = _rope(dkr, cc, ns1, ns2).astype(dp_ref.dtype)
        dwq_acc[...] += _mm_tn(cqh * gq_ref[...], dqf_ref[...])
        dwkv_acc[...] += _mm_tn(ckvh * gkv_ref[...], dkvf_ref[...])

        @pl.when(pl.program_id(0) == nm - 1)
        def _():
            dwq_ref[...] = dwq_acc[...].astype(dwq_ref.dtype)
            dwkv_ref[...] = dwkv_acc[...].astype(dwkv_ref.dtype)

    row = lambda i: (i, 0)
    fixed = lambda i: (0, 0)
    head = lambda i: (0, i, 0)
    return pl.pallas_call(
        body, name="mla_prep_bwd", grid=(nm,),
        in_specs=[pl.BlockSpec((MLA_HEADS, tm, 256), head), pl.BlockSpec((MLA_HEADS, tm, 256), head),
                  pl.BlockSpec((MLA_HEADS, tm, 128), head),
                  pl.BlockSpec((tm, 768), lambda i: (i, CQ // 768)),
                  pl.BlockSpec((1, Q_RANK), fixed), pl.BlockSpec((1, KV_RANK), fixed),
                  pl.BlockSpec((Q_RANK, 2048), fixed), pl.BlockSpec((KV_RANK, 2048), fixed),
                  pl.BlockSpec((tm, 128), row), pl.BlockSpec((tm, 128), row),
                  pl.BlockSpec((tm, 128), row)],
        out_specs=[pl.BlockSpec((tm, 768), row), pl.BlockSpec((Q_RANK, 2048), fixed),
                   pl.BlockSpec((KV_RANK, 2048), fixed),
                   pl.BlockSpec((1, Q_RANK), fixed), pl.BlockSpec((1, KV_RANK), fixed)],
        out_shape=[_sds((s, 768), MXU_DTYPE), _sds((Q_RANK, 2048), GRAD_DTYPE),
                   _sds((KV_RANK, 2048), GRAD_DTYPE),
                   _sds((1, Q_RANK), F32), _sds((1, KV_RANK), F32)],
        scratch_shapes=[pltpu.VMEM((tm, 2048), MXU_DTYPE), pltpu.VMEM((tm, 2048), MXU_DTYPE),
                        pltpu.VMEM((Q_RANK, 2048), F32), pltpu.VMEM((KV_RANK, 2048), F32)],
        compiler_params=_params(1),
    )(dqh, dkh, dvh, proj, gq, gkv, w_q, w_kv, c, s1, s2)


def _in_proj_bwd(dqa, dgate, dlat, dkva, w, x, dx_out, g, blocks):
    s = x.shape[0]
    tm = min(256, s)
    nm = s // tm
    na = len(blocks)

    def body(*refs):
        dqa_ref, dg8_ref, dlat_ref, dkva_ref, w_ref, x_ref, dxo_ref, g_ref = refs[:8]
        dx_ref, dxb_ref, dg_ref = refs[8 + na:11 + na]
        if na:
            ex = _Exchange(refs[8:8 + na], refs[11 + na:11 + 2 * na], refs[11 + 2 * na:], gather=False)

        @pl.when(pl.program_id(0) == 0)
        def _():
            dg_ref[...] = jnp.zeros(dg_ref.shape, F32)
            if na:
                ex.start()

        dh = (_mm_nt(dqa_ref[...], w_ref[:, QA:QA + 1024])
              + _mm_nt(dg8_ref[:, 0:1024], w_ref[:, GA:GA + 1024])
              + _mm_nt(dg8_ref[:, 1024:2048], w_ref[:, GB:GB + 1024])
              + _mm_nt(dlat_ref[...], w_ref[:, CQ:CQ + 768])
              + _mm_nt(dkva_ref[...], w_ref[:, KVA:KVA + 256]))
        xh, r = _rownorm(x_ref[...])
        dg_ref[...] += jnp.sum(dh * xh, axis=0, keepdims=True)
        dx = dxo_ref[...] + _rownorm_bwd(dh * g_ref[...], xh, r)
        dx_ref[...] = dx
        dxb_ref[...] = dx.astype(dxb_ref.dtype)

        if na:
            @pl.when(pl.program_id(0) == nm - 1)
            def _():
                ex.wait()

    row = lambda i: (i, 0)
    fixed = lambda i: (0, 0)
    outs = pl.pallas_call(
        body, name="in_proj_bwd_scatter" if na else "in_proj_bwd", grid=(nm,),
        in_specs=[pl.BlockSpec((tm, 1024), row), pl.BlockSpec((tm, 2048), row),
                  pl.BlockSpec((tm, 768), row), pl.BlockSpec((tm, 256), row),
                  pl.BlockSpec((D_MODEL, NP), fixed, pipeline_mode=pl.Buffered(1)),
                  pl.BlockSpec((tm, D_MODEL), row), pl.BlockSpec((tm, D_MODEL), row),
                  pl.BlockSpec((1, D_MODEL), fixed)] + [HBM_SPEC] * na,
        out_specs=[pl.BlockSpec((tm, D_MODEL), row), pl.BlockSpec((tm, D_MODEL), row),
                   pl.BlockSpec((1, D_MODEL), fixed)] + [HBM_SPEC] * na,
        out_shape=[_sds((s, D_MODEL), F32), _sds((s, D_MODEL), MXU_DTYPE), _sds((1, D_MODEL), F32)]
        + _exchange_shapes(blocks, False),
        scratch_shapes=_Exchange.semaphores(na) if na else [],
        compiler_params=_params(1),
    )(dqa, dgate, dlat, dkva, w, x, dx_out, g, *blocks)
    return outs[0], outs[1], outs[2], list(outs[3:])


def _adamw_update(g, w_ref, m_ref, v_ref, g_ref, d_ref, nm_ref, nv_ref):
    m2 = ADAM_B1 * m_ref[...] + (1.0 - ADAM_B1) * g
    v2 = ADAM_B2 * v_ref[...] + (1.0 - ADAM_B2) * (g * g)
    m_hat = m2 / (1.0 - ADAM_B1 ** ADAM_STEP)
    v_hat = v2 / (1.0 - ADAM_B2 ** ADAM_STEP)
    g_ref[...] = g
    d_ref[...] = -ADAM_LR * (m_hat / (jnp.sqrt(v_hat) + ADAM_EPS) + ADAM_WD * w_ref[...])
    nm_ref[...] = m2
    nv_ref[...] = v2


def _reduce_adamw_in(parts, w_t, m_t, v_t, name):
    n_layers = len(parts)
    tc = 512
    nc = D_MODEL // tc

    def body(*refs):
        p_refs = refs[:n_layers]
        w_ref, m_ref, v_ref, g_ref, d_ref, nm_ref, nv_ref = refs[n_layers:]
        layer = pl.program_id(0)
        for l in range(n_layers):
            @pl.when(layer == l)
            def _(l=l):
                g = p_refs[l][0].astype(F32)
                for k in range(1, N_DEV):
                    g = g + p_refs[l][k].astype(F32)
                _adamw_update(g[0:SHARD_COLS], w_ref, m_ref, v_ref, g_ref, d_ref, nm_ref, nv_ref)

    def part_spec(l):
        return pl.BlockSpec((N_DEV, SHARD_PAD, tc),
                            lambda layer, j: (0, 0, jnp.where(layer == l, j, 0)))

    blk = pl.BlockSpec((None, SHARD_COLS, tc), lambda layer, j: (layer, 0, j))
    return pl.pallas_call(
        body, name=name, grid=(n_layers, nc),
        in_specs=[part_spec(l) for l in range(n_layers)] + [blk, blk, blk],
        out_specs=[blk, blk, blk, blk],
        out_shape=[_sds((n_layers, SHARD_COLS, D_MODEL), F32)] * 4,
        compiler_params=_params(2),
    )(*parts, w_t, m_t, v_t)


def _reduce_adamw(parts, w, m, v, name):
    n_layers = len(parts)
    rows, part_cols = parts[0].shape[1:]
    cols = w.shape[-1]
    lanes = -(-cols // 128) * 128
    tr = rows
    for cand in (1024, 512, 256, 128, 64, 32, 16, 8):
        if rows % cand == 0 and N_DEV * cand * lanes * 4 <= 8 * 1024 * 1024:
            tr = cand
            break
    nr = rows // tr

    def body(*refs):
        p_refs = refs[:n_layers]
        w_ref, m_ref, v_ref, g_ref, d_ref, nm_ref, nv_ref = refs[n_layers:]
        layer = pl.program_id(0)
        for l in range(n_layers):
            @pl.when(layer == l)
            def _(l=l):
                g = p_refs[l][0, :, 0:cols].astype(F32)
                for k in range(1, N_DEV):
                    g = g + p_refs[l][k, :, 0:cols].astype(F32)
                _adamw_update(g, w_ref, m_ref, v_ref, g_ref, d_ref, nm_ref, nv_ref)

    def part_spec(l):
        return pl.BlockSpec((N_DEV, tr, part_cols),
                            lambda layer, i: (0, jnp.where(layer == l, i, 0), 0))

    blk = pl.BlockSpec((tr, cols), lambda layer, i: (layer * nr + i, 0))
    return pl.pallas_call(
        body, name=name, grid=(n_layers, nr),
        in_specs=[part_spec(l) for l in range(n_layers)] + [blk, blk, blk],
        out_specs=[blk, blk, blk, blk],
        out_shape=[_sds((n_layers * rows, cols), F32)] * 4,
        compiler_params=_params(2),
    )(*parts, w, m, v)


def _position():
    x, y, c = lax.axis_index("x"), lax.axis_index("y"), lax.axis_index("c")
    return x, y, c


def _index(px, py, pc):
    return 4 * px + 2 * py + pc


HBM_SPEC = pl.BlockSpec(memory_space=pltpu.HBM)
IN_BLOCKED = (D_MODEL, N_DEV * SHARD_PAD)
IN_BLOCKED_T = (N_DEV * SHARD_PAD, D_MODEL)


def _block(ref, idx):
    if tuple(ref.shape) == IN_BLOCKED:
        return ref.at[:, pl.ds(pl.multiple_of(idx * SHARD_PAD, SHARD_PAD), SHARD_PAD)]
    if tuple(ref.shape) == IN_BLOCKED_T:
        return ref.at[pl.ds(pl.multiple_of(idx * SHARD_PAD, SHARD_PAD), SHARD_PAD)]
    return ref.at[idx]


class _Gather:
    def __init__(self, srcs, dsts, sems):
        self.srcs, self.dsts = srcs, dsts
        self.send_sems, self.recv_sems, self.local_sems = sems
        x, y, c = _position()
        self.c = c
        self.me, self.sibling = (x, y, c), (x, y, 1 - c)
        self.chips = [(1 - x, y), (x, 1 - y), (1 - x, 1 - y)]

    def _copy(self, a, k, block, to, own=False):
        slot = _block(self.dsts[a], _index(*block))
        return pltpu.make_async_remote_copy(
            src_ref=self.srcs[a] if own else slot, dst_ref=slot,
            send_sem=self.send_sems.at[7 * a + k], recv_sem=self.recv_sems.at[7 * a + k],
            device_id=to, device_id_type=MESH)

    def _local(self, a):
        return pltpu.make_async_copy(self.srcs[a], _block(self.dsts[a], _index(*self.me)),
                                     self.local_sems.at[a])

    def _first(self, a):
        return [self._copy(a, 0, self.me, self.sibling, own=True)] + [
            self._copy(a, 1 + j, self.me, (*chip, self.c), own=True)
            for j, chip in enumerate(self.chips)]

    def _passed(self, a, j):
        return self._copy(a, 4 + j, (*self.chips[j], self.c), self.sibling)

    def start(self):
        for a in range(len(self.srcs)):
            self._local(a).start()
            for cp in self._first(a):
                cp.start()

    def forward(self):
        for j, chip in enumerate(self.chips):
            for a in range(len(self.srcs)):
                self._copy(a, 1 + j, (*chip, self.c), self.me).wait_recv()
                self._passed(a, j).start()

    def finish(self):
        for a in range(len(self.srcs)):
            self._copy(a, 0, self.sibling, self.me).wait_recv()
            for j, chip in enumerate(self.chips):
                self._copy(a, 4 + j, (*chip, 1 - self.c), self.me).wait_recv()
            for cp in self._first(a):
                cp.wait_send()
            for j in range(3):
                self._passed(a, j).wait_send()
            self._local(a).wait()


def _all_gather(shards, name):
    na = len(shards)

    def body(*refs):
        g = _Gather(refs[:na], refs[na:2 * na], refs[2 * na:])
        g.start()
        g.forward()
        g.finish()

    return pl.pallas_call(
        body, name=name,
        in_specs=[HBM_SPEC] * na, out_specs=[HBM_SPEC] * na,
        out_shape=_exchange_shapes(shards, True),
        scratch_shapes=_Exchange.semaphores(na),
    )(*shards)


class _Exchange:
    def __init__(self, srcs, dsts, sems, gather):
        self.srcs, self.dsts, self.gather = srcs, dsts, gather
        self.send_sems, self.recv_sems, self.local_sems = sems
        x, y, c = _position()
        self.me = _index(x, y, c)
        self.peers = [(x ^ ((k >> 2) & 1), y ^ ((k >> 1) & 1), c ^ (k & 1)) for k in range(1, N_DEV)]

    @staticmethod
    def semaphores(na):
        return [pltpu.SemaphoreType.DMA((7 * na,)), pltpu.SemaphoreType.DMA((7 * na,)),
                pltpu.SemaphoreType.DMA((na,))]

    def _src(self, a, slot):
        return self.srcs[a] if self.gather else _block(self.srcs[a], slot)

    def _local(self, a):
        return pltpu.make_async_copy(self._src(a, self.me), self.dsts[a].at[self.me],
                                     self.local_sems.at[a])

    def _send(self, a, k):
        peer = self.peers[k]
        return pltpu.make_async_remote_copy(
            src_ref=self._src(a, _index(*peer)), dst_ref=self.dsts[a].at[self.me],
            send_sem=self.send_sems.at[7 * a + k], recv_sem=self.recv_sems.at[7 * a + k],
            device_id=peer, device_id_type=MESH)

    def _arrival(self, a, k):
        landed = self.dsts[a].at[_index(*self.peers[k])]
        return pltpu.make_async_remote_copy(
            src_ref=landed, dst_ref=landed,
            send_sem=self.send_sems.at[7 * a + k], recv_sem=self.recv_sems.at[7 * a + k],
            device_id=self.peers[k], device_id_type=MESH)

    def start(self):
        for a in range(len(self.srcs)):
            self._local(a).start()
            for k in range(N_DEV - 1):
                self._send(a, k).start()

    def wait(self):
        for a in range(len(self.srcs)):
            for k in range(N_DEV - 1):
                self._arrival(a, k).wait_recv()
            for k in range(N_DEV - 1):
                self._send(a, k).wait_send()
            self._local(a).wait()


def _exchange_shapes(arrays, gather):
    def shape(a):
        if gather:
            return IN_BLOCKED if a.shape == (D_MODEL, SHARD_PAD) else (N_DEV,) + a.shape
        return (N_DEV, SHARD_PAD, D_MODEL) if a.shape == IN_BLOCKED_T else a.shape
    return [_sds(shape(a), a.dtype) for a in arrays]


def _exchange_call(arrays, gather, name):
    na = len(arrays)

    def body(*refs):
        ex = _Exchange(refs[:na], refs[na:2 * na], refs[2 * na:], gather)
        ex.start()
        ex.wait()

    return pl.pallas_call(
        body, name=name,
        in_specs=[HBM_SPEC] * na, out_specs=[HBM_SPEC] * na,
        out_shape=_exchange_shapes(arrays, gather),
        scratch_shapes=_Exchange.semaphores(na),
    )(*arrays)


def _layer_fwd(x, small, w_in, g_rest, rest_shards, tabs, next_shards):
    attn_g, sinks, gq, gkv = small
    proj, h, gathered_rest = _in_proj(x, attn_g, w_in, rest_shards)
    w_q, w_kv, w_o = _rest_from_gathered(*(gathered_rest if rest_shards else g_rest))
    swa = _swa_fwd(proj, sinks)
    qh, kh, vh, vth = _mla_prep(proj, gq, gkv, w_q, w_kv, tabs)
    mla, lse, gathered = _mla_fwd(qh, kh, vth, next_shards)
    x_next, y = _out_proj(x, proj, swa, mla, w_o)
    return x_next, (x, proj, h, swa, qh, kh, vh, mla, lse, y), (w_in, w_q, w_kv, w_o), gathered


def _layer_bwd(dx, dxb, saved, small, weights, tabs, pending, send_own):
    attn_g, sinks, gq, gkv = small
    w_in, w_q, w_kv, w_o = weights
    x, proj, h, swa, qh, kh, vh, mla, lse, y = saved
    d_o = _matmul_tn(y, dxb, "grad_w_out")
    do_a, do_b, dgate, dlt = _out_proj_bwd(dxb, proj, swa, mla, w_o)
    dqa, dkva, dsk = _swa_bwd(proj, sinks, do_a, swa)
    dqh, dkh, dvh, received = _mla_bwd(qh, kh, vh, do_b, dlt, lse, pending)
    dlat, d_q, d_kv, d_gq, d_gkv = _mla_prep_bwd(dqh, dkh, dvh, proj, gq, gkv, w_q, w_kv, tabs)
    rest_blocks = _rest_grad_blocks(d_q, d_kv, d_o)
    d_inp, got_rest = _grad_w_in(h, dqa, dgate, dlat, dkva, rest_blocks if send_own else [])
    in_block = _in_grad_blocks(d_inp)
    dx, dxb, d_attn, got_in = _in_proj_bwd(dqa, dgate, dlat, dkva, w_in, x, dx, attn_g,
                                           [in_block] if send_own else [])
    small_grads = (d_attn, dsk[0:1, 0:SWA_HEADS], d_gq, d_gkv)
    return dx, dxb, small_grads, [in_block] + rest_blocks, received, got_in + got_rest


def _pack_small_grads(small_grads, d_final, loss):
    d_attn, d_sink, d_gq, d_gkv = zip(*small_grads)
    return jnp.concatenate([
        jnp.concatenate(d_attn, axis=0).reshape(64, 128),
        jnp.concatenate(d_gq, axis=0).reshape(12, 128),
        jnp.concatenate(d_gkv, axis=0).reshape(8, 128),
        d_final.reshape(16, 128),
        jnp.pad(jnp.concatenate(d_sink, axis=1), ((0, 0), (0, 64))),
        loss[0:1],
        jnp.zeros((PACK_ROWS - ROW_LOSS - 1, 128), F32)], axis=0)


def _pack_small(attn, qa, kva, final, sinks):
    return jnp.concatenate([
        attn.reshape(64, 128), qa.reshape(12, 128), kva.reshape(8, 128), final.reshape(16, 128),
        jnp.pad(sinks.reshape(1, 64), ((0, 0), (0, 64))),
        jnp.zeros((PACK_ROWS - ROW_SINK - 1, 128), F32)], axis=0)


def _unpack_small(p):
    return (p[ROW_ATTN:ROW_QA].reshape(DEPTH, D_MODEL), p[ROW_SINK, 0:64].reshape(DEPTH, SWA_HEADS),
            p[ROW_QA:ROW_KVA].reshape(DEPTH, Q_RANK), p[ROW_KVA:ROW_FINAL].reshape(DEPTH, KV_RANK),
            p[ROW_FINAL:ROW_SINK].reshape(D_MODEL))


def kernel(x, attn_norm_g, w_in, swa_sinks, q_a_norm_g, kv_a_norm_g, w_q_b, w_kv_b, w_out, final_norm_g, loss_target, m_attn_norm_g, m_w_in, m_swa_sinks, m_q_a_norm_g, m_kv_a_norm_g, m_w_q_b, m_w_kv_b, m_w_out, m_final_norm_g, v_attn_norm_g, v_w_in, v_swa_sinks, v_q_a_norm_g, v_kv_a_norm_g, v_w_q_b, v_w_kv_b, v_w_out, v_final_norm_g):
    xs, tgt = x[0], loss_target[0]
    tabs = _rope_tables(xs.shape[0])
    shards = [w.astype(MXU_DTYPE) for w in (w_in, w_q_b, w_kv_b, w_out)]
    shards[0] = jnp.pad(shards[0], ((0, 0), (0, 0), (0, SHARD_PAD - SHARD_COLS)))
    layer_shards = lambda l: [w[l] for w in shards]
    smalls = [(attn_norm_g[l:l + 1], swa_sinks[l], q_a_norm_g[l:l + 1], kv_a_norm_g[l:l + 1])
              for l in range(DEPTH)]

    gathered = list(_all_gather(layer_shards(0)[:1], "gather_weights")) + [None] * 3
    weights, saved = [None] * DEPTH, []
    for l in range(DEPTH):
        next_shards = layer_shards(l + 1) if l + 1 < DEPTH else []
        rest_shards = layer_shards(0)[1:] if l == 0 else []
        xs, acts, weights[l], gathered = _layer_fwd(
            xs, smalls[l], _w_in_from_gathered(gathered[0]), gathered[1:], rest_shards, tabs,
            next_shards)
        saved.append(acts)
    dx, dxb, d_final, loss = _final_loss(xs, tgt, final_norm_g.reshape(1, D_MODEL))

    received, small_grads, pending = [None] * DEPTH, [None] * DEPTH, []
    for l in reversed(range(DEPTH)):
        dx, dxb, small_grads[l], blocks, arrived, arrived_own = _layer_bwd(
            dx, dxb, saved[l], smalls[l], weights[l], tabs, pending, send_own=(l == 0))
        if pending:
            received[l + 1] = arrived
        pending = blocks
    received[0] = arrived_own
    small = _exchange_call([_pack_small_grads(small_grads, d_final, loss)], True, "gather_small")[0]

    big = []
    for a, (w, m, v, name) in enumerate(zip((w_in, w_q_b, w_kv_b, w_out),
                                            (m_w_in, m_w_q_b, m_w_kv_b, m_w_out),
                                            (v_w_in, v_w_q_b, v_w_kv_b, v_w_out),
                                            ("adamw_w_in", "adamw_w_q_b", "adamw_w_kv_b",
                                             "adamw_w_out"))):
        parts = [received[l][a] for l in range(DEPTH)]
        if a == 0:
            swap = lambda t: t.transpose(0, 2, 1)
            outs = _reduce_adamw_in(parts, swap(w), swap(m), swap(v), name)
            big.append([swap(t) for t in outs])
        else:
            cols = w.shape[-1]
            flat = lambda t: t.reshape(-1, cols)
            outs = _reduce_adamw(parts, flat(w), flat(m), flat(v), name)
            big.append([t.reshape(w.shape) for t in outs])

    sm = _reduce_adamw(
        [small],
        _pack_small(attn_norm_g, q_a_norm_g, kv_a_norm_g, final_norm_g, swa_sinks),
        _pack_small(m_attn_norm_g, m_q_a_norm_g, m_kv_a_norm_g, m_final_norm_g, m_swa_sinks),
        _pack_small(v_attn_norm_g, v_q_a_norm_g, v_kv_a_norm_g, v_final_norm_g, v_swa_sinks),
        "adamw_small")
    loss = sm[0][ROW_LOSS, 0]
    kinds = []
    for t in range(4):
        attn, sinks, qa, kva, final = _unpack_small(sm[t])
        b_in, b_q, b_kv, b_o = (big[i][t] for i in range(4))
        kinds.append((attn, b_in, sinks, qa, kva, b_q, b_kv, b_o, final))
    return (loss, dx[None], *kinds[0], *kinds[1], *kinds[2], *kinds[3])
```

```python
import functools

import jax
import jax.numpy as jnp
import numpy as np
from jax import lax
from jax.experimental import pallas as pl
from jax.experimental.pallas import tpu as pltpu

F32 = jnp.float32
BF16 = jnp.bfloat16
MXU_DTYPE = BF16
GRAD_DTYPE = BF16
PROJ_DTYPE = BF16
ATTN_DTYPE = BF16

D_MODEL = 2048
DEPTH = 4
EPS = 1e-6
NEG = -1e30
BLOCK = 128
SWA_HEADS = 16
MLA_HEADS = 8
Q_RANK = 384
KV_RANK = 256
MLA_SCALE = 192 ** -0.5
MLA_C2 = MLA_SCALE * 1.4426950408889634
ROPE_THETA = 10000.0
IN_WIDTH = 4032

ADAM_LR = 0.001
ADAM_B1 = 0.9
ADAM_B2 = 0.999
ADAM_EPS = 1e-08
ADAM_WD = 0.01
ADAM_STEP = 10

N_DEV = 8
MESH = pl.DeviceIdType.MESH

NP = 4096
QA, GA, GB, CQ, CKV, KR, KVA = 0, 1024, 2048, 3072, 3456, 3712, 3840

ROW_ATTN, ROW_QA, ROW_KVA, ROW_FINAL, ROW_SINK, ROW_LOSS, PACK_ROWS = 0, 64, 76, 84, 100, 101, 104

VMEM_LIMIT = 56 * 1024 * 1024
MLA_TILE = 512
GATHER_FORWARD_HEAD = 7


def _sds(shape, dtype):
    return jax.ShapeDtypeStruct(shape, dtype)


def _params(n_axes):
    return pltpu.CompilerParams(dimension_semantics=("arbitrary",) * n_axes,
                                vmem_limit_bytes=VMEM_LIMIT)


def _mm(a, b):
    return jnp.dot(a.astype(MXU_DTYPE), b.astype(MXU_DTYPE), preferred_element_type=F32)


def _mm_nt(a, b):
    return lax.dot_general(a.astype(MXU_DTYPE), b.astype(MXU_DTYPE),
                           (((1,), (1,)), ((), ())), preferred_element_type=F32)


def _mm_tn(a, b):
    return lax.dot_general(a.astype(MXU_DTYPE), b.astype(MXU_DTYPE),
                           (((0,), (0,)), ((), ())), preferred_element_type=F32)


def _rownorm(x):
    r = lax.rsqrt(jnp.mean(x * x, axis=-1, keepdims=True) + EPS)
    return x * r, r


def _rownorm_bwd(dxh, xh, r):
    return r * (dxh - xh * jnp.mean(dxh * xh, axis=-1, keepdims=True))


def _rope(t, c, s1, s2):
    return t * c + pltpu.roll(t, 32, 1) * s1 + pltpu.roll(t, 96, 1) * s2


SHARD_COLS = IN_WIDTH // N_DEV
SHARD_PAD = 512


def _orig_col_of_padded():
    o = np.full((NP,), -1, np.int64)
    for start, width, orig in ((QA, 1024, 0), (KVA, 256, 1024), (GA, 1024, 1280), (CQ, 384, 2304),
                               (CKV, 256, 2688), (KR, 64, 2944), (GB, 1024, 3008)):
        o[start:start + width] = np.arange(orig, orig + width)
    return o


def _device_major_src():
    o = _orig_col_of_padded()
    return np.where(o >= 0, o + (SHARD_PAD - SHARD_COLS) * (o // SHARD_COLS), -1)


def _kernel_layout_src():
    o = _orig_col_of_padded()
    where = np.full((IN_WIDTH,), -1, np.int64)
    where[o[o >= 0]] = np.nonzero(o >= 0)[0]
    e = np.arange(N_DEV * SHARD_PAD)
    k, c = e // SHARD_PAD, e % SHARD_PAD
    return np.where(c < SHARD_COLS, where[np.minimum(SHARD_COLS * k + c, IN_WIDTH - 1)], -1)


def _permute_columns(x, src_of, name, transposed=False):
    rows, n_in = x.shape
    n_out = len(src_of)
    plan, mats = [], []
    for t in range(n_out // 128):
        srcs = src_of[128 * t:128 * (t + 1)]
        entry = []
        for u in sorted(set(int(s) // 128 for s in srcs if s >= 0)):
            m = np.zeros((128, 128), np.float32)
            for c, s in enumerate(srcs):
                if s >= 0 and s // 128 == u:
                    m[s % 128, c] = 1.0
            entry.append((u, len(mats)))
            mats.append(m)
        plan.append(entry)
    tr = min(512, rows)

    def body(x_ref, p_ref, o_ref):
        for t, entry in enumerate(plan):
            acc = jnp.zeros((tr, 128), F32)
            for u, idx in entry:
                acc = acc + jnp.dot(x_ref[:, 128 * u:128 * (u + 1)], p_ref[idx],
                                    preferred_element_type=F32)
            if transposed:
                o_ref[128 * t:128 * (t + 1), :] = acc.T.astype(o_ref.dtype)
            else:
                o_ref[:, 128 * t:128 * (t + 1)] = acc.astype(o_ref.dtype)

    table = jnp.asarray(np.stack(mats), x.dtype)
    return pl.pallas_call(
        body, name=name, grid=(rows // tr,),
        in_specs=[pl.BlockSpec((tr, n_in), lambda i: (i, 0)),
                  pl.BlockSpec(table.shape, lambda i: (0, 0, 0))],
        out_specs=(pl.BlockSpec((n_out, tr), lambda i: (0, i)) if transposed
                   else pl.BlockSpec((tr, n_out), lambda i: (i, 0))),
        out_shape=_sds((n_out, rows) if transposed else (rows, n_out), x.dtype),
        compiler_params=_params(1),
    )(x, table)


def _w_in_from_gathered(g_in):
    return _permute_columns(g_in, _device_major_src(), "w_in_layout")


def _rest_from_gathered(g_qb, g_kvb, g_out):
    qb = g_qb.transpose(1, 0, 2)
    rope = jnp.pad(qb[..., 128:], ((0, 0), (0, 0), (0, 64)))
    w_q = jnp.concatenate([qb[..., :128].reshape(Q_RANK, 1024),
                           rope.reshape(Q_RANK, 1024)], axis=-1)
    kvb = g_kvb.transpose(1, 0, 2)
    w_kv = jnp.concatenate([kvb[..., :128].reshape(KV_RANK, 1024),
                            kvb[..., 128:].reshape(KV_RANK, 1024)], axis=-1)
    w_o = g_out.reshape(D_MODEL, D_MODEL)
    return w_q, w_kv, w_o


def _in_grad_blocks(d_inp):
    return _permute_columns(d_inp, _kernel_layout_src(), "grad_w_in_layout", transposed=True)


def _rest_grad_blocks(d_q, d_kv, d_o):
    qn = d_q[:, :1024].reshape(Q_RANK, 8, 128)
    qr = d_q[:, 1024:].reshape(Q_RANK, 8, 128)[..., :64]
    b_q = jnp.concatenate([qn, qr], axis=-1).transpose(1, 0, 2)
    kn = d_kv[:, :1024].reshape(KV_RANK, 8, 128)
    vv = d_kv[:, 1024:].reshape(KV_RANK, 8, 128)
    b_kv = jnp.concatenate([kn, vv], axis=-1).transpose(1, 0, 2)
    b_o = d_o.reshape(N_DEV, 256, D_MODEL)
    return [b_q, b_kv, b_o]


def _rope_tables(s):
    pos = jnp.arange(s, dtype=F32)
    inv_freq = ROPE_THETA ** (-jnp.arange(0, 64, 2, dtype=F32) / 64)
    ang = pos[:, None] * inv_freq[None, :]
    cos, sin = jnp.cos(ang), jnp.sin(ang)
    z32 = jnp.zeros((s, 32), F32)
    z64 = jnp.zeros((s, 64), F32)
    c = jnp.concatenate([cos, cos, z64], axis=1)
    s1 = jnp.concatenate([z32, sin, z64], axis=1)
    s2 = jnp.concatenate([-sin, z32, z64], axis=1)
    return c, s1, s2


def _in_proj(x, g, w, shards):
    s = x.shape[0]
    tm, tn = min(512, s), 1024
    nm = s // tm
    na = len(shards)

    def body(*refs):
        x_ref, g_ref, w_ref = refs[:3]
        o_ref, h_ref = refs[3 + na:5 + na]
        i = pl.program_id(0)
        if na:
            ex = _Gather(refs[3:3 + na], refs[5 + na:5 + 2 * na], refs[5 + 2 * na:])

            @pl.when(i == 0)
            def _():
                ex.start()

            @pl.when(i == (3 * nm) // 4)
            def _():
                ex.forward()

        xh, _ = _rownorm(x_ref[...])
        h_ref[...] = (xh * g_ref[...]).astype(h_ref.dtype)
        for j in range(NP // tn):
            cols = slice(j * tn, (j + 1) * tn)
            o_ref[:, cols] = jnp.dot(h_ref[...], w_ref[:, cols],
                                     preferred_element_type=F32).astype(o_ref.dtype)

        if na:
            @pl.when(i == nm - 1)
            def _():
                ex.finish()

    row = lambda i: (i, 0)
    fixed = lambda i: (0, 0)
    outs = pl.pallas_call(
        body, name="in_proj_gather" if na else "in_proj", grid=(nm,),
        in_specs=[pl.BlockSpec((tm, D_MODEL), row), pl.BlockSpec((1, D_MODEL), fixed),
                  pl.BlockSpec((D_MODEL, NP), fixed, pipeline_mode=pl.Buffered(1))]
        + [HBM_SPEC] * na,
        out_specs=[pl.BlockSpec((tm, NP), row), pl.BlockSpec((tm, D_MODEL), row)]
        + [HBM_SPEC] * na,
        out_shape=[_sds((s, NP), PROJ_DTYPE), _sds((s, D_MODEL), MXU_DTYPE)]
        + _exchange_shapes(shards, True),
        scratch_shapes=_Exchange.semaphores(na) if na else [],
        compiler_params=_params(1),
    )(x, g, w, *shards)
    return outs[0], outs[1], list(outs[2:])


def _swa_slopes():
    return [2.0 ** (-8.0 * (h + 1) / SWA_HEADS) for h in range(SWA_HEADS)]


SWA_STACK = 8 * BLOCK


def _swa_head(j, a):
    return 2 * (4 * j + a % 4) + a // 4


def _swa_operands(kv_p, kv_c):
    kk = jnp.concatenate([kv_p[:, :128], kv_c[:, :128]], axis=0)
    vv = jnp.concatenate([kv_p[:, 128:], kv_c[:, 128:]], axis=0)
    left = lax.broadcasted_iota(jnp.int32, (2 * BLOCK, 128), 1) < 64

    def heads(t):
        return [jnp.where(left, t, 0.0), jnp.where(left, pltpu.roll(t, 64, 1), 0.0)]

    return heads(kk), heads(vv), left


def _swa_by_block(vals):
    a = lax.broadcasted_iota(jnp.int32, (1, SWA_STACK), 1) >> 7
    row = jnp.full((1, SWA_STACK), vals[7], F32)
    for t in range(6, -1, -1):
        row = jnp.where(a == t, vals[t], row)
    return row


def _swa_bias(n, j):
    slopes = _swa_slopes()
    ki = lax.broadcasted_iota(jnp.int32, (2 * BLOCK, SWA_STACK), 0)
    r = lax.broadcasted_iota(jnp.int32, (2 * BLOCK, SWA_STACK), 1)
    delta = BLOCK + (r & (BLOCK - 1)) - ki
    valid = (delta >= 0) & (delta < BLOCK) & ((n - 1) * BLOCK + ki >= 0)
    slope = _swa_by_block([slopes[_swa_head(j, a)] for a in range(8)])
    return jnp.where(valid, -slope * delta.astype(F32), NEG)


def _swa_fill_bias(n, bias_ref):
    @pl.when(n <= 1)
    def _():
        for j in range(2):
            bias_ref[j] = _swa_bias(n, j)


def _swa_sink_row(sink_ref, j):
    return _swa_by_block([sink_ref[_swa_head(j, a)] for a in range(8)])


def _swa_pairs(ref, j):
    return jnp.concatenate([ref[:, 128 * (4 * j + a):128 * (4 * j + a + 1)] for a in range(4)],
                           axis=0).astype(F32)


def _swa_stack(pairs):
    return jnp.concatenate([pairs, pltpu.roll(pairs, 64, 1)], axis=0)


def _swa_unstack(t):
    left = lax.broadcasted_iota(jnp.int32, (4 * BLOCK, 128), 1) < 64
    return jnp.where(left, t[0:4 * BLOCK], pltpu.roll(t[4 * BLOCK:8 * BLOCK], 64, 1))


def _swa_softmax(qs, kmat, bias, sink):
    sc = _mm_nt(kmat, qs) + bias
    m = jnp.maximum(jnp.max(sc, axis=0, keepdims=True), sink)
    ex = jnp.exp(sc - m)
    es = jnp.exp(sink - m)
    return ex, es, 1.0 / (jnp.sum(ex, axis=0, keepdims=True) + es)


def _swa_fwd(proj, sinks):
    s = proj.shape[0]
    nb = s // BLOCK

    def body(sink_ref, q_ref, kp_ref, kc_ref, o_ref, bias_ref):
        n = pl.program_id(0)
        _swa_fill_bias(n, bias_ref)
        k, v, _ = _swa_operands(kp_ref[...].astype(F32), kc_ref[...].astype(F32))
        for j in range(2):
            qs = _swa_stack(_swa_pairs(q_ref, j) * 0.125)
            ex, _, inv = _swa_softmax(qs, k[j], bias_ref[j], _swa_sink_row(sink_ref, j))
            o_t = _mm(v[j].T, ex) * inv
            out = jnp.concatenate([o_t[0:64, 0:4 * BLOCK], o_t[0:64, 4 * BLOCK:8 * BLOCK]], axis=0).T
            for a in range(4):
                o_ref[:, 128 * (4 * j + a):128 * (4 * j + a + 1)] = out[128 * a:128 * (a + 1)].astype(
                    o_ref.dtype)

    return pl.pallas_call(
        body, name="swa_fwd", grid=(nb,),
        in_specs=[pl.BlockSpec(memory_space=pltpu.SMEM),
                  pl.BlockSpec((BLOCK, 1024), lambda n: (n, 0)),
                  pl.BlockSpec((BLOCK, 256), lambda n: (jnp.maximum(n - 1, 0), KVA // 256)),
                  pl.BlockSpec((BLOCK, 256), lambda n: (n, KVA // 256))],
        out_specs=pl.BlockSpec((BLOCK, 1024), lambda n: (n, 0)),
        out_shape=_sds((s, 1024), ATTN_DTYPE),
        scratch_shapes=[pltpu.VMEM((2, 2 * BLOCK, SWA_STACK), F32)],
        compiler_params=_params(1),
    )(sinks, proj, proj, proj)


def _mla_prep(proj, gq, gkv, w_q, w_kv, tabs):
    s = proj.shape[0]
    tm = min(512, s)
    c, s1, s2 = tabs

    def body(p_ref, gq_ref, gkv_ref, wq_ref, wkv_ref, c_ref, s1_ref, s2_ref,
             q_ref, k_ref, v_ref, vt_ref):
        cqh, _ = _rownorm(p_ref[:, 0:384].astype(F32))
        ckvh, _ = _rownorm(p_ref[:, 384:640].astype(F32))
        q = _mm(cqh * gq_ref[...], wq_ref[...])
        kv = _mm(ckvh * gkv_ref[...], wkv_ref[...])
        cc, ss1, ss2 = c_ref[...], s1_ref[...], s2_ref[...]
        krr = _rope(p_ref[:, 640:768].astype(F32), cc, ss1, ss2).astype(k_ref.dtype)
        for h in range(MLA_HEADS):
            q_ref[h, :, 0:128] = q[:, 128 * h:128 * (h + 1)].astype(q_ref.dtype)
            q_ref[h, :, 128:256] = _rope(q[:, 1024 + 128 * h:1024 + 128 * (h + 1)],
                                         cc, ss1, ss2).astype(q_ref.dtype)
            k_ref[h, :, 0:128] = kv[:, 128 * h:128 * (h + 1)].astype(k_ref.dtype)
            k_ref[h, :, 128:256] = krr
            vv = kv[:, 1024 + 128 * h:1024 + 128 * (h + 1)]
            v_ref[h] = vv.astype(v_ref.dtype)
            vt_ref[h, 0:128, :] = vv.T.astype(vt_ref.dtype)
            vt_ref[h, 128:256, :] = jnp.ones((128, tm), vt_ref.dtype)

    row = lambda i: (i, 0)
    fixed = lambda i: (0, 0)
    return pl.pallas_call(
        body, name="mla_prep", grid=(s // tm,),
        in_specs=[pl.BlockSpec((tm, 768), lambda i: (i, CQ // 768)),
                  pl.BlockSpec((1, Q_RANK), fixed), pl.BlockSpec((1, KV_RANK), fixed),
                  pl.BlockSpec((Q_RANK, 2048), fixed), pl.BlockSpec((KV_RANK, 2048), fixed),
                  pl.BlockSpec((tm, 128), row), pl.BlockSpec((tm, 128), row),
                  pl.BlockSpec((tm, 128), row)],
        out_specs=[pl.BlockSpec((MLA_HEADS, tm, 256), lambda i: (0, i, 0)),
                   pl.BlockSpec((MLA_HEADS, tm, 256), lambda i: (0, i, 0)),
                   pl.BlockSpec((MLA_HEADS, tm, 128), lambda i: (0, i, 0)),
                   pl.BlockSpec((MLA_HEADS, 256, tm), lambda i: (0, 0, i))],
        out_shape=[_sds((MLA_HEADS, s, 256), MXU_DTYPE), _sds((MLA_HEADS, s, 256), MXU_DTYPE),
                   _sds((MLA_HEADS, s, 128), MXU_DTYPE), _sds((MLA_HEADS, 256, s), MXU_DTYPE)],
        compiler_params=_params(1),
    )(proj, gq, gkv, w_q, w_kv, c, s1, s2)


def _scores_t(k, q, t, diagonal):
    sc = _mm_nt(k, q)
    if diagonal:
        key = lax.broadcasted_iota(jnp.int32, (t, t), 0)
        query = lax.broadcasted_iota(jnp.int32, (t, t), 1)
        sc = jnp.where(key <= query, sc, NEG)
    return sc


def _mla_fwd(qh, kh, vth, shards):
    s = qh.shape[1]
    t = min(MLA_TILE, s)
    nq = s // t
    na = len(shards)

    def body(*refs):
        q_ref, k_ref, vt_ref = refs[:3]
        o_ref, lse_ref = refs[3 + na:5 + na]
        m_ref, acc_ref = refs[5 + 2 * na:7 + 2 * na]
        h, i = pl.program_id(0), pl.program_id(1)
        if na:
            ex = _Gather(refs[3:3 + na], refs[5 + na:5 + 2 * na], refs[7 + 2 * na:])

            @pl.when((h == 0) & (i == 0))
            def _():
                ex.start()

            @pl.when((h == GATHER_FORWARD_HEAD) & (i == 0))
            def _():
                ex.forward()

        m_ref[...] = jnp.full(m_ref.shape, NEG, F32)
        acc_ref[...] = jnp.zeros(acc_ref.shape, F32)

        def step(start, width, diagonal):
            keys = pl.ds(pl.multiple_of(start, t), width)
            if diagonal:
                sc = _scores_t(k_ref[keys, :], q_ref[...], t, True)
            else:
                sc = _mm_nt(k_ref[keys, :], q_ref[...])
            m_prev = m_ref[...]
            m_new = jnp.maximum(m_prev, jnp.max(sc, axis=0, keepdims=True))
            alpha = jnp.exp2((m_prev - m_new) * MLA_C2)
            p = jnp.exp2((sc - m_new[0:1, :]) * MLA_C2)
            acc_ref[...] = alpha[0:1, :] * acc_ref[...] + _mm(vt_ref[:, keys], p)
            m_ref[...] = m_new

        def below_diagonal(jj, carry):
            step(jj * (4 * t), 4 * t, False)
            return carry

        lax.fori_loop(0, i // 4, below_diagonal, 0)

        @pl.when(i % 4 >= 2)
        def _():
            step((i // 4) * (4 * t), 2 * t, False)

        @pl.when(i % 2 == 1)
        def _():
            step((i - 1) * t, t, False)

        step(i * t, t, True)
        l = acc_ref[128:136, :]
        o_ref[...] = (acc_ref[0:128, :] / l[0:1, :]).T.astype(o_ref.dtype)
        lse_ref[...] = m_ref[...] * MLA_C2 + jnp.log2(l)

        if na:
            @pl.when((h == MLA_HEADS - 1) & (i == nq - 1))
            def _():
                ex.finish()

    head = lambda h, i: (h, 0, 0)
    outs = pl.pallas_call(
        body, name="mla_fwd_gather" if na else "mla_fwd", grid=(MLA_HEADS, nq),
        in_specs=[pl.BlockSpec((None, t, 256), lambda h, i: (h, i, 0)),
                  pl.BlockSpec((None, s, 256), head),
                  pl.BlockSpec((None, 256, s), head)] + [HBM_SPEC] * na,
        out_specs=[pl.BlockSpec((t, 128), lambda h, i: (i, h)),
                   pl.BlockSpec((None, 8, t), lambda h, i: (h, 0, i))] + [HBM_SPEC] * na,
        out_shape=[_sds((s, 1024), ATTN_DTYPE), _sds((MLA_HEADS, 8, s), F32)]
        + _exchange_shapes(shards, True),
        scratch_shapes=[pltpu.VMEM((8, t), F32), pltpu.VMEM((256, t), F32)]
        + (_Exchange.semaphores(na) if na else []),
        compiler_params=_params(2),
    )(qh, kh, vth, *shards)
    return outs[0], outs[1], list(outs[2:])


def _silu_parts(g):
    sg = jax.nn.sigmoid(g)
    return g * sg, sg * (1.0 + g * (1.0 - sg))


def _out_proj(x, proj, swa, mla, w_out):
    s = x.shape[0]
    tm = min(512, s)

    def body(x_ref, ga_ref, gb_ref, a_ref, b_ref, w_ref, xo_ref, y_ref):
        sa, _ = _silu_parts(ga_ref[...].astype(F32))
        sb, _ = _silu_parts(gb_ref[...].astype(F32))
        y_ref[:, 0:1024] = (a_ref[...].astype(F32) * sa).astype(y_ref.dtype)
        y_ref[:, 1024:2048] = (b_ref[...].astype(F32) * sb).astype(y_ref.dtype)
        xo_ref[...] = x_ref[...] + jnp.dot(y_ref[...], w_ref[...], preferred_element_type=F32)

    row = lambda i: (i, 0)
    return pl.pallas_call(
        body, name="out_proj", grid=(s // tm,),
        in_specs=[pl.BlockSpec((tm, D_MODEL), row),
                  pl.BlockSpec((tm, 1024), lambda i: (i, GA // 1024)),
                  pl.BlockSpec((tm, 1024), lambda i: (i, GB // 1024)),
                  pl.BlockSpec((tm, 1024), row), pl.BlockSpec((tm, 1024), row),
                  pl.BlockSpec((D_MODEL, D_MODEL), lambda i: (0, 0), pipeline_mode=pl.Buffered(1))],
        out_specs=[pl.BlockSpec((tm, D_MODEL), row), pl.BlockSpec((tm, D_MODEL), row)],
        out_shape=[_sds((s, D_MODEL), F32), _sds((s, D_MODEL), MXU_DTYPE)],
        compiler_params=_params(1),
    )(x, proj, proj, swa, mla, w_out)


def _final_loss(x, tgt, g):
    s = x.shape[0]
    tm = min(512, s)

    def body(x_ref, t_ref, g_ref, dx_ref, dxb_ref, dg_ref, loss_ref):
        @pl.when(pl.program_id(0) == 0)
        def _():
            dg_ref[...] = jnp.zeros(dg_ref.shape, F32)
            loss_ref[...] = jnp.zeros(loss_ref.shape, F32)
        xh, r = _rownorm(x_ref[...])
        gg = g_ref[...]
        err = xh * gg - t_ref[...]
        per_row = jnp.mean(err * err, axis=-1, keepdims=True)
        loss_ref[...] += 0.5 * jnp.sum(per_row, axis=0, keepdims=True)
        dy = err * (1.0 / D_MODEL)
        dg_ref[...] += jnp.sum(dy * xh, axis=0, keepdims=True)
        dx = _rownorm_bwd(dy * gg, xh, r)
        dx_ref[...] = dx
        dxb_ref[...] = dx.astype(dxb_ref.dtype)

    row = lambda i: (i, 0)
    fixed = lambda i: (0, 0)
    return pl.pallas_call(
        body, name="final_loss", grid=(s // tm,),
        in_specs=[pl.BlockSpec((tm, D_MODEL), row), pl.BlockSpec((tm, D_MODEL), row),
                  pl.BlockSpec((1, D_MODEL), fixed)],
        out_specs=[pl.BlockSpec((tm, D_MODEL), row), pl.BlockSpec((tm, D_MODEL), row),
                   pl.BlockSpec((1, D_MODEL), fixed), pl.BlockSpec((8, 128), fixed)],
        out_shape=[_sds((s, D_MODEL), F32), _sds((s, D_MODEL), MXU_DTYPE), _sds((1, D_MODEL), F32),
                   _sds((8, 128), F32)],
        compiler_params=_params(1),
    )(x, tgt, g)


def _out_proj_bwd(dx, proj, swa, mla, w_out):
    s = dx.shape[0]
    tm = min(512, s)

    def body(dx_ref, ga_ref, gb_ref, a_ref, b_ref, w_ref, doa_ref, dob_ref, dg_ref, dlt_ref):
        dx = dx_ref[...].astype(MXU_DTYPE)
        dya = _mm_nt(dx, w_ref[0:1024, :])
        sa, dsa = _silu_parts(ga_ref[...].astype(F32))
        doa_ref[...] = dya * sa
        dg_ref[:, 0:1024] = (dya * a_ref[...].astype(F32) * dsa).astype(dg_ref.dtype)
        dyb = _mm_nt(dx, w_ref[1024:2048, :])
        sb, dsb = _silu_parts(gb_ref[...].astype(F32))
        b = b_ref[...].astype(F32)
        dob = dyb * sb
        dob_ref[...] = dob.astype(dob_ref.dtype)
        dg_ref[:, 1024:2048] = (dyb * b * dsb).astype(dg_ref.dtype)
        prod = dob * b
        for h in range(MLA_HEADS):
            dlt = jnp.sum(prod[:, 128 * h:128 * (h + 1)], axis=1, keepdims=True)
            dlt_ref[h] = jnp.broadcast_to(dlt, (tm, 128)).T[0:8, :]

    row = lambda i: (i, 0)
    return pl.pallas_call(
        body, name="out_proj_bwd", grid=(s // tm,),
        in_specs=[pl.BlockSpec((tm, D_MODEL), row),
                  pl.BlockSpec((tm, 1024), lambda i: (i, GA // 1024)),
                  pl.BlockSpec((tm, 1024), lambda i: (i, GB // 1024)),
                  pl.BlockSpec((tm, 1024), row), pl.BlockSpec((tm, 1024), row),
                  pl.BlockSpec((D_MODEL, D_MODEL), lambda i: (0, 0), pipeline_mode=pl.Buffered(1))],
        out_specs=[pl.BlockSpec((tm, 1024), row), pl.BlockSpec((tm, 1024), row),
                   pl.BlockSpec((tm, D_MODEL), row),
                   pl.BlockSpec((MLA_HEADS, 8, tm), lambda i: (0, 0, i))],
        out_shape=[_sds((s, 1024), F32), _sds((s, 1024), MXU_DTYPE), _sds((s, D_MODEL), MXU_DTYPE),
                   _sds((MLA_HEADS, 8, s), F32)],
        compiler_params=_params(1),
    )(dx, proj, proj, swa, mla, w_out)


def _matmul_tn(a, b, name):
    s, m = a.shape
    n = b.shape[1]
    tm, tn, tk = min(1024, m), min(1024, n), min(2048, s)
    nk = s // tk

    def body(a_ref, b_ref, o_ref, acc_ref):
        k = pl.program_id(2)

        @pl.when(k == 0)
        def _():
            acc_ref[...] = jnp.zeros(acc_ref.shape, F32)
        acc_ref[...] += _mm_tn(a_ref[...], b_ref[...])

        @pl.when(k == nk - 1)
        def _():
            o_ref[...] = acc_ref[...].astype(o_ref.dtype)

    return pl.pallas_call(
        body, name=name, grid=(m // tm, n // tn, nk),
        in_specs=[pl.BlockSpec((tk, tm), lambda i, j, k: (k, i)),
                  pl.BlockSpec((tk, tn), lambda i, j, k: (k, j))],
        out_specs=pl.BlockSpec((tm, tn), lambda i, j, k: (i, j)),
        out_shape=_sds((m, n), GRAD_DTYPE),
        scratch_shapes=[pltpu.VMEM((tm, tn), F32)],
        compiler_params=_params(3),
    )(a, b)


def _grad_w_in(h, dqa, dgate, dlat, dkva, blocks):
    s = h.shape[0]
    tm, tn, tk = 1024, 1024, min(2048, s)
    nk = s // tk
    grid = (D_MODEL // tm, NP // tn, nk)
    na = len(blocks)

    def body(*refs):
        a_ref, dqa_ref, dg8_ref, dlat_ref, dkva_ref = refs[:5]
        o_ref = refs[5 + na]
        acc_ref = refs[6 + 2 * na]
        i, j, k = pl.program_id(0), pl.program_id(1), pl.program_id(2)
        if na:
            ex = _Exchange(refs[5:5 + na], refs[6 + na:6 + 2 * na], refs[7 + 2 * na:], gather=False)

            @pl.when((i == 0) & (j == 0) & (k == 0))
            def _():
                ex.start()

        @pl.when(k == 0)
        def _():
            acc_ref[...] = jnp.zeros(acc_ref.shape, F32)

        @pl.when(j == QA // tn)
        def _():
            acc_ref[...] += _mm_tn(a_ref[...], dqa_ref[...])

        @pl.when((j == GA // tn) | (j == GB // tn))
        def _():
            acc_ref[...] += _mm_tn(a_ref[...], dg8_ref[...])

        @pl.when(j == CQ // tn)
        def _():
            acc_ref[:, 0:768] += _mm_tn(a_ref[...], dlat_ref[...])
            acc_ref[:, 768:1024] += _mm_tn(a_ref[...], dkva_ref[...])

        @pl.when(k == nk - 1)
        def _():
            o_ref[...] = acc_ref[...].astype(o_ref.dtype)

        if na:
            @pl.when((i == grid[0] - 1) & (j == grid[1] - 1) & (k == nk - 1))
            def _():
                ex.wait()

    def when(group, width):
        return pl.BlockSpec((tk, width), lambda i, j, k: (jnp.where(j == group, k, 0), 0))

    outs = pl.pallas_call(
        body, name="grad_w_in_scatter" if na else "grad_w_in", grid=grid,
        in_specs=[pl.BlockSpec((tk, tm), lambda i, j, k: (k, i)),
                  when(QA // tn, 1024),
                  pl.BlockSpec((tk, 1024), lambda i, j, k: (
                      jnp.where((j == GA // tn) | (j == GB // tn), k, 0),
                      jnp.clip(j - GA // tn, 0, 1))),
                  when(CQ // tn, 768), when(CQ // tn, 256)] + [HBM_SPEC] * na,
        out_specs=[pl.BlockSpec((tm, tn), lambda i, j, k: (i, j))] + [HBM_SPEC] * na,
        out_shape=[_sds((D_MODEL, NP), GRAD_DTYPE)] + _exchange_shapes(blocks, False),
        scratch_shapes=[pltpu.VMEM((tm, tn), F32)] + (_Exchange.semaphores(na) if na else []),
        compiler_params=_params(3),
    )(h, dqa, dgate, dlat, dkva, *blocks)
    return outs[0], list(outs[1:])


def _swa_bwd(proj, sinks, do, o):
    s = proj.shape[0]
    nb = s // BLOCK

    def body(sink_ref, q_ref, kp_ref, kc_ref, do_ref, o_ref, dq_ref, dkv_ref, dsink_ref,
             carry_ref, bias_ref):
        n = pl.program_id(0)
        _swa_fill_bias(n, bias_ref)

        @pl.when(n == 0)
        def _():
            carry_ref[...] = jnp.zeros(carry_ref.shape, F32)
            dsink_ref[...] = jnp.zeros(dsink_ref.shape, F32)

        @pl.when(n < nb)
        def _():
            k, v, left = _swa_operands(kp_ref[...].astype(F32), kc_ref[...].astype(F32))
            lane_s = lax.broadcasted_iota(jnp.int32, (8, 128), 1)
            dk, dv = [None, None], [None, None]
            dsink = jnp.zeros((8, 128), F32)
            for j in range(2):
                qs = _swa_stack(_swa_pairs(q_ref, j) * 0.125)
                do_pairs = _swa_pairs(do_ref, j)
                dos = _swa_stack(do_pairs)
                prod_t = (do_pairs * _swa_pairs(o_ref, j)).T
                dlt = jnp.concatenate([jnp.sum(prod_t[0:64], axis=0, keepdims=True),
                                       jnp.sum(prod_t[64:128], axis=0, keepdims=True)], axis=1)
                ex, es, inv = _swa_softmax(qs, k[j], bias_ref[j], _swa_sink_row(sink_ref, j))
                p = ex * inv
                ds = p * (_mm_nt(v[j], dos) - dlt)
                sink_term = es * inv * dlt
                for a in range(8):
                    dsh = -jnp.sum(sink_term[:, 128 * a:128 * (a + 1)], axis=1, keepdims=True)
                    dsink = dsink + jnp.where(lane_s == _swa_head(j, a), dsh, 0.0)
                dq = _swa_unstack(_mm_tn(ds, k[j]))
                dv[j] = _mm(p, dos)
                dk[j] = _mm(ds, qs)
                for a in range(4):
                    cols = slice(128 * (4 * j + a), 128 * (4 * j + a + 1))
                    dq_ref[:, cols] = (dq[128 * a:128 * (a + 1)] * 0.125).astype(dq_ref.dtype)

            def merge(t):
                return jnp.where(left, t[0], pltpu.roll(t[1], 64, 1))

            contrib = jnp.concatenate([merge(dk), merge(dv)], axis=1)
            dkv_ref[...] = (carry_ref[...] + contrib[0:BLOCK]).astype(dkv_ref.dtype)
            carry_ref[...] = contrib[BLOCK:2 * BLOCK]
            dsink_ref[...] += dsink

        @pl.when(n == nb)
        def _():
            dkv_ref[...] = carry_ref[...].astype(dkv_ref.dtype)

    cur = lambda n: (jnp.minimum(n, nb - 1), 0)
    return pl.pallas_call(
        body, name="swa_bwd", grid=(nb + 1,),
        in_specs=[pl.BlockSpec(memory_space=pltpu.SMEM),
                  pl.BlockSpec((BLOCK, 1024), cur),
                  pl.BlockSpec((BLOCK, 256), lambda n: (jnp.clip(n - 1, 0, nb - 1), KVA // 256)),
                  pl.BlockSpec((BLOCK, 256), lambda n: (jnp.minimum(n, nb - 1), KVA // 256)),
                  pl.BlockSpec((BLOCK, 1024), cur), pl.BlockSpec((BLOCK, 1024), cur)],
        out_specs=[pl.BlockSpec((BLOCK, 1024), cur),
                   pl.BlockSpec((BLOCK, 256), lambda n: (jnp.maximum(n - 1, 0), 0)),
                   pl.BlockSpec((8, 128), lambda n: (0, 0))],
        out_shape=[_sds((s, 1024), MXU_DTYPE), _sds((s, 256), MXU_DTYPE), _sds((8, 128), F32)],
        scratch_shapes=[pltpu.VMEM((BLOCK, 256), F32), pltpu.VMEM((2, 2 * BLOCK, SWA_STACK), F32)],
        compiler_params=_params(1),
    )(sinks, proj, proj, proj, do, o)


def _mla_bwd(qh, kh, vh, do, dlt, lse, blocks):
    s = qh.shape[1]
    t = min(MLA_TILE, s)
    nq = s // t

    na = len(blocks)

    def body(*refs):
        q_ref, k_ref, v_ref, do_ref, dlt_ref, lse_ref = refs[:6]
        dq_ref, dk_ref, dv_ref = refs[6 + na:9 + na]
        h, j = pl.program_id(0), pl.program_id(1)
        if na:
            ex = _Exchange(refs[6:6 + na], refs[9 + na:9 + 2 * na], refs[9 + 2 * na:], gather=False)

            @pl.when((h == 0) & (j == 0))
            def _():
                ex.start()

        def step(start, width, diagonal):
            rows = pl.ds(pl.multiple_of(start, t), width)
            q, k, dout = q_ref[rows, :], k_ref[...], do_ref[rows, :]
            sc = _scores_t(k, q, t, True) if diagonal else _mm_nt(k, q)
            p = jnp.exp2(sc * MLA_C2 - lse_ref[0:1, rows])
            dv = _mm(p, dout)
            ds = p * (_mm_nt(v_ref[...], dout) - dlt_ref[0:1, rows])
            dk = _mm(ds, q)
            dq = _mm_tn(ds, k)
            return rows, dq, dk, dv

        rows, dq, dk, dv = step(j * t, t, True)
        dk_ref[...] = dk
        dv_ref[...] = dv

        @pl.when(j == 0)
        def _():
            dq_ref[rows, :] = dq * MLA_SCALE

        @pl.when(j > 0)
        def _():
            dq_ref[rows, :] = (dq_ref[rows, :] + dq) * MLA_SCALE

        def above_diagonal(start, width):
            rows, dq, dk, dv = step(start, width, False)
            dk_ref[...] += dk
            dv_ref[...] += dv

            @pl.when(j == 0)
            def _():
                dq_ref[rows, :] = dq

            @pl.when(j > 0)
            def _():
                dq_ref[rows, :] += dq

        n_above = nq - 1 - j

        def quad(jj, carry):
            above_diagonal((j + 1 + 4 * jj) * t, 4 * t)
            return carry

        lax.fori_loop(0, n_above // 4, quad, 0)

        @pl.when(n_above % 4 >= 2)
        def _():
            above_diagonal((j + 1 + 4 * (n_above // 4)) * t, 2 * t)

        @pl.when(n_above % 2 == 1)
        def _():
            above_diagonal((nq - 1) * t, t)

        dk_ref[...] *= MLA_SCALE

        if na:
            @pl.when((h == MLA_HEADS - 1) & (j == nq - 1))
            def _():
                ex.wait()

    head = lambda h, j: (h, 0, 0)
    kv_map = lambda h, j: (h, j, 0)
    outs = pl.pallas_call(
        body, name="mla_bwd_scatter" if na else "mla_bwd", grid=(MLA_HEADS, nq),
        in_specs=[pl.BlockSpec((None, s, 256), head), pl.BlockSpec((None, t, 256), kv_map),
                  pl.BlockSpec((None, t, 128), kv_map),
                  pl.BlockSpec((s, 128), lambda h, j: (0, h)),
                  pl.BlockSpec((None, 8, s), head), pl.BlockSpec((None, 8, s), head)]
        + [HBM_SPEC] * na,
        out_specs=[pl.BlockSpec((None, s, 256), head),
                   pl.BlockSpec((None, t, 256), kv_map), pl.BlockSpec((None, t, 128), kv_map)]
        + [HBM_SPEC] * na,
        out_shape=[_sds((MLA_HEADS, s, 256), F32), _sds((MLA_HEADS, s, 256), F32),
                   _sds((MLA_HEADS, s, 128), F32)] + _exchange_shapes(blocks, False),
        scratch_shapes=_Exchange.semaphores(na) if na else [],
        compiler_params=_params(2),
    )(qh, kh, vh, do, dlt, lse, *blocks)
    return outs[0], outs[1], outs[2], list(outs[3:])


def _mla_prep_bwd(dqh, dkh, dvh, proj, gq, gkv, w_q, w_kv, tabs):
    s = proj.shape[0]
    tm = min(256, s)
    nm = s // tm
    c, s1, s2 = tabs

    def body(dq_ref, dk_ref, dv_ref, p_ref, gq_ref, gkv_ref, wq_ref, wkv_ref,
             c_ref, s1_ref, s2_ref, dp_ref, dwq_ref, dwkv_ref, dgq_ref, dgkv_ref,
             dqf_ref, dkvf_ref, dwq_acc, dwkv_acc):
        @pl.when(pl.program_id(0) == 0)
        def _():
            dgq_ref[...] = jnp.zeros(dgq_ref.shape, F32)
            dgkv_ref[...] = jnp.zeros(dgkv_ref.shape, F32)
            dwq_acc[...] = jnp.zeros(dwq_acc.shape, F32)
            dwkv_acc[...] = jnp.zeros(dwkv_acc.shape, F32)
        cc, ns1, ns2 = c_ref[...], -s1_ref[...], -s2_ref[...]
        dkr = jnp.zeros((tm, 128), F32)
        for h in range(MLA_HEADS):
            dqf_ref[:, 128 * h:128 * (h + 1)] = dq_ref[h, :, 0:128].astype(dqf_ref.dtype)
            dqf_ref[:, 1024 + 128 * h:1024 + 128 * (h + 1)] = _rope(
                dq_ref[h, :, 128:256], cc, ns1, ns2).astype(dqf_ref.dtype)
            dkvf_ref[:, 128 * h:128 * (h + 1)] = dk_ref[h, :, 0:128].astype(dkvf_ref.dtype)
            dkvf_ref[:, 1024 + 128 * h:1024 + 128 * (h + 1)] = dv_ref[h].astype(dkvf_ref.dtype)
            dkr = dkr + dk_ref[h, :, 128:256]
        dcqn = _mm_nt(dqf_ref[...], wq_ref[...])
        dckvn = _mm_nt(dkvf_ref[...], wkv_ref[...])
        cqh, rq = _rownorm(p_ref[:, 0:384].astype(F32))
        ckvh, rkv = _rownorm(p_ref[:, 384:640].astype(F32))
        dgq_ref[...] += jnp.sum(dcqn * cqh, axis=0, keepdims=True)
        dgkv_ref[...] += jnp.sum(dckvn * ckvh, axis=0, keepdims=True)
        dp_ref[:, 0:384] = _rownorm_bwd(dcqn * gq_ref[...], cqh, rq).astype(dp_ref.dtype)
        dp_ref[:, 384:640] = _rownorm_bwd(dckvn * gkv_ref[...], ckvh, rkv).astype(dp_ref.dtype)
        dp_ref[:, 640:768] = _rope(dkr, cc, ns1, ns2).astype(dp_ref.dtype)
        dwq_acc[...] += _mm_tn(cqh * gq_ref[...], dqf_ref[...])
        dwkv_acc[...] += _mm_tn(ckvh * gkv_ref[...], dkvf_ref[...])

        @pl.when(pl.program_id(0) == nm - 1)
        def _():
            dwq_ref[...] = dwq_acc[...].astype(dwq_ref.dtype)
            dwkv_ref[...] = dwkv_acc[...].astype(dwkv_ref.dtype)

    row = lambda i: (i, 0)
    fixed = lambda i: (0, 0)
    head = lambda i: (0, i, 0)
    return pl.pallas_call(
        body, name="mla_prep_bwd", grid=(nm,),
        in_specs=[pl.BlockSpec((MLA_HEADS, tm, 256), head), pl.BlockSpec((MLA_HEADS, tm, 256), head),
                  pl.BlockSpec((MLA_HEADS, tm, 128), head),
                  pl.BlockSpec((tm, 768), lambda i: (i, CQ // 768)),
                  pl.BlockSpec((1, Q_RANK), fixed), pl.BlockSpec((1, KV_RANK), fixed),
                  pl.BlockSpec((Q_RANK, 2048), fixed), pl.BlockSpec((KV_RANK, 2048), fixed),
                  pl.BlockSpec((tm, 128), row), pl.BlockSpec((tm, 128), row),
                  pl.BlockSpec((tm, 128), row)],
        out_specs=[pl.BlockSpec((tm, 768), row), pl.BlockSpec((Q_RANK, 2048), fixed),
                   pl.BlockSpec((KV_RANK, 2048), fixed),
                   pl.BlockSpec((1, Q_RANK), fixed), pl.BlockSpec((1, KV_RANK), fixed)],
        out_shape=[_sds((s, 768), MXU_DTYPE), _sds((Q_RANK, 2048), GRAD_DTYPE),
                   _sds((KV_RANK, 2048), GRAD_DTYPE),
                   _sds((1, Q_RANK), F32), _sds((1, KV_RANK), F32)],
        scratch_shapes=[pltpu.VMEM((tm, 2048), MXU_DTYPE), pltpu.VMEM((tm, 2048), MXU_DTYPE),
                        pltpu.VMEM((Q_RANK, 2048), F32), pltpu.VMEM((KV_RANK, 2048), F32)],
        compiler_params=_params(1),
    )(dqh, dkh, dvh, proj, gq, gkv, w_q, w_kv, c, s1, s2)


def _in_proj_bwd(dqa, dgate, dlat, dkva, w, x, dx_out, g, blocks):
    s = x.shape[0]
    tm = min(256, s)
    nm = s // tm
    na = len(blocks)

    def body(*refs):
        dqa_ref, dg8_ref, dlat_ref, dkva_ref, w_ref, x_ref, dxo_ref, g_ref = refs[:8]
        dx_ref, dxb_ref, dg_ref = refs[8 + na:11 + na]
        if na:
            ex = _Exchange(refs[8:8 + na], refs[11 + na:11 + 2 * na], refs[11 + 2 * na:], gather=False)

        @pl.when(pl.program_id(0) == 0)
        def _():
            dg_ref[...] = jnp.zeros(dg_ref.shape, F32)
            if na:
                ex.start()

        dh = (_mm_nt(dqa_ref[...], w_ref[:, QA:QA + 1024])
              + _mm_nt(dg8_ref[:, 0:1024], w_ref[:, GA:GA + 1024])
              + _mm_nt(dg8_ref[:, 1024:2048], w_ref[:, GB:GB + 1024])
              + _mm_nt(dlat_ref[...], w_ref[:, CQ:CQ + 768])
              + _mm_nt(dkva_ref[...], w_ref[:, KVA:KVA + 256]))
        xh, r = _rownorm(x_ref[...])
        dg_ref[...] += jnp.sum(dh * xh, axis=0, keepdims=True)
        dx = dxo_ref[...] + _rownorm_bwd(dh * g_ref[...], xh, r)
        dx_ref[...] = dx
        dxb_ref[...] = dx.astype(dxb_ref.dtype)

        if na:
            @pl.when(pl.program_id(0) == nm - 1)
            def _():
                ex.wait()

    row = lambda i: (i, 0)
    fixed = lambda i: (0, 0)
    outs = pl.pallas_call(
        body, name="in_proj_bwd_scatter" if na else "in_proj_bwd", grid=(nm,),
        in_specs=[pl.BlockSpec((tm, 1024), row), pl.BlockSpec((tm, 2048), row),
                  pl.BlockSpec((tm, 768), row), pl.BlockSpec((tm, 256), row),
                  pl.BlockSpec((D_MODEL, NP), fixed, pipeline_mode=pl.Buffered(1)),
                  pl.BlockSpec((tm, D_MODEL), row), pl.BlockSpec((tm, D_MODEL), row),
                  pl.BlockSpec((1, D_MODEL), fixed)] + [HBM_SPEC] * na,
        out_specs=[pl.BlockSpec((tm, D_MODEL), row), pl.BlockSpec((tm, D_MODEL), row),
                   pl.BlockSpec((1, D_MODEL), fixed)] + [HBM_SPEC] * na,
        out_shape=[_sds((s, D_MODEL), F32), _sds((s, D_MODEL), MXU_DTYPE), _sds((1, D_MODEL), F32)]
        + _exchange_shapes(blocks, False),
        scratch_shapes=_Exchange.semaphores(na) if na else [],
        compiler_params=_params(1),
    )(dqa, dgate, dlat, dkva, w, x, dx_out, g, *blocks)
    return outs[0], outs[1], outs[2], list(outs[3:])


def _adamw_update(g, w_ref, m_ref, v_ref, g_ref, d_ref, nm_ref, nv_ref):
    m2 = ADAM_B1 * m_ref[...] + (1.0 - ADAM_B1) * g
    v2 = ADAM_B2 * v_ref[...] + (1.0 - ADAM_B2) * (g * g)
    m_hat = m2 / (1.0 - ADAM_B1 ** ADAM_STEP)
    v_hat = v2 / (1.0 - ADAM_B2 ** ADAM_STEP)
    g_ref[...] = g
    d_ref[...] = -ADAM_LR * (m_hat / (jnp.sqrt(v_hat) + ADAM_EPS) + ADAM_WD * w_ref[...])
    nm_ref[...] = m2
    nv_ref[...] = v2


def _reduce_adamw_in(parts, w_t, m_t, v_t, name):
    n_layers = len(parts)
    tc = 512
    nc = D_MODEL // tc

    def body(*refs):
        p_refs = refs[:n_layers]
        w_ref, m_ref, v_ref, g_ref, d_ref, nm_ref, nv_ref = refs[n_layers:]
        layer = pl.program_id(0)
        for l in range(n_layers):
            @pl.when(layer == l)
            def _(l=l):
                g = p_refs[l][0].astype(F32)
                for k in range(1, N_DEV):
                    g = g + p_refs[l][k].astype(F32)
                _adamw_update(g[0:SHARD_COLS], w_ref, m_ref, v_ref, g_ref, d_ref, nm_ref, nv_ref)

    def part_spec(l):
        return pl.BlockSpec((N_DEV, SHARD_PAD, tc),
                            lambda layer, j: (0, 0, jnp.where(layer == l, j, 0)))

    blk = pl.BlockSpec((None, SHARD_COLS, tc), lambda layer, j: (layer, 0, j))
    return pl.pallas_call(
        body, name=name, grid=(n_layers, nc),
        in_specs=[part_spec(l) for l in range(n_layers)] + [blk, blk, blk],
        out_specs=[blk, blk, blk, blk],
        out_shape=[_sds((n_layers, SHARD_COLS, D_MODEL), F32)] * 4,
        compiler_params=_params(2),
    )(*parts, w_t, m_t, v_t)


def _reduce_adamw(parts, w, m, v, name):
    n_layers = len(parts)
    rows, part_cols = parts[0].shape[1:]
    cols = w.shape[-1]
    lanes = -(-cols // 128) * 128
    tr = rows
    for cand in (1024, 512, 256, 128, 64, 32, 16, 8):
        if rows % cand == 0 and N_DEV * cand * lanes * 4 <= 8 * 1024 * 1024:
            tr = cand
            break
    nr = rows // tr

    def body(*refs):
        p_refs = refs[:n_layers]
        w_ref, m_ref, v_ref, g_ref, d_ref, nm_ref, nv_ref = refs[n_layers:]
        layer = pl.program_id(0)
        for l in range(n_layers):
            @pl.when(layer == l)
            def _(l=l):
                g = p_refs[l][0, :, 0:cols].astype(F32)
                for k in range(1, N_DEV):
                    g = g + p_refs[l][k, :, 0:cols].astype(F32)
                _adamw_update(g, w_ref, m_ref, v_ref, g_ref, d_ref, nm_ref, nv_ref)

    def part_spec(l):
        return pl.BlockSpec((N_DEV, tr, part_cols),
                            lambda layer, i: (0, jnp.where(layer == l, i, 0), 0))

    blk = pl.BlockSpec((tr, cols), lambda layer, i: (layer * nr + i, 0))
    return pl.pallas_call(
        body, name=name, grid=(n_layers, nr),
        in_specs=[part_spec(l) for l in range(n_layers)] + [blk, blk, blk],
        out_specs=[blk, blk, blk, blk],
        out_shape=[_sds((n_layers * rows, cols), F32)] * 4,
        compiler_params=_params(2),
    )(*parts, w, m, v)


def _position():
    x, y, c = lax.axis_index("x"), lax.axis_index("y"), lax.axis_index("c")
    return x, y, c


def _index(px, py, pc):
    return 4 * px + 2 * py + pc


HBM_SPEC = pl.BlockSpec(memory_space=pltpu.HBM)
IN_BLOCKED = (D_MODEL, N_DEV * SHARD_PAD)
IN_BLOCKED_T = (N_DEV * SHARD_PAD, D_MODEL)


def _block(ref, idx):
    if tuple(ref.shape) == IN_BLOCKED:
        return ref.at[:, pl.ds(pl.multiple_of(idx * SHARD_PAD, SHARD_PAD), SHARD_PAD)]
    if tuple(ref.shape) == IN_BLOCKED_T:
        return ref.at[pl.ds(pl.multiple_of(idx * SHARD_PAD, SHARD_PAD), SHARD_PAD)]
    return ref.at[idx]


class _Gather:
    def __init__(self, srcs, dsts, sems):
        self.srcs, self.dsts = srcs, dsts
        self.send_sems, self.recv_sems, self.local_sems = sems
        x, y, c = _position()
        self.c = c
        self.me, self.sibling = (x, y, c), (x, y, 1 - c)
        self.chips = [(1 - x, y), (x, 1 - y), (1 - x, 1 - y)]

    def _copy(self, a, k, block, to, own=False):
        slot = _block(self.dsts[a], _index(*block))
        return pltpu.make_async_remote_copy(
            src_ref=self.srcs[a] if own else slot, dst_ref=slot,
            send_sem=self.send_sems.at[7 * a + k], recv_sem=self.recv_sems.at[7 * a + k],
            device_id=to, device_id_type=MESH)

    def _local(self, a):
        return pltpu.make_async_copy(self.srcs[a], _block(self.dsts[a], _index(*self.me)),
                                     self.local_sems.at[a])

    def _first(self, a):
        return [self._copy(a, 0, self.me, self.sibling, own=True)] + [
            self._copy(a, 1 + j, self.me, (*chip, self.c), own=True)
            for j, chip in enumerate(self.chips)]

    def _passed(self, a, j):
        return self._copy(a, 4 + j, (*self.chips[j], self.c), self.sibling)

    def start(self):
        for a in range(len(self.srcs)):
            self._local(a).start()
            for cp in self._first(a):
                cp.start()

    def forward(self):
        for j, chip in enumerate(self.chips):
            for a in range(len(self.srcs)):
                self._copy(a, 1 + j, (*chip, self.c), self.me).wait_recv()
                self._passed(a, j).start()

    def finish(self):
        for a in range(len(self.srcs)):
            self._copy(a, 0, self.sibling, self.me).wait_recv()
            for j, chip in enumerate(self.chips):
                self._copy(a, 4 + j, (*chip, 1 - self.c), self.me).wait_recv()
            for cp in self._first(a):
                cp.wait_send()
            for j in range(3):
                self._passed(a, j).wait_send()
            self._local(a).wait()


def _all_gather(shards, name):
    na = len(shards)

    def body(*refs):
        g = _Gather(refs[:na], refs[na:2 * na], refs[2 * na:])
        g.start()
        g.forward()
        g.finish()

    return pl.pallas_call(
        body, name=name,
        in_specs=[HBM_SPEC] * na, out_specs=[HBM_SPEC] * na,
        out_shape=_exchange_shapes(shards, True),
        scratch_shapes=_Exchange.semaphores(na),
    )(*shards)


class _Exchange:
    def __init__(self, srcs, dsts, sems, gather):
        self.srcs, self.dsts, self.gather = srcs, dsts, gather
        self.send_sems, self.recv_sems, self.local_sems = sems
        x, y, c = _position()
        self.me = _index(x, y, c)
        self.peers = [(x ^ ((k >> 2) & 1), y ^ ((k >> 1) & 1), c ^ (k & 1)) for k in range(1, N_DEV)]

    @staticmethod
    def semaphores(na):
        return [pltpu.SemaphoreType.DMA((7 * na,)), pltpu.SemaphoreType.DMA((7 * na,)),
                pltpu.SemaphoreType.DMA((na,))]

    def _src(self, a, slot):
        return self.srcs[a] if self.gather else _block(self.srcs[a], slot)

    def _local(self, a):
        return pltpu.make_async_copy(self._src(a, self.me), self.dsts[a].at[self.me],
                                     self.local_sems.at[a])

    def _send(self, a, k):
        peer = self.peers[k]
        return pltpu.make_async_remote_copy(
            src_ref=self._src(a, _index(*peer)), dst_ref=self.dsts[a].at[self.me],
            send_sem=self.send_sems.at[7 * a + k], recv_sem=self.recv_sems.at[7 * a + k],
            device_id=peer, device_id_type=MESH)

    def _arrival(self, a, k):
        landed = self.dsts[a].at[_index(*self.peers[k])]
        return pltpu.make_async_remote_copy(
            src_ref=landed, dst_ref=landed,
            send_sem=self.send_sems.at[7 * a + k], recv_sem=self.recv_sems.at[7 * a + k],
            device_id=self.peers[k], device_id_type=MESH)

    def start(self):
        for a in range(len(self.srcs)):
            self._local(a).start()
            for k in range(N_DEV - 1):
                self._send(a, k).start()

    def wait(self):
        for a in range(len(self.srcs)):
            for k in range(N_DEV - 1):
                self._arrival(a, k).wait_recv()
            for k in range(N_DEV - 1):
                self._send(a, k).wait_send()
            self._local(a).wait()


def _exchange_shapes(arrays, gather):
    def shape(a):
        if gather:
            return IN_BLOCKED if a.shape == (D_MODEL, SHARD_PAD) else (N_DEV,) + a.shape
        return (N_DEV, SHARD_PAD, D_MODEL) if a.shape == IN_BLOCKED_T else a.shape
    return [_sds(shape(a), a.dtype) for a in arrays]


def _exchange_call(arrays, gather, name):
    na = len(arrays)

    def body(*refs):
        ex = _Exchange(refs[:na], refs[na:2 * na], refs[2 * na:], gather)
        ex.start()
        ex.wait()

    return pl.pallas_call(
        body, name=name,
        in_specs=[HBM_SPEC] * na, out_specs=[HBM_SPEC] * na,
        out_shape=_exchange_shapes(arrays, gather),
        scratch_shapes=_Exchange.semaphores(na),
    )(*arrays)


def _layer_fwd(x, small, w_in, g_rest, rest_shards, tabs, next_shards):
    attn_g, sinks, gq, gkv = small
    proj, h, gathered_rest = _in_proj(x, attn_g, w_in, rest_shards)
    w_q, w_kv, w_o = _rest_from_gathered(*(gathered_rest if rest_shards else g_rest))
    swa = _swa_fwd(proj, sinks)
    qh, kh, vh, vth = _mla_prep(proj, gq, gkv, w_q, w_kv, tabs)
    mla, lse, gathered = _mla_fwd(qh, kh, vth, next_shards)
    x_next, y = _out_proj(x, proj, swa, mla, w_o)
    return x_next, (x, proj, h, swa, qh, kh, vh, mla, lse, y), (w_in, w_q, w_kv, w_o), gathered


def _layer_bwd(dx, dxb, saved, small, weights, tabs, pending, send_own):
    attn_g, sinks, gq, gkv = small
    w_in, w_q, w_kv, w_o = weights
    x, proj, h, swa, qh, kh, vh, mla, lse, y = saved
    d_o = _matmul_tn(y, dxb, "grad_w_out")
    do_a, do_b, dgate, dlt = _out_proj_bwd(dxb, proj, swa, mla, w_o)
    dqa, dkva, dsk = _swa_bwd(proj, sinks, do_a, swa)
    dqh, dkh, dvh, received = _mla_bwd(qh, kh, vh, do_b, dlt, lse, pending)
    dlat, d_q, d_kv, d_gq, d_gkv = _mla_prep_bwd(dqh, dkh, dvh, proj, gq, gkv, w_q, w_kv, tabs)
    rest_blocks = _rest_grad_blocks(d_q, d_kv, d_o)
    d_inp, got_rest = _grad_w_in(h, dqa, dgate, dlat, dkva, rest_blocks if send_own else [])
    in_block = _in_grad_blocks(d_inp)
    dx, dxb, d_attn, got_in = _in_proj_bwd(dqa, dgate, dlat, dkva, w_in, x, dx, attn_g,
                                           [in_block] if send_own else [])
    small_grads = (d_attn, dsk[0:1, 0:SWA_HEADS], d_gq, d_gkv)
    return dx, dxb, small_grads, [in_block] + rest_blocks, received, got_in + got_rest


def _pack_small_grads(small_grads, d_final, loss):
    d_attn, d_sink, d_gq, d_gkv = zip(*small_grads)
    return jnp.concatenate([
        jnp.concatenate(d_attn, axis=0).reshape(64, 128),
        jnp.concatenate(d_gq, axis=0).reshape(12, 128),
        jnp.concatenate(d_gkv, axis=0).reshape(8, 128),
        d_final.reshape(16, 128),
        jnp.pad(jnp.concatenate(d_sink, axis=1), ((0, 0), (0, 64))),
        loss[0:1],
        jnp.zeros((PACK_ROWS - ROW_LOSS - 1, 128), F32)], axis=0)


def _pack_small(attn, qa, kva, final, sinks):
    return jnp.concatenate([
        attn.reshape(64, 128), qa.reshape(12, 128), kva.reshape(8, 128), final.reshape(16, 128),
        jnp.pad(sinks.reshape(1, 64), ((0, 0), (0, 64))),
        jnp.zeros((PACK_ROWS - ROW_SINK - 1, 128), F32)], axis=0)


def _unpack_small(p):
    return (p[ROW_ATTN:ROW_QA].reshape(DEPTH, D_MODEL), p[ROW_SINK, 0:64].reshape(DEPTH, SWA_HEADS),
            p[ROW_QA:ROW_KVA].reshape(DEPTH, Q_RANK), p[ROW_KVA:ROW_FINAL].reshape(DEPTH, KV_RANK),
            p[ROW_FINAL:ROW_SINK].reshape(D_MODEL))


def kernel(x, attn_norm_g, w_in, swa_sinks, q_a_norm_g, kv_a_norm_g, w_q_b, w_kv_b, w_out, final_norm_g, loss_target, m_attn_norm_g, m_w_in, m_swa_sinks, m_q_a_norm_g, m_kv_a_norm_g, m_w_q_b, m_w_kv_b, m_w_out, m_final_norm_g, v_attn_norm_g, v_w_in, v_swa_sinks, v_q_a_norm_g, v_kv_a_norm_g, v_w_q_b, v_w_kv_b, v_w_out, v_final_norm_g):
    xs, tgt = x[0], loss_target[0]
    tabs = _rope_tables(xs.shape[0])
    shards = [w.astype(MXU_DTYPE) for w in (w_in, w_q_b, w_kv_b, w_out)]
    shards[0] = jnp.pad(shards[0], ((0, 0), (0, 0), (0, SHARD_PAD - SHARD_COLS)))
    layer_shards = lambda l: [w[l] for w in shards]
    smalls = [(attn_norm_g[l:l + 1], swa_sinks[l], q_a_norm_g[l:l + 1], kv_a_norm_g[l:l + 1])
              for l in range(DEPTH)]

    gathered = list(_all_gather(layer_shards(0)[:1], "gather_weights")) + [None] * 3
    weights, saved = [None] * DEPTH, []
    for l in range(DEPTH):
        next_shards = layer_shards(l + 1) if l + 1 < DEPTH else []
        rest_shards = layer_shards(0)[1:] if l == 0 else []
        xs, acts, weights[l], gathered = _layer_fwd(
            xs, smalls[l], _w_in_from_gathered(gathered[0]), gathered[1:], rest_shards, tabs,
            next_shards)
        saved.append(acts)
    dx, dxb, d_final, loss = _final_loss(xs, tgt, final_norm_g.reshape(1, D_MODEL))

    received, small_grads, pending = [None] * DEPTH, [None] * DEPTH, []
    for l in reversed(range(DEPTH)):
        dx, dxb, small_grads[l], blocks, arrived, arrived_own = _layer_bwd(
            dx, dxb, saved[l], smalls[l], weights[l], tabs, pending, send_own=(l == 0))
        if pending:
            received[l + 1] = arrived
        pending = blocks
    received[0] = arrived_own
    small = _exchange_call([_pack_small_grads(small_grads, d_final, loss)], True, "gather_small")[0]

    big = []
    for a, (w, m, v, name) in enumerate(zip((w_in, w_q_b, w_kv_b, w_out),
                                            (m_w_in, m_w_q_b, m_w_kv_b, m_w_out),
                                            (v_w_in, v_w_q_b, v_w_kv_b, v_w_out),
                                            ("adamw_w_in", "adamw_w_q_b", "adamw_w_kv_b",
                                             "adamw_w_out"))):
        parts = [received[l][a] for l in range(DEPTH)]
        if a == 0:
            swap = lambda t: t.transpose(0, 2, 1)
            outs = _reduce_adamw_in(parts, swap(w), swap(m), swap(v), name)
            big.append([swap(t) for t in outs])
        else:
            cols = w.shape[-1]
            flat = lambda t: t.reshape(-1, cols)
            outs = _reduce_adamw(parts, flat(w), flat(m), flat(v), name)
            big.append([t.reshape(w.shape) for t in outs])

    sm = _reduce_adamw(
        [small],
        _pack_small(attn_norm_g, q_a_norm_g, kv_a_norm_g, final_norm_g, swa_sinks),
        _pack_small(m_attn_norm_g, m_q_a_norm_g, m_kv_a_norm_g, m_final_norm_g, m_swa_sinks),
        _pack_small(v_attn_norm_g, v_q_a_norm_g, v_kv_a_norm_g, v_final_norm_g, v_swa_sinks),
        "adamw_small")
    loss = sm[0][ROW_LOSS, 0]
    kinds = []
    for t in range(4):
        attn, sinks, qa, kva, final = _unpack_small(sm[t])
        b_in, b_q, b_kv, b_o = (big[i][t] for i in range(4))
        kinds.append((attn, b_in, sinks, qa, kva, b_q, b_kv, b_o, final))
    return (loss, dx[None], *kinds[0], *kinds[1], *kinds[2], *kinds[3])
```

```python
import functools

import jax
import jax.numpy as jnp
import numpy as np
from jax import lax
from jax.experimental import pallas as pl
from jax.experimental.pallas import tpu as pltpu

F32 = jnp.float32
BF16 = jnp.bfloat16
MXU_DTYPE = BF16
GRAD_DTYPE = BF16
PROJ_DTYPE = BF16
ATTN_DTYPE = BF16

D_MODEL = 2048
DEPTH = 4
EPS = 1e-6
NEG = -1e30
BLOCK = 128
SWA_HEADS = 16
MLA_HEADS = 8
Q_RANK = 384
KV_RANK = 256
MLA_SCALE = 192 ** -0.5
MLA_C2 = MLA_SCALE * 1.4426950408889634
ROPE_THETA = 10000.0
IN_WIDTH = 4032

ADAM_LR = 0.001
ADAM_B1 = 0.9
ADAM_B2 = 0.999
ADAM_EPS = 1e-08
ADAM_WD = 0.01
ADAM_STEP = 10

N_DEV = 8
MESH = pl.DeviceIdType.MESH

NP = 4096
QA, GA, GB, CQ, CKV, KR, KVA = 0, 1024, 2048, 3072, 3456, 3712, 3840

ROW_ATTN, ROW_QA, ROW_KVA, ROW_FINAL, ROW_SINK, ROW_LOSS, PACK_ROWS = 0, 64, 76, 84, 100, 101, 104

VMEM_LIMIT = 56 * 1024 * 1024
MLA_TILE = 512
GATHER_FORWARD_HEAD = 7


def _sds(shape, dtype):
    return jax.ShapeDtypeStruct(shape, dtype)


def _params(n_axes):
    return pltpu.CompilerParams(dimension_semantics=("arbitrary",) * n_axes,
                                vmem_limit_bytes=VMEM_LIMIT)


def _mm(a, b):
    return jnp.dot(a.astype(MXU_DTYPE), b.astype(MXU_DTYPE), preferred_element_type=F32)


def _mm_nt(a, b):
    return lax.dot_general(a.astype(MXU_DTYPE), b.astype(MXU_DTYPE),
                           (((1,), (1,)), ((), ())), preferred_element_type=F32)


def _mm_tn(a, b):
    return lax.dot_general(a.astype(MXU_DTYPE), b.astype(MXU_DTYPE),
                           (((0,), (0,)), ((), ())), preferred_element_type=F32)


def _rownorm(x):
    r = lax.rsqrt(jnp.mean(x * x, axis=-1, keepdims=True) + EPS)
    return x * r, r


def _rownorm_bwd(dxh, xh, r):
    return r * (dxh - xh * jnp.mean(dxh * xh, axis=-1, keepdims=True))


def _rope(t, c, s1, s2):
    return t * c + pltpu.roll(t, 32, 1) * s1 + pltpu.roll(t, 96, 1) * s2


SHARD_COLS = IN_WIDTH // N_DEV
SHARD_PAD = 512


def _orig_col_of_padded():
    o = np.full((NP,), -1, np.int64)
    for start, width, orig in ((QA, 1024, 0), (KVA, 256, 1024), (GA, 1024, 1280), (CQ, 384, 2304),
                               (CKV, 256, 2688), (KR, 64, 2944), (GB, 1024, 3008)):
        o[start:start + width] = np.arange(orig, orig + width)
    return o


def _device_major_src():
    o = _orig_col_of_padded()
    return np.where(o >= 0, o + (SHARD_PAD - SHARD_COLS) * (o // SHARD_COLS), -1)


def _kernel_layout_src():
    o = _orig_col_of_padded()
    where = np.full((IN_WIDTH,), -1, np.int64)
    where[o[o >= 0]] = np.nonzero(o >= 0)[0]
    e = np.arange(N_DEV * SHARD_PAD)
    k, c = e // SHARD_PAD, e % SHARD_PAD
    return np.where(c < SHARD_COLS, where[np.minimum(SHARD_COLS * k + c, IN_WIDTH - 1)], -1)


def _permute_columns(x, src_of, name, transposed=False):
    rows, n_in = x.shape
    n_out = len(src_of)
    plan, mats = [], []
    for t in range(n_out // 128):
        srcs = src_of[128 * t:128 * (t + 1)]
        entry = []
        for u in sorted(set(int(s) // 128 for s in srcs if s >= 0)):
            m = np.zeros((128, 128), np.float32)
            for c, s in enumerate(srcs):
                if s >= 0 and s // 128 == u:
                    m[s % 128, c] = 1.0
            entry.append((u, len(mats)))
            mats.append(m)
        plan.append(entry)
    tr = min(512, rows)

    def body(x_ref, p_ref, o_ref):
        for t, entry in enumerate(plan):
            acc = jnp.zeros((tr, 128), F32)
            for u, idx in entry:
                acc = acc + jnp.dot(x_ref[:, 128 * u:128 * (u + 1)], p_ref[idx],
                                    preferred_element_type=F32)
            if transposed:
                o_ref[128 * t:128 * (t + 1), :] = acc.T.astype(o_ref.dtype)
            else:
                o_ref[:, 128 * t:128 * (t + 1)] = acc.astype(o_ref.dtype)

    table = jnp.asarray(np.stack(mats), x.dtype)
    return pl.pallas_call(
        body, name=name, grid=(rows // tr,),
        in_specs=[pl.BlockSpec((tr, n_in), lambda i: (i, 0)),
                  pl.BlockSpec(table.shape, lambda i: (0, 0, 0))],
        out_specs=(pl.BlockSpec((n_out, tr), lambda i: (0, i)) if transposed
                   else pl.BlockSpec((tr, n_out), lambda i: (i, 0))),
        out_shape=_sds((n_out, rows) if transposed else (rows, n_out), x.dtype),
        compiler_params=_params(1),
    )(x, table)


def _w_in_from_gathered(g_in):
    return _permute_columns(g_in, _device_major_src(), "w_in_layout")


def _rest_from_gathered(g_qb, g_kvb, g_out):
    qb = g_qb.transpose(1, 0, 2)
    rope = jnp.pad(qb[..., 128:], ((0, 0), (0, 0), (0, 64)))
    w_q = jnp.concatenate([qb[..., :128].reshape(Q_RANK, 1024),
                           rope.reshape(Q_RANK, 1024)], axis=-1)
    kvb = g_kvb.transpose(1, 0, 2)
    w_kv = jnp.concatenate([kvb[..., :128].reshape(KV_RANK, 1024),
                            kvb[..., 128:].reshape(KV_RANK, 1024)], axis=-1)
    w_o = g_out.reshape(D_MODEL, D_MODEL)
    return w_q, w_kv, w_o


def _in_grad_blocks(d_inp):
    return _permute_columns(d_inp, _kernel_layout_src(), "grad_w_in_layout", transposed=True)


def _qkv_grad_blocks(d_q, d_kv):
    qn = d_q[:, :1024].reshape(Q_RANK, 8, 128)
    qr = d_q[:, 1024:].reshape(Q_RANK, 8, 128)[..., :64]
    b_q = jnp.concatenate([qn, qr], axis=-1).transpose(1, 0, 2)
    kn = d_kv[:, :1024].reshape(KV_RANK, 8, 128)
    vv = d_kv[:, 1024:].reshape(KV_RANK, 8, 128)
    b_kv = jnp.concatenate([kn, vv], axis=-1).transpose(1, 0, 2)
    return [b_q, b_kv]


def _rope_tables(s):
    pos = jnp.arange(s, dtype=F32)
    inv_freq = ROPE_THETA ** (-jnp.arange(0, 64, 2, dtype=F32) / 64)
    ang = pos[:, None] * inv_freq[None, :]
    cos, sin = jnp.cos(ang), jnp.sin(ang)
    z32 = jnp.zeros((s, 32), F32)
    z64 = jnp.zeros((s, 64), F32)
    c = jnp.concatenate([cos, cos, z64], axis=1)
    s1 = jnp.concatenate([z32, sin, z64], axis=1)
    s2 = jnp.concatenate([-sin, z32, z64], axis=1)
    return c, s1, s2


def _in_proj(x, g, w, shards):
    s = x.shape[0]
    tm, tn = min(512, s), 1024
    nm = s // tm
    na = len(shards)

    def body(*refs):
        x_ref, g_ref, w_ref = refs[:3]
        o_ref, h_ref = refs[3 + na:5 + na]
        i = pl.program_id(0)
        if na:
            ex = _Gather(refs[3:3 + na], refs[5 + na:5 + 2 * na], refs[5 + 2 * na:])

            @pl.when(i == 0)
            def _():
                ex.start()

            @pl.when(i == (3 * nm) // 4)
            def _():
                ex.forward()

        xh, _ = _rownorm(x_ref[...])
        h_ref[...] = (xh * g_ref[...]).astype(h_ref.dtype)
        for j in range(NP // tn):
            cols = slice(j * tn, (j + 1) * tn)
            o_ref[:, cols] = jnp.dot(h_ref[...], w_ref[:, cols],
                                     preferred_element_type=F32).astype(o_ref.dtype)

        if na:
            @pl.when(i == nm - 1)
            def _():
                ex.finish()

    row = lambda i: (i, 0)
    fixed = lambda i: (0, 0)
    outs = pl.pallas_call(
        body, name="in_proj_gather" if na else "in_proj", grid=(nm,),
        in_specs=[pl.BlockSpec((tm, D_MODEL), row), pl.BlockSpec((1, D_MODEL), fixed),
                  pl.BlockSpec((D_MODEL, NP), fixed, pipeline_mode=pl.Buffered(1))]
        + [HBM_SPEC] * na,
        out_specs=[pl.BlockSpec((tm, NP), row), pl.BlockSpec((tm, D_MODEL), row)]
        + [HBM_SPEC] * na,
        out_shape=[_sds((s, NP), PROJ_DTYPE), _sds((s, D_MODEL), MXU_DTYPE)]
        + _exchange_shapes(shards, True),
        scratch_shapes=_Exchange.semaphores(na) if na else [],
        compiler_params=_params(1),
    )(x, g, w, *shards)
    return outs[0], outs[1], list(outs[2:])


def _swa_slopes():
    return [2.0 ** (-8.0 * (h + 1) / SWA_HEADS) for h in range(SWA_HEADS)]


SWA_STACK = 8 * BLOCK


def _swa_head(j, a):
    return 2 * (4 * j + a % 4) + a // 4


def _swa_operands(kv_p, kv_c):
    kk = jnp.concatenate([kv_p[:, :128], kv_c[:, :128]], axis=0)
    vv = jnp.concatenate([kv_p[:, 128:], kv_c[:, 128:]], axis=0)
    left = lax.broadcasted_iota(jnp.int32, (2 * BLOCK, 128), 1) < 64

    def heads(t):
        return [jnp.where(left, t, 0.0), jnp.where(left, pltpu.roll(t, 64, 1), 0.0)]

    return heads(kk), heads(vv), left


def _swa_by_block(vals):
    a = lax.broadcasted_iota(jnp.int32, (1, SWA_STACK), 1) >> 7
    row = jnp.full((1, SWA_STACK), vals[7], F32)
    for t in range(6, -1, -1):
        row = jnp.where(a == t, vals[t], row)
    return row


def _swa_bias(n, j):
    slopes = _swa_slopes()
    ki = lax.broadcasted_iota(jnp.int32, (2 * BLOCK, SWA_STACK), 0)
    r = lax.broadcasted_iota(jnp.int32, (2 * BLOCK, SWA_STACK), 1)
    delta = BLOCK + (r & (BLOCK - 1)) - ki
    valid = (delta >= 0) & (delta < BLOCK) & ((n - 1) * BLOCK + ki >= 0)
    slope = _swa_by_block([slopes[_swa_head(j, a)] for a in range(8)])
    return jnp.where(valid, -slope * delta.astype(F32), NEG)


def _swa_fill_bias(n, bias_ref):
    @pl.when(n <= 1)
    def _():
        for j in range(2):
            bias_ref[j] = _swa_bias(n, j)


def _swa_sink_row(sink_ref, j):
    return _swa_by_block([sink_ref[_swa_head(j, a)] for a in range(8)])


def _swa_pairs(ref, j):
    return jnp.concatenate([ref[:, 128 * (4 * j + a):128 * (4 * j + a + 1)] for a in range(4)],
                           axis=0).astype(F32)


def _swa_stack(pairs):
    return jnp.concatenate([pairs, pltpu.roll(pairs, 64, 1)], axis=0)


def _swa_unstack(t):
    left = lax.broadcasted_iota(jnp.int32, (4 * BLOCK, 128), 1) < 64
    return jnp.where(left, t[0:4 * BLOCK], pltpu.roll(t[4 * BLOCK:8 * BLOCK], 64, 1))


def _swa_softmax(qs, kmat, bias, sink):
    sc = _mm_nt(kmat, qs) + bias
    m = jnp.maximum(jnp.max(sc, axis=0, keepdims=True), sink)
    ex = jnp.exp(sc - m)
    es = jnp.exp(sink - m)
    return ex, es, 1.0 / (jnp.sum(ex, axis=0, keepdims=True) + es)


def _swa_fwd(proj, sinks):
    s = proj.shape[0]
    nb = s // BLOCK

    def body(sink_ref, q_ref, kp_ref, kc_ref, o_ref, bias_ref):
        n = pl.program_id(0)
        _swa_fill_bias(n, bias_ref)
        k, v, _ = _swa_operands(kp_ref[...].astype(F32), kc_ref[...].astype(F32))
        for j in range(2):
            qs = _swa_stack(_swa_pairs(q_ref, j) * 0.125)
            ex, _, inv = _swa_softmax(qs, k[j], bias_ref[j], _swa_sink_row(sink_ref, j))
            o_t = _mm(v[j].T, ex) * inv
            out = jnp.concatenate([o_t[0:64, 0:4 * BLOCK], o_t[0:64, 4 * BLOCK:8 * BLOCK]], axis=0).T
            for a in range(4):
                o_ref[:, 128 * (4 * j + a):128 * (4 * j + a + 1)] = out[128 * a:128 * (a + 1)].astype(
                    o_ref.dtype)

    return pl.pallas_call(
        body, name="swa_fwd", grid=(nb,),
        in_specs=[pl.BlockSpec(memory_space=pltpu.SMEM),
                  pl.BlockSpec((BLOCK, 1024), lambda n: (n, 0)),
                  pl.BlockSpec((BLOCK, 256), lambda n: (jnp.maximum(n - 1, 0), KVA // 256)),
                  pl.BlockSpec((BLOCK, 256), lambda n: (n, KVA // 256))],
        out_specs=pl.BlockSpec((BLOCK, 1024), lambda n: (n, 0)),
        out_shape=_sds((s, 1024), ATTN_DTYPE),
        scratch_shapes=[pltpu.VMEM((2, 2 * BLOCK, SWA_STACK), F32)],
        compiler_params=_params(1),
    )(sinks, proj, proj, proj)


def _mla_prep(proj, gq, gkv, w_q, w_kv, tabs):
    s = proj.shape[0]
    tm = min(512, s)
    c, s1, s2 = tabs

    def body(p_ref, gq_ref, gkv_ref, wq_ref, wkv_ref, c_ref, s1_ref, s2_ref,
             q_ref, k_ref, v_ref, vt_ref):
        cqh, _ = _rownorm(p_ref[:, 0:384].astype(F32))
        ckvh, _ = _rownorm(p_ref[:, 384:640].astype(F32))
        q = _mm(cqh * gq_ref[...], wq_ref[...])
        kv = _mm(ckvh * gkv_ref[...], wkv_ref[...])
        cc, ss1, ss2 = c_ref[...], s1_ref[...], s2_ref[...]
        krr = _rope(p_ref[:, 640:768].astype(F32), cc, ss1, ss2).astype(k_ref.dtype)
        for h in range(MLA_HEADS):
            q_ref[h, :, 0:128] = q[:, 128 * h:128 * (h + 1)].astype(q_ref.dtype)
            q_ref[h, :, 128:256] = _rope(q[:, 1024 + 128 * h:1024 + 128 * (h + 1)],
                                         cc, ss1, ss2).astype(q_ref.dtype)
            k_ref[h, :, 0:128] = kv[:, 128 * h:128 * (h + 1)].astype(k_ref.dtype)
            k_ref[h, :, 128:256] = krr
            vv = kv[:, 1024 + 128 * h:1024 + 128 * (h + 1)]
            v_ref[h] = vv.astype(v_ref.dtype)
            vt_ref[h, 0:128, :] = vv.T.astype(vt_ref.dtype)
            vt_ref[h, 128:256, :] = jnp.ones((128, tm), vt_ref.dtype)

    row = lambda i: (i, 0)
    fixed = lambda i: (0, 0)
    return pl.pallas_call(
        body, name="mla_prep", grid=(s // tm,),
        in_specs=[pl.BlockSpec((tm, 768), lambda i: (i, CQ // 768)),
                  pl.BlockSpec((1, Q_RANK), fixed), pl.BlockSpec((1, KV_RANK), fixed),
                  pl.BlockSpec((Q_RANK, 2048), fixed), pl.BlockSpec((KV_RANK, 2048), fixed),
                  pl.BlockSpec((tm, 128), row), pl.BlockSpec((tm, 128), row),
                  pl.BlockSpec((tm, 128), row)],
        out_specs=[pl.BlockSpec((MLA_HEADS, tm, 256), lambda i: (0, i, 0)),
                   pl.BlockSpec((MLA_HEADS, tm, 256), lambda i: (0, i, 0)),
                   pl.BlockSpec((MLA_HEADS, tm, 128), lambda i: (0, i, 0)),
                   pl.BlockSpec((MLA_HEADS, 256, tm), lambda i: (0, 0, i))],
        out_shape=[_sds((MLA_HEADS, s, 256), MXU_DTYPE), _sds((MLA_HEADS, s, 256), MXU_DTYPE),
                   _sds((MLA_HEADS, s, 128), MXU_DTYPE), _sds((MLA_HEADS, 256, s), MXU_DTYPE)],
        compiler_params=_params(1),
    )(proj, gq, gkv, w_q, w_kv, c, s1, s2)


def _scores_t(k, q, t, diagonal):
    sc = _mm_nt(k, q)
    if diagonal:
        key = lax.broadcasted_iota(jnp.int32, (t, t), 0)
        query = lax.broadcasted_iota(jnp.int32, (t, t), 1)
        sc = jnp.where(key <= query, sc, NEG)
    return sc


def _mla_fwd(qh, kh, vth, shards):
    s = qh.shape[1]
    t = min(MLA_TILE, s)
    nq = s // t
    na = len(shards)

    def body(*refs):
        q_ref, k_ref, vt_ref = refs[:3]
        o_ref, lse_ref = refs[3 + na:5 + na]
        m_ref, acc_ref = refs[5 + 2 * na:7 + 2 * na]
        h, i = pl.program_id(0), pl.program_id(1)
        if na:
            ex = _Gather(refs[3:3 + na], refs[5 + na:5 + 2 * na], refs[7 + 2 * na:])

            @pl.when((h == 0) & (i == 0))
            def _():
                ex.start()

            @pl.when((h == GATHER_FORWARD_HEAD) & (i == 0))
            def _():
                ex.forward()

        m_ref[...] = jnp.full(m_ref.shape, NEG, F32)
        acc_ref[...] = jnp.zeros(acc_ref.shape, F32)

        def step(start, width, diagonal):
            keys = pl.ds(pl.multiple_of(start, t), width)
            if diagonal:
                sc = _scores_t(k_ref[keys, :], q_ref[...], t, True)
            else:
                sc = _mm_nt(k_ref[keys, :], q_ref[...])
            m_prev = m_ref[...]
            m_new = jnp.maximum(m_prev, jnp.max(sc, axis=0, keepdims=True))
            alpha = jnp.exp2((m_prev - m_new) * MLA_C2)
            p = jnp.exp2((sc - m_new[0:1, :]) * MLA_C2)
            acc_ref[...] = alpha[0:1, :] * acc_ref[...] + _mm(vt_ref[:, keys], p)
            m_ref[...] = m_new

        def below_diagonal(jj, carry):
            step(jj * (4 * t), 4 * t, False)
            return carry

        lax.fori_loop(0, i // 4, below_diagonal, 0)

        @pl.when(i % 4 >= 2)
        def _():
            step((i // 4) * (4 * t), 2 * t, False)

        @pl.when(i % 2 == 1)
        def _():
            step((i - 1) * t, t, False)

        step(i * t, t, True)
        l = acc_ref[128:136, :]
        o_ref[...] = (acc_ref[0:128, :] / l[0:1, :]).T.astype(o_ref.dtype)
        lse_ref[...] = m_ref[...] * MLA_C2 + jnp.log2(l)

        if na:
            @pl.when((h == MLA_HEADS - 1) & (i == nq - 1))
            def _():
                ex.finish()

    head = lambda h, i: (h, 0, 0)
    outs = pl.pallas_call(
        body, name="mla_fwd_gather" if na else "mla_fwd", grid=(MLA_HEADS, nq),
        in_specs=[pl.BlockSpec((None, t, 256), lambda h, i: (h, i, 0)),
                  pl.BlockSpec((None, s, 256), head),
                  pl.BlockSpec((None, 256, s), head)] + [HBM_SPEC] * na,
        out_specs=[pl.BlockSpec((t, 128), lambda h, i: (i, h)),
                   pl.BlockSpec((None, 8, t), lambda h, i: (h, 0, i))] + [HBM_SPEC] * na,
        out_shape=[_sds((s, 1024), ATTN_DTYPE), _sds((MLA_HEADS, 8, s), F32)]
        + _exchange_shapes(shards, True),
        scratch_shapes=[pltpu.VMEM((8, t), F32), pltpu.VMEM((256, t), F32)]
        + (_Exchange.semaphores(na) if na else []),
        compiler_params=_params(2),
    )(qh, kh, vth, *shards)
    return outs[0], outs[1], list(outs[2:])


def _silu_parts(g):
    sg = jax.nn.sigmoid(g)
    return g * sg, sg * (1.0 + g * (1.0 - sg))


def _out_proj(x, proj, swa, mla, w_out):
    s = x.shape[0]
    tm = min(512, s)

    def body(x_ref, ga_ref, gb_ref, a_ref, b_ref, w_ref, xo_ref, y_ref):
        sa, _ = _silu_parts(ga_ref[...].astype(F32))
        sb, _ = _silu_parts(gb_ref[...].astype(F32))
        y_ref[:, 0:1024] = (a_ref[...].astype(F32) * sa).astype(y_ref.dtype)
        y_ref[:, 1024:2048] = (b_ref[...].astype(F32) * sb).astype(y_ref.dtype)
        xo_ref[...] = x_ref[...] + jnp.dot(y_ref[...], w_ref[...], preferred_element_type=F32)

    row = lambda i: (i, 0)
    return pl.pallas_call(
        body, name="out_proj", grid=(s // tm,),
        in_specs=[pl.BlockSpec((tm, D_MODEL), row),
                  pl.BlockSpec((tm, 1024), lambda i: (i, GA // 1024)),
                  pl.BlockSpec((tm, 1024), lambda i: (i, GB // 1024)),
                  pl.BlockSpec((tm, 1024), row), pl.BlockSpec((tm, 1024), row),
                  pl.BlockSpec((D_MODEL, D_MODEL), lambda i: (0, 0), pipeline_mode=pl.Buffered(1))],
        out_specs=[pl.BlockSpec((tm, D_MODEL), row), pl.BlockSpec((tm, D_MODEL), row)],
        out_shape=[_sds((s, D_MODEL), F32), _sds((s, D_MODEL), MXU_DTYPE)],
        compiler_params=_params(1),
    )(x, proj, proj, swa, mla, w_out)


def _final_loss(x, tgt, g):
    s = x.shape[0]
    tm = min(512, s)

    def body(x_ref, t_ref, g_ref, dx_ref, dxb_ref, dg_ref, loss_ref):
        @pl.when(pl.program_id(0) == 0)
        def _():
            dg_ref[...] = jnp.zeros(dg_ref.shape, F32)
            loss_ref[...] = jnp.zeros(loss_ref.shape, F32)
        xh, r = _rownorm(x_ref[...])
        gg = g_ref[...]
        err = xh * gg - t_ref[...]
        per_row = jnp.mean(err * err, axis=-1, keepdims=True)
        loss_ref[...] += 0.5 * jnp.sum(per_row, axis=0, keepdims=True)
        dy = err * (1.0 / D_MODEL)
        dg_ref[...] += jnp.sum(dy * xh, axis=0, keepdims=True)
        dx = _rownorm_bwd(dy * gg, xh, r)
        dx_ref[...] = dx
        dxb_ref[...] = dx.astype(dxb_ref.dtype)

    row = lambda i: (i, 0)
    fixed = lambda i: (0, 0)
    return pl.pallas_call(
        body, name="final_loss", grid=(s // tm,),
        in_specs=[pl.BlockSpec((tm, D_MODEL), row), pl.BlockSpec((tm, D_MODEL), row),
                  pl.BlockSpec((1, D_MODEL), fixed)],
        out_specs=[pl.BlockSpec((tm, D_MODEL), row), pl.BlockSpec((tm, D_MODEL), row),
                   pl.BlockSpec((1, D_MODEL), fixed), pl.BlockSpec((8, 128), fixed)],
        out_shape=[_sds((s, D_MODEL), F32), _sds((s, D_MODEL), MXU_DTYPE), _sds((1, D_MODEL), F32),
                   _sds((8, 128), F32)],
        compiler_params=_params(1),
    )(x, tgt, g)


def _out_proj_bwd(dx, proj, swa, mla, w_out):
    s = dx.shape[0]
    tm = min(512, s)

    def body(dx_ref, ga_ref, gb_ref, a_ref, b_ref, w_ref, doa_ref, dob_ref, dg_ref, dlt_ref):
        dx = dx_ref[...].astype(MXU_DTYPE)
        dya = _mm_nt(dx, w_ref[0:1024, :])
        sa, dsa = _silu_parts(ga_ref[...].astype(F32))
        doa_ref[...] = dya * sa
        dg_ref[:, 0:1024] = (dya * a_ref[...].astype(F32) * dsa).astype(dg_ref.dtype)
        dyb = _mm_nt(dx, w_ref[1024:2048, :])
        sb, dsb = _silu_parts(gb_ref[...].astype(F32))
        b = b_ref[...].astype(F32)
        dob = dyb * sb
        dob_ref[...] = dob.astype(dob_ref.dtype)
        dg_ref[:, 1024:2048] = (dyb * b * dsb).astype(dg_ref.dtype)
        prod = dob * b
        for h in range(MLA_HEADS):
            dlt = jnp.sum(prod[:, 128 * h:128 * (h + 1)], axis=1, keepdims=True)
            dlt_ref[h] = jnp.broadcast_to(dlt, (tm, 128)).T[0:8, :]

    row = lambda i: (i, 0)
    return pl.pallas_call(
        body, name="out_proj_bwd", grid=(s // tm,),
        in_specs=[pl.BlockSpec((tm, D_MODEL), row),
                  pl.BlockSpec((tm, 1024), lambda i: (i, GA // 1024)),
                  pl.BlockSpec((tm, 1024), lambda i: (i, GB // 1024)),
                  pl.BlockSpec((tm, 1024), row), pl.BlockSpec((tm, 1024), row),
                  pl.BlockSpec((D_MODEL, D_MODEL), lambda i: (0, 0), pipeline_mode=pl.Buffered(1))],
        out_specs=[pl.BlockSpec((tm, 1024), row), pl.BlockSpec((tm, 1024), row),
                   pl.BlockSpec((tm, D_MODEL), row),
                   pl.BlockSpec((MLA_HEADS, 8, tm), lambda i: (0, 0, i))],
        out_shape=[_sds((s, 1024), F32), _sds((s, 1024), MXU_DTYPE), _sds((s, D_MODEL), MXU_DTYPE),
                   _sds((MLA_HEADS, 8, s), F32)],
        compiler_params=_params(1),
    )(dx, proj, proj, swa, mla, w_out)


def _matmul_tn(a, b, name):
    s, m = a.shape
    n = b.shape[1]
    tm, tn, tk = min(1024, m), min(1024, n), min(2048, s)
    nk = s // tk

    def body(a_ref, b_ref, o_ref, acc_ref):
        k = pl.program_id(2)

        @pl.when(k == 0)
        def _():
            acc_ref[...] = jnp.zeros(acc_ref.shape, F32)
        acc_ref[...] += _mm_tn(a_ref[...], b_ref[...])

        @pl.when(k == nk - 1)
        def _():
            o_ref[...] = acc_ref[...].astype(o_ref.dtype)

    return pl.pallas_call(
        body, name=name, grid=(m // tm, n // tn, nk),
        in_specs=[pl.BlockSpec((tk, tm), lambda i, j, k: (k, i)),
                  pl.BlockSpec((tk, tn), lambda i, j, k: (k, j))],
        out_specs=pl.BlockSpec((tm, tn), lambda i, j, k: (i, j)),
        out_shape=_sds((m, n), GRAD_DTYPE),
        scratch_shapes=[pltpu.VMEM((tm, tn), F32)],
        compiler_params=_params(3),
    )(a, b)


def _grad_w_in(h, dqa, dgate, dlat, dkva, blocks):
    s = h.shape[0]
    tm, tn, tk = 1024, 1024, min(2048, s)
    nk = s // tk
    grid = (D_MODEL // tm, NP // tn, nk)
    na = len(blocks)

    def body(*refs):
        a_ref, dqa_ref, dg8_ref, dlat_ref, dkva_ref = refs[:5]
        o_ref = refs[5 + na]
        acc_ref = refs[6 + 2 * na]
        i, j, k = pl.program_id(0), pl.program_id(1), pl.program_id(2)
        if na:
            ex = _Exchange(refs[5:5 + na], refs[6 + na:6 + 2 * na], refs[7 + 2 * na:], gather=False)

            @pl.when((i == 0) & (j == 0) & (k == 0))
            def _():
                ex.start()

        @pl.when(k == 0)
        def _():
            acc_ref[...] = jnp.zeros(acc_ref.shape, F32)

        @pl.when(j == QA // tn)
        def _():
            acc_ref[...] += _mm_tn(a_ref[...], dqa_ref[...])

        @pl.when((j == GA // tn) | (j == GB // tn))
        def _():
            acc_ref[...] += _mm_tn(a_ref[...], dg8_ref[...])

        @pl.when(j == CQ // tn)
        def _():
            acc_ref[:, 0:768] += _mm_tn(a_ref[...], dlat_ref[...])
            acc_ref[:, 768:1024] += _mm_tn(a_ref[...], dkva_ref[...])

        @pl.when(k == nk - 1)
        def _():
            o_ref[...] = acc_ref[...].astype(o_ref.dtype)

        if na:
            @pl.when((i == grid[0] - 1) & (j == grid[1] - 1) & (k == nk - 1))
            def _():
                ex.wait()

    def when(group, width):
        return pl.BlockSpec((tk, width), lambda i, j, k: (jnp.where(j == group, k, 0), 0))

    outs = pl.pallas_call(
        body, name="grad_w_in_scatter" if na else "grad_w_in", grid=grid,
        in_specs=[pl.BlockSpec((tk, tm), lambda i, j, k: (k, i)),
                  when(QA // tn, 1024),
                  pl.BlockSpec((tk, 1024), lambda i, j, k: (
                      jnp.where((j == GA // tn) | (j == GB // tn), k, 0),
                      jnp.clip(j - GA // tn, 0, 1))),
                  when(CQ // tn, 768), when(CQ // tn, 256)] + [HBM_SPEC] * na,
        out_specs=[pl.BlockSpec((tm, tn), lambda i, j, k: (i, j))] + [HBM_SPEC] * na,
        out_shape=[_sds((D_MODEL, NP), GRAD_DTYPE)] + _exchange_shapes(blocks, False),
        scratch_shapes=[pltpu.VMEM((tm, tn), F32)] + (_Exchange.semaphores(na) if na else []),
        compiler_params=_params(3),
    )(h, dqa, dgate, dlat, dkva, *blocks)
    return outs[0], list(outs[1:])


def _swa_bwd(proj, sinks, do, o, blocks):
    s = proj.shape[0]
    nb = s // BLOCK
    na = len(blocks)

    def body(*refs):
        sink_ref, q_ref, kp_ref, kc_ref, do_ref, o_ref = refs[:6]
        dq_ref, dkv_ref, dsink_ref = refs[6 + na:9 + na]
        carry_ref, bias_ref = refs[9 + 2 * na:11 + 2 * na]
        n = pl.program_id(0)
        _swa_fill_bias(n, bias_ref)
        if na:
            ex = _Exchange(refs[6:6 + na], refs[9 + na:9 + 2 * na], refs[11 + 2 * na:], gather=False)

        @pl.when(n == 0)
        def _():
            carry_ref[...] = jnp.zeros(carry_ref.shape, F32)
            dsink_ref[...] = jnp.zeros(dsink_ref.shape, F32)
            if na:
                ex.start()

        @pl.when(n < nb)
        def _():
            k, v, left = _swa_operands(kp_ref[...].astype(F32), kc_ref[...].astype(F32))
            lane_s = lax.broadcasted_iota(jnp.int32, (8, 128), 1)
            dk, dv = [None, None], [None, None]
            dsink = jnp.zeros((8, 128), F32)
            for j in range(2):
                qs = _swa_stack(_swa_pairs(q_ref, j) * 0.125)
                do_pairs = _swa_pairs(do_ref, j)
                dos = _swa_stack(do_pairs)
                prod_t = (do_pairs * _swa_pairs(o_ref, j)).T
                dlt = jnp.concatenate([jnp.sum(prod_t[0:64], axis=0, keepdims=True),
                                       jnp.sum(prod_t[64:128], axis=0, keepdims=True)], axis=1)
                ex, es, inv = _swa_softmax(qs, k[j], bias_ref[j], _swa_sink_row(sink_ref, j))
                p = ex * inv
                ds = p * (_mm_nt(v[j], dos) - dlt)
                sink_term = es * inv * dlt
                for a in range(8):
                    dsh = -jnp.sum(sink_term[:, 128 * a:128 * (a + 1)], axis=1, keepdims=True)
                    dsink = dsink + jnp.where(lane_s == _swa_head(j, a), dsh, 0.0)
                dq = _swa_unstack(_mm_tn(ds, k[j]))
                dv[j] = _mm(p, dos)
                dk[j] = _mm(ds, qs)
                for a in range(4):
                    cols = slice(128 * (4 * j + a), 128 * (4 * j + a + 1))
                    dq_ref[:, cols] = (dq[128 * a:128 * (a + 1)] * 0.125).astype(dq_ref.dtype)

            def merge(t):
                return jnp.where(left, t[0], pltpu.roll(t[1], 64, 1))

            contrib = jnp.concatenate([merge(dk), merge(dv)], axis=1)
            dkv_ref[...] = (carry_ref[...] + contrib[0:BLOCK]).astype(dkv_ref.dtype)
            carry_ref[...] = contrib[BLOCK:2 * BLOCK]
            dsink_ref[...] += dsink

        @pl.when(n == nb)
        def _():
            dkv_ref[...] = carry_ref[...].astype(dkv_ref.dtype)
            if na:
                ex.wait()

    cur = lambda n: (jnp.minimum(n, nb - 1), 0)
    outs = pl.pallas_call(
        body, name="swa_bwd_scatter" if na else "swa_bwd", grid=(nb + 1,),
        in_specs=[pl.BlockSpec(memory_space=pltpu.SMEM),
                  pl.BlockSpec((BLOCK, 1024), cur),
                  pl.BlockSpec((BLOCK, 256), lambda n: (jnp.clip(n - 1, 0, nb - 1), KVA // 256)),
                  pl.BlockSpec((BLOCK, 256), lambda n: (jnp.minimum(n, nb - 1), KVA // 256)),
                  pl.BlockSpec((BLOCK, 1024), cur), pl.BlockSpec((BLOCK, 1024), cur)]
        + [HBM_SPEC] * na,
        out_specs=[pl.BlockSpec((BLOCK, 1024), cur),
                   pl.BlockSpec((BLOCK, 256), lambda n: (jnp.maximum(n - 1, 0), 0)),
                   pl.BlockSpec((8, 128), lambda n: (0, 0))] + [HBM_SPEC] * na,
        out_shape=[_sds((s, 1024), MXU_DTYPE), _sds((s, 256), MXU_DTYPE), _sds((8, 128), F32)]
        + _exchange_shapes(blocks, False),
        scratch_shapes=[pltpu.VMEM((BLOCK, 256), F32), pltpu.VMEM((2, 2 * BLOCK, SWA_STACK), F32)]
        + (_Exchange.semaphores(na) if na else []),
        compiler_params=_params(1),
    )(sinks, proj, proj, proj, do, o, *blocks)
    return outs[0], outs[1], outs[2], list(outs[3:])


def _mla_bwd(qh, kh, vh, do, dlt, lse, blocks):
    s = qh.shape[1]
    t = min(MLA_TILE, s)
    nq = s // t

    na = len(blocks)

    def body(*refs):
        q_ref, k_ref, v_ref, do_ref, dlt_ref, lse_ref = refs[:6]
        dq_ref, dk_ref, dv_ref = refs[6 + na:9 + na]
        h, j = pl.program_id(0), pl.program_id(1)
        if na:
            ex = _Exchange(refs[6:6 + na], refs[9 + na:9 + 2 * na], refs[9 + 2 * na:], gather=False)

            @pl.when((h == 0) & (j == 0))
            def _():
                ex.start()

        def step(start, width, diagonal):
            rows = pl.ds(pl.multiple_of(start, t), width)
            q, k, dout = q_ref[rows, :], k_ref[...], do_ref[rows, :]
            sc = _scores_t(k, q, t, True) if diagonal else _mm_nt(k, q)
            p = jnp.exp2(sc * MLA_C2 - lse_ref[0:1, rows])
            dv = _mm(p, dout)
            ds = p * (_mm_nt(v_ref[...], dout) - dlt_ref[0:1, rows])
            dk = _mm(ds, q)
            dq = _mm_tn(ds, k)
            return rows, dq, dk, dv

        rows, dq, dk, dv = step(j * t, t, True)
        dk_ref[...] = dk
        dv_ref[...] = dv

        @pl.when(j == 0)
        def _():
            dq_ref[rows, :] = dq * MLA_SCALE

        @pl.when(j > 0)
        def _():
            dq_ref[rows, :] = (dq_ref[rows, :] + dq) * MLA_SCALE

        def above_diagonal(start, width):
            rows, dq, dk, dv = step(start, width, False)
            dk_ref[...] += dk
            dv_ref[...] += dv

            @pl.when(j == 0)
            def _():
                dq_ref[rows, :] = dq

            @pl.when(j > 0)
            def _():
                dq_ref[rows, :] += dq

        n_above = nq - 1 - j

        def quad(jj, carry):
            above_diagonal((j + 1 + 4 * jj) * t, 4 * t)
            return carry

        lax.fori_loop(0, n_above // 4, quad, 0)

        @pl.when(n_above % 4 >= 2)
        def _():
            above_diagonal((j + 1 + 4 * (n_above // 4)) * t, 2 * t)

        @pl.when(n_above % 2 == 1)
        def _():
            above_diagonal((nq - 1) * t, t)

        dk_ref[...] *= MLA_SCALE

        if na:
            @pl.when((h == MLA_HEADS - 1) & (j == nq - 1))
            def _():
                ex.wait()

    head = lambda h, j: (h, 0, 0)
    kv_map = lambda h, j: (h, j, 0)
    outs = pl.pallas_call(
        body, name="mla_bwd_scatter" if na else "mla_bwd", grid=(MLA_HEADS, nq),
        in_specs=[pl.BlockSpec((None, s, 256), head), pl.BlockSpec((None, t, 256), kv_map),
                  pl.BlockSpec((None, t, 128), kv_map),
                  pl.BlockSpec((s, 128), lambda h, j: (0, h)),
                  pl.BlockSpec((None, 8, s), head), pl.BlockSpec((None, 8, s), head)]
        + [HBM_SPEC] * na,
        out_specs=[pl.BlockSpec((None, s, 256), head),
                   pl.BlockSpec((None, t, 256), kv_map), pl.BlockSpec((None, t, 128), kv_map)]
        + [HBM_SPEC] * na,
        out_shape=[_sds((MLA_HEADS, s, 256), F32), _sds((MLA_HEADS, s, 256), F32),
                   _sds((MLA_HEADS, s, 128), F32)] + _exchange_shapes(blocks, False),
        scratch_shapes=_Exchange.semaphores(na) if na else [],
        compiler_params=_params(2),
    )(qh, kh, vh, do, dlt, lse, *blocks)
    return outs[0], outs[1], outs[2], list(outs[3:])


def _mla_prep_bwd(dqh, dkh, dvh, proj, gq, gkv, w_q, w_kv, tabs):
    s = proj.shape[0]
    tm = min(256, s)
    nm = s // tm
    c, s1, s2 = tabs

    def body(dq_ref, dk_ref, dv_ref, p_ref, gq_ref, gkv_ref, wq_ref, wkv_ref,
             c_ref, s1_ref, s2_ref, dp_ref, dwq_ref, dwkv_ref, dgq_ref, dgkv_ref,
             dqf_ref, dkvf_ref, dwq_acc, dwkv_acc):
        @pl.when(pl.program_id(0) == 0)
        def _():
            dgq_ref[...] = jnp.zeros(dgq_ref.shape, F32)
            dgkv_ref[...] = jnp.zeros(dgkv_ref.shape, F32)
            dwq_acc[...] = jnp.zeros(dwq_acc.shape, F32)
            dwkv_acc[...] = jnp.zeros(dwkv_acc.shape, F32)
        cc, ns1, ns2 = c_ref[...], -s1_ref[...], -s2_ref[...]
        dkr = jnp.zeros((tm, 128), F32)
        for h in range(MLA_HEADS):
            dqf_ref[:, 128 * h:128 * (h + 1)] = dq_ref[h, :, 0:128].astype(dqf_ref.dtype)
            dqf_ref[:, 1024 + 128 * h:1024 + 128 * (h + 1)] = _rope(
                dq_ref[h, :, 128:256], cc, ns1, ns2).astype(dqf_ref.dtype)
            dkvf_ref[:, 128 * h:128 * (h + 1)] = dk_ref[h, :, 0:128].astype(dkvf_ref.dtype)
            dkvf_ref[:, 1024 + 128 * h:1024 + 128 * (h + 1)] = dv_ref[h].astype(dkvf_ref.dtype)
            dkr = dkr + dk_ref[h, :, 128:256]
        dcqn = _mm_nt(dqf_ref[...], wq_ref[...])
        dckvn = _mm_nt(dkvf_ref[...], wkv_ref[...])
        cqh, rq = _rownorm(p_ref[:, 0:384].astype(F32))
        ckvh, rkv = _rownorm(p_ref[:, 384:640].astype(F32))
        dgq_ref[...] += jnp.sum(dcqn * cqh, axis=0, keepdims=True)
        dgkv_ref[...] += jnp.sum(dckvn * ckvh, axis=0, keepdims=True)
        dp_ref[:, 0:384] = _rownorm_bwd(dcqn * gq_ref[...], cqh, rq).astype(dp_ref.dtype)
        dp_ref[:, 384:640] = _rownorm_bwd(dckvn * gkv_ref[...], ckvh, rkv).astype(dp_ref.dtype)
        dp_ref[:, 640:768] = _rope(dkr, cc, ns1, ns2).astype(dp_ref.dtype)
        dwq_acc[...] += _mm_tn(cqh * gq_ref[...], dqf_ref[...])
        dwkv_acc[...] += _mm_tn(ckvh * gkv_ref[...], dkvf_ref[...])

        @pl.when(pl.program_id(0) == nm - 1)
        def _():
            dwq_ref[...] = dwq_acc[...].astype(dwq_ref.dtype)
            dwkv_ref[...] = dwkv_acc[...].astype(dwkv_ref.dtype)

    row = lambda i: (i, 0)
    fixed = lambda i: (0, 0)
    head = lambda i: (0, i, 0)
    return pl.pallas_call(
        body, name="mla_prep_bwd", grid=(nm,),
        in_specs=[pl.BlockSpec((MLA_HEADS, tm, 256), head), pl.BlockSpec((MLA_HEADS, tm, 256), head),
                  pl.BlockSpec((MLA_HEADS, tm, 128), head),
                  pl.BlockSpec((tm, 768), lambda i: (i, CQ // 768)),
                  pl.BlockSpec((1, Q_RANK), fixed), pl.BlockSpec((1, KV_RANK), fixed),
                  pl.BlockSpec((Q_RANK, 2048), fixed), pl.BlockSpec((KV_RANK, 2048), fixed),
                  pl.BlockSpec((tm, 128), row), pl.BlockSpec((tm, 128), row),
                  pl.BlockSpec((tm, 128), row)],
        out_specs=[pl.BlockSpec((tm, 768), row), pl.BlockSpec((Q_RANK, 2048), fixed),
                   pl.BlockSpec((KV_RANK, 2048), fixed),
                   pl.BlockSpec((1, Q_RANK), fixed), pl.BlockSpec((1, KV_RANK), fixed)],
        out_shape=[_sds((s, 768), MXU_DTYPE), _sds((Q_RANK, 2048), GRAD_DTYPE),
                   _sds((KV_RANK, 2048), GRAD_DTYPE),
                   _sds((1, Q_RANK), F32), _sds((1, KV_RANK), F32)],
        scratch_shapes=[pltpu.VMEM((tm, 2048), MXU_DTYPE), pltpu.VMEM((tm, 2048), MXU_DTYPE),
                        pltpu.VMEM((Q_RANK, 2048), F32), pltpu.VMEM((KV_RANK, 2048), F32)],
        compiler_params=_params(1),
    )(dqh, dkh, dvh, proj, gq, gkv, w_q, w_kv, c, s1, s2)


def _in_proj_bwd(dqa, dgate, dlat, dkva, w, x, dx_out, g, blocks):
    s = x.shape[0]
    tm = min(256, s)
    nm = s // tm
    na = len(blocks)

    def body(*refs):
        dqa_ref, dg8_ref, dlat_ref, dkva_ref, w_ref, x_ref, dxo_ref, g_ref = refs[:8]
        dx_ref, dxb_ref, dg_ref = refs[8 + na:11 + na]
        if na:
            ex = _Exchange(refs[8:8 + na], refs[11 + na:11 + 2 * na], refs[11 + 2 * na:], gather=False)

        @pl.when(pl.program_id(0) == 0)
        def _():
            dg_ref[...] = jnp.zeros(dg_ref.shape, F32)
            if na:
                ex.start()

        dh = (_mm_nt(dqa_ref[...], w_ref[:, QA:QA + 1024])
              + _mm_nt(dg8_ref[:, 0:1024], w_ref[:, GA:GA + 1024])
              + _mm_nt(dg8_ref[:, 1024:2048], w_ref[:, GB:GB + 1024])
              + _mm_nt(dlat_ref[...], w_ref[:, CQ:CQ + 768])
              + _mm_nt(dkva_ref[...], w_ref[:, KVA:KVA + 256]))
        xh, r = _rownorm(x_ref[...])
        dg_ref[...] += jnp.sum(dh * xh, axis=0, keepdims=True)
        dx = dxo_ref[...] + _rownorm_bwd(dh * g_ref[...], xh, r)
        dx_ref[...] = dx
        dxb_ref[...] = dx.astype(dxb_ref.dtype)

        if na:
            @pl.when(pl.program_id(0) == nm - 1)
            def _():
                ex.wait()

    row = lambda i: (i, 0)
    fixed = lambda i: (0, 0)
    outs = pl.pallas_call(
        body, name="in_proj_bwd_scatter" if na else "in_proj_bwd", grid=(nm,),
        in_specs=[pl.BlockSpec((tm, 1024), row), pl.BlockSpec((tm, 2048), row),
                  pl.BlockSpec((tm, 768), row), pl.BlockSpec((tm, 256), row),
                  pl.BlockSpec((D_MODEL, NP), fixed, pipeline_mode=pl.Buffered(1)),
                  pl.BlockSpec((tm, D_MODEL), row), pl.BlockSpec((tm, D_MODEL), row),
                  pl.BlockSpec((1, D_MODEL), fixed)] + [HBM_SPEC] * na,
        out_specs=[pl.BlockSpec((tm, D_MODEL), row), pl.BlockSpec((tm, D_MODEL), row),
                   pl.BlockSpec((1, D_MODEL), fixed)] + [HBM_SPEC] * na,
        out_shape=[_sds((s, D_MODEL), F32), _sds((s, D_MODEL), MXU_DTYPE), _sds((1, D_MODEL), F32)]
        + _exchange_shapes(blocks, False),
        scratch_shapes=_Exchange.semaphores(na) if na else [],
        compiler_params=_params(1),
    )(dqa, dgate, dlat, dkva, w, x, dx_out, g, *blocks)
    return outs[0], outs[1], outs[2], list(outs[3:])


def _adamw_update(g, w_ref, m_ref, v_ref, g_ref, d_ref, nm_ref, nv_ref):
    m2 = ADAM_B1 * m_ref[...] + (1.0 - ADAM_B1) * g
    v2 = ADAM_B2 * v_ref[...] + (1.0 - ADAM_B2) * (g * g)
    m_hat = m2 / (1.0 - ADAM_B1 ** ADAM_STEP)
    v_hat = v2 / (1.0 - ADAM_B2 ** ADAM_STEP)
    g_ref[...] = g
    d_ref[...] = -ADAM_LR * (m_hat / (jnp.sqrt(v_hat) + ADAM_EPS) + ADAM_WD * w_ref[...])
    nm_ref[...] = m2
    nv_ref[...] = v2


def _reduce_adamw_in(parts, w_t, m_t, v_t, name):
    n_layers = len(parts)
    tc = 512
    nc = D_MODEL // tc

    def body(*refs):
        p_refs = refs[:n_layers]
        w_ref, m_ref, v_ref, g_ref, d_ref, nm_ref, nv_ref = refs[n_layers:]
        layer = pl.program_id(0)
        for l in range(n_layers):
            @pl.when(layer == l)
            def _(l=l):
                g = p_refs[l][0].astype(F32)
                for k in range(1, N_DEV):
                    g = g + p_refs[l][k].astype(F32)
                _adamw_update(g[0:SHARD_COLS], w_ref, m_ref, v_ref, g_ref, d_ref, nm_ref, nv_ref)

    def part_spec(l):
        return pl.BlockSpec((N_DEV, SHARD_PAD, tc),
                            lambda layer, j: (0, 0, jnp.where(layer == l, j, 0)))

    blk = pl.BlockSpec((None, SHARD_COLS, tc), lambda layer, j: (layer, 0, j))
    return pl.pallas_call(
        body, name=name, grid=(n_layers, nc),
        in_specs=[part_spec(l) for l in range(n_layers)] + [blk, blk, blk],
        out_specs=[blk, blk, blk, blk],
        out_shape=[_sds((n_layers, SHARD_COLS, D_MODEL), F32)] * 4,
        compiler_params=_params(2),
    )(*parts, w_t, m_t, v_t)


def _reduce_adamw(parts, w, m, v, name):
    n_layers = len(parts)
    rows, part_cols = parts[0].shape[1:]
    cols = w.shape[-1]
    lanes = -(-cols // 128) * 128
    tr = rows
    for cand in (1024, 512, 256, 128, 64, 32, 16, 8):
        if rows % cand == 0 and N_DEV * cand * lanes * 4 <= 8 * 1024 * 1024:
            tr = cand
            break
    nr = rows // tr

    def body(*refs):
        p_refs = refs[:n_layers]
        w_ref, m_ref, v_ref, g_ref, d_ref, nm_ref, nv_ref = refs[n_layers:]
        layer = pl.program_id(0)
        for l in range(n_layers):
            @pl.when(layer == l)
            def _(l=l):
                g = p_refs[l][0, :, 0:cols].astype(F32)
                for k in range(1, N_DEV):
                    g = g + p_refs[l][k, :, 0:cols].astype(F32)
                _adamw_update(g, w_ref, m_ref, v_ref, g_ref, d_ref, nm_ref, nv_ref)

    def part_spec(l):
        return pl.BlockSpec((N_DEV, tr, part_cols),
                            lambda layer, i: (0, jnp.where(layer == l, i, 0), 0))

    blk = pl.BlockSpec((tr, cols), lambda layer, i: (layer * nr + i, 0))
    return pl.pallas_call(
        body, name=name, grid=(n_layers, nr),
        in_specs=[part_spec(l) for l in range(n_layers)] + [blk, blk, blk],
        out_specs=[blk, blk, blk, blk],
        out_shape=[_sds((n_layers * rows, cols), F32)] * 4,
        compiler_params=_params(2),
    )(*parts, w, m, v)


def _position():
    x, y, c = lax.axis_index("x"), lax.axis_index("y"), lax.axis_index("c")
    return x, y, c


def _index(px, py, pc):
    return 4 * px + 2 * py + pc


HBM_SPEC = pl.BlockSpec(memory_space=pltpu.HBM)
IN_BLOCKED = (D_MODEL, N_DEV * SHARD_PAD)
IN_BLOCKED_T = (N_DEV * SHARD_PAD, D_MODEL)


def _block(ref, idx):
    if tuple(ref.shape) == IN_BLOCKED:
        return ref.at[:, pl.ds(pl.multiple_of(idx * SHARD_PAD, SHARD_PAD), SHARD_PAD)]
    if tuple(ref.shape) == IN_BLOCKED_T:
        return ref.at[pl.ds(pl.multiple_of(idx * SHARD_PAD, SHARD_PAD), SHARD_PAD)]
    return ref.at[idx]


class _Gather:
    def __init__(self, srcs, dsts, sems):
        self.srcs, self.dsts = srcs, dsts
        self.send_sems, self.recv_sems, self.local_sems = sems
        x, y, c = _position()
        self.c = c
        self.me, self.sibling = (x, y, c), (x, y, 1 - c)
        self.chips = [(1 - x, y), (x, 1 - y), (1 - x, 1 - y)]

    def _copy(self, a, k, block, to, own=False):
        slot = _block(self.dsts[a], _index(*block))
        return pltpu.make_async_remote_copy(
            src_ref=self.srcs[a] if own else slot, dst_ref=slot,
            send_sem=self.send_sems.at[7 * a + k], recv_sem=self.recv_sems.at[7 * a + k],
            device_id=to, device_id_type=MESH)

    def _local(self, a):
        return pltpu.make_async_copy(self.srcs[a], _block(self.dsts[a], _index(*self.me)),
                                     self.local_sems.at[a])

    def _first(self, a):
        return [self._copy(a, 0, self.me, self.sibling, own=True)] + [
            self._copy(a, 1 + j, self.me, (*chip, self.c), own=True)
            for j, chip in enumerate(self.chips)]

    def _passed(self, a, j):
        return self._copy(a, 4 + j, (*self.chips[j], self.c), self.sibling)

    def start(self):
        for a in range(len(self.srcs)):
            self._local(a).start()
            for cp in self._first(a):
                cp.start()

    def forward(self):
        for j, chip in enumerate(self.chips):
            for a in range(len(self.srcs)):
                self._copy(a, 1 + j, (*chip, self.c), self.me).wait_recv()
                self._passed(a, j).start()

    def finish(self):
        for a in range(len(self.srcs)):
            self._copy(a, 0, self.sibling, self.me).wait_recv()
            for j, chip in enumerate(self.chips):
                self._copy(a, 4 + j, (*chip, 1 - self.c), self.me).wait_recv()
            for cp in self._first(a):
                cp.wait_send()
            for j in range(3):
                self._passed(a, j).wait_send()
            self._local(a).wait()


def _all_gather(shards, name):
    na = len(shards)

    def body(*refs):
        g = _Gather(refs[:na], refs[na:2 * na], refs[2 * na:])
        g.start()
        g.forward()
        g.finish()

    return pl.pallas_call(
        body, name=name,
        in_specs=[HBM_SPEC] * na, out_specs=[HBM_SPEC] * na,
        out_shape=_exchange_shapes(shards, True),
        scratch_shapes=_Exchange.semaphores(na),
    )(*shards)


class _Exchange:
    def __init__(self, srcs, dsts, sems, gather):
        self.srcs, self.dsts, self.gather = srcs, dsts, gather
        self.send_sems, self.recv_sems, self.local_sems = sems
        x, y, c = _position()
        self.me = _index(x, y, c)
        self.peers = [(x ^ ((k >> 2) & 1), y ^ ((k >> 1) & 1), c ^ (k & 1)) for k in range(1, N_DEV)]

    @staticmethod
    def semaphores(na):
        return [pltpu.SemaphoreType.DMA((7 * na,)), pltpu.SemaphoreType.DMA((7 * na,)),
                pltpu.SemaphoreType.DMA((na,))]

    def _src(self, a, slot):
        return self.srcs[a] if self.gather else _block(self.srcs[a], slot)

    def _local(self, a):
        return pltpu.make_async_copy(self._src(a, self.me), self.dsts[a].at[self.me],
                                     self.local_sems.at[a])

    def _send(self, a, k):
        peer = self.peers[k]
        return pltpu.make_async_remote_copy(
            src_ref=self._src(a, _index(*peer)), dst_ref=self.dsts[a].at[self.me],
            send_sem=self.send_sems.at[7 * a + k], recv_sem=self.recv_sems.at[7 * a + k],
            device_id=peer, device_id_type=MESH)

    def _arrival(self, a, k):
        landed = self.dsts[a].at[_index(*self.peers[k])]
        return pltpu.make_async_remote_copy(
            src_ref=landed, dst_ref=landed,
            send_sem=self.send_sems.at[7 * a + k], recv_sem=self.recv_sems.at[7 * a + k],
            device_id=self.peers[k], device_id_type=MESH)

    def start(self):
        for a in range(len(self.srcs)):
            self._local(a).start()
            for k in range(N_DEV - 1):
                self._send(a, k).start()

    def wait(self):
        for a in range(len(self.srcs)):
            for k in range(N_DEV - 1):
                self._arrival(a, k).wait_recv()
            for k in range(N_DEV - 1):
                self._send(a, k).wait_send()
            self._local(a).wait()


def _exchange_shapes(arrays, gather):
    def shape(a):
        if gather:
            return IN_BLOCKED if a.shape == (D_MODEL, SHARD_PAD) else (N_DEV,) + a.shape
        return (N_DEV, SHARD_PAD, D_MODEL) if a.shape == IN_BLOCKED_T else a.shape
    return [_sds(shape(a), a.dtype) for a in arrays]


def _exchange_call(arrays, gather, name):
    na = len(arrays)

    def body(*refs):
        ex = _Exchange(refs[:na], refs[na:2 * na], refs[2 * na:], gather)
        ex.start()
        ex.wait()

    return pl.pallas_call(
        body, name=name,
        in_specs=[HBM_SPEC] * na, out_specs=[HBM_SPEC] * na,
        out_shape=_exchange_shapes(arrays, gather),
        scratch_shapes=_Exchange.semaphores(na),
    )(*arrays)


def _layer_fwd(x, small, w_in, g_rest, rest_shards, tabs, next_shards):
    attn_g, sinks, gq, gkv = small
    proj, h, gathered_rest = _in_proj(x, attn_g, w_in, rest_shards)
    w_q, w_kv, w_o = _rest_from_gathered(*(gathered_rest if rest_shards else g_rest))
    swa = _swa_fwd(proj, sinks)
    qh, kh, vh, vth = _mla_prep(proj, gq, gkv, w_q, w_kv, tabs)
    mla, lse, gathered = _mla_fwd(qh, kh, vth, next_shards)
    x_next, y = _out_proj(x, proj, swa, mla, w_o)
    return x_next, (x, proj, h, swa, qh, kh, vh, mla, lse, y), (w_in, w_q, w_kv, w_o), gathered


def _layer_bwd(dx, dxb, saved, small, weights, tabs, pending, send_own):
    attn_g, sinks, gq, gkv = small
    w_in, w_q, w_kv, w_o = weights
    x, proj, h, swa, qh, kh, vh, mla, lse, y = saved
    d_o = _matmul_tn(y, dxb, "grad_w_out")
    o_block = d_o.reshape(N_DEV, 256, D_MODEL)
    do_a, do_b, dgate, dlt = _out_proj_bwd(dxb, proj, swa, mla, w_o)
    dqa, dkva, dsk, got_o = _swa_bwd(proj, sinks, do_a, swa, [o_block] if send_own else [])
    dqh, dkh, dvh, received = _mla_bwd(qh, kh, vh, do_b, dlt, lse, pending)
    dlat, d_q, d_kv, d_gq, d_gkv = _mla_prep_bwd(dqh, dkh, dvh, proj, gq, gkv, w_q, w_kv, tabs)
    qkv_blocks = _qkv_grad_blocks(d_q, d_kv)
    d_inp, got_qkv = _grad_w_in(h, dqa, dgate, dlat, dkva, qkv_blocks if send_own else [])
    in_block = _in_grad_blocks(d_inp)
    dx, dxb, d_attn, got_in = _in_proj_bwd(dqa, dgate, dlat, dkva, w_in, x, dx, attn_g,
                                           [in_block] if send_own else [])
    small_grads = (d_attn, dsk[0:1, 0:SWA_HEADS], d_gq, d_gkv)
    blocks = [in_block] + qkv_blocks + [o_block]
    return dx, dxb, small_grads, blocks, received, got_in + got_qkv + got_o


def _pack_small_grads(small_grads, d_final, loss):
    d_attn, d_sink, d_gq, d_gkv = zip(*small_grads)
    return jnp.concatenate([
        jnp.concatenate(d_attn, axis=0).reshape(64, 128),
        jnp.concatenate(d_gq, axis=0).reshape(12, 128),
        jnp.concatenate(d_gkv, axis=0).reshape(8, 128),
        d_final.reshape(16, 128),
        jnp.pad(jnp.concatenate(d_sink, axis=1), ((0, 0), (0, 64))),
        loss[0:1],
        jnp.zeros((PACK_ROWS - ROW_LOSS - 1, 128), F32)], axis=0)


def _pack_small(attn, qa, kva, final, sinks):
    return jnp.concatenate([
        attn.reshape(64, 128), qa.reshape(12, 128), kva.reshape(8, 128), final.reshape(16, 128),
        jnp.pad(sinks.reshape(1, 64), ((0, 0), (0, 64))),
        jnp.zeros((PACK_ROWS - ROW_SINK - 1, 128), F32)], axis=0)


def _unpack_small(p):
    return (p[ROW_ATTN:ROW_QA].reshape(DEPTH, D_MODEL), p[ROW_SINK, 0:64].reshape(DEPTH, SWA_HEADS),
            p[ROW_QA:ROW_KVA].reshape(DEPTH, Q_RANK), p[ROW_KVA:ROW_FINAL].reshape(DEPTH, KV_RANK),
            p[ROW_FINAL:ROW_SINK].reshape(D_MODEL))


def kernel(x, attn_norm_g, w_in, swa_sinks, q_a_norm_g, kv_a_norm_g, w_q_b, w_kv_b, w_out, final_norm_g, loss_target, m_attn_norm_g, m_w_in, m_swa_sinks, m_q_a_norm_g, m_kv_a_norm_g, m_w_q_b, m_w_kv_b, m_w_out, m_final_norm_g, v_attn_norm_g, v_w_in, v_swa_sinks, v_q_a_norm_g, v_kv_a_norm_g, v_w_q_b, v_w_kv_b, v_w_out, v_final_norm_g):
    xs, tgt = x[0], loss_target[0]
    tabs = _rope_tables(xs.shape[0])
    shards = [w.astype(MXU_DTYPE) for w in (w_in, w_q_b, w_kv_b, w_out)]
    shards[0] = jnp.pad(shards[0], ((0, 0), (0, 0), (0, SHARD_PAD - SHARD_COLS)))
    layer_shards = lambda l: [w[l] for w in shards]
    smalls = [(attn_norm_g[l:l + 1], swa_sinks[l], q_a_norm_g[l:l + 1], kv_a_norm_g[l:l + 1])
              for l in range(DEPTH)]

    gathered = list(_all_gather(layer_shards(0)[:1], "gather_weights")) + [None] * 3
    weights, saved = [None] * DEPTH, []
    for l in range(DEPTH):
        next_shards = layer_shards(l + 1) if l + 1 < DEPTH else []
        rest_shards = layer_shards(0)[1:] if l == 0 else []
        xs, acts, weights[l], gathered = _layer_fwd(
            xs, smalls[l], _w_in_from_gathered(gathered[0]), gathered[1:], rest_shards, tabs,
            next_shards)
        saved.append(acts)
    dx, dxb, d_final, loss = _final_loss(xs, tgt, final_norm_g.reshape(1, D_MODEL))

    received, small_grads, pending = [None] * DEPTH, [None] * DEPTH, []
    for l in reversed(range(DEPTH)):
        dx, dxb, small_grads[l], blocks, arrived, arrived_own = _layer_bwd(
            dx, dxb, saved[l], smalls[l], weights[l], tabs, pending, send_own=(l == 0))
        if pending:
            received[l + 1] = arrived
        pending = blocks
    received[0] = arrived_own
    small = _exchange_call([_pack_small_grads(small_grads, d_final, loss)], True, "gather_small")[0]

    big = []
    for a, (w, m, v, name) in enumerate(zip((w_in, w_q_b, w_kv_b, w_out),
                                            (m_w_in, m_w_q_b, m_w_kv_b, m_w_out),
                                            (v_w_in, v_w_q_b, v_w_kv_b, v_w_out),
                                            ("adamw_w_in", "adamw_w_q_b", "adamw_w_kv_b",
                                             "adamw_w_out"))):
        parts = [received[l][a] for l in range(DEPTH)]
        if a == 0:
            swap = lambda t: t.transpose(0, 2, 1)
            outs = _reduce_adamw_in(parts, swap(w), swap(m), swap(v), name)
            big.append([swap(t) for t in outs])
        else:
            cols = w.shape[-1]
            flat = lambda t: t.reshape(-1, cols)
            outs = _reduce_adamw(parts, flat(w), flat(m), flat(v), name)
            big.append([t.reshape(w.shape) for t in outs])

    sm = _reduce_adamw(
        [small],
        _pack_small(attn_norm_g, q_a_norm_g, kv_a_norm_g, final_norm_g, swa_sinks),
        _pack_small(m_attn_norm_g, m_q_a_norm_g, m_kv_a_norm_g, m_final_norm_g, m_swa_sinks),
        _pack_small(v_attn_norm_g, v_q_a_norm_g, v_kv_a_norm_g, v_final_norm_g, v_swa_sinks),
        "adamw_small")
    loss = sm[0][ROW_LOSS, 0]
    kinds = []
    for t in range(4):
        attn, sinks, qa, kva, final = _unpack_small(sm[t])
        b_in, b_q, b_kv, b_o = (big[i][t] for i in range(4))
        kinds.append((attn, b_in, sinks, qa, kva, b_q, b_kv, b_o, final))
    return (loss, dx[None], *kinds[0], *kinds[1], *kinds[2], *kinds[3])
```

```python
import jax
import jax.numpy as jnp
import numpy as np
from jax import lax
from jax.experimental import pallas as pl
from jax.experimental.pallas import tpu as pltpu

F32 = jnp.float32
BF16 = jnp.bfloat16
MXU_DTYPE = BF16
GRAD_DTYPE = BF16
PROJ_DTYPE = BF16
ATTN_DTYPE = BF16

D_MODEL = 2048
DEPTH = 4
EPS = 1e-6
NEG = -1e30
BLOCK = 128
SWA_HEADS = 16
MLA_HEADS = 8
Q_RANK = 384
KV_RANK = 256
MLA_SCALE = 192 ** -0.5
MLA_C2 = MLA_SCALE * 1.4426950408889634
ROPE_THETA = 10000.0
IN_WIDTH = 4032

ADAM_LR = 0.001
ADAM_B1 = 0.9
ADAM_B2 = 0.999
ADAM_EPS = 1e-08
ADAM_WD = 0.01
ADAM_STEP = 10

N_DEV = 8
MESH = pl.DeviceIdType.MESH

NP = 4096
QA, GA, GB, CQ, CKV, KR, KVA = 0, 1024, 2048, 3072, 3456, 3712, 3840

ROW_ATTN, ROW_QA, ROW_KVA, ROW_FINAL, ROW_SINK, ROW_LOSS, PACK_ROWS = 0, 64, 76, 84, 100, 101, 104

VMEM_LIMIT = 56 * 1024 * 1024
MLA_TILE = 512
GATHER_FORWARD_HEAD = 7


def _sds(shape, dtype):
    return jax.ShapeDtypeStruct(shape, dtype)


def _params(n_axes):
    return pltpu.CompilerParams(dimension_semantics=("arbitrary",) * n_axes,
                                vmem_limit_bytes=VMEM_LIMIT)


def _mm(a, b):
    return jnp.dot(a.astype(MXU_DTYPE), b.astype(MXU_DTYPE), preferred_element_type=F32)


def _mm_nt(a, b):
    return lax.dot_general(a.astype(MXU_DTYPE), b.astype(MXU_DTYPE),
                           (((1,), (1,)), ((), ())), preferred_element_type=F32)


def _mm_tn(a, b):
    return lax.dot_general(a.astype(MXU_DTYPE), b.astype(MXU_DTYPE),
                           (((0,), (0,)), ((), ())), preferred_element_type=F32)


def _rownorm(x):
    r = lax.rsqrt(jnp.mean(x * x, axis=-1, keepdims=True) + EPS)
    return x * r, r


def _rownorm_bwd(dxh, xh, r):
    return r * (dxh - xh * jnp.mean(dxh * xh, axis=-1, keepdims=True))


def _rope(t, c, s1, s2):
    return t * c + pltpu.roll(t, 32, 1) * s1 + pltpu.roll(t, 96, 1) * s2


SHARD_COLS = IN_WIDTH // N_DEV
SHARD_PAD = 512


def _orig_col_of_padded():
    o = np.full((NP,), -1, np.int64)
    for start, width, orig in ((QA, 1024, 0), (KVA, 256, 1024), (GA, 1024, 1280), (CQ, 384, 2304),
                               (CKV, 256, 2688), (KR, 64, 2944), (GB, 1024, 3008)):
        o[start:start + width] = np.arange(orig, orig + width)
    return o


def _device_major_src():
    o = _orig_col_of_padded()
    return np.where(o >= 0, o + (SHARD_PAD - SHARD_COLS) * (o // SHARD_COLS), -1)


def _kernel_layout_src():
    o = _orig_col_of_padded()
    where = np.full((IN_WIDTH,), -1, np.int64)
    where[o[o >= 0]] = np.nonzero(o >= 0)[0]
    e = np.arange(N_DEV * SHARD_PAD)
    k, c = e // SHARD_PAD, e % SHARD_PAD
    return np.where(c < SHARD_COLS, where[np.minimum(SHARD_COLS * k + c, IN_WIDTH - 1)], -1)


def _permute_columns(x, src_of, name, transposed=False):
    rows, n_in = x.shape
    n_out = len(src_of)
    plan, mats = [], []
    for t in range(n_out // 128):
        srcs = src_of[128 * t:128 * (t + 1)]
        entry = []
        for u in sorted(set(int(s) // 128 for s in srcs if s >= 0)):
            m = np.zeros((128, 128), np.float32)
            for c, s in enumerate(srcs):
                if s >= 0 and s // 128 == u:
                    m[s % 128, c] = 1.0
            entry.append((u, len(mats)))
            mats.append(m)
        plan.append(entry)
    tr = min(512, rows)

    def body(x_ref, p_ref, o_ref):
        for t, entry in enumerate(plan):
            acc = jnp.zeros((tr, 128), F32)
            for u, idx in entry:
                acc = acc + jnp.dot(x_ref[:, 128 * u:128 * (u + 1)], p_ref[idx],
                                    preferred_element_type=F32)
            if transposed:
                o_ref[128 * t:128 * (t + 1), :] = acc.T.astype(o_ref.dtype)
            else:
                o_ref[:, 128 * t:128 * (t + 1)] = acc.astype(o_ref.dtype)

    table = jnp.asarray(np.stack(mats), x.dtype)
    return pl.pallas_call(
        body, name=name, grid=(rows // tr,),
        in_specs=[pl.BlockSpec((tr, n_in), lambda i: (i, 0)),
                  pl.BlockSpec(table.shape, lambda i: (0, 0, 0))],
        out_specs=(pl.BlockSpec((n_out, tr), lambda i: (0, i)) if transposed
                   else pl.BlockSpec((tr, n_out), lambda i: (i, 0))),
        out_shape=_sds((n_out, rows) if transposed else (rows, n_out), x.dtype),
        compiler_params=_params(1),
    )(x, table)


def _w_in_from_gathered(g_in):
    return _permute_columns(g_in, _device_major_src(), "w_in_layout")


def _rest_from_gathered(g_qb, g_kvb, g_out):
    qb = g_qb.transpose(1, 0, 2)
    rope = jnp.pad(qb[..., 128:], ((0, 0), (0, 0), (0, 64)))
    w_q = jnp.concatenate([qb[..., :128].reshape(Q_RANK, 1024),
                           rope.reshape(Q_RANK, 1024)], axis=-1)
    kvb = g_kvb.transpose(1, 0, 2)
    w_kv = jnp.concatenate([kvb[..., :128].reshape(KV_RANK, 1024),
                            kvb[..., 128:].reshape(KV_RANK, 1024)], axis=-1)
    w_o = g_out.reshape(D_MODEL, D_MODEL)
    return w_q, w_kv, w_o


def _in_grad_blocks(d_inp):
    return _permute_columns(d_inp, _kernel_layout_src(), "grad_w_in_layout", transposed=True)


def _qkv_grad_blocks(d_q, d_kv):
    qn = d_q[:, :1024].reshape(Q_RANK, 8, 128)
    qr = d_q[:, 1024:].reshape(Q_RANK, 8, 128)[..., :64]
    b_q = jnp.concatenate([qn, qr], axis=-1).transpose(1, 0, 2)
    kn = d_kv[:, :1024].reshape(KV_RANK, 8, 128)
    vv = d_kv[:, 1024:].reshape(KV_RANK, 8, 128)
    b_kv = jnp.concatenate([kn, vv], axis=-1).transpose(1, 0, 2)
    return [b_q, b_kv]


def _rope_tables(s):
    pos = jnp.arange(s, dtype=F32)
    inv_freq = ROPE_THETA ** (-jnp.arange(0, 64, 2, dtype=F32) / 64)
    ang = pos[:, None] * inv_freq[None, :]
    cos, sin = jnp.cos(ang), jnp.sin(ang)
    z32 = jnp.zeros((s, 32), F32)
    z64 = jnp.zeros((s, 64), F32)
    c = jnp.concatenate([cos, cos, z64], axis=1)
    s1 = jnp.concatenate([z32, sin, z64], axis=1)
    s2 = jnp.concatenate([-sin, z32, z64], axis=1)
    return c, s1, s2


def _in_proj(x, g, w, shards):
    s = x.shape[0]
    tm, tn = min(512, s), 1024
    nm = s // tm
    na = len(shards)

    def body(*refs):
        x_ref, g_ref, w_ref = refs[:3]
        o_ref, h_ref = refs[3 + na:5 + na]
        i = pl.program_id(0)
        if na:
            ex = _Gather(refs[3:3 + na], refs[5 + na:5 + 2 * na], refs[5 + 2 * na:])

            @pl.when(i == 0)
            def _():
                ex.start()

            @pl.when(i == (3 * nm) // 4)
            def _():
                ex.forward()

        xh, _ = _rownorm(x_ref[...])
        h_ref[...] = (xh * g_ref[...]).astype(h_ref.dtype)
        for j in range(NP // tn):
            cols = slice(j * tn, (j + 1) * tn)
            o_ref[:, cols] = jnp.dot(h_ref[...], w_ref[:, cols],
                                     preferred_element_type=F32).astype(o_ref.dtype)

        if na:
            @pl.when(i == nm - 1)
            def _():
                ex.finish()

    row = lambda i: (i, 0)
    fixed = lambda i: (0, 0)
    outs = pl.pallas_call(
        body, name="in_proj_gather" if na else "in_proj", grid=(nm,),
        in_specs=[pl.BlockSpec((tm, D_MODEL), row), pl.BlockSpec((1, D_MODEL), fixed),
                  pl.BlockSpec((D_MODEL, NP), fixed, pipeline_mode=pl.Buffered(1))]
        + [HBM_SPEC] * na,
        out_specs=[pl.BlockSpec((tm, NP), row), pl.BlockSpec((tm, D_MODEL), row)]
        + [HBM_SPEC] * na,
        out_shape=[_sds((s, NP), PROJ_DTYPE), _sds((s, D_MODEL), MXU_DTYPE)]
        + _exchange_shapes(shards, True),
        scratch_shapes=_Exchange.semaphores(na) if na else [],
        compiler_params=_params(1),
    )(x, g, w, *shards)
    return outs[0], outs[1], list(outs[2:])


def _swa_slopes():
    return [2.0 ** (-8.0 * (h + 1) / SWA_HEADS) for h in range(SWA_HEADS)]


SWA_STACK = 8 * BLOCK


def _swa_head(j, a):
    return 2 * (4 * j + a % 4) + a // 4


def _swa_operands(kv_p, kv_c):
    kk = jnp.concatenate([kv_p[:, :128], kv_c[:, :128]], axis=0)
    vv = jnp.concatenate([kv_p[:, 128:], kv_c[:, 128:]], axis=0)
    left = lax.broadcasted_iota(jnp.int32, (2 * BLOCK, 128), 1) < 64

    def heads(t):
        return [jnp.where(left, t, 0.0), jnp.where(left, pltpu.roll(t, 64, 1), 0.0)]

    return heads(kk), heads(vv), left


def _swa_by_block(vals):
    a = lax.broadcasted_iota(jnp.int32, (1, SWA_STACK), 1) >> 7
    row = jnp.full((1, SWA_STACK), vals[7], F32)
    for t in range(6, -1, -1):
        row = jnp.where(a == t, vals[t], row)
    return row


def _swa_bias(n, j):
    slopes = _swa_slopes()
    ki = lax.broadcasted_iota(jnp.int32, (2 * BLOCK, SWA_STACK), 0)
    r = lax.broadcasted_iota(jnp.int32, (2 * BLOCK, SWA_STACK), 1)
    delta = BLOCK + (r & (BLOCK - 1)) - ki
    valid = (delta >= 0) & (delta < BLOCK) & ((n - 1) * BLOCK + ki >= 0)
    slope = _swa_by_block([slopes[_swa_head(j, a)] for a in range(8)])
    return jnp.where(valid, -slope * delta.astype(F32), NEG)


def _swa_fill_bias(n, bias_ref):
    @pl.when(n <= 1)
    def _():
        for j in range(2):
            bias_ref[j] = _swa_bias(n, j)


def _swa_sink_row(sink_ref, j):
    return _swa_by_block([sink_ref[_swa_head(j, a)] for a in range(8)])


def _swa_pairs(ref, j):
    return jnp.concatenate([ref[:, 128 * (4 * j + a):128 * (4 * j + a + 1)] for a in range(4)],
                           axis=0).astype(F32)


def _swa_stack(pairs):
    return jnp.concatenate([pairs, pltpu.roll(pairs, 64, 1)], axis=0)


def _swa_unstack(t):
    left = lax.broadcasted_iota(jnp.int32, (4 * BLOCK, 128), 1) < 64
    return jnp.where(left, t[0:4 * BLOCK], pltpu.roll(t[4 * BLOCK:8 * BLOCK], 64, 1))


def _swa_softmax(qs, kmat, bias, sink):
    sc = _mm_nt(kmat, qs) + bias
    m = jnp.maximum(jnp.max(sc, axis=0, keepdims=True), sink)
    ex = jnp.exp(sc - m)
    es = jnp.exp(sink - m)
    return ex, es, 1.0 / (jnp.sum(ex, axis=0, keepdims=True) + es)


def _swa_fwd(proj, sinks):
    s = proj.shape[0]
    nb = s // BLOCK

    def body(sink_ref, q_ref, kp_ref, kc_ref, o_ref, bias_ref):
        n = pl.program_id(0)
        _swa_fill_bias(n, bias_ref)
        k, v, _ = _swa_operands(kp_ref[...].astype(F32), kc_ref[...].astype(F32))
        for j in range(2):
            qs = _swa_stack(_swa_pairs(q_ref, j) * 0.125)
            ex, _, inv = _swa_softmax(qs, k[j], bias_ref[j], _swa_sink_row(sink_ref, j))
            o_t = _mm(v[j].T, ex) * inv
            out = jnp.concatenate([o_t[0:64, 0:4 * BLOCK], o_t[0:64, 4 * BLOCK:8 * BLOCK]], axis=0).T
            for a in range(4):
                o_ref[:, 128 * (4 * j + a):128 * (4 * j + a + 1)] = out[128 * a:128 * (a + 1)].astype(
                    o_ref.dtype)

    return pl.pallas_call(
        body, name="swa_fwd", grid=(nb,),
        in_specs=[pl.BlockSpec(memory_space=pltpu.SMEM),
                  pl.BlockSpec((BLOCK, 1024), lambda n: (n, 0)),
                  pl.BlockSpec((BLOCK, 256), lambda n: (jnp.maximum(n - 1, 0), KVA // 256)),
                  pl.BlockSpec((BLOCK, 256), lambda n: (n, KVA // 256))],
        out_specs=pl.BlockSpec((BLOCK, 1024), lambda n: (n, 0)),
        out_shape=_sds((s, 1024), ATTN_DTYPE),
        scratch_shapes=[pltpu.VMEM((2, 2 * BLOCK, SWA_STACK), F32)],
        compiler_params=_params(1),
    )(sinks, proj, proj, proj)


def _mla_prep(proj, gq, gkv, w_q, w_kv, tabs):
    s = proj.shape[0]
    tm = min(512, s)
    c, s1, s2 = tabs

    def body(p_ref, gq_ref, gkv_ref, wq_ref, wkv_ref, c_ref, s1_ref, s2_ref,
             q_ref, k_ref, v_ref, vt_ref):
        cqh, _ = _rownorm(p_ref[:, 0:384].astype(F32))
        ckvh, _ = _rownorm(p_ref[:, 384:640].astype(F32))
        q = _mm(cqh * gq_ref[...], wq_ref[...])
        kv = _mm(ckvh * gkv_ref[...], wkv_ref[...])
        cc, ss1, ss2 = c_ref[...], s1_ref[...], s2_ref[...]
        krr = _rope(p_ref[:, 640:768].astype(F32), cc, ss1, ss2).astype(k_ref.dtype)
        for h in range(MLA_HEADS):
            q_ref[h, :, 0:128] = q[:, 128 * h:128 * (h + 1)].astype(q_ref.dtype)
            q_ref[h, :, 128:256] = _rope(q[:, 1024 + 128 * h:1024 + 128 * (h + 1)],
                                         cc, ss1, ss2).astype(q_ref.dtype)
            k_ref[h, :, 0:128] = kv[:, 128 * h:128 * (h + 1)].astype(k_ref.dtype)
            k_ref[h, :, 128:256] = krr
            vv = kv[:, 1024 + 128 * h:1024 + 128 * (h + 1)]
            v_ref[h] = vv.astype(v_ref.dtype)
            vt_ref[h, 0:128, :] = vv.T.astype(vt_ref.dtype)
            vt_ref[h, 128:256, :] = jnp.ones((128, tm), vt_ref.dtype)

    row = lambda i: (i, 0)
    fixed = lambda i: (0, 0)
    return pl.pallas_call(
        body, name="mla_prep", grid=(s // tm,),
        in_specs=[pl.BlockSpec((tm, 768), lambda i: (i, CQ // 768)),
                  pl.BlockSpec((1, Q_RANK), fixed), pl.BlockSpec((1, KV_RANK), fixed),
                  pl.BlockSpec((Q_RANK, 2048), fixed), pl.BlockSpec((KV_RANK, 2048), fixed),
                  pl.BlockSpec((tm, 128), row), pl.BlockSpec((tm, 128), row),
                  pl.BlockSpec((tm, 128), row)],
        out_specs=[pl.BlockSpec((MLA_HEADS, tm, 256), lambda i: (0, i, 0)),
                   pl.BlockSpec((MLA_HEADS, tm, 256), lambda i: (0, i, 0)),
                   pl.BlockSpec((MLA_HEADS, tm, 128), lambda i: (0, i, 0)),
                   pl.BlockSpec((MLA_HEADS, 256, tm), lambda i: (0, 0, i))],
        out_shape=[_sds((MLA_HEADS, s, 256), MXU_DTYPE), _sds((MLA_HEADS, s, 256), MXU_DTYPE),
                   _sds((MLA_HEADS, s, 128), MXU_DTYPE), _sds((MLA_HEADS, 256, s), MXU_DTYPE)],
        compiler_params=_params(1),
    )(proj, gq, gkv, w_q, w_kv, c, s1, s2)


def _scores_t(k, q, t, diagonal):
    sc = _mm_nt(k, q)
    if diagonal:
        key = lax.broadcasted_iota(jnp.int32, sc.shape, 0)
        query = lax.broadcasted_iota(jnp.int32, sc.shape, 1)
        sc = jnp.where(key <= query, sc, NEG)
    return sc


def _mla_fwd(qh, kh, vth, shards):
    s = qh.shape[1]
    t = min(MLA_TILE, s)
    nq = s // t
    na = len(shards)

    def body(*refs):
        q_ref, k_ref, vt_ref = refs[:3]
        o_ref, lse_ref = refs[3 + na:5 + na]
        m_ref, acc_ref = refs[5 + 2 * na:7 + 2 * na]
        h, i = pl.program_id(0), pl.program_id(1)
        if na:
            ex = _Gather(refs[3:3 + na], refs[5 + na:5 + 2 * na], refs[7 + 2 * na:])

            @pl.when((h == 0) & (i == 0))
            def _():
                ex.start()

            @pl.when((h == GATHER_FORWARD_HEAD) & (i == 0))
            def _():
                ex.forward()

        m_ref[...] = jnp.full(m_ref.shape, NEG, F32)
        acc_ref[...] = jnp.zeros(acc_ref.shape, F32)

        def step(start, width):
            keys = pl.ds(pl.multiple_of(start, t), width)
            sc = _mm_nt(k_ref[keys, :], q_ref[...])
            m_prev = m_ref[...]
            m_new = jnp.maximum(m_prev, jnp.max(sc, axis=0, keepdims=True))
            alpha = jnp.exp2((m_prev - m_new) * MLA_C2)
            p = jnp.exp2((sc - m_new[0:1, :]) * MLA_C2)
            acc_ref[...] = alpha[0:1, :] * acc_ref[...] + _mm(vt_ref[:, keys], p)
            m_ref[...] = m_new

        def diagonal_step():
            half = t // 2
            k_lo = pl.ds(pl.multiple_of(i * t, half), half)
            k_hi = pl.ds(pl.multiple_of(i * t + half, half), half)
            sc_lo = _scores_t(k_ref[k_lo, :], q_ref[...], t, True)
            sc_hi = _scores_t(k_ref[k_hi, :], q_ref[half:t, :], t, True)
            m_prev = m_ref[...]
            m_lo = jnp.maximum(m_prev, jnp.max(sc_lo, axis=0, keepdims=True))
            m_new = jnp.concatenate(
                [m_lo[:, 0:half],
                 jnp.maximum(m_lo[:, half:t], jnp.max(sc_hi, axis=0, keepdims=True))], axis=1)
            alpha = jnp.exp2((m_prev - m_new) * MLA_C2)
            p_lo = jnp.exp2((sc_lo - m_new[0:1, :]) * MLA_C2)
            p_hi = jnp.exp2((sc_hi - m_new[0:1, half:t]) * MLA_C2)
            acc_ref[...] = alpha[0:1, :] * acc_ref[...] + _mm(vt_ref[:, k_lo], p_lo)
            acc_ref[:, half:t] += _mm(vt_ref[:, k_hi], p_hi)
            m_ref[...] = m_new

        def below_diagonal(jj, carry):
            step(jj * (4 * t), 4 * t)
            return carry

        lax.fori_loop(0, i // 4, below_diagonal, 0)

        @pl.when(i % 4 >= 2)
        def _():
            step((i // 4) * (4 * t), 2 * t)

        @pl.when(i % 2 == 1)
        def _():
            step((i - 1) * t, t)

        diagonal_step()
        l = acc_ref[128:136, :]
        o_ref[...] = (acc_ref[0:128, :] / l[0:1, :]).T.astype(o_ref.dtype)
        lse_ref[...] = m_ref[...] * MLA_C2 + jnp.log2(l)

        if na:
            @pl.when((h == MLA_HEADS - 1) & (i == nq - 1))
            def _():
                ex.finish()

    head = lambda h, i: (h, 0, 0)
    outs = pl.pallas_call(
        body, name="mla_fwd_gather" if na else "mla_fwd", grid=(MLA_HEADS, nq),
        in_specs=[pl.BlockSpec((None, t, 256), lambda h, i: (h, i, 0)),
                  pl.BlockSpec((None, s, 256), head),
                  pl.BlockSpec((None, 256, s), head)] + [HBM_SPEC] * na,
        out_specs=[pl.BlockSpec((t, 128), lambda h, i: (i, h)),
                   pl.BlockSpec((None, 8, t), lambda h, i: (h, 0, i))] + [HBM_SPEC] * na,
        out_shape=[_sds((s, 1024), ATTN_DTYPE), _sds((MLA_HEADS, 8, s), F32)]
        + _exchange_shapes(shards, True),
        scratch_shapes=[pltpu.VMEM((8, t), F32), pltpu.VMEM((256, t), F32)]
        + (_Exchange.semaphores(na) if na else []),
        compiler_params=_params(2),
    )(qh, kh, vth, *shards)
    return outs[0], outs[1], list(outs[2:])


def _silu_parts(g):
    sg = jax.nn.sigmoid(g)
    return g * sg, sg * (1.0 + g * (1.0 - sg))


def _out_proj(x, proj, swa, mla, w_out):
    s = x.shape[0]
    tm = min(512, s)

    def body(x_ref, ga_ref, gb_ref, a_ref, b_ref, w_ref, xo_ref, y_ref):
        sa, _ = _silu_parts(ga_ref[...].astype(F32))
        sb, _ = _silu_parts(gb_ref[...].astype(F32))
        y_ref[:, 0:1024] = (a_ref[...].astype(F32) * sa).astype(y_ref.dtype)
        y_ref[:, 1024:2048] = (b_ref[...].astype(F32) * sb).astype(y_ref.dtype)
        xo_ref[...] = x_ref[...] + jnp.dot(y_ref[...], w_ref[...], preferred_element_type=F32)

    row = lambda i: (i, 0)
    return pl.pallas_call(
        body, name="out_proj", grid=(s // tm,),
        in_specs=[pl.BlockSpec((tm, D_MODEL), row),
                  pl.BlockSpec((tm, 1024), lambda i: (i, GA // 1024)),
                  pl.BlockSpec((tm, 1024), lambda i: (i, GB // 1024)),
                  pl.BlockSpec((tm, 1024), row), pl.BlockSpec((tm, 1024), row),
                  pl.BlockSpec((D_MODEL, D_MODEL), lambda i: (0, 0), pipeline_mode=pl.Buffered(1))],
        out_specs=[pl.BlockSpec((tm, D_MODEL), row), pl.BlockSpec((tm, D_MODEL), row)],
        out_shape=[_sds((s, D_MODEL), F32), _sds((s, D_MODEL), MXU_DTYPE)],
        compiler_params=_params(1),
    )(x, proj, proj, swa, mla, w_out)


def _final_loss(x, tgt, g):
    s = x.shape[0]
    tm = min(512, s)

    def body(x_ref, t_ref, g_ref, dx_ref, dxb_ref, dg_ref, loss_ref):
        @pl.when(pl.program_id(0) == 0)
        def _():
            dg_ref[...] = jnp.zeros(dg_ref.shape, F32)
            loss_ref[...] = jnp.zeros(loss_ref.shape, F32)
        xh, r = _rownorm(x_ref[...])
        gg = g_ref[...]
        err = xh * gg - t_ref[...]
        per_row = jnp.mean(err * err, axis=-1, keepdims=True)
        loss_ref[...] += 0.5 * jnp.sum(per_row, axis=0, keepdims=True)
        dy = err * (1.0 / D_MODEL)
        dg_ref[...] += jnp.sum(dy * xh, axis=0, keepdims=True)
        dx = _rownorm_bwd(dy * gg, xh, r)
        dx_ref[...] = dx
        dxb_ref[...] = dx.astype(dxb_ref.dtype)

    row = lambda i: (i, 0)
    fixed = lambda i: (0, 0)
    return pl.pallas_call(
        body, name="final_loss", grid=(s // tm,),
        in_specs=[pl.BlockSpec((tm, D_MODEL), row), pl.BlockSpec((tm, D_MODEL), row),
                  pl.BlockSpec((1, D_MODEL), fixed)],
        out_specs=[pl.BlockSpec((tm, D_MODEL), row), pl.BlockSpec((tm, D_MODEL), row),
                   pl.BlockSpec((1, D_MODEL), fixed), pl.BlockSpec((8, 128), fixed)],
        out_shape=[_sds((s, D_MODEL), F32), _sds((s, D_MODEL), MXU_DTYPE), _sds((1, D_MODEL), F32),
                   _sds((8, 128), F32)],
        compiler_params=_params(1),
    )(x, tgt, g)


def _out_proj_bwd(dx, proj, swa, mla, w_out):
    s = dx.shape[0]
    tm = min(512, s)

    def body(dx_ref, ga_ref, gb_ref, a_ref, b_ref, w_ref, doa_ref, dob_ref, dg_ref, dlt_ref):
        dx = dx_ref[...].astype(MXU_DTYPE)
        dya = _mm_nt(dx, w_ref[0:1024, :])
        sa, dsa = _silu_parts(ga_ref[...].astype(F32))
        doa_ref[...] = dya * sa
        dg_ref[:, 0:1024] = (dya * a_ref[...].astype(F32) * dsa).astype(dg_ref.dtype)
        dyb = _mm_nt(dx, w_ref[1024:2048, :])
        sb, dsb = _silu_parts(gb_ref[...].astype(F32))
        b = b_ref[...].astype(F32)
        dob = dyb * sb
        dob_ref[...] = dob.astype(dob_ref.dtype)
        dg_ref[:, 1024:2048] = (dyb * b * dsb).astype(dg_ref.dtype)
        prod = dob * b
        for h in range(MLA_HEADS):
            dlt = jnp.sum(prod[:, 128 * h:128 * (h + 1)], axis=1, keepdims=True)
            dlt_ref[h] = jnp.broadcast_to(dlt, (tm, 128)).T[0:8, :]

    row = lambda i: (i, 0)
    return pl.pallas_call(
        body, name="out_proj_bwd", grid=(s // tm,),
        in_specs=[pl.BlockSpec((tm, D_MODEL), row),
                  pl.BlockSpec((tm, 1024), lambda i: (i, GA // 1024)),
                  pl.BlockSpec((tm, 1024), lambda i: (i, GB // 1024)),
                  pl.BlockSpec((tm, 1024), row), pl.BlockSpec((tm, 1024), row),
                  pl.BlockSpec((D_MODEL, D_MODEL), lambda i: (0, 0), pipeline_mode=pl.Buffered(1))],
        out_specs=[pl.BlockSpec((tm, 1024), row), pl.BlockSpec((tm, 1024), row),
                   pl.BlockSpec((tm, D_MODEL), row),
                   pl.BlockSpec((MLA_HEADS, 8, tm), lambda i: (0, 0, i))],
        out_shape=[_sds((s, 1024), F32), _sds((s, 1024), MXU_DTYPE), _sds((s, D_MODEL), MXU_DTYPE),
                   _sds((MLA_HEADS, 8, s), F32)],
        compiler_params=_params(1),
    )(dx, proj, proj, swa, mla, w_out)


def _matmul_tn(a, b, name):
    s, m = a.shape
    n = b.shape[1]
    tm, tn, tk = min(1024, m), min(1024, n), min(2048, s)
    nk = s // tk

    def body(a_ref, b_ref, o_ref, acc_ref):
        k = pl.program_id(2)

        @pl.when(k == 0)
        def _():
            acc_ref[...] = jnp.zeros(acc_ref.shape, F32)
        acc_ref[...] += _mm_tn(a_ref[...], b_ref[...])

        @pl.when(k == nk - 1)
        def _():
            o_ref[...] = acc_ref[...].astype(o_ref.dtype)

    return pl.pallas_call(
        body, name=name, grid=(m // tm, n // tn, nk),
        in_specs=[pl.BlockSpec((tk, tm), lambda i, j, k: (k, i)),
                  pl.BlockSpec((tk, tn), lambda i, j, k: (k, j))],
        out_specs=pl.BlockSpec((tm, tn), lambda i, j, k: (i, j)),
        out_shape=_sds((m, n), GRAD_DTYPE),
        scratch_shapes=[pltpu.VMEM((tm, tn), F32)],
        compiler_params=_params(3),
    )(a, b)


def _grad_w_in(h, dqa, dgate, dlat, dkva, blocks):
    s = h.shape[0]
    tm, tn, tk = 1024, 1024, min(2048, s)
    nk = s // tk
    grid = (D_MODEL // tm, NP // tn, nk)
    na = len(blocks)

    def body(*refs):
        a_ref, dqa_ref, dg8_ref, dlat_ref, dkva_ref = refs[:5]
        o_ref = refs[5 + na]
        acc_ref = refs[6 + 2 * na]
        i, j, k = pl.program_id(0), pl.program_id(1), pl.program_id(2)
        if na:
            ex = _Exchange(refs[5:5 + na], refs[6 + na:6 + 2 * na], refs[7 + 2 * na:], gather=False)

            @pl.when((i == 0) & (j == 0) & (k == 0))
            def _():
                ex.start()

        @pl.when(k == 0)
        def _():
            acc_ref[...] = jnp.zeros(acc_ref.shape, F32)

        @pl.when(j == QA // tn)
        def _():
            acc_ref[...] += _mm_tn(a_ref[...], dqa_ref[...])

        @pl.when((j == GA // tn) | (j == GB // tn))
        def _():
            acc_ref[...] += _mm_tn(a_ref[...], dg8_ref[...])

        @pl.when(j == CQ // tn)
        def _():
            acc_ref[:, 0:768] += _mm_tn(a_ref[...], dlat_ref[...])
            acc_ref[:, 768:1024] += _mm_tn(a_ref[...], dkva_ref[...])

        @pl.when(k == nk - 1)
        def _():
            o_ref[...] = acc_ref[...].astype(o_ref.dtype)

        if na:
            @pl.when((i == grid[0] - 1) & (j == grid[1] - 1) & (k == nk - 1))
            def _():
                ex.wait()

    def when(group, width):
        return pl.BlockSpec((tk, width), lambda i, j, k: (jnp.where(j == group, k, 0), 0))

    outs = pl.pallas_call(
        body, name="grad_w_in_scatter" if na else "grad_w_in", grid=grid,
        in_specs=[pl.BlockSpec((tk, tm), lambda i, j, k: (k, i)),
                  when(QA // tn, 1024),
                  pl.BlockSpec((tk, 1024), lambda i, j, k: (
                      jnp.where((j == GA // tn) | (j == GB // tn), k, 0),
                      jnp.clip(j - GA // tn, 0, 1))),
                  when(CQ // tn, 768), when(CQ // tn, 256)] + [HBM_SPEC] * na,
        out_specs=[pl.BlockSpec((tm, tn), lambda i, j, k: (i, j))] + [HBM_SPEC] * na,
        out_shape=[_sds((D_MODEL, NP), GRAD_DTYPE)] + _exchange_shapes(blocks, False),
        scratch_shapes=[pltpu.VMEM((tm, tn), F32)] + (_Exchange.semaphores(na) if na else []),
        compiler_params=_params(3),
    )(h, dqa, dgate, dlat, dkva, *blocks)
    return outs[0], list(outs[1:])


def _swa_bwd(proj, sinks, do, o, blocks):
    s = proj.shape[0]
    nb = s // BLOCK
    na = len(blocks)

    def body(*refs):
        sink_ref, q_ref, kp_ref, kc_ref, do_ref, o_ref = refs[:6]
        dq_ref, dkv_ref, dsink_ref = refs[6 + na:9 + na]
        carry_ref, bias_ref = refs[9 + 2 * na:11 + 2 * na]
        n = pl.program_id(0)
        _swa_fill_bias(n, bias_ref)
        if na:
            ex = _Exchange(refs[6:6 + na], refs[9 + na:9 + 2 * na], refs[11 + 2 * na:], gather=False)

        @pl.when(n == 0)
        def _():
            carry_ref[...] = jnp.zeros(carry_ref.shape, F32)
            dsink_ref[...] = jnp.zeros(dsink_ref.shape, F32)
            if na:
                ex.start()

        @pl.when(n < nb)
        def _():
            k, v, left = _swa_operands(kp_ref[...].astype(F32), kc_ref[...].astype(F32))
            lane_s = lax.broadcasted_iota(jnp.int32, (8, 128), 1)
            dk, dv = [None, None], [None, None]
            dsink = jnp.zeros((8, 128), F32)
            for j in range(2):
                qs = _swa_stack(_swa_pairs(q_ref, j) * 0.125)
                do_pairs = _swa_pairs(do_ref, j)
                dos = _swa_stack(do_pairs)
                prod_t = (do_pairs * _swa_pairs(o_ref, j)).T
                dlt = jnp.concatenate([jnp.sum(prod_t[0:64], axis=0, keepdims=True),
                                       jnp.sum(prod_t[64:128], axis=0, keepdims=True)], axis=1)
                ex, es, inv = _swa_softmax(qs, k[j], bias_ref[j], _swa_sink_row(sink_ref, j))
                p = ex * inv
                ds = p * (_mm_nt(v[j], dos) - dlt)
                sink_term = es * inv * dlt
                for a in range(8):
                    dsh = -jnp.sum(sink_term[:, 128 * a:128 * (a + 1)], axis=1, keepdims=True)
                    dsink = dsink + jnp.where(lane_s == _swa_head(j, a), dsh, 0.0)
                dq = _swa_unstack(_mm_tn(ds, k[j]))
                dv[j] = _mm(p, dos)
                dk[j] = _mm(ds, qs)
                for a in range(4):
                    cols = slice(128 * (4 * j + a), 128 * (4 * j + a + 1))
                    dq_ref[:, cols] = (dq[128 * a:128 * (a + 1)] * 0.125).astype(dq_ref.dtype)

            def merge(t):
                return jnp.where(left, t[0], pltpu.roll(t[1], 64, 1))

            contrib = jnp.concatenate([merge(dk), merge(dv)], axis=1)
            dkv_ref[...] = (carry_ref[...] + contrib[0:BLOCK]).astype(dkv_ref.dtype)
            carry_ref[...] = contrib[BLOCK:2 * BLOCK]
            dsink_ref[...] += dsink

        @pl.when(n == nb)
        def _():
            dkv_ref[...] = carry_ref[...].astype(dkv_ref.dtype)
            if na:
                ex.wait()

    cur = lambda n: (jnp.minimum(n, nb - 1), 0)
    outs = pl.pallas_call(
        body, name="swa_bwd_scatter" if na else "swa_bwd", grid=(nb + 1,),
        in_specs=[pl.BlockSpec(memory_space=pltpu.SMEM),
                  pl.BlockSpec((BLOCK, 1024), cur),
                  pl.BlockSpec((BLOCK, 256), lambda n: (jnp.clip(n - 1, 0, nb - 1), KVA // 256)),
                  pl.BlockSpec((BLOCK, 256), lambda n: (jnp.minimum(n, nb - 1), KVA // 256)),
                  pl.BlockSpec((BLOCK, 1024), cur), pl.BlockSpec((BLOCK, 1024), cur)]
        + [HBM_SPEC] * na,
        out_specs=[pl.BlockSpec((BLOCK, 1024), cur),
                   pl.BlockSpec((BLOCK, 256), lambda n: (jnp.maximum(n - 1, 0), 0)),
                   pl.BlockSpec((8, 128), lambda n: (0, 0))] + [HBM_SPEC] * na,
        out_shape=[_sds((s, 1024), MXU_DTYPE), _sds((s, 256), MXU_DTYPE), _sds((8, 128), F32)]
        + _exchange_shapes(blocks, False),
        scratch_shapes=[pltpu.VMEM((BLOCK, 256), F32), pltpu.VMEM((2, 2 * BLOCK, SWA_STACK), F32)]
        + (_Exchange.semaphores(na) if na else []),
        compiler_params=_params(1),
    )(sinks, proj, proj, proj, do, o, *blocks)
    return outs[0], outs[1], outs[2], list(outs[3:])


def _mla_bwd(qh, kh, vh, do, dlt, lse, blocks):
    s = qh.shape[1]
    t = min(MLA_TILE, s)
    nq = s // t

    na = len(blocks)

    def body(*refs):
        q_ref, k_ref, v_ref, do_ref, dlt_ref, lse_ref = refs[:6]
        dq_ref, dk_ref, dv_ref = refs[6 + na:9 + na]
        h, j = pl.program_id(0), pl.program_id(1)
        if na:
            ex = _Exchange(refs[6:6 + na], refs[9 + na:9 + 2 * na], refs[9 + 2 * na:], gather=False)

            @pl.when((h == 0) & (j == 0))
            def _():
                ex.start()

        def step(start, width, diagonal=False):
            rows = pl.ds(pl.multiple_of(start, t), width)
            q, k, dout = q_ref[rows, :], k_ref[...], do_ref[rows, :]
            sc = _scores_t(k, q, t, diagonal)
            p = jnp.exp2(sc * MLA_C2 - lse_ref[0:1, rows])
            dv = _mm(p, dout)
            ds = p * (_mm_nt(v_ref[...], dout) - dlt_ref[0:1, rows])
            dk = _mm(ds, q)
            dq = _mm_tn(ds, k)
            return rows, dq, dk, dv

        rows, dq, dk, dv = step(j * t, t, diagonal=True)
        dk_ref[...] = dk
        dv_ref[...] = dv

        @pl.when(j == 0)
        def _():
            dq_ref[rows, :] = dq * MLA_SCALE

        @pl.when(j > 0)
        def _():
            dq_ref[rows, :] = (dq_ref[rows, :] + dq) * MLA_SCALE

        def above_diagonal(start, width):
            rows, dq, dk, dv = step(start, width)
            dk_ref[...] += dk
            dv_ref[...] += dv

            @pl.when(j == 0)
            def _():
                dq_ref[rows, :] = dq

            @pl.when(j > 0)
            def _():
                dq_ref[rows, :] += dq

        n_above = nq - 1 - j

        def quad(jj, carry):
            above_diagonal((j + 1 + 4 * jj) * t, 4 * t)
            return carry

        lax.fori_loop(0, n_above // 4, quad, 0)

        @pl.when(n_above % 4 >= 2)
        def _():
            above_diagonal((j + 1 + 4 * (n_above // 4)) * t, 2 * t)

        @pl.when(n_above % 2 == 1)
        def _():
            above_diagonal((nq - 1) * t, t)

        dk_ref[...] *= MLA_SCALE

        if na:
            @pl.when((h == MLA_HEADS - 1) & (j == nq - 1))
            def _():
                ex.wait()

    head = lambda h, j: (h, 0, 0)
    kv_map = lambda h, j: (h, j, 0)
    outs = pl.pallas_call(
        body, name="mla_bwd_scatter" if na else "mla_bwd", grid=(MLA_HEADS, nq),
        in_specs=[pl.BlockSpec((None, s, 256), head), pl.BlockSpec((None, t, 256), kv_map),
                  pl.BlockSpec((None, t, 128), kv_map),
                  pl.BlockSpec((s, 128), lambda h, j: (0, h)),
                  pl.BlockSpec((None, 8, s), head), pl.BlockSpec((None, 8, s), head)]
        + [HBM_SPEC] * na,
        out_specs=[pl.BlockSpec((None, s, 256), head),
                   pl.BlockSpec((None, t, 256), kv_map), pl.BlockSpec((None, t, 128), kv_map)]
        + [HBM_SPEC] * na,
        out_shape=[_sds((MLA_HEADS, s, 256), F32), _sds((MLA_HEADS, s, 256), F32),
                   _sds((MLA_HEADS, s, 128), F32)] + _exchange_shapes(blocks, False),
        scratch_shapes=_Exchange.semaphores(na) if na else [],
        compiler_params=_params(2),
    )(qh, kh, vh, do, dlt, lse, *blocks)
    return outs[0], outs[1], outs[2], list(outs[3:])


def _mla_prep_bwd(dqh, dkh, dvh, proj, gq, gkv, w_q, w_kv, tabs):
    s = proj.shape[0]
    tm = min(256, s)
    nm = s // tm
    c, s1, s2 = tabs

    def body(dq_ref, dk_ref, dv_ref, p_ref, gq_ref, gkv_ref, wq_ref, wkv_ref,
             c_ref, s1_ref, s2_ref, dp_ref, dwq_ref, dwkv_ref, dgq_ref, dgkv_ref,
             dqf_ref, dkvf_ref, dwq_acc, dwkv_acc):
        @pl.when(pl.program_id(0) == 0)
        def _():
            dgq_ref[...] = jnp.zeros(dgq_ref.shape, F32)
            dgkv_ref[...] = jnp.zeros(dgkv_ref.shape, F32)
            dwq_acc[...] = jnp.zeros(dwq_acc.shape, F32)
            dwkv_acc[...] = jnp.zeros(dwkv_acc.shape, F32)
        cc, ns1, ns2 = c_ref[...], -s1_ref[...], -s2_ref[...]
        dkr = jnp.zeros((tm, 128), F32)
        for h in range(MLA_HEADS):
            dqf_ref[:, 128 * h:128 * (h + 1)] = dq_ref[h, :, 0:128].astype(dqf_ref.dtype)
            dqf_ref[:, 1024 + 128 * h:1024 + 128 * (h + 1)] = _rope(
                dq_ref[h, :, 128:256], cc, ns1, ns2).astype(dqf_ref.dtype)
            dkvf_ref[:, 128 * h:128 * (h + 1)] = dk_ref[h, :, 0:128].astype(dkvf_ref.dtype)
            dkvf_ref[:, 1024 + 128 * h:1024 + 128 * (h + 1)] = dv_ref[h].astype(dkvf_ref.dtype)
            dkr = dkr + dk_ref[h, :, 128:256]
        dcqn = _mm_nt(dqf_ref[...], wq_ref[...])
        dckvn = _mm_nt(dkvf_ref[...], wkv_ref[...])
        cqh, rq = _rownorm(p_ref[:, 0:384].astype(F32))
        ckvh, rkv = _rownorm(p_ref[:, 384:640].astype(F32))
        dgq_ref[...] += jnp.sum(dcqn * cqh, axis=0, keepdims=True)
        dgkv_ref[...] += jnp.sum(dckvn * ckvh, axis=0, keepdims=True)
        dp_ref[:, 0:384] = _rownorm_bwd(dcqn * gq_ref[...], cqh, rq).astype(dp_ref.dtype)
        dp_ref[:, 384:640] = _rownorm_bwd(dckvn * gkv_ref[...], ckvh, rkv).astype(dp_ref.dtype)
        dp_ref[:, 640:768] = _rope(dkr, cc, ns1, ns2).astype(dp_ref.dtype)
        dwq_acc[...] += _mm_tn(cqh * gq_ref[...], dqf_ref[...])
        dwkv_acc[...] += _mm_tn(ckvh * gkv_ref[...], dkvf_ref[...])

        @pl.when(pl.program_id(0) == nm - 1)
        def _():
            dwq_ref[...] = dwq_acc[...].astype(dwq_ref.dtype)
            dwkv_ref[...] = dwkv_acc[...].astype(dwkv_ref.dtype)

    row = lambda i: (i, 0)
    fixed = lambda i: (0, 0)
    head = lambda i: (0, i, 0)
    return pl.pallas_call(
        body, name="mla_prep_bwd", grid=(nm,),
        in_specs=[pl.BlockSpec((MLA_HEADS, tm, 256), head), pl.BlockSpec((MLA_HEADS, tm, 256), head),
                  pl.BlockSpec((MLA_HEADS, tm, 128), head),
                  pl.BlockSpec((tm, 768), lambda i: (i, CQ // 768)),
                  pl.BlockSpec((1, Q_RANK), fixed), pl.BlockSpec((1, KV_RANK), fixed),
                  pl.BlockSpec((Q_RANK, 2048), fixed), pl.BlockSpec((KV_RANK, 2048), fixed),
                  pl.BlockSpec((tm, 128), row), pl.BlockSpec((tm, 128), row),
                  pl.BlockSpec((tm, 128), row)],
        out_specs=[pl.BlockSpec((tm, 768), row), pl.BlockSpec((Q_RANK, 2048), fixed),
                   pl.BlockSpec((KV_RANK, 2048), fixed),
                   pl.BlockSpec((1, Q_RANK), fixed), pl.BlockSpec((1, KV_RANK), fixed)],
        out_shape=[_sds((s, 768), MXU_DTYPE), _sds((Q_RANK, 2048), GRAD_DTYPE),
                   _sds((KV_RANK, 2048), GRAD_DTYPE),
                   _sds((1, Q_RANK), F32), _sds((1, KV_RANK), F32)],
        scratch_shapes=[pltpu.VMEM((tm, 2048), MXU_DTYPE), pltpu.VMEM((tm, 2048), MXU_DTYPE),
                        pltpu.VMEM((Q_RANK, 2048), F32), pltpu.VMEM((KV_RANK, 2048), F32)],
        compiler_params=_params(1),
    )(dqh, dkh, dvh, proj, gq, gkv, w_q, w_kv, c, s1, s2)


def _in_proj_bwd(dqa, dgate, dlat, dkva, w, x, dx_out, g, blocks):
    s = x.shape[0]
    tm = min(256, s)
    nm = s // tm
    na = len(blocks)

    def body(*refs):
        dqa_ref, dg8_ref, dlat_ref, dkva_ref, w_ref, x_ref, dxo_ref, g_ref = refs[:8]
        dx_ref, dxb_ref, dg_ref = refs[8 + na:11 + na]
        if na:
            ex = _Exchange(refs[8:8 + na], refs[11 + na:11 + 2 * na], refs[11 + 2 * na:], gather=False)

        @pl.when(pl.program_id(0) == 0)
        def _():
            dg_ref[...] = jnp.zeros(dg_ref.shape, F32)
            if na:
                ex.start()

        dh = (_mm_nt(dqa_ref[...], w_ref[:, QA:QA + 1024])
              + _mm_nt(dg8_ref[:, 0:1024], w_ref[:, GA:GA + 1024])
              + _mm_nt(dg8_ref[:, 1024:2048], w_ref[:, GB:GB + 1024])
              + _mm_nt(dlat_ref[...], w_ref[:, CQ:CQ + 768])
              + _mm_nt(dkva_ref[...], w_ref[:, KVA:KVA + 256]))
        xh, r = _rownorm(x_ref[...])
        dg_ref[...] += jnp.sum(dh * xh, axis=0, keepdims=True)
        dx = dxo_ref[...] + _rownorm_bwd(dh * g_ref[...], xh, r)
        dx_ref[...] = dx
        dxb_ref[...] = dx.astype(dxb_ref.dtype)

        if na:
            @pl.when(pl.program_id(0) == nm - 1)
            def _():
                ex.wait()

    row = lambda i: (i, 0)
    fixed = lambda i: (0, 0)
    outs = pl.pallas_call(
        body, name="in_proj_bwd_scatter" if na else "in_proj_bwd", grid=(nm,),
        in_specs=[pl.BlockSpec((tm, 1024), row), pl.BlockSpec((tm, 2048), row),
                  pl.BlockSpec((tm, 768), row), pl.BlockSpec((tm, 256), row),
                  pl.BlockSpec((D_MODEL, NP), fixed, pipeline_mode=pl.Buffered(1)),
                  pl.BlockSpec((tm, D_MODEL), row), pl.BlockSpec((tm, D_MODEL), row),
                  pl.BlockSpec((1, D_MODEL), fixed)] + [HBM_SPEC] * na,
        out_specs=[pl.BlockSpec((tm, D_MODEL), row), pl.BlockSpec((tm, D_MODEL), row),
                   pl.BlockSpec((1, D_MODEL), fixed)] + [HBM_SPEC] * na,
        out_shape=[_sds((s, D_MODEL), F32), _sds((s, D_MODEL), MXU_DTYPE), _sds((1, D_MODEL), F32)]
        + _exchange_shapes(blocks, False),
        scratch_shapes=_Exchange.semaphores(na) if na else [],
        compiler_params=_params(1),
    )(dqa, dgate, dlat, dkva, w, x, dx_out, g, *blocks)
    return outs[0], outs[1], outs[2], list(outs[3:])


def _adamw_update(g, w_ref, m_ref, v_ref, g_ref, d_ref, nm_ref, nv_ref):
    m2 = ADAM_B1 * m_ref[...] + (1.0 - ADAM_B1) * g
    v2 = ADAM_B2 * v_ref[...] + (1.0 - ADAM_B2) * (g * g)
    m_hat = m2 / (1.0 - ADAM_B1 ** ADAM_STEP)
    v_hat = v2 / (1.0 - ADAM_B2 ** ADAM_STEP)
    g_ref[...] = g
    d_ref[...] = -ADAM_LR * (m_hat / (jnp.sqrt(v_hat) + ADAM_EPS) + ADAM_WD * w_ref[...])
    nm_ref[...] = m2
    nv_ref[...] = v2


def _reduce_adamw_in(parts, w_t, m_t, v_t, name):
    n_layers = len(parts)
    tc = 512
    nc = D_MODEL // tc

    def body(*refs):
        p_refs = refs[:n_layers]
        w_ref, m_ref, v_ref, g_ref, d_ref, nm_ref, nv_ref = refs[n_layers:]
        layer = pl.program_id(0)
        for l in range(n_layers):
            @pl.when(layer == l)
            def _(l=l):
                g = p_refs[l][0].astype(F32)
                for k in range(1, N_DEV):
                    g = g + p_refs[l][k].astype(F32)
                _adamw_update(g[0:SHARD_COLS], w_ref, m_ref, v_ref, g_ref, d_ref, nm_ref, nv_ref)

    def part_spec(l):
        return pl.BlockSpec((N_DEV, SHARD_PAD, tc),
                            lambda layer, j: (0, 0, jnp.where(layer == l, j, 0)))

    blk = pl.BlockSpec((None, SHARD_COLS, tc), lambda layer, j: (layer, 0, j))
    return pl.pallas_call(
        body, name=name, grid=(n_layers, nc),
        in_specs=[part_spec(l) for l in range(n_layers)] + [blk, blk, blk],
        out_specs=[blk, blk, blk, blk],
        out_shape=[_sds((n_layers, SHARD_COLS, D_MODEL), F32)] * 4,
        compiler_params=_params(2),
    )(*parts, w_t, m_t, v_t)


def _reduce_adamw(parts, w, m, v, name):
    n_layers = len(parts)
    rows, part_cols = parts[0].shape[1:]
    cols = w.shape[-1]
    lanes = -(-cols // 128) * 128
    tr = rows
    for cand in (1024, 512, 256, 128, 64, 32, 16, 8):
        if rows % cand == 0 and N_DEV * cand * lanes * 4 <= 8 * 1024 * 1024:
            tr = cand
            break
    nr = rows // tr

    def body(*refs):
        p_refs = refs[:n_layers]
        w_ref, m_ref, v_ref, g_ref, d_ref, nm_ref, nv_ref = refs[n_layers:]
        layer = pl.program_id(0)
        for l in range(n_layers):
            @pl.when(layer == l)
            def _(l=l):
                g = p_refs[l][0, :, 0:cols].astype(F32)
                for k in range(1, N_DEV):
                    g = g + p_refs[l][k, :, 0:cols].astype(F32)
                _adamw_update(g, w_ref, m_ref, v_ref, g_ref, d_ref, nm_ref, nv_ref)

    def part_spec(l):
        return pl.BlockSpec((N_DEV, tr, part_cols),
                            lambda layer, i: (0, jnp.where(layer == l, i, 0), 0))

    blk = pl.BlockSpec((tr, cols), lambda layer, i: (layer * nr + i, 0))
    return pl.pallas_call(
        body, name=name, grid=(n_layers, nr),
        in_specs=[part_spec(l) for l in range(n_layers)] + [blk, blk, blk],
        out_specs=[blk, blk, blk, blk],
        out_shape=[_sds((n_layers * rows, cols), F32)] * 4,
        compiler_params=_params(2),
    )(*parts, w, m, v)


def _position():
    x, y, c = lax.axis_index("x"), lax.axis_index("y"), lax.axis_index("c")
    return x, y, c


def _index(px, py, pc):
    return 4 * px + 2 * py + pc


HBM_SPEC = pl.BlockSpec(memory_space=pltpu.HBM)
IN_BLOCKED = (D_MODEL, N_DEV * SHARD_PAD)
IN_BLOCKED_T = (N_DEV * SHARD_PAD, D_MODEL)


def _block(ref, idx):
    if tuple(ref.shape) == IN_BLOCKED:
        return ref.at[:, pl.ds(pl.multiple_of(idx * SHARD_PAD, SHARD_PAD), SHARD_PAD)]
    if tuple(ref.shape) == IN_BLOCKED_T:
        return ref.at[pl.ds(pl.multiple_of(idx * SHARD_PAD, SHARD_PAD), SHARD_PAD)]
    return ref.at[idx]


class _Gather:
    def __init__(self, srcs, dsts, sems):
        self.srcs, self.dsts = srcs, dsts
        self.send_sems, self.recv_sems, self.local_sems = sems
        x, y, c = _position()
        self.c = c
        self.me, self.sibling = (x, y, c), (x, y, 1 - c)
        self.chips = [(1 - x, y), (x, 1 - y), (1 - x, 1 - y)]

    def _copy(self, a, k, block, to, own=False):
        slot = _block(self.dsts[a], _index(*block))
        return pltpu.make_async_remote_copy(
            src_ref=self.srcs[a] if own else slot, dst_ref=slot,
            send_sem=self.send_sems.at[7 * a + k], recv_sem=self.recv_sems.at[7 * a + k],
            device_id=to, device_id_type=MESH)

    def _local(self, a):
        return pltpu.make_async_copy(self.srcs[a], _block(self.dsts[a], _index(*self.me)),
                                     self.local_sems.at[a])

    def _first(self, a):
        return [self._copy(a, 0, self.me, self.sibling, own=True)] + [
            self._copy(a, 1 + j, self.me, (*chip, self.c), own=True)
            for j, chip in enumerate(self.chips)]

    def _passed(self, a, j):
        return self._copy(a, 4 + j, (*self.chips[j], self.c), self.sibling)

    def start(self):
        for a in range(len(self.srcs)):
            self._local(a).start()
            for cp in self._first(a):
                cp.start()

    def forward(self):
        for j, chip in enumerate(self.chips):
            for a in range(len(self.srcs)):
                self._copy(a, 1 + j, (*chip, self.c), self.me).wait_recv()
                self._passed(a, j).start()

    def finish(self):
        for a in range(len(self.srcs)):
            self._copy(a, 0, self.sibling, self.me).wait_recv()
            for j, chip in enumerate(self.chips):
                self._copy(a, 4 + j, (*chip, 1 - self.c), self.me).wait_recv()
            for cp in self._first(a):
                cp.wait_send()
            for j in range(3):
                self._passed(a, j).wait_send()
            self._local(a).wait()


def _all_gather(shards, name):
    na = len(shards)

    def body(*refs):
        g = _Gather(refs[:na], refs[na:2 * na], refs[2 * na:])
        g.start()
        g.forward()
        g.finish()

    return pl.pallas_call(
        body, name=name,
        in_specs=[HBM_SPEC] * na, out_specs=[HBM_SPEC] * na,
        out_shape=_exchange_shapes(shards, True),
        scratch_shapes=_Exchange.semaphores(na),
    )(*shards)


class _Exchange:
    def __init__(self, srcs, dsts, sems, gather):
        self.srcs, self.dsts, self.gather = srcs, dsts, gather
        self.send_sems, self.recv_sems, self.local_sems = sems
        x, y, c = _position()
        self.me = _index(x, y, c)
        self.peers = [(x ^ ((k >> 2) & 1), y ^ ((k >> 1) & 1), c ^ (k & 1)) for k in range(1, N_DEV)]

    @staticmethod
    def semaphores(na):
        return [pltpu.SemaphoreType.DMA((7 * na,)), pltpu.SemaphoreType.DMA((7 * na,)),
                pltpu.SemaphoreType.DMA((na,))]

    def _src(self, a, slot):
        return self.srcs[a] if self.gather else _block(self.srcs[a], slot)

    def _local(self, a):
        return pltpu.make_async_copy(self._src(a, self.me), self.dsts[a].at[self.me],
                                     self.local_sems.at[a])

    def _send(self, a, k):
        peer = self.peers[k]
        return pltpu.make_async_remote_copy(
            src_ref=self._src(a, _index(*peer)), dst_ref=self.dsts[a].at[self.me],
            send_sem=self.send_sems.at[7 * a + k], recv_sem=self.recv_sems.at[7 * a + k],
            device_id=peer, device_id_type=MESH)

    def _arrival(self, a, k):
        landed = self.dsts[a].at[_index(*self.peers[k])]
        return pltpu.make_async_remote_copy(
            src_ref=landed, dst_ref=landed,
            send_sem=self.send_sems.at[7 * a + k], recv_sem=self.recv_sems.at[7 * a + k],
            device_id=self.peers[k], device_id_type=MESH)

    def start(self):
        for a in range(len(self.srcs)):
            self._local(a).start()
            for k in range(N_DEV - 1):
                self._send(a, k).start()

    def wait(self):
        for a in range(len(self.srcs)):
            for k in range(N_DEV - 1):
                self._arrival(a, k).wait_recv()
            for k in range(N_DEV - 1):
                self._send(a, k).wait_send()
            self._local(a).wait()


def _exchange_shapes(arrays, gather):
    def shape(a):
        if gather:
            return IN_BLOCKED if a.shape == (D_MODEL, SHARD_PAD) else (N_DEV,) + a.shape
        return (N_DEV, SHARD_PAD, D_MODEL) if a.shape == IN_BLOCKED_T else a.shape
    return [_sds(shape(a), a.dtype) for a in arrays]


def _exchange_call(arrays, gather, name):
    na = len(arrays)

    def body(*refs):
        ex = _Exchange(refs[:na], refs[na:2 * na], refs[2 * na:], gather)
        ex.start()
        ex.wait()

    return pl.pallas_call(
        body, name=name,
        in_specs=[HBM_SPEC] * na, out_specs=[HBM_SPEC] * na,
        out_shape=_exchange_shapes(arrays, gather),
        scratch_shapes=_Exchange.semaphores(na),
    )(*arrays)


def _layer_fwd(x, small, w_in, g_rest, rest_shards, tabs, next_shards):
    attn_g, sinks, gq, gkv = small
    proj, h, gathered_rest = _in_proj(x, attn_g, w_in, rest_shards)
    w_q, w_kv, w_o = _rest_from_gathered(*(gathered_rest if rest_shards else g_rest))
    swa = _swa_fwd(proj, sinks)
    qh, kh, vh, vth = _mla_prep(proj, gq, gkv, w_q, w_kv, tabs)
    mla, lse, gathered = _mla_fwd(qh, kh, vth, next_shards)
    x_next, y = _out_proj(x, proj, swa, mla, w_o)
    return x_next, (x, proj, h, swa, qh, kh, vh, mla, lse, y), (w_in, w_q, w_kv, w_o), gathered


def _layer_bwd(dx, dxb, saved, small, weights, tabs, pending, send_own):
    attn_g, sinks, gq, gkv = small
    w_in, w_q, w_kv, w_o = weights
    x, proj, h, swa, qh, kh, vh, mla, lse, y = saved
    d_o = _matmul_tn(y, dxb, "grad_w_out")
    o_block = d_o.reshape(N_DEV, 256, D_MODEL)
    do_a, do_b, dgate, dlt = _out_proj_bwd(dxb, proj, swa, mla, w_o)
    dqa, dkva, dsk, got_o = _swa_bwd(proj, sinks, do_a, swa, [o_block] if send_own else [])
    dqh, dkh, dvh, received = _mla_bwd(qh, kh, vh, do_b, dlt, lse, pending)
    dlat, d_q, d_kv, d_gq, d_gkv = _mla_prep_bwd(dqh, dkh, dvh, proj, gq, gkv, w_q, w_kv, tabs)
    qkv_blocks = _qkv_grad_blocks(d_q, d_kv)
    d_inp, got_qkv = _grad_w_in(h, dqa, dgate, dlat, dkva, qkv_blocks if send_own else [])
    in_block = _in_grad_blocks(d_inp)
    dx, dxb, d_attn, got_in = _in_proj_bwd(dqa, dgate, dlat, dkva, w_in, x, dx, attn_g,
                                           [in_block] if send_own else [])
    small_grads = (d_attn, dsk[0:1, 0:SWA_HEADS], d_gq, d_gkv)
    blocks = [in_block] + qkv_blocks + [o_block]
    return dx, dxb, small_grads, blocks, received, got_in + got_qkv + got_o


def _pack_small_grads(small_grads, d_final, loss):
    d_attn, d_sink, d_gq, d_gkv = zip(*small_grads)
    return jnp.concatenate([
        jnp.concatenate(d_attn, axis=0).reshape(64, 128),
        jnp.concatenate(d_gq, axis=0).reshape(12, 128),
        jnp.concatenate(d_gkv, axis=0).reshape(8, 128),
        d_final.reshape(16, 128),
        jnp.pad(jnp.concatenate(d_sink, axis=1), ((0, 0), (0, 64))),
        loss[0:1],
        jnp.zeros((PACK_ROWS - ROW_LOSS - 1, 128), F32)], axis=0)


def _pack_small(attn, qa, kva, final, sinks):
    return jnp.concatenate([
        attn.reshape(64, 128), qa.reshape(12, 128), kva.reshape(8, 128), final.reshape(16, 128),
        jnp.pad(sinks.reshape(1, 64), ((0, 0), (0, 64))),
        jnp.zeros((PACK_ROWS - ROW_SINK - 1, 128), F32)], axis=0)


def _unpack_small(p):
    return (p[ROW_ATTN:ROW_QA].reshape(DEPTH, D_MODEL), p[ROW_SINK, 0:64].reshape(DEPTH, SWA_HEADS),
            p[ROW_QA:ROW_KVA].reshape(DEPTH, Q_RANK), p[ROW_KVA:ROW_FINAL].reshape(DEPTH, KV_RANK),
            p[ROW_FINAL:ROW_SINK].reshape(D_MODEL))


def kernel(x, attn_norm_g, w_in, swa_sinks, q_a_norm_g, kv_a_norm_g, w_q_b, w_kv_b, w_out, final_norm_g, loss_target, m_attn_norm_g, m_w_in, m_swa_sinks, m_q_a_norm_g, m_kv_a_norm_g, m_w_q_b, m_w_kv_b, m_w_out, m_final_norm_g, v_attn_norm_g, v_w_in, v_swa_sinks, v_q_a_norm_g, v_kv_a_norm_g, v_w_q_b, v_w_kv_b, v_w_out, v_final_norm_g):
    xs, tgt = x[0], loss_target[0]
    tabs = _rope_tables(xs.shape[0])
    shards = [w.astype(MXU_DTYPE) for w in (w_in, w_q_b, w_kv_b, w_out)]
    shards[0] = jnp.pad(shards[0], ((0, 0), (0, 0), (0, SHARD_PAD - SHARD_COLS)))
    layer_shards = lambda l: [w[l] for w in shards]
    smalls = [(attn_norm_g[l:l + 1], swa_sinks[l], q_a_norm_g[l:l + 1], kv_a_norm_g[l:l + 1])
              for l in range(DEPTH)]

    gathered = list(_all_gather(layer_shards(0)[:1], "gather_weights")) + [None] * 3
    weights, saved = [None] * DEPTH, []
    for l in range(DEPTH):
        next_shards = layer_shards(l + 1) if l + 1 < DEPTH else []
        rest_shards = layer_shards(0)[1:] if l == 0 else []
        xs, acts, weights[l], gathered = _layer_fwd(
            xs, smalls[l], _w_in_from_gathered(gathered[0]), gathered[1:], rest_shards, tabs,
            next_shards)
        saved.append(acts)
    dx, dxb, d_final, loss = _final_loss(xs, tgt, final_norm_g.reshape(1, D_MODEL))

    received, small_grads, pending = [None] * DEPTH, [None] * DEPTH, []
    for l in reversed(range(DEPTH)):
        dx, dxb, small_grads[l], blocks, arrived, arrived_own = _layer_bwd(
            dx, dxb, saved[l], smalls[l], weights[l], tabs, pending, send_own=(l == 0))
        if pending:
            received[l + 1] = arrived
        pending = blocks
    received[0] = arrived_own
    small = _exchange_call([_pack_small_grads(small_grads, d_final, loss)], True, "gather_small")[0]

    big = []
    for a, (w, m, v, name) in enumerate(zip((w_in, w_q_b, w_kv_b, w_out),
                                            (m_w_in, m_w_q_b, m_w_kv_b, m_w_out),
                                            (v_w_in, v_w_q_b, v_w_kv_b, v_w_out),
                                            ("adamw_w_in", "adamw_w_q_b", "adamw_w_kv_b",
                                             "adamw_w_out"))):
        parts = [received[l][a] for l in range(DEPTH)]
        if a == 0:
            swap = lambda t: t.transpose(0, 2, 1)
            outs = _reduce_adamw_in(parts, swap(w), swap(m), swap(v), name)
            big.append([swap(t) for t in outs])
        else:
            cols = w.shape[-1]
            flat = lambda t: t.reshape(-1, cols)
            outs = _reduce_adamw(parts, flat(w), flat(m), flat(v), name)
            big.append([t.reshape(w.shape) for t in outs])

    sm = _reduce_adamw(
        [small],
        _pack_small(attn_norm_g, q_a_norm_g, kv_a_norm_g, final_norm_g, swa_sinks),
        _pack_small(m_attn_norm_g, m_q_a_norm_g, m_kv_a_norm_g, m_final_norm_g, m_swa_sinks),
        _pack_small(v_attn_norm_g, v_q_a_norm_g, v_kv_a_norm_g, v_final_norm_g, v_swa_sinks),
        "adamw_small")
    loss = sm[0][ROW_LOSS, 0]
    kinds = []
    for t in range(4):
        attn, sinks, qa, kva, final = _unpack_small(sm[t])
        b_in, b_q, b_kv, b_o = (big[i][t] for i in range(4))
        kinds.append((attn, b_in, sinks, qa, kva, b_q, b_kv, b_o, final))
    return (loss, dx[None], *kinds[0], *kinds[1], *kinds[2], *kinds[3])
```

```python
import jax
import jax.numpy as jnp
import numpy as np
from jax import lax
from jax.experimental import pallas as pl
from jax.experimental.pallas import tpu as pltpu

F32 = jnp.float32
BF16 = jnp.bfloat16
MXU_DTYPE = BF16
GRAD_DTYPE = BF16
PROJ_DTYPE = BF16
ATTN_DTYPE = BF16

D_MODEL = 2048
DEPTH = 4
EPS = 1e-6
NEG = -1e30
BLOCK = 128
SWA_HEADS = 16
MLA_HEADS = 8
Q_RANK = 384
KV_RANK = 256
MLA_SCALE = 192 ** -0.5
MLA_C2 = MLA_SCALE * 1.4426950408889634
ROPE_THETA = 10000.0
IN_WIDTH = 4032

ADAM_LR = 0.001
ADAM_B1 = 0.9
ADAM_B2 = 0.999
ADAM_EPS = 1e-08
ADAM_WD = 0.01
ADAM_STEP = 10

N_DEV = 8
MESH = pl.DeviceIdType.MESH

NP = 4096
QA, GA, GB, CQ, CKV, KR, KVA = 0, 1024, 2048, 3072, 3456, 3712, 3840

ROW_ATTN, ROW_QA, ROW_KVA, ROW_FINAL, ROW_SINK, ROW_LOSS, PACK_ROWS = 0, 64, 76, 84, 100, 101, 104

VMEM_LIMIT = 56 * 1024 * 1024
MLA_TILE = 512
GATHER_FORWARD_HEAD = 7


def _sds(shape, dtype):
    return jax.ShapeDtypeStruct(shape, dtype)


def _params(n_axes):
    return pltpu.CompilerParams(dimension_semantics=("arbitrary",) * n_axes,
                                vmem_limit_bytes=VMEM_LIMIT)


def _mm(a, b):
    return jnp.dot(a.astype(MXU_DTYPE), b.astype(MXU_DTYPE), preferred_element_type=F32)


def _mm_nt(a, b):
    return lax.dot_general(a.astype(MXU_DTYPE), b.astype(MXU_DTYPE),
                           (((1,), (1,)), ((), ())), preferred_element_type=F32)


def _mm_tn(a, b):
    return lax.dot_general(a.astype(MXU_DTYPE), b.astype(MXU_DTYPE),
                           (((0,), (0,)), ((), ())), preferred_element_type=F32)


def _rownorm(x):
    r = lax.rsqrt(jnp.mean(x * x, axis=-1, keepdims=True) + EPS)
    return x * r, r


def _rownorm_bwd(dxh, xh, r):
    return r * (dxh - xh * jnp.mean(dxh * xh, axis=-1, keepdims=True))


def _rope(t, c, s1, s2):
    return t * c + pltpu.roll(t, 32, 1) * s1 + pltpu.roll(t, 96, 1) * s2


SHARD_COLS = IN_WIDTH // N_DEV
SHARD_PAD = 512


def _orig_col_of_padded():
    o = np.full((NP,), -1, np.int64)
    for start, width, orig in ((QA, 1024, 0), (KVA, 256, 1024), (GA, 1024, 1280), (CQ, 384, 2304),
                               (CKV, 256, 2688), (KR, 64, 2944), (GB, 1024, 3008)):
        o[start:start + width] = np.arange(orig, orig + width)
    return o


def _device_major_src():
    o = _orig_col_of_padded()
    return np.where(o >= 0, o + (SHARD_PAD - SHARD_COLS) * (o // SHARD_COLS), -1)


def _kernel_layout_src():
    o = _orig_col_of_padded()
    where = np.full((IN_WIDTH,), -1, np.int64)
    where[o[o >= 0]] = np.nonzero(o >= 0)[0]
    e = np.arange(N_DEV * SHARD_PAD)
    k, c = e // SHARD_PAD, e % SHARD_PAD
    return np.where(c < SHARD_COLS, where[np.minimum(SHARD_COLS * k + c, IN_WIDTH - 1)], -1)


def _permute_columns(x, src_of, name, transposed=False):
    rows, n_in = x.shape
    n_out = len(src_of)
    plan, mats = [], []
    for t in range(n_out // 128):
        srcs = src_of[128 * t:128 * (t + 1)]
        entry = []
        for u in sorted(set(int(s) // 128 for s in srcs if s >= 0)):
            m = np.zeros((128, 128), np.float32)
            for c, s in enumerate(srcs):
                if s >= 0 and s // 128 == u:
                    m[s % 128, c] = 1.0
            entry.append((u, len(mats)))
            mats.append(m)
        plan.append(entry)
    tr = min(512, rows)

    def body(x_ref, p_ref, o_ref):
        for t, entry in enumerate(plan):
            acc = jnp.zeros((tr, 128), F32)
            for u, idx in entry:
                acc = acc + jnp.dot(x_ref[:, 128 * u:128 * (u + 1)], p_ref[idx],
                                    preferred_element_type=F32)
            if transposed:
                o_ref[128 * t:128 * (t + 1), :] = acc.T.astype(o_ref.dtype)
            else:
                o_ref[:, 128 * t:128 * (t + 1)] = acc.astype(o_ref.dtype)

    table = jnp.asarray(np.stack(mats), x.dtype)
    return pl.pallas_call(
        body, name=name, grid=(rows // tr,),
        in_specs=[pl.BlockSpec((tr, n_in), lambda i: (i, 0)),
                  pl.BlockSpec(table.shape, lambda i: (0, 0, 0))],
        out_specs=(pl.BlockSpec((n_out, tr), lambda i: (0, i)) if transposed
                   else pl.BlockSpec((tr, n_out), lambda i: (i, 0))),
        out_shape=_sds((n_out, rows) if transposed else (rows, n_out), x.dtype),
        compiler_params=_params(1),
    )(x, table)


def _w_in_from_gathered(g_in):
    return _permute_columns(g_in, _device_major_src(), "w_in_layout")


def _rest_from_gathered(g_qb, g_kvb, g_out):
    qb = g_qb.transpose(1, 0, 2)
    rope = jnp.pad(qb[..., 128:], ((0, 0), (0, 0), (0, 64)))
    w_q = jnp.concatenate([qb[..., :128].reshape(Q_RANK, 1024),
                           rope.reshape(Q_RANK, 1024)], axis=-1)
    kvb = g_kvb.transpose(1, 0, 2)
    w_kv = jnp.concatenate([kvb[..., :128].reshape(KV_RANK, 1024),
                            kvb[..., 128:].reshape(KV_RANK, 1024)], axis=-1)
    w_o = g_out.reshape(D_MODEL, D_MODEL)
    return w_q, w_kv, w_o


def _in_grad_blocks(d_inp):
    return _permute_columns(d_inp, _kernel_layout_src(), "grad_w_in_layout", transposed=True)


def _qkv_grad_blocks(d_q, d_kv):
    qn = d_q[:, :1024].reshape(Q_RANK, 8, 128)
    qr = d_q[:, 1024:].reshape(Q_RANK, 8, 128)[..., :64]
    b_q = jnp.concatenate([qn, qr], axis=-1).transpose(1, 0, 2)
    kn = d_kv[:, :1024].reshape(KV_RANK, 8, 128)
    vv = d_kv[:, 1024:].reshape(KV_RANK, 8, 128)
    b_kv = jnp.concatenate([kn, vv], axis=-1).transpose(1, 0, 2)
    return [b_q, b_kv]


def _rope_tables(s):
    pos = jnp.arange(s, dtype=F32)
    inv_freq = ROPE_THETA ** (-jnp.arange(0, 64, 2, dtype=F32) / 64)
    ang = pos[:, None] * inv_freq[None, :]
    cos, sin = jnp.cos(ang), jnp.sin(ang)
    z32 = jnp.zeros((s, 32), F32)
    z64 = jnp.zeros((s, 64), F32)
    c = jnp.concatenate([cos, cos, z64], axis=1)
    s1 = jnp.concatenate([z32, sin, z64], axis=1)
    s2 = jnp.concatenate([-sin, z32, z64], axis=1)
    return c, s1, s2


def _in_proj(x, g, w, shards):
    s = x.shape[0]
    tm, tn = min(512, s), 1024
    nm = s // tm
    na = len(shards)

    def body(*refs):
        x_ref, g_ref, w_ref = refs[:3]
        o_ref, h_ref = refs[3 + na:5 + na]
        i = pl.program_id(0)
        if na:
            ex = _Gather(refs[3:3 + na], refs[5 + na:5 + 2 * na], refs[5 + 2 * na:])

            @pl.when(i == 0)
            def _():
                ex.start()

            @pl.when(i == (3 * nm) // 4)
            def _():
                ex.forward()

        xh, _ = _rownorm(x_ref[...])
        h_ref[...] = (xh * g_ref[...]).astype(h_ref.dtype)
        for j in range(NP // tn):
            cols = slice(j * tn, (j + 1) * tn)
            o_ref[:, cols] = jnp.dot(h_ref[...], w_ref[:, cols],
                                     preferred_element_type=F32).astype(o_ref.dtype)

        if na:
            @pl.when(i == nm - 1)
            def _():
                ex.finish()

    row = lambda i: (i, 0)
    fixed = lambda i: (0, 0)
    outs = pl.pallas_call(
        body, name="in_proj_gather" if na else "in_proj", grid=(nm,),
        in_specs=[pl.BlockSpec((tm, D_MODEL), row), pl.BlockSpec((1, D_MODEL), fixed),
                  pl.BlockSpec((D_MODEL, NP), fixed, pipeline_mode=pl.Buffered(1))]
        + [HBM_SPEC] * na,
        out_specs=[pl.BlockSpec((tm, NP), row), pl.BlockSpec((tm, D_MODEL), row)]
        + [HBM_SPEC] * na,
        out_shape=[_sds((s, NP), PROJ_DTYPE), _sds((s, D_MODEL), MXU_DTYPE)]
        + _exchange_shapes(shards, True),
        scratch_shapes=_Exchange.semaphores(na) if na else [],
        compiler_params=_params(1),
    )(x, g, w, *shards)
    return outs[0], outs[1], list(outs[2:])


def _swa_slopes():
    return [2.0 ** (-8.0 * (h + 1) / SWA_HEADS) for h in range(SWA_HEADS)]


SWA_STACK = SWA_HEADS * BLOCK


def _swa_head(b):
    j, a = b // 8, b % 8
    return 2 * (4 * j + a % 4) + a // 4


def _swa_kv(kv_p, kv_c):
    return (jnp.concatenate([kv_p[:, :128], kv_c[:, :128]], axis=0),
            jnp.concatenate([kv_p[:, 128:], kv_c[:, 128:]], axis=0))


def _swa_by_block(vals):
    b = lax.broadcasted_iota(jnp.int32, (1, SWA_STACK), 1) >> 7
    row = jnp.full((1, SWA_STACK), vals[-1], F32)
    for t in range(len(vals) - 2, -1, -1):
        row = jnp.where(b == t, vals[t], row)
    return row


def _swa_bias(n):
    slopes = _swa_slopes()
    ki = lax.broadcasted_iota(jnp.int32, (2 * BLOCK, SWA_STACK), 0)
    r = lax.broadcasted_iota(jnp.int32, (2 * BLOCK, SWA_STACK), 1)
    delta = BLOCK + (r & (BLOCK - 1)) - ki
    valid = (delta >= 0) & (delta < BLOCK) & ((n - 1) * BLOCK + ki >= 0)
    slope = _swa_by_block([slopes[_swa_head(b)] for b in range(SWA_HEADS)])
    return jnp.where(valid, -slope * delta.astype(F32), NEG)


def _swa_fill_bias(n, bias_ref):
    @pl.when(n <= 1)
    def _():
        bias_ref[...] = _swa_bias(n)


def _swa_sink_row(sink_ref):
    return _swa_by_block([sink_ref[_swa_head(b)] for b in range(SWA_HEADS)])


def _swa_pairs(ref, j):
    return jnp.concatenate([ref[:, 128 * (4 * j + a):128 * (4 * j + a + 1)] for a in range(4)],
                           axis=0).astype(F32)


def _swa_stack(ref):
    left = lax.broadcasted_iota(jnp.int32, (8 * BLOCK, 128), 1) < 64
    p0, p1 = _swa_pairs(ref, 0), _swa_pairs(ref, 1)
    s0 = jnp.concatenate([p0, pltpu.roll(p0, 64, 1)], axis=0)
    s1 = jnp.concatenate([pltpu.roll(p1, 64, 1), p1], axis=0)
    return jnp.concatenate([jnp.where(left, s0, 0.0), jnp.where(left, 0.0, s1)], axis=0)


def _swa_unstack(t):
    left = lax.broadcasted_iota(jnp.int32, (4 * BLOCK, 128), 1) < 64
    n = 4 * BLOCK
    g0 = jnp.where(left, t[0:n], pltpu.roll(t[n:2 * n], 64, 1))
    g1 = jnp.where(left, pltpu.roll(t[2 * n:3 * n], 64, 1), t[3 * n:4 * n])
    return g0, g1


def _swa_softmax(qs, kk, bias, sink):
    sc = _mm_nt(kk, qs) + bias
    m = jnp.maximum(jnp.max(sc, axis=0, keepdims=True), sink)
    ex = jnp.exp(sc - m)
    es = jnp.exp(sink - m)
    return ex, es, 1.0 / (jnp.sum(ex, axis=0, keepdims=True) + es)


def _swa_fwd(proj, sinks):
    s = proj.shape[0]
    nb = s // BLOCK

    def body(sink_ref, q_ref, kp_ref, kc_ref, o_ref, bias_ref):
        n = pl.program_id(0)
        _swa_fill_bias(n, bias_ref)
        kk, vv = _swa_kv(kp_ref[...].astype(F32), kc_ref[...].astype(F32))
        qs = _swa_stack(q_ref) * 0.125
        ex, _, inv = _swa_softmax(qs, kk, bias_ref[...], _swa_sink_row(sink_ref))
        o_t = _mm(vv.T, ex) * inv
        n4 = 4 * BLOCK
        for j in range(2):
            rows = slice(64 * j, 64 * (j + 1))
            out = jnp.concatenate([o_t[rows, 2 * j * n4:(2 * j + 1) * n4],
                                   o_t[rows, (2 * j + 1) * n4:(2 * j + 2) * n4]], axis=0).T
            for a in range(4):
                o_ref[:, 128 * (4 * j + a):128 * (4 * j + a + 1)] = out[128 * a:128 * (a + 1)].astype(
                    o_ref.dtype)

    return pl.pallas_call(
        body, name="swa_fwd", grid=(nb,),
        in_specs=[pl.BlockSpec(memory_space=pltpu.SMEM),
                  pl.BlockSpec((BLOCK, 1024), lambda n: (n, 0)),
                  pl.BlockSpec((BLOCK, 256), lambda n: (jnp.maximum(n - 1, 0), KVA // 256)),
                  pl.BlockSpec((BLOCK, 256), lambda n: (n, KVA // 256))],
        out_specs=pl.BlockSpec((BLOCK, 1024), lambda n: (n, 0)),
        out_shape=_sds((s, 1024), ATTN_DTYPE),
        scratch_shapes=[pltpu.VMEM((2 * BLOCK, SWA_STACK), F32)],
        compiler_params=_params(1),
    )(sinks, proj, proj, proj)


def _mla_prep(proj, gq, gkv, w_q, w_kv, tabs):
    s = proj.shape[0]
    tm = min(512, s)
    c, s1, s2 = tabs

    def body(p_ref, gq_ref, gkv_ref, wq_ref, wkv_ref, c_ref, s1_ref, s2_ref,
             q_ref, k_ref, v_ref, vt_ref):
        cqh, _ = _rownorm(p_ref[:, 0:384].astype(F32))
        ckvh, _ = _rownorm(p_ref[:, 384:640].astype(F32))
        q = _mm(cqh * gq_ref[...], wq_ref[...])
        kv = _mm(ckvh * gkv_ref[...], wkv_ref[...])
        cc, ss1, ss2 = c_ref[...], s1_ref[...], s2_ref[...]
        krr = _rope(p_ref[:, 640:768].astype(F32), cc, ss1, ss2).astype(k_ref.dtype)
        for h in range(MLA_HEADS):
            q_ref[h, :, 0:128] = q[:, 128 * h:128 * (h + 1)].astype(q_ref.dtype)
            q_ref[h, :, 128:256] = _rope(q[:, 1024 + 128 * h:1024 + 128 * (h + 1)],
                                         cc, ss1, ss2).astype(q_ref.dtype)
            k_ref[h, :, 0:128] = kv[:, 128 * h:128 * (h + 1)].astype(k_ref.dtype)
            k_ref[h, :, 128:256] = krr
            vv = kv[:, 1024 + 128 * h:1024 + 128 * (h + 1)]
            v_ref[h] = vv.astype(v_ref.dtype)
            vt_ref[h, 0:128, :] = vv.T.astype(vt_ref.dtype)
            vt_ref[h, 128:256, :] = jnp.ones((128, tm), vt_ref.dtype)

    row = lambda i: (i, 0)
    fixed = lambda i: (0, 0)
    return pl.pallas_call(
        body, name="mla_prep", grid=(s // tm,),
        in_specs=[pl.BlockSpec((tm, 768), lambda i: (i, CQ // 768)),
                  pl.BlockSpec((1, Q_RANK), fixed), pl.BlockSpec((1, KV_RANK), fixed),
                  pl.BlockSpec((Q_RANK, 2048), fixed), pl.BlockSpec((KV_RANK, 2048), fixed),
                  pl.BlockSpec((tm, 128), row), pl.BlockSpec((tm, 128), row),
                  pl.BlockSpec((tm, 128), row)],
        out_specs=[pl.BlockSpec((MLA_HEADS, tm, 256), lambda i: (0, i, 0)),
                   pl.BlockSpec((MLA_HEADS, tm, 256), lambda i: (0, i, 0)),
                   pl.BlockSpec((MLA_HEADS, tm, 128), lambda i: (0, i, 0)),
                   pl.BlockSpec((MLA_HEADS, 256, tm), lambda i: (0, 0, i))],
        out_shape=[_sds((MLA_HEADS, s, 256), MXU_DTYPE), _sds((MLA_HEADS, s, 256), MXU_DTYPE),
                   _sds((MLA_HEADS, s, 128), MXU_DTYPE), _sds((MLA_HEADS, 256, s), MXU_DTYPE)],
        compiler_params=_params(1),
    )(proj, gq, gkv, w_q, w_kv, c, s1, s2)


def _scores_t(k, q, t, diagonal):
    sc = _mm_nt(k, q)
    if diagonal:
        key = lax.broadcasted_iota(jnp.int32, sc.shape, 0)
        query = lax.broadcasted_iota(jnp.int32, sc.shape, 1)
        sc = jnp.where(key <= query, sc, NEG)
    return sc


def _mla_fwd(qh, kh, vth, shards):
    s = qh.shape[1]
    t = min(MLA_TILE, s)
    nq = s // t
    na = len(shards)

    def body(*refs):
        q_ref, k_ref, vt_ref = refs[:3]
        o_ref, lse_ref = refs[3 + na:5 + na]
        m_ref, acc_ref = refs[5 + 2 * na:7 + 2 * na]
        h, i = pl.program_id(0), pl.program_id(1)
        if na:
            ex = _Gather(refs[3:3 + na], refs[5 + na:5 + 2 * na], refs[7 + 2 * na:])

            @pl.when((h == 0) & (i == 0))
            def _():
                ex.start()

            @pl.when((h == GATHER_FORWARD_HEAD) & (i == 0))
            def _():
                ex.forward()

        m_ref[...] = jnp.full(m_ref.shape, NEG, F32)
        acc_ref[...] = jnp.zeros(acc_ref.shape, F32)

        def step(start, width):
            keys = pl.ds(pl.multiple_of(start, t), width)
            sc = _mm_nt(k_ref[keys, :], q_ref[...])
            m_prev = m_ref[...]
            m_new = jnp.maximum(m_prev, jnp.max(sc, axis=0, keepdims=True))
            alpha = jnp.exp2((m_prev - m_new) * MLA_C2)
            p = jnp.exp2((sc - m_new[0:1, :]) * MLA_C2)
            acc_ref[...] = alpha[0:1, :] * acc_ref[...] + _mm(vt_ref[:, keys], p)
            m_ref[...] = m_new

        def diagonal_step():
            half = t // 2
            k_lo = pl.ds(pl.multiple_of(i * t, half), half)
            k_hi = pl.ds(pl.multiple_of(i * t + half, half), half)
            sc_lo = _scores_t(k_ref[k_lo, :], q_ref[...], t, True)
            sc_hi = _scores_t(k_ref[k_hi, :], q_ref[half:t, :], t, True)
            m_prev = m_ref[...]
            m_lo = jnp.maximum(m_prev, jnp.max(sc_lo, axis=0, keepdims=True))
            m_new = jnp.concatenate(
                [m_lo[:, 0:half],
                 jnp.maximum(m_lo[:, half:t], jnp.max(sc_hi, axis=0, keepdims=True))], axis=1)
            alpha = jnp.exp2((m_prev - m_new) * MLA_C2)
            p_lo = jnp.exp2((sc_lo - m_new[0:1, :]) * MLA_C2)
            p_hi = jnp.exp2((sc_hi - m_new[0:1, half:t]) * MLA_C2)
            acc_ref[...] = alpha[0:1, :] * acc_ref[...] + _mm(vt_ref[:, k_lo], p_lo)
            acc_ref[:, half:t] += _mm(vt_ref[:, k_hi], p_hi)
            m_ref[...] = m_new

        def below_diagonal(jj, carry):
            step(jj * (4 * t), 4 * t)
            return carry

        lax.fori_loop(0, i // 4, below_diagonal, 0)

        @pl.when(i % 4 >= 2)
        def _():
            step((i // 4) * (4 * t), 2 * t)

        @pl.when(i % 2 == 1)
        def _():
            step((i - 1) * t, t)

        diagonal_step()
        l = acc_ref[128:136, :]
        o_ref[...] = (acc_ref[0:128, :] / l[0:1, :]).T.astype(o_ref.dtype)
        lse_ref[...] = m_ref[...] * MLA_C2 + jnp.log2(l)

        if na:
            @pl.when((h == MLA_HEADS - 1) & (i == nq - 1))
            def _():
                ex.finish()

    head = lambda h, i: (h, 0, 0)
    outs = pl.pallas_call(
        body, name="mla_fwd_gather" if na else "mla_fwd", grid=(MLA_HEADS, nq),
        in_specs=[pl.BlockSpec((None, t, 256), lambda h, i: (h, i, 0)),
                  pl.BlockSpec((None, s, 256), head),
                  pl.BlockSpec((None, 256, s), head)] + [HBM_SPEC] * na,
        out_specs=[pl.BlockSpec((t, 128), lambda h, i: (i, h)),
                   pl.BlockSpec((None, 8, t), lambda h, i: (h, 0, i))] + [HBM_SPEC] * na,
        out_shape=[_sds((s, 1024), ATTN_DTYPE), _sds((MLA_HEADS, 8, s), F32)]
        + _exchange_shapes(shards, True),
        scratch_shapes=[pltpu.VMEM((8, t), F32), pltpu.VMEM((256, t), F32)]
        + (_Exchange.semaphores(na) if na else []),
        compiler_params=_params(2),
    )(qh, kh, vth, *shards)
    return outs[0], outs[1], list(outs[2:])


def _silu_parts(g):
    sg = jax.nn.sigmoid(g)
    return g * sg, sg * (1.0 + g * (1.0 - sg))


def _out_proj(x, proj, swa, mla, w_out):
    s = x.shape[0]
    tm = min(512, s)

    def body(x_ref, ga_ref, gb_ref, a_ref, b_ref, w_ref, xo_ref, y_ref):
        sa, _ = _silu_parts(ga_ref[...].astype(F32))
        sb, _ = _silu_parts(gb_ref[...].astype(F32))
        y_ref[:, 0:1024] = (a_ref[...].astype(F32) * sa).astype(y_ref.dtype)
        y_ref[:, 1024:2048] = (b_ref[...].astype(F32) * sb).astype(y_ref.dtype)
        xo_ref[...] = x_ref[...] + jnp.dot(y_ref[...], w_ref[...], preferred_element_type=F32)

    row = lambda i: (i, 0)
    return pl.pallas_call(
        body, name="out_proj", grid=(s // tm,),
        in_specs=[pl.BlockSpec((tm, D_MODEL), row),
                  pl.BlockSpec((tm, 1024), lambda i: (i, GA // 1024)),
                  pl.BlockSpec((tm, 1024), lambda i: (i, GB // 1024)),
                  pl.BlockSpec((tm, 1024), row), pl.BlockSpec((tm, 1024), row),
                  pl.BlockSpec((D_MODEL, D_MODEL), lambda i: (0, 0), pipeline_mode=pl.Buffered(1))],
        out_specs=[pl.BlockSpec((tm, D_MODEL), row), pl.BlockSpec((tm, D_MODEL), row)],
        out_shape=[_sds((s, D_MODEL), F32), _sds((s, D_MODEL), MXU_DTYPE)],
        compiler_params=_params(1),
    )(x, proj, proj, swa, mla, w_out)


def _final_loss(x, tgt, g):
    s = x.shape[0]
    tm = min(512, s)

    def body(x_ref, t_ref, g_ref, dx_ref, dxb_ref, dg_ref, loss_ref):
        @pl.when(pl.program_id(0) == 0)
        def _():
            dg_ref[...] = jnp.zeros(dg_ref.shape, F32)
            loss_ref[...] = jnp.zeros(loss_ref.shape, F32)
        xh, r = _rownorm(x_ref[...])
        gg = g_ref[...]
        err = xh * gg - t_ref[...]
        per_row = jnp.mean(err * err, axis=-1, keepdims=True)
        loss_ref[...] += 0.5 * jnp.sum(per_row, axis=0, keepdims=True)
        dy = err * (1.0 / D_MODEL)
        dg_ref[...] += jnp.sum(dy * xh, axis=0, keepdims=True)
        dx = _rownorm_bwd(dy * gg, xh, r)
        dx_ref[...] = dx
        dxb_ref[...] = dx.astype(dxb_ref.dtype)

    row = lambda i: (i, 0)
    fixed = lambda i: (0, 0)
    return pl.pallas_call(
        body, name="final_loss", grid=(s // tm,),
        in_specs=[pl.BlockSpec((tm, D_MODEL), row), pl.BlockSpec((tm, D_MODEL), row),
                  pl.BlockSpec((1, D_MODEL), fixed)],
        out_specs=[pl.BlockSpec((tm, D_MODEL), row), pl.BlockSpec((tm, D_MODEL), row),
                   pl.BlockSpec((1, D_MODEL), fixed), pl.BlockSpec((8, 128), fixed)],
        out_shape=[_sds((s, D_MODEL), F32), _sds((s, D_MODEL), MXU_DTYPE), _sds((1, D_MODEL), F32),
                   _sds((8, 128), F32)],
        compiler_params=_params(1),
    )(x, tgt, g)


def _out_proj_bwd(dx, proj, swa, mla, w_out):
    s = dx.shape[0]
    tm = min(512, s)

    def body(dx_ref, ga_ref, gb_ref, a_ref, b_ref, w_ref, doa_ref, dob_ref, dg_ref, dlt_ref):
        dx = dx_ref[...].astype(MXU_DTYPE)
        dya = _mm_nt(dx, w_ref[0:1024, :])
        sa, dsa = _silu_parts(ga_ref[...].astype(F32))
        doa_ref[...] = dya * sa
        dg_ref[:, 0:1024] = (dya * a_ref[...].astype(F32) * dsa).astype(dg_ref.dtype)
        dyb = _mm_nt(dx, w_ref[1024:2048, :])
        sb, dsb = _silu_parts(gb_ref[...].astype(F32))
        b = b_ref[...].astype(F32)
        dob = dyb * sb
        dob_ref[...] = dob.astype(dob_ref.dtype)
        dg_ref[:, 1024:2048] = (dyb * b * dsb).astype(dg_ref.dtype)
        prod = dob * b
        for h in range(MLA_HEADS):
            dlt = jnp.sum(prod[:, 128 * h:128 * (h + 1)], axis=1, keepdims=True)
            dlt_ref[h] = jnp.broadcast_to(dlt, (tm, 128)).T[0:8, :]

    row = lambda i: (i, 0)
    return pl.pallas_call(
        body, name="out_proj_bwd", grid=(s // tm,),
        in_specs=[pl.BlockSpec((tm, D_MODEL), row),
                  pl.BlockSpec((tm, 1024), lambda i: (i, GA // 1024)),
                  pl.BlockSpec((tm, 1024), lambda i: (i, GB // 1024)),
                  pl.BlockSpec((tm, 1024), row), pl.BlockSpec((tm, 1024), row),
                  pl.BlockSpec((D_MODEL, D_MODEL), lambda i: (0, 0), pipeline_mode=pl.Buffered(1))],
        out_specs=[pl.BlockSpec((tm, 1024), row), pl.BlockSpec((tm, 1024), row),
                   pl.BlockSpec((tm, D_MODEL), row),
                   pl.BlockSpec((MLA_HEADS, 8, tm), lambda i: (0, 0, i))],
        out_shape=[_sds((s, 1024), F32), _sds((s, 1024), MXU_DTYPE), _sds((s, D_MODEL), MXU_DTYPE),
                   _sds((MLA_HEADS, 8, s), F32)],
        compiler_params=_params(1),
    )(dx, proj, proj, swa, mla, w_out)


def _matmul_tn(a, b, name):
    s, m = a.shape
    n = b.shape[1]
    tm, tn, tk = min(1024, m), min(1024, n), min(2048, s)
    nk = s // tk

    def body(a_ref, b_ref, o_ref, acc_ref):
        k = pl.program_id(2)

        @pl.when(k == 0)
        def _():
            acc_ref[...] = jnp.zeros(acc_ref.shape, F32)
        acc_ref[...] += _mm_tn(a_ref[...], b_ref[...])

        @pl.when(k == nk - 1)
        def _():
            o_ref[...] = acc_ref[...].astype(o_ref.dtype)

    return pl.pallas_call(
        body, name=name, grid=(m // tm, n // tn, nk),
        in_specs=[pl.BlockSpec((tk, tm), lambda i, j, k: (k, i)),
                  pl.BlockSpec((tk, tn), lambda i, j, k: (k, j))],
        out_specs=pl.BlockSpec((tm, tn), lambda i, j, k: (i, j)),
        out_shape=_sds((m, n), GRAD_DTYPE),
        scratch_shapes=[pltpu.VMEM((tm, tn), F32)],
        compiler_params=_params(3),
    )(a, b)


def _grad_w_in(h, dqa, dgate, dlat, dkva, blocks):
    s = h.shape[0]
    tm, tn, tk = 1024, 1024, min(2048, s)
    nk = s // tk
    grid = (D_MODEL // tm, NP // tn, nk)
    na = len(blocks)

    def body(*refs):
        a_ref, dqa_ref, dg8_ref, dlat_ref, dkva_ref = refs[:5]
        o_ref = refs[5 + na]
        acc_ref = refs[6 + 2 * na]
        i, j, k = pl.program_id(0), pl.program_id(1), pl.program_id(2)
        if na:
            ex = _Exchange(refs[5:5 + na], refs[6 + na:6 + 2 * na], refs[7 + 2 * na:], gather=False)

            @pl.when((i == 0) & (j == 0) & (k == 0))
            def _():
                ex.start()

        @pl.when(k == 0)
        def _():
            acc_ref[...] = jnp.zeros(acc_ref.shape, F32)

        @pl.when(j == QA // tn)
        def _():
            acc_ref[...] += _mm_tn(a_ref[...], dqa_ref[...])

        @pl.when((j == GA // tn) | (j == GB // tn))
        def _():
            acc_ref[...] += _mm_tn(a_ref[...], dg8_ref[...])

        @pl.when(j == CQ // tn)
        def _():
            acc_ref[:, 0:768] += _mm_tn(a_ref[...], dlat_ref[...])
            acc_ref[:, 768:1024] += _mm_tn(a_ref[...], dkva_ref[...])

        @pl.when(k == nk - 1)
        def _():
            o_ref[...] = acc_ref[...].astype(o_ref.dtype)

        if na:
            @pl.when((i == grid[0] - 1) & (j == grid[1] - 1) & (k == nk - 1))
            def _():
                ex.wait()

    def when(group, width):
        return pl.BlockSpec((tk, width), lambda i, j, k: (jnp.where(j == group, k, 0), 0))

    outs = pl.pallas_call(
        body, name="grad_w_in_scatter" if na else "grad_w_in", grid=grid,
        in_specs=[pl.BlockSpec((tk, tm), lambda i, j, k: (k, i)),
                  when(QA // tn, 1024),
                  pl.BlockSpec((tk, 1024), lambda i, j, k: (
                      jnp.where((j == GA // tn) | (j == GB // tn), k, 0),
                      jnp.clip(j - GA // tn, 0, 1))),
                  when(CQ // tn, 768), when(CQ // tn, 256)] + [HBM_SPEC] * na,
        out_specs=[pl.BlockSpec((tm, tn), lambda i, j, k: (i, j))] + [HBM_SPEC] * na,
        out_shape=[_sds((D_MODEL, NP), GRAD_DTYPE)] + _exchange_shapes(blocks, False),
        scratch_shapes=[pltpu.VMEM((tm, tn), F32)] + (_Exchange.semaphores(na) if na else []),
        compiler_params=_params(3),
    )(h, dqa, dgate, dlat, dkva, *blocks)
    return outs[0], list(outs[1:])


def _swa_bwd(proj, sinks, do, o, blocks):
    s = proj.shape[0]
    nb = s // BLOCK
    na = len(blocks)

    def body(*refs):
        sink_ref, q_ref, kp_ref, kc_ref, do_ref, o_ref = refs[:6]
        dq_ref, dkv_ref, dsink_ref = refs[6 + na:9 + na]
        carry_ref, bias_ref = refs[9 + 2 * na:11 + 2 * na]
        n = pl.program_id(0)
        _swa_fill_bias(n, bias_ref)
        if na:
            ex = _Exchange(refs[6:6 + na], refs[9 + na:9 + 2 * na], refs[11 + 2 * na:], gather=False)

        @pl.when(n == 0)
        def _():
            carry_ref[...] = jnp.zeros(carry_ref.shape, F32)
            dsink_ref[...] = jnp.zeros(dsink_ref.shape, F32)
            if na:
                ex.start()

        @pl.when(n < nb)
        def _():
            kk, vv = _swa_kv(kp_ref[...].astype(F32), kc_ref[...].astype(F32))
            lane_s = lax.broadcasted_iota(jnp.int32, (8, 128), 1)
            qs = _swa_stack(q_ref) * 0.125
            dos = _swa_stack(do_ref)
            dlt = []
            for j in range(2):
                prod_t = (_swa_pairs(do_ref, j) * _swa_pairs(o_ref, j)).T
                dlt += [jnp.sum(prod_t[0:64], axis=0, keepdims=True),
                        jnp.sum(prod_t[64:128], axis=0, keepdims=True)]
            dlt = jnp.concatenate(dlt, axis=1)
            ex, es, inv = _swa_softmax(qs, kk, bias_ref[...], _swa_sink_row(sink_ref))
            p = ex * inv
            ds = p * (_mm_nt(vv, dos) - dlt)
            sink_term = es * inv * dlt
            dsink = jnp.zeros((8, 128), F32)
            for b in range(SWA_HEADS):
                dsh = -jnp.sum(sink_term[:, 128 * b:128 * (b + 1)], axis=1, keepdims=True)
                dsink = dsink + jnp.where(lane_s == _swa_head(b), dsh, 0.0)
            for j, dq in enumerate(_swa_unstack(_mm_tn(ds, kk))):
                for a in range(4):
                    cols = slice(128 * (4 * j + a), 128 * (4 * j + a + 1))
                    dq_ref[:, cols] = (dq[128 * a:128 * (a + 1)] * 0.125).astype(dq_ref.dtype)
            contrib = jnp.concatenate([_mm(ds, qs), _mm(p, dos)], axis=1)
            dkv_ref[...] = (carry_ref[...] + contrib[0:BLOCK]).astype(dkv_ref.dtype)
            carry_ref[...] = contrib[BLOCK:2 * BLOCK]
            dsink_ref[...] += dsink

        @pl.when(n == nb)
        def _():
            dkv_ref[...] = carry_ref[...].astype(dkv_ref.dtype)
            if na:
                ex.wait()

    cur = lambda n: (jnp.minimum(n, nb - 1), 0)
    outs = pl.pallas_call(
        body, name="swa_bwd_scatter" if na else "swa_bwd", grid=(nb + 1,),
        in_specs=[pl.BlockSpec(memory_space=pltpu.SMEM),
                  pl.BlockSpec((BLOCK, 1024), cur),
                  pl.BlockSpec((BLOCK, 256), lambda n: (jnp.clip(n - 1, 0, nb - 1), KVA // 256)),
                  pl.BlockSpec((BLOCK, 256), lambda n: (jnp.minimum(n, nb - 1), KVA // 256)),
                  pl.BlockSpec((BLOCK, 1024), cur), pl.BlockSpec((BLOCK, 1024), cur)]
        + [HBM_SPEC] * na,
        out_specs=[pl.BlockSpec((BLOCK, 1024), cur),
                   pl.BlockSpec((BLOCK, 256), lambda n: (jnp.maximum(n - 1, 0), 0)),
                   pl.BlockSpec((8, 128), lambda n: (0, 0))] + [HBM_SPEC] * na,
        out_shape=[_sds((s, 1024), MXU_DTYPE), _sds((s, 256), MXU_DTYPE), _sds((8, 128), F32)]
        + _exchange_shapes(blocks, False),
        scratch_shapes=[pltpu.VMEM((BLOCK, 256), F32), pltpu.VMEM((2 * BLOCK, SWA_STACK), F32)]
        + (_Exchange.semaphores(na) if na else []),
        compiler_params=_params(1),
    )(sinks, proj, proj, proj, do, o, *blocks)
    return outs[0], outs[1], outs[2], list(outs[3:])


def _mla_bwd(qh, kh, vh, do, dlt, lse, blocks):
    s = qh.shape[1]
    t = min(MLA_TILE, s)
    nq = s // t

    na = len(blocks)

    def body(*refs):
        q_ref, k_ref, v_ref, do_ref, dlt_ref, lse_ref = refs[:6]
        dq_ref, dk_ref, dv_ref = refs[6 + na:9 + na]
        h, j = pl.program_id(0), pl.program_id(1)
        if na:
            ex = _Exchange(refs[6:6 + na], refs[9 + na:9 + 2 * na], refs[9 + 2 * na:], gather=False)

            @pl.when((h == 0) & (j == 0))
            def _():
                ex.start()

        def step(start, width, diagonal=False):
            rows = pl.ds(pl.multiple_of(start, t), width)
            q, k, dout = q_ref[rows, :], k_ref[...], do_ref[rows, :]
            sc = _scores_t(k, q, t, diagonal)
            p = jnp.exp2(sc * MLA_C2 - lse_ref[0:1, rows])
            dv = _mm(p, dout)
            ds = p * (_mm_nt(v_ref[...], dout) - dlt_ref[0:1, rows])
            dk = _mm(ds, q)
            dq = _mm_tn(ds, k)
            return rows, dq, dk, dv

        rows, dq, dk, dv = step(j * t, t, diagonal=True)
        dk_ref[...] = dk
        dv_ref[...] = dv

        @pl.when(j == 0)
        def _():
            dq_ref[rows, :] = dq * MLA_SCALE

        @pl.when(j > 0)
        def _():
            dq_ref[rows, :] = (dq_ref[rows, :] + dq) * MLA_SCALE

        def above_diagonal(start, width):
            rows, dq, dk, dv = step(start, width)
            dk_ref[...] += dk
            dv_ref[...] += dv

            @pl.when(j == 0)
            def _():
                dq_ref[rows, :] = dq

            @pl.when(j > 0)
            def _():
                dq_ref[rows, :] += dq

        n_above = nq - 1 - j

        def quad(jj, carry):
            above_diagonal((j + 1 + 4 * jj) * t, 4 * t)
            return carry

        lax.fori_loop(0, n_above // 4, quad, 0)

        @pl.when(n_above % 4 >= 2)
        def _():
            above_diagonal((j + 1 + 4 * (n_above // 4)) * t, 2 * t)

        @pl.when(n_above % 2 == 1)
        def _():
            above_diagonal((nq - 1) * t, t)

        dk_ref[...] *= MLA_SCALE

        if na:
            @pl.when((h == MLA_HEADS - 1) & (j == nq - 1))
            def _():
                ex.wait()

    head = lambda h, j: (h, 0, 0)
    kv_map = lambda h, j: (h, j, 0)
    outs = pl.pallas_call(
        body, name="mla_bwd_scatter" if na else "mla_bwd", grid=(MLA_HEADS, nq),
        in_specs=[pl.BlockSpec((None, s, 256), head), pl.BlockSpec((None, t, 256), kv_map),
                  pl.BlockSpec((None, t, 128), kv_map),
                  pl.BlockSpec((s, 128), lambda h, j: (0, h)),
                  pl.BlockSpec((None, 8, s), head), pl.BlockSpec((None, 8, s), head)]
        + [HBM_SPEC] * na,
        out_specs=[pl.BlockSpec((None, s, 256), head),
                   pl.BlockSpec((None, t, 256), kv_map), pl.BlockSpec((None, t, 128), kv_map)]
        + [HBM_SPEC] * na,
        out_shape=[_sds((MLA_HEADS, s, 256), F32), _sds((MLA_HEADS, s, 256), F32),
                   _sds((MLA_HEADS, s, 128), F32)] + _exchange_shapes(blocks, False),
        scratch_shapes=_Exchange.semaphores(na) if na else [],
        compiler_params=_params(2),
    )(qh, kh, vh, do, dlt, lse, *blocks)
    return outs[0], outs[1], outs[2], list(outs[3:])


def _mla_prep_bwd(dqh, dkh, dvh, proj, gq, gkv, w_q, w_kv, tabs):
    s = proj.shape[0]
    tm = min(256, s)
    nm = s // tm
    c, s1, s2 = tabs

    def body(dq_ref, dk_ref, dv_ref, p_ref, gq_ref, gkv_ref, wq_ref, wkv_ref,
             c_ref, s1_ref, s2_ref, dp_ref, dwq_ref, dwkv_ref, dgq_ref, dgkv_ref,
             dqf_ref, dkvf_ref, dwq_acc, dwkv_acc):
        @pl.when(pl.program_id(0) == 0)
        def _():
            dgq_ref[...] = jnp.zeros(dgq_ref.shape, F32)
            dgkv_ref[...] = jnp.zeros(dgkv_ref.shape, F32)
            dwq_acc[...] = jnp.zeros(dwq_acc.shape, F32)
            dwkv_acc[...] = jnp.zeros(dwkv_acc.shape, F32)
        cc, ns1, ns2 = c_ref[...], -s1_ref[...], -s2_ref[...]
        dkr = jnp.zeros((tm, 128), F32)
        for h in range(MLA_HEADS):
            dqf_ref[:, 128 * h:128 * (h + 1)] = dq_ref[h, :, 0:128].astype(dqf_ref.dtype)
            dqf_ref[:, 1024 + 128 * h:1024 + 128 * (h + 1)] = _rope(
                dq_ref[h, :, 128:256], cc, ns1, ns2).astype(dqf_ref.dtype)
            dkvf_ref[:, 128 * h:128 * (h + 1)] = dk_ref[h, :, 0:128].astype(dkvf_ref.dtype)
            dkvf_ref[:, 1024 + 128 * h:1024 + 128 * (h + 1)] = dv_ref[h].astype(dkvf_ref.dtype)
            dkr = dkr + dk_ref[h, :, 128:256]
        dcqn = _mm_nt(dqf_ref[...], wq_ref[...])
        dckvn = _mm_nt(dkvf_ref[...], wkv_ref[...])
        cqh, rq = _rownorm(p_ref[:, 0:384].astype(F32))
        ckvh, rkv = _rownorm(p_ref[:, 384:640].astype(F32))
        dgq_ref[...] += jnp.sum(dcqn * cqh, axis=0, keepdims=True)
        dgkv_ref[...] += jnp.sum(dckvn * ckvh, axis=0, keepdims=True)
        dp_ref[:, 0:384] = _rownorm_bwd(dcqn * gq_ref[...], cqh, rq).astype(dp_ref.dtype)
        dp_ref[:, 384:640] = _rownorm_bwd(dckvn * gkv_ref[...], ckvh, rkv).astype(dp_ref.dtype)
        dp_ref[:, 640:768] = _rope(dkr, cc, ns1, ns2).astype(dp_ref.dtype)
        dwq_acc[...] += _mm_tn(cqh * gq_ref[...], dqf_ref[...])
        dwkv_acc[...] += _mm_tn(ckvh * gkv_ref[...], dkvf_ref[...])

        @pl.when(pl.program_id(0) == nm - 1)
        def _():
            dwq_ref[...] = dwq_acc[...].astype(dwq_ref.dtype)
            dwkv_ref[...] = dwkv_acc[...].astype(dwkv_ref.dtype)

    row = lambda i: (i, 0)
    fixed = lambda i: (0, 0)
    head = lambda i: (0, i, 0)
    return pl.pallas_call(
        body, name="mla_prep_bwd", grid=(nm,),
        in_specs=[pl.BlockSpec((MLA_HEADS, tm, 256), head), pl.BlockSpec((MLA_HEADS, tm, 256), head),
                  pl.BlockSpec((MLA_HEADS, tm, 128), head),
                  pl.BlockSpec((tm, 768), lambda i: (i, CQ // 768)),
                  pl.BlockSpec((1, Q_RANK), fixed), pl.BlockSpec((1, KV_RANK), fixed),
                  pl.BlockSpec((Q_RANK, 2048), fixed), pl.BlockSpec((KV_RANK, 2048), fixed),
                  pl.BlockSpec((tm, 128), row), pl.BlockSpec((tm, 128), row),
                  pl.BlockSpec((tm, 128), row)],
        out_specs=[pl.BlockSpec((tm, 768), row), pl.BlockSpec((Q_RANK, 2048), fixed),
                   pl.BlockSpec((KV_RANK, 2048), fixed),
                   pl.BlockSpec((1, Q_RANK), fixed), pl.BlockSpec((1, KV_RANK), fixed)],
        out_shape=[_sds((s, 768), MXU_DTYPE), _sds((Q_RANK, 2048), GRAD_DTYPE),
                   _sds((KV_RANK, 2048), GRAD_DTYPE),
                   _sds((1, Q_RANK), F32), _sds((1, KV_RANK), F32)],
        scratch_shapes=[pltpu.VMEM((tm, 2048), MXU_DTYPE), pltpu.VMEM((tm, 2048), MXU_DTYPE),
                        pltpu.VMEM((Q_RANK, 2048), F32), pltpu.VMEM((KV_RANK, 2048), F32)],
        compiler_params=_params(1),
    )(dqh, dkh, dvh, proj, gq, gkv, w_q, w_kv, c, s1, s2)


def _in_proj_bwd(dqa, dgate, dlat, dkva, w, x, dx_out, g, blocks):
    s = x.shape[0]
    tm = min(256, s)
    nm = s // tm
    na = len(blocks)

    def body(*refs):
        dqa_ref, dg8_ref, dlat_ref, dkva_ref, w_ref, x_ref, dxo_ref, g_ref = refs[:8]
        dx_ref, dxb_ref, dg_ref = refs[8 + na:11 + na]
        if na:
            ex = _Exchange(refs[8:8 + na], refs[11 + na:11 + 2 * na], refs[11 + 2 * na:], gather=False)

        @pl.when(pl.program_id(0) == 0)
        def _():
            dg_ref[...] = jnp.zeros(dg_ref.shape, F32)
            if na:
                ex.start()

        dh = (_mm_nt(dqa_ref[...], w_ref[:, QA:QA + 1024])
              + _mm_nt(dg8_ref[:, 0:1024], w_ref[:, GA:GA + 1024])
              + _mm_nt(dg8_ref[:, 1024:2048], w_ref[:, GB:GB + 1024])
              + _mm_nt(dlat_ref[...], w_ref[:, CQ:CQ + 768])
              + _mm_nt(dkva_ref[...], w_ref[:, KVA:KVA + 256]))
        xh, r = _rownorm(x_ref[...])
        dg_ref[...] += jnp.sum(dh * xh, axis=0, keepdims=True)
        dx = dxo_ref[...] + _rownorm_bwd(dh * g_ref[...], xh, r)
        dx_ref[...] = dx
        dxb_ref[...] = dx.astype(dxb_ref.dtype)

        if na:
            @pl.when(pl.program_id(0) == nm - 1)
            def _():
                ex.wait()

    row = lambda i: (i, 0)
    fixed = lambda i: (0, 0)
    outs = pl.pallas_call(
        body, name="in_proj_bwd_scatter" if na else "in_proj_bwd", grid=(nm,),
        in_specs=[pl.BlockSpec((tm, 1024), row), pl.BlockSpec((tm, 2048), row),
                  pl.BlockSpec((tm, 768), row), pl.BlockSpec((tm, 256), row),
                  pl.BlockSpec((D_MODEL, NP), fixed, pipeline_mode=pl.Buffered(1)),
                  pl.BlockSpec((tm, D_MODEL), row), pl.BlockSpec((tm, D_MODEL), row),
                  pl.BlockSpec((1, D_MODEL), fixed)] + [HBM_SPEC] * na,
        out_specs=[pl.BlockSpec((tm, D_MODEL), row), pl.BlockSpec((tm, D_MODEL), row),
                   pl.BlockSpec((1, D_MODEL), fixed)] + [HBM_SPEC] * na,
        out_shape=[_sds((s, D_MODEL), F32), _sds((s, D_MODEL), MXU_DTYPE), _sds((1, D_MODEL), F32)]
        + _exchange_shapes(blocks, False),
        scratch_shapes=_Exchange.semaphores(na) if na else [],
        compiler_params=_params(1),
    )(dqa, dgate, dlat, dkva, w, x, dx_out, g, *blocks)
    return outs[0], outs[1], outs[2], list(outs[3:])


def _adamw_update(g, w_ref, m_ref, v_ref, g_ref, d_ref, nm_ref, nv_ref):
    m2 = ADAM_B1 * m_ref[...] + (1.0 - ADAM_B1) * g
    v2 = ADAM_B2 * v_ref[...] + (1.0 - ADAM_B2) * (g * g)
    m_hat = m2 / (1.0 - ADAM_B1 ** ADAM_STEP)
    v_hat = v2 / (1.0 - ADAM_B2 ** ADAM_STEP)
    g_ref[...] = g
    d_ref[...] = -ADAM_LR * (m_hat / (jnp.sqrt(v_hat) + ADAM_EPS) + ADAM_WD * w_ref[...])
    nm_ref[...] = m2
    nv_ref[...] = v2


def _reduce_adamw_in(parts, w_t, m_t, v_t, name):
    n_layers = len(parts)
    tc = 512
    nc = D_MODEL // tc

    def body(*refs):
        p_refs = refs[:n_layers]
        w_ref, m_ref, v_ref, g_ref, d_ref, nm_ref, nv_ref = refs[n_layers:]
        layer = pl.program_id(0)
        for l in range(n_layers):
            @pl.when(layer == l)
            def _(l=l):
                g = p_refs[l][0].astype(F32)
                for k in range(1, N_DEV):
                    g = g + p_refs[l][k].astype(F32)
                _adamw_update(g[0:SHARD_COLS], w_ref, m_ref, v_ref, g_ref, d_ref, nm_ref, nv_ref)

    def part_spec(l):
        return pl.BlockSpec((N_DEV, SHARD_PAD, tc),
                            lambda layer, j: (0, 0, jnp.where(layer == l, j, 0)))

    blk = pl.BlockSpec((None, SHARD_COLS, tc), lambda layer, j: (layer, 0, j))
    return pl.pallas_call(
        body, name=name, grid=(n_layers, nc),
        in_specs=[part_spec(l) for l in range(n_layers)] + [blk, blk, blk],
        out_specs=[blk, blk, blk, blk],
        out_shape=[_sds((n_layers, SHARD_COLS, D_MODEL), F32)] * 4,
        compiler_params=_params(2),
    )(*parts, w_t, m_t, v_t)


def _reduce_adamw(parts, w, m, v, name):
    n_layers = len(parts)
    rows, part_cols = parts[0].shape[1:]
    cols = w.shape[-1]
    lanes = -(-cols // 128) * 128
    tr = rows
    for cand in (1024, 512, 256, 128, 64, 32, 16, 8):
        if rows % cand == 0 and N_DEV * cand * lanes * 4 <= 8 * 1024 * 1024:
            tr = cand
            break
    nr = rows // tr

    def body(*refs):
        p_refs = refs[:n_layers]
        w_ref, m_ref, v_ref, g_ref, d_ref, nm_ref, nv_ref = refs[n_layers:]
        layer = pl.program_id(0)
        for l in range(n_layers):
            @pl.when(layer == l)
            def _(l=l):
                g = p_refs[l][0, :, 0:cols].astype(F32)
                for k in range(1, N_DEV):
                    g = g + p_refs[l][k, :, 0:cols].astype(F32)
                _adamw_update(g, w_ref, m_ref, v_ref, g_ref, d_ref, nm_ref, nv_ref)

    def part_spec(l):
        return pl.BlockSpec((N_DEV, tr, part_cols),
                            lambda layer, i: (0, jnp.where(layer == l, i, 0), 0))

    blk = pl.BlockSpec((tr, cols), lambda layer, i: (layer * nr + i, 0))
    return pl.pallas_call(
        body, name=name, grid=(n_layers, nr),
        in_specs=[part_spec(l) for l in range(n_layers)] + [blk, blk, blk],
        out_specs=[blk, blk, blk, blk],
        out_shape=[_sds((n_layers * rows, cols), F32)] * 4,
        compiler_params=_params(2),
    )(*parts, w, m, v)


def _position():
    x, y, c = lax.axis_index("x"), lax.axis_index("y"), lax.axis_index("c")
    return x, y, c


def _index(px, py, pc):
    return 4 * px + 2 * py + pc


HBM_SPEC = pl.BlockSpec(memory_space=pltpu.HBM)
IN_BLOCKED = (D_MODEL, N_DEV * SHARD_PAD)
IN_BLOCKED_T = (N_DEV * SHARD_PAD, D_MODEL)


def _block(ref, idx):
    if tuple(ref.shape) == IN_BLOCKED:
        return ref.at[:, pl.ds(pl.multiple_of(idx * SHARD_PAD, SHARD_PAD), SHARD_PAD)]
    if tuple(ref.shape) == IN_BLOCKED_T:
        return ref.at[pl.ds(pl.multiple_of(idx * SHARD_PAD, SHARD_PAD), SHARD_PAD)]
    return ref.at[idx]


class _Gather:
    def __init__(self, srcs, dsts, sems):
        self.srcs, self.dsts = srcs, dsts
        self.send_sems, self.recv_sems, self.local_sems = sems
        x, y, c = _position()
        self.c = c
        self.me, self.sibling = (x, y, c), (x, y, 1 - c)
        self.chips = [(1 - x, y), (x, 1 - y), (1 - x, 1 - y)]

    def _copy(self, a, k, block, to, own=False):
        slot = _block(self.dsts[a], _index(*block))
        return pltpu.make_async_remote_copy(
            src_ref=self.srcs[a] if own else slot, dst_ref=slot,
            send_sem=self.send_sems.at[7 * a + k], recv_sem=self.recv_sems.at[7 * a + k],
            device_id=to, device_id_type=MESH)

    def _local(self, a):
        return pltpu.make_async_copy(self.srcs[a], _block(self.dsts[a], _index(*self.me)),
                                     self.local_sems.at[a])

    def _first(self, a):
        return [self._copy(a, 0, self.me, self.sibling, own=True)] + [
            self._copy(a, 1 + j, self.me, (*chip, self.c), own=True)
            for j, chip in enumerate(self.chips)]

    def _passed(self, a, j):
        return self._copy(a, 4 + j, (*self.chips[j], self.c), self.sibling)

    def start(self):
        for a in range(len(self.srcs)):
            self._local(a).start()
            for cp in self._first(a):
                cp.start()

    def forward(self):
        for j, chip in enumerate(self.chips):
            for a in range(len(self.srcs)):
                self._copy(a, 1 + j, (*chip, self.c), self.me).wait_recv()
                self._passed(a, j).start()

    def finish(self):
        for a in range(len(self.srcs)):
            self._copy(a, 0, self.sibling, self.me).wait_recv()
            for j, chip in enumerate(self.chips):
                self._copy(a, 4 + j, (*chip, 1 - self.c), self.me).wait_recv()
            for cp in self._first(a):
                cp.wait_send()
            for j in range(3):
                self._passed(a, j).wait_send()
            self._local(a).wait()


def _all_gather(shards, name):
    na = len(shards)

    def body(*refs):
        g = _Gather(refs[:na], refs[na:2 * na], refs[2 * na:])
        g.start()
        g.forward()
        g.finish()

    return pl.pallas_call(
        body, name=name,
        in_specs=[HBM_SPEC] * na, out_specs=[HBM_SPEC] * na,
        out_shape=_exchange_shapes(shards, True),
        scratch_shapes=_Exchange.semaphores(na),
    )(*shards)


class _Exchange:
    def __init__(self, srcs, dsts, sems, gather):
        self.srcs, self.dsts, self.gather = srcs, dsts, gather
        self.send_sems, self.recv_sems, self.local_sems = sems
        x, y, c = _position()
        self.me = _index(x, y, c)
        self.peers = [(x ^ ((k >> 2) & 1), y ^ ((k >> 1) & 1), c ^ (k & 1)) for k in range(1, N_DEV)]

    @staticmethod
    def semaphores(na):
        return [pltpu.SemaphoreType.DMA((7 * na,)), pltpu.SemaphoreType.DMA((7 * na,)),
                pltpu.SemaphoreType.DMA((na,))]

    def _src(self, a, slot):
        return self.srcs[a] if self.gather else _block(self.srcs[a], slot)

    def _local(self, a):
        return pltpu.make_async_copy(self._src(a, self.me), self.dsts[a].at[self.me],
                                     self.local_sems.at[a])

    def _send(self, a, k):
        peer = self.peers[k]
        return pltpu.make_async_remote_copy(
            src_ref=self._src(a, _index(*peer)), dst_ref=self.dsts[a].at[self.me],
            send_sem=self.send_sems.at[7 * a + k], recv_sem=self.recv_sems.at[7 * a + k],
            device_id=peer, device_id_type=MESH)

    def _arrival(self, a, k):
        landed = self.dsts[a].at[_index(*self.peers[k])]
        return pltpu.make_async_remote_copy(
            src_ref=landed, dst_ref=landed,
            send_sem=self.send_sems.at[7 * a + k], recv_sem=self.recv_sems.at[7 * a + k],
            device_id=self.peers[k], device_id_type=MESH)

    def start(self):
        for a in range(len(self.srcs)):
            self._local(a).start()
            for k in range(N_DEV - 1):
                self._send(a, k).start()

    def wait(self):
        for a in range(len(self.srcs)):
            for k in range(N_DEV - 1):
                self._arrival(a, k).wait_recv()
            for k in range(N_DEV - 1):
                self._send(a, k).wait_send()
            self._local(a).wait()


def _exchange_shapes(arrays, gather):
    def shape(a):
        if gather:
            return IN_BLOCKED if a.shape == (D_MODEL, SHARD_PAD) else (N_DEV,) + a.shape
        return (N_DEV, SHARD_PAD, D_MODEL) if a.shape == IN_BLOCKED_T else a.shape
    return [_sds(shape(a), a.dtype) for a in arrays]


def _exchange_call(arrays, gather, name):
    na = len(arrays)

    def body(*refs):
        ex = _Exchange(refs[:na], refs[na:2 * na], refs[2 * na:], gather)
        ex.start()
        ex.wait()

    return pl.pallas_call(
        body, name=name,
        in_specs=[HBM_SPEC] * na, out_specs=[HBM_SPEC] * na,
        out_shape=_exchange_shapes(arrays, gather),
        scratch_shapes=_Exchange.semaphores(na),
    )(*arrays)


def _layer_fwd(x, small, w_in, g_rest, rest_shards, tabs, next_shards):
    attn_g, sinks, gq, gkv = small
    proj, h, gathered_rest = _in_proj(x, attn_g, w_in, rest_shards)
    w_q, w_kv, w_o = _rest_from_gathered(*(gathered_rest if rest_shards else g_rest))
    swa = _swa_fwd(proj, sinks)
    qh, kh, vh, vth = _mla_prep(proj, gq, gkv, w_q, w_kv, tabs)
    mla, lse, gathered = _mla_fwd(qh, kh, vth, next_shards)
    x_next, y = _out_proj(x, proj, swa, mla, w_o)
    return x_next, (x, proj, h, swa, qh, kh, vh, mla, lse, y), (w_in, w_q, w_kv, w_o), gathered


def _layer_bwd(dx, dxb, saved, small, weights, tabs, pending, send_own):
    attn_g, sinks, gq, gkv = small
    w_in, w_q, w_kv, w_o = weights
    x, proj, h, swa, qh, kh, vh, mla, lse, y = saved
    d_o = _matmul_tn(y, dxb, "grad_w_out")
    o_block = d_o.reshape(N_DEV, 256, D_MODEL)
    do_a, do_b, dgate, dlt = _out_proj_bwd(dxb, proj, swa, mla, w_o)
    dqa, dkva, dsk, got_o = _swa_bwd(proj, sinks, do_a, swa, [o_block] if send_own else [])
    dqh, dkh, dvh, received = _mla_bwd(qh, kh, vh, do_b, dlt, lse, pending)
    dlat, d_q, d_kv, d_gq, d_gkv = _mla_prep_bwd(dqh, dkh, dvh, proj, gq, gkv, w_q, w_kv, tabs)
    qkv_blocks = _qkv_grad_blocks(d_q, d_kv)
    d_inp, got_qkv = _grad_w_in(h, dqa, dgate, dlat, dkva, qkv_blocks if send_own else [])
    in_block = _in_grad_blocks(d_inp)
    dx, dxb, d_attn, got_in = _in_proj_bwd(dqa, dgate, dlat, dkva, w_in, x, dx, attn_g,
                                           [in_block] if send_own else [])
    small_grads = (d_attn, dsk[0:1, 0:SWA_HEADS], d_gq, d_gkv)
    blocks = [in_block] + qkv_blocks + [o_block]
    return dx, dxb, small_grads, blocks, received, got_in + got_qkv + got_o


def _pack_small_grads(small_grads, d_final, loss):
    d_attn, d_sink, d_gq, d_gkv = zip(*small_grads)
    return jnp.concatenate([
        jnp.concatenate(d_attn, axis=0).reshape(64, 128),
        jnp.concatenate(d_gq, axis=0).reshape(12, 128),
        jnp.concatenate(d_gkv, axis=0).reshape(8, 128),
        d_final.reshape(16, 128),
        jnp.pad(jnp.concatenate(d_sink, axis=1), ((0, 0), (0, 64))),
        loss[0:1],
        jnp.zeros((PACK_ROWS - ROW_LOSS - 1, 128), F32)], axis=0)


def _pack_small(attn, qa, kva, final, sinks):
    return jnp.concatenate([
        attn.reshape(64, 128), qa.reshape(12, 128), kva.reshape(8, 128), final.reshape(16, 128),
        jnp.pad(sinks.reshape(1, 64), ((0, 0), (0, 64))),
        jnp.zeros((PACK_ROWS - ROW_SINK - 1, 128), F32)], axis=0)


def _unpack_small(p):
    return (p[ROW_ATTN:ROW_QA].reshape(DEPTH, D_MODEL), p[ROW_SINK, 0:64].reshape(DEPTH, SWA_HEADS),
            p[ROW_QA:ROW_KVA].reshape(DEPTH, Q_RANK), p[ROW_KVA:ROW_FINAL].reshape(DEPTH, KV_RANK),
            p[ROW_FINAL:ROW_SINK].reshape(D_MODEL))


def kernel(x, attn_norm_g, w_in, swa_sinks, q_a_norm_g, kv_a_norm_g, w_q_b, w_kv_b, w_out, final_norm_g, loss_target, m_attn_norm_g, m_w_in, m_swa_sinks, m_q_a_norm_g, m_kv_a_norm_g, m_w_q_b, m_w_kv_b, m_w_out, m_final_norm_g, v_attn_norm_g, v_w_in, v_swa_sinks, v_q_a_norm_g, v_kv_a_norm_g, v_w_q_b, v_w_kv_b, v_w_out, v_final_norm_g):
    xs, tgt = x[0], loss_target[0]
    tabs = _rope_tables(xs.shape[0])
    shards = [w.astype(MXU_DTYPE) for w in (w_in, w_q_b, w_kv_b, w_out)]
    shards[0] = jnp.pad(shards[0], ((0, 0), (0, 0), (0, SHARD_PAD - SHARD_COLS)))
    layer_shards = lambda l: [w[l] for w in shards]
    smalls = [(attn_norm_g[l:l + 1], swa_sinks[l], q_a_norm_g[l:l + 1], kv_a_norm_g[l:l + 1])
              for l in range(DEPTH)]

    gathered = list(_all_gather(layer_shards(0)[:1], "gather_weights")) + [None] * 3
    weights, saved = [None] * DEPTH, []
    for l in range(DEPTH):
        next_shards = layer_shards(l + 1) if l + 1 < DEPTH else []
        rest_shards = layer_shards(0)[1:] if l == 0 else []
        xs, acts, weights[l], gathered = _layer_fwd(
            xs, smalls[l], _w_in_from_gathered(gathered[0]), gathered[1:], rest_shards, tabs,
            next_shards)
        saved.append(acts)
    dx, dxb, d_final, loss = _final_loss(xs, tgt, final_norm_g.reshape(1, D_MODEL))

    received, small_grads, pending = [None] * DEPTH, [None] * DEPTH, []
    for l in reversed(range(DEPTH)):
        dx, dxb, small_grads[l], blocks, arrived, arrived_own = _layer_bwd(
            dx, dxb, saved[l], smalls[l], weights[l], tabs, pending, send_own=(l == 0))
        if pending:
            received[l + 1] = arrived
        pending = blocks
    received[0] = arrived_own
    small = _exchange_call([_pack_small_grads(small_grads, d_final, loss)], True, "gather_small")[0]

    big = []
    for a, (w, m, v, name) in enumerate(zip((w_in, w_q_b, w_kv_b, w_out),
                                            (m_w_in, m_w_q_b, m_w_kv_b, m_w_out),
                                            (v_w_in, v_w_q_b, v_w_kv_b, v_w_out),
                                            ("adamw_w_in", "adamw_w_q_b", "adamw_w_kv_b",
                                             "adamw_w_out"))):
        parts = [received[l][a] for l in range(DEPTH)]
        if a == 0:
            swap = lambda t: t.transpose(0, 2, 1)
            outs = _reduce_adamw_in(parts, swap(w), swap(m), swap(v), name)
            big.append([swap(t) for t in outs])
        else:
            cols = w.shape[-1]
            flat = lambda t: t.reshape(-1, cols)
            outs = _reduce_adamw(parts, flat(w), flat(m), flat(v), name)
            big.append([t.reshape(w.shape) for t in outs])

    sm = _reduce_adamw(
        [small],
        _pack_small(attn_norm_g, q_a_norm_g, kv_a_norm_g, final_norm_g, swa_sinks),
        _pack_small(m_attn_norm_g, m_q_a_norm_g, m_kv_a_norm_g, m_final_norm_g, m_swa_sinks),
        _pack_small(v_attn_norm_g, v_q_a_norm_g, v_kv_a_norm_g, v_final_norm_g, v_swa_sinks),
        "adamw_small")
    loss = sm[0][ROW_LOSS, 0]
    kinds = []
    for t in range(4):
        attn, sinks, qa, kva, final = _unpack_small(sm[t])
        b_in, b_q, b_kv, b_o = (big[i][t] for i in range(4))
        kinds.append((attn, b_in, sinks, qa, kva, b_q, b_kv, b_o, final))
    return (loss, dx[None], *kinds[0], *kinds[1], *kinds[2], *kinds[3])
```

```python
import jax
import jax.numpy as jnp
import numpy as np
from jax import lax
from jax.experimental import pallas as pl
from jax.experimental.pallas import tpu as pltpu

F32 = jnp.float32
BF16 = jnp.bfloat16
MXU_DTYPE = BF16
GRAD_DTYPE = BF16
PROJ_DTYPE = BF16
ATTN_DTYPE = BF16

D_MODEL = 2048
DEPTH = 4
EPS = 1e-6
NEG = -1e30
BLOCK = 128
SWA_HEADS = 16
MLA_HEADS = 8
Q_RANK = 384
KV_RANK = 256
MLA_SCALE = 192 ** -0.5
MLA_C2 = MLA_SCALE * 1.4426950408889634
ROPE_THETA = 10000.0
IN_WIDTH = 4032

ADAM_LR = 0.001
ADAM_B1 = 0.9
ADAM_B2 = 0.999
ADAM_EPS = 1e-08
ADAM_WD = 0.01
ADAM_STEP = 10

N_DEV = 8
MESH = pl.DeviceIdType.MESH

NP = 4096
QA, GA, GB, CQ, CKV, KR, KVA = 0, 1024, 2048, 3072, 3456, 3712, 3840

ROW_ATTN, ROW_QA, ROW_KVA, ROW_FINAL, ROW_SINK, ROW_LOSS, PACK_ROWS = 0, 64, 76, 84, 100, 101, 104

VMEM_LIMIT = 56 * 1024 * 1024
MLA_TILE = 512
GATHER_FORWARD_HEAD = 7


def _sds(shape, dtype):
    return jax.ShapeDtypeStruct(shape, dtype)


def _params(n_axes):
    return pltpu.CompilerParams(dimension_semantics=("arbitrary",) * n_axes,
                                vmem_limit_bytes=VMEM_LIMIT)


def _mm(a, b):
    return jnp.dot(a.astype(MXU_DTYPE), b.astype(MXU_DTYPE), preferred_element_type=F32)


def _mm_nt(a, b):
    return lax.dot_general(a.astype(MXU_DTYPE), b.astype(MXU_DTYPE),
                           (((1,), (1,)), ((), ())), preferred_element_type=F32)


def _mm_tn(a, b):
    return lax.dot_general(a.astype(MXU_DTYPE), b.astype(MXU_DTYPE),
                           (((0,), (0,)), ((), ())), preferred_element_type=F32)


def _rownorm(x):
    r = lax.rsqrt(jnp.mean(x * x, axis=-1, keepdims=True) + EPS)
    return x * r, r


def _rownorm_bwd(dxh, xh, r):
    return r * (dxh - xh * jnp.mean(dxh * xh, axis=-1, keepdims=True))


def _rope(t, c, s1, s2):
    return t * c + pltpu.roll(t, 32, 1) * s1 + pltpu.roll(t, 96, 1) * s2


SHARD_COLS = IN_WIDTH // N_DEV
SHARD_PAD = 512


def _orig_col_of_padded():
    o = np.full((NP,), -1, np.int64)
    for start, width, orig in ((QA, 1024, 0), (KVA, 256, 1024), (GA, 1024, 1280), (CQ, 384, 2304),
                               (CKV, 256, 2688), (KR, 64, 2944), (GB, 1024, 3008)):
        o[start:start + width] = np.arange(orig, orig + width)
    return o


def _device_major_src():
    o = _orig_col_of_padded()
    return np.where(o >= 0, o + (SHARD_PAD - SHARD_COLS) * (o // SHARD_COLS), -1)


def _kernel_layout_src():
    o = _orig_col_of_padded()
    where = np.full((IN_WIDTH,), -1, np.int64)
    where[o[o >= 0]] = np.nonzero(o >= 0)[0]
    e = np.arange(N_DEV * SHARD_PAD)
    k, c = e // SHARD_PAD, e % SHARD_PAD
    return np.where(c < SHARD_COLS, where[np.minimum(SHARD_COLS * k + c, IN_WIDTH - 1)], -1)


def _permute_columns(x, src_of, name, transposed=False):
    rows, n_in = x.shape
    n_out = len(src_of)
    plan, mats = [], []
    for t in range(n_out // 128):
        srcs = src_of[128 * t:128 * (t + 1)]
        entry = []
        for u in sorted(set(int(s) // 128 for s in srcs if s >= 0)):
            m = np.zeros((128, 128), np.float32)
            for c, s in enumerate(srcs):
                if s >= 0 and s // 128 == u:
                    m[s % 128, c] = 1.0
            entry.append((u, len(mats)))
            mats.append(m)
        plan.append(entry)
    tr = min(512, rows)

    def body(x_ref, p_ref, o_ref):
        for t, entry in enumerate(plan):
            acc = jnp.zeros((tr, 128), F32)
            for u, idx in entry:
                acc = acc + jnp.dot(x_ref[:, 128 * u:128 * (u + 1)], p_ref[idx],
                                    preferred_element_type=F32)
            if transposed:
                o_ref[128 * t:128 * (t + 1), :] = acc.T.astype(o_ref.dtype)
            else:
                o_ref[:, 128 * t:128 * (t + 1)] = acc.astype(o_ref.dtype)

    table = jnp.asarray(np.stack(mats), x.dtype)
    return pl.pallas_call(
        body, name=name, grid=(rows // tr,),
        in_specs=[pl.BlockSpec((tr, n_in), lambda i: (i, 0)),
                  pl.BlockSpec(table.shape, lambda i: (0, 0, 0))],
        out_specs=(pl.BlockSpec((n_out, tr), lambda i: (0, i)) if transposed
                   else pl.BlockSpec((tr, n_out), lambda i: (i, 0))),
        out_shape=_sds((n_out, rows) if transposed else (rows, n_out), x.dtype),
        compiler_params=_params(1),
    )(x, table)


def _w_in_from_gathered(g_in):
    return _permute_columns(g_in, _device_major_src(), "w_in_layout")


def _rest_from_gathered(g_qb, g_kvb, g_out):
    qb = g_qb.transpose(1, 0, 2)
    rope = jnp.pad(qb[..., 128:], ((0, 0), (0, 0), (0, 64)))
    w_q = jnp.concatenate([qb[..., :128].reshape(Q_RANK, 1024),
                           rope.reshape(Q_RANK, 1024)], axis=-1)
    kvb = g_kvb.transpose(1, 0, 2)
    w_kv = jnp.concatenate([kvb[..., :128].reshape(KV_RANK, 1024),
                            kvb[..., 128:].reshape(KV_RANK, 1024)], axis=-1)
    w_o = g_out.reshape(D_MODEL, D_MODEL)
    return w_q, w_kv, w_o


def _in_grad_blocks(d_inp):
    return _permute_columns(d_inp, _kernel_layout_src(), "grad_w_in_layout", transposed=True)


def _qkv_grad_blocks(d_q, d_kv):
    qn = d_q[:, :1024].reshape(Q_RANK, 8, 128)
    qr = d_q[:, 1024:].reshape(Q_RANK, 8, 128)[..., :64]
    b_q = jnp.concatenate([qn, qr], axis=-1).transpose(1, 0, 2)
    kn = d_kv[:, :1024].reshape(KV_RANK, 8, 128)
    vv = d_kv[:, 1024:].reshape(KV_RANK, 8, 128)
    b_kv = jnp.concatenate([kn, vv], axis=-1).transpose(1, 0, 2)
    return [b_q, b_kv]


def _rope_tables(s):
    pos = jnp.arange(s, dtype=F32)
    inv_freq = ROPE_THETA ** (-jnp.arange(0, 64, 2, dtype=F32) / 64)
    ang = pos[:, None] * inv_freq[None, :]
    cos, sin = jnp.cos(ang), jnp.sin(ang)
    z32 = jnp.zeros((s, 32), F32)
    z64 = jnp.zeros((s, 64), F32)
    c = jnp.concatenate([cos, cos, z64], axis=1)
    s1 = jnp.concatenate([z32, sin, z64], axis=1)
    s2 = jnp.concatenate([-sin, z32, z64], axis=1)
    return c, s1, s2


def _in_proj(x, g, w, shards):
    s = x.shape[0]
    tm, tn = min(512, s), 1024
    nm = s // tm
    na = len(shards)

    def body(*refs):
        x_ref, g_ref, w_ref = refs[:3]
        o_ref, h_ref = refs[3 + na:5 + na]
        i = pl.program_id(0)
        if na:
            ex = _Gather(refs[3:3 + na], refs[5 + na:5 + 2 * na], refs[5 + 2 * na:])

            @pl.when(i == 0)
            def _():
                ex.start()

            @pl.when(i == (3 * nm) // 4)
            def _():
                ex.forward()

        xh, _ = _rownorm(x_ref[...])
        h_ref[...] = (xh * g_ref[...]).astype(h_ref.dtype)
        for j in range(NP // tn):
            cols = slice(j * tn, (j + 1) * tn)
            o_ref[:, cols] = jnp.dot(h_ref[...], w_ref[:, cols],
                                     preferred_element_type=F32).astype(o_ref.dtype)

        if na:
            @pl.when(i == nm - 1)
            def _():
                ex.finish()

    row = lambda i: (i, 0)
    fixed = lambda i: (0, 0)
    outs = pl.pallas_call(
        body, name="in_proj_gather" if na else "in_proj", grid=(nm,),
        in_specs=[pl.BlockSpec((tm, D_MODEL), row), pl.BlockSpec((1, D_MODEL), fixed),
                  pl.BlockSpec((D_MODEL, NP), fixed, pipeline_mode=pl.Buffered(1))]
        + [HBM_SPEC] * na,
        out_specs=[pl.BlockSpec((tm, NP), row), pl.BlockSpec((tm, D_MODEL), row)]
        + [HBM_SPEC] * na,
        out_shape=[_sds((s, NP), PROJ_DTYPE), _sds((s, D_MODEL), MXU_DTYPE)]
        + _exchange_shapes(shards, True),
        scratch_shapes=_Exchange.semaphores(na) if na else [],
        compiler_params=_params(1),
    )(x, g, w, *shards)
    return outs[0], outs[1], list(outs[2:])


def _swa_slopes():
    return [2.0 ** (-8.0 * (h + 1) / SWA_HEADS) for h in range(SWA_HEADS)]


SWA_STACK = SWA_HEADS * BLOCK


def _swa_head(b):
    j, a = b // 8, b % 8
    return 2 * (4 * j + a % 4) + a // 4


def _swa_kv(kv_p, kv_c):
    return (jnp.concatenate([kv_p[:, :128], kv_c[:, :128]], axis=0),
            jnp.concatenate([kv_p[:, 128:], kv_c[:, 128:]], axis=0))


def _swa_by_block(vals):
    b = lax.broadcasted_iota(jnp.int32, (1, SWA_STACK), 1) >> 7
    row = jnp.full((1, SWA_STACK), vals[-1], F32)
    for t in range(len(vals) - 2, -1, -1):
        row = jnp.where(b == t, vals[t], row)
    return row


def _swa_bias(n):
    slopes = _swa_slopes()
    ki = lax.broadcasted_iota(jnp.int32, (2 * BLOCK, SWA_STACK), 0)
    r = lax.broadcasted_iota(jnp.int32, (2 * BLOCK, SWA_STACK), 1)
    delta = BLOCK + (r & (BLOCK - 1)) - ki
    valid = (delta >= 0) & (delta < BLOCK) & ((n - 1) * BLOCK + ki >= 0)
    slope = _swa_by_block([slopes[_swa_head(b)] for b in range(SWA_HEADS)])
    return jnp.where(valid, -slope * delta.astype(F32), NEG)


def _swa_fill_bias(n, bias_ref):
    @pl.when(n <= 1)
    def _():
        bias_ref[...] = _swa_bias(n)


def _swa_sink_row(sink_ref):
    return _swa_by_block([sink_ref[_swa_head(b)] for b in range(SWA_HEADS)])


def _swa_pairs(ref, j):
    return jnp.concatenate([ref[:, 128 * (4 * j + a):128 * (4 * j + a + 1)] for a in range(4)],
                           axis=0).astype(F32)


def _swa_stack(ref):
    left = lax.broadcasted_iota(jnp.int32, (8 * BLOCK, 128), 1) < 64
    p0, p1 = _swa_pairs(ref, 0), _swa_pairs(ref, 1)
    s0 = jnp.concatenate([p0, pltpu.roll(p0, 64, 1)], axis=0)
    s1 = jnp.concatenate([pltpu.roll(p1, 64, 1), p1], axis=0)
    return jnp.concatenate([jnp.where(left, s0, 0.0), jnp.where(left, 0.0, s1)], axis=0)


def _swa_unstack(t):
    left = lax.broadcasted_iota(jnp.int32, (4 * BLOCK, 128), 1) < 64
    n = 4 * BLOCK
    g0 = jnp.where(left, t[0:n], pltpu.roll(t[n:2 * n], 64, 1))
    g1 = jnp.where(left, pltpu.roll(t[2 * n:3 * n], 64, 1), t[3 * n:4 * n])
    return g0, g1


def _swa_softmax(qs, kk, bias, sink):
    sc = _mm_nt(kk, qs) + bias
    m = jnp.maximum(jnp.max(sc, axis=0, keepdims=True), sink)
    ex = jnp.exp(sc - m)
    es = jnp.exp(sink - m)
    return ex, es, 1.0 / (jnp.sum(ex, axis=0, keepdims=True) + es)


def _swa_fwd(proj, sinks):
    s = proj.shape[0]
    nb = s // BLOCK

    def body(sink_ref, q_ref, kp_ref, kc_ref, o_ref, bias_ref):
        n = pl.program_id(0)
        _swa_fill_bias(n, bias_ref)
        kk, vv = _swa_kv(kp_ref[...].astype(F32), kc_ref[...].astype(F32))
        qs = _swa_stack(q_ref) * 0.125
        ex, _, inv = _swa_softmax(qs, kk, bias_ref[...], _swa_sink_row(sink_ref))
        o_t = _mm(vv.T, ex) * inv
        n4 = 4 * BLOCK
        for j in range(2):
            rows = slice(64 * j, 64 * (j + 1))
            out = jnp.concatenate([o_t[rows, 2 * j * n4:(2 * j + 1) * n4],
                                   o_t[rows, (2 * j + 1) * n4:(2 * j + 2) * n4]], axis=0).T
            for a in range(4):
                o_ref[:, 128 * (4 * j + a):128 * (4 * j + a + 1)] = out[128 * a:128 * (a + 1)].astype(
                    o_ref.dtype)

    return pl.pallas_call(
        body, name="swa_fwd", grid=(nb,),
        in_specs=[pl.BlockSpec(memory_space=pltpu.SMEM),
                  pl.BlockSpec((BLOCK, 1024), lambda n: (n, 0)),
                  pl.BlockSpec((BLOCK, 256), lambda n: (jnp.maximum(n - 1, 0), KVA // 256)),
                  pl.BlockSpec((BLOCK, 256), lambda n: (n, KVA // 256))],
        out_specs=pl.BlockSpec((BLOCK, 1024), lambda n: (n, 0)),
        out_shape=_sds((s, 1024), ATTN_DTYPE),
        scratch_shapes=[pltpu.VMEM((2 * BLOCK, SWA_STACK), F32)],
        compiler_params=_params(1),
    )(sinks, proj, proj, proj)


def _mla_prep(proj, gq, gkv, w_q, w_kv, tabs):
    s = proj.shape[0]
    tm = min(512, s)
    c, s1, s2 = tabs

    def body(p_ref, gq_ref, gkv_ref, wq_ref, wkv_ref, c_ref, s1_ref, s2_ref,
             q_ref, k_ref, v_ref, vt_ref):
        cqh, _ = _rownorm(p_ref[:, 0:384].astype(F32))
        ckvh, _ = _rownorm(p_ref[:, 384:640].astype(F32))
        q = _mm(cqh * gq_ref[...], wq_ref[...])
        kv = _mm(ckvh * gkv_ref[...], wkv_ref[...])
        cc, ss1, ss2 = c_ref[...], s1_ref[...], s2_ref[...]
        krr = _rope(p_ref[:, 640:768].astype(F32), cc, ss1, ss2).astype(k_ref.dtype)
        for h in range(MLA_HEADS):
            q_ref[h, :, 0:128] = q[:, 128 * h:128 * (h + 1)].astype(q_ref.dtype)
            q_ref[h, :, 128:256] = _rope(q[:, 1024 + 128 * h:1024 + 128 * (h + 1)],
                                         cc, ss1, ss2).astype(q_ref.dtype)
            k_ref[h, :, 0:128] = kv[:, 128 * h:128 * (h + 1)].astype(k_ref.dtype)
            k_ref[h, :, 128:256] = krr
            vv = kv[:, 1024 + 128 * h:1024 + 128 * (h + 1)]
            v_ref[h] = vv.astype(v_ref.dtype)
            vt_ref[h, 0:128, :] = vv.T.astype(vt_ref.dtype)
            vt_ref[h, 128:256, :] = jnp.ones((128, tm), vt_ref.dtype)

    row = lambda i: (i, 0)
    fixed = lambda i: (0, 0)
    return pl.pallas_call(
        body, name="mla_prep", grid=(s // tm,),
        in_specs=[pl.BlockSpec((tm, 768), lambda i: (i, CQ // 768)),
                  pl.BlockSpec((1, Q_RANK), fixed), pl.BlockSpec((1, KV_RANK), fixed),
                  pl.BlockSpec((Q_RANK, 2048), fixed), pl.BlockSpec((KV_RANK, 2048), fixed),
                  pl.BlockSpec((tm, 128), row), pl.BlockSpec((tm, 128), row),
                  pl.BlockSpec((tm, 128), row)],
        out_specs=[pl.BlockSpec((MLA_HEADS, tm, 256), lambda i: (0, i, 0)),
                   pl.BlockSpec((MLA_HEADS, tm, 256), lambda i: (0, i, 0)),
                   pl.BlockSpec((MLA_HEADS, tm, 128), lambda i: (0, i, 0)),
                   pl.BlockSpec((MLA_HEADS, 256, tm), lambda i: (0, 0, i))],
        out_shape=[_sds((MLA_HEADS, s, 256), MXU_DTYPE), _sds((MLA_HEADS, s, 256), MXU_DTYPE),
                   _sds((MLA_HEADS, s, 128), MXU_DTYPE), _sds((MLA_HEADS, 256, s), MXU_DTYPE)],
        compiler_params=_params(1),
    )(proj, gq, gkv, w_q, w_kv, c, s1, s2)


def _scores_t(k, q, t, diagonal):
    sc = _mm_nt(k, q)
    if diagonal:
        key = lax.broadcasted_iota(jnp.int32, sc.shape, 0)
        query = lax.broadcasted_iota(jnp.int32, sc.shape, 1)
        sc = jnp.where(key <= query, sc, NEG)
    return sc


def _mla_fwd(qh, kh, vth, shards):
    s = qh.shape[1]
    t = min(MLA_TILE, s)
    nq = s // t
    na = len(shards)

    def body(*refs):
        q_ref, k_ref, vt_ref = refs[:3]
        o_ref, lse_ref = refs[3 + na:5 + na]
        m_ref, acc_ref = refs[5 + 2 * na:7 + 2 * na]
        h, i = pl.program_id(0), pl.program_id(1)
        if na:
            ex = _Gather(refs[3:3 + na], refs[5 + na:5 + 2 * na], refs[7 + 2 * na:])

            @pl.when((h == 0) & (i == 0))
            def _():
                ex.start()

            @pl.when((h == GATHER_FORWARD_HEAD) & (i == 0))
            def _():
                ex.forward()

        m_ref[...] = jnp.full(m_ref.shape, NEG, F32)
        acc_ref[...] = jnp.zeros(acc_ref.shape, F32)

        def step(start, width):
            keys = pl.ds(pl.multiple_of(start, t), width)
            sc = _mm_nt(k_ref[keys, :], q_ref[...])
            m_prev = m_ref[...]
            m_new = jnp.maximum(m_prev, jnp.max(sc, axis=0, keepdims=True))
            alpha = jnp.exp2((m_prev - m_new) * MLA_C2)
            p = jnp.exp2((sc - m_new[0:1, :]) * MLA_C2)
            acc_ref[...] = alpha[0:1, :] * acc_ref[...] + _mm(vt_ref[:, keys], p)
            m_ref[...] = m_new

        def diagonal_step():
            half = t // 2
            k_lo = pl.ds(pl.multiple_of(i * t, half), half)
            k_hi = pl.ds(pl.multiple_of(i * t + half, half), half)
            sc_lo = _scores_t(k_ref[k_lo, :], q_ref[...], t, True)
            sc_hi = _scores_t(k_ref[k_hi, :], q_ref[half:t, :], t, True)
            m_prev = m_ref[...]
            m_lo = jnp.maximum(m_prev, jnp.max(sc_lo, axis=0, keepdims=True))
            m_new = jnp.concatenate(
                [m_lo[:, 0:half],
                 jnp.maximum(m_lo[:, half:t], jnp.max(sc_hi, axis=0, keepdims=True))], axis=1)
            alpha = jnp.exp2((m_prev - m_new) * MLA_C2)
            p_lo = jnp.exp2((sc_lo - m_new[0:1, :]) * MLA_C2)
            p_hi = jnp.exp2((sc_hi - m_new[0:1, half:t]) * MLA_C2)
            acc_ref[...] = alpha[0:1, :] * acc_ref[...] + _mm(vt_ref[:, k_lo], p_lo)
            acc_ref[:, half:t] += _mm(vt_ref[:, k_hi], p_hi)
            m_ref[...] = m_new

        def below_diagonal(jj, carry):
            step(jj * (4 * t), 4 * t)
            return carry

        lax.fori_loop(0, i // 4, below_diagonal, 0)

        @pl.when(i % 4 >= 2)
        def _():
            step((i // 4) * (4 * t), 2 * t)

        @pl.when(i % 2 == 1)
        def _():
            step((i - 1) * t, t)

        diagonal_step()
        l = acc_ref[128:136, :]
        o_ref[...] = (acc_ref[0:128, :] / l[0:1, :]).T.astype(o_ref.dtype)
        lse_ref[...] = m_ref[...] * MLA_C2 + jnp.log2(l)

        if na:
            @pl.when((h == MLA_HEADS - 1) & (i == nq - 1))
            def _():
                ex.finish()

    head = lambda h, i: (h, 0, 0)
    outs = pl.pallas_call(
        body, name="mla_fwd_gather" if na else "mla_fwd", grid=(MLA_HEADS, nq),
        in_specs=[pl.BlockSpec((None, t, 256), lambda h, i: (h, i, 0)),
                  pl.BlockSpec((None, s, 256), head),
                  pl.BlockSpec((None, 256, s), head)] + [HBM_SPEC] * na,
        out_specs=[pl.BlockSpec((t, 128), lambda h, i: (i, h)),
                   pl.BlockSpec((None, 8, t), lambda h, i: (h, 0, i))] + [HBM_SPEC] * na,
        out_shape=[_sds((s, 1024), ATTN_DTYPE), _sds((MLA_HEADS, 8, s), F32)]
        + _exchange_shapes(shards, True),
        scratch_shapes=[pltpu.VMEM((8, t), F32), pltpu.VMEM((256, t), F32)]
        + (_Exchange.semaphores(na) if na else []),
        compiler_params=_params(2),
    )(qh, kh, vth, *shards)
    return outs[0], outs[1], list(outs[2:])


def _silu_parts(g):
    sg = jax.nn.sigmoid(g)
    return g * sg, sg * (1.0 + g * (1.0 - sg))


def _out_proj(x, proj, swa, mla, w_out):
    s = x.shape[0]
    tm = min(512, s)

    def body(x_ref, ga_ref, gb_ref, a_ref, b_ref, w_ref, xo_ref, y_ref):
        sa, _ = _silu_parts(ga_ref[...].astype(F32))
        sb, _ = _silu_parts(gb_ref[...].astype(F32))
        y_ref[:, 0:1024] = (a_ref[...].astype(F32) * sa).astype(y_ref.dtype)
        y_ref[:, 1024:2048] = (b_ref[...].astype(F32) * sb).astype(y_ref.dtype)
        xo_ref[...] = x_ref[...] + jnp.dot(y_ref[...], w_ref[...], preferred_element_type=F32)

    row = lambda i: (i, 0)
    return pl.pallas_call(
        body, name="out_proj", grid=(s // tm,),
        in_specs=[pl.BlockSpec((tm, D_MODEL), row),
                  pl.BlockSpec((tm, 1024), lambda i: (i, GA // 1024)),
                  pl.BlockSpec((tm, 1024), lambda i: (i, GB // 1024)),
                  pl.BlockSpec((tm, 1024), row), pl.BlockSpec((tm, 1024), row),
                  pl.BlockSpec((D_MODEL, D_MODEL), lambda i: (0, 0), pipeline_mode=pl.Buffered(1))],
        out_specs=[pl.BlockSpec((tm, D_MODEL), row), pl.BlockSpec((tm, D_MODEL), row)],
        out_shape=[_sds((s, D_MODEL), F32), _sds((s, D_MODEL), MXU_DTYPE)],
        compiler_params=_params(1),
    )(x, proj, proj, swa, mla, w_out)


def _final_loss(x, tgt, g):
    s = x.shape[0]
    tm = min(512, s)

    def body(x_ref, t_ref, g_ref, dx_ref, dxb_ref, dg_ref, loss_ref):
        @pl.when(pl.program_id(0) == 0)
        def _():
            dg_ref[...] = jnp.zeros(dg_ref.shape, F32)
            loss_ref[...] = jnp.zeros(loss_ref.shape, F32)
        xh, r = _rownorm(x_ref[...])
        gg = g_ref[...]
        err = xh * gg - t_ref[...]
        per_row = jnp.mean(err * err, axis=-1, keepdims=True)
        loss_ref[...] += 0.5 * jnp.sum(per_row, axis=0, keepdims=True)
        dy = err * (1.0 / D_MODEL)
        dg_ref[...] += jnp.sum(dy * xh, axis=0, keepdims=True)
        dx = _rownorm_bwd(dy * gg, xh, r)
        dx_ref[...] = dx
        dxb_ref[...] = dx.astype(dxb_ref.dtype)

    row = lambda i: (i, 0)
    fixed = lambda i: (0, 0)
    return pl.pallas_call(
        body, name="final_loss", grid=(s // tm,),
        in_specs=[pl.BlockSpec((tm, D_MODEL), row), pl.BlockSpec((tm, D_MODEL), row),
                  pl.BlockSpec((1, D_MODEL), fixed)],
        out_specs=[pl.BlockSpec((tm, D_MODEL), row), pl.BlockSpec((tm, D_MODEL), row),
                   pl.BlockSpec((1, D_MODEL), fixed), pl.BlockSpec((8, 128), fixed)],
        out_shape=[_sds((s, D_MODEL), F32), _sds((s, D_MODEL), MXU_DTYPE), _sds((1, D_MODEL), F32),
                   _sds((8, 128), F32)],
        compiler_params=_params(1),
    )(x, tgt, g)


def _out_proj_bwd(dx, proj, swa, mla, w_out):
    s = dx.shape[0]
    tm = min(512, s)

    def body(dx_ref, ga_ref, gb_ref, a_ref, b_ref, w_ref, doa_ref, dob_ref, dg_ref, dlt_ref):
        dx = dx_ref[...].astype(MXU_DTYPE)
        dya = _mm_nt(dx, w_ref[0:1024, :])
        sa, dsa = _silu_parts(ga_ref[...].astype(F32))
        doa_ref[...] = dya * sa
        dg_ref[:, 0:1024] = (dya * a_ref[...].astype(F32) * dsa).astype(dg_ref.dtype)
        dyb = _mm_nt(dx, w_ref[1024:2048, :])
        sb, dsb = _silu_parts(gb_ref[...].astype(F32))
        b = b_ref[...].astype(F32)
        dob = dyb * sb
        dob_ref[...] = dob.astype(dob_ref.dtype)
        dg_ref[:, 1024:2048] = (dyb * b * dsb).astype(dg_ref.dtype)
        prod = dob * b
        for h in range(MLA_HEADS):
            dlt = jnp.sum(prod[:, 128 * h:128 * (h + 1)], axis=1, keepdims=True)
            dlt_ref[h] = jnp.broadcast_to(dlt, (tm, 128)).T[0:8, :]

    row = lambda i: (i, 0)
    return pl.pallas_call(
        body, name="out_proj_bwd", grid=(s // tm,),
        in_specs=[pl.BlockSpec((tm, D_MODEL), row),
                  pl.BlockSpec((tm, 1024), lambda i: (i, GA // 1024)),
                  pl.BlockSpec((tm, 1024), lambda i: (i, GB // 1024)),
                  pl.BlockSpec((tm, 1024), row), pl.BlockSpec((tm, 1024), row),
                  pl.BlockSpec((D_MODEL, D_MODEL), lambda i: (0, 0), pipeline_mode=pl.Buffered(1))],
        out_specs=[pl.BlockSpec((tm, 1024), row), pl.BlockSpec((tm, 1024), row),
                   pl.BlockSpec((tm, D_MODEL), row),
                   pl.BlockSpec((MLA_HEADS, 8, tm), lambda i: (0, 0, i))],
        out_shape=[_sds((s, 1024), F32), _sds((s, 1024), MXU_DTYPE), _sds((s, D_MODEL), MXU_DTYPE),
                   _sds((MLA_HEADS, 8, s), F32)],
        compiler_params=_params(1),
    )(dx, proj, proj, swa, mla, w_out)


def _matmul_tn(a, b, name):
    s, m = a.shape
    n = b.shape[1]
    tm, tn, tk = min(1024, m), min(1024, n), min(2048, s)
    nk = s // tk

    def body(a_ref, b_ref, o_ref, acc_ref):
        k = pl.program_id(2)

        @pl.when(k == 0)
        def _():
            acc_ref[...] = jnp.zeros(acc_ref.shape, F32)
        acc_ref[...] += _mm_tn(a_ref[...], b_ref[...])

        @pl.when(k == nk - 1)
        def _():
            o_ref[...] = acc_ref[...].astype(o_ref.dtype)

    return pl.pallas_call(
        body, name=name, grid=(m // tm, n // tn, nk),
        in_specs=[pl.BlockSpec((tk, tm), lambda i, j, k: (k, i)),
                  pl.BlockSpec((tk, tn), lambda i, j, k: (k, j))],
        out_specs=pl.BlockSpec((tm, tn), lambda i, j, k: (i, j)),
        out_shape=_sds((m, n), GRAD_DTYPE),
        scratch_shapes=[pltpu.VMEM((tm, tn), F32)],
        compiler_params=_params(3),
    )(a, b)


def _grad_w_in(h, dqa, dgate, dlat, dkva, blocks):
    s = h.shape[0]
    tm, tn, tk = 1024, 1024, min(2048, s)
    nk = s // tk
    grid = (D_MODEL // tm, NP // tn, nk)
    na = len(blocks)

    def body(*refs):
        a_ref, dqa_ref, dg8_ref, dlat_ref, dkva_ref = refs[:5]
        o_ref = refs[5 + na]
        acc_ref = refs[6 + 2 * na]
        i, j, k = pl.program_id(0), pl.program_id(1), pl.program_id(2)
        if na:
            ex = _Exchange(refs[5:5 + na], refs[6 + na:6 + 2 * na], refs[7 + 2 * na:], gather=False)

            @pl.when((i == 0) & (j == 0) & (k == 0))
            def _():
                ex.start()

        @pl.when(k == 0)
        def _():
            acc_ref[...] = jnp.zeros(acc_ref.shape, F32)

        @pl.when(j == QA // tn)
        def _():
            acc_ref[...] += _mm_tn(a_ref[...], dqa_ref[...])

        @pl.when((j == GA // tn) | (j == GB // tn))
        def _():
            acc_ref[...] += _mm_tn(a_ref[...], dg8_ref[...])

        @pl.when(j == CQ // tn)
        def _():
            acc_ref[:, 0:768] += _mm_tn(a_ref[...], dlat_ref[...])
            acc_ref[:, 768:1024] += _mm_tn(a_ref[...], dkva_ref[...])

        @pl.when(k == nk - 1)
        def _():
            o_ref[...] = acc_ref[...].astype(o_ref.dtype)

        if na:
            @pl.when((i == grid[0] - 1) & (j == grid[1] - 1) & (k == nk - 1))
            def _():
                ex.wait()

    def when(group, width):
        return pl.BlockSpec((tk, width), lambda i, j, k: (jnp.where(j == group, k, 0), 0))

    outs = pl.pallas_call(
        body, name="grad_w_in_scatter" if na else "grad_w_in", grid=grid,
        in_specs=[pl.BlockSpec((tk, tm), lambda i, j, k: (k, i)),
                  when(QA // tn, 1024),
                  pl.BlockSpec((tk, 1024), lambda i, j, k: (
                      jnp.where((j == GA // tn) | (j == GB // tn), k, 0),
                      jnp.clip(j - GA // tn, 0, 1))),
                  when(CQ // tn, 768), when(CQ // tn, 256)] + [HBM_SPEC] * na,
        out_specs=[pl.BlockSpec((tm, tn), lambda i, j, k: (i, j))] + [HBM_SPEC] * na,
        out_shape=[_sds((D_MODEL, NP), GRAD_DTYPE)] + _exchange_shapes(blocks, False),
        scratch_shapes=[pltpu.VMEM((tm, tn), F32)] + (_Exchange.semaphores(na) if na else []),
        compiler_params=_params(3),
    )(h, dqa, dgate, dlat, dkva, *blocks)
    return outs[0], list(outs[1:])


def _swa_bwd(proj, sinks, do, o, blocks):
    s = proj.shape[0]
    nb = s // BLOCK
    na = len(blocks)

    def body(*refs):
        sink_ref, q_ref, kp_ref, kc_ref, do_ref, o_ref = refs[:6]
        dq_ref, dkv_ref, dsink_ref = refs[6 + na:9 + na]
        carry_ref, bias_ref = refs[9 + 2 * na:11 + 2 * na]
        n = pl.program_id(0)
        _swa_fill_bias(n, bias_ref)
        if na:
            ex = _Exchange(refs[6:6 + na], refs[9 + na:9 + 2 * na], refs[11 + 2 * na:], gather=False)

        @pl.when(n == 0)
        def _():
            carry_ref[...] = jnp.zeros(carry_ref.shape, F32)
            dsink_ref[...] = jnp.zeros(dsink_ref.shape, F32)
            if na:
                ex.start()

        @pl.when(n < nb)
        def _():
            kk, vv = _swa_kv(kp_ref[...].astype(F32), kc_ref[...].astype(F32))
            lane_s = lax.broadcasted_iota(jnp.int32, (8, 128), 1)
            qs = _swa_stack(q_ref) * 0.125
            dos = _swa_stack(do_ref)
            dlt = []
            for j in range(2):
                prod_t = (_swa_pairs(do_ref, j) * _swa_pairs(o_ref, j)).T
                dlt += [jnp.sum(prod_t[0:64], axis=0, keepdims=True),
                        jnp.sum(prod_t[64:128], axis=0, keepdims=True)]
            dlt = jnp.concatenate(dlt, axis=1)
            ex, es, inv = _swa_softmax(qs, kk, bias_ref[...], _swa_sink_row(sink_ref))
            p = ex * inv
            ds = p * (_mm_nt(vv, dos) - dlt)
            sink_term = es * inv * dlt
            dsink = jnp.zeros((8, 128), F32)
            for b in range(SWA_HEADS):
                dsh = -jnp.sum(sink_term[:, 128 * b:128 * (b + 1)], axis=1, keepdims=True)
                dsink = dsink + jnp.where(lane_s == _swa_head(b), dsh, 0.0)
            for j, dq in enumerate(_swa_unstack(_mm_tn(ds, kk))):
                for a in range(4):
                    cols = slice(128 * (4 * j + a), 128 * (4 * j + a + 1))
                    dq_ref[:, cols] = (dq[128 * a:128 * (a + 1)] * 0.125).astype(dq_ref.dtype)
            contrib = jnp.concatenate([_mm(ds, qs), _mm(p, dos)], axis=1)
            dkv_ref[...] = (carry_ref[...] + contrib[0:BLOCK]).astype(dkv_ref.dtype)
            carry_ref[...] = contrib[BLOCK:2 * BLOCK]
            dsink_ref[...] += dsink

        @pl.when(n == nb)
        def _():
            dkv_ref[...] = carry_ref[...].astype(dkv_ref.dtype)
            if na:
                ex.wait()

    cur = lambda n: (jnp.minimum(n, nb - 1), 0)
    outs = pl.pallas_call(
        body, name="swa_bwd_scatter" if na else "swa_bwd", grid=(nb + 1,),
        in_specs=[pl.BlockSpec(memory_space=pltpu.SMEM),
                  pl.BlockSpec((BLOCK, 1024), cur),
                  pl.BlockSpec((BLOCK, 256), lambda n: (jnp.clip(n - 1, 0, nb - 1), KVA // 256)),
                  pl.BlockSpec((BLOCK, 256), lambda n: (jnp.minimum(n, nb - 1), KVA // 256)),
                  pl.BlockSpec((BLOCK, 1024), cur), pl.BlockSpec((BLOCK, 1024), cur)]
        + [HBM_SPEC] * na,
        out_specs=[pl.BlockSpec((BLOCK, 1024), cur),
                   pl.BlockSpec((BLOCK, 256), lambda n: (jnp.maximum(n - 1, 0), 0)),
                   pl.BlockSpec((8, 128), lambda n: (0, 0))] + [HBM_SPEC] * na,
        out_shape=[_sds((s, 1024), MXU_DTYPE), _sds((s, 256), MXU_DTYPE), _sds((8, 128), F32)]
        + _exchange_shapes(blocks, False),
        scratch_shapes=[pltpu.VMEM((BLOCK, 256), F32), pltpu.VMEM((2 * BLOCK, SWA_STACK), F32)]
        + (_Exchange.semaphores(na) if na else []),
        compiler_params=_params(1),
    )(sinks, proj, proj, proj, do, o, *blocks)
    return outs[0], outs[1], outs[2], list(outs[3:])


def _mla_bwd(qh, kh, vh, do, dlt, lse, blocks):
    s = qh.shape[1]
    t = min(MLA_TILE, s)
    nq = s // t

    na = len(blocks)

    def body(*refs):
        q_ref, k_ref, v_ref, do_ref, dlt_ref, lse_ref = refs[:6]
        dq_ref, dk_ref, dv_ref = refs[6 + na:9 + na]
        h, j = pl.program_id(0), pl.program_id(1)
        if na:
            ex = _Exchange(refs[6:6 + na], refs[9 + na:9 + 2 * na], refs[9 + 2 * na:], gather=False)

            @pl.when((h == 0) & (j == 0))
            def _():
                ex.start()

        def step(start, width, diagonal=False):
            rows = pl.ds(pl.multiple_of(start, t), width)
            q, k, dout = q_ref[rows, :], k_ref[...], do_ref[rows, :]
            sc = _scores_t(k, q, t, diagonal)
            p = jnp.exp2(sc * MLA_C2 - lse_ref[0:1, rows])
            dv = _mm(p, dout)
            ds = p * (_mm_nt(v_ref[...], dout) - dlt_ref[0:1, rows])
            dk = _mm(ds, q)
            dq = _mm_tn(ds, k)
            return rows, dq, dk, dv

        rows, dq, dk, dv = step(j * t, t, diagonal=True)
        dk_ref[...] = dk
        dv_ref[...] = dv

        @pl.when(j == 0)
        def _():
            dq_ref[rows, :] = dq * MLA_SCALE

        @pl.when(j > 0)
        def _():
            dq_ref[rows, :] = (dq_ref[rows, :] + dq) * MLA_SCALE

        def above_diagonal(start, width):
            rows, dq, dk, dv = step(start, width)
            dk_ref[...] += dk
            dv_ref[...] += dv

            @pl.when(j == 0)
            def _():
                dq_ref[rows, :] = dq

            @pl.when(j > 0)
            def _():
                dq_ref[rows, :] += dq

        n_above = nq - 1 - j

        def quad(jj, carry):
            above_diagonal((j + 1 + 4 * jj) * t, 4 * t)
            return carry

        lax.fori_loop(0, n_above // 4, quad, 0)

        @pl.when(n_above % 4 >= 2)
        def _():
            above_diagonal((j + 1 + 4 * (n_above // 4)) * t, 2 * t)

        @pl.when(n_above % 2 == 1)
        def _():
            above_diagonal((nq - 1) * t, t)

        dk_ref[...] *= MLA_SCALE

        if na:
            @pl.when((h == MLA_HEADS - 1) & (j == nq - 1))
            def _():
                ex.wait()

    head = lambda h, j: (h, 0, 0)
    kv_map = lambda h, j: (h, j, 0)
    outs = pl.pallas_call(
        body, name="mla_bwd_scatter" if na else "mla_bwd", grid=(MLA_HEADS, nq),
        in_specs=[pl.BlockSpec((None, s, 256), head), pl.BlockSpec((None, t, 256), kv_map),
                  pl.BlockSpec((None, t, 128), kv_map),
                  pl.BlockSpec((s, 128), lambda h, j: (0, h)),
                  pl.BlockSpec((None, 8, s), head), pl.BlockSpec((None, 8, s), head)]
        + [HBM_SPEC] * na,
        out_specs=[pl.BlockSpec((None, s, 256), head),
                   pl.BlockSpec((None, t, 256), kv_map), pl.BlockSpec((None, t, 128), kv_map)]
        + [HBM_SPEC] * na,
        out_shape=[_sds((MLA_HEADS, s, 256), F32), _sds((MLA_HEADS, s, 256), F32),
                   _sds((MLA_HEADS, s, 128), F32)] + _exchange_shapes(blocks, False),
        scratch_shapes=_Exchange.semaphores(na) if na else [],
        compiler_params=_params(2),
    )(qh, kh, vh, do, dlt, lse, *blocks)
    return outs[0], outs[1], outs[2], list(outs[3:])


def _mla_prep_bwd(dqh, dkh, dvh, proj, gq, gkv, w_q, w_kv, tabs):
    s = proj.shape[0]
    tm = min(512, s)
    nm = s // tm
    c, s1, s2 = tabs

    def body(dq_ref, dk_ref, dv_ref, p_ref, gq_ref, gkv_ref, wq_ref, wkv_ref,
             c_ref, s1_ref, s2_ref, dp_ref, dwq_ref, dwkv_ref, dgq_ref, dgkv_ref,
             dqf_ref, dkvf_ref, dwq_acc, dwkv_acc):
        @pl.when(pl.program_id(0) == 0)
        def _():
            dgq_ref[...] = jnp.zeros(dgq_ref.shape, F32)
            dgkv_ref[...] = jnp.zeros(dgkv_ref.shape, F32)
            dwq_acc[...] = jnp.zeros(dwq_acc.shape, F32)
            dwkv_acc[...] = jnp.zeros(dwkv_acc.shape, F32)
        cc, ns1, ns2 = c_ref[...], -s1_ref[...], -s2_ref[...]
        dkr = jnp.zeros((tm, 128), F32)
        for h in range(MLA_HEADS):
            dqf_ref[:, 128 * h:128 * (h + 1)] = dq_ref[h, :, 0:128].astype(dqf_ref.dtype)
            dqf_ref[:, 1024 + 128 * h:1024 + 128 * (h + 1)] = _rope(
                dq_ref[h, :, 128:256], cc, ns1, ns2).astype(dqf_ref.dtype)
            dkvf_ref[:, 128 * h:128 * (h + 1)] = dk_ref[h, :, 0:128].astype(dkvf_ref.dtype)
            dkvf_ref[:, 1024 + 128 * h:1024 + 128 * (h + 1)] = dv_ref[h].astype(dkvf_ref.dtype)
            dkr = dkr + dk_ref[h, :, 128:256]
        dcqn = _mm_nt(dqf_ref[...], wq_ref[...])
        dckvn = _mm_nt(dkvf_ref[...], wkv_ref[...])
        cqh, rq = _rownorm(p_ref[:, 0:384].astype(F32))
        ckvh, rkv = _rownorm(p_ref[:, 384:640].astype(F32))
        dgq_ref[...] += jnp.sum(dcqn * cqh, axis=0, keepdims=True)
        dgkv_ref[...] += jnp.sum(dckvn * ckvh, axis=0, keepdims=True)
        dp_ref[:, 0:384] = _rownorm_bwd(dcqn * gq_ref[...], cqh, rq).astype(dp_ref.dtype)
        dp_ref[:, 384:640] = _rownorm_bwd(dckvn * gkv_ref[...], ckvh, rkv).astype(dp_ref.dtype)
        dp_ref[:, 640:768] = _rope(dkr, cc, ns1, ns2).astype(dp_ref.dtype)
        dwq_acc[...] += _mm_tn(cqh * gq_ref[...], dqf_ref[...])
        dwkv_acc[...] += _mm_tn(ckvh * gkv_ref[...], dkvf_ref[...])

        @pl.when(pl.program_id(0) == nm - 1)
        def _():
            dwq_ref[...] = dwq_acc[...].astype(dwq_ref.dtype)
            dwkv_ref[...] = dwkv_acc[...].astype(dwkv_ref.dtype)

    row = lambda i: (i, 0)
    fixed = lambda i: (0, 0)
    head = lambda i: (0, i, 0)
    return pl.pallas_call(
        body, name="mla_prep_bwd", grid=(nm,),
        in_specs=[pl.BlockSpec((MLA_HEADS, tm, 256), head), pl.BlockSpec((MLA_HEADS, tm, 256), head),
                  pl.BlockSpec((MLA_HEADS, tm, 128), head),
                  pl.BlockSpec((tm, 768), lambda i: (i, CQ // 768)),
                  pl.BlockSpec((1, Q_RANK), fixed), pl.BlockSpec((1, KV_RANK), fixed),
                  pl.BlockSpec((Q_RANK, 2048), fixed), pl.BlockSpec((KV_RANK, 2048), fixed),
                  pl.BlockSpec((tm, 128), row), pl.BlockSpec((tm, 128), row),
                  pl.BlockSpec((tm, 128), row)],
        out_specs=[pl.BlockSpec((tm, 768), row), pl.BlockSpec((Q_RANK, 2048), fixed),
                   pl.BlockSpec((KV_RANK, 2048), fixed),
                   pl.BlockSpec((1, Q_RANK), fixed), pl.BlockSpec((1, KV_RANK), fixed)],
        out_shape=[_sds((s, 768), MXU_DTYPE), _sds((Q_RANK, 2048), GRAD_DTYPE),
                   _sds((KV_RANK, 2048), GRAD_DTYPE),
                   _sds((1, Q_RANK), F32), _sds((1, KV_RANK), F32)],
        scratch_shapes=[pltpu.VMEM((tm, 2048), MXU_DTYPE), pltpu.VMEM((tm, 2048), MXU_DTYPE),
                        pltpu.VMEM((Q_RANK, 2048), F32), pltpu.VMEM((KV_RANK, 2048), F32)],
        compiler_params=_params(1),
    )(dqh, dkh, dvh, proj, gq, gkv, w_q, w_kv, c, s1, s2)


def _in_proj_bwd(dqa, dgate, dlat, dkva, w, x, dx_out, g, blocks):
    s = x.shape[0]
    tm = min(256, s)
    nm = s // tm
    na = len(blocks)

    def body(*refs):
        dqa_ref, dg8_ref, dlat_ref, dkva_ref, w_ref, x_ref, dxo_ref, g_ref = refs[:8]
        dx_ref, dxb_ref, dg_ref = refs[8 + na:11 + na]
        if na:
            ex = _Exchange(refs[8:8 + na], refs[11 + na:11 + 2 * na], refs[11 + 2 * na:], gather=False)

        @pl.when(pl.program_id(0) == 0)
        def _():
            dg_ref[...] = jnp.zeros(dg_ref.shape, F32)
            if na:
                ex.start()

        dh = (_mm_nt(dqa_ref[...], w_ref[:, QA:QA + 1024])
              + _mm_nt(dg8_ref[:, 0:1024], w_ref[:, GA:GA + 1024])
              + _mm_nt(dg8_ref[:, 1024:2048], w_ref[:, GB:GB + 1024])
              + _mm_nt(dlat_ref[...], w_ref[:, CQ:CQ + 768])
              + _mm_nt(dkva_ref[...], w_ref[:, KVA:KVA + 256]))
        xh, r = _rownorm(x_ref[...])
        dg_ref[...] += jnp.sum(dh * xh, axis=0, keepdims=True)
        dx = dxo_ref[...] + _rownorm_bwd(dh * g_ref[...], xh, r)
        dx_ref[...] = dx
        dxb_ref[...] = dx.astype(dxb_ref.dtype)

        if na:
            @pl.when(pl.program_id(0) == nm - 1)
            def _():
                ex.wait()

    row = lambda i: (i, 0)
    fixed = lambda i: (0, 0)
    outs = pl.pallas_call(
        body, name="in_proj_bwd_scatter" if na else "in_proj_bwd", grid=(nm,),
        in_specs=[pl.BlockSpec((tm, 1024), row), pl.BlockSpec((tm, 2048), row),
                  pl.BlockSpec((tm, 768), row), pl.BlockSpec((tm, 256), row),
                  pl.BlockSpec((D_MODEL, NP), fixed, pipeline_mode=pl.Buffered(1)),
                  pl.BlockSpec((tm, D_MODEL), row), pl.BlockSpec((tm, D_MODEL), row),
                  pl.BlockSpec((1, D_MODEL), fixed)] + [HBM_SPEC] * na,
        out_specs=[pl.BlockSpec((tm, D_MODEL), row), pl.BlockSpec((tm, D_MODEL), row),
                   pl.BlockSpec((1, D_MODEL), fixed)] + [HBM_SPEC] * na,
        out_shape=[_sds((s, D_MODEL), F32), _sds((s, D_MODEL), MXU_DTYPE), _sds((1, D_MODEL), F32)]
        + _exchange_shapes(blocks, False),
        scratch_shapes=_Exchange.semaphores(na) if na else [],
        compiler_params=_params(1),
    )(dqa, dgate, dlat, dkva, w, x, dx_out, g, *blocks)
    return outs[0], outs[1], outs[2], list(outs[3:])


def _adamw_update(g, w_ref, m_ref, v_ref, g_ref, d_ref, nm_ref, nv_ref):
    m2 = ADAM_B1 * m_ref[...] + (1.0 - ADAM_B1) * g
    v2 = ADAM_B2 * v_ref[...] + (1.0 - ADAM_B2) * (g * g)
    m_hat = m2 / (1.0 - ADAM_B1 ** ADAM_STEP)
    v_hat = v2 / (1.0 - ADAM_B2 ** ADAM_STEP)
    g_ref[...] = g
    d_ref[...] = -ADAM_LR * (m_hat / (jnp.sqrt(v_hat) + ADAM_EPS) + ADAM_WD * w_ref[...])
    nm_ref[...] = m2
    nv_ref[...] = v2


def _reduce_adamw_in(parts, w_t, m_t, v_t, name):
    n_layers = len(parts)
    tc = 512
    nc = D_MODEL // tc

    def body(*refs):
        p_refs = refs[:n_layers]
        w_ref, m_ref, v_ref, g_ref, d_ref, nm_ref, nv_ref = refs[n_layers:]
        layer = pl.program_id(0)
        for l in range(n_layers):
            @pl.when(layer == l)
            def _(l=l):
                g = p_refs[l][0].astype(F32)
                for k in range(1, N_DEV):
                    g = g + p_refs[l][k].astype(F32)
                _adamw_update(g[0:SHARD_COLS], w_ref, m_ref, v_ref, g_ref, d_ref, nm_ref, nv_ref)

    def part_spec(l):
        return pl.BlockSpec((N_DEV, SHARD_PAD, tc),
                            lambda layer, j: (0, 0, jnp.where(layer == l, j, 0)))

    blk = pl.BlockSpec((None, SHARD_COLS, tc), lambda layer, j: (layer, 0, j))
    return pl.pallas_call(
        body, name=name, grid=(n_layers, nc),
        in_specs=[part_spec(l) for l in range(n_layers)] + [blk, blk, blk],
        out_specs=[blk, blk, blk, blk],
        out_shape=[_sds((n_layers, SHARD_COLS, D_MODEL), F32)] * 4,
        compiler_params=_params(2),
    )(*parts, w_t, m_t, v_t)


def _reduce_adamw(parts, w, m, v, name):
    n_layers = len(parts)
    rows, part_cols = parts[0].shape[1:]
    cols = w.shape[-1]
    lanes = -(-cols // 128) * 128
    tr = rows
    for cand in (1024, 512, 256, 128, 64, 32, 16, 8):
        if rows % cand == 0 and N_DEV * cand * lanes * 4 <= 8 * 1024 * 1024:
            tr = cand
            break
    nr = rows // tr

    def body(*refs):
        p_refs = refs[:n_layers]
        w_ref, m_ref, v_ref, g_ref, d_ref, nm_ref, nv_ref = refs[n_layers:]
        layer = pl.program_id(0)
        for l in range(n_layers):
            @pl.when(layer == l)
            def _(l=l):
                g = p_refs[l][0, :, 0:cols].astype(F32)
                for k in range(1, N_DEV):
                    g = g + p_refs[l][k, :, 0:cols].astype(F32)
                _adamw_update(g, w_ref, m_ref, v_ref, g_ref, d_ref, nm_ref, nv_ref)

    def part_spec(l):
        return pl.BlockSpec((N_DEV, tr, part_cols),
                            lambda layer, i: (0, jnp.where(layer == l, i, 0), 0))

    blk = pl.BlockSpec((tr, cols), lambda layer, i: (layer * nr + i, 0))
    return pl.pallas_call(
        body, name=name, grid=(n_layers, nr),
        in_specs=[part_spec(l) for l in range(n_layers)] + [blk, blk, blk],
        out_specs=[blk, blk, blk, blk],
        out_shape=[_sds((n_layers * rows, cols), F32)] * 4,
        compiler_params=_params(2),
    )(*parts, w, m, v)


def _position():
    x, y, c = lax.axis_index("x"), lax.axis_index("y"), lax.axis_index("c")
    return x, y, c


def _index(px, py, pc):
    return 4 * px + 2 * py + pc


HBM_SPEC = pl.BlockSpec(memory_space=pltpu.HBM)
IN_BLOCKED = (D_MODEL, N_DEV * SHARD_PAD)
IN_BLOCKED_T = (N_DEV * SHARD_PAD, D_MODEL)


def _block(ref, idx):
    if tuple(ref.shape) == IN_BLOCKED:
        return ref.at[:, pl.ds(pl.multiple_of(idx * SHARD_PAD, SHARD_PAD), SHARD_PAD)]
    if tuple(ref.shape) == IN_BLOCKED_T:
        return ref.at[pl.ds(pl.multiple_of(idx * SHARD_PAD, SHARD_PAD), SHARD_PAD)]
    return ref.at[idx]


class _Gather:
    def __init__(self, srcs, dsts, sems):
        self.srcs, self.dsts = srcs, dsts
        self.send_sems, self.recv_sems, self.local_sems = sems
        x, y, c = _position()
        self.c = c
        self.me, self.sibling = (x, y, c), (x, y, 1 - c)
        self.chips = [(1 - x, y), (x, 1 - y), (1 - x, 1 - y)]

    def _copy(self, a, k, block, to, own=False):
        slot = _block(self.dsts[a], _index(*block))
        return pltpu.make_async_remote_copy(
            src_ref=self.srcs[a] if own else slot, dst_ref=slot,
            send_sem=self.send_sems.at[7 * a + k], recv_sem=self.recv_sems.at[7 * a + k],
            device_id=to, device_id_type=MESH)

    def _local(self, a):
        return pltpu.make_async_copy(self.srcs[a], _block(self.dsts[a], _index(*self.me)),
                                     self.local_sems.at[a])

    def _first(self, a):
        return [self._copy(a, 0, self.me, self.sibling, own=True)] + [
            self._copy(a, 1 + j, self.me, (*chip, self.c), own=True)
            for j, chip in enumerate(self.chips)]

    def _passed(self, a, j):
        return self._copy(a, 4 + j, (*self.chips[j], self.c), self.sibling)

    def start(self):
        for a in range(len(self.srcs)):
            self._local(a).start()
            for cp in self._first(a):
                cp.start()

    def forward(self):
        for j, chip in enumerate(self.chips):
            for a in range(len(self.srcs)):
                self._copy(a, 1 + j, (*chip, self.c), self.me).wait_recv()
                self._passed(a, j).start()

    def finish(self):
        for a in range(len(self.srcs)):
            self._copy(a, 0, self.sibling, self.me).wait_recv()
            for j, chip in enumerate(self.chips):
                self._copy(a, 4 + j, (*chip, 1 - self.c), self.me).wait_recv()
            for cp in self._first(a):
                cp.wait_send()
            for j in range(3):
                self._passed(a, j).wait_send()
            self._local(a).wait()


def _all_gather(shards, name):
    na = len(shards)

    def body(*refs):
        g = _Gather(refs[:na], refs[na:2 * na], refs[2 * na:])
        g.start()
        g.forward()
        g.finish()

    return pl.pallas_call(
        body, name=name,
        in_specs=[HBM_SPEC] * na, out_specs=[HBM_SPEC] * na,
        out_shape=_exchange_shapes(shards, True),
        scratch_shapes=_Exchange.semaphores(na),
    )(*shards)


class _Exchange:
    def __init__(self, srcs, dsts, sems, gather):
        self.srcs, self.dsts, self.gather = srcs, dsts, gather
        self.send_sems, self.recv_sems, self.local_sems = sems
        x, y, c = _position()
        self.me = _index(x, y, c)
        self.peers = [(x ^ ((k >> 2) & 1), y ^ ((k >> 1) & 1), c ^ (k & 1)) for k in range(1, N_DEV)]

    @staticmethod
    def semaphores(na):
        return [pltpu.SemaphoreType.DMA((7 * na,)), pltpu.SemaphoreType.DMA((7 * na,)),
                pltpu.SemaphoreType.DMA((na,))]

    def _src(self, a, slot):
        return self.srcs[a] if self.gather else _block(self.srcs[a], slot)

    def _local(self, a):
        return pltpu.make_async_copy(self._src(a, self.me), self.dsts[a].at[self.me],
                                     self.local_sems.at[a])

    def _send(self, a, k):
        peer = self.peers[k]
        return pltpu.make_async_remote_copy(
            src_ref=self._src(a, _index(*peer)), dst_ref=self.dsts[a].at[self.me],
            send_sem=self.send_sems.at[7 * a + k], recv_sem=self.recv_sems.at[7 * a + k],
            device_id=peer, device_id_type=MESH)

    def _arrival(self, a, k):
        landed = self.dsts[a].at[_index(*self.peers[k])]
        return pltpu.make_async_remote_copy(
            src_ref=landed, dst_ref=landed,
            send_sem=self.send_sems.at[7 * a + k], recv_sem=self.recv_sems.at[7 * a + k],
            device_id=self.peers[k], device_id_type=MESH)

    def start(self):
        for a in range(len(self.srcs)):
            self._local(a).start()
            for k in range(N_DEV - 1):
                self._send(a, k).start()

    def wait(self):
        for a in range(len(self.srcs)):
            for k in range(N_DEV - 1):
                self._arrival(a, k).wait_recv()
            for k in range(N_DEV - 1):
                self._send(a, k).wait_send()
            self._local(a).wait()


def _exchange_shapes(arrays, gather):
    def shape(a):
        if gather:
            return IN_BLOCKED if a.shape == (D_MODEL, SHARD_PAD) else (N_DEV,) + a.shape
        return (N_DEV, SHARD_PAD, D_MODEL) if a.shape == IN_BLOCKED_T else a.shape
    return [_sds(shape(a), a.dtype) for a in arrays]


def _exchange_call(arrays, gather, name):
    na = len(arrays)

    def body(*refs):
        ex = _Exchange(refs[:na], refs[na:2 * na], refs[2 * na:], gather)
        ex.start()
        ex.wait()

    return pl.pallas_call(
        body, name=name,
        in_specs=[HBM_SPEC] * na, out_specs=[HBM_SPEC] * na,
        out_shape=_exchange_shapes(arrays, gather),
        scratch_shapes=_Exchange.semaphores(na),
    )(*arrays)


def _layer_fwd(x, small, w_in, g_rest, rest_shards, tabs, next_shards):
    attn_g, sinks, gq, gkv = small
    proj, h, gathered_rest = _in_proj(x, attn_g, w_in, rest_shards)
    w_q, w_kv, w_o = _rest_from_gathered(*(gathered_rest if rest_shards else g_rest))
    swa = _swa_fwd(proj, sinks)
    qh, kh, vh, vth = _mla_prep(proj, gq, gkv, w_q, w_kv, tabs)
    mla, lse, gathered = _mla_fwd(qh, kh, vth, next_shards)
    x_next, y = _out_proj(x, proj, swa, mla, w_o)
    return x_next, (x, proj, h, swa, qh, kh, vh, mla, lse, y), (w_in, w_q, w_kv, w_o), gathered


def _layer_bwd(dx, dxb, saved, small, weights, tabs, pending, send_own):
    attn_g, sinks, gq, gkv = small
    w_in, w_q, w_kv, w_o = weights
    x, proj, h, swa, qh, kh, vh, mla, lse, y = saved
    d_o = _matmul_tn(y, dxb, "grad_w_out")
    o_block = d_o.reshape(N_DEV, 256, D_MODEL)
    do_a, do_b, dgate, dlt = _out_proj_bwd(dxb, proj, swa, mla, w_o)
    dqa, dkva, dsk, got_o = _swa_bwd(proj, sinks, do_a, swa, [o_block] if send_own else [])
    dqh, dkh, dvh, received = _mla_bwd(qh, kh, vh, do_b, dlt, lse, pending)
    dlat, d_q, d_kv, d_gq, d_gkv = _mla_prep_bwd(dqh, dkh, dvh, proj, gq, gkv, w_q, w_kv, tabs)
    qkv_blocks = _qkv_grad_blocks(d_q, d_kv)
    d_inp, got_qkv = _grad_w_in(h, dqa, dgate, dlat, dkva, qkv_blocks if send_own else [])
    in_block = _in_grad_blocks(d_inp)
    dx, dxb, d_attn, got_in = _in_proj_bwd(dqa, dgate, dlat, dkva, w_in, x, dx, attn_g,
                                           [in_block] if send_own else [])
    small_grads = (d_attn, dsk[0:1, 0:SWA_HEADS], d_gq, d_gkv)
    blocks = [in_block] + qkv_blocks + [o_block]
    return dx, dxb, small_grads, blocks, received, got_in + got_qkv + got_o


def _pack_small_grads(small_grads, d_final, loss):
    d_attn, d_sink, d_gq, d_gkv = zip(*small_grads)
    return jnp.concatenate([
        jnp.concatenate(d_attn, axis=0).reshape(64, 128),
        jnp.concatenate(d_gq, axis=0).reshape(12, 128),
        jnp.concatenate(d_gkv, axis=0).reshape(8, 128),
        d_final.reshape(16, 128),
        jnp.pad(jnp.concatenate(d_sink, axis=1), ((0, 0), (0, 64))),
        loss[0:1],
        jnp.zeros((PACK_ROWS - ROW_LOSS - 1, 128), F32)], axis=0)


def _pack_small(attn, qa, kva, final, sinks):
    return jnp.concatenate([
        attn.reshape(64, 128), qa.reshape(12, 128), kva.reshape(8, 128), final.reshape(16, 128),
        jnp.pad(sinks.reshape(1, 64), ((0, 0), (0, 64))),
        jnp.zeros((PACK_ROWS - ROW_SINK - 1, 128), F32)], axis=0)


def _unpack_small(p):
    return (p[ROW_ATTN:ROW_QA].reshape(DEPTH, D_MODEL), p[ROW_SINK, 0:64].reshape(DEPTH, SWA_HEADS),
            p[ROW_QA:ROW_KVA].reshape(DEPTH, Q_RANK), p[ROW_KVA:ROW_FINAL].reshape(DEPTH, KV_RANK),
            p[ROW_FINAL:ROW_SINK].reshape(D_MODEL))


def kernel(x, attn_norm_g, w_in, swa_sinks, q_a_norm_g, kv_a_norm_g, w_q_b, w_kv_b, w_out, final_norm_g, loss_target, m_attn_norm_g, m_w_in, m_swa_sinks, m_q_a_norm_g, m_kv_a_norm_g, m_w_q_b, m_w_kv_b, m_w_out, m_final_norm_g, v_attn_norm_g, v_w_in, v_swa_sinks, v_q_a_norm_g, v_kv_a_norm_g, v_w_q_b, v_w_kv_b, v_w_out, v_final_norm_g):
    xs, tgt = x[0], loss_target[0]
    tabs = _rope_tables(xs.shape[0])
    shards = [w.astype(MXU_DTYPE) for w in (w_in, w_q_b, w_kv_b, w_out)]
    shards[0] = jnp.pad(shards[0], ((0, 0), (0, 0), (0, SHARD_PAD - SHARD_COLS)))
    layer_shards = lambda l: [w[l] for w in shards]
    smalls = [(attn_norm_g[l:l + 1], swa_sinks[l], q_a_norm_g[l:l + 1], kv_a_norm_g[l:l + 1])
              for l in range(DEPTH)]

    gathered = list(_all_gather(layer_shards(0)[:1], "gather_weights")) + [None] * 3
    weights, saved = [None] * DEPTH, []
    for l in range(DEPTH):
        next_shards = layer_shards(l + 1) if l + 1 < DEPTH else []
        rest_shards = layer_shards(0)[1:] if l == 0 else []
        xs, acts, weights[l], gathered = _layer_fwd(
            xs, smalls[l], _w_in_from_gathered(gathered[0]), gathered[1:], rest_shards, tabs,
            next_shards)
        saved.append(acts)
    dx, dxb, d_final, loss = _final_loss(xs, tgt, final_norm_g.reshape(1, D_MODEL))

    received, small_grads, pending = [None] * DEPTH, [None] * DEPTH, []
    for l in reversed(range(DEPTH)):
        dx, dxb, small_grads[l], blocks, arrived, arrived_own = _layer_bwd(
            dx, dxb, saved[l], smalls[l], weights[l], tabs, pending, send_own=(l == 0))
        if pending:
            received[l + 1] = arrived
        pending = blocks
    received[0] = arrived_own
    small = _exchange_call([_pack_small_grads(small_grads, d_final, loss)], True, "gather_small")[0]

    big = []
    for a, (w, m, v, name) in enumerate(zip((w_in, w_q_b, w_kv_b, w_out),
                                            (m_w_in, m_w_q_b, m_w_kv_b, m_w_out),
                                            (v_w_in, v_w_q_b, v_w_kv_b, v_w_out),
                                            ("adamw_w_in", "adamw_w_q_b", "adamw_w_kv_b",
                                             "adamw_w_out"))):
        parts = [received[l][a] for l in range(DEPTH)]
        if a == 0:
            swap = lambda t: t.transpose(0, 2, 1)
            outs = _reduce_adamw_in(parts, swap(w), swap(m), swap(v), name)
            big.append([swap(t) for t in outs])
        else:
            cols = w.shape[-1]
            flat = lambda t: t.reshape(-1, cols)
            outs = _reduce_adamw(parts, flat(w), flat(m), flat(v), name)
            big.append([t.reshape(w.shape) for t in outs])

    sm = _reduce_adamw(
        [small],
        _pack_small(attn_norm_g, q_a_norm_g, kv_a_norm_g, final_norm_g, swa_sinks),
        _pack_small(m_attn_norm_g, m_q_a_norm_g, m_kv_a_norm_g, m_final_norm_g, m_swa_sinks),
        _pack_small(v_attn_norm_g, v_q_a_norm_g, v_kv_a_norm_g, v_final_norm_g, v_swa_sinks),
        "adamw_small")
    loss = sm[0][ROW_LOSS, 0]
    kinds = []
    for t in range(4):
        attn, sinks, qa, kva, final = _unpack_small(sm[t])
        b_in, b_q, b_kv, b_o = (big[i][t] for i in range(4))
        kinds.append((attn, b_in, sinks, qa, kva, b_q, b_kv, b_o, final))
    return (loss, dx[None], *kinds[0], *kinds[1], *kinds[2], *kinds[3])
```

```python
import jax
import jax.numpy as jnp
import numpy as np
from jax import lax
from jax.experimental import pallas as pl
from jax.experimental.pallas import tpu as pltpu

F32 = jnp.float32
BF16 = jnp.bfloat16
MXU_DTYPE = BF16
GRAD_DTYPE = BF16
PROJ_DTYPE = BF16
ATTN_DTYPE = BF16

D_MODEL = 2048
DEPTH = 4
EPS = 1e-6
NEG = -1e30
BLOCK = 128
SWA_HEADS = 16
MLA_HEADS = 8
Q_RANK = 384
KV_RANK = 256
MLA_SCALE = 192 ** -0.5
MLA_C2 = MLA_SCALE * 1.4426950408889634
ROPE_THETA = 10000.0
IN_WIDTH = 4032

ADAM_LR = 0.001
ADAM_B1 = 0.9
ADAM_B2 = 0.999
ADAM_EPS = 1e-08
ADAM_WD = 0.01
ADAM_STEP = 10

N_DEV = 8
MESH = pl.DeviceIdType.MESH

NP = 4096
QA, GA, GB, CQ, CKV, KR, KVA = 0, 1024, 2048, 3072, 3456, 3712, 3840

ROW_ATTN, ROW_QA, ROW_KVA, ROW_FINAL, ROW_SINK, ROW_LOSS, PACK_ROWS = 0, 64, 76, 84, 100, 101, 104

VMEM_LIMIT = 56 * 1024 * 1024
MLA_TILE = 512
GATHER_FORWARD_HEAD = 7


def _sds(shape, dtype):
    return jax.ShapeDtypeStruct(shape, dtype)


def _params(n_axes):
    return pltpu.CompilerParams(dimension_semantics=("arbitrary",) * n_axes,
                                vmem_limit_bytes=VMEM_LIMIT)


def _mm(a, b):
    return jnp.dot(a.astype(MXU_DTYPE), b.astype(MXU_DTYPE), preferred_element_type=F32)


def _mm_nt(a, b):
    return lax.dot_general(a.astype(MXU_DTYPE), b.astype(MXU_DTYPE),
                           (((1,), (1,)), ((), ())), preferred_element_type=F32)


def _mm_tn(a, b):
    return lax.dot_general(a.astype(MXU_DTYPE), b.astype(MXU_DTYPE),
                           (((0,), (0,)), ((), ())), preferred_element_type=F32)


def _rownorm(x):
    r = lax.rsqrt(jnp.mean(x * x, axis=-1, keepdims=True) + EPS)
    return x * r, r


def _rownorm_bwd(dxh, xh, r):
    return r * (dxh - xh * jnp.mean(dxh * xh, axis=-1, keepdims=True))


def _rope(t, c, s1, s2):
    return t * c + pltpu.roll(t, 32, 1) * s1 + pltpu.roll(t, 96, 1) * s2


SHARD_COLS = IN_WIDTH // N_DEV
SHARD_PAD = 512


def _orig_col_of_padded():
    o = np.full((NP,), -1, np.int64)
    for start, width, orig in ((QA, 1024, 0), (KVA, 256, 1024), (GA, 1024, 1280), (CQ, 384, 2304),
                               (CKV, 256, 2688), (KR, 64, 2944), (GB, 1024, 3008)):
        o[start:start + width] = np.arange(orig, orig + width)
    return o


def _device_major_src():
    o = _orig_col_of_padded()
    return np.where(o >= 0, o + (SHARD_PAD - SHARD_COLS) * (o // SHARD_COLS), -1)


def _kernel_layout_src():
    o = _orig_col_of_padded()
    where = np.full((IN_WIDTH,), -1, np.int64)
    where[o[o >= 0]] = np.nonzero(o >= 0)[0]
    e = np.arange(N_DEV * SHARD_PAD)
    k, c = e // SHARD_PAD, e % SHARD_PAD
    return np.where(c < SHARD_COLS, where[np.minimum(SHARD_COLS * k + c, IN_WIDTH - 1)], -1)


def _permute_columns(x, src_of, name, transposed=False):
    rows, n_in = x.shape
    n_out = len(src_of)
    plan, mats = [], []
    for t in range(n_out // 128):
        srcs = src_of[128 * t:128 * (t + 1)]
        entry = []
        for u in sorted(set(int(s) // 128 for s in srcs if s >= 0)):
            m = np.zeros((128, 128), np.float32)
            for c, s in enumerate(srcs):
                if s >= 0 and s // 128 == u:
                    m[s % 128, c] = 1.0
            entry.append((u, len(mats)))
            mats.append(m)
        plan.append(entry)
    tr = min(512, rows)

    def body(x_ref, p_ref, o_ref):
        for t, entry in enumerate(plan):
            acc = jnp.zeros((tr, 128), F32)
            for u, idx in entry:
                acc = acc + jnp.dot(x_ref[:, 128 * u:128 * (u + 1)], p_ref[idx],
                                    preferred_element_type=F32)
            if transposed:
                o_ref[128 * t:128 * (t + 1), :] = acc.T.astype(o_ref.dtype)
            else:
                o_ref[:, 128 * t:128 * (t + 1)] = acc.astype(o_ref.dtype)

    table = jnp.asarray(np.stack(mats), x.dtype)
    return pl.pallas_call(
        body, name=name, grid=(rows // tr,),
        in_specs=[pl.BlockSpec((tr, n_in), lambda i: (i, 0)),
                  pl.BlockSpec(table.shape, lambda i: (0, 0, 0))],
        out_specs=(pl.BlockSpec((n_out, tr), lambda i: (0, i)) if transposed
                   else pl.BlockSpec((tr, n_out), lambda i: (i, 0))),
        out_shape=_sds((n_out, rows) if transposed else (rows, n_out), x.dtype),
        compiler_params=_params(1),
    )(x, table)


def _w_in_from_gathered(g_in):
    return _permute_columns(g_in, _device_major_src(), "w_in_layout")


def _rest_from_gathered(g_qb, g_kvb, g_out):
    qb = g_qb.transpose(1, 0, 2)
    rope = jnp.pad(qb[..., 128:], ((0, 0), (0, 0), (0, 64)))
    w_q = jnp.concatenate([qb[..., :128].reshape(Q_RANK, 1024),
                           rope.reshape(Q_RANK, 1024)], axis=-1)
    kvb = g_kvb.transpose(1, 0, 2)
    w_kv = jnp.concatenate([kvb[..., :128].reshape(KV_RANK, 1024),
                            kvb[..., 128:].reshape(KV_RANK, 1024)], axis=-1)
    w_o = g_out.reshape(D_MODEL, D_MODEL)
    return w_q, w_kv, w_o


def _in_grad_blocks(d_inp):
    return _permute_columns(d_inp, _kernel_layout_src(), "grad_w_in_layout", transposed=True)


def _qkv_grad_blocks(d_q, d_kv):
    qn = d_q[:, :1024].reshape(Q_RANK, 8, 128)
    qr = d_q[:, 1024:].reshape(Q_RANK, 8, 128)[..., :64]
    b_q = jnp.concatenate([qn, qr], axis=-1).transpose(1, 0, 2)
    kn = d_kv[:, :1024].reshape(KV_RANK, 8, 128)
    vv = d_kv[:, 1024:].reshape(KV_RANK, 8, 128)
    b_kv = jnp.concatenate([kn, vv], axis=-1).transpose(1, 0, 2)
    return [b_q, b_kv]


def _rope_tables(s):
    pos = jnp.arange(s, dtype=F32)
    inv_freq = ROPE_THETA ** (-jnp.arange(0, 64, 2, dtype=F32) / 64)
    ang = pos[:, None] * inv_freq[None, :]
    cos, sin = jnp.cos(ang), jnp.sin(ang)
    z32 = jnp.zeros((s, 32), F32)
    z64 = jnp.zeros((s, 64), F32)
    c = jnp.concatenate([cos, cos, z64], axis=1)
    s1 = jnp.concatenate([z32, sin, z64], axis=1)
    s2 = jnp.concatenate([-sin, z32, z64], axis=1)
    return c, s1, s2


def _in_proj(x, g, w, shards):
    s = x.shape[0]
    tm, tn = min(512, s), 1024
    nm = s // tm
    na = len(shards)

    def body(*refs):
        x_ref, g_ref, w_ref = refs[:3]
        o_ref, h_ref = refs[3 + na:5 + na]
        i = pl.program_id(0)
        if na:
            ex = _Gather(refs[3:3 + na], refs[5 + na:5 + 2 * na], refs[5 + 2 * na:])

            @pl.when(i == 0)
            def _():
                ex.start()

            @pl.when(i == (7 * nm) // 8)
            def _():
                ex.forward()

        xh, _ = _rownorm(x_ref[...])
        h_ref[...] = (xh * g_ref[...]).astype(h_ref.dtype)
        for j in range(NP // tn):
            cols = slice(j * tn, (j + 1) * tn)
            o_ref[:, cols] = jnp.dot(h_ref[...], w_ref[:, cols],
                                     preferred_element_type=F32).astype(o_ref.dtype)

        if na:
            @pl.when(i == nm - 1)
            def _():
                ex.finish()

    row = lambda i: (i, 0)
    fixed = lambda i: (0, 0)
    outs = pl.pallas_call(
        body, name="in_proj_gather" if na else "in_proj", grid=(nm,),
        in_specs=[pl.BlockSpec((tm, D_MODEL), row), pl.BlockSpec((1, D_MODEL), fixed),
                  pl.BlockSpec((D_MODEL, NP), fixed, pipeline_mode=pl.Buffered(1))]
        + [HBM_SPEC] * na,
        out_specs=[pl.BlockSpec((tm, NP), row), pl.BlockSpec((tm, D_MODEL), row)]
        + [HBM_SPEC] * na,
        out_shape=[_sds((s, NP), PROJ_DTYPE), _sds((s, D_MODEL), MXU_DTYPE)]
        + _exchange_shapes(shards, True),
        scratch_shapes=_Exchange.semaphores(na) if na else [],
        compiler_params=_params(1),
    )(x, g, w, *shards)
    return outs[0], outs[1], list(outs[2:])


def _swa_slopes():
    return [2.0 ** (-8.0 * (h + 1) / SWA_HEADS) for h in range(SWA_HEADS)]


SWA_STACK = SWA_HEADS * BLOCK


def _swa_head(b):
    j, a = b // 8, b % 8
    return 2 * (4 * j + a % 4) + a // 4


def _swa_kv(kv_p, kv_c):
    return (jnp.concatenate([kv_p[:, :128], kv_c[:, :128]], axis=0),
            jnp.concatenate([kv_p[:, 128:], kv_c[:, 128:]], axis=0))


def _swa_by_block(vals):
    b = lax.broadcasted_iota(jnp.int32, (1, SWA_STACK), 1) >> 7
    row = jnp.full((1, SWA_STACK), vals[-1], F32)
    for t in range(len(vals) - 2, -1, -1):
        row = jnp.where(b == t, vals[t], row)
    return row


def _swa_bias(n):
    slopes = _swa_slopes()
    ki = lax.broadcasted_iota(jnp.int32, (2 * BLOCK, SWA_STACK), 0)
    r = lax.broadcasted_iota(jnp.int32, (2 * BLOCK, SWA_STACK), 1)
    delta = BLOCK + (r & (BLOCK - 1)) - ki
    valid = (delta >= 0) & (delta < BLOCK) & ((n - 1) * BLOCK + ki >= 0)
    slope = _swa_by_block([slopes[_swa_head(b)] for b in range(SWA_HEADS)])
    return jnp.where(valid, -slope * delta.astype(F32), NEG)


def _swa_fill_bias(n, bias_ref):
    @pl.when(n <= 1)
    def _():
        bias_ref[...] = _swa_bias(n)


def _swa_sink_row(sink_ref):
    return _swa_by_block([sink_ref[_swa_head(b)] for b in range(SWA_HEADS)])


def _swa_pairs(ref, j):
    return jnp.concatenate([ref[:, 128 * (4 * j + a):128 * (4 * j + a + 1)] for a in range(4)],
                           axis=0).astype(F32)


def _swa_stack(ref):
    left = lax.broadcasted_iota(jnp.int32, (8 * BLOCK, 128), 1) < 64
    p0, p1 = _swa_pairs(ref, 0), _swa_pairs(ref, 1)
    s0 = jnp.concatenate([p0, pltpu.roll(p0, 64, 1)], axis=0)
    s1 = jnp.concatenate([pltpu.roll(p1, 64, 1), p1], axis=0)
    return jnp.concatenate([jnp.where(left, s0, 0.0), jnp.where(left, 0.0, s1)], axis=0)


def _swa_unstack(t):
    left = lax.broadcasted_iota(jnp.int32, (4 * BLOCK, 128), 1) < 64
    n = 4 * BLOCK
    g0 = jnp.where(left, t[0:n], pltpu.roll(t[n:2 * n], 64, 1))
    g1 = jnp.where(left, pltpu.roll(t[2 * n:3 * n], 64, 1), t[3 * n:4 * n])
    return g0, g1


def _swa_softmax(qs, kk, bias, sink):
    sc = _mm_nt(kk, qs) + bias
    m = jnp.maximum(jnp.max(sc, axis=0, keepdims=True), sink)
    ex = jnp.exp(sc - m)
    es = jnp.exp(sink - m)
    return ex, es, 1.0 / (jnp.sum(ex, axis=0, keepdims=True) + es)


def _swa_fwd(proj, sinks):
    s = proj.shape[0]
    nb = s // BLOCK

    def body(sink_ref, q_ref, kp_ref, kc_ref, o_ref, bias_ref):
        n = pl.program_id(0)
        _swa_fill_bias(n, bias_ref)
        kk, vv = _swa_kv(kp_ref[...].astype(F32), kc_ref[...].astype(F32))
        qs = _swa_stack(q_ref) * 0.125
        ex, _, inv = _swa_softmax(qs, kk, bias_ref[...], _swa_sink_row(sink_ref))
        o_t = _mm(vv.T, ex) * inv
        n4 = 4 * BLOCK
        for j in range(2):
            rows = slice(64 * j, 64 * (j + 1))
            out = jnp.concatenate([o_t[rows, 2 * j * n4:(2 * j + 1) * n4],
                                   o_t[rows, (2 * j + 1) * n4:(2 * j + 2) * n4]], axis=0).T
            for a in range(4):
                o_ref[:, 128 * (4 * j + a):128 * (4 * j + a + 1)] = out[128 * a:128 * (a + 1)].astype(
                    o_ref.dtype)

    return pl.pallas_call(
        body, name="swa_fwd", grid=(nb,),
        in_specs=[pl.BlockSpec(memory_space=pltpu.SMEM),
                  pl.BlockSpec((BLOCK, 1024), lambda n: (n, 0)),
                  pl.BlockSpec((BLOCK, 256), lambda n: (jnp.maximum(n - 1, 0), KVA // 256)),
                  pl.BlockSpec((BLOCK, 256), lambda n: (n, KVA // 256))],
        out_specs=pl.BlockSpec((BLOCK, 1024), lambda n: (n, 0)),
        out_shape=_sds((s, 1024), ATTN_DTYPE),
        scratch_shapes=[pltpu.VMEM((2 * BLOCK, SWA_STACK), F32)],
        compiler_params=_params(1),
    )(sinks, proj, proj, proj)


def _mla_prep(proj, gq, gkv, w_q, w_kv, tabs):
    s = proj.shape[0]
    tm = min(512, s)
    c, s1, s2 = tabs

    def body(p_ref, gq_ref, gkv_ref, wq_ref, wkv_ref, c_ref, s1_ref, s2_ref,
             q_ref, k_ref, v_ref, vt_ref):
        cqh, _ = _rownorm(p_ref[:, 0:384].astype(F32))
        ckvh, _ = _rownorm(p_ref[:, 384:640].astype(F32))
        q = _mm(cqh * gq_ref[...], wq_ref[...])
        kv = _mm(ckvh * gkv_ref[...], wkv_ref[...])
        cc, ss1, ss2 = c_ref[...], s1_ref[...], s2_ref[...]
        krr = _rope(p_ref[:, 640:768].astype(F32), cc, ss1, ss2).astype(k_ref.dtype)
        for h in range(MLA_HEADS):
            q_ref[h, :, 0:128] = q[:, 128 * h:128 * (h + 1)].astype(q_ref.dtype)
            q_ref[h, :, 128:256] = _rope(q[:, 1024 + 128 * h:1024 + 128 * (h + 1)],
                                         cc, ss1, ss2).astype(q_ref.dtype)
            k_ref[h, :, 0:128] = kv[:, 128 * h:128 * (h + 1)].astype(k_ref.dtype)
            k_ref[h, :, 128:256] = krr
            vv = kv[:, 1024 + 128 * h:1024 + 128 * (h + 1)]
            v_ref[h] = vv.astype(v_ref.dtype)
            vt_ref[h, 0:128, :] = vv.T.astype(vt_ref.dtype)
            vt_ref[h, 128:256, :] = jnp.ones((128, tm), vt_ref.dtype)

    row = lambda i: (i, 0)
    fixed = lambda i: (0, 0)
    return pl.pallas_call(
        body, name="mla_prep", grid=(s // tm,),
        in_specs=[pl.BlockSpec((tm, 768), lambda i: (i, CQ // 768)),
                  pl.BlockSpec((1, Q_RANK), fixed), pl.BlockSpec((1, KV_RANK), fixed),
                  pl.BlockSpec((Q_RANK, 2048), fixed), pl.BlockSpec((KV_RANK, 2048), fixed),
                  pl.BlockSpec((tm, 128), row), pl.BlockSpec((tm, 128), row),
                  pl.BlockSpec((tm, 128), row)],
        out_specs=[pl.BlockSpec((MLA_HEADS, tm, 256), lambda i: (0, i, 0)),
                   pl.BlockSpec((MLA_HEADS, tm, 256), lambda i: (0, i, 0)),
                   pl.BlockSpec((MLA_HEADS, tm, 128), lambda i: (0, i, 0)),
                   pl.BlockSpec((MLA_HEADS, 256, tm), lambda i: (0, 0, i))],
        out_shape=[_sds((MLA_HEADS, s, 256), MXU_DTYPE), _sds((MLA_HEADS, s, 256), MXU_DTYPE),
                   _sds((MLA_HEADS, s, 128), MXU_DTYPE), _sds((MLA_HEADS, 256, s), MXU_DTYPE)],
        compiler_params=_params(1),
    )(proj, gq, gkv, w_q, w_kv, c, s1, s2)


def _scores_t(k, q, t, diagonal):
    sc = _mm_nt(k, q)
    if diagonal:
        key = lax.broadcasted_iota(jnp.int32, sc.shape, 0)
        query = lax.broadcasted_iota(jnp.int32, sc.shape, 1)
        sc = jnp.where(key <= query, sc, NEG)
    return sc


def _mla_fwd(qh, kh, vth, shards):
    s = qh.shape[1]
    t = min(MLA_TILE, s)
    nq = s // t
    na = len(shards)

    def body(*refs):
        q_ref, k_ref, vt_ref = refs[:3]
        o_ref, lse_ref = refs[3 + na:5 + na]
        m_ref, acc_ref = refs[5 + 2 * na:7 + 2 * na]
        h, i = pl.program_id(0), pl.program_id(1)
        if na:
            ex = _Gather(refs[3:3 + na], refs[5 + na:5 + 2 * na], refs[7 + 2 * na:])

            @pl.when((h == 0) & (i == 0))
            def _():
                ex.start()

            @pl.when((h == GATHER_FORWARD_HEAD) & (i == 0))
            def _():
                ex.forward()

        m_ref[...] = jnp.full(m_ref.shape, NEG, F32)
        acc_ref[...] = jnp.zeros(acc_ref.shape, F32)

        def step(start, width):
            keys = pl.ds(pl.multiple_of(start, t), width)
            sc = _mm_nt(k_ref[keys, :], q_ref[...])
            m_prev = m_ref[...]
            m_new = jnp.maximum(m_prev, jnp.max(sc, axis=0, keepdims=True))
            alpha = jnp.exp2((m_prev - m_new) * MLA_C2)
            p = jnp.exp2((sc - m_new[0:1, :]) * MLA_C2)
            acc_ref[...] = alpha[0:1, :] * acc_ref[...] + _mm(vt_ref[:, keys], p)
            m_ref[...] = m_new

        def diagonal_step():
            half = t // 2
            k_lo = pl.ds(pl.multiple_of(i * t, half), half)
            k_hi = pl.ds(pl.multiple_of(i * t + half, half), half)
            sc_lo = _scores_t(k_ref[k_lo, :], q_ref[...], t, True)
            sc_hi = _scores_t(k_ref[k_hi, :], q_ref[half:t, :], t, True)
            m_prev = m_ref[...]
            m_lo = jnp.maximum(m_prev, jnp.max(sc_lo, axis=0, keepdims=True))
            m_new = jnp.concatenate(
                [m_lo[:, 0:half],
                 jnp.maximum(m_lo[:, half:t], jnp.max(sc_hi, axis=0, keepdims=True))], axis=1)
            alpha = jnp.exp2((m_prev - m_new) * MLA_C2)
            p_lo = jnp.exp2((sc_lo - m_new[0:1, :]) * MLA_C2)
            p_hi = jnp.exp2((sc_hi - m_new[0:1, half:t]) * MLA_C2)
            acc_ref[...] = alpha[0:1, :] * acc_ref[...] + _mm(vt_ref[:, k_lo], p_lo)
            acc_ref[:, half:t] += _mm(vt_ref[:, k_hi], p_hi)
            m_ref[...] = m_new

        def below_diagonal(jj, carry):
            step(jj * (4 * t), 4 * t)
            return carry

        lax.fori_loop(0, i // 4, below_diagonal, 0)

        @pl.when(i % 4 >= 2)
        def _():
            step((i // 4) * (4 * t), 2 * t)

        @pl.when(i % 2 == 1)
        def _():
            step((i - 1) * t, t)

        diagonal_step()
        l = acc_ref[128:136, :]
        o_ref[...] = (acc_ref[0:128, :] / l[0:1, :]).T.astype(o_ref.dtype)
        lse_ref[...] = m_ref[...] * MLA_C2 + jnp.log2(l)

        if na:
            @pl.when((h == MLA_HEADS - 1) & (i == nq - 1))
            def _():
                ex.finish()

    head = lambda h, i: (h, 0, 0)
    outs = pl.pallas_call(
        body, name="mla_fwd_gather" if na else "mla_fwd", grid=(MLA_HEADS, nq),
        in_specs=[pl.BlockSpec((None, t, 256), lambda h, i: (h, i, 0)),
                  pl.BlockSpec((None, s, 256), head),
                  pl.BlockSpec((None, 256, s), head)] + [HBM_SPEC] * na,
        out_specs=[pl.BlockSpec((t, 128), lambda h, i: (i, h)),
                   pl.BlockSpec((None, 8, t), lambda h, i: (h, 0, i))] + [HBM_SPEC] * na,
        out_shape=[_sds((s, 1024), ATTN_DTYPE), _sds((MLA_HEADS, 8, s), F32)]
        + _exchange_shapes(shards, True),
        scratch_shapes=[pltpu.VMEM((8, t), F32), pltpu.VMEM((256, t), F32)]
        + (_Exchange.semaphores(na) if na else []),
        compiler_params=_params(2),
    )(qh, kh, vth, *shards)
    return outs[0], outs[1], list(outs[2:])


def _silu_parts(g):
    sg = jax.nn.sigmoid(g)
    return g * sg, sg * (1.0 + g * (1.0 - sg))


def _out_proj(x, proj, swa, mla, w_out):
    s = x.shape[0]
    tm = min(512, s)

    def body(x_ref, ga_ref, gb_ref, a_ref, b_ref, w_ref, xo_ref, y_ref):
        sa, _ = _silu_parts(ga_ref[...].astype(F32))
        sb, _ = _silu_parts(gb_ref[...].astype(F32))
        y_ref[:, 0:1024] = (a_ref[...].astype(F32) * sa).astype(y_ref.dtype)
        y_ref[:, 1024:2048] = (b_ref[...].astype(F32) * sb).astype(y_ref.dtype)
        xo_ref[...] = x_ref[...] + jnp.dot(y_ref[...], w_ref[...], preferred_element_type=F32)

    row = lambda i: (i, 0)
    return pl.pallas_call(
        body, name="out_proj", grid=(s // tm,),
        in_specs=[pl.BlockSpec((tm, D_MODEL), row),
                  pl.BlockSpec((tm, 1024), lambda i: (i, GA // 1024)),
                  pl.BlockSpec((tm, 1024), lambda i: (i, GB // 1024)),
                  pl.BlockSpec((tm, 1024), row), pl.BlockSpec((tm, 1024), row),
                  pl.BlockSpec((D_MODEL, D_MODEL), lambda i: (0, 0), pipeline_mode=pl.Buffered(1))],
        out_specs=[pl.BlockSpec((tm, D_MODEL), row), pl.BlockSpec((tm, D_MODEL), row)],
        out_shape=[_sds((s, D_MODEL), F32), _sds((s, D_MODEL), MXU_DTYPE)],
        compiler_params=_params(1),
    )(x, proj, proj, swa, mla, w_out)


def _final_loss(x, tgt, g):
    s = x.shape[0]
    tm = min(512, s)

    def body(x_ref, t_ref, g_ref, dx_ref, dxb_ref, dg_ref, loss_ref):
        @pl.when(pl.program_id(0) == 0)
        def _():
            dg_ref[...] = jnp.zeros(dg_ref.shape, F32)
            loss_ref[...] = jnp.zeros(loss_ref.shape, F32)
        xh, r = _rownorm(x_ref[...])
        gg = g_ref[...]
        err = xh * gg - t_ref[...]
        per_row = jnp.mean(err * err, axis=-1, keepdims=True)
        loss_ref[...] += 0.5 * jnp.sum(per_row, axis=0, keepdims=True)
        dy = err * (1.0 / D_MODEL)
        dg_ref[...] += jnp.sum(dy * xh, axis=0, keepdims=True)
        dx = _rownorm_bwd(dy * gg, xh, r)
        dx_ref[...] = dx
        dxb_ref[...] = dx.astype(dxb_ref.dtype)

    row = lambda i: (i, 0)
    fixed = lambda i: (0, 0)
    return pl.pallas_call(
        body, name="final_loss", grid=(s // tm,),
        in_specs=[pl.BlockSpec((tm, D_MODEL), row), pl.BlockSpec((tm, D_MODEL), row),
                  pl.BlockSpec((1, D_MODEL), fixed)],
        out_specs=[pl.BlockSpec((tm, D_MODEL), row), pl.BlockSpec((tm, D_MODEL), row),
                   pl.BlockSpec((1, D_MODEL), fixed), pl.BlockSpec((8, 128), fixed)],
        out_shape=[_sds((s, D_MODEL), F32), _sds((s, D_MODEL), MXU_DTYPE), _sds((1, D_MODEL), F32),
                   _sds((8, 128), F32)],
        compiler_params=_params(1),
    )(x, tgt, g)


def _out_proj_bwd(dx, proj, swa, mla, w_out):
    s = dx.shape[0]
    tm = min(512, s)

    def body(dx_ref, ga_ref, gb_ref, a_ref, b_ref, w_ref, doa_ref, dob_ref, dg_ref, dlt_ref):
        dx = dx_ref[...].astype(MXU_DTYPE)
        dya = _mm_nt(dx, w_ref[0:1024, :])
        sa, dsa = _silu_parts(ga_ref[...].astype(F32))
        doa_ref[...] = (dya * sa).astype(doa_ref.dtype)
        dg_ref[:, 0:1024] = (dya * a_ref[...].astype(F32) * dsa).astype(dg_ref.dtype)
        dyb = _mm_nt(dx, w_ref[1024:2048, :])
        sb, dsb = _silu_parts(gb_ref[...].astype(F32))
        b = b_ref[...].astype(F32)
        dob = dyb * sb
        dob_ref[...] = dob.astype(dob_ref.dtype)
        dg_ref[:, 1024:2048] = (dyb * b * dsb).astype(dg_ref.dtype)
        prod = dob * b
        for h in range(MLA_HEADS):
            dlt = jnp.sum(prod[:, 128 * h:128 * (h + 1)], axis=1, keepdims=True)
            dlt_ref[h] = jnp.broadcast_to(dlt, (tm, 128)).T[0:8, :]

    row = lambda i: (i, 0)
    return pl.pallas_call(
        body, name="out_proj_bwd", grid=(s // tm,),
        in_specs=[pl.BlockSpec((tm, D_MODEL), row),
                  pl.BlockSpec((tm, 1024), lambda i: (i, GA // 1024)),
                  pl.BlockSpec((tm, 1024), lambda i: (i, GB // 1024)),
                  pl.BlockSpec((tm, 1024), row), pl.BlockSpec((tm, 1024), row),
                  pl.BlockSpec((D_MODEL, D_MODEL), lambda i: (0, 0), pipeline_mode=pl.Buffered(1))],
        out_specs=[pl.BlockSpec((tm, 1024), row), pl.BlockSpec((tm, 1024), row),
                   pl.BlockSpec((tm, D_MODEL), row),
                   pl.BlockSpec((MLA_HEADS, 8, tm), lambda i: (0, 0, i))],
        out_shape=[_sds((s, 1024), ATTN_DTYPE), _sds((s, 1024), MXU_DTYPE), _sds((s, D_MODEL), MXU_DTYPE),
                   _sds((MLA_HEADS, 8, s), F32)],
        compiler_params=_params(1),
    )(dx, proj, proj, swa, mla, w_out)


def _matmul_tn(a, b, name):
    s, m = a.shape
    n = b.shape[1]
    tm, tn, tk = min(1024, m), min(1024, n), min(2048, s)
    nk = s // tk

    def body(a_ref, b_ref, o_ref, acc_ref):
        k = pl.program_id(2)

        @pl.when(k == 0)
        def _():
            acc_ref[...] = jnp.zeros(acc_ref.shape, F32)
        acc_ref[...] += _mm_tn(a_ref[...], b_ref[...])

        @pl.when(k == nk - 1)
        def _():
            o_ref[...] = acc_ref[...].astype(o_ref.dtype)

    return pl.pallas_call(
        body, name=name, grid=(m // tm, n // tn, nk),
        in_specs=[pl.BlockSpec((tk, tm), lambda i, j, k: (k, i)),
                  pl.BlockSpec((tk, tn), lambda i, j, k: (k, j))],
        out_specs=pl.BlockSpec((tm, tn), lambda i, j, k: (i, j)),
        out_shape=_sds((m, n), GRAD_DTYPE),
        scratch_shapes=[pltpu.VMEM((tm, tn), F32)],
        compiler_params=_params(3),
    )(a, b)


def _grad_w_in(h, dqa, dgate, dlat, dkva, blocks):
    s = h.shape[0]
    tm, tn, tk = 1024, 1024, min(2048, s)
    nk = s // tk
    grid = (D_MODEL // tm, NP // tn, nk)
    na = len(blocks)

    def body(*refs):
        a_ref, dqa_ref, dg8_ref, dlat_ref, dkva_ref = refs[:5]
        o_ref = refs[5 + na]
        acc_ref = refs[6 + 2 * na]
        i, j, k = pl.program_id(0), pl.program_id(1), pl.program_id(2)
        if na:
            ex = _Exchange(refs[5:5 + na], refs[6 + na:6 + 2 * na], refs[7 + 2 * na:], gather=False)

            @pl.when((i == 0) & (j == 0) & (k == 0))
            def _():
                ex.start()

        @pl.when(k == 0)
        def _():
            acc_ref[...] = jnp.zeros(acc_ref.shape, F32)

        @pl.when(j == QA // tn)
        def _():
            acc_ref[...] += _mm_tn(a_ref[...], dqa_ref[...])

        @pl.when((j == GA // tn) | (j == GB // tn))
        def _():
            acc_ref[...] += _mm_tn(a_ref[...], dg8_ref[...])

        @pl.when(j == CQ // tn)
        def _():
            acc_ref[:, 0:768] += _mm_tn(a_ref[...], dlat_ref[...])
            acc_ref[:, 768:1024] += _mm_tn(a_ref[...], dkva_ref[...])

        @pl.when(k == nk - 1)
        def _():
            o_ref[...] = acc_ref[...].astype(o_ref.dtype)

        if na:
            @pl.when((i == grid[0] - 1) & (j == grid[1] - 1) & (k == nk - 1))
            def _():
                ex.wait()

    def when(group, width):
        return pl.BlockSpec((tk, width), lambda i, j, k: (jnp.where(j == group, k, 0), 0))

    outs = pl.pallas_call(
        body, name="grad_w_in_scatter" if na else "grad_w_in", grid=grid,
        in_specs=[pl.BlockSpec((tk, tm), lambda i, j, k: (k, i)),
                  when(QA // tn, 1024),
                  pl.BlockSpec((tk, 1024), lambda i, j, k: (
                      jnp.where((j == GA // tn) | (j == GB // tn), k, 0),
                      jnp.clip(j - GA // tn, 0, 1))),
                  when(CQ // tn, 768), when(CQ // tn, 256)] + [HBM_SPEC] * na,
        out_specs=[pl.BlockSpec((tm, tn), lambda i, j, k: (i, j))] + [HBM_SPEC] * na,
        out_shape=[_sds((D_MODEL, NP), GRAD_DTYPE)] + _exchange_shapes(blocks, False),
        scratch_shapes=[pltpu.VMEM((tm, tn), F32)] + (_Exchange.semaphores(na) if na else []),
        compiler_params=_params(3),
    )(h, dqa, dgate, dlat, dkva, *blocks)
    return outs[0], list(outs[1:])


def _swa_bwd(proj, sinks, do, o, blocks):
    s = proj.shape[0]
    nb = s // BLOCK
    na = len(blocks)

    def body(*refs):
        sink_ref, q_ref, kp_ref, kc_ref, do_ref, o_ref = refs[:6]
        dq_ref, dkv_ref, dsink_ref = refs[6 + na:9 + na]
        carry_ref, bias_ref = refs[9 + 2 * na:11 + 2 * na]
        n = pl.program_id(0)
        _swa_fill_bias(n, bias_ref)
        if na:
            ex = _Exchange(refs[6:6 + na], refs[9 + na:9 + 2 * na], refs[11 + 2 * na:], gather=False)

        @pl.when(n == 0)
        def _():
            carry_ref[...] = jnp.zeros(carry_ref.shape, F32)
            dsink_ref[...] = jnp.zeros(dsink_ref.shape, F32)
            if na:
                ex.start()

        @pl.when(n < nb)
        def _():
            kk, vv = _swa_kv(kp_ref[...].astype(F32), kc_ref[...].astype(F32))
            lane_s = lax.broadcasted_iota(jnp.int32, (8, 128), 1)
            qs = _swa_stack(q_ref) * 0.125
            dos = _swa_stack(do_ref)
            dlt = []
            for j in range(2):
                prod_t = (_swa_pairs(do_ref, j) * _swa_pairs(o_ref, j)).T
                dlt += [jnp.sum(prod_t[0:64], axis=0, keepdims=True),
                        jnp.sum(prod_t[64:128], axis=0, keepdims=True)]
            dlt = jnp.concatenate(dlt, axis=1)
            ex, es, inv = _swa_softmax(qs, kk, bias_ref[...], _swa_sink_row(sink_ref))
            p = ex * inv
            ds = p * (_mm_nt(vv, dos) - dlt)
            sink_term = es * inv * dlt
            dsink = jnp.zeros((8, 128), F32)
            for b in range(SWA_HEADS):
                dsh = -jnp.sum(sink_term[:, 128 * b:128 * (b + 1)], axis=1, keepdims=True)
                dsink = dsink + jnp.where(lane_s == _swa_head(b), dsh, 0.0)
            for j, dq in enumerate(_swa_unstack(_mm_tn(ds, kk))):
                for a in range(4):
                    cols = slice(128 * (4 * j + a), 128 * (4 * j + a + 1))
                    dq_ref[:, cols] = (dq[128 * a:128 * (a + 1)] * 0.125).astype(dq_ref.dtype)
            contrib = jnp.concatenate([_mm(ds, qs), _mm(p, dos)], axis=1)
            dkv_ref[...] = (carry_ref[...] + contrib[0:BLOCK]).astype(dkv_ref.dtype)
            carry_ref[...] = contrib[BLOCK:2 * BLOCK]
            dsink_ref[...] += dsink

        @pl.when(n == nb)
        def _():
            dkv_ref[...] = carry_ref[...].astype(dkv_ref.dtype)
            if na:
                ex.wait()

    cur = lambda n: (jnp.minimum(n, nb - 1), 0)
    outs = pl.pallas_call(
        body, name="swa_bwd_scatter" if na else "swa_bwd", grid=(nb + 1,),
        in_specs=[pl.BlockSpec(memory_space=pltpu.SMEM),
                  pl.BlockSpec((BLOCK, 1024), cur),
                  pl.BlockSpec((BLOCK, 256), lambda n: (jnp.clip(n - 1, 0, nb - 1), KVA // 256)),
                  pl.BlockSpec((BLOCK, 256), lambda n: (jnp.minimum(n, nb - 1), KVA // 256)),
                  pl.BlockSpec((BLOCK, 1024), cur), pl.BlockSpec((BLOCK, 1024), cur)]
        + [HBM_SPEC] * na,
        out_specs=[pl.BlockSpec((BLOCK, 1024), cur),
                   pl.BlockSpec((BLOCK, 256), lambda n: (jnp.maximum(n - 1, 0), 0)),
                   pl.BlockSpec((8, 128), lambda n: (0, 0))] + [HBM_SPEC] * na,
        out_shape=[_sds((s, 1024), MXU_DTYPE), _sds((s, 256), MXU_DTYPE), _sds((8, 128), F32)]
        + _exchange_shapes(blocks, False),
        scratch_shapes=[pltpu.VMEM((BLOCK, 256), F32), pltpu.VMEM((2 * BLOCK, SWA_STACK), F32)]
        + (_Exchange.semaphores(na) if na else []),
        compiler_params=_params(1),
    )(sinks, proj, proj, proj, do, o, *blocks)
    return outs[0], outs[1], outs[2], list(outs[3:])


def _mla_bwd(qh, kh, vh, do, dlt, lse, blocks):
    s = qh.shape[1]
    t = min(MLA_TILE, s)
    nq = s // t

    na = len(blocks)

    def body(*refs):
        q_ref, k_ref, v_ref, do_ref, dlt_ref, lse_ref = refs[:6]
        dq_ref, dk_ref, dv_ref = refs[6 + na:9 + na]
        h, j = pl.program_id(0), pl.program_id(1)
        if na:
            ex = _Exchange(refs[6:6 + na], refs[9 + na:9 + 2 * na], refs[9 + 2 * na:], gather=False)

            @pl.when((h == 0) & (j == 0))
            def _():
                ex.start()

        def step(start, width, diagonal=False):
            rows = pl.ds(pl.multiple_of(start, t), width)
            q, k, dout = q_ref[rows, :], k_ref[...], do_ref[rows, :]
            sc = _scores_t(k, q, t, diagonal)
            p = jnp.exp2(sc * MLA_C2 - lse_ref[0:1, rows])
            dv = _mm(p, dout)
            ds = p * (_mm_nt(v_ref[...], dout) - dlt_ref[0:1, rows])
            dk = _mm(ds, q)
            dq = _mm_tn(ds, k)
            return rows, dq, dk, dv

        rows, dq, dk, dv = step(j * t, t, diagonal=True)
        dk_ref[...] = dk
        dv_ref[...] = dv

        @pl.when(j == 0)
        def _():
            dq_ref[rows, :] = dq * MLA_SCALE

        @pl.when(j > 0)
        def _():
            dq_ref[rows, :] = (dq_ref[rows, :] + dq) * MLA_SCALE

        def above_diagonal(start, width):
            rows, dq, dk, dv = step(start, width)
            dk_ref[...] += dk
            dv_ref[...] += dv

            @pl.when(j == 0)
            def _():
                dq_ref[rows, :] = dq

            @pl.when(j > 0)
            def _():
                dq_ref[rows, :] += dq

        n_above = nq - 1 - j

        def quad(jj, carry):
            above_diagonal((j + 1 + 4 * jj) * t, 4 * t)
            return carry

        lax.fori_loop(0, n_above // 4, quad, 0)

        @pl.when(n_above % 4 >= 2)
        def _():
            above_diagonal((j + 1 + 4 * (n_above // 4)) * t, 2 * t)

        @pl.when(n_above % 2 == 1)
        def _():
            above_diagonal((nq - 1) * t, t)

        dk_ref[...] *= MLA_SCALE

        if na:
            @pl.when((h == MLA_HEADS - 1) & (j == nq - 1))
            def _():
                ex.wait()

    head = lambda h, j: (h, 0, 0)
    kv_map = lambda h, j: (h, j, 0)
    outs = pl.pallas_call(
        body, name="mla_bwd_scatter" if na else "mla_bwd", grid=(MLA_HEADS, nq),
        in_specs=[pl.BlockSpec((None, s, 256), head), pl.BlockSpec((None, t, 256), kv_map),
                  pl.BlockSpec((None, t, 128), kv_map),
                  pl.BlockSpec((s, 128), lambda h, j: (0, h)),
                  pl.BlockSpec((None, 8, s), head), pl.BlockSpec((None, 8, s), head)]
        + [HBM_SPEC] * na,
        out_specs=[pl.BlockSpec((None, s, 256), head),
                   pl.BlockSpec((None, t, 256), kv_map), pl.BlockSpec((None, t, 128), kv_map)]
        + [HBM_SPEC] * na,
        out_shape=[_sds((MLA_HEADS, s, 256), F32), _sds((MLA_HEADS, s, 256), F32),
                   _sds((MLA_HEADS, s, 128), F32)] + _exchange_shapes(blocks, False),
        scratch_shapes=_Exchange.semaphores(na) if na else [],
        compiler_params=_params(2),
    )(qh, kh, vh, do, dlt, lse, *blocks)
    return outs[0], outs[1], outs[2], list(outs[3:])


def _mla_prep_bwd(dqh, dkh, dvh, proj, gq, gkv, w_q, w_kv, tabs):
    s = proj.shape[0]
    tm = min(512, s)
    nm = s // tm
    c, s1, s2 = tabs

    def body(dq_ref, dk_ref, dv_ref, p_ref, gq_ref, gkv_ref, wq_ref, wkv_ref,
             c_ref, s1_ref, s2_ref, dp_ref, dwq_ref, dwkv_ref, dgq_ref, dgkv_ref,
             dqf_ref, dkvf_ref, dwq_acc, dwkv_acc):
        @pl.when(pl.program_id(0) == 0)
        def _():
            dgq_ref[...] = jnp.zeros(dgq_ref.shape, F32)
            dgkv_ref[...] = jnp.zeros(dgkv_ref.shape, F32)
            dwq_acc[...] = jnp.zeros(dwq_acc.shape, F32)
            dwkv_acc[...] = jnp.zeros(dwkv_acc.shape, F32)
        cc, ns1, ns2 = c_ref[...], -s1_ref[...], -s2_ref[...]
        dkr = jnp.zeros((tm, 128), F32)
        for h in range(MLA_HEADS):
            dqf_ref[:, 128 * h:128 * (h + 1)] = dq_ref[h, :, 0:128].astype(dqf_ref.dtype)
            dqf_ref[:, 1024 + 128 * h:1024 + 128 * (h + 1)] = _rope(
                dq_ref[h, :, 128:256], cc, ns1, ns2).astype(dqf_ref.dtype)
            dkvf_ref[:, 128 * h:128 * (h + 1)] = dk_ref[h, :, 0:128].astype(dkvf_ref.dtype)
            dkvf_ref[:, 1024 + 128 * h:1024 + 128 * (h + 1)] = dv_ref[h].astype(dkvf_ref.dtype)
            dkr = dkr + dk_ref[h, :, 128:256]
        dcqn = _mm_nt(dqf_ref[...], wq_ref[...])
        dckvn = _mm_nt(dkvf_ref[...], wkv_ref[...])
        cqh, rq = _rownorm(p_ref[:, 0:384].astype(F32))
        ckvh, rkv = _rownorm(p_ref[:, 384:640].astype(F32))
        dgq_ref[...] += jnp.sum(dcqn * cqh, axis=0, keepdims=True)
        dgkv_ref[...] += jnp.sum(dckvn * ckvh, axis=0, keepdims=True)
        dp_ref[:, 0:384] = _rownorm_bwd(dcqn * gq_ref[...], cqh, rq).astype(dp_ref.dtype)
        dp_ref[:, 384:640] = _rownorm_bwd(dckvn * gkv_ref[...], ckvh, rkv).astype(dp_ref.dtype)
        dp_ref[:, 640:768] = _rope(dkr, cc, ns1, ns2).astype(dp_ref.dtype)
        dwq_acc[...] += _mm_tn(cqh * gq_ref[...], dqf_ref[...])
        dwkv_acc[...] += _mm_tn(ckvh * gkv_ref[...], dkvf_ref[...])

        @pl.when(pl.program_id(0) == nm - 1)
        def _():
            dwq_ref[...] = dwq_acc[...].astype(dwq_ref.dtype)
            dwkv_ref[...] = dwkv_acc[...].astype(dwkv_ref.dtype)

    row = lambda i: (i, 0)
    fixed = lambda i: (0, 0)
    head = lambda i: (0, i, 0)
    return pl.pallas_call(
        body, name="mla_prep_bwd", grid=(nm,),
        in_specs=[pl.BlockSpec((MLA_HEADS, tm, 256), head), pl.BlockSpec((MLA_HEADS, tm, 256), head),
                  pl.BlockSpec((MLA_HEADS, tm, 128), head),
                  pl.BlockSpec((tm, 768), lambda i: (i, CQ // 768)),
                  pl.BlockSpec((1, Q_RANK), fixed), pl.BlockSpec((1, KV_RANK), fixed),
                  pl.BlockSpec((Q_RANK, 2048), fixed), pl.BlockSpec((KV_RANK, 2048), fixed),
                  pl.BlockSpec((tm, 128), row), pl.BlockSpec((tm, 128), row),
                  pl.BlockSpec((tm, 128), row)],
        out_specs=[pl.BlockSpec((tm, 768), row), pl.BlockSpec((Q_RANK, 2048), fixed),
                   pl.BlockSpec((KV_RANK, 2048), fixed),
                   pl.BlockSpec((1, Q_RANK), fixed), pl.BlockSpec((1, KV_RANK), fixed)],
        out_shape=[_sds((s, 768), MXU_DTYPE), _sds((Q_RANK, 2048), GRAD_DTYPE),
                   _sds((KV_RANK, 2048), GRAD_DTYPE),
                   _sds((1, Q_RANK), F32), _sds((1, KV_RANK), F32)],
        scratch_shapes=[pltpu.VMEM((tm, 2048), MXU_DTYPE), pltpu.VMEM((tm, 2048), MXU_DTYPE),
                        pltpu.VMEM((Q_RANK, 2048), F32), pltpu.VMEM((KV_RANK, 2048), F32)],
        compiler_params=_params(1),
    )(dqh, dkh, dvh, proj, gq, gkv, w_q, w_kv, c, s1, s2)


def _in_proj_bwd(dqa, dgate, dlat, dkva, w, x, dx_out, g, blocks):
    s = x.shape[0]
    tm = min(256, s)
    nm = s // tm
    na = len(blocks)

    def body(*refs):
        dqa_ref, dg8_ref, dlat_ref, dkva_ref, w_ref, x_ref, dxo_ref, g_ref = refs[:8]
        dx_ref, dxb_ref, dg_ref = refs[8 + na:11 + na]
        if na:
            ex = _Exchange(refs[8:8 + na], refs[11 + na:11 + 2 * na], refs[11 + 2 * na:], gather=False)

        @pl.when(pl.program_id(0) == 0)
        def _():
            dg_ref[...] = jnp.zeros(dg_ref.shape, F32)
            if na:
                ex.start()

        dh = (_mm_nt(dqa_ref[...], w_ref[:, QA:QA + 1024])
              + _mm_nt(dg8_ref[:, 0:1024], w_ref[:, GA:GA + 1024])
              + _mm_nt(dg8_ref[:, 1024:2048], w_ref[:, GB:GB + 1024])
              + _mm_nt(dlat_ref[...], w_ref[:, CQ:CQ + 768])
              + _mm_nt(dkva_ref[...], w_ref[:, KVA:KVA + 256]))
        xh, r = _rownorm(x_ref[...])
        dg_ref[...] += jnp.sum(dh * xh, axis=0, keepdims=True)
        dx = dxo_ref[...] + _rownorm_bwd(dh * g_ref[...], xh, r)
        dx_ref[...] = dx
        dxb_ref[...] = dx.astype(dxb_ref.dtype)

        if na:
            @pl.when(pl.program_id(0) == nm - 1)
            def _():
                ex.wait()

    row = lambda i: (i, 0)
    fixed = lambda i: (0, 0)
    outs = pl.pallas_call(
        body, name="in_proj_bwd_scatter" if na else "in_proj_bwd", grid=(nm,),
        in_specs=[pl.BlockSpec((tm, 1024), row), pl.BlockSpec((tm, 2048), row),
                  pl.BlockSpec((tm, 768), row), pl.BlockSpec((tm, 256), row),
                  pl.BlockSpec((D_MODEL, NP), fixed, pipeline_mode=pl.Buffered(1)),
                  pl.BlockSpec((tm, D_MODEL), row), pl.BlockSpec((tm, D_MODEL), row),
                  pl.BlockSpec((1, D_MODEL), fixed)] + [HBM_SPEC] * na,
        out_specs=[pl.BlockSpec((tm, D_MODEL), row), pl.BlockSpec((tm, D_MODEL), row),
                   pl.BlockSpec((1, D_MODEL), fixed)] + [HBM_SPEC] * na,
        out_shape=[_sds((s, D_MODEL), F32), _sds((s, D_MODEL), MXU_DTYPE), _sds((1, D_MODEL), F32)]
        + _exchange_shapes(blocks, False),
        scratch_shapes=_Exchange.semaphores(na) if na else [],
        compiler_params=_params(1),
    )(dqa, dgate, dlat, dkva, w, x, dx_out, g, *blocks)
    return outs[0], outs[1], outs[2], list(outs[3:])


def _adamw_update(g, w_ref, m_ref, v_ref, g_ref, d_ref, nm_ref, nv_ref):
    m2 = ADAM_B1 * m_ref[...] + (1.0 - ADAM_B1) * g
    v2 = ADAM_B2 * v_ref[...] + (1.0 - ADAM_B2) * (g * g)
    m_hat = m2 / (1.0 - ADAM_B1 ** ADAM_STEP)
    v_hat = v2 / (1.0 - ADAM_B2 ** ADAM_STEP)
    g_ref[...] = g
    d_ref[...] = -ADAM_LR * (m_hat / (jnp.sqrt(v_hat) + ADAM_EPS) + ADAM_WD * w_ref[...])
    nm_ref[...] = m2
    nv_ref[...] = v2


def _reduce_adamw_in(parts, w_t, m_t, v_t, name):
    n_layers = len(parts)
    tc = 512
    nc = D_MODEL // tc

    def body(*refs):
        p_refs = refs[:n_layers]
        w_ref, m_ref, v_ref, g_ref, d_ref, nm_ref, nv_ref = refs[n_layers:]
        layer = pl.program_id(0)
        for l in range(n_layers):
            @pl.when(layer == l)
            def _(l=l):
                g = p_refs[l][0].astype(F32)
                for k in range(1, N_DEV):
                    g = g + p_refs[l][k].astype(F32)
                _adamw_update(g[0:SHARD_COLS], w_ref, m_ref, v_ref, g_ref, d_ref, nm_ref, nv_ref)

    def part_spec(l):
        return pl.BlockSpec((N_DEV, SHARD_PAD, tc),
                            lambda layer, j: (0, 0, jnp.where(layer == l, j, 0)))

    blk = pl.BlockSpec((None, SHARD_COLS, tc), lambda layer, j: (layer, 0, j))
    return pl.pallas_call(
        body, name=name, grid=(n_layers, nc),
        in_specs=[part_spec(l) for l in range(n_layers)] + [blk, blk, blk],
        out_specs=[blk, blk, blk, blk],
        out_shape=[_sds((n_layers, SHARD_COLS, D_MODEL), F32)] * 4,
        compiler_params=_params(2),
    )(*parts, w_t, m_t, v_t)


def _reduce_adamw(parts, w, m, v, name):
    n_layers = len(parts)
    rows, part_cols = parts[0].shape[1:]
    cols = w.shape[-1]
    lanes = -(-cols // 128) * 128
    tr = rows
    for cand in (1024, 512, 256, 128, 64, 32, 16, 8):
        if rows % cand == 0 and N_DEV * cand * lanes * 4 <= 8 * 1024 * 1024:
            tr = cand
            break
    nr = rows // tr

    def body(*refs):
        p_refs = refs[:n_layers]
        w_ref, m_ref, v_ref, g_ref, d_ref, nm_ref, nv_ref = refs[n_layers:]
        layer = pl.program_id(0)
        for l in range(n_layers):
            @pl.when(layer == l)
            def _(l=l):
                g = p_refs[l][0, :, 0:cols].astype(F32)
                for k in range(1, N_DEV):
                    g = g + p_refs[l][k, :, 0:cols].astype(F32)
                _adamw_update(g, w_ref, m_ref, v_ref, g_ref, d_ref, nm_ref, nv_ref)

    def part_spec(l):
        return pl.BlockSpec((N_DEV, tr, part_cols),
                            lambda layer, i: (0, jnp.where(layer == l, i, 0), 0))

    blk = pl.BlockSpec((tr, cols), lambda layer, i: (layer * nr + i, 0))
    return pl.pallas_call(
        body, name=name, grid=(n_layers, nr),
        in_specs=[part_spec(l) for l in range(n_layers)] + [blk, blk, blk],
        out_specs=[blk, blk, blk, blk],
        out_shape=[_sds((n_layers * rows, cols), F32)] * 4,
        compiler_params=_params(2),
    )(*parts, w, m, v)


def _position():
    x, y, c = lax.axis_index("x"), lax.axis_index("y"), lax.axis_index("c")
    return x, y, c


def _index(px, py, pc):
    return 4 * px + 2 * py + pc


HBM_SPEC = pl.BlockSpec(memory_space=pltpu.HBM)
IN_BLOCKED = (D_MODEL, N_DEV * SHARD_PAD)
IN_BLOCKED_T = (N_DEV * SHARD_PAD, D_MODEL)


def _block(ref, idx):
    if tuple(ref.shape) == IN_BLOCKED:
        return ref.at[:, pl.ds(pl.multiple_of(idx * SHARD_PAD, SHARD_PAD), SHARD_PAD)]
    if tuple(ref.shape) == IN_BLOCKED_T:
        return ref.at[pl.ds(pl.multiple_of(idx * SHARD_PAD, SHARD_PAD), SHARD_PAD)]
    return ref.at[idx]


class _Gather:
    def __init__(self, srcs, dsts, sems):
        self.srcs, self.dsts = srcs, dsts
        self.send_sems, self.recv_sems, self.local_sems = sems
        x, y, c = _position()
        self.c = c
        self.me, self.sibling = (x, y, c), (x, y, 1 - c)
        self.chips = [(1 - x, y), (x, 1 - y), (1 - x, 1 - y)]

    def _copy(self, a, k, block, to, own=False):
        slot = _block(self.dsts[a], _index(*block))
        return pltpu.make_async_remote_copy(
            src_ref=self.srcs[a] if own else slot, dst_ref=slot,
            send_sem=self.send_sems.at[7 * a + k], recv_sem=self.recv_sems.at[7 * a + k],
            device_id=to, device_id_type=MESH)

    def _local(self, a):
        return pltpu.make_async_copy(self.srcs[a], _block(self.dsts[a], _index(*self.me)),
                                     self.local_sems.at[a])

    def _first(self, a):
        return [self._copy(a, 0, self.me, self.sibling, own=True)] + [
            self._copy(a, 1 + j, self.me, (*chip, self.c), own=True)
            for j, chip in enumerate(self.chips)]

    def _passed(self, a, j):
        return self._copy(a, 4 + j, (*self.chips[j], self.c), self.sibling)

    def start(self):
        for a in range(len(self.srcs)):
            self._local(a).start()
            for cp in self._first(a):
                cp.start()

    def forward(self):
        for j, chip in enumerate(self.chips):
            for a in range(len(self.srcs)):
                self._copy(a, 1 + j, (*chip, self.c), self.me).wait_recv()
                self._passed(a, j).start()

    def finish(self):
        for a in range(len(self.srcs)):
            self._copy(a, 0, self.sibling, self.me).wait_recv()
            for j, chip in enumerate(self.chips):
                self._copy(a, 4 + j, (*chip, 1 - self.c), self.me).wait_recv()
            for cp in self._first(a):
                cp.wait_send()
            for j in range(3):
                self._passed(a, j).wait_send()
            self._local(a).wait()


def _all_gather(shards, name):
    na = len(shards)

    def body(*refs):
        g = _Gather(refs[:na], refs[na:2 * na], refs[2 * na:])
        g.start()
        g.forward()
        g.finish()

    return pl.pallas_call(
        body, name=name,
        in_specs=[HBM_SPEC] * na, out_specs=[HBM_SPEC] * na,
        out_shape=_exchange_shapes(shards, True),
        scratch_shapes=_Exchange.semaphores(na),
    )(*shards)


class _Exchange:
    def __init__(self, srcs, dsts, sems, gather):
        self.srcs, self.dsts, self.gather = srcs, dsts, gather
        self.send_sems, self.recv_sems, self.local_sems = sems
        x, y, c = _position()
        self.me = _index(x, y, c)
        self.peers = [(x ^ ((k >> 2) & 1), y ^ ((k >> 1) & 1), c ^ (k & 1)) for k in range(1, N_DEV)]

    @staticmethod
    def semaphores(na):
        return [pltpu.SemaphoreType.DMA((7 * na,)), pltpu.SemaphoreType.DMA((7 * na,)),
                pltpu.SemaphoreType.DMA((na,))]

    def _src(self, a, slot):
        return self.srcs[a] if self.gather else _block(self.srcs[a], slot)

    def _local(self, a):
        return pltpu.make_async_copy(self._src(a, self.me), self.dsts[a].at[self.me],
                                     self.local_sems.at[a])

    def _send(self, a, k):
        peer = self.peers[k]
        return pltpu.make_async_remote_copy(
            src_ref=self._src(a, _index(*peer)), dst_ref=self.dsts[a].at[self.me],
            send_sem=self.send_sems.at[7 * a + k], recv_sem=self.recv_sems.at[7 * a + k],
            device_id=peer, device_id_type=MESH)

    def _arrival(self, a, k):
        landed = self.dsts[a].at[_index(*self.peers[k])]
        return pltpu.make_async_remote_copy(
            src_ref=landed, dst_ref=landed,
            send_sem=self.send_sems.at[7 * a + k], recv_sem=self.recv_sems.at[7 * a + k],
            device_id=self.peers[k], device_id_type=MESH)

    def start(self):
        for a in range(len(self.srcs)):
            self._local(a).start()
            for k in range(N_DEV - 1):
                self._send(a, k).start()

    def wait(self):
        for a in range(len(self.srcs)):
            for k in range(N_DEV - 1):
                self._arrival(a, k).wait_recv()
            for k in range(N_DEV - 1):
                self._send(a, k).wait_send()
            self._local(a).wait()


def _exchange_shapes(arrays, gather):
    def shape(a):
        if gather:
            return IN_BLOCKED if a.shape == (D_MODEL, SHARD_PAD) else (N_DEV,) + a.shape
        return (N_DEV, SHARD_PAD, D_MODEL) if a.shape == IN_BLOCKED_T else a.shape
    return [_sds(shape(a), a.dtype) for a in arrays]


def _exchange_call(arrays, gather, name):
    na = len(arrays)

    def body(*refs):
        ex = _Exchange(refs[:na], refs[na:2 * na], refs[2 * na:], gather)
        ex.start()
        ex.wait()

    return pl.pallas_call(
        body, name=name,
        in_specs=[HBM_SPEC] * na, out_specs=[HBM_SPEC] * na,
        out_shape=_exchange_shapes(arrays, gather),
        scratch_shapes=_Exchange.semaphores(na),
    )(*arrays)


def _layer_fwd(x, small, w_in, g_rest, rest_shards, tabs, next_shards):
    attn_g, sinks, gq, gkv = small
    proj, h, gathered_rest = _in_proj(x, attn_g, w_in, rest_shards)
    w_q, w_kv, w_o = _rest_from_gathered(*(gathered_rest if rest_shards else g_rest))
    swa = _swa_fwd(proj, sinks)
    qh, kh, vh, vth = _mla_prep(proj, gq, gkv, w_q, w_kv, tabs)
    mla, lse, gathered = _mla_fwd(qh, kh, vth, next_shards)
    x_next, y = _out_proj(x, proj, swa, mla, w_o)
    return x_next, (x, proj, h, swa, qh, kh, vh, mla, lse, y), (w_in, w_q, w_kv, w_o), gathered


def _layer_bwd(dx, dxb, saved, small, weights, tabs, pending, send_own):
    attn_g, sinks, gq, gkv = small
    w_in, w_q, w_kv, w_o = weights
    x, proj, h, swa, qh, kh, vh, mla, lse, y = saved
    d_o = _matmul_tn(y, dxb, "grad_w_out")
    o_block = d_o.reshape(N_DEV, 256, D_MODEL)
    do_a, do_b, dgate, dlt = _out_proj_bwd(dxb, proj, swa, mla, w_o)
    dqa, dkva, dsk, got_o = _swa_bwd(proj, sinks, do_a, swa, [o_block] if send_own else [])
    dqh, dkh, dvh, received = _mla_bwd(qh, kh, vh, do_b, dlt, lse, pending)
    dlat, d_q, d_kv, d_gq, d_gkv = _mla_prep_bwd(dqh, dkh, dvh, proj, gq, gkv, w_q, w_kv, tabs)
    qkv_blocks = _qkv_grad_blocks(d_q, d_kv)
    d_inp, got_qkv = _grad_w_in(h, dqa, dgate, dlat, dkva, qkv_blocks if send_own else [])
    in_block = _in_grad_blocks(d_inp)
    dx, dxb, d_attn, got_in = _in_proj_bwd(dqa, dgate, dlat, dkva, w_in, x, dx, attn_g,
                                           [in_block] if send_own else [])
    small_grads = (d_attn, dsk[0:1, 0:SWA_HEADS], d_gq, d_gkv)
    blocks = [in_block] + qkv_blocks + [o_block]
    return dx, dxb, small_grads, blocks, received, got_in + got_qkv + got_o


def _pack_small_grads(small_grads, d_final, loss):
    d_attn, d_sink, d_gq, d_gkv = zip(*small_grads)
    return jnp.concatenate([
        jnp.concatenate(d_attn, axis=0).reshape(64, 128),
        jnp.concatenate(d_gq, axis=0).reshape(12, 128),
        jnp.concatenate(d_gkv, axis=0).reshape(8, 128),
        d_final.reshape(16, 128),
        jnp.pad(jnp.concatenate(d_sink, axis=1), ((0, 0), (0, 64))),
        loss[0:1],
        jnp.zeros((PACK_ROWS - ROW_LOSS - 1, 128), F32)], axis=0)


def _pack_small(attn, qa, kva, final, sinks):
    return jnp.concatenate([
        attn.reshape(64, 128), qa.reshape(12, 128), kva.reshape(8, 128), final.reshape(16, 128),
        jnp.pad(sinks.reshape(1, 64), ((0, 0), (0, 64))),
        jnp.zeros((PACK_ROWS - ROW_SINK - 1, 128), F32)], axis=0)


def _unpack_small(p):
    return (p[ROW_ATTN:ROW_QA].reshape(DEPTH, D_MODEL), p[ROW_SINK, 0:64].reshape(DEPTH, SWA_HEADS),
            p[ROW_QA:ROW_KVA].reshape(DEPTH, Q_RANK), p[ROW_KVA:ROW_FINAL].reshape(DEPTH, KV_RANK),
            p[ROW_FINAL:ROW_SINK].reshape(D_MODEL))


def kernel(x, attn_norm_g, w_in, swa_sinks, q_a_norm_g, kv_a_norm_g, w_q_b, w_kv_b, w_out, final_norm_g, loss_target, m_attn_norm_g, m_w_in, m_swa_sinks, m_q_a_norm_g, m_kv_a_norm_g, m_w_q_b, m_w_kv_b, m_w_out, m_final_norm_g, v_attn_norm_g, v_w_in, v_swa_sinks, v_q_a_norm_g, v_kv_a_norm_g, v_w_q_b, v_w_kv_b, v_w_out, v_final_norm_g):
    xs, tgt = x[0], loss_target[0]
    tabs = _rope_tables(xs.shape[0])
    shards = [w.astype(MXU_DTYPE) for w in (w_in, w_q_b, w_kv_b, w_out)]
    shards[0] = jnp.pad(shards[0], ((0, 0), (0, 0), (0, SHARD_PAD - SHARD_COLS)))
    layer_shards = lambda l: [w[l] for w in shards]
    smalls = [(attn_norm_g[l:l + 1], swa_sinks[l], q_a_norm_g[l:l + 1], kv_a_norm_g[l:l + 1])
              for l in range(DEPTH)]

    gathered = list(_all_gather(layer_shards(0)[:1], "gather_weights")) + [None] * 3
    weights, saved = [None] * DEPTH, []
    for l in range(DEPTH):
        next_shards = layer_shards(l + 1) if l + 1 < DEPTH else []
        rest_shards = layer_shards(0)[1:] if l == 0 else []
        xs, acts, weights[l], gathered = _layer_fwd(
            xs, smalls[l], _w_in_from_gathered(gathered[0]), gathered[1:], rest_shards, tabs,
            next_shards)
        saved.append(acts)
    dx, dxb, d_final, loss = _final_loss(xs, tgt, final_norm_g.reshape(1, D_MODEL))

    received, small_grads, pending = [None] * DEPTH, [None] * DEPTH, []
    for l in reversed(range(DEPTH)):
        dx, dxb, small_grads[l], blocks, arrived, arrived_own = _layer_bwd(
            dx, dxb, saved[l], smalls[l], weights[l], tabs, pending, send_own=(l == 0))
        if pending:
            received[l + 1] = arrived
        pending = blocks
    received[0] = arrived_own
    small = _exchange_call([_pack_small_grads(small_grads, d_final, loss)], True, "gather_small")[0]

    big = []
    for a, (w, m, v, name) in enumerate(zip((w_in, w_q_b, w_kv_b, w_out),
                                            (m_w_in, m_w_q_b, m_w_kv_b, m_w_out),
                                            (v_w_in, v_w_q_b, v_w_kv_b, v_w_out),
                                            ("adamw_w_in", "adamw_w_q_b", "adamw_w_kv_b",
                                             "adamw_w_out"))):
        parts = [received[l][a] for l in range(DEPTH)]
        if a == 0:
            swap = lambda t: t.transpose(0, 2, 1)
            outs = _reduce_adamw_in(parts, swap(w), swap(m), swap(v), name)
            big.append([swap(t) for t in outs])
        else:
            cols = w.shape[-1]
            flat = lambda t: t.reshape(-1, cols)
            outs = _reduce_adamw(parts, flat(w), flat(m), flat(v), name)
            big.append([t.reshape(w.shape) for t in outs])

    sm = _reduce_adamw(
        [small],
        _pack_small(attn_norm_g, q_a_norm_g, kv_a_norm_g, final_norm_g, swa_sinks),
        _pack_small(m_attn_norm_g, m_q_a_norm_g, m_kv_a_norm_g, m_final_norm_g, m_swa_sinks),
        _pack_small(v_attn_norm_g, v_q_a_norm_g, v_kv_a_norm_g, v_final_norm_g, v_swa_sinks),
        "adamw_small")
    loss = sm[0][ROW_LOSS, 0]
    kinds = []
    for t in range(4):
        attn, sinks, qa, kva, final = _unpack_small(sm[t])
        b_in, b_q, b_kv, b_o = (big[i][t] for i in range(4))
        kinds.append((attn, b_in, sinks, qa, kva, b_q, b_kv, b_o, final))
    return (loss, dx[None], *kinds[0], *kinds[1], *kinds[2], *kinds[3])
```

```python
import jax
import jax.numpy as jnp
import numpy as np
from jax import lax
from jax.experimental import pallas as pl
from jax.experimental.pallas import tpu as pltpu

F32 = jnp.float32
BF16 = jnp.bfloat16
MXU_DTYPE = BF16
GRAD_DTYPE = BF16
PROJ_DTYPE = BF16
ATTN_DTYPE = BF16

D_MODEL = 2048
DEPTH = 4
EPS = 1e-6
NEG = -1e30
BLOCK = 128
SWA_HEADS = 16
MLA_HEADS = 8
Q_RANK = 384
KV_RANK = 256
MLA_SCALE = 192 ** -0.5
MLA_C2 = MLA_SCALE * 1.4426950408889634
ROPE_THETA = 10000.0
IN_WIDTH = 4032

ADAM_LR = 0.001
ADAM_B1 = 0.9
ADAM_B2 = 0.999
ADAM_EPS = 1e-08
ADAM_WD = 0.01
ADAM_STEP = 10

N_DEV = 8
MESH = pl.DeviceIdType.MESH

NP = 4096
QA, GA, GB, CQ, CKV, KR, KVA = 0, 1024, 2048, 3072, 3456, 3712, 3840

ROW_ATTN, ROW_QA, ROW_KVA, ROW_FINAL, ROW_SINK, ROW_LOSS, PACK_ROWS = 0, 64, 76, 84, 100, 101, 104

VMEM_LIMIT = 56 * 1024 * 1024
STREAM_DEPTH = 3
MLA_TILE = 512
GATHER_FORWARD_HEAD = 7


def _sds(shape, dtype):
    return jax.ShapeDtypeStruct(shape, dtype)


def _params(n_axes):
    return pltpu.CompilerParams(dimension_semantics=("arbitrary",) * n_axes,
                                vmem_limit_bytes=VMEM_LIMIT)


def _mm(a, b):
    return jnp.dot(a.astype(MXU_DTYPE), b.astype(MXU_DTYPE), preferred_element_type=F32)


def _mm_nt(a, b):
    return lax.dot_general(a.astype(MXU_DTYPE), b.astype(MXU_DTYPE),
                           (((1,), (1,)), ((), ())), preferred_element_type=F32)


def _mm_tn(a, b):
    return lax.dot_general(a.astype(MXU_DTYPE), b.astype(MXU_DTYPE),
                           (((0,), (0,)), ((), ())), preferred_element_type=F32)


def _rownorm(x):
    r = lax.rsqrt(jnp.mean(x * x, axis=-1, keepdims=True) + EPS)
    return x * r, r


def _rownorm_bwd(dxh, xh, r):
    return r * (dxh - xh * jnp.mean(dxh * xh, axis=-1, keepdims=True))


def _rope(t, c, s1, s2):
    return t * c + pltpu.roll(t, 32, 1) * s1 + pltpu.roll(t, 96, 1) * s2


SHARD_COLS = IN_WIDTH // N_DEV
SHARD_PAD = 512


def _orig_col_of_padded():
    o = np.full((NP,), -1, np.int64)
    for start, width, orig in ((QA, 1024, 0), (KVA, 256, 1024), (GA, 1024, 1280), (CQ, 384, 2304),
                               (CKV, 256, 2688), (KR, 64, 2944), (GB, 1024, 3008)):
        o[start:start + width] = np.arange(orig, orig + width)
    return o


def _device_major_src():
    o = _orig_col_of_padded()
    return np.where(o >= 0, o + (SHARD_PAD - SHARD_COLS) * (o // SHARD_COLS), -1)


def _kernel_layout_src():
    o = _orig_col_of_padded()
    where = np.full((IN_WIDTH,), -1, np.int64)
    where[o[o >= 0]] = np.nonzero(o >= 0)[0]
    e = np.arange(N_DEV * SHARD_PAD)
    k, c = e // SHARD_PAD, e % SHARD_PAD
    return np.where(c < SHARD_COLS, where[np.minimum(SHARD_COLS * k + c, IN_WIDTH - 1)], -1)


def _permute_columns(x, src_of, name, transposed=False):
    rows, n_in = x.shape
    n_out = len(src_of)
    plan, mats = [], []
    for t in range(n_out // 128):
        srcs = src_of[128 * t:128 * (t + 1)]
        entry = []
        for u in sorted(set(int(s) // 128 for s in srcs if s >= 0)):
            m = np.zeros((128, 128), np.float32)
            for c, s in enumerate(srcs):
                if s >= 0 and s // 128 == u:
                    m[s % 128, c] = 1.0
            entry.append((u, len(mats)))
            mats.append(m)
        plan.append(entry)
    tr = min(512, rows)

    def body(x_ref, p_ref, o_ref):
        for t, entry in enumerate(plan):
            acc = jnp.zeros((tr, 128), F32)
            for u, idx in entry:
                acc = acc + jnp.dot(x_ref[:, 128 * u:128 * (u + 1)], p_ref[idx],
                                    preferred_element_type=F32)
            if transposed:
                o_ref[128 * t:128 * (t + 1), :] = acc.T.astype(o_ref.dtype)
            else:
                o_ref[:, 128 * t:128 * (t + 1)] = acc.astype(o_ref.dtype)

    table = jnp.asarray(np.stack(mats), x.dtype)
    return pl.pallas_call(
        body, name=name, grid=(rows // tr,),
        in_specs=[pl.BlockSpec((tr, n_in), lambda i: (i, 0)),
                  pl.BlockSpec(table.shape, lambda i: (0, 0, 0))],
        out_specs=(pl.BlockSpec((n_out, tr), lambda i: (0, i)) if transposed
                   else pl.BlockSpec((tr, n_out), lambda i: (i, 0))),
        out_shape=_sds((n_out, rows) if transposed else (rows, n_out), x.dtype),
        compiler_params=_params(1),
    )(x, table)


def _w_in_from_gathered(g_in):
    return _permute_columns(g_in, _device_major_src(), "w_in_layout")


def _rest_from_gathered(g_qb, g_kvb, g_out):
    qb = g_qb.transpose(1, 0, 2)
    rope = jnp.pad(qb[..., 128:], ((0, 0), (0, 0), (0, 64)))
    w_q = jnp.concatenate([qb[..., :128].reshape(Q_RANK, 1024),
                           rope.reshape(Q_RANK, 1024)], axis=-1)
    kvb = g_kvb.transpose(1, 0, 2)
    w_kv = jnp.concatenate([kvb[..., :128].reshape(KV_RANK, 1024),
                            kvb[..., 128:].reshape(KV_RANK, 1024)], axis=-1)
    w_o = g_out.reshape(D_MODEL, D_MODEL)
    return w_q, w_kv, w_o


def _in_grad_blocks(d_inp):
    return _permute_columns(d_inp, _kernel_layout_src(), "grad_w_in_layout", transposed=True)


def _qkv_grad_blocks(d_q, d_kv):
    qn = d_q[:, :1024].reshape(Q_RANK, 8, 128)
    qr = d_q[:, 1024:].reshape(Q_RANK, 8, 128)[..., :64]
    b_q = jnp.concatenate([qn, qr], axis=-1).transpose(1, 0, 2)
    kn = d_kv[:, :1024].reshape(KV_RANK, 8, 128)
    vv = d_kv[:, 1024:].reshape(KV_RANK, 8, 128)
    b_kv = jnp.concatenate([kn, vv], axis=-1).transpose(1, 0, 2)
    return [b_q, b_kv]


def _rope_tables(s):
    pos = jnp.arange(s, dtype=F32)
    inv_freq = ROPE_THETA ** (-jnp.arange(0, 64, 2, dtype=F32) / 64)
    ang = pos[:, None] * inv_freq[None, :]
    cos, sin = jnp.cos(ang), jnp.sin(ang)
    z32 = jnp.zeros((s, 32), F32)
    z64 = jnp.zeros((s, 64), F32)
    c = jnp.concatenate([cos, cos, z64], axis=1)
    s1 = jnp.concatenate([z32, sin, z64], axis=1)
    s2 = jnp.concatenate([-sin, z32, z64], axis=1)
    return c, s1, s2


def _in_proj(x, g, w, shards):
    s = x.shape[0]
    tm, tn = min(512, s), 1024
    nm = s // tm
    na = len(shards)

    def body(*refs):
        x_ref, g_ref, w_ref = refs[:3]
        o_ref, h_ref = refs[3 + na:5 + na]
        i = pl.program_id(0)
        if na:
            ex = _Gather(refs[3:3 + na], refs[5 + na:5 + 2 * na], refs[5 + 2 * na:])

            @pl.when(i == 0)
            def _():
                ex.start()

            @pl.when(i == (7 * nm) // 8)
            def _():
                ex.forward()

        xh, _ = _rownorm(x_ref[...])
        h_ref[...] = (xh * g_ref[...]).astype(h_ref.dtype)
        for j in range(NP // tn):
            cols = slice(j * tn, (j + 1) * tn)
            o_ref[:, cols] = jnp.dot(h_ref[...], w_ref[:, cols],
                                     preferred_element_type=F32).astype(o_ref.dtype)

        if na:
            @pl.when(i == nm - 1)
            def _():
                ex.finish()

    row = lambda i: (i, 0)
    fixed = lambda i: (0, 0)
    outs = pl.pallas_call(
        body, name="in_proj_gather" if na else "in_proj", grid=(nm,),
        in_specs=[pl.BlockSpec((tm, D_MODEL), row), pl.BlockSpec((1, D_MODEL), fixed),
                  pl.BlockSpec((D_MODEL, NP), fixed, pipeline_mode=pl.Buffered(1))]
        + [HBM_SPEC] * na,
        out_specs=[pl.BlockSpec((tm, NP), row), pl.BlockSpec((tm, D_MODEL), row)]
        + [HBM_SPEC] * na,
        out_shape=[_sds((s, NP), PROJ_DTYPE), _sds((s, D_MODEL), MXU_DTYPE)]
        + _exchange_shapes(shards, True),
        scratch_shapes=_Exchange.semaphores(na) if na else [],
        compiler_params=_params(1),
    )(x, g, w, *shards)
    return outs[0], outs[1], list(outs[2:])


def _swa_slopes():
    return [2.0 ** (-8.0 * (h + 1) / SWA_HEADS) for h in range(SWA_HEADS)]


SWA_STACK = SWA_HEADS * BLOCK


def _swa_head(b):
    j, a = b // 8, b % 8
    return 2 * (4 * j + a % 4) + a // 4


def _swa_kv(kv_p, kv_c):
    return (jnp.concatenate([kv_p[:, :128], kv_c[:, :128]], axis=0),
            jnp.concatenate([kv_p[:, 128:], kv_c[:, 128:]], axis=0))


def _swa_by_block(vals):
    b = lax.broadcasted_iota(jnp.int32, (1, SWA_STACK), 1) >> 7
    row = jnp.full((1, SWA_STACK), vals[-1], F32)
    for t in range(len(vals) - 2, -1, -1):
        row = jnp.where(b == t, vals[t], row)
    return row


def _swa_bias(n):
    slopes = _swa_slopes()
    ki = lax.broadcasted_iota(jnp.int32, (2 * BLOCK, SWA_STACK), 0)
    r = lax.broadcasted_iota(jnp.int32, (2 * BLOCK, SWA_STACK), 1)
    delta = BLOCK + (r & (BLOCK - 1)) - ki
    valid = (delta >= 0) & (delta < BLOCK) & ((n - 1) * BLOCK + ki >= 0)
    slope = _swa_by_block([slopes[_swa_head(b)] for b in range(SWA_HEADS)])
    return jnp.where(valid, -slope * delta.astype(F32), NEG)


def _swa_fill_bias(n, bias_ref):
    @pl.when(n <= 1)
    def _():
        bias_ref[...] = _swa_bias(n)


def _swa_sink_row(sink_ref):
    return _swa_by_block([sink_ref[_swa_head(b)] for b in range(SWA_HEADS)])


def _swa_pairs(ref, j):
    return jnp.concatenate([ref[:, 128 * (4 * j + a):128 * (4 * j + a + 1)] for a in range(4)],
                           axis=0).astype(F32)


def _swa_stack(ref):
    left = lax.broadcasted_iota(jnp.int32, (8 * BLOCK, 128), 1) < 64
    p0, p1 = _swa_pairs(ref, 0), _swa_pairs(ref, 1)
    s0 = jnp.concatenate([p0, pltpu.roll(p0, 64, 1)], axis=0)
    s1 = jnp.concatenate([pltpu.roll(p1, 64, 1), p1], axis=0)
    return jnp.concatenate([jnp.where(left, s0, 0.0), jnp.where(left, 0.0, s1)], axis=0)


def _swa_unstack(t):
    left = lax.broadcasted_iota(jnp.int32, (4 * BLOCK, 128), 1) < 64
    n = 4 * BLOCK
    g0 = jnp.where(left, t[0:n], pltpu.roll(t[n:2 * n], 64, 1))
    g1 = jnp.where(left, pltpu.roll(t[2 * n:3 * n], 64, 1), t[3 * n:4 * n])
    return g0, g1


def _swa_softmax(qs, kk, bias, sink):
    sc = _mm_nt(kk, qs) + bias
    m = jnp.maximum(jnp.max(sc, axis=0, keepdims=True), sink)
    ex = jnp.exp(sc - m)
    es = jnp.exp(sink - m)
    return ex, es, 1.0 / (jnp.sum(ex, axis=0, keepdims=True) + es)


def _swa_fwd(proj, sinks):
    s = proj.shape[0]
    nb = s // BLOCK

    def body(sink_ref, q_ref, kp_ref, kc_ref, o_ref, bias_ref):
        n = pl.program_id(0)
        _swa_fill_bias(n, bias_ref)
        kk, vv = _swa_kv(kp_ref[...].astype(F32), kc_ref[...].astype(F32))
        qs = _swa_stack(q_ref) * 0.125
        ex, _, inv = _swa_softmax(qs, kk, bias_ref[...], _swa_sink_row(sink_ref))
        o_t = _mm(vv.T, ex) * inv
        n4 = 4 * BLOCK
        for j in range(2):
            rows = slice(64 * j, 64 * (j + 1))
            out = jnp.concatenate([o_t[rows, 2 * j * n4:(2 * j + 1) * n4],
                                   o_t[rows, (2 * j + 1) * n4:(2 * j + 2) * n4]], axis=0).T
            for a in range(4):
                o_ref[:, 128 * (4 * j + a):128 * (4 * j + a + 1)] = out[128 * a:128 * (a + 1)].astype(
                    o_ref.dtype)

    return pl.pallas_call(
        body, name="swa_fwd", grid=(nb,),
        in_specs=[pl.BlockSpec(memory_space=pltpu.SMEM),
                  pl.BlockSpec((BLOCK, 1024), lambda n: (n, 0)),
                  pl.BlockSpec((BLOCK, 256), lambda n: (jnp.maximum(n - 1, 0), KVA // 256)),
                  pl.BlockSpec((BLOCK, 256), lambda n: (n, KVA // 256))],
        out_specs=pl.BlockSpec((BLOCK, 1024), lambda n: (n, 0)),
        out_shape=_sds((s, 1024), ATTN_DTYPE),
        scratch_shapes=[pltpu.VMEM((2 * BLOCK, SWA_STACK), F32)],
        compiler_params=_params(1),
    )(sinks, proj, proj, proj)


def _mla_prep(proj, gq, gkv, w_q, w_kv, tabs):
    s = proj.shape[0]
    tm = min(512, s)
    c, s1, s2 = tabs

    def body(p_ref, gq_ref, gkv_ref, wq_ref, wkv_ref, c_ref, s1_ref, s2_ref,
             q_ref, k_ref, v_ref, vt_ref):
        cqh, _ = _rownorm(p_ref[:, 0:384].astype(F32))
        ckvh, _ = _rownorm(p_ref[:, 384:640].astype(F32))
        q = _mm(cqh * gq_ref[...], wq_ref[...])
        kv = _mm(ckvh * gkv_ref[...], wkv_ref[...])
        cc, ss1, ss2 = c_ref[...], s1_ref[...], s2_ref[...]
        krr = _rope(p_ref[:, 640:768].astype(F32), cc, ss1, ss2).astype(k_ref.dtype)
        for h in range(MLA_HEADS):
            q_ref[h, :, 0:128] = q[:, 128 * h:128 * (h + 1)].astype(q_ref.dtype)
            q_ref[h, :, 128:256] = _rope(q[:, 1024 + 128 * h:1024 + 128 * (h + 1)],
                                         cc, ss1, ss2).astype(q_ref.dtype)
            k_ref[h, :, 0:128] = kv[:, 128 * h:128 * (h + 1)].astype(k_ref.dtype)
            k_ref[h, :, 128:256] = krr
            vv = kv[:, 1024 + 128 * h:1024 + 128 * (h + 1)]
            v_ref[h] = vv.astype(v_ref.dtype)
            vt_ref[h, 0:128, :] = vv.T.astype(vt_ref.dtype)
            vt_ref[h, 128:256, :] = jnp.ones((128, tm), vt_ref.dtype)

    row = lambda i: (i, 0)
    fixed = lambda i: (0, 0)
    return pl.pallas_call(
        body, name="mla_prep", grid=(s // tm,),
        in_specs=[pl.BlockSpec((tm, 768), lambda i: (i, CQ // 768)),
                  pl.BlockSpec((1, Q_RANK), fixed), pl.BlockSpec((1, KV_RANK), fixed),
                  pl.BlockSpec((Q_RANK, 2048), fixed), pl.BlockSpec((KV_RANK, 2048), fixed),
                  pl.BlockSpec((tm, 128), row), pl.BlockSpec((tm, 128), row),
                  pl.BlockSpec((tm, 128), row)],
        out_specs=[pl.BlockSpec((MLA_HEADS, tm, 256), lambda i: (0, i, 0)),
                   pl.BlockSpec((MLA_HEADS, tm, 256), lambda i: (0, i, 0)),
                   pl.BlockSpec((MLA_HEADS, tm, 128), lambda i: (0, i, 0)),
                   pl.BlockSpec((MLA_HEADS, 256, tm), lambda i: (0, 0, i))],
        out_shape=[_sds((MLA_HEADS, s, 256), MXU_DTYPE), _sds((MLA_HEADS, s, 256), MXU_DTYPE),
                   _sds((MLA_HEADS, s, 128), MXU_DTYPE), _sds((MLA_HEADS, 256, s), MXU_DTYPE)],
        compiler_params=_params(1),
    )(proj, gq, gkv, w_q, w_kv, c, s1, s2)


def _scores_t(k, q, t, diagonal):
    sc = _mm_nt(k, q)
    if diagonal:
        key = lax.broadcasted_iota(jnp.int32, sc.shape, 0)
        query = lax.broadcasted_iota(jnp.int32, sc.shape, 1)
        sc = jnp.where(key <= query, sc, NEG)
    return sc


def _mla_fwd(qh, kh, vth, shards):
    s = qh.shape[1]
    t = min(MLA_TILE, s)
    nq = s // t
    na = len(shards)

    def body(*refs):
        q_ref, k_ref, vt_ref = refs[:3]
        o_ref, lse_ref = refs[3 + na:5 + na]
        m_ref, acc_ref = refs[5 + 2 * na:7 + 2 * na]
        h, i = pl.program_id(0), pl.program_id(1)
        if na:
            ex = _Gather(refs[3:3 + na], refs[5 + na:5 + 2 * na], refs[7 + 2 * na:])

            @pl.when((h == 0) & (i == 0))
            def _():
                ex.start()

            @pl.when((h == GATHER_FORWARD_HEAD) & (i == 0))
            def _():
                ex.forward()

        m_ref[...] = jnp.full(m_ref.shape, NEG, F32)
        acc_ref[...] = jnp.zeros(acc_ref.shape, F32)

        def step(start, width):
            keys = pl.ds(pl.multiple_of(start, t), width)
            sc = _mm_nt(k_ref[keys, :], q_ref[...])
            m_prev = m_ref[...]
            m_new = jnp.maximum(m_prev, jnp.max(sc, axis=0, keepdims=True))
            alpha = jnp.exp2((m_prev - m_new) * MLA_C2)
            p = jnp.exp2((sc - m_new[0:1, :]) * MLA_C2)
            acc_ref[...] = alpha[0:1, :] * acc_ref[...] + _mm(vt_ref[:, keys], p)
            m_ref[...] = m_new

        def diagonal_step():
            half = t // 2
            k_lo = pl.ds(pl.multiple_of(i * t, half), half)
            k_hi = pl.ds(pl.multiple_of(i * t + half, half), half)
            sc_lo = _scores_t(k_ref[k_lo, :], q_ref[...], t, True)
            sc_hi = _scores_t(k_ref[k_hi, :], q_ref[half:t, :], t, True)
            m_prev = m_ref[...]
            m_lo = jnp.maximum(m_prev, jnp.max(sc_lo, axis=0, keepdims=True))
            m_new = jnp.concatenate(
                [m_lo[:, 0:half],
                 jnp.maximum(m_lo[:, half:t], jnp.max(sc_hi, axis=0, keepdims=True))], axis=1)
            alpha = jnp.exp2((m_prev - m_new) * MLA_C2)
            p_lo = jnp.exp2((sc_lo - m_new[0:1, :]) * MLA_C2)
            p_hi = jnp.exp2((sc_hi - m_new[0:1, half:t]) * MLA_C2)
            acc_ref[...] = alpha[0:1, :] * acc_ref[...] + _mm(vt_ref[:, k_lo], p_lo)
            acc_ref[:, half:t] += _mm(vt_ref[:, k_hi], p_hi)
            m_ref[...] = m_new

        def below_diagonal(jj, carry):
            step(jj * (4 * t), 4 * t)
            return carry

        lax.fori_loop(0, i // 4, below_diagonal, 0)

        @pl.when(i % 4 >= 2)
        def _():
            step((i // 4) * (4 * t), 2 * t)

        @pl.when(i % 2 == 1)
        def _():
            step((i - 1) * t, t)

        diagonal_step()
        l = acc_ref[128:136, :]
        o_ref[...] = (acc_ref[0:128, :] / l[0:1, :]).T.astype(o_ref.dtype)
        lse_ref[...] = m_ref[...] * MLA_C2 + jnp.log2(l)

        if na:
            @pl.when((h == MLA_HEADS - 1) & (i == nq - 1))
            def _():
                ex.finish()

    head = lambda h, i: (h, 0, 0)
    outs = pl.pallas_call(
        body, name="mla_fwd_gather" if na else "mla_fwd", grid=(MLA_HEADS, nq),
        in_specs=[pl.BlockSpec((None, t, 256), lambda h, i: (h, i, 0)),
                  pl.BlockSpec((None, s, 256), head),
                  pl.BlockSpec((None, 256, s), head)] + [HBM_SPEC] * na,
        out_specs=[pl.BlockSpec((t, 128), lambda h, i: (i, h)),
                   pl.BlockSpec((None, 8, t), lambda h, i: (h, 0, i))] + [HBM_SPEC] * na,
        out_shape=[_sds((s, 1024), ATTN_DTYPE), _sds((MLA_HEADS, 8, s), F32)]
        + _exchange_shapes(shards, True),
        scratch_shapes=[pltpu.VMEM((8, t), F32), pltpu.VMEM((256, t), F32)]
        + (_Exchange.semaphores(na) if na else []),
        compiler_params=_params(2),
    )(qh, kh, vth, *shards)
    return outs[0], outs[1], list(outs[2:])


def _silu_parts(g):
    sg = jax.nn.sigmoid(g)
    return g * sg, sg * (1.0 + g * (1.0 - sg))


def _out_proj(x, proj, swa, mla, w_out):
    s = x.shape[0]
    tm = min(512, s)
    nm = s // tm
    depth = min(STREAM_DEPTH, nm)

    def body(x_hbm, p_hbm, a_hbm, b_hbm, w_ref, xo_ref, y_ref, x_buf, g_buf, a_buf, b_buf, sems):
        i = pl.program_id(0)

        def fetch(tile, slot):
            rows = pl.ds(pl.multiple_of(tile * tm, tm), tm)
            return [pltpu.make_async_copy(x_hbm.at[rows], x_buf.at[slot], sems.at[slot, 0]),
                    pltpu.make_async_copy(p_hbm.at[rows, pl.ds(GA, 2048)], g_buf.at[slot],
                                          sems.at[slot, 1]),
                    pltpu.make_async_copy(a_hbm.at[rows], a_buf.at[slot], sems.at[slot, 2]),
                    pltpu.make_async_copy(b_hbm.at[rows], b_buf.at[slot], sems.at[slot, 3])]

        @pl.when(i == 0)
        def _():
            for t in range(depth):
                for cp in fetch(t, t):
                    cp.start()

        slot = i % depth
        for cp in fetch(i, slot):
            cp.wait()
        gates = g_buf[slot].astype(F32)
        sa, _ = _silu_parts(gates[:, 0:1024])
        sb, _ = _silu_parts(gates[:, 1024:2048])
        y_ref[:, 0:1024] = (a_buf[slot].astype(F32) * sa).astype(y_ref.dtype)
        y_ref[:, 1024:2048] = (b_buf[slot].astype(F32) * sb).astype(y_ref.dtype)
        xo_ref[...] = x_buf[slot] + jnp.dot(y_ref[...], w_ref[...], preferred_element_type=F32)

        @pl.when(i + depth < nm)
        def _():
            for cp in fetch(i + depth, slot):
                cp.start()

    row = lambda i: (i, 0)
    return pl.pallas_call(
        body, name="out_proj", grid=(nm,),
        in_specs=[HBM_SPEC, HBM_SPEC, HBM_SPEC, HBM_SPEC,
                  pl.BlockSpec((D_MODEL, D_MODEL), lambda i: (0, 0), pipeline_mode=pl.Buffered(1))],
        out_specs=[pl.BlockSpec((tm, D_MODEL), row), pl.BlockSpec((tm, D_MODEL), row)],
        out_shape=[_sds((s, D_MODEL), F32), _sds((s, D_MODEL), MXU_DTYPE)],
        scratch_shapes=[pltpu.VMEM((depth, tm, D_MODEL), F32),
                        pltpu.VMEM((depth, tm, 2048), proj.dtype),
                        pltpu.VMEM((depth, tm, 1024), swa.dtype),
                        pltpu.VMEM((depth, tm, 1024), mla.dtype),
                        pltpu.SemaphoreType.DMA((depth, 4))],
        compiler_params=_params(1),
    )(x, proj, swa, mla, w_out)


def _final_loss(x, tgt, g):
    s = x.shape[0]
    tm = min(512, s)

    def body(x_ref, t_ref, g_ref, dx_ref, dxb_ref, dg_ref, loss_ref):
        @pl.when(pl.program_id(0) == 0)
        def _():
            dg_ref[...] = jnp.zeros(dg_ref.shape, F32)
            loss_ref[...] = jnp.zeros(loss_ref.shape, F32)
        xh, r = _rownorm(x_ref[...])
        gg = g_ref[...]
        err = xh * gg - t_ref[...]
        per_row = jnp.mean(err * err, axis=-1, keepdims=True)
        loss_ref[...] += 0.5 * jnp.sum(per_row, axis=0, keepdims=True)
        dy = err * (1.0 / D_MODEL)
        dg_ref[...] += jnp.sum(dy * xh, axis=0, keepdims=True)
        dx = _rownorm_bwd(dy * gg, xh, r)
        dx_ref[...] = dx
        dxb_ref[...] = dx.astype(dxb_ref.dtype)

    row = lambda i: (i, 0)
    fixed = lambda i: (0, 0)
    return pl.pallas_call(
        body, name="final_loss", grid=(s // tm,),
        in_specs=[pl.BlockSpec((tm, D_MODEL), row), pl.BlockSpec((tm, D_MODEL), row),
                  pl.BlockSpec((1, D_MODEL), fixed)],
        out_specs=[pl.BlockSpec((tm, D_MODEL), row), pl.BlockSpec((tm, D_MODEL), row),
                   pl.BlockSpec((1, D_MODEL), fixed), pl.BlockSpec((8, 128), fixed)],
        out_shape=[_sds((s, D_MODEL), F32), _sds((s, D_MODEL), MXU_DTYPE), _sds((1, D_MODEL), F32),
                   _sds((8, 128), F32)],
        compiler_params=_params(1),
    )(x, tgt, g)


def _out_proj_bwd(dx, proj, swa, mla, w_out):
    s = dx.shape[0]
    tm = min(512, s)

    def body(dx_ref, ga_ref, gb_ref, a_ref, b_ref, w_ref, doa_ref, dob_ref, dg_ref, dlt_ref):
        dx = dx_ref[...].astype(MXU_DTYPE)
        dya = _mm_nt(dx, w_ref[0:1024, :])
        sa, dsa = _silu_parts(ga_ref[...].astype(F32))
        doa_ref[...] = (dya * sa).astype(doa_ref.dtype)
        dg_ref[:, 0:1024] = (dya * a_ref[...].astype(F32) * dsa).astype(dg_ref.dtype)
        dyb = _mm_nt(dx, w_ref[1024:2048, :])
        sb, dsb = _silu_parts(gb_ref[...].astype(F32))
        b = b_ref[...].astype(F32)
        dob = dyb * sb
        dob_ref[...] = dob.astype(dob_ref.dtype)
        dg_ref[:, 1024:2048] = (dyb * b * dsb).astype(dg_ref.dtype)
        prod = dob * b
        for h in range(MLA_HEADS):
            dlt = jnp.sum(prod[:, 128 * h:128 * (h + 1)], axis=1, keepdims=True)
            dlt_ref[h] = jnp.broadcast_to(dlt, (tm, 128)).T[0:8, :]

    row = lambda i: (i, 0)
    return pl.pallas_call(
        body, name="out_proj_bwd", grid=(s // tm,),
        in_specs=[pl.BlockSpec((tm, D_MODEL), row),
                  pl.BlockSpec((tm, 1024), lambda i: (i, GA // 1024)),
                  pl.BlockSpec((tm, 1024), lambda i: (i, GB // 1024)),
                  pl.BlockSpec((tm, 1024), row), pl.BlockSpec((tm, 1024), row),
                  pl.BlockSpec((D_MODEL, D_MODEL), lambda i: (0, 0), pipeline_mode=pl.Buffered(1))],
        out_specs=[pl.BlockSpec((tm, 1024), row), pl.BlockSpec((tm, 1024), row),
                   pl.BlockSpec((tm, D_MODEL), row),
                   pl.BlockSpec((MLA_HEADS, 8, tm), lambda i: (0, 0, i))],
        out_shape=[_sds((s, 1024), ATTN_DTYPE), _sds((s, 1024), MXU_DTYPE), _sds((s, D_MODEL), MXU_DTYPE),
                   _sds((MLA_HEADS, 8, s), F32)],
        compiler_params=_params(1),
    )(dx, proj, proj, swa, mla, w_out)


def _matmul_tn(a, b, name):
    s, m = a.shape
    n = b.shape[1]
    tm, tn, tk = min(1024, m), min(1024, n), min(2048, s)
    nk = s // tk

    def body(a_ref, b_ref, o_ref, acc_ref):
        k = pl.program_id(2)

        @pl.when(k == 0)
        def _():
            acc_ref[...] = jnp.zeros(acc_ref.shape, F32)
        acc_ref[...] += _mm_tn(a_ref[...], b_ref[...])

        @pl.when(k == nk - 1)
        def _():
            o_ref[...] = acc_ref[...].astype(o_ref.dtype)

    return pl.pallas_call(
        body, name=name, grid=(m // tm, n // tn, nk),
        in_specs=[pl.BlockSpec((tk, tm), lambda i, j, k: (k, i)),
                  pl.BlockSpec((tk, tn), lambda i, j, k: (k, j))],
        out_specs=pl.BlockSpec((tm, tn), lambda i, j, k: (i, j)),
        out_shape=_sds((m, n), GRAD_DTYPE),
        scratch_shapes=[pltpu.VMEM((tm, tn), F32)],
        compiler_params=_params(3),
    )(a, b)


def _grad_w_in(h, dqa, dgate, dlat, dkva, blocks):
    s = h.shape[0]
    tm, tn, tk = 1024, 1024, min(2048, s)
    nk = s // tk
    grid = (D_MODEL // tm, NP // tn, nk)
    na = len(blocks)

    def body(*refs):
        a_ref, dqa_ref, dg8_ref, dlat_ref, dkva_ref = refs[:5]
        o_ref = refs[5 + na]
        acc_ref = refs[6 + 2 * na]
        i, j, k = pl.program_id(0), pl.program_id(1), pl.program_id(2)
        if na:
            ex = _Exchange(refs[5:5 + na], refs[6 + na:6 + 2 * na], refs[7 + 2 * na:], gather=False)

            @pl.when((i == 0) & (j == 0) & (k == 0))
            def _():
                ex.start()

        @pl.when(k == 0)
        def _():
            acc_ref[...] = jnp.zeros(acc_ref.shape, F32)

        @pl.when(j == QA // tn)
        def _():
            acc_ref[...] += _mm_tn(a_ref[...], dqa_ref[...])

        @pl.when((j == GA // tn) | (j == GB // tn))
        def _():
            acc_ref[...] += _mm_tn(a_ref[...], dg8_ref[...])

        @pl.when(j == CQ // tn)
        def _():
            acc_ref[:, 0:768] += _mm_tn(a_ref[...], dlat_ref[...])
            acc_ref[:, 768:1024] += _mm_tn(a_ref[...], dkva_ref[...])

        @pl.when(k == nk - 1)
        def _():
            o_ref[...] = acc_ref[...].astype(o_ref.dtype)

        if na:
            @pl.when((i == grid[0] - 1) & (j == grid[1] - 1) & (k == nk - 1))
            def _():
                ex.wait()

    def when(group, width):
        return pl.BlockSpec((tk, width), lambda i, j, k: (jnp.where(j == group, k, 0), 0))

    outs = pl.pallas_call(
        body, name="grad_w_in_scatter" if na else "grad_w_in", grid=grid,
        in_specs=[pl.BlockSpec((tk, tm), lambda i, j, k: (k, i)),
                  when(QA // tn, 1024),
                  pl.BlockSpec((tk, 1024), lambda i, j, k: (
                      jnp.where((j == GA // tn) | (j == GB // tn), k, 0),
                      jnp.clip(j - GA // tn, 0, 1))),
                  when(CQ // tn, 768), when(CQ // tn, 256)] + [HBM_SPEC] * na,
        out_specs=[pl.BlockSpec((tm, tn), lambda i, j, k: (i, j))] + [HBM_SPEC] * na,
        out_shape=[_sds((D_MODEL, NP), GRAD_DTYPE)] + _exchange_shapes(blocks, False),
        scratch_shapes=[pltpu.VMEM((tm, tn), F32)] + (_Exchange.semaphores(na) if na else []),
        compiler_params=_params(3),
    )(h, dqa, dgate, dlat, dkva, *blocks)
    return outs[0], list(outs[1:])


def _swa_bwd(proj, sinks, do, o, blocks):
    s = proj.shape[0]
    nb = s // BLOCK
    na = len(blocks)

    def body(*refs):
        sink_ref, q_ref, kp_ref, kc_ref, do_ref, o_ref = refs[:6]
        dq_ref, dkv_ref, dsink_ref = refs[6 + na:9 + na]
        carry_ref, bias_ref = refs[9 + 2 * na:11 + 2 * na]
        n = pl.program_id(0)
        _swa_fill_bias(n, bias_ref)
        if na:
            ex = _Exchange(refs[6:6 + na], refs[9 + na:9 + 2 * na], refs[11 + 2 * na:], gather=False)

        @pl.when(n == 0)
        def _():
            carry_ref[...] = jnp.zeros(carry_ref.shape, F32)
            dsink_ref[...] = jnp.zeros(dsink_ref.shape, F32)
            if na:
                ex.start()

        @pl.when(n < nb)
        def _():
            kk, vv = _swa_kv(kp_ref[...].astype(F32), kc_ref[...].astype(F32))
            lane_s = lax.broadcasted_iota(jnp.int32, (8, 128), 1)
            qs = _swa_stack(q_ref) * 0.125
            dos = _swa_stack(do_ref)
            dlt = []
            for j in range(2):
                prod_t = (_swa_pairs(do_ref, j) * _swa_pairs(o_ref, j)).T
                dlt += [jnp.sum(prod_t[0:64], axis=0, keepdims=True),
                        jnp.sum(prod_t[64:128], axis=0, keepdims=True)]
            dlt = jnp.concatenate(dlt, axis=1)
            ex, es, inv = _swa_softmax(qs, kk, bias_ref[...], _swa_sink_row(sink_ref))
            p = ex * inv
            ds = p * (_mm_nt(vv, dos) - dlt)
            sink_term = es * inv * dlt
            dsink = jnp.zeros((8, 128), F32)
            for b in range(SWA_HEADS):
                dsh = -jnp.sum(sink_term[:, 128 * b:128 * (b + 1)], axis=1, keepdims=True)
                dsink = dsink + jnp.where(lane_s == _swa_head(b), dsh, 0.0)
            for j, dq in enumerate(_swa_unstack(_mm_tn(ds, kk))):
                for a in range(4):
                    cols = slice(128 * (4 * j + a), 128 * (4 * j + a + 1))
                    dq_ref[:, cols] = (dq[128 * a:128 * (a + 1)] * 0.125).astype(dq_ref.dtype)
            contrib = jnp.concatenate([_mm(ds, qs), _mm(p, dos)], axis=1)
            dkv_ref[...] = (carry_ref[...] + contrib[0:BLOCK]).astype(dkv_ref.dtype)
            carry_ref[...] = contrib[BLOCK:2 * BLOCK]
            dsink_ref[...] += dsink

        @pl.when(n == nb)
        def _():
            dkv_ref[...] = carry_ref[...].astype(dkv_ref.dtype)
            if na:
                ex.wait()

    cur = lambda n: (jnp.minimum(n, nb - 1), 0)
    outs = pl.pallas_call(
        body, name="swa_bwd_scatter" if na else "swa_bwd", grid=(nb + 1,),
        in_specs=[pl.BlockSpec(memory_space=pltpu.SMEM),
                  pl.BlockSpec((BLOCK, 1024), cur),
                  pl.BlockSpec((BLOCK, 256), lambda n: (jnp.clip(n - 1, 0, nb - 1), KVA // 256)),
                  pl.BlockSpec((BLOCK, 256), lambda n: (jnp.minimum(n, nb - 1), KVA // 256)),
                  pl.BlockSpec((BLOCK, 1024), cur), pl.BlockSpec((BLOCK, 1024), cur)]
        + [HBM_SPEC] * na,
        out_specs=[pl.BlockSpec((BLOCK, 1024), cur),
                   pl.BlockSpec((BLOCK, 256), lambda n: (jnp.maximum(n - 1, 0), 0)),
                   pl.BlockSpec((8, 128), lambda n: (0, 0))] + [HBM_SPEC] * na,
        out_shape=[_sds((s, 1024), MXU_DTYPE), _sds((s, 256), MXU_DTYPE), _sds((8, 128), F32)]
        + _exchange_shapes(blocks, False),
        scratch_shapes=[pltpu.VMEM((BLOCK, 256), F32), pltpu.VMEM((2 * BLOCK, SWA_STACK), F32)]
        + (_Exchange.semaphores(na) if na else []),
        compiler_params=_params(1),
    )(sinks, proj, proj, proj, do, o, *blocks)
    return outs[0], outs[1], outs[2], list(outs[3:])


def _mla_bwd(qh, kh, vh, do, dlt, lse, blocks):
    s = qh.shape[1]
    t = min(MLA_TILE, s)
    nq = s // t

    na = len(blocks)

    def body(*refs):
        q_ref, k_ref, v_ref, do_ref, dlt_ref, lse_ref = refs[:6]
        dq_ref, dk_ref, dv_ref = refs[6 + na:9 + na]
        h, j = pl.program_id(0), pl.program_id(1)
        if na:
            ex = _Exchange(refs[6:6 + na], refs[9 + na:9 + 2 * na], refs[9 + 2 * na:], gather=False)

            @pl.when((h == 0) & (j == 0))
            def _():
                ex.start()

        def step(start, width, diagonal=False):
            rows = pl.ds(pl.multiple_of(start, t), width)
            q, k, dout = q_ref[rows, :], k_ref[...], do_ref[rows, :]
            sc = _scores_t(k, q, t, diagonal)
            p = jnp.exp2(sc * MLA_C2 - lse_ref[0:1, rows])
            dv = _mm(p, dout)
            ds = p * (_mm_nt(v_ref[...], dout) - dlt_ref[0:1, rows])
            dk = _mm(ds, q)
            dq = _mm_tn(ds, k)
            return rows, dq, dk, dv

        rows, dq, dk, dv = step(j * t, t, diagonal=True)
        dk_ref[...] = dk
        dv_ref[...] = dv

        @pl.when(j == 0)
        def _():
            dq_ref[rows, :] = dq * MLA_SCALE

        @pl.when(j > 0)
        def _():
            dq_ref[rows, :] = (dq_ref[rows, :] + dq) * MLA_SCALE

        def above_diagonal(start, width):
            rows, dq, dk, dv = step(start, width)
            dk_ref[...] += dk
            dv_ref[...] += dv

            @pl.when(j == 0)
            def _():
                dq_ref[rows, :] = dq

            @pl.when(j > 0)
            def _():
                dq_ref[rows, :] += dq

        n_above = nq - 1 - j

        def quad(jj, carry):
            above_diagonal((j + 1 + 4 * jj) * t, 4 * t)
            return carry

        lax.fori_loop(0, n_above // 4, quad, 0)

        @pl.when(n_above % 4 >= 2)
        def _():
            above_diagonal((j + 1 + 4 * (n_above // 4)) * t, 2 * t)

        @pl.when(n_above % 2 == 1)
        def _():
            above_diagonal((nq - 1) * t, t)

        dk_ref[...] *= MLA_SCALE

        if na:
            @pl.when((h == MLA_HEADS - 1) & (j == nq - 1))
            def _():
                ex.wait()

    head = lambda h, j: (h, 0, 0)
    kv_map = lambda h, j: (h, j, 0)
    outs = pl.pallas_call(
        body, name="mla_bwd_scatter" if na else "mla_bwd", grid=(MLA_HEADS, nq),
        in_specs=[pl.BlockSpec((None, s, 256), head), pl.BlockSpec((None, t, 256), kv_map),
                  pl.BlockSpec((None, t, 128), kv_map),
                  pl.BlockSpec((s, 128), lambda h, j: (0, h)),
                  pl.BlockSpec((None, 8, s), head), pl.BlockSpec((None, 8, s), head)]
        + [HBM_SPEC] * na,
        out_specs=[pl.BlockSpec((None, s, 256), head),
                   pl.BlockSpec((None, t, 256), kv_map), pl.BlockSpec((None, t, 128), kv_map)]
        + [HBM_SPEC] * na,
        out_shape=[_sds((MLA_HEADS, s, 256), F32), _sds((MLA_HEADS, s, 256), F32),
                   _sds((MLA_HEADS, s, 128), F32)] + _exchange_shapes(blocks, False),
        scratch_shapes=_Exchange.semaphores(na) if na else [],
        compiler_params=_params(2),
    )(qh, kh, vh, do, dlt, lse, *blocks)
    return outs[0], outs[1], outs[2], list(outs[3:])


def _mla_prep_bwd(dqh, dkh, dvh, proj, gq, gkv, w_q, w_kv, tabs):
    s = proj.shape[0]
    tm = min(512, s)
    nm = s // tm
    c, s1, s2 = tabs

    def body(dq_ref, dk_ref, dv_ref, p_ref, gq_ref, gkv_ref, wq_ref, wkv_ref,
             c_ref, s1_ref, s2_ref, dp_ref, dwq_ref, dwkv_ref, dgq_ref, dgkv_ref,
             dqf_ref, dkvf_ref, dwq_acc, dwkv_acc):
        @pl.when(pl.program_id(0) == 0)
        def _():
            dgq_ref[...] = jnp.zeros(dgq_ref.shape, F32)
            dgkv_ref[...] = jnp.zeros(dgkv_ref.shape, F32)
            dwq_acc[...] = jnp.zeros(dwq_acc.shape, F32)
            dwkv_acc[...] = jnp.zeros(dwkv_acc.shape, F32)
        cc, ns1, ns2 = c_ref[...], -s1_ref[...], -s2_ref[...]
        dkr = jnp.zeros((tm, 128), F32)
        for h in range(MLA_HEADS):
            dqf_ref[:, 128 * h:128 * (h + 1)] = dq_ref[h, :, 0:128].astype(dqf_ref.dtype)
            dqf_ref[:, 1024 + 128 * h:1024 + 128 * (h + 1)] = _rope(
                dq_ref[h, :, 128:256], cc, ns1, ns2).astype(dqf_ref.dtype)
            dkvf_ref[:, 128 * h:128 * (h + 1)] = dk_ref[h, :, 0:128].astype(dkvf_ref.dtype)
            dkvf_ref[:, 1024 + 128 * h:1024 + 128 * (h + 1)] = dv_ref[h].astype(dkvf_ref.dtype)
            dkr = dkr + dk_ref[h, :, 128:256]
        dcqn = _mm_nt(dqf_ref[...], wq_ref[...])
        dckvn = _mm_nt(dkvf_ref[...], wkv_ref[...])
        cqh, rq = _rownorm(p_ref[:, 0:384].astype(F32))
        ckvh, rkv = _rownorm(p_ref[:, 384:640].astype(F32))
        dgq_ref[...] += jnp.sum(dcqn * cqh, axis=0, keepdims=True)
        dgkv_ref[...] += jnp.sum(dckvn * ckvh, axis=0, keepdims=True)
        dp_ref[:, 0:384] = _rownorm_bwd(dcqn * gq_ref[...], cqh, rq).astype(dp_ref.dtype)
        dp_ref[:, 384:640] = _rownorm_bwd(dckvn * gkv_ref[...], ckvh, rkv).astype(dp_ref.dtype)
        dp_ref[:, 640:768] = _rope(dkr, cc, ns1, ns2).astype(dp_ref.dtype)
        dwq_acc[...] += _mm_tn(cqh * gq_ref[...], dqf_ref[...])
        dwkv_acc[...] += _mm_tn(ckvh * gkv_ref[...], dkvf_ref[...])

        @pl.when(pl.program_id(0) == nm - 1)
        def _():
            dwq_ref[...] = dwq_acc[...].astype(dwq_ref.dtype)
            dwkv_ref[...] = dwkv_acc[...].astype(dwkv_ref.dtype)

    row = lambda i: (i, 0)
    fixed = lambda i: (0, 0)
    head = lambda i: (0, i, 0)
    return pl.pallas_call(
        body, name="mla_prep_bwd", grid=(nm,),
        in_specs=[pl.BlockSpec((MLA_HEADS, tm, 256), head), pl.BlockSpec((MLA_HEADS, tm, 256), head),
                  pl.BlockSpec((MLA_HEADS, tm, 128), head),
                  pl.BlockSpec((tm, 768), lambda i: (i, CQ // 768)),
                  pl.BlockSpec((1, Q_RANK), fixed), pl.BlockSpec((1, KV_RANK), fixed),
                  pl.BlockSpec((Q_RANK, 2048), fixed), pl.BlockSpec((KV_RANK, 2048), fixed),
                  pl.BlockSpec((tm, 128), row), pl.BlockSpec((tm, 128), row),
                  pl.BlockSpec((tm, 128), row)],
        out_specs=[pl.BlockSpec((tm, 768), row), pl.BlockSpec((Q_RANK, 2048), fixed),
                   pl.BlockSpec((KV_RANK, 2048), fixed),
                   pl.BlockSpec((1, Q_RANK), fixed), pl.BlockSpec((1, KV_RANK), fixed)],
        out_shape=[_sds((s, 768), MXU_DTYPE), _sds((Q_RANK, 2048), GRAD_DTYPE),
                   _sds((KV_RANK, 2048), GRAD_DTYPE),
                   _sds((1, Q_RANK), F32), _sds((1, KV_RANK), F32)],
        scratch_shapes=[pltpu.VMEM((tm, 2048), MXU_DTYPE), pltpu.VMEM((tm, 2048), MXU_DTYPE),
                        pltpu.VMEM((Q_RANK, 2048), F32), pltpu.VMEM((KV_RANK, 2048), F32)],
        compiler_params=_params(1),
    )(dqh, dkh, dvh, proj, gq, gkv, w_q, w_kv, c, s1, s2)


def _in_proj_bwd(dqa, dgate, dlat, dkva, w, x, dx_out, g, blocks):
    s = x.shape[0]
    tm = min(256, s)
    nm = s // tm
    na = len(blocks)

    def body(*refs):
        dqa_ref, dg8_ref, dlat_ref, dkva_ref, w_ref, x_ref, dxo_ref, g_ref = refs[:8]
        dx_ref, dxb_ref, dg_ref = refs[8 + na:11 + na]
        if na:
            ex = _Exchange(refs[8:8 + na], refs[11 + na:11 + 2 * na], refs[11 + 2 * na:], gather=False)

        @pl.when(pl.program_id(0) == 0)
        def _():
            dg_ref[...] = jnp.zeros(dg_ref.shape, F32)
            if na:
                ex.start()

        dh = (_mm_nt(dqa_ref[...], w_ref[:, QA:QA + 1024])
              + _mm_nt(dg8_ref[:, 0:1024], w_ref[:, GA:GA + 1024])
              + _mm_nt(dg8_ref[:, 1024:2048], w_ref[:, GB:GB + 1024])
              + _mm_nt(dlat_ref[...], w_ref[:, CQ:CQ + 768])
              + _mm_nt(dkva_ref[...], w_ref[:, KVA:KVA + 256]))
        xh, r = _rownorm(x_ref[...])
        dg_ref[...] += jnp.sum(dh * xh, axis=0, keepdims=True)
        dx = dxo_ref[...] + _rownorm_bwd(dh * g_ref[...], xh, r)
        dx_ref[...] = dx
        dxb_ref[...] = dx.astype(dxb_ref.dtype)

        if na:
            @pl.when(pl.program_id(0) == nm - 1)
            def _():
                ex.wait()

    row = lambda i: (i, 0)
    fixed = lambda i: (0, 0)
    outs = pl.pallas_call(
        body, name="in_proj_bwd_scatter" if na else "in_proj_bwd", grid=(nm,),
        in_specs=[pl.BlockSpec((tm, 1024), row), pl.BlockSpec((tm, 2048), row),
                  pl.BlockSpec((tm, 768), row), pl.BlockSpec((tm, 256), row),
                  pl.BlockSpec((D_MODEL, NP), fixed, pipeline_mode=pl.Buffered(1)),
                  pl.BlockSpec((tm, D_MODEL), row), pl.BlockSpec((tm, D_MODEL), row),
                  pl.BlockSpec((1, D_MODEL), fixed)] + [HBM_SPEC] * na,
        out_specs=[pl.BlockSpec((tm, D_MODEL), row), pl.BlockSpec((tm, D_MODEL), row),
                   pl.BlockSpec((1, D_MODEL), fixed)] + [HBM_SPEC] * na,
        out_shape=[_sds((s, D_MODEL), F32), _sds((s, D_MODEL), MXU_DTYPE), _sds((1, D_MODEL), F32)]
        + _exchange_shapes(blocks, False),
        scratch_shapes=_Exchange.semaphores(na) if na else [],
        compiler_params=_params(1),
    )(dqa, dgate, dlat, dkva, w, x, dx_out, g, *blocks)
    return outs[0], outs[1], outs[2], list(outs[3:])


def _adamw_update(g, w_ref, m_ref, v_ref, g_ref, d_ref, nm_ref, nv_ref):
    m2 = ADAM_B1 * m_ref[...] + (1.0 - ADAM_B1) * g
    v2 = ADAM_B2 * v_ref[...] + (1.0 - ADAM_B2) * (g * g)
    m_hat = m2 / (1.0 - ADAM_B1 ** ADAM_STEP)
    v_hat = v2 / (1.0 - ADAM_B2 ** ADAM_STEP)
    g_ref[...] = g
    d_ref[...] = -ADAM_LR * (m_hat / (jnp.sqrt(v_hat) + ADAM_EPS) + ADAM_WD * w_ref[...])
    nm_ref[...] = m2
    nv_ref[...] = v2


def _reduce_adamw_in(parts, w_t, m_t, v_t, name):
    n_layers = len(parts)
    tc = 512
    nc = D_MODEL // tc

    def body(*refs):
        p_refs = refs[:n_layers]
        w_ref, m_ref, v_ref, g_ref, d_ref, nm_ref, nv_ref = refs[n_layers:]
        layer = pl.program_id(0)
        for l in range(n_layers):
            @pl.when(layer == l)
            def _(l=l):
                g = p_refs[l][0].astype(F32)
                for k in range(1, N_DEV):
                    g = g + p_refs[l][k].astype(F32)
                _adamw_update(g[0:SHARD_COLS], w_ref, m_ref, v_ref, g_ref, d_ref, nm_ref, nv_ref)

    def part_spec(l):
        return pl.BlockSpec((N_DEV, SHARD_PAD, tc),
                            lambda layer, j: (0, 0, jnp.where(layer == l, j, 0)))

    blk = pl.BlockSpec((None, SHARD_COLS, tc), lambda layer, j: (layer, 0, j))
    return pl.pallas_call(
        body, name=name, grid=(n_layers, nc),
        in_specs=[part_spec(l) for l in range(n_layers)] + [blk, blk, blk],
        out_specs=[blk, blk, blk, blk],
        out_shape=[_sds((n_layers, SHARD_COLS, D_MODEL), F32)] * 4,
        compiler_params=_params(2),
    )(*parts, w_t, m_t, v_t)


def _reduce_adamw(parts, w, m, v, name):
    n_layers = len(parts)
    rows, part_cols = parts[0].shape[1:]
    cols = w.shape[-1]
    lanes = -(-cols // 128) * 128
    tr = rows
    for cand in (1024, 512, 256, 128, 64, 32, 16, 8):
        if rows % cand == 0 and N_DEV * cand * lanes * 4 <= 8 * 1024 * 1024:
            tr = cand
            break
    nr = rows // tr

    def body(*refs):
        p_refs = refs[:n_layers]
        w_ref, m_ref, v_ref, g_ref, d_ref, nm_ref, nv_ref = refs[n_layers:]
        layer = pl.program_id(0)
        for l in range(n_layers):
            @pl.when(layer == l)
            def _(l=l):
                g = p_refs[l][0, :, 0:cols].astype(F32)
                for k in range(1, N_DEV):
                    g = g + p_refs[l][k, :, 0:cols].astype(F32)
                _adamw_update(g, w_ref, m_ref, v_ref, g_ref, d_ref, nm_ref, nv_ref)

    def part_spec(l):
        return pl.BlockSpec((N_DEV, tr, part_cols),
                            lambda layer, i: (0, jnp.where(layer == l, i, 0), 0))

    blk = pl.BlockSpec((tr, cols), lambda layer, i: (layer * nr + i, 0))
    return pl.pallas_call(
        body, name=name, grid=(n_layers, nr),
        in_specs=[part_spec(l) for l in range(n_layers)] + [blk, blk, blk],
        out_specs=[blk, blk, blk, blk],
        out_shape=[_sds((n_layers * rows, cols), F32)] * 4,
        compiler_params=_params(2),
    )(*parts, w, m, v)


def _position():
    x, y, c = lax.axis_index("x"), lax.axis_index("y"), lax.axis_index("c")
    return x, y, c


def _index(px, py, pc):
    return 4 * px + 2 * py + pc


HBM_SPEC = pl.BlockSpec(memory_space=pltpu.HBM)
IN_BLOCKED = (D_MODEL, N_DEV * SHARD_PAD)
IN_BLOCKED_T = (N_DEV * SHARD_PAD, D_MODEL)


def _block(ref, idx):
    if tuple(ref.shape) == IN_BLOCKED:
        return ref.at[:, pl.ds(pl.multiple_of(idx * SHARD_PAD, SHARD_PAD), SHARD_PAD)]
    if tuple(ref.shape) == IN_BLOCKED_T:
        return ref.at[pl.ds(pl.multiple_of(idx * SHARD_PAD, SHARD_PAD), SHARD_PAD)]
    return ref.at[idx]


class _Gather:
    def __init__(self, srcs, dsts, sems):
        self.srcs, self.dsts = srcs, dsts
        self.send_sems, self.recv_sems, self.local_sems = sems
        x, y, c = _position()
        self.c = c
        self.me, self.sibling = (x, y, c), (x, y, 1 - c)
        self.chips = [(1 - x, y), (x, 1 - y), (1 - x, 1 - y)]

    def _copy(self, a, k, block, to, own=False):
        slot = _block(self.dsts[a], _index(*block))
        return pltpu.make_async_remote_copy(
            src_ref=self.srcs[a] if own else slot, dst_ref=slot,
            send_sem=self.send_sems.at[7 * a + k], recv_sem=self.recv_sems.at[7 * a + k],
            device_id=to, device_id_type=MESH)

    def _local(self, a):
        return pltpu.make_async_copy(self.srcs[a], _block(self.dsts[a], _index(*self.me)),
                                     self.local_sems.at[a])

    def _first(self, a):
        return [self._copy(a, 0, self.me, self.sibling, own=True)] + [
            self._copy(a, 1 + j, self.me, (*chip, self.c), own=True)
            for j, chip in enumerate(self.chips)]

    def _passed(self, a, j):
        return self._copy(a, 4 + j, (*self.chips[j], self.c), self.sibling)

    def start(self):
        for a in range(len(self.srcs)):
            self._local(a).start()
            for cp in self._first(a):
                cp.start()

    def forward(self):
        for j, chip in enumerate(self.chips):
            for a in range(len(self.srcs)):
                self._copy(a, 1 + j, (*chip, self.c), self.me).wait_recv()
                self._passed(a, j).start()

    def finish(self):
        for a in range(len(self.srcs)):
            self._copy(a, 0, self.sibling, self.me).wait_recv()
            for j, chip in enumerate(self.chips):
                self._copy(a, 4 + j, (*chip, 1 - self.c), self.me).wait_recv()
            for cp in self._first(a):
                cp.wait_send()
            for j in range(3):
                self._passed(a, j).wait_send()
            self._local(a).wait()


def _all_gather(shards, name):
    na = len(shards)

    def body(*refs):
        g = _Gather(refs[:na], refs[na:2 * na], refs[2 * na:])
        g.start()
        g.forward()
        g.finish()

    return pl.pallas_call(
        body, name=name,
        in_specs=[HBM_SPEC] * na, out_specs=[HBM_SPEC] * na,
        out_shape=_exchange_shapes(shards, True),
        scratch_shapes=_Exchange.semaphores(na),
    )(*shards)


class _Exchange:
    def __init__(self, srcs, dsts, sems, gather):
        self.srcs, self.dsts, self.gather = srcs, dsts, gather
        self.send_sems, self.recv_sems, self.local_sems = sems
        x, y, c = _position()
        self.me = _index(x, y, c)
        self.peers = [(x ^ ((k >> 2) & 1), y ^ ((k >> 1) & 1), c ^ (k & 1)) for k in range(1, N_DEV)]

    @staticmethod
    def semaphores(na):
        return [pltpu.SemaphoreType.DMA((7 * na,)), pltpu.SemaphoreType.DMA((7 * na,)),
                pltpu.SemaphoreType.DMA((na,))]

    def _src(self, a, slot):
        return self.srcs[a] if self.gather else _block(self.srcs[a], slot)

    def _local(self, a):
        return pltpu.make_async_copy(self._src(a, self.me), self.dsts[a].at[self.me],
                                     self.local_sems.at[a])

    def _send(self, a, k):
        peer = self.peers[k]
        return pltpu.make_async_remote_copy(
            src_ref=self._src(a, _index(*peer)), dst_ref=self.dsts[a].at[self.me],
            send_sem=self.send_sems.at[7 * a + k], recv_sem=self.recv_sems.at[7 * a + k],
            device_id=peer, device_id_type=MESH)

    def _arrival(self, a, k):
        landed = self.dsts[a].at[_index(*self.peers[k])]
        return pltpu.make_async_remote_copy(
            src_ref=landed, dst_ref=landed,
            send_sem=self.send_sems.at[7 * a + k], recv_sem=self.recv_sems.at[7 * a + k],
            device_id=self.peers[k], device_id_type=MESH)

    def start(self):
        for a in range(len(self.srcs)):
            self._local(a).start()
            for k in range(N_DEV - 1):
                self._send(a, k).start()

    def wait(self):
        for a in range(len(self.srcs)):
            for k in range(N_DEV - 1):
                self._arrival(a, k).wait_recv()
            for k in range(N_DEV - 1):
                self._send(a, k).wait_send()
            self._local(a).wait()


def _exchange_shapes(arrays, gather):
    def shape(a):
        if gather:
            return IN_BLOCKED if a.shape == (D_MODEL, SHARD_PAD) else (N_DEV,) + a.shape
        return (N_DEV, SHARD_PAD, D_MODEL) if a.shape == IN_BLOCKED_T else a.shape
    return [_sds(shape(a), a.dtype) for a in arrays]


def _exchange_call(arrays, gather, name):
    na = len(arrays)

    def body(*refs):
        ex = _Exchange(refs[:na], refs[na:2 * na], refs[2 * na:], gather)
        ex.start()
        ex.wait()

    return pl.pallas_call(
        body, name=name,
        in_specs=[HBM_SPEC] * na, out_specs=[HBM_SPEC] * na,
        out_shape=_exchange_shapes(arrays, gather),
        scratch_shapes=_Exchange.semaphores(na),
    )(*arrays)


def _layer_fwd(x, small, w_in, g_rest, rest_shards, tabs, next_shards):
    attn_g, sinks, gq, gkv = small
    proj, h, gathered_rest = _in_proj(x, attn_g, w_in, rest_shards)
    w_q, w_kv, w_o = _rest_from_gathered(*(gathered_rest if rest_shards else g_rest))
    swa = _swa_fwd(proj, sinks)
    qh, kh, vh, vth = _mla_prep(proj, gq, gkv, w_q, w_kv, tabs)
    mla, lse, gathered = _mla_fwd(qh, kh, vth, next_shards)
    x_next, y = _out_proj(x, proj, swa, mla, w_o)
    return x_next, (x, proj, h, swa, qh, kh, vh, mla, lse, y), (w_in, w_q, w_kv, w_o), gathered


def _layer_bwd(dx, dxb, saved, small, weights, tabs, pending, send_own):
    attn_g, sinks, gq, gkv = small
    w_in, w_q, w_kv, w_o = weights
    x, proj, h, swa, qh, kh, vh, mla, lse, y = saved
    d_o = _matmul_tn(y, dxb, "grad_w_out")
    o_block = d_o.reshape(N_DEV, 256, D_MODEL)
    do_a, do_b, dgate, dlt = _out_proj_bwd(dxb, proj, swa, mla, w_o)
    dqa, dkva, dsk, got_o = _swa_bwd(proj, sinks, do_a, swa, [o_block] if send_own else [])
    dqh, dkh, dvh, received = _mla_bwd(qh, kh, vh, do_b, dlt, lse, pending)
    dlat, d_q, d_kv, d_gq, d_gkv = _mla_prep_bwd(dqh, dkh, dvh, proj, gq, gkv, w_q, w_kv, tabs)
    qkv_blocks = _qkv_grad_blocks(d_q, d_kv)
    d_inp, got_qkv = _grad_w_in(h, dqa, dgate, dlat, dkva, qkv_blocks if send_own else [])
    in_block = _in_grad_blocks(d_inp)
    dx, dxb, d_attn, got_in = _in_proj_bwd(dqa, dgate, dlat, dkva, w_in, x, dx, attn_g,
                                           [in_block] if send_own else [])
    small_grads = (d_attn, dsk[0:1, 0:SWA_HEADS], d_gq, d_gkv)
    blocks = [in_block] + qkv_blocks + [o_block]
    return dx, dxb, small_grads, blocks, received, got_in + got_qkv + got_o


def _pack_small_grads(small_grads, d_final, loss):
    d_attn, d_sink, d_gq, d_gkv = zip(*small_grads)
    return jnp.concatenate([
        jnp.concatenate(d_attn, axis=0).reshape(64, 128),
        jnp.concatenate(d_gq, axis=0).reshape(12, 128),
        jnp.concatenate(d_gkv, axis=0).reshape(8, 128),
        d_final.reshape(16, 128),
        jnp.pad(jnp.concatenate(d_sink, axis=1), ((0, 0), (0, 64))),
        loss[0:1],
        jnp.zeros((PACK_ROWS - ROW_LOSS - 1, 128), F32)], axis=0)


def _pack_small(attn, qa, kva, final, sinks):
    return jnp.concatenate([
        attn.reshape(64, 128), qa.reshape(12, 128), kva.reshape(8, 128), final.reshape(16, 128),
        jnp.pad(sinks.reshape(1, 64), ((0, 0), (0, 64))),
        jnp.zeros((PACK_ROWS - ROW_SINK - 1, 128), F32)], axis=0)


def _unpack_small(p):
    return (p[ROW_ATTN:ROW_QA].reshape(DEPTH, D_MODEL), p[ROW_SINK, 0:64].reshape(DEPTH, SWA_HEADS),
            p[ROW_QA:ROW_KVA].reshape(DEPTH, Q_RANK), p[ROW_KVA:ROW_FINAL].reshape(DEPTH, KV_RANK),
            p[ROW_FINAL:ROW_SINK].reshape(D_MODEL))


def kernel(x, attn_norm_g, w_in, swa_sinks, q_a_norm_g, kv_a_norm_g, w_q_b, w_kv_b, w_out, final_norm_g, loss_target, m_attn_norm_g, m_w_in, m_swa_sinks, m_q_a_norm_g, m_kv_a_norm_g, m_w_q_b, m_w_kv_b, m_w_out, m_final_norm_g, v_attn_norm_g, v_w_in, v_swa_sinks, v_q_a_norm_g, v_kv_a_norm_g, v_w_q_b, v_w_kv_b, v_w_out, v_final_norm_g):
    xs, tgt = x[0], loss_target[0]
    tabs = _rope_tables(xs.shape[0])
    shards = [w.astype(MXU_DTYPE) for w in (w_in, w_q_b, w_kv_b, w_out)]
    shards[0] = jnp.pad(shards[0], ((0, 0), (0, 0), (0, SHARD_PAD - SHARD_COLS)))
    layer_shards = lambda l: [w[l] for w in shards]
    smalls = [(attn_norm_g[l:l + 1], swa_sinks[l], q_a_norm_g[l:l + 1], kv_a_norm_g[l:l + 1])
              for l in range(DEPTH)]

    gathered = list(_all_gather(layer_shards(0)[:1], "gather_weights")) + [None] * 3
    weights, saved = [None] * DEPTH, []
    for l in range(DEPTH):
        next_shards = layer_shards(l + 1) if l + 1 < DEPTH else []
        rest_shards = layer_shards(0)[1:] if l == 0 else []
        xs, acts, weights[l], gathered = _layer_fwd(
            xs, smalls[l], _w_in_from_gathered(gathered[0]), gathered[1:], rest_shards, tabs,
            next_shards)
        saved.append(acts)
    dx, dxb, d_final, loss = _final_loss(xs, tgt, final_norm_g.reshape(1, D_MODEL))

    received, small_grads, pending = [None] * DEPTH, [None] * DEPTH, []
    for l in reversed(range(DEPTH)):
        dx, dxb, small_grads[l], blocks, arrived, arrived_own = _layer_bwd(
            dx, dxb, saved[l], smalls[l], weights[l], tabs, pending, send_own=(l == 0))
        if pending:
            received[l + 1] = arrived
        pending = blocks
    received[0] = arrived_own
    small = _exchange_call([_pack_small_grads(small_grads, d_final, loss)], True, "gather_small")[0]

    big = []
    for a, (w, m, v, name) in enumerate(zip((w_in, w_q_b, w_kv_b, w_out),
                                            (m_w_in, m_w_q_b, m_w_kv_b, m_w_out),
                                            (v_w_in, v_w_q_b, v_w_kv_b, v_w_out),
                                            ("adamw_w_in", "adamw_w_q_b", "adamw_w_kv_b",
                                             "adamw_w_out"))):
        parts = [received[l][a] for l in range(DEPTH)]
        if a == 0:
            swap = lambda t: t.transpose(0, 2, 1)
            outs = _reduce_adamw_in(parts, swap(w), swap(m), swap(v), name)
            big.append([swap(t) for t in outs])
        else:
            cols = w.shape[-1]
            flat = lambda t: t.reshape(-1, cols)
            outs = _reduce_adamw(parts, flat(w), flat(m), flat(v), name)
            big.append([t.reshape(w.shape) for t in outs])

    sm = _reduce_adamw(
        [small],
        _pack_small(attn_norm_g, q_a_norm_g, kv_a_norm_g, final_norm_g, swa_sinks),
        _pack_small(m_attn_norm_g, m_q_a_norm_g, m_kv_a_norm_g, m_final_norm_g, m_swa_sinks),
        _pack_small(v_attn_norm_g, v_q_a_norm_g, v_kv_a_norm_g, v_final_norm_g, v_swa_sinks),
        "adamw_small")
    loss = sm[0][ROW_LOSS, 0]
    kinds = []
    for t in range(4):
        attn, sinks, qa, kva, final = _unpack_small(sm[t])
        b_in, b_q, b_kv, b_o = (big[i][t] for i in range(4))
        kinds.append((attn, b_in, sinks, qa, kva, b_q, b_kv, b_o, final))
    return (loss, dx[None], *kinds[0], *kinds[1], *kinds[2], *kinds[3])
```

```python
import jax
import jax.numpy as jnp
import numpy as np
from jax import lax
from jax.experimental import pallas as pl
from jax.experimental.pallas import tpu as pltpu

F32 = jnp.float32
BF16 = jnp.bfloat16
MXU_DTYPE = BF16
GRAD_DTYPE = BF16
PROJ_DTYPE = BF16
ATTN_DTYPE = BF16

D_MODEL = 2048
DEPTH = 4
EPS = 1e-6
NEG = -1e30
BLOCK = 128
SWA_HEADS = 16
MLA_HEADS = 8
Q_RANK = 384
KV_RANK = 256
MLA_SCALE = 192 ** -0.5
MLA_C2 = MLA_SCALE * 1.4426950408889634
ROPE_THETA = 10000.0
IN_WIDTH = 4032

ADAM_LR = 0.001
ADAM_B1 = 0.9
ADAM_B2 = 0.999
ADAM_EPS = 1e-08
ADAM_WD = 0.01
ADAM_STEP = 10

N_DEV = 8
MESH = pl.DeviceIdType.MESH

NP = 4096
QA, GA, GB, CQ, CKV, KR, KVA = 0, 1024, 2048, 3072, 3456, 3712, 3840

ROW_ATTN, ROW_QA, ROW_KVA, ROW_FINAL, ROW_SINK, ROW_LOSS, PACK_ROWS = 0, 64, 76, 84, 100, 101, 104

VMEM_LIMIT = 56 * 1024 * 1024
STREAM_DEPTH = 3
MLA_TILE = 512
GATHER_FORWARD_HEAD = 7


def _sds(shape, dtype):
    return jax.ShapeDtypeStruct(shape, dtype)


def _params(n_axes):
    return pltpu.CompilerParams(dimension_semantics=("arbitrary",) * n_axes,
                                vmem_limit_bytes=VMEM_LIMIT)


def _mm(a, b):
    return jnp.dot(a.astype(MXU_DTYPE), b.astype(MXU_DTYPE), preferred_element_type=F32)


def _mm_nt(a, b):
    return lax.dot_general(a.astype(MXU_DTYPE), b.astype(MXU_DTYPE),
                           (((1,), (1,)), ((), ())), preferred_element_type=F32)


def _mm_tn(a, b):
    return lax.dot_general(a.astype(MXU_DTYPE), b.astype(MXU_DTYPE),
                           (((0,), (0,)), ((), ())), preferred_element_type=F32)


def _rownorm(x):
    r = lax.rsqrt(jnp.mean(x * x, axis=-1, keepdims=True) + EPS)
    return x * r, r


def _rownorm_bwd(dxh, xh, r):
    return r * (dxh - xh * jnp.mean(dxh * xh, axis=-1, keepdims=True))


def _rope(t, c, s1, s2):
    return t * c + pltpu.roll(t, 32, 1) * s1 + pltpu.roll(t, 96, 1) * s2


SHARD_COLS = IN_WIDTH // N_DEV
SHARD_PAD = 512


def _orig_col_of_padded():
    o = np.full((NP,), -1, np.int64)
    for start, width, orig in ((QA, 1024, 0), (KVA, 256, 1024), (GA, 1024, 1280), (CQ, 384, 2304),
                               (CKV, 256, 2688), (KR, 64, 2944), (GB, 1024, 3008)):
        o[start:start + width] = np.arange(orig, orig + width)
    return o


def _device_major_src():
    o = _orig_col_of_padded()
    return np.where(o >= 0, o + (SHARD_PAD - SHARD_COLS) * (o // SHARD_COLS), -1)


def _kernel_layout_src():
    o = _orig_col_of_padded()
    where = np.full((IN_WIDTH,), -1, np.int64)
    where[o[o >= 0]] = np.nonzero(o >= 0)[0]
    e = np.arange(N_DEV * SHARD_PAD)
    k, c = e // SHARD_PAD, e % SHARD_PAD
    return np.where(c < SHARD_COLS, where[np.minimum(SHARD_COLS * k + c, IN_WIDTH - 1)], -1)


def _permute_columns(x, src_of, name, transposed=False):
    rows, n_in = x.shape
    n_out = len(src_of)
    plan, mats = [], []
    for t in range(n_out // 128):
        srcs = src_of[128 * t:128 * (t + 1)]
        entry = []
        for u in sorted(set(int(s) // 128 for s in srcs if s >= 0)):
            m = np.zeros((128, 128), np.float32)
            for c, s in enumerate(srcs):
                if s >= 0 and s // 128 == u:
                    m[s % 128, c] = 1.0
            entry.append((u, len(mats)))
            mats.append(m)
        plan.append(entry)
    tr = min(512, rows)

    def body(x_ref, p_ref, o_ref):
        for t, entry in enumerate(plan):
            acc = jnp.zeros((tr, 128), F32)
            for u, idx in entry:
                acc = acc + jnp.dot(x_ref[:, 128 * u:128 * (u + 1)], p_ref[idx],
                                    preferred_element_type=F32)
            if transposed:
                o_ref[128 * t:128 * (t + 1), :] = acc.T.astype(o_ref.dtype)
            else:
                o_ref[:, 128 * t:128 * (t + 1)] = acc.astype(o_ref.dtype)

    table = jnp.asarray(np.stack(mats), x.dtype)
    return pl.pallas_call(
        body, name=name, grid=(rows // tr,),
        in_specs=[pl.BlockSpec((tr, n_in), lambda i: (i, 0)),
                  pl.BlockSpec(table.shape, lambda i: (0, 0, 0))],
        out_specs=(pl.BlockSpec((n_out, tr), lambda i: (0, i)) if transposed
                   else pl.BlockSpec((tr, n_out), lambda i: (i, 0))),
        out_shape=_sds((n_out, rows) if transposed else (rows, n_out), x.dtype),
        compiler_params=_params(1),
    )(x, table)


def _w_in_from_gathered(g_in):
    return _permute_columns(g_in, _device_major_src(), "w_in_layout")


def _rest_from_gathered(g_qb, g_kvb, g_out):
    qb = g_qb.transpose(1, 0, 2)
    rope = jnp.pad(qb[..., 128:], ((0, 0), (0, 0), (0, 64)))
    w_q = jnp.concatenate([qb[..., :128].reshape(Q_RANK, 1024),
                           rope.reshape(Q_RANK, 1024)], axis=-1)
    kvb = g_kvb.transpose(1, 0, 2)
    w_kv = jnp.concatenate([kvb[..., :128].reshape(KV_RANK, 1024),
                            kvb[..., 128:].reshape(KV_RANK, 1024)], axis=-1)
    w_o = g_out.reshape(D_MODEL, D_MODEL)
    return w_q, w_kv, w_o


def _in_grad_blocks(d_inp):
    return _permute_columns(d_inp, _kernel_layout_src(), "grad_w_in_layout", transposed=True)


def _qkv_grad_blocks(d_q, d_kv):
    qn = d_q[:, :1024].reshape(Q_RANK, 8, 128)
    qr = d_q[:, 1024:].reshape(Q_RANK, 8, 128)[..., :64]
    b_q = jnp.concatenate([qn, qr], axis=-1).transpose(1, 0, 2)
    kn = d_kv[:, :1024].reshape(KV_RANK, 8, 128)
    vv = d_kv[:, 1024:].reshape(KV_RANK, 8, 128)
    b_kv = jnp.concatenate([kn, vv], axis=-1).transpose(1, 0, 2)
    return [b_q, b_kv]


def _rope_tables(s):
    pos = jnp.arange(s, dtype=F32)
    inv_freq = ROPE_THETA ** (-jnp.arange(0, 64, 2, dtype=F32) / 64)
    ang = pos[:, None] * inv_freq[None, :]
    cos, sin = jnp.cos(ang), jnp.sin(ang)
    z32 = jnp.zeros((s, 32), F32)
    z64 = jnp.zeros((s, 64), F32)
    c = jnp.concatenate([cos, cos, z64], axis=1)
    s1 = jnp.concatenate([z32, sin, z64], axis=1)
    s2 = jnp.concatenate([-sin, z32, z64], axis=1)
    return c, s1, s2


def _in_proj(x, g, w, shards):
    s = x.shape[0]
    tm, tn = min(512, s), 1024
    nm = s // tm
    na = len(shards)

    def body(*refs):
        x_ref, g_ref, w_ref = refs[:3]
        o_ref, h_ref = refs[3 + na:5 + na]
        i = pl.program_id(0)
        if na:
            ex = _Gather(refs[3:3 + na], refs[5 + na:5 + 2 * na], refs[5 + 2 * na:])

            @pl.when(i == 0)
            def _():
                ex.start()

            @pl.when(i == (7 * nm) // 8)
            def _():
                ex.forward()

        xh, _ = _rownorm(x_ref[...])
        h_ref[...] = (xh * g_ref[...]).astype(h_ref.dtype)
        for j in range(NP // tn):
            cols = slice(j * tn, (j + 1) * tn)
            o_ref[:, cols] = jnp.dot(h_ref[...], w_ref[:, cols],
                                     preferred_element_type=F32).astype(o_ref.dtype)

        if na:
            @pl.when(i == nm - 1)
            def _():
                ex.finish()

    row = lambda i: (i, 0)
    fixed = lambda i: (0, 0)
    outs = pl.pallas_call(
        body, name="in_proj_gather" if na else "in_proj", grid=(nm,),
        in_specs=[pl.BlockSpec((tm, D_MODEL), row), pl.BlockSpec((1, D_MODEL), fixed),
                  pl.BlockSpec((D_MODEL, NP), fixed, pipeline_mode=pl.Buffered(1))]
        + [HBM_SPEC] * na,
        out_specs=[pl.BlockSpec((tm, NP), row), pl.BlockSpec((tm, D_MODEL), row)]
        + [HBM_SPEC] * na,
        out_shape=[_sds((s, NP), PROJ_DTYPE), _sds((s, D_MODEL), MXU_DTYPE)]
        + _exchange_shapes(shards, True),
        scratch_shapes=_Exchange.semaphores(na) if na else [],
        compiler_params=_params(1),
    )(x, g, w, *shards)
    return outs[0], outs[1], list(outs[2:])


def _swa_slopes():
    return [2.0 ** (-8.0 * (h + 1) / SWA_HEADS) for h in range(SWA_HEADS)]


SWA_STACK = SWA_HEADS * BLOCK


def _swa_head(b):
    j, a = b // 8, b % 8
    return 2 * (4 * j + a % 4) + a // 4


def _swa_kv(kv_p, kv_c):
    return (jnp.concatenate([kv_p[:, :128], kv_c[:, :128]], axis=0),
            jnp.concatenate([kv_p[:, 128:], kv_c[:, 128:]], axis=0))


def _swa_by_block(vals):
    b = lax.broadcasted_iota(jnp.int32, (1, SWA_STACK), 1) >> 7
    row = jnp.full((1, SWA_STACK), vals[-1], F32)
    for t in range(len(vals) - 2, -1, -1):
        row = jnp.where(b == t, vals[t], row)
    return row


def _swa_bias(n):
    slopes = _swa_slopes()
    ki = lax.broadcasted_iota(jnp.int32, (2 * BLOCK, SWA_STACK), 0)
    r = lax.broadcasted_iota(jnp.int32, (2 * BLOCK, SWA_STACK), 1)
    delta = BLOCK + (r & (BLOCK - 1)) - ki
    valid = (delta >= 0) & (delta < BLOCK) & ((n - 1) * BLOCK + ki >= 0)
    slope = _swa_by_block([slopes[_swa_head(b)] for b in range(SWA_HEADS)])
    return jnp.where(valid, -slope * delta.astype(F32), NEG)


def _swa_fill_bias(n, bias_ref):
    @pl.when(n <= 1)
    def _():
        bias_ref[...] = _swa_bias(n)


def _swa_sink_row(sink_ref):
    return _swa_by_block([sink_ref[_swa_head(b)] for b in range(SWA_HEADS)])


def _swa_pairs(ref, j):
    return jnp.concatenate([ref[:, 128 * (4 * j + a):128 * (4 * j + a + 1)] for a in range(4)],
                           axis=0).astype(F32)


def _swa_stack(ref):
    left = lax.broadcasted_iota(jnp.int32, (8 * BLOCK, 128), 1) < 64
    p0, p1 = _swa_pairs(ref, 0), _swa_pairs(ref, 1)
    s0 = jnp.concatenate([p0, pltpu.roll(p0, 64, 1)], axis=0)
    s1 = jnp.concatenate([pltpu.roll(p1, 64, 1), p1], axis=0)
    return jnp.concatenate([jnp.where(left, s0, 0.0), jnp.where(left, 0.0, s1)], axis=0)


def _swa_unstack(t):
    left = lax.broadcasted_iota(jnp.int32, (4 * BLOCK, 128), 1) < 64
    n = 4 * BLOCK
    g0 = jnp.where(left, t[0:n], pltpu.roll(t[n:2 * n], 64, 1))
    g1 = jnp.where(left, pltpu.roll(t[2 * n:3 * n], 64, 1), t[3 * n:4 * n])
    return g0, g1


def _swa_softmax(qs, kk, bias, sink):
    sc = _mm_nt(kk, qs) + bias
    m = jnp.maximum(jnp.max(sc, axis=0, keepdims=True), sink)
    ex = jnp.exp(sc - m)
    es = jnp.exp(sink - m)
    return ex, es, 1.0 / (jnp.sum(ex, axis=0, keepdims=True) + es)


def _swa_fwd(proj, sinks):
    s = proj.shape[0]
    nb = s // BLOCK

    def body(sink_ref, q_ref, kp_ref, kc_ref, o_ref, bias_ref):
        n = pl.program_id(0)
        _swa_fill_bias(n, bias_ref)
        kk, vv = _swa_kv(kp_ref[...].astype(F32), kc_ref[...].astype(F32))
        qs = _swa_stack(q_ref) * 0.125
        ex, _, inv = _swa_softmax(qs, kk, bias_ref[...], _swa_sink_row(sink_ref))
        o_t = _mm(vv.T, ex) * inv
        n4 = 4 * BLOCK
        for j in range(2):
            rows = slice(64 * j, 64 * (j + 1))
            out = jnp.concatenate([o_t[rows, 2 * j * n4:(2 * j + 1) * n4],
                                   o_t[rows, (2 * j + 1) * n4:(2 * j + 2) * n4]], axis=0).T
            for a in range(4):
                o_ref[:, 128 * (4 * j + a):128 * (4 * j + a + 1)] = out[128 * a:128 * (a + 1)].astype(
                    o_ref.dtype)

    return pl.pallas_call(
        body, name="swa_fwd", grid=(nb,),
        in_specs=[pl.BlockSpec(memory_space=pltpu.SMEM),
                  pl.BlockSpec((BLOCK, 1024), lambda n: (n, 0)),
                  pl.BlockSpec((BLOCK, 256), lambda n: (jnp.maximum(n - 1, 0), KVA // 256)),
                  pl.BlockSpec((BLOCK, 256), lambda n: (n, KVA // 256))],
        out_specs=pl.BlockSpec((BLOCK, 1024), lambda n: (n, 0)),
        out_shape=_sds((s, 1024), ATTN_DTYPE),
        scratch_shapes=[pltpu.VMEM((2 * BLOCK, SWA_STACK), F32)],
        compiler_params=_params(1),
    )(sinks, proj, proj, proj)


def _mla_prep(proj, gq, gkv, w_q, w_kv, tabs):
    s = proj.shape[0]
    tm = min(512, s)
    c, s1, s2 = tabs

    def body(p_ref, gq_ref, gkv_ref, wq_ref, wkv_ref, c_ref, s1_ref, s2_ref,
             q_ref, k_ref, v_ref, vt_ref):
        cqh, _ = _rownorm(p_ref[:, 0:384].astype(F32))
        ckvh, _ = _rownorm(p_ref[:, 384:640].astype(F32))
        q = _mm(cqh * gq_ref[...], wq_ref[...])
        kv = _mm(ckvh * gkv_ref[...], wkv_ref[...])
        cc, ss1, ss2 = c_ref[...], s1_ref[...], s2_ref[...]
        krr = _rope(p_ref[:, 640:768].astype(F32), cc, ss1, ss2).astype(k_ref.dtype)
        for h in range(MLA_HEADS):
            q_ref[h, :, 0:128] = q[:, 128 * h:128 * (h + 1)].astype(q_ref.dtype)
            q_ref[h, :, 128:256] = _rope(q[:, 1024 + 128 * h:1024 + 128 * (h + 1)],
                                         cc, ss1, ss2).astype(q_ref.dtype)
            k_ref[h, :, 0:128] = kv[:, 128 * h:128 * (h + 1)].astype(k_ref.dtype)
            k_ref[h, :, 128:256] = krr
            vv = kv[:, 1024 + 128 * h:1024 + 128 * (h + 1)]
            v_ref[h] = vv.astype(v_ref.dtype)
            vt_ref[h, 0:128, :] = vv.T.astype(vt_ref.dtype)
            vt_ref[h, 128:256, :] = jnp.ones((128, tm), vt_ref.dtype)

    row = lambda i: (i, 0)
    fixed = lambda i: (0, 0)
    return pl.pallas_call(
        body, name="mla_prep", grid=(s // tm,),
        in_specs=[pl.BlockSpec((tm, 768), lambda i: (i, CQ // 768)),
                  pl.BlockSpec((1, Q_RANK), fixed), pl.BlockSpec((1, KV_RANK), fixed),
                  pl.BlockSpec((Q_RANK, 2048), fixed), pl.BlockSpec((KV_RANK, 2048), fixed),
                  pl.BlockSpec((tm, 128), row), pl.BlockSpec((tm, 128), row),
                  pl.BlockSpec((tm, 128), row)],
        out_specs=[pl.BlockSpec((MLA_HEADS, tm, 256), lambda i: (0, i, 0)),
                   pl.BlockSpec((MLA_HEADS, tm, 256), lambda i: (0, i, 0)),
                   pl.BlockSpec((MLA_HEADS, tm, 128), lambda i: (0, i, 0)),
                   pl.BlockSpec((MLA_HEADS, 256, tm), lambda i: (0, 0, i))],
        out_shape=[_sds((MLA_HEADS, s, 256), MXU_DTYPE), _sds((MLA_HEADS, s, 256), MXU_DTYPE),
                   _sds((MLA_HEADS, s, 128), MXU_DTYPE), _sds((MLA_HEADS, 256, s), MXU_DTYPE)],
        compiler_params=_params(1),
    )(proj, gq, gkv, w_q, w_kv, c, s1, s2)


def _scores_t(k, q, t, diagonal):
    sc = _mm_nt(k, q)
    if diagonal:
        key = lax.broadcasted_iota(jnp.int32, sc.shape, 0)
        query = lax.broadcasted_iota(jnp.int32, sc.shape, 1)
        sc = jnp.where(key <= query, sc, NEG)
    return sc


def _mla_fwd(qh, kh, vth, shards):
    s = qh.shape[1]
    t = min(MLA_TILE, s)
    nq = s // t
    na = len(shards)

    def body(*refs):
        q_ref, k_ref, vt_ref = refs[:3]
        o_ref, lse_ref = refs[3 + na:5 + na]
        m_ref, acc_ref = refs[5 + 2 * na:7 + 2 * na]
        h, i = pl.program_id(0), pl.program_id(1)
        if na:
            ex = _Gather(refs[3:3 + na], refs[5 + na:5 + 2 * na], refs[7 + 2 * na:])

            @pl.when((h == 0) & (i == 0))
            def _():
                ex.start()

            @pl.when((h == GATHER_FORWARD_HEAD) & (i == 0))
            def _():
                ex.forward()

        m_ref[...] = jnp.full(m_ref.shape, NEG, F32)
        acc_ref[...] = jnp.zeros(acc_ref.shape, F32)

        def step(start, width):
            keys = pl.ds(pl.multiple_of(start, t), width)
            sc = _mm_nt(k_ref[keys, :], q_ref[...])
            m_prev = m_ref[...]
            m_new = jnp.maximum(m_prev, jnp.max(sc, axis=0, keepdims=True))
            alpha = jnp.exp2((m_prev - m_new) * MLA_C2)
            p = jnp.exp2((sc - m_new[0:1, :]) * MLA_C2)
            acc_ref[...] = alpha[0:1, :] * acc_ref[...] + _mm(vt_ref[:, keys], p)
            m_ref[...] = m_new

        def diagonal_step():
            half = t // 2
            k_lo = pl.ds(pl.multiple_of(i * t, half), half)
            k_hi = pl.ds(pl.multiple_of(i * t + half, half), half)
            sc_lo = _scores_t(k_ref[k_lo, :], q_ref[...], t, True)
            sc_hi = _scores_t(k_ref[k_hi, :], q_ref[half:t, :], t, True)
            m_prev = m_ref[...]
            m_lo = jnp.maximum(m_prev, jnp.max(sc_lo, axis=0, keepdims=True))
            m_new = jnp.concatenate(
                [m_lo[:, 0:half],
                 jnp.maximum(m_lo[:, half:t], jnp.max(sc_hi, axis=0, keepdims=True))], axis=1)
            alpha = jnp.exp2((m_prev - m_new) * MLA_C2)
            p_lo = jnp.exp2((sc_lo - m_new[0:1, :]) * MLA_C2)
            p_hi = jnp.exp2((sc_hi - m_new[0:1, half:t]) * MLA_C2)
            acc_ref[...] = alpha[0:1, :] * acc_ref[...] + _mm(vt_ref[:, k_lo], p_lo)
            acc_ref[:, half:t] += _mm(vt_ref[:, k_hi], p_hi)
            m_ref[...] = m_new

        def below_diagonal(jj, carry):
            step(jj * (4 * t), 4 * t)
            return carry

        lax.fori_loop(0, i // 4, below_diagonal, 0)

        @pl.when(i % 4 >= 2)
        def _():
            step((i // 4) * (4 * t), 2 * t)

        @pl.when(i % 2 == 1)
        def _():
            step((i - 1) * t, t)

        diagonal_step()
        l = acc_ref[128:136, :]
        o_ref[...] = (acc_ref[0:128, :] / l[0:1, :]).T.astype(o_ref.dtype)
        lse_ref[...] = m_ref[...] * MLA_C2 + jnp.log2(l)

        if na:
            @pl.when((h == MLA_HEADS - 1) & (i == nq - 1))
            def _():
                ex.finish()

    head = lambda h, i: (h, 0, 0)
    outs = pl.pallas_call(
        body, name="mla_fwd_gather" if na else "mla_fwd", grid=(MLA_HEADS, nq),
        in_specs=[pl.BlockSpec((None, t, 256), lambda h, i: (h, i, 0)),
                  pl.BlockSpec((None, s, 256), head),
                  pl.BlockSpec((None, 256, s), head)] + [HBM_SPEC] * na,
        out_specs=[pl.BlockSpec((t, 128), lambda h, i: (i, h)),
                   pl.BlockSpec((None, 8, t), lambda h, i: (h, 0, i))] + [HBM_SPEC] * na,
        out_shape=[_sds((s, 1024), ATTN_DTYPE), _sds((MLA_HEADS, 8, s), F32)]
        + _exchange_shapes(shards, True),
        scratch_shapes=[pltpu.VMEM((8, t), F32), pltpu.VMEM((256, t), F32)]
        + (_Exchange.semaphores(na) if na else []),
        compiler_params=_params(2),
    )(qh, kh, vth, *shards)
    return outs[0], outs[1], list(outs[2:])


def _silu_parts(g):
    sg = jax.nn.sigmoid(g)
    return g * sg, sg * (1.0 + g * (1.0 - sg))


def _out_proj(x, proj, swa, mla, w_out):
    s = x.shape[0]
    tm = min(512, s)
    nm = s // tm
    depth = min(STREAM_DEPTH, nm)

    def body(x_hbm, p_hbm, a_hbm, b_hbm, w_ref, xo_ref, y_ref, x_buf, g_buf, a_buf, b_buf, sems):
        i = pl.program_id(0)

        def fetch(tile, slot):
            rows = pl.ds(pl.multiple_of(tile * tm, tm), tm)
            return [pltpu.make_async_copy(x_hbm.at[rows], x_buf.at[slot], sems.at[slot, 0]),
                    pltpu.make_async_copy(p_hbm.at[rows, pl.ds(GA, 2048)], g_buf.at[slot],
                                          sems.at[slot, 1]),
                    pltpu.make_async_copy(a_hbm.at[rows], a_buf.at[slot], sems.at[slot, 2]),
                    pltpu.make_async_copy(b_hbm.at[rows], b_buf.at[slot], sems.at[slot, 3])]

        @pl.when(i == 0)
        def _():
            for t in range(depth):
                for cp in fetch(t, t):
                    cp.start()

        slot = i % depth
        for cp in fetch(i, slot):
            cp.wait()
        gates = g_buf[slot].astype(F32)
        sa, _ = _silu_parts(gates[:, 0:1024])
        sb, _ = _silu_parts(gates[:, 1024:2048])
        y_ref[:, 0:1024] = (a_buf[slot].astype(F32) * sa).astype(y_ref.dtype)
        y_ref[:, 1024:2048] = (b_buf[slot].astype(F32) * sb).astype(y_ref.dtype)
        xo_ref[...] = x_buf[slot] + jnp.dot(y_ref[...], w_ref[...], preferred_element_type=F32)

        @pl.when(i + depth < nm)
        def _():
            for cp in fetch(i + depth, slot):
                cp.start()

    row = lambda i: (i, 0)
    return pl.pallas_call(
        body, name="out_proj", grid=(nm,),
        in_specs=[HBM_SPEC, HBM_SPEC, HBM_SPEC, HBM_SPEC,
                  pl.BlockSpec((D_MODEL, D_MODEL), lambda i: (0, 0), pipeline_mode=pl.Buffered(1))],
        out_specs=[pl.BlockSpec((tm, D_MODEL), row), pl.BlockSpec((tm, D_MODEL), row)],
        out_shape=[_sds((s, D_MODEL), F32), _sds((s, D_MODEL), MXU_DTYPE)],
        scratch_shapes=[pltpu.VMEM((depth, tm, D_MODEL), F32),
                        pltpu.VMEM((depth, tm, 2048), proj.dtype),
                        pltpu.VMEM((depth, tm, 1024), swa.dtype),
                        pltpu.VMEM((depth, tm, 1024), mla.dtype),
                        pltpu.SemaphoreType.DMA((depth, 4))],
        compiler_params=_params(1),
    )(x, proj, swa, mla, w_out)


def _final_loss(x, tgt, g):
    s = x.shape[0]
    tm = min(512, s)

    def body(x_ref, t_ref, g_ref, dx_ref, dxb_ref, dg_ref, loss_ref):
        @pl.when(pl.program_id(0) == 0)
        def _():
            dg_ref[...] = jnp.zeros(dg_ref.shape, F32)
            loss_ref[...] = jnp.zeros(loss_ref.shape, F32)
        xh, r = _rownorm(x_ref[...])
        gg = g_ref[...]
        err = xh * gg - t_ref[...]
        per_row = jnp.mean(err * err, axis=-1, keepdims=True)
        loss_ref[...] += 0.5 * jnp.sum(per_row, axis=0, keepdims=True)
        dy = err * (1.0 / D_MODEL)
        dg_ref[...] += jnp.sum(dy * xh, axis=0, keepdims=True)
        dx = _rownorm_bwd(dy * gg, xh, r)
        dx_ref[...] = dx
        dxb_ref[...] = dx.astype(dxb_ref.dtype)

    row = lambda i: (i, 0)
    fixed = lambda i: (0, 0)
    return pl.pallas_call(
        body, name="final_loss", grid=(s // tm,),
        in_specs=[pl.BlockSpec((tm, D_MODEL), row), pl.BlockSpec((tm, D_MODEL), row),
                  pl.BlockSpec((1, D_MODEL), fixed)],
        out_specs=[pl.BlockSpec((tm, D_MODEL), row), pl.BlockSpec((tm, D_MODEL), row),
                   pl.BlockSpec((1, D_MODEL), fixed), pl.BlockSpec((8, 128), fixed)],
        out_shape=[_sds((s, D_MODEL), F32), _sds((s, D_MODEL), MXU_DTYPE), _sds((1, D_MODEL), F32),
                   _sds((8, 128), F32)],
        compiler_params=_params(1),
    )(x, tgt, g)


def _out_proj_bwd(dx, proj, swa, mla, w_out):
    s = dx.shape[0]
    tm = min(512, s)
    nm = s // tm
    depth = min(STREAM_DEPTH, nm)

    def body(dx_hbm, p_hbm, a_hbm, b_hbm, w_ref, doa_ref, dob_ref, dg_ref, dlt_ref,
             dx_buf, g_buf, a_buf, b_buf, sems):
        i = pl.program_id(0)

        def fetch(tile, slot):
            rows = pl.ds(pl.multiple_of(tile * tm, tm), tm)
            return [pltpu.make_async_copy(dx_hbm.at[rows], dx_buf.at[slot], sems.at[slot, 0]),
                    pltpu.make_async_copy(p_hbm.at[rows, pl.ds(GA, 2048)], g_buf.at[slot],
                                          sems.at[slot, 1]),
                    pltpu.make_async_copy(a_hbm.at[rows], a_buf.at[slot], sems.at[slot, 2]),
                    pltpu.make_async_copy(b_hbm.at[rows], b_buf.at[slot], sems.at[slot, 3])]

        @pl.when(i == 0)
        def _():
            for t in range(depth):
                for cp in fetch(t, t):
                    cp.start()

        slot = i % depth
        for cp in fetch(i, slot):
            cp.wait()
        dx = dx_buf[slot].astype(MXU_DTYPE)
        gates = g_buf[slot].astype(F32)
        a_blk = a_buf[slot].astype(F32)
        b = b_buf[slot].astype(F32)
        dya = _mm_nt(dx, w_ref[0:1024, :])
        sa, dsa = _silu_parts(gates[:, 0:1024])
        doa_ref[...] = (dya * sa).astype(doa_ref.dtype)
        dg_ref[:, 0:1024] = (dya * a_blk * dsa).astype(dg_ref.dtype)
        dyb = _mm_nt(dx, w_ref[1024:2048, :])
        sb, dsb = _silu_parts(gates[:, 1024:2048])
        dob = dyb * sb
        dob_ref[...] = dob.astype(dob_ref.dtype)
        dg_ref[:, 1024:2048] = (dyb * b * dsb).astype(dg_ref.dtype)
        prod = dob * b
        for h in range(MLA_HEADS):
            dlt = jnp.sum(prod[:, 128 * h:128 * (h + 1)], axis=1, keepdims=True)
            dlt_ref[h] = jnp.broadcast_to(dlt, (tm, 128)).T[0:8, :]

        @pl.when(i + depth < nm)
        def _():
            for cp in fetch(i + depth, slot):
                cp.start()

    row = lambda i: (i, 0)
    return pl.pallas_call(
        body, name="out_proj_bwd", grid=(nm,),
        in_specs=[HBM_SPEC, HBM_SPEC, HBM_SPEC, HBM_SPEC,
                  pl.BlockSpec((D_MODEL, D_MODEL), lambda i: (0, 0), pipeline_mode=pl.Buffered(1))],
        out_specs=[pl.BlockSpec((tm, 1024), row), pl.BlockSpec((tm, 1024), row),
                   pl.BlockSpec((tm, D_MODEL), row),
                   pl.BlockSpec((MLA_HEADS, 8, tm), lambda i: (0, 0, i))],
        out_shape=[_sds((s, 1024), ATTN_DTYPE), _sds((s, 1024), MXU_DTYPE), _sds((s, D_MODEL), MXU_DTYPE),
                   _sds((MLA_HEADS, 8, s), F32)],
        scratch_shapes=[pltpu.VMEM((depth, tm, D_MODEL), dx.dtype),
                        pltpu.VMEM((depth, tm, 2048), proj.dtype),
                        pltpu.VMEM((depth, tm, 1024), swa.dtype),
                        pltpu.VMEM((depth, tm, 1024), mla.dtype),
                        pltpu.SemaphoreType.DMA((depth, 4))],
        compiler_params=_params(1),
    )(dx, proj, swa, mla, w_out)


def _matmul_tn(a, b, name):
    s, m = a.shape
    n = b.shape[1]
    tm, tn, tk = min(1024, m), min(1024, n), min(2048, s)
    nk = s // tk

    def body(a_ref, b_ref, o_ref, acc_ref):
        k = pl.program_id(2)

        @pl.when(k == 0)
        def _():
            acc_ref[...] = jnp.zeros(acc_ref.shape, F32)
        acc_ref[...] += _mm_tn(a_ref[...], b_ref[...])

        @pl.when(k == nk - 1)
        def _():
            o_ref[...] = acc_ref[...].astype(o_ref.dtype)

    return pl.pallas_call(
        body, name=name, grid=(m // tm, n // tn, nk),
        in_specs=[pl.BlockSpec((tk, tm), lambda i, j, k: (k, i)),
                  pl.BlockSpec((tk, tn), lambda i, j, k: (k, j))],
        out_specs=pl.BlockSpec((tm, tn), lambda i, j, k: (i, j)),
        out_shape=_sds((m, n), GRAD_DTYPE),
        scratch_shapes=[pltpu.VMEM((tm, tn), F32)],
        compiler_params=_params(3),
    )(a, b)


def _grad_w_in(h, dqa, dgate, dlat, dkva, blocks):
    s = h.shape[0]
    tm, tn, tk = 1024, 1024, min(2048, s)
    nk = s // tk
    grid = (D_MODEL // tm, NP // tn, nk)
    na = len(blocks)

    def body(*refs):
        a_ref, dqa_ref, dg8_ref, dlat_ref, dkva_ref = refs[:5]
        o_ref = refs[5 + na]
        acc_ref = refs[6 + 2 * na]
        i, j, k = pl.program_id(0), pl.program_id(1), pl.program_id(2)
        if na:
            ex = _Exchange(refs[5:5 + na], refs[6 + na:6 + 2 * na], refs[7 + 2 * na:], gather=False)

            @pl.when((i == 0) & (j == 0) & (k == 0))
            def _():
                ex.start()

        @pl.when(k == 0)
        def _():
            acc_ref[...] = jnp.zeros(acc_ref.shape, F32)

        @pl.when(j == QA // tn)
        def _():
            acc_ref[...] += _mm_tn(a_ref[...], dqa_ref[...])

        @pl.when((j == GA // tn) | (j == GB // tn))
        def _():
            acc_ref[...] += _mm_tn(a_ref[...], dg8_ref[...])

        @pl.when(j == CQ // tn)
        def _():
            acc_ref[:, 0:768] += _mm_tn(a_ref[...], dlat_ref[...])
            acc_ref[:, 768:1024] += _mm_tn(a_ref[...], dkva_ref[...])

        @pl.when(k == nk - 1)
        def _():
            o_ref[...] = acc_ref[...].astype(o_ref.dtype)

        if na:
            @pl.when((i == grid[0] - 1) & (j == grid[1] - 1) & (k == nk - 1))
            def _():
                ex.wait()

    def when(group, width):
        return pl.BlockSpec((tk, width), lambda i, j, k: (jnp.where(j == group, k, 0), 0))

    outs = pl.pallas_call(
        body, name="grad_w_in_scatter" if na else "grad_w_in", grid=grid,
        in_specs=[pl.BlockSpec((tk, tm), lambda i, j, k: (k, i)),
                  when(QA // tn, 1024),
                  pl.BlockSpec((tk, 1024), lambda i, j, k: (
                      jnp.where((j == GA // tn) | (j == GB // tn), k, 0),
                      jnp.clip(j - GA // tn, 0, 1))),
                  when(CQ // tn, 768), when(CQ // tn, 256)] + [HBM_SPEC] * na,
        out_specs=[pl.BlockSpec((tm, tn), lambda i, j, k: (i, j))] + [HBM_SPEC] * na,
        out_shape=[_sds((D_MODEL, NP), GRAD_DTYPE)] + _exchange_shapes(blocks, False),
        scratch_shapes=[pltpu.VMEM((tm, tn), F32)] + (_Exchange.semaphores(na) if na else []),
        compiler_params=_params(3),
    )(h, dqa, dgate, dlat, dkva, *blocks)
    return outs[0], list(outs[1:])


def _swa_bwd(proj, sinks, do, o, blocks):
    s = proj.shape[0]
    nb = s // BLOCK
    na = len(blocks)

    def body(*refs):
        sink_ref, q_ref, kp_ref, kc_ref, do_ref, o_ref = refs[:6]
        dq_ref, dkv_ref, dsink_ref = refs[6 + na:9 + na]
        carry_ref, bias_ref = refs[9 + 2 * na:11 + 2 * na]
        n = pl.program_id(0)
        _swa_fill_bias(n, bias_ref)
        if na:
            ex = _Exchange(refs[6:6 + na], refs[9 + na:9 + 2 * na], refs[11 + 2 * na:], gather=False)

        @pl.when(n == 0)
        def _():
            carry_ref[...] = jnp.zeros(carry_ref.shape, F32)
            dsink_ref[...] = jnp.zeros(dsink_ref.shape, F32)
            if na:
                ex.start()

        @pl.when(n < nb)
        def _():
            kk, vv = _swa_kv(kp_ref[...].astype(F32), kc_ref[...].astype(F32))
            lane_s = lax.broadcasted_iota(jnp.int32, (8, 128), 1)
            qs = _swa_stack(q_ref) * 0.125
            dos = _swa_stack(do_ref)
            dlt = []
            for j in range(2):
                prod_t = (_swa_pairs(do_ref, j) * _swa_pairs(o_ref, j)).T
                dlt += [jnp.sum(prod_t[0:64], axis=0, keepdims=True),
                        jnp.sum(prod_t[64:128], axis=0, keepdims=True)]
            dlt = jnp.concatenate(dlt, axis=1)
            ex, es, inv = _swa_softmax(qs, kk, bias_ref[...], _swa_sink_row(sink_ref))
            p = ex * inv
            ds = p * (_mm_nt(vv, dos) - dlt)
            sink_term = es * inv * dlt
            dsink = jnp.zeros((8, 128), F32)
            for b in range(SWA_HEADS):
                dsh = -jnp.sum(sink_term[:, 128 * b:128 * (b + 1)], axis=1, keepdims=True)
                dsink = dsink + jnp.where(lane_s == _swa_head(b), dsh, 0.0)
            for j, dq in enumerate(_swa_unstack(_mm_tn(ds, kk))):
                for a in range(4):
                    cols = slice(128 * (4 * j + a), 128 * (4 * j + a + 1))
                    dq_ref[:, cols] = (dq[128 * a:128 * (a + 1)] * 0.125).astype(dq_ref.dtype)
            contrib = jnp.concatenate([_mm(ds, qs), _mm(p, dos)], axis=1)
            dkv_ref[...] = (carry_ref[...] + contrib[0:BLOCK]).astype(dkv_ref.dtype)
            carry_ref[...] = contrib[BLOCK:2 * BLOCK]
            dsink_ref[...] += dsink

        @pl.when(n == nb)
        def _():
            dkv_ref[...] = carry_ref[...].astype(dkv_ref.dtype)
            if na:
                ex.wait()

    cur = lambda n: (jnp.minimum(n, nb - 1), 0)
    outs = pl.pallas_call(
        body, name="swa_bwd_scatter" if na else "swa_bwd", grid=(nb + 1,),
        in_specs=[pl.BlockSpec(memory_space=pltpu.SMEM),
                  pl.BlockSpec((BLOCK, 1024), cur),
                  pl.BlockSpec((BLOCK, 256), lambda n: (jnp.clip(n - 1, 0, nb - 1), KVA // 256)),
                  pl.BlockSpec((BLOCK, 256), lambda n: (jnp.minimum(n, nb - 1), KVA // 256)),
                  pl.BlockSpec((BLOCK, 1024), cur), pl.BlockSpec((BLOCK, 1024), cur)]
        + [HBM_SPEC] * na,
        out_specs=[pl.BlockSpec((BLOCK, 1024), cur),
                   pl.BlockSpec((BLOCK, 256), lambda n: (jnp.maximum(n - 1, 0), 0)),
                   pl.BlockSpec((8, 128), lambda n: (0, 0))] + [HBM_SPEC] * na,
        out_shape=[_sds((s, 1024), MXU_DTYPE), _sds((s, 256), MXU_DTYPE), _sds((8, 128), F32)]
        + _exchange_shapes(blocks, False),
        scratch_shapes=[pltpu.VMEM((BLOCK, 256), F32), pltpu.VMEM((2 * BLOCK, SWA_STACK), F32)]
        + (_Exchange.semaphores(na) if na else []),
        compiler_params=_params(1),
    )(sinks, proj, proj, proj, do, o, *blocks)
    return outs[0], outs[1], outs[2], list(outs[3:])


def _mla_bwd(qh, kh, vh, do, dlt, lse, blocks):
    s = qh.shape[1]
    t = min(MLA_TILE, s)
    nq = s // t

    na = len(blocks)

    def body(*refs):
        q_ref, k_ref, v_ref, do_ref, dlt_ref, lse_ref = refs[:6]
        dq_ref, dk_ref, dv_ref = refs[6 + na:9 + na]
        h, j = pl.program_id(0), pl.program_id(1)
        if na:
            ex = _Exchange(refs[6:6 + na], refs[9 + na:9 + 2 * na], refs[9 + 2 * na:], gather=False)

            @pl.when((h == 0) & (j == 0))
            def _():
                ex.start()

        def step(start, width, diagonal=False):
            rows = pl.ds(pl.multiple_of(start, t), width)
            q, k, dout = q_ref[rows, :], k_ref[...], do_ref[rows, :]
            sc = _scores_t(k, q, t, diagonal)
            p = jnp.exp2(sc * MLA_C2 - lse_ref[0:1, rows])
            dv = _mm(p, dout)
            ds = p * (_mm_nt(v_ref[...], dout) - dlt_ref[0:1, rows])
            dk = _mm(ds, q)
            dq = _mm_tn(ds, k)
            return rows, dq, dk, dv

        rows, dq, dk, dv = step(j * t, t, diagonal=True)
        dk_ref[...] = dk
        dv_ref[...] = dv

        @pl.when(j == 0)
        def _():
            dq_ref[rows, :] = dq * MLA_SCALE

        @pl.when(j > 0)
        def _():
            dq_ref[rows, :] = (dq_ref[rows, :] + dq) * MLA_SCALE

        def above_diagonal(start, width):
            rows, dq, dk, dv = step(start, width)
            dk_ref[...] += dk
            dv_ref[...] += dv

            @pl.when(j == 0)
            def _():
                dq_ref[rows, :] = dq

            @pl.when(j > 0)
            def _():
                dq_ref[rows, :] += dq

        n_above = nq - 1 - j

        def quad(jj, carry):
            above_diagonal((j + 1 + 4 * jj) * t, 4 * t)
            return carry

        lax.fori_loop(0, n_above // 4, quad, 0)

        @pl.when(n_above % 4 >= 2)
        def _():
            above_diagonal((j + 1 + 4 * (n_above // 4)) * t, 2 * t)

        @pl.when(n_above % 2 == 1)
        def _():
            above_diagonal((nq - 1) * t, t)

        dk_ref[...] *= MLA_SCALE

        if na:
            @pl.when((h == MLA_HEADS - 1) & (j == nq - 1))
            def _():
                ex.wait()

    head = lambda h, j: (h, 0, 0)
    kv_map = lambda h, j: (h, j, 0)
    outs = pl.pallas_call(
        body, name="mla_bwd_scatter" if na else "mla_bwd", grid=(MLA_HEADS, nq),
        in_specs=[pl.BlockSpec((None, s, 256), head), pl.BlockSpec((None, t, 256), kv_map),
                  pl.BlockSpec((None, t, 128), kv_map),
                  pl.BlockSpec((s, 128), lambda h, j: (0, h)),
                  pl.BlockSpec((None, 8, s), head), pl.BlockSpec((None, 8, s), head)]
        + [HBM_SPEC] * na,
        out_specs=[pl.BlockSpec((None, s, 256), head),
                   pl.BlockSpec((None, t, 256), kv_map), pl.BlockSpec((None, t, 128), kv_map)]
        + [HBM_SPEC] * na,
        out_shape=[_sds((MLA_HEADS, s, 256), F32), _sds((MLA_HEADS, s, 256), F32),
                   _sds((MLA_HEADS, s, 128), F32)] + _exchange_shapes(blocks, False),
        scratch_shapes=_Exchange.semaphores(na) if na else [],
        compiler_params=_params(2),
    )(qh, kh, vh, do, dlt, lse, *blocks)
    return outs[0], outs[1], outs[2], list(outs[3:])


def _mla_prep_bwd(dqh, dkh, dvh, proj, gq, gkv, w_q, w_kv, tabs):
    s = proj.shape[0]
    tm = min(512, s)
    nm = s // tm
    c, s1, s2 = tabs

    def body(dq_ref, dk_ref, dv_ref, p_ref, gq_ref, gkv_ref, wq_ref, wkv_ref,
             c_ref, s1_ref, s2_ref, dp_ref, dwq_ref, dwkv_ref, dgq_ref, dgkv_ref,
             dqf_ref, dkvf_ref, dwq_acc, dwkv_acc):
        @pl.when(pl.program_id(0) == 0)
        def _():
            dgq_ref[...] = jnp.zeros(dgq_ref.shape, F32)
            dgkv_ref[...] = jnp.zeros(dgkv_ref.shape, F32)
            dwq_acc[...] = jnp.zeros(dwq_acc.shape, F32)
            dwkv_acc[...] = jnp.zeros(dwkv_acc.shape, F32)
        cc, ns1, ns2 = c_ref[...], -s1_ref[...], -s2_ref[...]
        dkr = jnp.zeros((tm, 128), F32)
        for h in range(MLA_HEADS):
            dqf_ref[:, 128 * h:128 * (h + 1)] = dq_ref[h, :, 0:128].astype(dqf_ref.dtype)
            dqf_ref[:, 1024 + 128 * h:1024 + 128 * (h + 1)] = _rope(
                dq_ref[h, :, 128:256], cc, ns1, ns2).astype(dqf_ref.dtype)
            dkvf_ref[:, 128 * h:128 * (h + 1)] = dk_ref[h, :, 0:128].astype(dkvf_ref.dtype)
            dkvf_ref[:, 1024 + 128 * h:1024 + 128 * (h + 1)] = dv_ref[h].astype(dkvf_ref.dtype)
            dkr = dkr + dk_ref[h, :, 128:256]
        dcqn = _mm_nt(dqf_ref[...], wq_ref[...])
        dckvn = _mm_nt(dkvf_ref[...], wkv_ref[...])
        cqh, rq = _rownorm(p_ref[:, 0:384].astype(F32))
        ckvh, rkv = _rownorm(p_ref[:, 384:640].astype(F32))
        dgq_ref[...] += jnp.sum(dcqn * cqh, axis=0, keepdims=True)
        dgkv_ref[...] += jnp.sum(dckvn * ckvh, axis=0, keepdims=True)
        dp_ref[:, 0:384] = _rownorm_bwd(dcqn * gq_ref[...], cqh, rq).astype(dp_ref.dtype)
        dp_ref[:, 384:640] = _rownorm_bwd(dckvn * gkv_ref[...], ckvh, rkv).astype(dp_ref.dtype)
        dp_ref[:, 640:768] = _rope(dkr, cc, ns1, ns2).astype(dp_ref.dtype)
        dwq_acc[...] += _mm_tn(cqh * gq_ref[...], dqf_ref[...])
        dwkv_acc[...] += _mm_tn(ckvh * gkv_ref[...], dkvf_ref[...])

        @pl.when(pl.program_id(0) == nm - 1)
        def _():
            dwq_ref[...] = dwq_acc[...].astype(dwq_ref.dtype)
            dwkv_ref[...] = dwkv_acc[...].astype(dwkv_ref.dtype)

    row = lambda i: (i, 0)
    fixed = lambda i: (0, 0)
    head = lambda i: (0, i, 0)
    return pl.pallas_call(
        body, name="mla_prep_bwd", grid=(nm,),
        in_specs=[pl.BlockSpec((MLA_HEADS, tm, 256), head), pl.BlockSpec((MLA_HEADS, tm, 256), head),
                  pl.BlockSpec((MLA_HEADS, tm, 128), head),
                  pl.BlockSpec((tm, 768), lambda i: (i, CQ // 768)),
                  pl.BlockSpec((1, Q_RANK), fixed), pl.BlockSpec((1, KV_RANK), fixed),
                  pl.BlockSpec((Q_RANK, 2048), fixed), pl.BlockSpec((KV_RANK, 2048), fixed),
                  pl.BlockSpec((tm, 128), row), pl.BlockSpec((tm, 128), row),
                  pl.BlockSpec((tm, 128), row)],
        out_specs=[pl.BlockSpec((tm, 768), row), pl.BlockSpec((Q_RANK, 2048), fixed),
                   pl.BlockSpec((KV_RANK, 2048), fixed),
                   pl.BlockSpec((1, Q_RANK), fixed), pl.BlockSpec((1, KV_RANK), fixed)],
        out_shape=[_sds((s, 768), MXU_DTYPE), _sds((Q_RANK, 2048), GRAD_DTYPE),
                   _sds((KV_RANK, 2048), GRAD_DTYPE),
                   _sds((1, Q_RANK), F32), _sds((1, KV_RANK), F32)],
        scratch_shapes=[pltpu.VMEM((tm, 2048), MXU_DTYPE), pltpu.VMEM((tm, 2048), MXU_DTYPE),
                        pltpu.VMEM((Q_RANK, 2048), F32), pltpu.VMEM((KV_RANK, 2048), F32)],
        compiler_params=_params(1),
    )(dqh, dkh, dvh, proj, gq, gkv, w_q, w_kv, c, s1, s2)


def _in_proj_bwd(dqa, dgate, dlat, dkva, w, x, dx_out, g, blocks):
    s = x.shape[0]
    tm = min(256, s)
    nm = s // tm
    na = len(blocks)

    def body(*refs):
        dqa_ref, dg8_ref, dlat_ref, dkva_ref, w_ref, x_ref, dxo_ref, g_ref = refs[:8]
        dx_ref, dxb_ref, dg_ref = refs[8 + na:11 + na]
        if na:
            ex = _Exchange(refs[8:8 + na], refs[11 + na:11 + 2 * na], refs[11 + 2 * na:], gather=False)

        @pl.when(pl.program_id(0) == 0)
        def _():
            dg_ref[...] = jnp.zeros(dg_ref.shape, F32)
            if na:
                ex.start()

        dh = (_mm_nt(dqa_ref[...], w_ref[:, QA:QA + 1024])
              + _mm_nt(dg8_ref[:, 0:1024], w_ref[:, GA:GA + 1024])
              + _mm_nt(dg8_ref[:, 1024:2048], w_ref[:, GB:GB + 1024])
              + _mm_nt(dlat_ref[...], w_ref[:, CQ:CQ + 768])
              + _mm_nt(dkva_ref[...], w_ref[:, KVA:KVA + 256]))
        xh, r = _rownorm(x_ref[...])
        dg_ref[...] += jnp.sum(dh * xh, axis=0, keepdims=True)
        dx = dxo_ref[...] + _rownorm_bwd(dh * g_ref[...], xh, r)
        dx_ref[...] = dx
        dxb_ref[...] = dx.astype(dxb_ref.dtype)

        if na:
            @pl.when(pl.program_id(0) == nm - 1)
            def _():
                ex.wait()

    row = lambda i: (i, 0)
    fixed = lambda i: (0, 0)
    outs = pl.pallas_call(
        body, name="in_proj_bwd_scatter" if na else "in_proj_bwd", grid=(nm,),
        in_specs=[pl.BlockSpec((tm, 1024), row), pl.BlockSpec((tm, 2048), row),
                  pl.BlockSpec((tm, 768), row), pl.BlockSpec((tm, 256), row),
                  pl.BlockSpec((D_MODEL, NP), fixed, pipeline_mode=pl.Buffered(1)),
                  pl.BlockSpec((tm, D_MODEL), row), pl.BlockSpec((tm, D_MODEL), row),
                  pl.BlockSpec((1, D_MODEL), fixed)] + [HBM_SPEC] * na,
        out_specs=[pl.BlockSpec((tm, D_MODEL), row), pl.BlockSpec((tm, D_MODEL), row),
                   pl.BlockSpec((1, D_MODEL), fixed)] + [HBM_SPEC] * na,
        out_shape=[_sds((s, D_MODEL), F32), _sds((s, D_MODEL), MXU_DTYPE), _sds((1, D_MODEL), F32)]
        + _exchange_shapes(blocks, False),
        scratch_shapes=_Exchange.semaphores(na) if na else [],
        compiler_params=_params(1),
    )(dqa, dgate, dlat, dkva, w, x, dx_out, g, *blocks)
    return outs[0], outs[1], outs[2], list(outs[3:])


def _adamw_update(g, w_ref, m_ref, v_ref, g_ref, d_ref, nm_ref, nv_ref):
    m2 = ADAM_B1 * m_ref[...] + (1.0 - ADAM_B1) * g
    v2 = ADAM_B2 * v_ref[...] + (1.0 - ADAM_B2) * (g * g)
    m_hat = m2 / (1.0 - ADAM_B1 ** ADAM_STEP)
    v_hat = v2 / (1.0 - ADAM_B2 ** ADAM_STEP)
    g_ref[...] = g
    d_ref[...] = -ADAM_LR * (m_hat / (jnp.sqrt(v_hat) + ADAM_EPS) + ADAM_WD * w_ref[...])
    nm_ref[...] = m2
    nv_ref[...] = v2


def _reduce_adamw_in(parts, w_t, m_t, v_t, name):
    n_layers = len(parts)
    tc = 512
    nc = D_MODEL // tc

    def body(*refs):
        p_refs = refs[:n_layers]
        w_ref, m_ref, v_ref, g_ref, d_ref, nm_ref, nv_ref = refs[n_layers:]
        layer = pl.program_id(0)
        for l in range(n_layers):
            @pl.when(layer == l)
            def _(l=l):
                g = p_refs[l][0].astype(F32)
                for k in range(1, N_DEV):
                    g = g + p_refs[l][k].astype(F32)
                _adamw_update(g[0:SHARD_COLS], w_ref, m_ref, v_ref, g_ref, d_ref, nm_ref, nv_ref)

    def part_spec(l):
        return pl.BlockSpec((N_DEV, SHARD_PAD, tc),
                            lambda layer, j: (0, 0, jnp.where(layer == l, j, 0)))

    blk = pl.BlockSpec((None, SHARD_COLS, tc), lambda layer, j: (layer, 0, j))
    return pl.pallas_call(
        body, name=name, grid=(n_layers, nc),
        in_specs=[part_spec(l) for l in range(n_layers)] + [blk, blk, blk],
        out_specs=[blk, blk, blk, blk],
        out_shape=[_sds((n_layers, SHARD_COLS, D_MODEL), F32)] * 4,
        compiler_params=_params(2),
    )(*parts, w_t, m_t, v_t)


def _reduce_adamw(parts, w, m, v, name):
    n_layers = len(parts)
    rows, part_cols = parts[0].shape[1:]
    cols = w.shape[-1]
    lanes = -(-cols // 128) * 128
    tr = rows
    for cand in (1024, 512, 256, 128, 64, 32, 16, 8):
        if rows % cand == 0 and N_DEV * cand * lanes * 4 <= 8 * 1024 * 1024:
            tr = cand
            break
    nr = rows // tr

    def body(*refs):
        p_refs = refs[:n_layers]
        w_ref, m_ref, v_ref, g_ref, d_ref, nm_ref, nv_ref = refs[n_layers:]
        layer = pl.program_id(0)
        for l in range(n_layers):
            @pl.when(layer == l)
            def _(l=l):
                g = p_refs[l][0, :, 0:cols].astype(F32)
                for k in range(1, N_DEV):
                    g = g + p_refs[l][k, :, 0:cols].astype(F32)
                _adamw_update(g, w_ref, m_ref, v_ref, g_ref, d_ref, nm_ref, nv_ref)

    def part_spec(l):
        return pl.BlockSpec((N_DEV, tr, part_cols),
                            lambda layer, i: (0, jnp.where(layer == l, i, 0), 0))

    blk = pl.BlockSpec((tr, cols), lambda layer, i: (layer * nr + i, 0))
    return pl.pallas_call(
        body, name=name, grid=(n_layers, nr),
        in_specs=[part_spec(l) for l in range(n_layers)] + [blk, blk, blk],
        out_specs=[blk, blk, blk, blk],
        out_shape=[_sds((n_layers * rows, cols), F32)] * 4,
        compiler_params=_params(2),
    )(*parts, w, m, v)


def _position():
    x, y, c = lax.axis_index("x"), lax.axis_index("y"), lax.axis_index("c")
    return x, y, c


def _index(px, py, pc):
    return 4 * px + 2 * py + pc


HBM_SPEC = pl.BlockSpec(memory_space=pltpu.HBM)
IN_BLOCKED = (D_MODEL, N_DEV * SHARD_PAD)
IN_BLOCKED_T = (N_DEV * SHARD_PAD, D_MODEL)


def _block(ref, idx):
    if tuple(ref.shape) == IN_BLOCKED:
        return ref.at[:, pl.ds(pl.multiple_of(idx * SHARD_PAD, SHARD_PAD), SHARD_PAD)]
    if tuple(ref.shape) == IN_BLOCKED_T:
        return ref.at[pl.ds(pl.multiple_of(idx * SHARD_PAD, SHARD_PAD), SHARD_PAD)]
    return ref.at[idx]


class _Gather:
    def __init__(self, srcs, dsts, sems):
        self.srcs, self.dsts = srcs, dsts
        self.send_sems, self.recv_sems, self.local_sems = sems
        x, y, c = _position()
        self.c = c
        self.me, self.sibling = (x, y, c), (x, y, 1 - c)
        self.chips = [(1 - x, y), (x, 1 - y), (1 - x, 1 - y)]

    def _copy(self, a, k, block, to, own=False):
        slot = _block(self.dsts[a], _index(*block))
        return pltpu.make_async_remote_copy(
            src_ref=self.srcs[a] if own else slot, dst_ref=slot,
            send_sem=self.send_sems.at[7 * a + k], recv_sem=self.recv_sems.at[7 * a + k],
            device_id=to, device_id_type=MESH)

    def _local(self, a):
        return pltpu.make_async_copy(self.srcs[a], _block(self.dsts[a], _index(*self.me)),
                                     self.local_sems.at[a])

    def _first(self, a):
        return [self._copy(a, 0, self.me, self.sibling, own=True)] + [
            self._copy(a, 1 + j, self.me, (*chip, self.c), own=True)
            for j, chip in enumerate(self.chips)]

    def _passed(self, a, j):
        return self._copy(a, 4 + j, (*self.chips[j], self.c), self.sibling)

    def start(self):
        for a in range(len(self.srcs)):
            self._local(a).start()
            for cp in self._first(a):
                cp.start()

    def forward(self):
        for j, chip in enumerate(self.chips):
            for a in range(len(self.srcs)):
                self._copy(a, 1 + j, (*chip, self.c), self.me).wait_recv()
                self._passed(a, j).start()

    def finish(self):
        for a in range(len(self.srcs)):
            self._copy(a, 0, self.sibling, self.me).wait_recv()
            for j, chip in enumerate(self.chips):
                self._copy(a, 4 + j, (*chip, 1 - self.c), self.me).wait_recv()
            for cp in self._first(a):
                cp.wait_send()
            for j in range(3):
                self._passed(a, j).wait_send()
            self._local(a).wait()


def _all_gather(shards, name):
    na = len(shards)

    def body(*refs):
        g = _Gather(refs[:na], refs[na:2 * na], refs[2 * na:])
        g.start()
        g.forward()
        g.finish()

    return pl.pallas_call(
        body, name=name,
        in_specs=[HBM_SPEC] * na, out_specs=[HBM_SPEC] * na,
        out_shape=_exchange_shapes(shards, True),
        scratch_shapes=_Exchange.semaphores(na),
    )(*shards)


class _Exchange:
    def __init__(self, srcs, dsts, sems, gather):
        self.srcs, self.dsts, self.gather = srcs, dsts, gather
        self.send_sems, self.recv_sems, self.local_sems = sems
        x, y, c = _position()
        self.me = _index(x, y, c)
        self.peers = [(x ^ ((k >> 2) & 1), y ^ ((k >> 1) & 1), c ^ (k & 1)) for k in range(1, N_DEV)]

    @staticmethod
    def semaphores(na):
        return [pltpu.SemaphoreType.DMA((7 * na,)), pltpu.SemaphoreType.DMA((7 * na,)),
                pltpu.SemaphoreType.DMA((na,))]

    def _src(self, a, slot):
        return self.srcs[a] if self.gather else _block(self.srcs[a], slot)

    def _local(self, a):
        return pltpu.make_async_copy(self._src(a, self.me), self.dsts[a].at[self.me],
                                     self.local_sems.at[a])

    def _send(self, a, k):
        peer = self.peers[k]
        return pltpu.make_async_remote_copy(
            src_ref=self._src(a, _index(*peer)), dst_ref=self.dsts[a].at[self.me],
            send_sem=self.send_sems.at[7 * a + k], recv_sem=self.recv_sems.at[7 * a + k],
            device_id=peer, device_id_type=MESH)

    def _arrival(self, a, k):
        landed = self.dsts[a].at[_index(*self.peers[k])]
        return pltpu.make_async_remote_copy(
            src_ref=landed, dst_ref=landed,
            send_sem=self.send_sems.at[7 * a + k], recv_sem=self.recv_sems.at[7 * a + k],
            device_id=self.peers[k], device_id_type=MESH)

    def start(self):
        for a in range(len(self.srcs)):
            self._local(a).start()
            for k in range(N_DEV - 1):
                self._send(a, k).start()

    def wait(self):
        for a in range(len(self.srcs)):
            for k in range(N_DEV - 1):
                self._arrival(a, k).wait_recv()
            for k in range(N_DEV - 1):
                self._send(a, k).wait_send()
            self._local(a).wait()


def _exchange_shapes(arrays, gather):
    def shape(a):
        if gather:
            return IN_BLOCKED if a.shape == (D_MODEL, SHARD_PAD) else (N_DEV,) + a.shape
        return (N_DEV, SHARD_PAD, D_MODEL) if a.shape == IN_BLOCKED_T else a.shape
    return [_sds(shape(a), a.dtype) for a in arrays]


def _exchange_call(arrays, gather, name):
    na = len(arrays)

    def body(*refs):
        ex = _Exchange(refs[:na], refs[na:2 * na], refs[2 * na:], gather)
        ex.start()
        ex.wait()

    return pl.pallas_call(
        body, name=name,
        in_specs=[HBM_SPEC] * na, out_specs=[HBM_SPEC] * na,
        out_shape=_exchange_shapes(arrays, gather),
        scratch_shapes=_Exchange.semaphores(na),
    )(*arrays)


def _layer_fwd(x, small, w_in, g_rest, rest_shards, tabs, next_shards):
    attn_g, sinks, gq, gkv = small
    proj, h, gathered_rest = _in_proj(x, attn_g, w_in, rest_shards)
    w_q, w_kv, w_o = _rest_from_gathered(*(gathered_rest if rest_shards else g_rest))
    swa = _swa_fwd(proj, sinks)
    qh, kh, vh, vth = _mla_prep(proj, gq, gkv, w_q, w_kv, tabs)
    mla, lse, gathered = _mla_fwd(qh, kh, vth, next_shards)
    x_next, y = _out_proj(x, proj, swa, mla, w_o)
    return x_next, (x, proj, h, swa, qh, kh, vh, mla, lse, y), (w_in, w_q, w_kv, w_o), gathered


def _layer_bwd(dx, dxb, saved, small, weights, tabs, pending, send_own):
    attn_g, sinks, gq, gkv = small
    w_in, w_q, w_kv, w_o = weights
    x, proj, h, swa, qh, kh, vh, mla, lse, y = saved
    d_o = _matmul_tn(y, dxb, "grad_w_out")
    o_block = d_o.reshape(N_DEV, 256, D_MODEL)
    do_a, do_b, dgate, dlt = _out_proj_bwd(dxb, proj, swa, mla, w_o)
    dqa, dkva, dsk, got_o = _swa_bwd(proj, sinks, do_a, swa, [o_block] if send_own else [])
    dqh, dkh, dvh, received = _mla_bwd(qh, kh, vh, do_b, dlt, lse, pending)
    dlat, d_q, d_kv, d_gq, d_gkv = _mla_prep_bwd(dqh, dkh, dvh, proj, gq, gkv, w_q, w_kv, tabs)
    qkv_blocks = _qkv_grad_blocks(d_q, d_kv)
    d_inp, got_qkv = _grad_w_in(h, dqa, dgate, dlat, dkva, qkv_blocks if send_own else [])
    in_block = _in_grad_blocks(d_inp)
    dx, dxb, d_attn, got_in = _in_proj_bwd(dqa, dgate, dlat, dkva, w_in, x, dx, attn_g,
                                           [in_block] if send_own else [])
    small_grads = (d_attn, dsk[0:1, 0:SWA_HEADS], d_gq, d_gkv)
    blocks = [in_block] + qkv_blocks + [o_block]
    return dx, dxb, small_grads, blocks, received, got_in + got_qkv + got_o


def _pack_small_grads(small_grads, d_final, loss):
    d_attn, d_sink, d_gq, d_gkv = zip(*small_grads)
    return jnp.concatenate([
        jnp.concatenate(d_attn, axis=0).reshape(64, 128),
        jnp.concatenate(d_gq, axis=0).reshape(12, 128),
        jnp.concatenate(d_gkv, axis=0).reshape(8, 128),
        d_final.reshape(16, 128),
        jnp.pad(jnp.concatenate(d_sink, axis=1), ((0, 0), (0, 64))),
        loss[0:1],
        jnp.zeros((PACK_ROWS - ROW_LOSS - 1, 128), F32)], axis=0)


def _pack_small(attn, qa, kva, final, sinks):
    return jnp.concatenate([
        attn.reshape(64, 128), qa.reshape(12, 128), kva.reshape(8, 128), final.reshape(16, 128),
        jnp.pad(sinks.reshape(1, 64), ((0, 0), (0, 64))),
        jnp.zeros((PACK_ROWS - ROW_SINK - 1, 128), F32)], axis=0)


def _unpack_small(p):
    return (p[ROW_ATTN:ROW_QA].reshape(DEPTH, D_MODEL), p[ROW_SINK, 0:64].reshape(DEPTH, SWA_HEADS),
            p[ROW_QA:ROW_KVA].reshape(DEPTH, Q_RANK), p[ROW_KVA:ROW_FINAL].reshape(DEPTH, KV_RANK),
            p[ROW_FINAL:ROW_SINK].reshape(D_MODEL))


def kernel(x, attn_norm_g, w_in, swa_sinks, q_a_norm_g, kv_a_norm_g, w_q_b, w_kv_b, w_out, final_norm_g, loss_target, m_attn_norm_g, m_w_in, m_swa_sinks, m_q_a_norm_g, m_kv_a_norm_g, m_w_q_b, m_w_kv_b, m_w_out, m_final_norm_g, v_attn_norm_g, v_w_in, v_swa_sinks, v_q_a_norm_g, v_kv_a_norm_g, v_w_q_b, v_w_kv_b, v_w_out, v_final_norm_g):
    xs, tgt = x[0], loss_target[0]
    tabs = _rope_tables(xs.shape[0])
    shards = [w.astype(MXU_DTYPE) for w in (w_in, w_q_b, w_kv_b, w_out)]
    shards[0] = jnp.pad(shards[0], ((0, 0), (0, 0), (0, SHARD_PAD - SHARD_COLS)))
    layer_shards = lambda l: [w[l] for w in shards]
    smalls = [(attn_norm_g[l:l + 1], swa_sinks[l], q_a_norm_g[l:l + 1], kv_a_norm_g[l:l + 1])
              for l in range(DEPTH)]

    gathered = list(_all_gather(layer_shards(0)[:1], "gather_weights")) + [None] * 3
    weights, saved = [None] * DEPTH, []
    for l in range(DEPTH):
        next_shards = layer_shards(l + 1) if l + 1 < DEPTH else []
        rest_shards = layer_shards(0)[1:] if l == 0 else []
        xs, acts, weights[l], gathered = _layer_fwd(
            xs, smalls[l], _w_in_from_gathered(gathered[0]), gathered[1:], rest_shards, tabs,
            next_shards)
        saved.append(acts)
    dx, dxb, d_final, loss = _final_loss(xs, tgt, final_norm_g.reshape(1, D_MODEL))

    received, small_grads, pending = [None] * DEPTH, [None] * DEPTH, []
    for l in reversed(range(DEPTH)):
        dx, dxb, small_grads[l], blocks, arrived, arrived_own = _layer_bwd(
            dx, dxb, saved[l], smalls[l], weights[l], tabs, pending, send_own=(l == 0))
        if pending:
            received[l + 1] = arrived
        pending = blocks
    received[0] = arrived_own
    small = _exchange_call([_pack_small_grads(small_grads, d_final, loss)], True, "gather_small")[0]

    big = []
    for a, (w, m, v, name) in enumerate(zip((w_in, w_q_b, w_kv_b, w_out),
                                            (m_w_in, m_w_q_b, m_w_kv_b, m_w_out),
                                            (v_w_in, v_w_q_b, v_w_kv_b, v_w_out),
                                            ("adamw_w_in", "adamw_w_q_b", "adamw_w_kv_b",
                                             "adamw_w_out"))):
        parts = [received[l][a] for l in range(DEPTH)]
        if a == 0:
            swap = lambda t: t.transpose(0, 2, 1)
            outs = _reduce_adamw_in(parts, swap(w), swap(m), swap(v), name)
            big.append([swap(t) for t in outs])
        else:
            cols = w.shape[-1]
            flat = lambda t: t.reshape(-1, cols)
            outs = _reduce_adamw(parts, flat(w), flat(m), flat(v), name)
            big.append([t.reshape(w.shape) for t in outs])

    sm = _reduce_adamw(
        [small],
        _pack_small(attn_norm_g, q_a_norm_g, kv_a_norm_g, final_norm_g, swa_sinks),
        _pack_small(m_attn_norm_g, m_q_a_norm_g, m_kv_a_norm_g, m_final_norm_g, m_swa_sinks),
        _pack_small(v_attn_norm_g, v_q_a_norm_g, v_kv_a_norm_g, v_final_norm_g, v_swa_sinks),
        "adamw_small")
    loss = sm[0][ROW_LOSS, 0]
    kinds = []
    for t in range(4):
        attn, sinks, qa, kva, final = _unpack_small(sm[t])
        b_in, b_q, b_kv, b_o = (big[i][t] for i in range(4))
        kinds.append((attn, b_in, sinks, qa, kva, b_q, b_kv, b_o, final))
    return (loss, dx[None], *kinds[0], *kinds[1], *kinds[2], *kinds[3])
```
